```python
import jax, jax.numpy as jnp
from jax import lax
import numpy as np

D_MODEL = 2048
BATCH = 2
SEQ = 8192
DEPTH = 1
DEC_BATCH = 8
DEC_SEQ = 16
PAST_LEN = 1024

CHUNK = 64
D_MIX = D_MODEL
C_CONV = D_MIX // 2
CONV_WIDTH = 31
GLA_HEADS = 4
GLA_DV = (D_MIX - C_CONV) // GLA_HEADS
GLA_DK = GLA_DV // 2
GATE_RANK = 16
GATE_TEMP = 16.0
GLA_BLOCK = CHUNK // 4
N_GROUPS = 8
EXPERTS_PER_GROUP = 8
N_EXPERTS = N_GROUPS * EXPERTS_PER_GROUP
TOP_K = 2
D_FF_EXPERT = D_MODEL // 4
MOE_BLOCK = 128
IN_COLS = 2 * C_CONV + 2 * GLA_HEADS * GLA_DK + 2 * GLA_HEADS * GLA_DV + GATE_RANK
EPS = 1e-6

kernel_name = 'hybrid_conformer_gla_hiermoe_stream_step'


def rms_norm(x, g):
    x32 = x.astype(jnp.float32)
    y = x32 * lax.rsqrt(jnp.mean(x32 * x32, axis=-1, keepdims=True) + EPS)
    return (y * g.astype(jnp.float32)).astype(x.dtype)


def layer_norm(x, g, b):
    x32 = x.astype(jnp.float32)
    mu = jnp.mean(x32, axis=-1, keepdims=True)
    xc = x32 - mu
    y = xc * lax.rsqrt(jnp.mean(xc * xc, axis=-1, keepdims=True) + EPS)
    return (y * g.astype(jnp.float32) + b.astype(jnp.float32)).astype(x.dtype)


def gla_recurrence(q, k, v, la, s0):
    bsz, t, nh, _ = q.shape
    dv = v.shape[-1]
    nb = -(-t // GLA_BLOCK)
    pad = nb * GLA_BLOCK - t

    def blocks(a):
        a = jnp.pad(a.astype(jnp.float32), ((0, 0), (0, pad), (0, 0), (0, 0)))
        return a.reshape(bsz, nb, GLA_BLOCK, nh, a.shape[-1])

    q, k, v, la = blocks(q), blocks(k), blocks(v), blocks(la)
    b = jnp.cumsum(la, axis=2)
    b_end = b[:, :, -1:]
    q_in = q * jnp.exp(b)
    k_in = k * jnp.exp(-b)
    k_out = k * jnp.exp(b_end - b)
    causal = jnp.tril(jnp.ones((GLA_BLOCK, GLA_BLOCK), dtype=bool))
    att = jnp.where(causal, jnp.einsum('bnihd,bnjhd->bnhij', q_in, k_in), 0.0)
    o_intra = jnp.einsum('bnhij,bnjhv->bnihv', att, v)

    def step(s, xs):
        qb, kb, vb, db = xs
        o = jnp.einsum('bihd,bhdv->bihv', qb, s)
        s = db[..., None] * s + jnp.einsum('bjhd,bjhv->bhdv', kb, vb)
        return s, o

    xs = (jnp.moveaxis(q_in, 1, 0), jnp.moveaxis(k_out, 1, 0), jnp.moveaxis(v, 1, 0),
          jnp.moveaxis(jnp.exp(b_end[:, :, 0]), 1, 0))
    s_fin, o_inter = lax.scan(step, s0.astype(jnp.float32), xs)
    o = o_intra + jnp.moveaxis(o_inter, 0, 1)
    return o.reshape(bsz, nb * GLA_BLOCK, nh, dv)[:, :t], s_fin


def parallel_mixer(h, conv_hist, gla_state, w_in, w_lr2, b_lr2, w_dw, b_dw, conv_ln_g, conv_ln_b, gla_norm_g, w_out):
    bsz, t, _ = h.shape
    proj = h @ w_in
    qk = GLA_HEADS * GLA_DK
    vv = GLA_HEADS * GLA_DV
    cuts = [C_CONV, 2 * C_CONV, 2 * C_CONV + qk, 2 * C_CONV + 2 * qk,
            2 * C_CONV + 2 * qk + vv, 2 * C_CONV + 2 * qk + 2 * vv]
    a, a_gate, q, k, v, g, lr = jnp.split(proj, cuts, axis=-1)
    u = a * jax.nn.sigmoid(a_gate)
    full = jnp.concatenate([conv_hist.astype(u.dtype), u], axis=1)
    c = lax.conv_general_dilated(full, w_dw[:, None, :].astype(full.dtype), (1,), 'VALID',
                                 dimension_numbers=('NWC', 'WIO', 'NWC'),
                                 feature_group_count=C_CONV) + b_dw
    new_hist = full[:, -(CONV_WIDTH - 1):]
    c = jax.nn.silu(layer_norm(c, conv_ln_g, conv_ln_b))
    la = jax.nn.log_sigmoid((lr @ w_lr2 + b_lr2).astype(jnp.float32)) / GATE_TEMP
    q = q.reshape(bsz, t, GLA_HEADS, GLA_DK) * (GLA_DK ** -0.5)
    k = k.reshape(bsz, t, GLA_HEADS, GLA_DK)
    v = v.reshape(bsz, t, GLA_HEADS, GLA_DV)
    la = la.reshape(bsz, t, GLA_HEADS, GLA_DK)
    o, new_state = gla_recurrence(q, k, v, la, gla_state)
    o = o * lax.rsqrt(jnp.mean(o * o, axis=-1, keepdims=True) + EPS) * gla_norm_g.astype(jnp.float32)
    o = o * jax.nn.silu(g.reshape(bsz, t, GLA_HEADS, GLA_DV).astype(jnp.float32))
    o = o.reshape(bsz, t, vv).astype(h.dtype)
    mix = jnp.concatenate([c, o], axis=-1) @ w_out
    return mix, new_hist, new_state


def hier_moe(x2d, w_rc, b_rc, w_rf, b_rf, w_eg, w_eu, w_ed):
    n, d = x2d.shape
    x32 = x2d.astype(jnp.float32)
    p_c = jax.nn.softmax(x32 @ w_rc.astype(jnp.float32) + b_rc.astype(jnp.float32), axis=-1)
    p_g, g_idx = lax.top_k(p_c, 1)
    lf = jnp.einsum('nd,gde->nge', x32, w_rf.astype(jnp.float32)) + b_rf.astype(jnp.float32)
    lf = lf[jnp.arange(n), g_idx[:, 0]]
    w2, j_idx = lax.top_k(jax.nn.softmax(lf, axis=-1), TOP_K)
    gate = p_g * (w2 / jnp.sum(w2, axis=-1, keepdims=True))
    expert = g_idx * EXPERTS_PER_GROUP + j_idx
    n_assign = n * TOP_K
    e_flat = expert.reshape(-1).astype(jnp.int32)
    tok = jnp.repeat(jnp.arange(n, dtype=jnp.int32), TOP_K)
    w_flat = gate.reshape(-1)
    order = jnp.argsort(e_flat)
    e_s, tok_s, w_s = e_flat[order], tok[order], w_flat[order]
    counts = jnp.bincount(e_flat, length=N_EXPERTS).astype(jnp.int32)
    padded = ((counts + MOE_BLOCK - 1) // MOE_BLOCK) * MOE_BLOCK
    start = jnp.cumsum(counts) - counts
    p_end = jnp.cumsum(padded)
    p_start = p_end - padded
    dest = p_start[e_s] + (jnp.arange(n_assign, dtype=jnp.int32) - start[e_s])
    n_blocks = -(-n_assign // MOE_BLOCK) + N_EXPERTS
    rows = n_blocks * MOE_BLOCK
    row_tok = jnp.zeros((rows,), jnp.int32).at[dest].set(tok_s)
    row_w = jnp.zeros((rows,), jnp.float32).at[dest].set(w_s)
    block_e = jnp.minimum(jnp.searchsorted(p_end, jnp.arange(n_blocks, dtype=jnp.int32) * MOE_BLOCK, side='right'),
                          N_EXPERTS - 1)

    def expert_block(args):
        toks, e = args
        xb = jnp.take(x2d, toks, axis=0)
        hb = jax.nn.silu(xb @ w_eg[e]) * (xb @ w_eu[e])
        return hb @ w_ed[e]

    out = lax.map(expert_block, (row_tok.reshape(n_blocks, MOE_BLOCK), block_e))
    out = out.reshape(rows, d) * row_w[:, None].astype(x2d.dtype)
    return jnp.zeros_like(x2d).at[row_tok].add(out)


def run_trunk(x, conv_hist, gla_state, norm1_g, w_in, w_lr2, b_lr2, w_dw, b_dw, conv_ln_g, conv_ln_b,
              gla_norm_g, w_out, norm2_g, w_router_coarse, b_router_coarse, w_router_fine, b_router_fine,
              w_exp_gate, w_exp_up, w_exp_down, norm_f_g):
    bsz, t, d = x.shape
    hists, states = [], []
    for l in range(DEPTH):
        h = rms_norm(x, norm1_g[l])
        mix, nh, ns = parallel_mixer(h, conv_hist[l], gla_state[l], w_in[l], w_lr2[l], b_lr2[l], w_dw[l], b_dw[l],
                                     conv_ln_g[l], conv_ln_b[l], gla_norm_g[l], w_out[l])
        x = x + mix
        h = rms_norm(x, norm2_g[l]).reshape(bsz * t, d)
        x = x + hier_moe(h, w_router_coarse[l], b_router_coarse[l], w_router_fine[l], b_router_fine[l],
                         w_exp_gate[l], w_exp_up[l], w_exp_down[l]).reshape(bsz, t, d)
        hists.append(nh)
        states.append(ns)
    return rms_norm(x, norm_f_g), jnp.stack(hists), jnp.stack(states)


def setup_inputs(seed: int = 0) -> dict:
    key = jax.random.key(seed)
    ks = jax.random.split(key, 24)

    def nrm(k, shape, scale):
        return jax.random.normal(k, shape, jnp.float32) * scale

    return {
        'x_prompt': nrm(ks[0], (BATCH, SEQ, D_MODEL), 1.0),
        'x_sample': nrm(ks[1], (DEC_BATCH, DEC_SEQ, D_MODEL), 1.0),
        'cache_conv': nrm(ks[2], (DEPTH, DEC_BATCH, CONV_WIDTH - 1, C_CONV), 0.5),
        'state_gla': nrm(ks[3], (DEPTH, DEC_BATCH, GLA_HEADS, GLA_DK, GLA_DV), 0.1),
        'norm1_g': 1.0 + nrm(ks[4], (DEPTH, D_MODEL), 0.02),
        'w_in': nrm(ks[5], (DEPTH, D_MODEL, IN_COLS), D_MODEL ** -0.5),
        'w_lr2': nrm(ks[6], (DEPTH, GATE_RANK, GLA_HEADS * GLA_DK), GATE_RANK ** -0.5),
        'b_lr2': nrm(ks[7], (DEPTH, GLA_HEADS * GLA_DK), 0.1),
        'w_dw': nrm(ks[8], (DEPTH, CONV_WIDTH, C_CONV), CONV_WIDTH ** -0.5),
        'b_dw': nrm(ks[9], (DEPTH, C_CONV), 0.02),
        'conv_ln_g': 1.0 + nrm(ks[10], (DEPTH, C_CONV), 0.02),
        'conv_ln_b': nrm(ks[11], (DEPTH, C_CONV), 0.02),
        'gla_norm_g': 1.0 + nrm(ks[12], (DEPTH, GLA_DV), 0.02),
        'w_out': nrm(ks[13], (DEPTH, D_MIX, D_MODEL), D_MIX ** -0.5),
        'norm2_g': 1.0 + nrm(ks[14], (DEPTH, D_MODEL), 0.02),
        'w_router_coarse': nrm(ks[15], (DEPTH, D_MODEL, N_GROUPS), D_MODEL ** -0.5),
        'b_router_coarse': nrm(ks[16], (DEPTH, N_GROUPS), 0.01),
        'w_router_fine': nrm(ks[17], (DEPTH, N_GROUPS, D_MODEL, EXPERTS_PER_GROUP), D_MODEL ** -0.5),
        'b_router_fine': nrm(ks[18], (DEPTH, N_GROUPS, EXPERTS_PER_GROUP), 0.01),
        'w_exp_gate': nrm(ks[19], (DEPTH, N_EXPERTS, D_MODEL, D_FF_EXPERT), D_MODEL ** -0.5),
        'w_exp_up': nrm(ks[20], (DEPTH, N_EXPERTS, D_MODEL, D_FF_EXPERT), D_MODEL ** -0.5),
        'w_exp_down': nrm(ks[21], (DEPTH, N_EXPERTS, D_FF_EXPERT, D_MODEL), D_FF_EXPERT ** -0.5),
        'norm_f_g': 1.0 + nrm(ks[22], (D_MODEL,), 0.02),
    }


def reference(x_prompt, x_sample, cache_conv, state_gla, norm1_g, w_in, w_lr2, b_lr2, w_dw, b_dw, conv_ln_g,
              conv_ln_b, gla_norm_g, w_out, norm2_g, w_router_coarse, b_router_coarse, w_router_fine,
              b_router_fine, w_exp_gate, w_exp_up, w_exp_down, norm_f_g):
    bp = x_prompt.shape[0]
    zero_hist = jnp.zeros((DEPTH, bp, CONV_WIDTH - 1, C_CONV), x_prompt.dtype)
    zero_state = jnp.zeros((DEPTH, bp, GLA_HEADS, GLA_DK, GLA_DV), jnp.float32)
    y_prompt, conv_prompt, gla_prompt = run_trunk(
        x_prompt, zero_hist, zero_state, norm1_g, w_in, w_lr2, b_lr2, w_dw, b_dw, conv_ln_g, conv_ln_b,
        gla_norm_g, w_out, norm2_g, w_router_coarse, b_router_coarse, w_router_fine, b_router_fine,
        w_exp_gate, w_exp_up, w_exp_down, norm_f_g)
    y_sample, conv_sample, gla_sample = run_trunk(
        x_sample, cache_conv, state_gla, norm1_g, w_in, w_lr2, b_lr2, w_dw, b_dw, conv_ln_g, conv_ln_b,
        gla_norm_g, w_out, norm2_g, w_router_coarse, b_router_coarse, w_router_fine, b_router_fine,
        w_exp_gate, w_exp_up, w_exp_down, norm_f_g)
    return (y_prompt, y_sample, conv_prompt, gla_prompt, conv_sample, gla_sample)
```

```python
import functools

import jax
import jax.numpy as jnp
from jax import lax
from jax.experimental import pallas as pl
from jax.experimental.pallas import tpu as pltpu

F32 = jnp.float32
BF16 = jnp.bfloat16
I32 = jnp.int32
EPS = 1e-6
GATE_TEMP = 16.0
HIGHEST = lax.Precision.HIGHEST

LANES = 128
ROW_TILE = 128
EXPERT_ROWS = 256
HIST_PAD = 32
VMEM_LIMIT = 56 * 1024 * 1024


def _params(semantics, vmem=VMEM_LIMIT):
    return pltpu.CompilerParams(dimension_semantics=semantics, vmem_limit_bytes=vmem)


def _rms(x, g):
    return x * lax.rsqrt(jnp.mean(x * x, axis=-1, keepdims=True) + EPS) * g


def _silu(x):
    return x * jax.nn.sigmoid(x)


def _log_sigmoid(z):
    return jnp.minimum(z, 0.0) - jnp.log(1.0 + jnp.exp(-jnp.abs(z)))


def _pick(i, n_first, first_ref, second_ref):
    return jnp.where(i < n_first, first_ref[...], second_ref[...])


def _split_maps(n_first):
    first = lambda i: (jnp.minimum(i, n_first - 1), 0)
    second = lambda i: (jnp.maximum(i - n_first, 0), 0)
    return first, second


def _inproj_kernel(xp_ref, xs_ref, g1_ref, w_ref, wlr2_ref, blr2_ref,
                   u_ref, q_ref, k_ref, v_ref, g_ref, la_ref, *, n_first, c_conv, qk, vv, dk):
    i = pl.program_id(0)
    x = _pick(i, n_first, xp_ref, xs_ref)
    h = _rms(x, g1_ref[...]).astype(BF16)

    def mm(lo, width):
        return jnp.dot(h, w_ref[:, lo:lo + width], preferred_element_type=F32)

    a = mm(0, c_conv)
    a_gate = mm(c_conv, c_conv)
    u_ref[...] = a * jax.nn.sigmoid(a_gate)
    off = 2 * c_conv
    q_ref[...] = mm(off, qk) * (dk ** -0.5)
    k_ref[...] = mm(off + qk, qk)
    v_ref[...] = mm(off + 2 * qk, vv)
    g_ref[...] = mm(off + 2 * qk + vv, vv)
    lr = mm(off + 2 * qk + 2 * vv, LANES)
    z = jnp.dot(lr, wlr2_ref[...], precision=HIGHEST, preferred_element_type=F32) + blr2_ref[...]
    la_ref[...] = _log_sigmoid(z) * (1.0 / GATE_TEMP)


def _inproj(xp, xs, g1, w_pad, wlr2_pad, blr2, *, c_conv, qk, vv, dk):
    n_p, d = xp.shape
    n_s = xs.shape[0]
    n_all = n_p + n_s
    n_first = n_p // ROW_TILE
    grid = (n_all // ROW_TILE,)
    first, second = _split_maps(n_first)
    const = lambda i: (0, 0)
    row = lambda i: (i, 0)
    widths = (c_conv, qk, qk, vv, vv, qk)
    return pl.pallas_call(
        functools.partial(_inproj_kernel, n_first=n_first, c_conv=c_conv, qk=qk, vv=vv, dk=dk),
        grid=grid,
        in_specs=[
            pl.BlockSpec((ROW_TILE, d), first),
            pl.BlockSpec((ROW_TILE, d), second),
            pl.BlockSpec((1, d), const),
            pl.BlockSpec(w_pad.shape, const, pipeline_mode=pl.Buffered(1)),
            pl.BlockSpec(wlr2_pad.shape, const),
            pl.BlockSpec((1, qk), const),
        ],
        out_specs=[pl.BlockSpec((ROW_TILE, w), row) for w in widths],
        out_shape=[jax.ShapeDtypeStruct((n_all, w), F32) for w in widths],
        compiler_params=_params(("arbitrary",)),
        name="inproj",
    )(xp, xs, g1, w_pad, wlr2_pad, blr2)


def _conv_kernel(u_ref, hist_ref, w_ref, b_ref, lg_ref, lb_ref, c_ref, win, cbuf, *, tt, width):
    i = pl.program_id(1)

    @pl.when(i == 0)
    def _():
        win[0:HIST_PAD, :] = hist_ref[0]

    @pl.when(i > 0)
    def _():
        win[0:HIST_PAD, :] = win[tt:tt + HIST_PAD, :]

    win[HIST_PAD:HIST_PAD + tt, :] = u_ref[...]
    lead = HIST_PAD - (width - 1)
    n_ch = u_ref.shape[1]
    for cb in range(n_ch // LANES):
        cs = slice(cb * LANES, (cb + 1) * LANES)
        acc = jnp.broadcast_to(b_ref[:, cs], (tt, LANES))
        for j in range(width):
            acc = acc + w_ref[j:j + 1, cs] * win[lead + j:lead + j + tt, cs]
        cbuf[:, cs] = acc
    c = cbuf[...]
    mu = jnp.mean(c, axis=-1, keepdims=True)
    xc = c - mu
    y = xc * lax.rsqrt(jnp.mean(xc * xc, axis=-1, keepdims=True) + EPS) * lg_ref[...] + lb_ref[...]
    c_ref[...] = _silu(y).astype(c_ref.dtype)


def _conv(u_all, hist_pad, w_dw, b_dw, ln_g, ln_b, *, row0, bsz, t, tt):
    n_ch = u_all.shape[1]
    width = w_dw.shape[0]
    nt = t // tt
    blk0 = row0 // tt
    const = lambda b, i: (0, 0)
    return pl.pallas_call(
        functools.partial(_conv_kernel, tt=tt, width=width),
        grid=(bsz, nt),
        in_specs=[
            pl.BlockSpec((tt, n_ch), lambda b, i: (blk0 + b * nt + i, 0)),
            pl.BlockSpec((1, HIST_PAD, n_ch), lambda b, i: (b, 0, 0)),
            pl.BlockSpec(w_dw.shape, const),
            pl.BlockSpec((1, n_ch), const),
            pl.BlockSpec((1, n_ch), const),
            pl.BlockSpec((1, n_ch), const),
        ],
        out_specs=pl.BlockSpec((tt, n_ch), lambda b, i: (b * nt + i, 0)),
        out_shape=jax.ShapeDtypeStruct((bsz * t, n_ch), BF16),
        scratch_shapes=[pltpu.VMEM((tt + HIST_PAD, n_ch), F32), pltpu.VMEM((tt, n_ch), F32)],
        compiler_params=_params(("arbitrary", "arbitrary")),
        name="conv",
    )(u_all, hist_pad, w_dw, b_dw, ln_g, ln_b)


def _gla_kernel(q_ref, k_ref, v_ref, g_ref, la_ref, s0_ref, gn_ref, o_ref, sout_ref, state,
                *, chunk, heads, dk, dv):
    i = pl.program_id(1)

    @pl.when(i == 0)
    def _():
        state[...] = s0_ref[0]

    la = la_ref[...]
    r = lax.broadcasted_iota(I32, (chunk, chunk), 0)
    c = lax.broadcasted_iota(I32, (chunk, chunk), 1)
    causal = c <= r
    b = jnp.dot(causal.astype(F32), la, precision=HIGHEST, preferred_element_type=F32)
    b_end = b[chunk - 1:chunk, :]
    q_in = (q_ref[...] * jnp.exp(b)).astype(BF16)
    k_in = (k_ref[...] * jnp.exp(-b)).astype(BF16)
    k_out = (k_ref[...] * jnp.exp(b_end - b)).astype(BF16)
    decay_row = jnp.exp(b_end)
    eye = lax.broadcasted_iota(I32, (dk, dk), 0) == lax.broadcasted_iota(I32, (dk, dk), 1)
    for h in range(heads):
        ks = slice(h * dk, (h + 1) * dk)
        vs = slice(h * dv, (h + 1) * dv)
        vh = v_ref[:, vs].astype(BF16)
        att = lax.dot_general(q_in[:, ks], k_in[:, ks], (((1,), (1,)), ((), ())), preferred_element_type=F32)
        att = jnp.where(causal, att, 0.0).astype(BF16)
        s_h = state[h]
        o = jnp.dot(att, vh, preferred_element_type=F32)
        o = o + jnp.dot(q_in[:, ks], s_h.astype(BF16), preferred_element_type=F32)
        decay_col = jnp.sum(jnp.where(eye, jnp.broadcast_to(decay_row[:, ks], (dk, dk)), 0.0), axis=1, keepdims=True)
        state[h] = decay_col * s_h + lax.dot_general(k_out[:, ks], vh, (((0,), (0,)), ((), ())),
                                                     preferred_element_type=F32)
        o = o * lax.rsqrt(jnp.mean(o * o, axis=-1, keepdims=True) + EPS) * gn_ref[...]
        o_ref[:, vs] = (o * _silu(g_ref[:, vs])).astype(o_ref.dtype)

    @pl.when(i == pl.num_programs(1) - 1)
    def _():
        sout_ref[0] = state[...]


def _gla(q_all, k_all, v_all, g_all, la_all, s0, gn, *, row0, bsz, t, chunk):
    heads, dk, dv = s0.shape[1:]
    nt = t // chunk
    blk0 = row0 // chunk
    rows = lambda b, i: (blk0 + b * nt + i, 0)
    return pl.pallas_call(
        functools.partial(_gla_kernel, chunk=chunk, heads=heads, dk=dk, dv=dv),
        grid=(bsz, nt),
        in_specs=[
            pl.BlockSpec((chunk, heads * dk), rows),
            pl.BlockSpec((chunk, heads * dk), rows),
            pl.BlockSpec((chunk, heads * dv), rows),
            pl.BlockSpec((chunk, heads * dv), rows),
            pl.BlockSpec((chunk, heads * dk), rows),
            pl.BlockSpec((1, heads, dk, dv), lambda b, i: (b, 0, 0, 0)),
            pl.BlockSpec((1, dv), lambda b, i: (0, 0)),
        ],
        out_specs=[
            pl.BlockSpec((chunk, heads * dv), lambda b, i: (b * nt + i, 0)),
            pl.BlockSpec((1, heads, dk, dv), lambda b, i: (b, 0, 0, 0)),
        ],
        out_shape=[
            jax.ShapeDtypeStruct((bsz * t, heads * dv), BF16),
            jax.ShapeDtypeStruct((bsz, heads, dk, dv), F32),
        ],
        scratch_shapes=[pltpu.VMEM((heads, dk, dv), F32)],
        compiler_params=_params(("arbitrary", "arbitrary")),
        name="gla",
    )(q_all, k_all, v_all, g_all, la_all, s0, gn)


def _outproj_kernel(xp_ref, xs_ref, cp_ref, cs_ref, op_ref, os_ref, w_ref, g2_ref, wr_ref, br_ref,
                    x1_ref, h2_ref, ids_ref, gates_ref, *, n_first, c_conv, n_groups, per_group):
    i = pl.program_id(0)
    x = _pick(i, n_first, xp_ref, xs_ref)
    cc = _pick(i, n_first, cp_ref, cs_ref)
    oo = _pick(i, n_first, op_ref, os_ref)
    mix = jnp.dot(cc, w_ref[0:c_conv, :], preferred_element_type=F32)
    mix = mix + jnp.dot(oo, w_ref[c_conv:, :], preferred_element_type=F32)
    x1 = x + mix
    x1_ref[...] = x1
    h2 = _rms(x1, g2_ref[...])
    h2_ref[...] = h2
    logits = lax.dot_general(wr_ref[...], h2, (((1,), (1,)), ((), ())), precision=HIGHEST,
                             preferred_element_type=F32) + br_ref[...]
    tm = x.shape[0]
    n_exp = n_groups * per_group
    lc = logits[0:n_groups, :]
    mc = jnp.max(lc, axis=0, keepdims=True)
    p_group = 1.0 / jnp.sum(jnp.exp(lc - mc), axis=0, keepdims=True)
    rows_c = lax.broadcasted_iota(I32, (n_groups, tm), 0)
    g_idx = jnp.min(jnp.where(lc == mc, rows_c, n_groups), axis=0, keepdims=True)
    lf = logits[n_groups:n_groups + n_exp, :]
    rows_f = lax.broadcasted_iota(I32, (n_exp, tm), 0)
    in_group = (rows_f >= g_idx * per_group) & (rows_f < (g_idx + 1) * per_group)
    neg = jnp.float32(-jnp.inf)
    l1 = jnp.where(in_group, lf, neg)
    m1 = jnp.max(l1, axis=0, keepdims=True)
    e1 = jnp.min(jnp.where(l1 == m1, rows_f, n_exp), axis=0, keepdims=True)
    l2 = jnp.where(rows_f == e1, neg, l1)
    m2 = jnp.max(l2, axis=0, keepdims=True)
    e2 = jnp.min(jnp.where(l2 == m2, rows_f, n_exp), axis=0, keepdims=True)
    r2 = jnp.exp(m2 - m1)
    w1 = 1.0 / (1.0 + r2)
    row8 = lax.broadcasted_iota(I32, (8, tm), 0)
    ids_ref[...] = jnp.where(row8 == 0, e1, jnp.where(row8 == 1, e2, 0))
    gates_ref[...] = jnp.where(row8 == 0, p_group * w1, jnp.where(row8 == 1, p_group * (r2 * w1), 0.0))


def _outproj(xp, xs, cp, cs, op, os_, w_out, g2, wr, br, *, n_groups, per_group):
    n_p, d = xp.shape
    n_all = n_p + xs.shape[0]
    n_first = n_p // ROW_TILE
    c_conv = cp.shape[1]
    vv = op.shape[1]
    first, second = _split_maps(n_first)
    const = lambda i: (0, 0)
    row = lambda i: (i, 0)
    col = lambda i: (0, i)
    return pl.pallas_call(
        functools.partial(_outproj_kernel, n_first=n_first, c_conv=c_conv, n_groups=n_groups, per_group=per_group),
        grid=(n_all // ROW_TILE,),
        in_specs=[
            pl.BlockSpec((ROW_TILE, d), first), pl.BlockSpec((ROW_TILE, d), second),
            pl.BlockSpec((ROW_TILE, c_conv), first), pl.BlockSpec((ROW_TILE, c_conv), second),
            pl.BlockSpec((ROW_TILE, vv), first), pl.BlockSpec((ROW_TILE, vv), second),
            pl.BlockSpec(w_out.shape, const, pipeline_mode=pl.Buffered(1)),
            pl.BlockSpec((1, d), const),
            pl.BlockSpec(wr.shape, const),
            pl.BlockSpec(br.shape, const),
        ],
        out_specs=[
            pl.BlockSpec((ROW_TILE, d), row), pl.BlockSpec((ROW_TILE, d), row),
            pl.BlockSpec((8, ROW_TILE), col), pl.BlockSpec((8, ROW_TILE), col),
        ],
        out_shape=[
            jax.ShapeDtypeStruct((n_all, d), F32), jax.ShapeDtypeStruct((n_all, d), F32),
            jax.ShapeDtypeStruct((8, n_all), I32), jax.ShapeDtypeStruct((8, n_all), F32),
        ],
        compiler_params=_params(("arbitrary",)),
        name="outproj",
    )(xp, xs, cp, cs, op, os_, w_out, g2, wr, br)


def _plan_kernel(ids_ref, dest_ref, blk_ref, counts, starts, carry, *, n_exp, n_blk_lanes):
    phase = pl.program_id(0)
    i = pl.program_id(1)
    tm = ids_ref.shape[1]
    rows = lax.broadcasted_iota(I32, (n_exp, tm), 0)
    oh0 = (rows == ids_ref[0:1, :]).astype(F32)
    oh1 = (rows == ids_ref[1:2, :]).astype(F32)
    both = oh0 + oh1
    tile_counts = jnp.sum(both, axis=1, keepdims=True)

    @pl.when((phase == 0) & (i == 0))
    def _():
        counts[...] = jnp.zeros_like(counts)

    @pl.when(phase == 0)
    def _():
        counts[...] += tile_counts

    @pl.when((phase == 1) & (i == 0))
    def _():
        padded = jnp.floor((counts[...] + (EXPERT_ROWS - 1)) * (1.0 / EXPERT_ROWS)) * EXPERT_ROWS
        er = lax.broadcasted_iota(I32, (n_exp, n_exp), 0)
        ec = lax.broadcasted_iota(I32, (n_exp, n_exp), 1)
        start = jnp.dot((ec < er).astype(F32), jnp.broadcast_to(padded, (n_exp, LANES)), precision=HIGHEST,
                        preferred_element_type=F32)[:, 0:1]
        starts[...] = start
        carry[...] = jnp.zeros_like(carry)
        ends = start + padded
        first_row = (lax.broadcasted_iota(I32, (n_exp, n_blk_lanes), 1) * EXPERT_ROWS).astype(F32)
        blk_e = jnp.sum((ends <= first_row).astype(F32), axis=0, keepdims=True)
        blk_e = jnp.minimum(blk_e, n_exp - 1.0)
        n_used = jnp.sum(padded, axis=0, keepdims=True) * (1.0 / EXPERT_ROWS)
        row8 = lax.broadcasted_iota(I32, (8, n_blk_lanes), 0)
        blk_ref[...] = jnp.where(row8 == 0, blk_e, jnp.where(row8 == 1, n_used, 0.0)).astype(I32)

    @pl.when(phase == 1)
    def _():
        tr = lax.broadcasted_iota(I32, (tm, tm), 0)
        tc = lax.broadcasted_iota(I32, (tm, tm), 1)
        earlier = jnp.dot(both.astype(BF16), (tr < tc).astype(BF16), preferred_element_type=F32)
        base = earlier + carry[...] + starts[...]
        d0 = jnp.sum(oh0 * base, axis=0, keepdims=True)
        d1 = jnp.sum(oh1 * base, axis=0, keepdims=True)
        row8 = lax.broadcasted_iota(I32, (8, tm), 0)
        dest_ref[...] = jnp.where(row8 == 0, d0, jnp.where(row8 == 1, d1, 0.0)).astype(I32)
        carry[...] += tile_counts


def _plan(ids, *, n_exp, n_blk_lanes):
    n_all = ids.shape[1]
    nblk = n_all // ROW_TILE
    return pl.pallas_call(
        functools.partial(_plan_kernel, n_exp=n_exp, n_blk_lanes=n_blk_lanes),
        grid=(2, nblk),
        in_specs=[pl.BlockSpec((8, ROW_TILE), lambda p, i: (0, i))],
        out_specs=[
            pl.BlockSpec((8, ROW_TILE), lambda p, i: (0, i * p)),
            pl.BlockSpec((8, n_blk_lanes), lambda p, i: (0, 0)),
        ],
        out_shape=[jax.ShapeDtypeStruct((8, n_all), I32), jax.ShapeDtypeStruct((8, n_blk_lanes), I32)],
        scratch_shapes=[pltpu.VMEM((n_exp, 1), F32), pltpu.VMEM((n_exp, 1), F32), pltpu.VMEM((n_exp, 1), F32)],
        compiler_params=_params(("arbitrary", "arbitrary")),
        name="plan",
    )(ids)


def _row_copy(src, s, dst, d, sem):
    return pltpu.make_async_copy(src.at[pl.ds(s, 1)], dst.at[pl.ds(d, 1)], sem)


def _dispatch_kernel(dest_ref, h2_ref, zero_ref, xs_ref, sems):
    del zero_ref
    i = pl.program_id(0)
    n = pl.num_programs(0)
    tm = dest_ref.shape[1]
    slot = i % 2

    def issue(t, carry):
        tok = i * tm + t
        _row_copy(h2_ref, tok, xs_ref, dest_ref[0, t], sems.at[slot]).start()
        _row_copy(h2_ref, tok, xs_ref, dest_ref[1, t], sems.at[slot]).start()
        return carry

    lax.fori_loop(0, tm, issue, 0)

    def drain(s):
        def body(t, carry):
            _row_copy(h2_ref, 0, xs_ref, 0, sems.at[s]).wait()
            return carry
        lax.fori_loop(0, 2 * tm, body, 0)

    @pl.when(i > 0)
    def _():
        drain(1 - slot)

    @pl.when(i == n - 1)
    def _():
        drain(slot)


def _dispatch(dest, h2, rows_pad):
    n_all, d = h2.shape
    zeros = jnp.zeros((rows_pad, d), F32)
    return pl.pallas_call(
        _dispatch_kernel,
        grid=(n_all // ROW_TILE,),
        in_specs=[
            pl.BlockSpec((8, ROW_TILE), lambda i: (0, i), memory_space=pltpu.SMEM),
            pl.BlockSpec(memory_space=pl.ANY),
            pl.BlockSpec(memory_space=pl.ANY),
        ],
        out_specs=pl.BlockSpec(memory_space=pl.ANY),
        out_shape=jax.ShapeDtypeStruct((rows_pad, d), F32),
        scratch_shapes=[pltpu.SemaphoreType.DMA((2,))],
        input_output_aliases={2: 0},
        compiler_params=_params(("arbitrary",)),
        name="dispatch",
    )(dest, h2, zeros)


def _experts_kernel(blk_e_ref, n_used_ref, xs_ref, wg_ref, wu_ref, wd_ref, ys_ref, wg_bf, wu_bf, wd_bf):
    b = pl.program_id(0)

    @pl.when(b < n_used_ref[0])
    def _():
        e = blk_e_ref[b]
        prev = blk_e_ref[jnp.maximum(b - 1, 0)]

        @pl.when((b == 0) | (e != prev))
        def _():
            wg_bf[...] = wg_ref[0].astype(BF16)
            wu_bf[...] = wu_ref[0].astype(BF16)
            wd_bf[...] = wd_ref[0].astype(BF16)

        x = xs_ref[...].astype(BF16)
        hg = jnp.dot(x, wg_bf[...], preferred_element_type=F32)
        hu = jnp.dot(x, wu_bf[...], preferred_element_type=F32)
        hb = (_silu(hg) * hu).astype(BF16)
        ys_ref[...] = jnp.dot(hb, wd_bf[...], preferred_element_type=F32)

    @pl.when(b >= n_used_ref[0])
    def _():
        ys_ref[...] = jnp.zeros_like(ys_ref)


def _experts(blk_e, n_used, xs, w_gate, w_up, w_down):
    rows_pad, d = xs.shape
    n_blocks = rows_pad // EXPERT_ROWS
    ff = w_gate.shape[2]
    used = lambda b, be, nu: jnp.minimum(b, nu[0] - 1)
    row = lambda b, be, nu: (used(b, be, nu), 0)
    wmap = lambda b, be, nu: (be[used(b, be, nu)], 0, 0)
    grid_spec = pltpu.PrefetchScalarGridSpec(
        num_scalar_prefetch=2,
        grid=(n_blocks,),
        in_specs=[
            pl.BlockSpec((EXPERT_ROWS, d), row),
            pl.BlockSpec((1, d, ff), wmap),
            pl.BlockSpec((1, d, ff), wmap),
            pl.BlockSpec((1, ff, d), wmap),
        ],
        out_specs=pl.BlockSpec((EXPERT_ROWS, d), lambda b, be, nu: (b, 0)),
        scratch_shapes=[pltpu.VMEM((d, ff), BF16), pltpu.VMEM((d, ff), BF16), pltpu.VMEM((ff, d), BF16)],
    )
    return pl.pallas_call(
        _experts_kernel,
        grid_spec=grid_spec,
        out_shape=jax.ShapeDtypeStruct((rows_pad, d), F32),
        compiler_params=_params(("arbitrary",)),
        name="experts",
    )(blk_e, n_used, xs, w_gate, w_up, w_down)


def _combine_kernel(dest_ref, gates_ref, x1_ref, ys_ref, gf_ref, yp_ref, ysmp_ref, buf0, buf1, sem, *, n_first):
    i = pl.program_id(0)
    tm = x1_ref.shape[0]

    def issue(t, carry):
        _row_copy(ys_ref, dest_ref[0, t], buf0, t, sem).start()
        _row_copy(ys_ref, dest_ref[1, t], buf1, t, sem).start()
        return carry

    lax.fori_loop(0, tm, issue, 0)

    def drain(t, carry):
        _row_copy(ys_ref, 0, buf0, 0, sem).wait()
        return carry

    lax.fori_loop(0, 2 * tm, drain, 0)
    x2 = x1_ref[...] + gates_ref[:, 0:1] * buf0[...] + gates_ref[:, 1:2] * buf1[...]
    y = _rms(x2, gf_ref[...])

    @pl.when(i < n_first)
    def _():
        yp_ref[...] = y

    @pl.when(i >= n_first)
    def _():
        ysmp_ref[...] = y


def _combine(dest, gates_t, x1, ys, gf, *, n_p):
    n_all, d = x1.shape
    n_first = n_p // ROW_TILE
    first, second = _split_maps(n_first)
    return pl.pallas_call(
        functools.partial(_combine_kernel, n_first=n_first),
        grid=(n_all // ROW_TILE,),
        in_specs=[
            pl.BlockSpec((8, ROW_TILE), lambda i: (0, i), memory_space=pltpu.SMEM),
            pl.BlockSpec((ROW_TILE, 8), lambda i: (i, 0)),
            pl.BlockSpec((ROW_TILE, d), lambda i: (i, 0)),
            pl.BlockSpec(memory_space=pl.ANY),
            pl.BlockSpec((1, d), lambda i: (0, 0)),
        ],
        out_specs=[pl.BlockSpec((ROW_TILE, d), first), pl.BlockSpec((ROW_TILE, d), second)],
        out_shape=[jax.ShapeDtypeStruct((n_p, d), F32), jax.ShapeDtypeStruct((n_all - n_p, d), F32)],
        scratch_shapes=[pltpu.VMEM((ROW_TILE, d), F32), pltpu.VMEM((ROW_TILE, d), F32), pltpu.SemaphoreType.DMA(())],
        compiler_params=_params(("arbitrary",)),
        name="combine",
    )(dest, gates_t, x1, ys, gf)


def _chunk_for(t):
    return 64 if t % 64 == 0 else t


def kernel(x_prompt, x_sample, cache_conv, state_gla, norm1_g, w_in, w_lr2, b_lr2, w_dw, b_dw, conv_ln_g, conv_ln_b, gla_norm_g, w_out, norm2_g, w_router_coarse, b_router_coarse, w_router_fine, b_router_fine, w_exp_gate, w_exp_up, w_exp_down, norm_f_g):
    assert norm1_g.shape[0] == 1, "single trunk layer"
    bp, tp, d = x_prompt.shape
    bs, ts, _ = x_sample.shape
    heads, dk, dv = state_gla.shape[2:]
    c_conv = w_dw.shape[2]
    width = w_dw.shape[1]
    rank = w_lr2.shape[1]
    qk, vv = heads * dk, heads * dv
    n_groups, _, per_group = w_router_fine.shape[1:]
    n_exp = n_groups * per_group
    n_p, n_s = bp * tp, bs * ts
    n_all = n_p + n_s
    assert n_p % ROW_TILE == 0 and n_s % ROW_TILE == 0 and width - 1 <= HIST_PAD

    xp = x_prompt.reshape(n_p, d)
    xs = x_sample.reshape(n_s, d)
    row = lambda a: a.reshape(1, -1)

    w_in_pad = jnp.pad(w_in[0], ((0, 0), (0, LANES - rank))).astype(BF16)
    w_lr2_pad = jnp.pad(w_lr2[0], ((0, LANES - rank), (0, 0)))
    u, q, k, v, g, la = _inproj(xp, xs, row(norm1_g[0]), w_in_pad, w_lr2_pad, row(b_lr2[0]),
                                c_conv=c_conv, qk=qk, vv=vv, dk=dk)

    hist_p = jnp.zeros((bp, HIST_PAD, c_conv), F32)
    hist_s = jnp.pad(cache_conv[0], ((0, 0), (HIST_PAD - (width - 1), 0), (0, 0)))
    conv_args = (w_dw[0], row(b_dw[0]), row(conv_ln_g[0]), row(conv_ln_b[0]))
    c_p = _conv(u, hist_p, *conv_args, row0=0, bsz=bp, t=tp, tt=_chunk_for(tp))
    c_s = _conv(u, hist_s, *conv_args, row0=n_p, bsz=bs, t=ts, tt=_chunk_for(ts))

    gn = row(gla_norm_g[0])
    s0_p = jnp.zeros((bp, heads, dk, dv), F32)
    o_p, gla_p = _gla(q, k, v, g, la, s0_p, gn, row0=0, bsz=bp, t=tp, chunk=_chunk_for(tp))
    o_s, gla_s = _gla(q, k, v, g, la, state_gla[0], gn, row0=n_p, bsz=bs, t=ts, chunk=_chunk_for(ts))

    wr = jnp.concatenate([w_router_coarse[0].T,
                          jnp.transpose(w_router_fine[0], (0, 2, 1)).reshape(n_exp, d)], axis=0)
    br = jnp.concatenate([b_router_coarse[0], b_router_fine[0].reshape(n_exp)])
    r_rows = -(-(n_groups + n_exp) // 8) * 8
    wr = jnp.pad(wr, ((0, r_rows - wr.shape[0]), (0, 0)))
    br = jnp.pad(br, (0, r_rows - br.shape[0])).reshape(r_rows, 1)
    x1, h2, ids, gates = _outproj(xp, xs, c_p, c_s, o_p, o_s, w_out[0].astype(BF16), row(norm2_g[0]), wr, br,
                                  n_groups=n_groups, per_group=per_group)

    n_blocks = (2 * n_all) // EXPERT_ROWS + n_exp
    n_blk_lanes = -(-n_blocks // LANES) * LANES
    dest, blk = _plan(ids, n_exp=n_exp, n_blk_lanes=n_blk_lanes)
    xs_sorted = _dispatch(dest, h2, n_blocks * EXPERT_ROWS)
    ys_sorted = _experts(blk[0, :n_blocks], blk[1, :1], xs_sorted, w_exp_gate[0], w_exp_up[0], w_exp_down[0])
    y_p, y_s = _combine(dest, gates.T, x1, ys_sorted, row(norm_f_g), n_p=n_p)

    u_p = u[:n_p].reshape(bp, tp, c_conv)
    u_s = u[n_p:].reshape(bs, ts, c_conv)
    keep = width - 1
    conv_prompt = u_p[:, tp - keep:]
    conv_sample = jnp.concatenate([cache_conv[0], u_s], axis=1)[:, -keep:]
    return (y_p.reshape(bp, tp, d), y_s.reshape(bs, ts, d), conv_prompt[None], gla_p[None],
            conv_sample[None], gla_s[None])
```

```python
import functools

import jax
import jax.numpy as jnp
from jax import lax
from jax.experimental import pallas as pl
from jax.experimental.pallas import tpu as pltpu

F32 = jnp.float32
BF16 = jnp.bfloat16
I32 = jnp.int32
EPS = 1e-6
GATE_TEMP = 16.0
HIGHEST = lax.Precision.HIGHEST

LANES = 128
ROW_TILE = 128
EXPERT_ROWS = 256
HIST_PAD = 32
VMEM_LIMIT = 56 * 1024 * 1024


def _params(semantics, vmem=VMEM_LIMIT):
    return pltpu.CompilerParams(dimension_semantics=semantics, vmem_limit_bytes=vmem)


def _rms(x, g):
    return x * lax.rsqrt(jnp.mean(x * x, axis=-1, keepdims=True) + EPS) * g


def _silu(x):
    return x * jax.nn.sigmoid(x)


def _log_sigmoid(z):
    return jnp.minimum(z, 0.0) - jnp.log(1.0 + jnp.exp(-jnp.abs(z)))


def _pick(i, n_first, first_ref, second_ref):
    return jnp.where(i < n_first, first_ref[...], second_ref[...])


def _split_maps(n_first):
    first = lambda i: (jnp.minimum(i, n_first - 1), 0)
    second = lambda i: (jnp.maximum(i - n_first, 0), 0)
    return first, second


def _inproj_kernel(xp_ref, xs_ref, g1_ref, w_ref, wlr2_ref, blr2_ref,
                   u_ref, q_ref, k_ref, v_ref, g_ref, la_ref, *, n_first, c_conv, qk, vv, dk):
    i = pl.program_id(0)
    x = _pick(i, n_first, xp_ref, xs_ref)
    h = _rms(x, g1_ref[...]).astype(BF16)

    def mm(lo, width):
        return jnp.dot(h, w_ref[:, lo:lo + width], preferred_element_type=F32)

    a = mm(0, c_conv)
    a_gate = mm(c_conv, c_conv)
    u_ref[...] = a * jax.nn.sigmoid(a_gate)
    off = 2 * c_conv
    q_ref[...] = mm(off, qk) * (dk ** -0.5)
    k_ref[...] = mm(off + qk, qk)
    v_ref[...] = mm(off + 2 * qk, vv)
    g_ref[...] = mm(off + 2 * qk + vv, vv)
    lr = mm(off + 2 * qk + 2 * vv, LANES)
    z = jnp.dot(lr, wlr2_ref[...], precision=HIGHEST, preferred_element_type=F32) + blr2_ref[...]
    la_ref[...] = _log_sigmoid(z) * (1.0 / GATE_TEMP)


def _inproj(xp, xs, g1, w_pad, wlr2_pad, blr2, *, c_conv, qk, vv, dk):
    n_p, d = xp.shape
    n_s = xs.shape[0]
    n_all = n_p + n_s
    n_first = n_p // ROW_TILE
    grid = (n_all // ROW_TILE,)
    first, second = _split_maps(n_first)
    const = lambda i: (0, 0)
    row = lambda i: (i, 0)
    widths = (c_conv, qk, qk, vv, vv, qk)
    return pl.pallas_call(
        functools.partial(_inproj_kernel, n_first=n_first, c_conv=c_conv, qk=qk, vv=vv, dk=dk),
        grid=grid,
        in_specs=[
            pl.BlockSpec((ROW_TILE, d), first),
            pl.BlockSpec((ROW_TILE, d), second),
            pl.BlockSpec((1, d), const),
            pl.BlockSpec(w_pad.shape, const, pipeline_mode=pl.Buffered(1)),
            pl.BlockSpec(wlr2_pad.shape, const),
            pl.BlockSpec((1, qk), const),
        ],
        out_specs=[pl.BlockSpec((ROW_TILE, w), row) for w in widths],
        out_shape=[jax.ShapeDtypeStruct((n_all, w), F32) for w in widths],
        compiler_params=_params(("arbitrary",)),
        name="inproj",
    )(xp, xs, g1, w_pad, wlr2_pad, blr2)


def _conv_kernel(u_ref, hist_ref, w_ref, b_ref, lg_ref, lb_ref, c_ref, win, cbuf, *, tt, width):
    i = pl.program_id(1)

    @pl.when(i == 0)
    def _():
        win[0:HIST_PAD, :] = hist_ref[0]

    @pl.when(i > 0)
    def _():
        win[0:HIST_PAD, :] = win[tt:tt + HIST_PAD, :]

    win[HIST_PAD:HIST_PAD + tt, :] = u_ref[...]
    lead = HIST_PAD - (width - 1)
    n_ch = u_ref.shape[1]
    for cb in range(n_ch // LANES):
        cs = slice(cb * LANES, (cb + 1) * LANES)
        acc = jnp.broadcast_to(b_ref[:, cs], (tt, LANES))
        for j in range(width):
            acc = acc + w_ref[j:j + 1, cs] * win[lead + j:lead + j + tt, cs]
        cbuf[:, cs] = acc
    c = cbuf[...]
    mu = jnp.mean(c, axis=-1, keepdims=True)
    xc = c - mu
    y = xc * lax.rsqrt(jnp.mean(xc * xc, axis=-1, keepdims=True) + EPS) * lg_ref[...] + lb_ref[...]
    c_ref[...] = _silu(y).astype(c_ref.dtype)


def _conv(u_all, hist_pad, w_dw, b_dw, ln_g, ln_b, *, row0, bsz, t, tt):
    n_ch = u_all.shape[1]
    width = w_dw.shape[0]
    nt = t // tt
    blk0 = row0 // tt
    const = lambda b, i: (0, 0)
    return pl.pallas_call(
        functools.partial(_conv_kernel, tt=tt, width=width),
        grid=(bsz, nt),
        in_specs=[
            pl.BlockSpec((tt, n_ch), lambda b, i: (blk0 + b * nt + i, 0)),
            pl.BlockSpec((1, HIST_PAD, n_ch), lambda b, i: (b, 0, 0)),
            pl.BlockSpec(w_dw.shape, const),
            pl.BlockSpec((1, n_ch), const),
            pl.BlockSpec((1, n_ch), const),
            pl.BlockSpec((1, n_ch), const),
        ],
        out_specs=pl.BlockSpec((tt, n_ch), lambda b, i: (b * nt + i, 0)),
        out_shape=jax.ShapeDtypeStruct((bsz * t, n_ch), BF16),
        scratch_shapes=[pltpu.VMEM((tt + HIST_PAD, n_ch), F32), pltpu.VMEM((tt, n_ch), F32)],
        compiler_params=_params(("arbitrary", "arbitrary")),
        name="conv",
    )(u_all, hist_pad, w_dw, b_dw, ln_g, ln_b)


def _gla_kernel(q_ref, k_ref, v_ref, g_ref, la_ref, s0_ref, gn_ref, o_ref, sout_ref, state,
                *, chunk, heads, dk, dv):
    i = pl.program_id(1)

    @pl.when(i == 0)
    def _():
        state[...] = s0_ref[0]

    la = la_ref[...]
    r = lax.broadcasted_iota(I32, (chunk, chunk), 0)
    c = lax.broadcasted_iota(I32, (chunk, chunk), 1)
    causal = c <= r
    b = jnp.dot(causal.astype(F32), la, precision=HIGHEST, preferred_element_type=F32)
    b_end = b[chunk - 1:chunk, :]
    q_in = (q_ref[...] * jnp.exp(b)).astype(BF16)
    k_in = (k_ref[...] * jnp.exp(-b)).astype(BF16)
    k_out = (k_ref[...] * jnp.exp(b_end - b)).astype(BF16)
    decay_row = jnp.exp(b_end)
    eye = lax.broadcasted_iota(I32, (dk, dk), 0) == lax.broadcasted_iota(I32, (dk, dk), 1)
    for h in range(heads):
        ks = slice(h * dk, (h + 1) * dk)
        vs = slice(h * dv, (h + 1) * dv)
        vh = v_ref[:, vs].astype(BF16)
        att = lax.dot_general(q_in[:, ks], k_in[:, ks], (((1,), (1,)), ((), ())), preferred_element_type=F32)
        att = jnp.where(causal, att, 0.0).astype(BF16)
        s_h = state[h]
        o = jnp.dot(att, vh, preferred_element_type=F32)
        o = o + jnp.dot(q_in[:, ks], s_h.astype(BF16), preferred_element_type=F32)
        decay_col = jnp.sum(jnp.where(eye, jnp.broadcast_to(decay_row[:, ks], (dk, dk)), 0.0), axis=1, keepdims=True)
        state[h] = decay_col * s_h + lax.dot_general(k_out[:, ks], vh, (((0,), (0,)), ((), ())),
                                                     preferred_element_type=F32)
        o = o * lax.rsqrt(jnp.mean(o * o, axis=-1, keepdims=True) + EPS) * gn_ref[...]
        o_ref[:, vs] = (o * _silu(g_ref[:, vs])).astype(o_ref.dtype)

    @pl.when(i == pl.num_programs(1) - 1)
    def _():
        sout_ref[0] = state[...]


def _gla(q_all, k_all, v_all, g_all, la_all, s0, gn, *, row0, bsz, t, chunk):
    heads, dk, dv = s0.shape[1:]
    nt = t // chunk
    blk0 = row0 // chunk
    rows = lambda b, i: (blk0 + b * nt + i, 0)
    return pl.pallas_call(
        functools.partial(_gla_kernel, chunk=chunk, heads=heads, dk=dk, dv=dv),
        grid=(bsz, nt),
        in_specs=[
            pl.BlockSpec((chunk, heads * dk), rows),
            pl.BlockSpec((chunk, heads * dk), rows),
            pl.BlockSpec((chunk, heads * dv), rows),
            pl.BlockSpec((chunk, heads * dv), rows),
            pl.BlockSpec((chunk, heads * dk), rows),
            pl.BlockSpec((1, heads, dk, dv), lambda b, i: (b, 0, 0, 0)),
            pl.BlockSpec((1, dv), lambda b, i: (0, 0)),
        ],
        out_specs=[
            pl.BlockSpec((chunk, heads * dv), lambda b, i: (b * nt + i, 0)),
            pl.BlockSpec((1, heads, dk, dv), lambda b, i: (b, 0, 0, 0)),
        ],
        out_shape=[
            jax.ShapeDtypeStruct((bsz * t, heads * dv), BF16),
            jax.ShapeDtypeStruct((bsz, heads, dk, dv), F32),
        ],
        scratch_shapes=[pltpu.VMEM((heads, dk, dv), F32)],
        compiler_params=_params(("arbitrary", "arbitrary")),
        name="gla",
    )(q_all, k_all, v_all, g_all, la_all, s0, gn)


def _outproj_kernel(xp_ref, xs_ref, cp_ref, cs_ref, op_ref, os_ref, w_ref, g2_ref, wr_ref, br_ref,
                    x1_ref, h2_ref, ids_ref, gates_ref, *, n_first, c_conv, n_groups, per_group):
    i = pl.program_id(0)
    x = _pick(i, n_first, xp_ref, xs_ref)
    cc = _pick(i, n_first, cp_ref, cs_ref)
    oo = _pick(i, n_first, op_ref, os_ref)
    mix = jnp.dot(cc, w_ref[0:c_conv, :], preferred_element_type=F32)
    mix = mix + jnp.dot(oo, w_ref[c_conv:, :], preferred_element_type=F32)
    x1 = x + mix
    x1_ref[...] = x1
    h2 = _rms(x1, g2_ref[...])
    h2_ref[...] = h2
    logits = lax.dot_general(wr_ref[...], h2, (((1,), (1,)), ((), ())), precision=HIGHEST,
                             preferred_element_type=F32) + br_ref[...]
    tm = x.shape[0]
    n_exp = n_groups * per_group
    lc = logits[0:n_groups, :]
    mc = jnp.max(lc, axis=0, keepdims=True)
    p_group = 1.0 / jnp.sum(jnp.exp(lc - mc), axis=0, keepdims=True)
    rows_c = lax.broadcasted_iota(I32, (n_groups, tm), 0)
    g_idx = jnp.min(jnp.where(lc == mc, rows_c, n_groups), axis=0, keepdims=True)
    lf = logits[n_groups:n_groups + n_exp, :]
    rows_f = lax.broadcasted_iota(I32, (n_exp, tm), 0)
    in_group = (rows_f >= g_idx * per_group) & (rows_f < (g_idx + 1) * per_group)
    neg = jnp.float32(-jnp.inf)
    l1 = jnp.where(in_group, lf, neg)
    m1 = jnp.max(l1, axis=0, keepdims=True)
    e1 = jnp.min(jnp.where(l1 == m1, rows_f, n_exp), axis=0, keepdims=True)
    l2 = jnp.where(rows_f == e1, neg, l1)
    m2 = jnp.max(l2, axis=0, keepdims=True)
    e2 = jnp.min(jnp.where(l2 == m2, rows_f, n_exp), axis=0, keepdims=True)
    r2 = jnp.exp(m2 - m1)
    w1 = 1.0 / (1.0 + r2)
    row8 = lax.broadcasted_iota(I32, (8, tm), 0)
    ids_ref[...] = jnp.where(row8 == 0, e1, jnp.where(row8 == 1, e2, 0))
    gates_ref[...] = jnp.where(row8 == 0, p_group * w1, jnp.where(row8 == 1, p_group * (r2 * w1), 0.0))


def _outproj(xp, xs, cp, cs, op, os_, w_out, g2, wr, br, *, n_groups, per_group):
    n_p, d = xp.shape
    n_all = n_p + xs.shape[0]
    n_first = n_p // ROW_TILE
    c_conv = cp.shape[1]
    vv = op.shape[1]
    first, second = _split_maps(n_first)
    const = lambda i: (0, 0)
    row = lambda i: (i, 0)
    col = lambda i: (0, i)
    return pl.pallas_call(
        functools.partial(_outproj_kernel, n_first=n_first, c_conv=c_conv, n_groups=n_groups, per_group=per_group),
        grid=(n_all // ROW_TILE,),
        in_specs=[
            pl.BlockSpec((ROW_TILE, d), first), pl.BlockSpec((ROW_TILE, d), second),
            pl.BlockSpec((ROW_TILE, c_conv), first), pl.BlockSpec((ROW_TILE, c_conv), second),
            pl.BlockSpec((ROW_TILE, vv), first), pl.BlockSpec((ROW_TILE, vv), second),
            pl.BlockSpec(w_out.shape, const, pipeline_mode=pl.Buffered(1)),
            pl.BlockSpec((1, d), const),
            pl.BlockSpec(wr.shape, const),
            pl.BlockSpec(br.shape, const),
        ],
        out_specs=[
            pl.BlockSpec((ROW_TILE, d), row), pl.BlockSpec((ROW_TILE, d), row),
            pl.BlockSpec((8, ROW_TILE), col), pl.BlockSpec((8, ROW_TILE), col),
        ],
        out_shape=[
            jax.ShapeDtypeStruct((n_all, d), F32), jax.ShapeDtypeStruct((n_all, d), F32),
            jax.ShapeDtypeStruct((8, n_all), I32), jax.ShapeDtypeStruct((8, n_all), F32),
        ],
        compiler_params=_params(("arbitrary",)),
        name="outproj",
    )(xp, xs, cp, cs, op, os_, w_out, g2, wr, br)


def _plan_kernel(ids_ref, dest_ref, blk_ref, counts, starts, carry, *, n_exp, n_blk_lanes):
    phase = pl.program_id(0)
    i = pl.program_id(1)
    tm = ids_ref.shape[1]
    rows = lax.broadcasted_iota(I32, (n_exp, tm), 0)
    oh0 = (rows == ids_ref[0:1, :]).astype(F32)
    oh1 = (rows == ids_ref[1:2, :]).astype(F32)
    both = oh0 + oh1
    tile_counts = jnp.sum(both, axis=1, keepdims=True)

    @pl.when((phase == 0) & (i == 0))
    def _():
        counts[...] = jnp.zeros_like(counts)

    @pl.when(phase == 0)
    def _():
        counts[...] += tile_counts

    @pl.when((phase == 1) & (i == 0))
    def _():
        padded = jnp.floor((counts[...] + (EXPERT_ROWS - 1)) * (1.0 / EXPERT_ROWS)) * EXPERT_ROWS
        er = lax.broadcasted_iota(I32, (n_exp, n_exp), 0)
        ec = lax.broadcasted_iota(I32, (n_exp, n_exp), 1)
        start = jnp.dot((ec < er).astype(F32), jnp.broadcast_to(padded, (n_exp, LANES)), precision=HIGHEST,
                        preferred_element_type=F32)[:, 0:1]
        starts[...] = start
        carry[...] = jnp.zeros_like(carry)
        ends = start + padded
        first_row = (lax.broadcasted_iota(I32, (n_exp, n_blk_lanes), 1) * EXPERT_ROWS).astype(F32)
        blk_e = jnp.sum((ends <= first_row).astype(F32), axis=0, keepdims=True)
        blk_e = jnp.minimum(blk_e, n_exp - 1.0)
        n_used = jnp.sum(padded, axis=0, keepdims=True) * (1.0 / EXPERT_ROWS)
        row8 = lax.broadcasted_iota(I32, (8, n_blk_lanes), 0)
        blk_ref[...] = jnp.where(row8 == 0, blk_e, jnp.where(row8 == 1, n_used, 0.0)).astype(I32)

    @pl.when(phase == 1)
    def _():
        tr = lax.broadcasted_iota(I32, (tm, tm), 0)
        tc = lax.broadcasted_iota(I32, (tm, tm), 1)
        earlier = jnp.dot(both.astype(BF16), (tr < tc).astype(BF16), preferred_element_type=F32)
        base = earlier + carry[...] + starts[...]
        d0 = jnp.sum(oh0 * base, axis=0, keepdims=True)
        d1 = jnp.sum(oh1 * base, axis=0, keepdims=True)
        row8 = lax.broadcasted_iota(I32, (8, tm), 0)
        dest_ref[...] = jnp.where(row8 == 0, d0, jnp.where(row8 == 1, d1, 0.0)).astype(I32)
        carry[...] += tile_counts


def _plan(ids, *, n_exp, n_blk_lanes):
    n_all = ids.shape[1]
    nblk = n_all // ROW_TILE
    return pl.pallas_call(
        functools.partial(_plan_kernel, n_exp=n_exp, n_blk_lanes=n_blk_lanes),
        grid=(2, nblk),
        in_specs=[pl.BlockSpec((8, ROW_TILE), lambda p, i: (0, i))],
        out_specs=[
            pl.BlockSpec((8, ROW_TILE), lambda p, i: (0, i * p)),
            pl.BlockSpec((8, n_blk_lanes), lambda p, i: (0, 0)),
        ],
        out_shape=[jax.ShapeDtypeStruct((8, n_all), I32), jax.ShapeDtypeStruct((8, n_blk_lanes), I32)],
        scratch_shapes=[pltpu.VMEM((n_exp, 1), F32), pltpu.VMEM((n_exp, 1), F32), pltpu.VMEM((n_exp, 1), F32)],
        compiler_params=_params(("arbitrary", "arbitrary")),
        name="plan",
    )(ids)


def _row_copy(src, s, dst, d, sem):
    return pltpu.make_async_copy(src.at[pl.ds(s, 1)], dst.at[pl.ds(d, 1)], sem)


def _dispatch_kernel(dest_ref, h2_ref, zero_ref, xs_ref, sem):
    del zero_ref
    tm = dest_ref.shape[1]

    def issue(t, carry):
        _row_copy(h2_ref, t, xs_ref, dest_ref[0, t], sem).start()
        _row_copy(h2_ref, t, xs_ref, dest_ref[1, t], sem).start()
        return carry

    lax.fori_loop(0, tm, issue, 0)

    def drain(t, carry):
        _row_copy(h2_ref, 0, xs_ref, 0, sem).wait()
        return carry

    lax.fori_loop(0, 2 * tm, drain, 0)


def _dispatch(dest, h2, rows_pad):
    n_all, d = h2.shape
    zeros = jnp.zeros((rows_pad, d), F32)
    return pl.pallas_call(
        _dispatch_kernel,
        grid=(n_all // ROW_TILE,),
        in_specs=[
            pl.BlockSpec((8, ROW_TILE), lambda i: (0, i), memory_space=pltpu.SMEM),
            pl.BlockSpec((ROW_TILE, d), lambda i: (i, 0)),
            pl.BlockSpec(memory_space=pl.ANY),
        ],
        out_specs=pl.BlockSpec(memory_space=pl.ANY),
        out_shape=jax.ShapeDtypeStruct((rows_pad, d), F32),
        scratch_shapes=[pltpu.SemaphoreType.DMA(())],
        input_output_aliases={2: 0},
        compiler_params=_params(("arbitrary",)),
        name="dispatch",
    )(dest, h2, zeros)


def _experts_kernel(blk_e_ref, n_used_ref, xs_ref, wg_ref, wu_ref, wd_ref, ys_ref, wg_bf, wu_bf, wd_bf):
    b = pl.program_id(0)

    @pl.when(b < n_used_ref[0])
    def _():
        e = blk_e_ref[b]
        prev = blk_e_ref[jnp.maximum(b - 1, 0)]

        @pl.when((b == 0) | (e != prev))
        def _():
            wg_bf[...] = wg_ref[0].astype(BF16)
            wu_bf[...] = wu_ref[0].astype(BF16)
            wd_bf[...] = wd_ref[0].astype(BF16)

        x = xs_ref[...].astype(BF16)
        hg = jnp.dot(x, wg_bf[...], preferred_element_type=F32)
        hu = jnp.dot(x, wu_bf[...], preferred_element_type=F32)
        hb = (_silu(hg) * hu).astype(BF16)
        ys_ref[...] = jnp.dot(hb, wd_bf[...], preferred_element_type=F32)

    @pl.when(b >= n_used_ref[0])
    def _():
        ys_ref[...] = jnp.zeros_like(ys_ref)


def _experts(blk_e, n_used, xs, w_gate, w_up, w_down):
    rows_pad, d = xs.shape
    n_blocks = rows_pad // EXPERT_ROWS
    ff = w_gate.shape[2]
    used = lambda b, be, nu: jnp.minimum(b, nu[0] - 1)
    row = lambda b, be, nu: (used(b, be, nu), 0)
    wmap = lambda b, be, nu: (be[used(b, be, nu)], 0, 0)
    grid_spec = pltpu.PrefetchScalarGridSpec(
        num_scalar_prefetch=2,
        grid=(n_blocks,),
        in_specs=[
            pl.BlockSpec((EXPERT_ROWS, d), row),
            pl.BlockSpec((1, d, ff), wmap),
            pl.BlockSpec((1, d, ff), wmap),
            pl.BlockSpec((1, ff, d), wmap),
        ],
        out_specs=pl.BlockSpec((EXPERT_ROWS, d), lambda b, be, nu: (b, 0)),
        scratch_shapes=[pltpu.VMEM((d, ff), BF16), pltpu.VMEM((d, ff), BF16), pltpu.VMEM((ff, d), BF16)],
    )
    return pl.pallas_call(
        _experts_kernel,
        grid_spec=grid_spec,
        out_shape=jax.ShapeDtypeStruct((rows_pad, d), F32),
        compiler_params=_params(("arbitrary",)),
        name="experts",
    )(blk_e, n_used, xs, w_gate, w_up, w_down)


def _combine_kernel(dest_ref, gates_ref, x1_ref, ys_ref, gf_ref, yp_ref, ysmp_ref, buf0, buf1, sem, *, n_first):
    i = pl.program_id(0)
    tm = x1_ref.shape[0]

    def issue(t, carry):
        _row_copy(ys_ref, dest_ref[0, t], buf0, t, sem).start()
        _row_copy(ys_ref, dest_ref[1, t], buf1, t, sem).start()
        return carry

    lax.fori_loop(0, tm, issue, 0)

    def drain(t, carry):
        _row_copy(ys_ref, 0, buf0, 0, sem).wait()
        return carry

    lax.fori_loop(0, 2 * tm, drain, 0)
    x2 = x1_ref[...] + gates_ref[:, 0:1] * buf0[...] + gates_ref[:, 1:2] * buf1[...]
    y = _rms(x2, gf_ref[...])

    @pl.when(i < n_first)
    def _():
        yp_ref[...] = y

    @pl.when(i >= n_first)
    def _():
        ysmp_ref[...] = y


def _combine(dest, gates_t, x1, ys, gf, *, n_p):
    n_all, d = x1.shape
    n_first = n_p // ROW_TILE
    first, second = _split_maps(n_first)
    return pl.pallas_call(
        functools.partial(_combine_kernel, n_first=n_first),
        grid=(n_all // ROW_TILE,),
        in_specs=[
            pl.BlockSpec((8, ROW_TILE), lambda i: (0, i), memory_space=pltpu.SMEM),
            pl.BlockSpec((ROW_TILE, 8), lambda i: (i, 0)),
            pl.BlockSpec((ROW_TILE, d), lambda i: (i, 0)),
            pl.BlockSpec(memory_space=pl.ANY),
            pl.BlockSpec((1, d), lambda i: (0, 0)),
        ],
        out_specs=[pl.BlockSpec((ROW_TILE, d), first), pl.BlockSpec((ROW_TILE, d), second)],
        out_shape=[jax.ShapeDtypeStruct((n_p, d), F32), jax.ShapeDtypeStruct((n_all - n_p, d), F32)],
        scratch_shapes=[pltpu.VMEM((ROW_TILE, d), F32), pltpu.VMEM((ROW_TILE, d), F32), pltpu.SemaphoreType.DMA(())],
        compiler_params=_params(("arbitrary",)),
        name="combine",
    )(dest, gates_t, x1, ys, gf)


def _chunk_for(t):
    return 64 if t % 64 == 0 else t


def kernel(x_prompt, x_sample, cache_conv, state_gla, norm1_g, w_in, w_lr2, b_lr2, w_dw, b_dw, conv_ln_g, conv_ln_b, gla_norm_g, w_out, norm2_g, w_router_coarse, b_router_coarse, w_router_fine, b_router_fine, w_exp_gate, w_exp_up, w_exp_down, norm_f_g):
    assert norm1_g.shape[0] == 1, "single trunk layer"
    bp, tp, d = x_prompt.shape
    bs, ts, _ = x_sample.shape
    heads, dk, dv = state_gla.shape[2:]
    c_conv = w_dw.shape[2]
    width = w_dw.shape[1]
    rank = w_lr2.shape[1]
    qk, vv = heads * dk, heads * dv
    n_groups, _, per_group = w_router_fine.shape[1:]
    n_exp = n_groups * per_group
    n_p, n_s = bp * tp, bs * ts
    n_all = n_p + n_s
    assert n_p % ROW_TILE == 0 and n_s % ROW_TILE == 0 and width - 1 <= HIST_PAD

    xp = x_prompt.reshape(n_p, d)
    xs = x_sample.reshape(n_s, d)
    row = lambda a: a.reshape(1, -1)

    w_in_pad = jnp.pad(w_in[0], ((0, 0), (0, LANES - rank))).astype(BF16)
    w_lr2_pad = jnp.pad(w_lr2[0], ((0, LANES - rank), (0, 0)))
    u, q, k, v, g, la = _inproj(xp, xs, row(norm1_g[0]), w_in_pad, w_lr2_pad, row(b_lr2[0]),
                                c_conv=c_conv, qk=qk, vv=vv, dk=dk)

    hist_p = jnp.zeros((bp, HIST_PAD, c_conv), F32)
    hist_s = jnp.pad(cache_conv[0], ((0, 0), (HIST_PAD - (width - 1), 0), (0, 0)))
    conv_args = (w_dw[0], row(b_dw[0]), row(conv_ln_g[0]), row(conv_ln_b[0]))
    c_p = _conv(u, hist_p, *conv_args, row0=0, bsz=bp, t=tp, tt=_chunk_for(tp))
    c_s = _conv(u, hist_s, *conv_args, row0=n_p, bsz=bs, t=ts, tt=_chunk_for(ts))

    gn = row(gla_norm_g[0])
    s0_p = jnp.zeros((bp, heads, dk, dv), F32)
    o_p, gla_p = _gla(q, k, v, g, la, s0_p, gn, row0=0, bsz=bp, t=tp, chunk=_chunk_for(tp))
    o_s, gla_s = _gla(q, k, v, g, la, state_gla[0], gn, row0=n_p, bsz=bs, t=ts, chunk=_chunk_for(ts))

    wr = jnp.concatenate([w_router_coarse[0].T,
                          jnp.transpose(w_router_fine[0], (0, 2, 1)).reshape(n_exp, d)], axis=0)
    br = jnp.concatenate([b_router_coarse[0], b_router_fine[0].reshape(n_exp)])
    r_rows = -(-(n_groups + n_exp) // 8) * 8
    wr = jnp.pad(wr, ((0, r_rows - wr.shape[0]), (0, 0)))
    br = jnp.pad(br, (0, r_rows - br.shape[0])).reshape(r_rows, 1)
    x1, h2, ids, gates = _outproj(xp, xs, c_p, c_s, o_p, o_s, w_out[0].astype(BF16), row(norm2_g[0]), wr, br,
                                  n_groups=n_groups, per_group=per_group)

    n_blocks = (2 * n_all) // EXPERT_ROWS + n_exp
    n_blk_lanes = -(-n_blocks // LANES) * LANES
    dest, blk = _plan(ids, n_exp=n_exp, n_blk_lanes=n_blk_lanes)
    xs_sorted = _dispatch(dest, h2, n_blocks * EXPERT_ROWS)
    ys_sorted = _experts(blk[0, :n_blocks], blk[1, :1], xs_sorted, w_exp_gate[0], w_exp_up[0], w_exp_down[0])
    y_p, y_s = _combine(dest, gates.T, x1, ys_sorted, row(norm_f_g), n_p=n_p)

    u_p = u[:n_p].reshape(bp, tp, c_conv)
    u_s = u[n_p:].reshape(bs, ts, c_conv)
    keep = width - 1
    conv_prompt = u_p[:, tp - keep:]
    conv_sample = jnp.concatenate([cache_conv[0], u_s], axis=1)[:, -keep:]
    return (y_p.reshape(bp, tp, d), y_s.reshape(bs, ts, d), conv_prompt[None], gla_p[None],
            conv_sample[None], gla_s[None])
```

```python
import functools

import jax
import jax.numpy as jnp
from jax import lax
from jax.experimental import pallas as pl
from jax.experimental.pallas import tpu as pltpu

F32 = jnp.float32
BF16 = jnp.bfloat16
I32 = jnp.int32
U32 = jnp.uint32
EPS = 1e-6
GATE_TEMP = 16.0
HIGHEST = lax.Precision.HIGHEST

LANES = 128
ROW_TILE = 128
EXPERT_ROWS = 256
HIST_PAD = 32
DMA_UNROLL = 8
VMEM_LIMIT = 56 * 1024 * 1024


def _params(semantics, vmem=VMEM_LIMIT):
    return pltpu.CompilerParams(dimension_semantics=semantics, vmem_limit_bytes=vmem)


def _rms(x, g):
    return x * lax.rsqrt(jnp.mean(x * x, axis=-1, keepdims=True) + EPS) * g


def _silu(x):
    return x * jax.nn.sigmoid(x)


def _log_sigmoid(z):
    return jnp.minimum(z, 0.0) - jnp.log(1.0 + jnp.exp(-jnp.abs(z)))


def _pick(i, n_first, first_ref, second_ref):
    return jnp.where(i < n_first, first_ref[...], second_ref[...])


def _split_maps(n_first):
    first = lambda i: (jnp.minimum(i, n_first - 1), 0)
    second = lambda i: (jnp.maximum(i - n_first, 0), 0)
    return first, second


def _pack_pairs(x):
    half = x.shape[1] // 2
    hi = lax.bitcast_convert_type(x[:, :half].astype(BF16).astype(F32), U32)
    lo = lax.bitcast_convert_type(x[:, half:].astype(BF16).astype(F32), U32)
    return hi | (lo >> 16)


def _unpack_pairs(p):
    hi = lax.bitcast_convert_type(p & jnp.uint32(0xFFFF0000), F32)
    lo = lax.bitcast_convert_type(p << 16, F32)
    return hi, lo


def _inproj_kernel(xp_ref, xs_ref, g1_ref, w_ref, wlr2_ref, blr2_ref,
                   u_ref, q_ref, k_ref, v_ref, g_ref, la_ref, *, n_first, c_conv, qk, vv, dk):
    i = pl.program_id(0)
    x = _pick(i, n_first, xp_ref, xs_ref)
    h = _rms(x, g1_ref[...]).astype(BF16)

    def mm(lo, width):
        return jnp.dot(h, w_ref[:, lo:lo + width], preferred_element_type=F32)

    a = mm(0, c_conv)
    a_gate = mm(c_conv, c_conv)
    u_ref[...] = a * jax.nn.sigmoid(a_gate)
    off = 2 * c_conv
    q_ref[...] = mm(off, qk) * (dk ** -0.5)
    k_ref[...] = mm(off + qk, qk)
    v_ref[...] = mm(off + 2 * qk, vv)
    g_ref[...] = mm(off + 2 * qk + vv, vv)
    lr = mm(off + 2 * qk + 2 * vv, LANES)
    z = jnp.dot(lr, wlr2_ref[...], precision=HIGHEST, preferred_element_type=F32) + blr2_ref[...]
    la_ref[...] = _log_sigmoid(z) * (1.0 / GATE_TEMP)


def _inproj(xp, xs, g1, w_pad, wlr2_pad, blr2, *, c_conv, qk, vv, dk):
    n_p, d = xp.shape
    n_s = xs.shape[0]
    n_all = n_p + n_s
    n_first = n_p // ROW_TILE
    grid = (n_all // ROW_TILE,)
    first, second = _split_maps(n_first)
    const = lambda i: (0, 0)
    row = lambda i: (i, 0)
    widths = (c_conv, qk, qk, vv, vv, qk)
    return pl.pallas_call(
        functools.partial(_inproj_kernel, n_first=n_first, c_conv=c_conv, qk=qk, vv=vv, dk=dk),
        grid=grid,
        in_specs=[
            pl.BlockSpec((ROW_TILE, d), first),
            pl.BlockSpec((ROW_TILE, d), second),
            pl.BlockSpec((1, d), const),
            pl.BlockSpec(w_pad.shape, const, pipeline_mode=pl.Buffered(1)),
            pl.BlockSpec(wlr2_pad.shape, const),
            pl.BlockSpec((1, qk), const),
        ],
        out_specs=[pl.BlockSpec((ROW_TILE, w), row) for w in widths],
        out_shape=[jax.ShapeDtypeStruct((n_all, w), F32) for w in widths],
        compiler_params=_params(("arbitrary",)),
        name="inproj",
    )(xp, xs, g1, w_pad, wlr2_pad, blr2)


def _conv_kernel(u_ref, hist_ref, w_ref, b_ref, lg_ref, lb_ref, c_ref, win, cbuf, *, tt, width):
    i = pl.program_id(1)

    @pl.when(i == 0)
    def _():
        win[0:HIST_PAD, :] = hist_ref[0]

    @pl.when(i > 0)
    def _():
        win[0:HIST_PAD, :] = win[tt:tt + HIST_PAD, :]

    win[HIST_PAD:HIST_PAD + tt, :] = u_ref[...]
    lead = HIST_PAD - (width - 1)
    n_ch = u_ref.shape[1]
    for cb in range(n_ch // LANES):
        cs = slice(cb * LANES, (cb + 1) * LANES)
        acc = jnp.broadcast_to(b_ref[:, cs], (tt, LANES))
        for j in range(width):
            acc = acc + w_ref[j:j + 1, cs] * win[lead + j:lead + j + tt, cs]
        cbuf[:, cs] = acc
    c = cbuf[...]
    mu = jnp.mean(c, axis=-1, keepdims=True)
    xc = c - mu
    y = xc * lax.rsqrt(jnp.mean(xc * xc, axis=-1, keepdims=True) + EPS) * lg_ref[...] + lb_ref[...]
    c_ref[...] = _silu(y).astype(c_ref.dtype)


def _conv(u_all, hist_pad, w_dw, b_dw, ln_g, ln_b, *, row0, bsz, t, tt):
    n_ch = u_all.shape[1]
    width = w_dw.shape[0]
    nt = t // tt
    blk0 = row0 // tt
    const = lambda b, i: (0, 0)
    return pl.pallas_call(
        functools.partial(_conv_kernel, tt=tt, width=width),
        grid=(bsz, nt),
        in_specs=[
            pl.BlockSpec((tt, n_ch), lambda b, i: (blk0 + b * nt + i, 0)),
            pl.BlockSpec((1, HIST_PAD, n_ch), lambda b, i: (b, 0, 0)),
            pl.BlockSpec(w_dw.shape, const),
            pl.BlockSpec((1, n_ch), const),
            pl.BlockSpec((1, n_ch), const),
            pl.BlockSpec((1, n_ch), const),
        ],
        out_specs=pl.BlockSpec((tt, n_ch), lambda b, i: (b * nt + i, 0)),
        out_shape=jax.ShapeDtypeStruct((bsz * t, n_ch), BF16),
        scratch_shapes=[pltpu.VMEM((tt + HIST_PAD, n_ch), F32), pltpu.VMEM((tt, n_ch), F32)],
        compiler_params=_params(("arbitrary", "arbitrary")),
        name="conv",
    )(u_all, hist_pad, w_dw, b_dw, ln_g, ln_b)


def _gla_kernel(q_ref, k_ref, v_ref, g_ref, la_ref, s0_ref, gn_ref, o_ref, sout_ref, state,
                *, chunk, heads, dk, dv):
    i = pl.program_id(1)

    @pl.when(i == 0)
    def _():
        state[...] = s0_ref[0]

    la = la_ref[...]
    r = lax.broadcasted_iota(I32, (chunk, chunk), 0)
    c = lax.broadcasted_iota(I32, (chunk, chunk), 1)
    causal = c <= r
    b = jnp.dot(causal.astype(F32), la, precision=HIGHEST, preferred_element_type=F32)
    b_end = b[chunk - 1:chunk, :]
    q_in = (q_ref[...] * jnp.exp(b)).astype(BF16)
    k_in = (k_ref[...] * jnp.exp(-b)).astype(BF16)
    k_out = (k_ref[...] * jnp.exp(b_end - b)).astype(BF16)
    decay_row = jnp.exp(b_end)
    eye = lax.broadcasted_iota(I32, (dk, dk), 0) == lax.broadcasted_iota(I32, (dk, dk), 1)
    for h in range(heads):
        ks = slice(h * dk, (h + 1) * dk)
        vs = slice(h * dv, (h + 1) * dv)
        vh = v_ref[:, vs].astype(BF16)
        att = lax.dot_general(q_in[:, ks], k_in[:, ks], (((1,), (1,)), ((), ())), preferred_element_type=F32)
        att = jnp.where(causal, att, 0.0).astype(BF16)
        s_h = state[h]
        o = jnp.dot(att, vh, preferred_element_type=F32)
        o = o + jnp.dot(q_in[:, ks], s_h.astype(BF16), preferred_element_type=F32)
        decay_col = jnp.sum(jnp.where(eye, jnp.broadcast_to(decay_row[:, ks], (dk, dk)), 0.0), axis=1, keepdims=True)
        state[h] = decay_col * s_h + lax.dot_general(k_out[:, ks], vh, (((0,), (0,)), ((), ())),
                                                     preferred_element_type=F32)
        o = o * lax.rsqrt(jnp.mean(o * o, axis=-1, keepdims=True) + EPS) * gn_ref[...]
        o_ref[:, vs] = (o * _silu(g_ref[:, vs])).astype(o_ref.dtype)

    @pl.when(i == pl.num_programs(1) - 1)
    def _():
        sout_ref[0] = state[...]


def _gla(q_all, k_all, v_all, g_all, la_all, s0, gn, *, row0, bsz, t, chunk):
    heads, dk, dv = s0.shape[1:]
    nt = t // chunk
    blk0 = row0 // chunk
    rows = lambda b, i: (blk0 + b * nt + i, 0)
    return pl.pallas_call(
        functools.partial(_gla_kernel, chunk=chunk, heads=heads, dk=dk, dv=dv),
        grid=(bsz, nt),
        in_specs=[
            pl.BlockSpec((chunk, heads * dk), rows),
            pl.BlockSpec((chunk, heads * dk), rows),
            pl.BlockSpec((chunk, heads * dv), rows),
            pl.BlockSpec((chunk, heads * dv), rows),
            pl.BlockSpec((chunk, heads * dk), rows),
            pl.BlockSpec((1, heads, dk, dv), lambda b, i: (b, 0, 0, 0)),
            pl.BlockSpec((1, dv), lambda b, i: (0, 0)),
        ],
        out_specs=[
            pl.BlockSpec((chunk, heads * dv), lambda b, i: (b * nt + i, 0)),
            pl.BlockSpec((1, heads, dk, dv), lambda b, i: (b, 0, 0, 0)),
        ],
        out_shape=[
            jax.ShapeDtypeStruct((bsz * t, heads * dv), BF16),
            jax.ShapeDtypeStruct((bsz, heads, dk, dv), F32),
        ],
        scratch_shapes=[pltpu.VMEM((heads, dk, dv), F32)],
        compiler_params=_params(("arbitrary", "arbitrary")),
        name="gla",
    )(q_all, k_all, v_all, g_all, la_all, s0, gn)


def _outproj_kernel(xp_ref, xs_ref, cp_ref, cs_ref, op_ref, os_ref, w_ref, g2_ref, wr_ref, br_ref,
                    x1_ref, h2p_ref, ids_ref, gates_ref, *, n_first, c_conv, n_groups, per_group):
    i = pl.program_id(0)
    x = _pick(i, n_first, xp_ref, xs_ref)
    cc = _pick(i, n_first, cp_ref, cs_ref)
    oo = _pick(i, n_first, op_ref, os_ref)
    mix = jnp.dot(cc, w_ref[0:c_conv, :], preferred_element_type=F32)
    mix = mix + jnp.dot(oo, w_ref[c_conv:, :], preferred_element_type=F32)
    x1 = x + mix
    x1_ref[...] = x1
    h2 = _rms(x1, g2_ref[...])
    h2p_ref[...] = _pack_pairs(h2)
    logits = lax.dot_general(wr_ref[...], h2, (((1,), (1,)), ((), ())), precision=HIGHEST,
                             preferred_element_type=F32) + br_ref[...]
    tm = x.shape[0]
    n_exp = n_groups * per_group
    lc = logits[0:n_groups, :]
    mc = jnp.max(lc, axis=0, keepdims=True)
    p_group = 1.0 / jnp.sum(jnp.exp(lc - mc), axis=0, keepdims=True)
    rows_c = lax.broadcasted_iota(I32, (n_groups, tm), 0)
    g_idx = jnp.min(jnp.where(lc == mc, rows_c, n_groups), axis=0, keepdims=True)
    lf = logits[n_groups:n_groups + n_exp, :]
    rows_f = lax.broadcasted_iota(I32, (n_exp, tm), 0)
    in_group = (rows_f >= g_idx * per_group) & (rows_f < (g_idx + 1) * per_group)
    neg = jnp.float32(-jnp.inf)
    l1 = jnp.where(in_group, lf, neg)
    m1 = jnp.max(l1, axis=0, keepdims=True)
    e1 = jnp.min(jnp.where(l1 == m1, rows_f, n_exp), axis=0, keepdims=True)
    l2 = jnp.where(rows_f == e1, neg, l1)
    m2 = jnp.max(l2, axis=0, keepdims=True)
    e2 = jnp.min(jnp.where(l2 == m2, rows_f, n_exp), axis=0, keepdims=True)
    r2 = jnp.exp(m2 - m1)
    w1 = 1.0 / (1.0 + r2)
    row8 = lax.broadcasted_iota(I32, (8, tm), 0)
    ids_ref[...] = jnp.where(row8 == 0, e1, jnp.where(row8 == 1, e2, 0))
    gates_ref[...] = jnp.where(row8 == 0, p_group * w1, jnp.where(row8 == 1, p_group * (r2 * w1), 0.0))


def _outproj(xp, xs, cp, cs, op, os_, w_out, g2, wr, br, *, n_groups, per_group):
    n_p, d = xp.shape
    n_all = n_p + xs.shape[0]
    n_first = n_p // ROW_TILE
    c_conv = cp.shape[1]
    vv = op.shape[1]
    first, second = _split_maps(n_first)
    const = lambda i: (0, 0)
    row = lambda i: (i, 0)
    col = lambda i: (0, i)
    return pl.pallas_call(
        functools.partial(_outproj_kernel, n_first=n_first, c_conv=c_conv, n_groups=n_groups, per_group=per_group),
        grid=(n_all // ROW_TILE,),
        in_specs=[
            pl.BlockSpec((ROW_TILE, d), first), pl.BlockSpec((ROW_TILE, d), second),
            pl.BlockSpec((ROW_TILE, c_conv), first), pl.BlockSpec((ROW_TILE, c_conv), second),
            pl.BlockSpec((ROW_TILE, vv), first), pl.BlockSpec((ROW_TILE, vv), second),
            pl.BlockSpec(w_out.shape, const, pipeline_mode=pl.Buffered(1)),
            pl.BlockSpec((1, d), const),
            pl.BlockSpec(wr.shape, const),
            pl.BlockSpec(br.shape, const),
        ],
        out_specs=[
            pl.BlockSpec((ROW_TILE, d), row), pl.BlockSpec((ROW_TILE, d // 2), row),
            pl.BlockSpec((8, ROW_TILE), col), pl.BlockSpec((8, ROW_TILE), col),
        ],
        out_shape=[
            jax.ShapeDtypeStruct((n_all, d), F32), jax.ShapeDtypeStruct((n_all, d // 2), U32),
            jax.ShapeDtypeStruct((8, n_all), I32), jax.ShapeDtypeStruct((8, n_all), F32),
        ],
        compiler_params=_params(("arbitrary",)),
        name="outproj",
    )(xp, xs, cp, cs, op, os_, w_out, g2, wr, br)


def _plan_kernel(ids_ref, dest_ref, blk_ref, *, n_exp, n_blk_lanes):
    n_tiles = ids_ref.shape[1] // LANES
    rows = lax.broadcasted_iota(I32, (n_exp, LANES), 0)

    def tile(j):
        return pl.ds(pl.multiple_of(j * LANES, LANES), LANES)

    def one_hots(j):
        ids = ids_ref[:, tile(j)]
        return (rows == ids[0:1, :]).astype(F32), (rows == ids[1:2, :]).astype(F32)

    def count_body(j, acc):
        oh0, oh1 = one_hots(j)
        return acc + oh0 + oh1

    counts = jnp.sum(lax.fori_loop(0, n_tiles, count_body, jnp.zeros((n_exp, LANES), F32)), axis=1, keepdims=True)
    padded = jnp.floor((counts + (EXPERT_ROWS - 1)) * (1.0 / EXPERT_ROWS)) * EXPERT_ROWS
    er = lax.broadcasted_iota(I32, (n_exp, n_exp), 0)
    ec = lax.broadcasted_iota(I32, (n_exp, n_exp), 1)
    start = jnp.dot((ec < er).astype(F32), jnp.broadcast_to(padded, (n_exp, LANES)), precision=HIGHEST,
                    preferred_element_type=F32)[:, 0:1]
    ends = start + padded
    first_row = (lax.broadcasted_iota(I32, (n_exp, n_blk_lanes), 1) * EXPERT_ROWS).astype(F32)
    blk_e = jnp.minimum(jnp.sum((ends <= first_row).astype(F32), axis=0, keepdims=True), n_exp - 1.0)
    n_used = jnp.sum(padded, axis=0, keepdims=True) * (1.0 / EXPERT_ROWS)
    row8b = lax.broadcasted_iota(I32, (8, n_blk_lanes), 0)
    blk_ref[...] = jnp.where(row8b == 0, blk_e, jnp.where(row8b == 1, n_used, 0.0)).astype(I32)

    tr = lax.broadcasted_iota(I32, (LANES, LANES), 0)
    tc = lax.broadcasted_iota(I32, (LANES, LANES), 1)
    strictly_earlier = (tr < tc).astype(BF16)
    row8 = lax.broadcasted_iota(I32, (8, LANES), 0)

    def dest_body(j, carry):
        oh0, oh1 = one_hots(j)
        both = oh0 + oh1
        earlier = jnp.dot(both.astype(BF16), strictly_earlier, preferred_element_type=F32)
        base = earlier + carry
        d0 = jnp.sum(oh0 * base, axis=0, keepdims=True)
        d1 = jnp.sum(oh1 * base, axis=0, keepdims=True)
        dest_ref[:, tile(j)] = jnp.where(row8 == 0, d0, jnp.where(row8 == 1, d1, 0.0)).astype(I32)
        return carry + jnp.sum(both, axis=1, keepdims=True)

    lax.fori_loop(0, n_tiles, dest_body, start)


def _plan(ids, *, n_exp, n_blk_lanes):
    n_all = ids.shape[1]
    return pl.pallas_call(
        functools.partial(_plan_kernel, n_exp=n_exp, n_blk_lanes=n_blk_lanes),
        out_shape=[jax.ShapeDtypeStruct((8, n_all), I32), jax.ShapeDtypeStruct((8, n_blk_lanes), I32)],
        compiler_params=pltpu.CompilerParams(vmem_limit_bytes=VMEM_LIMIT),
        name="plan",
    )(ids)


def _invert_kernel(dest_ref, tok_ref, *, n_tok):
    rows = tok_ref.shape[0]

    def init(j, carry):
        for r in range(DMA_UNROLL):
            tok_ref[j * DMA_UNROLL + r] = 0
        return carry

    lax.fori_loop(0, rows // DMA_UNROLL, init, 0)

    def scatter(j, carry):
        for r in range(DMA_UNROLL):
            t = j * DMA_UNROLL + r
            tok_ref[dest_ref[t]] = t
            tok_ref[dest_ref[n_tok + t]] = t
        return carry

    lax.fori_loop(0, n_tok // DMA_UNROLL, scatter, 0)


def _invert(dest_flat, rows_pad, n_tok):
    return pl.pallas_call(
        functools.partial(_invert_kernel, n_tok=n_tok),
        in_specs=[pl.BlockSpec(memory_space=pltpu.SMEM)],
        out_specs=pl.BlockSpec(memory_space=pltpu.SMEM),
        out_shape=jax.ShapeDtypeStruct((rows_pad,), I32),
        name="invert",
    )(dest_flat)


def _row_copy(src, s, dst, d, sem):
    return pltpu.make_async_copy(src.at[pl.ds(s, 1)], dst.at[pl.ds(d, 1)], sem)


def _experts_kernel(blk_e_ref, n_used_ref, tok_ref, h2p_ref, wg_ref, wu_ref, wd_ref, ysp_ref,
                    xbuf, wg_bf, wu_bf, wd_bf, sems):
    b = pl.program_id(0)
    n_used = n_used_ref[0]
    tb = xbuf.shape[1]

    def gather(blk, slot):
        base = blk * tb

        def body(j, carry):
            for r in range(DMA_UNROLL):
                row = j * DMA_UNROLL + r
                _row_copy(h2p_ref, tok_ref[base + row], xbuf.at[slot], row, sems.at[slot]).start()
            return carry

        lax.fori_loop(0, tb // DMA_UNROLL, body, 0)

    @pl.when(b == 0)
    def _():
        gather(0, 0)

    @pl.when(b + 1 < n_used)
    def _():
        gather(b + 1, (b + 1) % 2)

    @pl.when(b < n_used)
    def _():
        slot = b % 2
        e = blk_e_ref[b]
        prev = blk_e_ref[jnp.maximum(b - 1, 0)]

        @pl.when((b == 0) | (e != prev))
        def _():
            wg_bf[...] = wg_ref[0].astype(BF16)
            wu_bf[...] = wu_ref[0].astype(BF16)
            wd_bf[...] = wd_ref[0].astype(BF16)

        pltpu.make_async_copy(h2p_ref.at[pl.ds(0, tb)], xbuf.at[slot], sems.at[slot]).wait()
        hi, lo = _unpack_pairs(xbuf[slot])
        x = jnp.concatenate([hi.astype(BF16), lo.astype(BF16)], axis=1)
        hg = jnp.dot(x, wg_bf[...], preferred_element_type=F32)
        hu = jnp.dot(x, wu_bf[...], preferred_element_type=F32)
        hb = (_silu(hg) * hu).astype(BF16)
        ysp_ref[...] = _pack_pairs(jnp.dot(hb, wd_bf[...], preferred_element_type=F32))

    @pl.when(b >= n_used)
    def _():
        ysp_ref[...] = jnp.zeros_like(ysp_ref)


def _experts(blk_e, n_used, row_tok, h2p, w_gate, w_up, w_down):
    rows_pad = row_tok.shape[0]
    half = h2p.shape[1]
    n_blocks = rows_pad // EXPERT_ROWS
    d, ff = w_gate.shape[1:]
    wmap = lambda b, be, nu, tok: (be[jnp.maximum(jnp.minimum(b, nu[0] - 1), 0)], 0, 0)
    grid_spec = pltpu.PrefetchScalarGridSpec(
        num_scalar_prefetch=3,
        grid=(n_blocks,),
        in_specs=[
            pl.BlockSpec(memory_space=pl.ANY),
            pl.BlockSpec((1, d, ff), wmap),
            pl.BlockSpec((1, d, ff), wmap),
            pl.BlockSpec((1, ff, d), wmap),
        ],
        out_specs=pl.BlockSpec((EXPERT_ROWS, half), lambda b, be, nu, tok: (b, 0)),
        scratch_shapes=[
            pltpu.VMEM((2, EXPERT_ROWS, half), U32),
            pltpu.VMEM((d, ff), BF16), pltpu.VMEM((d, ff), BF16), pltpu.VMEM((ff, d), BF16),
            pltpu.SemaphoreType.DMA((2,)),
        ],
    )
    return pl.pallas_call(
        _experts_kernel,
        grid_spec=grid_spec,
        out_shape=jax.ShapeDtypeStruct((rows_pad, half), U32),
        compiler_params=_params(("arbitrary",)),
        name="experts",
    )(blk_e, n_used, row_tok, h2p, w_gate, w_up, w_down)


def _combine_kernel(dest_ref, dest_next_ref, gates_ref, x1_ref, ysp_ref, gf_ref, yp_ref, ysmp_ref,
                    buf0, buf1, sems, *, n_first):
    i = pl.program_id(0)
    n = pl.num_programs(0)
    tm = x1_ref.shape[0]
    slot = i % 2

    def gather(d_ref, s):
        def body(j, carry):
            for r in range(DMA_UNROLL):
                t = j * DMA_UNROLL + r
                _row_copy(ysp_ref, d_ref[0, t], buf0.at[s], t, sems.at[s]).start()
                _row_copy(ysp_ref, d_ref[1, t], buf1.at[s], t, sems.at[s]).start()
            return carry

        lax.fori_loop(0, tm // DMA_UNROLL, body, 0)

    @pl.when(i == 0)
    def _():
        gather(dest_ref, 0)

    @pl.when(i + 1 < n)
    def _():
        gather(dest_next_ref, 1 - slot)

    pltpu.make_async_copy(ysp_ref.at[pl.ds(0, tm)], buf0.at[slot], sems.at[slot]).wait()
    pltpu.make_async_copy(ysp_ref.at[pl.ds(0, tm)], buf1.at[slot], sems.at[slot]).wait()
    hi0, lo0 = _unpack_pairs(buf0[slot])
    hi1, lo1 = _unpack_pairs(buf1[slot])
    g0 = gates_ref[:, 0:1]
    g1 = gates_ref[:, 1:2]
    moe = jnp.concatenate([g0 * hi0 + g1 * hi1, g0 * lo0 + g1 * lo1], axis=1)
    y = _rms(x1_ref[...] + moe, gf_ref[...])

    @pl.when(i < n_first)
    def _():
        yp_ref[...] = y

    @pl.when(i >= n_first)
    def _():
        ysmp_ref[...] = y


def _combine(dest, gates_t, x1, ysp, gf, *, n_p):
    n_all, d = x1.shape
    half = ysp.shape[1]
    n_first = n_p // ROW_TILE
    n_tiles = n_all // ROW_TILE
    first, second = _split_maps(n_first)
    return pl.pallas_call(
        functools.partial(_combine_kernel, n_first=n_first),
        grid=(n_tiles,),
        in_specs=[
            pl.BlockSpec((8, ROW_TILE), lambda i: (0, i), memory_space=pltpu.SMEM),
            pl.BlockSpec((8, ROW_TILE), lambda i: (0, jnp.minimum(i + 1, n_tiles - 1)), memory_space=pltpu.SMEM),
            pl.BlockSpec((ROW_TILE, 8), lambda i: (i, 0)),
            pl.BlockSpec((ROW_TILE, d), lambda i: (i, 0)),
            pl.BlockSpec(memory_space=pl.ANY),
            pl.BlockSpec((1, d), lambda i: (0, 0)),
        ],
        out_specs=[pl.BlockSpec((ROW_TILE, d), first), pl.BlockSpec((ROW_TILE, d), second)],
        out_shape=[jax.ShapeDtypeStruct((n_p, d), F32), jax.ShapeDtypeStruct((n_all - n_p, d), F32)],
        scratch_shapes=[pltpu.VMEM((2, ROW_TILE, half), U32), pltpu.VMEM((2, ROW_TILE, half), U32),
                        pltpu.SemaphoreType.DMA((2,))],
        compiler_params=_params(("arbitrary",)),
        name="combine",
    )(dest, dest, gates_t, x1, ysp, gf)


def _chunk_for(t):
    return 64 if t % 64 == 0 else t


def kernel(x_prompt, x_sample, cache_conv, state_gla, norm1_g, w_in, w_lr2, b_lr2, w_dw, b_dw, conv_ln_g, conv_ln_b, gla_norm_g, w_out, norm2_g, w_router_coarse, b_router_coarse, w_router_fine, b_router_fine, w_exp_gate, w_exp_up, w_exp_down, norm_f_g):
    assert norm1_g.shape[0] == 1, "single trunk layer"
    bp, tp, d = x_prompt.shape
    bs, ts, _ = x_sample.shape
    heads, dk, dv = state_gla.shape[2:]
    c_conv = w_dw.shape[2]
    width = w_dw.shape[1]
    rank = w_lr2.shape[1]
    qk, vv = heads * dk, heads * dv
    n_groups, _, per_group = w_router_fine.shape[1:]
    n_exp = n_groups * per_group
    n_p, n_s = bp * tp, bs * ts
    n_all = n_p + n_s
    assert n_p % ROW_TILE == 0 and n_s % ROW_TILE == 0 and width - 1 <= HIST_PAD

    xp = x_prompt.reshape(n_p, d)
    xs = x_sample.reshape(n_s, d)
    row = lambda a: a.reshape(1, -1)

    w_in_pad = jnp.pad(w_in[0], ((0, 0), (0, LANES - rank))).astype(BF16)
    w_lr2_pad = jnp.pad(w_lr2[0], ((0, LANES - rank), (0, 0)))
    u, q, k, v, g, la = _inproj(xp, xs, row(norm1_g[0]), w_in_pad, w_lr2_pad, row(b_lr2[0]),
                                c_conv=c_conv, qk=qk, vv=vv, dk=dk)

    hist_p = jnp.zeros((bp, HIST_PAD, c_conv), F32)
    hist_s = jnp.pad(cache_conv[0], ((0, 0), (HIST_PAD - (width - 1), 0), (0, 0)))
    conv_args = (w_dw[0], row(b_dw[0]), row(conv_ln_g[0]), row(conv_ln_b[0]))
    c_p = _conv(u, hist_p, *conv_args, row0=0, bsz=bp, t=tp, tt=_chunk_for(tp))
    c_s = _conv(u, hist_s, *conv_args, row0=n_p, bsz=bs, t=ts, tt=_chunk_for(ts))

    gn = row(gla_norm_g[0])
    s0_p = jnp.zeros((bp, heads, dk, dv), F32)
    o_p, gla_p = _gla(q, k, v, g, la, s0_p, gn, row0=0, bsz=bp, t=tp, chunk=_chunk_for(tp))
    o_s, gla_s = _gla(q, k, v, g, la, state_gla[0], gn, row0=n_p, bsz=bs, t=ts, chunk=_chunk_for(ts))

    wr = jnp.concatenate([w_router_coarse[0].T,
                          jnp.transpose(w_router_fine[0], (0, 2, 1)).reshape(n_exp, d)], axis=0)
    br = jnp.concatenate([b_router_coarse[0], b_router_fine[0].reshape(n_exp)])
    r_rows = -(-(n_groups + n_exp) // 8) * 8
    wr = jnp.pad(wr, ((0, r_rows - wr.shape[0]), (0, 0)))
    br = jnp.pad(br, (0, r_rows - br.shape[0])).reshape(r_rows, 1)
    x1, h2p, ids, gates = _outproj(xp, xs, c_p, c_s, o_p, o_s, w_out[0].astype(BF16), row(norm2_g[0]), wr, br,
                                   n_groups=n_groups, per_group=per_group)

    n_blocks = (2 * n_all) // EXPERT_ROWS + n_exp
    n_blk_lanes = -(-n_blocks // LANES) * LANES
    dest, blk = _plan(ids, n_exp=n_exp, n_blk_lanes=n_blk_lanes)
    row_tok = _invert(dest[0:2].reshape(-1), n_blocks * EXPERT_ROWS, n_all)
    ysp = _experts(blk[0, :n_blocks], blk[1, :1], row_tok, h2p, w_exp_gate[0], w_exp_up[0], w_exp_down[0])
    y_p, y_s = _combine(dest, gates.T, x1, ysp, row(norm_f_g), n_p=n_p)

    keep = width - 1
    u_p = u[:n_p].reshape(bp, tp, c_conv)[:, tp - keep:]
    u_s = u[n_p:].reshape(bs, ts, c_conv)
    conv_sample = jnp.concatenate([cache_conv[0], u_s], axis=1)[:, -keep:]
    return (y_p.reshape(bp, tp, d), y_s.reshape(bs, ts, d), u_p[None], gla_p[None],
            conv_sample[None], gla_s[None])
```

```python
import functools

import jax
import jax.numpy as jnp
from jax import lax
from jax.experimental import pallas as pl
from jax.experimental.pallas import tpu as pltpu

F32 = jnp.float32
BF16 = jnp.bfloat16
I32 = jnp.int32
U32 = jnp.uint32
EPS = 1e-6
GATE_TEMP = 16.0
HIGHEST = lax.Precision.HIGHEST

LANES = 128
ROW_TILE = 128
EXPERT_ROWS = 256
HIST_PAD = 32
DMA_UNROLL = 8
VMEM_LIMIT = 56 * 1024 * 1024


def _params(semantics, vmem=VMEM_LIMIT):
    return pltpu.CompilerParams(dimension_semantics=semantics, vmem_limit_bytes=vmem)


def _rms(x, g):
    return x * lax.rsqrt(jnp.mean(x * x, axis=-1, keepdims=True) + EPS) * g


def _silu(x):
    return x * jax.nn.sigmoid(x)


def _log_sigmoid(z):
    return jnp.minimum(z, 0.0) - jnp.log(1.0 + jnp.exp(-jnp.abs(z)))


def _pick(i, n_first, first_ref, second_ref):
    return jnp.where(i < n_first, first_ref[...], second_ref[...])


def _split_maps(n_first):
    first = lambda i: (jnp.minimum(i, n_first - 1), 0)
    second = lambda i: (jnp.maximum(i - n_first, 0), 0)
    return first, second


def _pack_pairs(x):
    half = x.shape[1] // 2
    hi = lax.bitcast_convert_type(x[:, :half].astype(BF16).astype(F32), U32)
    lo = lax.bitcast_convert_type(x[:, half:].astype(BF16).astype(F32), U32)
    return hi | (lo >> 16)


def _unpack_pairs(p):
    hi = lax.bitcast_convert_type(p & jnp.uint32(0xFFFF0000), F32)
    lo = lax.bitcast_convert_type(p << 16, F32)
    return hi, lo


def _inproj_kernel(xp_ref, xs_ref, g1_ref, w_ref, wlr2_ref, blr2_ref,
                   u_ref, q_ref, k_ref, v_ref, g_ref, la_ref, *, n_first, c_conv, qk, vv, dk):
    i = pl.program_id(0)
    x = _pick(i, n_first, xp_ref, xs_ref)
    h = _rms(x, g1_ref[...]).astype(BF16)

    def mm(lo, width):
        return jnp.dot(h, w_ref[:, lo:lo + width], preferred_element_type=F32)

    a = mm(0, c_conv)
    a_gate = mm(c_conv, c_conv)
    u_ref[...] = a * jax.nn.sigmoid(a_gate)
    off = 2 * c_conv
    q_ref[...] = mm(off, qk) * (dk ** -0.5)
    k_ref[...] = mm(off + qk, qk)
    v_ref[...] = mm(off + 2 * qk, vv)
    g_ref[...] = mm(off + 2 * qk + vv, vv)
    lr = mm(off + 2 * qk + 2 * vv, LANES)
    z = jnp.dot(lr, wlr2_ref[...], precision=HIGHEST, preferred_element_type=F32) + blr2_ref[...]
    la_ref[...] = _log_sigmoid(z) * (1.0 / GATE_TEMP)


def _inproj(xp, xs, g1, w_pad, wlr2_pad, blr2, *, c_conv, qk, vv, dk):
    n_p, d = xp.shape
    n_s = xs.shape[0]
    n_all = n_p + n_s
    n_first = n_p // ROW_TILE
    grid = (n_all // ROW_TILE,)
    first, second = _split_maps(n_first)
    const = lambda i: (0, 0)
    row = lambda i: (i, 0)
    widths = (c_conv, qk, qk, vv, vv, qk)
    return pl.pallas_call(
        functools.partial(_inproj_kernel, n_first=n_first, c_conv=c_conv, qk=qk, vv=vv, dk=dk),
        grid=grid,
        in_specs=[
            pl.BlockSpec((ROW_TILE, d), first),
            pl.BlockSpec((ROW_TILE, d), second),
            pl.BlockSpec((1, d), const),
            pl.BlockSpec(w_pad.shape, const, pipeline_mode=pl.Buffered(1)),
            pl.BlockSpec(wlr2_pad.shape, const),
            pl.BlockSpec((1, qk), const),
        ],
        out_specs=[pl.BlockSpec((ROW_TILE, w), row) for w in widths],
        out_shape=[jax.ShapeDtypeStruct((n_all, w), F32) for w in widths],
        compiler_params=_params(("arbitrary",)),
        name="inproj",
    )(xp, xs, g1, w_pad, wlr2_pad, blr2)


def _conv_kernel(u_ref, hist_ref, w_ref, b_ref, lg_ref, lb_ref, c_ref, win, cbuf, *, tt, width):
    i = pl.program_id(1)

    @pl.when(i == 0)
    def _():
        win[0:HIST_PAD, :] = hist_ref[0]

    @pl.when(i > 0)
    def _():
        win[0:HIST_PAD, :] = win[tt:tt + HIST_PAD, :]

    win[HIST_PAD:HIST_PAD + tt, :] = u_ref[...]
    lead = HIST_PAD - (width - 1)
    n_ch = u_ref.shape[1]
    for cb in range(n_ch // LANES):
        cs = slice(cb * LANES, (cb + 1) * LANES)
        acc = jnp.broadcast_to(b_ref[:, cs], (tt, LANES))
        for j in range(width):
            acc = acc + w_ref[j:j + 1, cs] * win[lead + j:lead + j + tt, cs]
        cbuf[:, cs] = acc
    c = cbuf[...]
    mu = jnp.mean(c, axis=-1, keepdims=True)
    xc = c - mu
    y = xc * lax.rsqrt(jnp.mean(xc * xc, axis=-1, keepdims=True) + EPS) * lg_ref[...] + lb_ref[...]
    c_ref[...] = _silu(y).astype(c_ref.dtype)


def _conv(u_all, hist_pad, w_dw, b_dw, ln_g, ln_b, *, row0, bsz, t, tt):
    n_ch = u_all.shape[1]
    width = w_dw.shape[0]
    nt = t // tt
    blk0 = row0 // tt
    const = lambda b, i: (0, 0)
    return pl.pallas_call(
        functools.partial(_conv_kernel, tt=tt, width=width),
        grid=(bsz, nt),
        in_specs=[
            pl.BlockSpec((tt, n_ch), lambda b, i: (blk0 + b * nt + i, 0)),
            pl.BlockSpec((1, HIST_PAD, n_ch), lambda b, i: (b, 0, 0)),
            pl.BlockSpec(w_dw.shape, const),
            pl.BlockSpec((1, n_ch), const),
            pl.BlockSpec((1, n_ch), const),
            pl.BlockSpec((1, n_ch), const),
        ],
        out_specs=pl.BlockSpec((tt, n_ch), lambda b, i: (b * nt + i, 0)),
        out_shape=jax.ShapeDtypeStruct((bsz * t, n_ch), BF16),
        scratch_shapes=[pltpu.VMEM((tt + HIST_PAD, n_ch), F32), pltpu.VMEM((tt, n_ch), F32)],
        compiler_params=_params(("arbitrary", "arbitrary")),
        name="conv",
    )(u_all, hist_pad, w_dw, b_dw, ln_g, ln_b)


def _gla_kernel(q_ref, k_ref, v_ref, g_ref, la_ref, s0_ref, gn_ref, o_ref, sout_ref, state,
                *, chunk, heads, dk, dv):
    i = pl.program_id(1)

    @pl.when(i == 0)
    def _():
        state[...] = s0_ref[0]

    la = la_ref[...]
    r = lax.broadcasted_iota(I32, (chunk, chunk), 0)
    c = lax.broadcasted_iota(I32, (chunk, chunk), 1)
    causal = c <= r
    b = jnp.dot(causal.astype(F32), la, precision=HIGHEST, preferred_element_type=F32)
    b_end = b[chunk - 1:chunk, :]
    q_in = (q_ref[...] * jnp.exp(b)).astype(BF16)
    k_in = (k_ref[...] * jnp.exp(-b)).astype(BF16)
    k_out = (k_ref[...] * jnp.exp(b_end - b)).astype(BF16)
    decay_row = jnp.exp(b_end)
    eye = lax.broadcasted_iota(I32, (dk, dk), 0) == lax.broadcasted_iota(I32, (dk, dk), 1)
    for h in range(heads):
        ks = slice(h * dk, (h + 1) * dk)
        vs = slice(h * dv, (h + 1) * dv)
        vh = v_ref[:, vs].astype(BF16)
        att = lax.dot_general(q_in[:, ks], k_in[:, ks], (((1,), (1,)), ((), ())), preferred_element_type=F32)
        att = jnp.where(causal, att, 0.0).astype(BF16)
        s_h = state[h]
        o = jnp.dot(att, vh, preferred_element_type=F32)
        o = o + jnp.dot(q_in[:, ks], s_h.astype(BF16), preferred_element_type=F32)
        decay_col = jnp.sum(jnp.where(eye, jnp.broadcast_to(decay_row[:, ks], (dk, dk)), 0.0), axis=1, keepdims=True)
        state[h] = decay_col * s_h + lax.dot_general(k_out[:, ks], vh, (((0,), (0,)), ((), ())),
                                                     preferred_element_type=F32)
        o = o * lax.rsqrt(jnp.mean(o * o, axis=-1, keepdims=True) + EPS) * gn_ref[...]
        o_ref[:, vs] = (o * _silu(g_ref[:, vs])).astype(o_ref.dtype)

    @pl.when(i == pl.num_programs(1) - 1)
    def _():
        sout_ref[0] = state[...]


def _gla(q_all, k_all, v_all, g_all, la_all, s0, gn, *, row0, bsz, t, chunk):
    heads, dk, dv = s0.shape[1:]
    nt = t // chunk
    blk0 = row0 // chunk
    rows = lambda b, i: (blk0 + b * nt + i, 0)
    return pl.pallas_call(
        functools.partial(_gla_kernel, chunk=chunk, heads=heads, dk=dk, dv=dv),
        grid=(bsz, nt),
        in_specs=[
            pl.BlockSpec((chunk, heads * dk), rows),
            pl.BlockSpec((chunk, heads * dk), rows),
            pl.BlockSpec((chunk, heads * dv), rows),
            pl.BlockSpec((chunk, heads * dv), rows),
            pl.BlockSpec((chunk, heads * dk), rows),
            pl.BlockSpec((1, heads, dk, dv), lambda b, i: (b, 0, 0, 0)),
            pl.BlockSpec((1, dv), lambda b, i: (0, 0)),
        ],
        out_specs=[
            pl.BlockSpec((chunk, heads * dv), lambda b, i: (b * nt + i, 0)),
            pl.BlockSpec((1, heads, dk, dv), lambda b, i: (b, 0, 0, 0)),
        ],
        out_shape=[
            jax.ShapeDtypeStruct((bsz * t, heads * dv), BF16),
            jax.ShapeDtypeStruct((bsz, heads, dk, dv), F32),
        ],
        scratch_shapes=[pltpu.VMEM((heads, dk, dv), F32)],
        compiler_params=_params(("arbitrary", "arbitrary")),
        name="gla",
    )(q_all, k_all, v_all, g_all, la_all, s0, gn)


def _outproj_kernel(xp_ref, xs_ref, cp_ref, cs_ref, op_ref, os_ref, w_ref, g2_ref, wr_ref, br_ref,
                    x1_ref, h2p_ref, ids_ref, gates_ref, *, n_first, c_conv, n_groups, per_group):
    i = pl.program_id(0)
    x = _pick(i, n_first, xp_ref, xs_ref)
    cc = _pick(i, n_first, cp_ref, cs_ref)
    oo = _pick(i, n_first, op_ref, os_ref)
    mix = jnp.dot(cc, w_ref[0:c_conv, :], preferred_element_type=F32)
    mix = mix + jnp.dot(oo, w_ref[c_conv:, :], preferred_element_type=F32)
    x1 = x + mix
    x1_ref[...] = x1
    h2 = _rms(x1, g2_ref[...])
    h2p_ref[...] = _pack_pairs(h2)
    logits = lax.dot_general(wr_ref[...], h2, (((1,), (1,)), ((), ())), precision=HIGHEST,
                             preferred_element_type=F32) + br_ref[...]
    tm = x.shape[0]
    n_exp = n_groups * per_group
    lc = logits[0:n_groups, :]
    mc = jnp.max(lc, axis=0, keepdims=True)
    p_group = 1.0 / jnp.sum(jnp.exp(lc - mc), axis=0, keepdims=True)
    rows_c = lax.broadcasted_iota(I32, (n_groups, tm), 0)
    g_idx = jnp.min(jnp.where(lc == mc, rows_c, n_groups), axis=0, keepdims=True)
    lf = logits[n_groups:n_groups + n_exp, :]
    rows_f = lax.broadcasted_iota(I32, (n_exp, tm), 0)
    in_group = (rows_f >= g_idx * per_group) & (rows_f < (g_idx + 1) * per_group)
    neg = jnp.float32(-jnp.inf)
    l1 = jnp.where(in_group, lf, neg)
    m1 = jnp.max(l1, axis=0, keepdims=True)
    e1 = jnp.min(jnp.where(l1 == m1, rows_f, n_exp), axis=0, keepdims=True)
    l2 = jnp.where(rows_f == e1, neg, l1)
    m2 = jnp.max(l2, axis=0, keepdims=True)
    e2 = jnp.min(jnp.where(l2 == m2, rows_f, n_exp), axis=0, keepdims=True)
    r2 = jnp.exp(m2 - m1)
    w1 = 1.0 / (1.0 + r2)
    row8 = lax.broadcasted_iota(I32, (8, tm), 0)
    ids_ref[...] = jnp.where(row8 == 0, e1, jnp.where(row8 == 1, e2, 0))
    gates_ref[...] = jnp.where(row8 == 0, p_group * w1, jnp.where(row8 == 1, p_group * (r2 * w1), 0.0))


def _outproj(xp, xs, cp, cs, op, os_, w_out, g2, wr, br, *, n_groups, per_group):
    n_p, d = xp.shape
    n_all = n_p + xs.shape[0]
    n_first = n_p // ROW_TILE
    c_conv = cp.shape[1]
    vv = op.shape[1]
    first, second = _split_maps(n_first)
    const = lambda i: (0, 0)
    row = lambda i: (i, 0)
    col = lambda i: (0, i)
    return pl.pallas_call(
        functools.partial(_outproj_kernel, n_first=n_first, c_conv=c_conv, n_groups=n_groups, per_group=per_group),
        grid=(n_all // ROW_TILE,),
        in_specs=[
            pl.BlockSpec((ROW_TILE, d), first), pl.BlockSpec((ROW_TILE, d), second),
            pl.BlockSpec((ROW_TILE, c_conv), first), pl.BlockSpec((ROW_TILE, c_conv), second),
            pl.BlockSpec((ROW_TILE, vv), first), pl.BlockSpec((ROW_TILE, vv), second),
            pl.BlockSpec(w_out.shape, const, pipeline_mode=pl.Buffered(1)),
            pl.BlockSpec((1, d), const),
            pl.BlockSpec(wr.shape, const),
            pl.BlockSpec(br.shape, const),
        ],
        out_specs=[
            pl.BlockSpec((ROW_TILE, d), row), pl.BlockSpec((ROW_TILE, d // 2), row),
            pl.BlockSpec((8, ROW_TILE), col), pl.BlockSpec((8, ROW_TILE), col),
        ],
        out_shape=[
            jax.ShapeDtypeStruct((n_all, d), F32), jax.ShapeDtypeStruct((n_all, d // 2), U32),
            jax.ShapeDtypeStruct((8, n_all), I32), jax.ShapeDtypeStruct((8, n_all), F32),
        ],
        compiler_params=_params(("arbitrary",)),
        name="outproj",
    )(xp, xs, cp, cs, op, os_, w_out, g2, wr, br)


def _plan_kernel(ids_ref, dest_ref, blk_ref, *, n_exp, n_blk_lanes):
    n_tiles = ids_ref.shape[1] // LANES
    rows = lax.broadcasted_iota(I32, (n_exp, LANES), 0)

    def tile(j):
        return pl.ds(pl.multiple_of(j * LANES, LANES), LANES)

    def one_hots(j):
        ids = ids_ref[:, tile(j)]
        return (rows == ids[0:1, :]).astype(F32), (rows == ids[1:2, :]).astype(F32)

    def count_body(j, acc):
        oh0, oh1 = one_hots(j)
        return acc + oh0 + oh1

    counts = jnp.sum(lax.fori_loop(0, n_tiles, count_body, jnp.zeros((n_exp, LANES), F32)), axis=1, keepdims=True)
    padded = jnp.floor((counts + (EXPERT_ROWS - 1)) * (1.0 / EXPERT_ROWS)) * EXPERT_ROWS
    er = lax.broadcasted_iota(I32, (n_exp, n_exp), 0)
    ec = lax.broadcasted_iota(I32, (n_exp, n_exp), 1)
    start = jnp.dot((ec < er).astype(F32), jnp.broadcast_to(padded, (n_exp, LANES)), precision=HIGHEST,
                    preferred_element_type=F32)[:, 0:1]
    lane = lax.broadcasted_iota(I32, (n_exp, n_blk_lanes), 1)
    on_diag = lane == lax.broadcasted_iota(I32, (n_exp, n_blk_lanes), 0)
    first_blk = jnp.sum(jnp.where(on_diag, start, 0.0), axis=0, keepdims=True)
    total = jnp.sum(padded, axis=0, keepdims=True)
    first_blk = jnp.where(lane[0:1, :] == n_exp, total, first_blk) * (1.0 / EXPERT_ROWS)
    blk_ref[...] = jnp.broadcast_to(first_blk, (8, n_blk_lanes)).astype(I32)

    tr = lax.broadcasted_iota(I32, (LANES, LANES), 0)
    tc = lax.broadcasted_iota(I32, (LANES, LANES), 1)
    strictly_earlier = (tr < tc).astype(BF16)
    row8 = lax.broadcasted_iota(I32, (8, LANES), 0)

    def dest_body(j, carry):
        oh0, oh1 = one_hots(j)
        both = oh0 + oh1
        earlier = jnp.dot(both.astype(BF16), strictly_earlier, preferred_element_type=F32)
        base = earlier + carry
        d0 = jnp.sum(oh0 * base, axis=0, keepdims=True)
        d1 = jnp.sum(oh1 * base, axis=0, keepdims=True)
        dest_ref[:, tile(j)] = jnp.where(row8 == 0, d0, jnp.where(row8 == 1, d1, 0.0)).astype(I32)
        return carry + jnp.sum(both, axis=1, keepdims=True)

    lax.fori_loop(0, n_tiles, dest_body, start)


def _plan(ids, *, n_exp, n_blk_lanes):
    n_all = ids.shape[1]
    return pl.pallas_call(
        functools.partial(_plan_kernel, n_exp=n_exp, n_blk_lanes=n_blk_lanes),
        out_shape=[jax.ShapeDtypeStruct((8, n_all), I32), jax.ShapeDtypeStruct((8, n_blk_lanes), I32)],
        compiler_params=pltpu.CompilerParams(vmem_limit_bytes=VMEM_LIMIT),
        name="plan",
    )(ids)


def _invert_kernel(dest_ref, tok_ref, *, n_tok):
    rows = tok_ref.shape[0]

    def init(j, carry):
        for r in range(DMA_UNROLL):
            tok_ref[j * DMA_UNROLL + r] = 0
        return carry

    lax.fori_loop(0, rows // DMA_UNROLL, init, 0)

    def scatter(j, carry):
        for r in range(DMA_UNROLL):
            t = j * DMA_UNROLL + r
            tok_ref[dest_ref[t]] = t
            tok_ref[dest_ref[n_tok + t]] = t
        return carry

    lax.fori_loop(0, n_tok // DMA_UNROLL, scatter, 0)


def _invert(dest_flat, rows_pad, n_tok):
    return pl.pallas_call(
        functools.partial(_invert_kernel, n_tok=n_tok),
        in_specs=[pl.BlockSpec(memory_space=pltpu.SMEM)],
        out_specs=pl.BlockSpec(memory_space=pltpu.SMEM),
        out_shape=jax.ShapeDtypeStruct((rows_pad,), I32),
        name="invert",
    )(dest_flat)


def _row_copy(src, s, dst, d, sem):
    return pltpu.make_async_copy(src.at[pl.ds(s, 1)], dst.at[pl.ds(d, 1)], sem)


def _experts_kernel(first_blk_ref, tok_ref, h2p_ref, wg_ref, wu_ref, wd_ref, ysp_ref,
                    xbuf, ybuf, wg_bf, wu_bf, wd_bf, gsems, ysems, *, n_exp):
    e = pl.program_id(0)
    tb = xbuf.shape[1]
    n_blocks = ysp_ref.shape[0] // tb
    b_lo = first_blk_ref[e]
    b_hi = first_blk_ref[e + 1]
    n_total = first_blk_ref[n_exp]

    def gather_wait(slot):
        pltpu.make_async_copy(h2p_ref.at[pl.ds(0, tb)], xbuf.at[slot], gsems.at[slot]).wait()

    def writeback(blk, slot):
        rows = pl.ds(pl.multiple_of(blk * tb, tb), tb)
        return pltpu.make_async_copy(ybuf.at[slot], ysp_ref.at[rows], ysems.at[slot])

    @pl.when(e == 0)
    def _():
        def body(j, carry):
            for r in range(DMA_UNROLL):
                row = j * DMA_UNROLL + r
                _row_copy(h2p_ref, tok_ref[row], xbuf.at[0], row, gsems.at[0]).start()
            return carry

        lax.fori_loop(0, tb // DMA_UNROLL, body, 0)

    wg_bf[...] = wg_ref[0].astype(BF16)
    wu_bf[...] = wu_ref[0].astype(BF16)
    wd_bf[...] = wd_ref[0].astype(BF16)

    def block(b, carry):
        slot = b % 2

        @pl.when(b >= 2)
        def _():
            writeback(b, slot).wait()

        gather_wait(slot)
        nxt = jnp.minimum(b + 1, n_total - 1) * tb
        for row in range(tb):
            _row_copy(h2p_ref, tok_ref[nxt + row], xbuf.at[1 - slot], row, gsems.at[1 - slot]).start()
        hi, lo = _unpack_pairs(xbuf[slot])
        x = jnp.concatenate([hi.astype(BF16), lo.astype(BF16)], axis=1)
        hg = jnp.dot(x, wg_bf[...], preferred_element_type=F32)
        hu = jnp.dot(x, wu_bf[...], preferred_element_type=F32)
        hb = (_silu(hg) * hu).astype(BF16)
        ybuf[slot] = _pack_pairs(jnp.dot(hb, wd_bf[...], preferred_element_type=F32))
        writeback(b, slot).start()
        return carry

    lax.fori_loop(b_lo, b_hi, block, 0)

    @pl.when(e == n_exp - 1)
    def _():
        gather_wait(n_total % 2)

        @pl.when(n_total >= 2)
        def _():
            writeback(0, n_total % 2).wait()

        writeback(0, (n_total + 1) % 2).wait()
        ybuf[0] = jnp.zeros(ybuf.shape[1:], ybuf.dtype)

        def zero(blk, carry):
            writeback(blk, 0).start()
            return carry

        lax.fori_loop(n_total, n_blocks, zero, 0)

        def zero_wait(blk, carry):
            writeback(0, 0).wait()
            return carry

        lax.fori_loop(n_total, n_blocks, zero_wait, 0)


def _experts(first_blk, row_tok, h2p, w_gate, w_up, w_down):
    rows_pad = row_tok.shape[0]
    half = h2p.shape[1]
    n_exp, d, ff = w_gate.shape
    wmap = lambda e, fb, tok: (e, 0, 0)
    grid_spec = pltpu.PrefetchScalarGridSpec(
        num_scalar_prefetch=2,
        grid=(n_exp,),
        in_specs=[
            pl.BlockSpec(memory_space=pl.ANY),
            pl.BlockSpec((1, d, ff), wmap),
            pl.BlockSpec((1, d, ff), wmap),
            pl.BlockSpec((1, ff, d), wmap),
        ],
        out_specs=pl.BlockSpec(memory_space=pl.ANY),
        scratch_shapes=[
            pltpu.VMEM((2, EXPERT_ROWS, half), U32), pltpu.VMEM((2, EXPERT_ROWS, half), U32),
            pltpu.VMEM((d, ff), BF16), pltpu.VMEM((d, ff), BF16), pltpu.VMEM((ff, d), BF16),
            pltpu.SemaphoreType.DMA((2,)), pltpu.SemaphoreType.DMA((2,)),
        ],
    )
    return pl.pallas_call(
        functools.partial(_experts_kernel, n_exp=n_exp),
        grid_spec=grid_spec,
        out_shape=jax.ShapeDtypeStruct((rows_pad, half), U32),
        compiler_params=_params(("arbitrary",)),
        name="experts",
    )(first_blk, row_tok, h2p, w_gate, w_up, w_down)


def _combine_kernel(dest_ref, dest_next_ref, gates_ref, x1_ref, ysp_ref, gf_ref, yp_ref, ysmp_ref,
                    buf0, buf1, sems, *, n_first):
    i = pl.program_id(0)
    n = pl.num_programs(0)
    tm = x1_ref.shape[0]
    slot = i % 2

    def gather(d_ref, s):
        def body(j, carry):
            for r in range(DMA_UNROLL):
                t = j * DMA_UNROLL + r
                _row_copy(ysp_ref, d_ref[0, t], buf0.at[s], t, sems.at[s]).start()
                _row_copy(ysp_ref, d_ref[1, t], buf1.at[s], t, sems.at[s]).start()
            return carry

        lax.fori_loop(0, tm // DMA_UNROLL, body, 0)

    @pl.when(i == 0)
    def _():
        gather(dest_ref, 0)

    @pl.when(i + 1 < n)
    def _():
        gather(dest_next_ref, 1 - slot)

    pltpu.make_async_copy(ysp_ref.at[pl.ds(0, tm)], buf0.at[slot], sems.at[slot]).wait()
    pltpu.make_async_copy(ysp_ref.at[pl.ds(0, tm)], buf1.at[slot], sems.at[slot]).wait()
    hi0, lo0 = _unpack_pairs(buf0[slot])
    hi1, lo1 = _unpack_pairs(buf1[slot])
    g0 = gates_ref[:, 0:1]
    g1 = gates_ref[:, 1:2]
    moe = jnp.concatenate([g0 * hi0 + g1 * hi1, g0 * lo0 + g1 * lo1], axis=1)
    y = _rms(x1_ref[...] + moe, gf_ref[...])

    @pl.when(i < n_first)
    def _():
        yp_ref[...] = y

    @pl.when(i >= n_first)
    def _():
        ysmp_ref[...] = y


def _combine(dest, gates_t, x1, ysp, gf, *, n_p):
    n_all, d = x1.shape
    half = ysp.shape[1]
    n_first = n_p // ROW_TILE
    n_tiles = n_all // ROW_TILE
    first, second = _split_maps(n_first)
    return pl.pallas_call(
        functools.partial(_combine_kernel, n_first=n_first),
        grid=(n_tiles,),
        in_specs=[
            pl.BlockSpec((8, ROW_TILE), lambda i: (0, i), memory_space=pltpu.SMEM),
            pl.BlockSpec((8, ROW_TILE), lambda i: (0, jnp.minimum(i + 1, n_tiles - 1)), memory_space=pltpu.SMEM),
            pl.BlockSpec((ROW_TILE, 8), lambda i: (i, 0)),
            pl.BlockSpec((ROW_TILE, d), lambda i: (i, 0)),
            pl.BlockSpec(memory_space=pl.ANY),
            pl.BlockSpec((1, d), lambda i: (0, 0)),
        ],
        out_specs=[pl.BlockSpec((ROW_TILE, d), first), pl.BlockSpec((ROW_TILE, d), second)],
        out_shape=[jax.ShapeDtypeStruct((n_p, d), F32), jax.ShapeDtypeStruct((n_all - n_p, d), F32)],
        scratch_shapes=[pltpu.VMEM((2, ROW_TILE, half), U32), pltpu.VMEM((2, ROW_TILE, half), U32),
                        pltpu.SemaphoreType.DMA((2,))],
        compiler_params=_params(("arbitrary",)),
        name="combine",
    )(dest, dest, gates_t, x1, ysp, gf)


def _chunk_for(t):
    return 64 if t % 64 == 0 else t


def kernel(x_prompt, x_sample, cache_conv, state_gla, norm1_g, w_in, w_lr2, b_lr2, w_dw, b_dw, conv_ln_g, conv_ln_b, gla_norm_g, w_out, norm2_g, w_router_coarse, b_router_coarse, w_router_fine, b_router_fine, w_exp_gate, w_exp_up, w_exp_down, norm_f_g):
    assert norm1_g.shape[0] == 1, "single trunk layer"
    bp, tp, d = x_prompt.shape
    bs, ts, _ = x_sample.shape
    heads, dk, dv = state_gla.shape[2:]
    c_conv = w_dw.shape[2]
    width = w_dw.shape[1]
    rank = w_lr2.shape[1]
    qk, vv = heads * dk, heads * dv
    n_groups, _, per_group = w_router_fine.shape[1:]
    n_exp = n_groups * per_group
    n_p, n_s = bp * tp, bs * ts
    n_all = n_p + n_s
    assert n_p % ROW_TILE == 0 and n_s % ROW_TILE == 0 and width - 1 <= HIST_PAD

    xp = x_prompt.reshape(n_p, d)
    xs = x_sample.reshape(n_s, d)
    row = lambda a: a.reshape(1, -1)

    w_in_pad = jnp.pad(w_in[0], ((0, 0), (0, LANES - rank))).astype(BF16)
    w_lr2_pad = jnp.pad(w_lr2[0], ((0, LANES - rank), (0, 0)))
    u, q, k, v, g, la = _inproj(xp, xs, row(norm1_g[0]), w_in_pad, w_lr2_pad, row(b_lr2[0]),
                                c_conv=c_conv, qk=qk, vv=vv, dk=dk)

    hist_p = jnp.zeros((bp, HIST_PAD, c_conv), F32)
    hist_s = jnp.pad(cache_conv[0], ((0, 0), (HIST_PAD - (width - 1), 0), (0, 0)))
    conv_args = (w_dw[0], row(b_dw[0]), row(conv_ln_g[0]), row(conv_ln_b[0]))
    c_p = _conv(u, hist_p, *conv_args, row0=0, bsz=bp, t=tp, tt=_chunk_for(tp))
    c_s = _conv(u, hist_s, *conv_args, row0=n_p, bsz=bs, t=ts, tt=_chunk_for(ts))

    gn = row(gla_norm_g[0])
    s0_p = jnp.zeros((bp, heads, dk, dv), F32)
    o_p, gla_p = _gla(q, k, v, g, la, s0_p, gn, row0=0, bsz=bp, t=tp, chunk=_chunk_for(tp))
    o_s, gla_s = _gla(q, k, v, g, la, state_gla[0], gn, row0=n_p, bsz=bs, t=ts, chunk=_chunk_for(ts))

    wr = jnp.concatenate([w_router_coarse[0].T,
                          jnp.transpose(w_router_fine[0], (0, 2, 1)).reshape(n_exp, d)], axis=0)
    br = jnp.concatenate([b_router_coarse[0], b_router_fine[0].reshape(n_exp)])
    r_rows = -(-(n_groups + n_exp) // 8) * 8
    wr = jnp.pad(wr, ((0, r_rows - wr.shape[0]), (0, 0)))
    br = jnp.pad(br, (0, r_rows - br.shape[0])).reshape(r_rows, 1)
    x1, h2p, ids, gates = _outproj(xp, xs, c_p, c_s, o_p, o_s, w_out[0].astype(BF16), row(norm2_g[0]), wr, br,
                                   n_groups=n_groups, per_group=per_group)

    n_blocks = (2 * n_all) // EXPERT_ROWS + n_exp
    n_blk_lanes = -(-n_blocks // LANES) * LANES
    dest, blk = _plan(ids, n_exp=n_exp, n_blk_lanes=n_blk_lanes)
    row_tok = _invert(dest[0:2].reshape(-1), n_blocks * EXPERT_ROWS, n_all)
    ysp = _experts(blk[0, :n_exp + 1], row_tok, h2p, w_exp_gate[0], w_exp_up[0], w_exp_down[0])
    y_p, y_s = _combine(dest, gates.T, x1, ysp, row(norm_f_g), n_p=n_p)

    keep = width - 1
    u_p = u[:n_p].reshape(bp, tp, c_conv)[:, tp - keep:]
    u_s = u[n_p:].reshape(bs, ts, c_conv)
    conv_sample = jnp.concatenate([cache_conv[0], u_s], axis=1)[:, -keep:]
    return (y_p.reshape(bp, tp, d), y_s.reshape(bs, ts, d), u_p[None], gla_p[None],
            conv_sample[None], gla_s[None])
```

```python
import functools

import jax
import jax.numpy as jnp
from jax import lax
from jax.experimental import pallas as pl
from jax.experimental.pallas import tpu as pltpu

F32 = jnp.float32
BF16 = jnp.bfloat16
I32 = jnp.int32
U32 = jnp.uint32
EPS = 1e-6
GATE_TEMP = 16.0
HIGHEST = lax.Precision.HIGHEST

LANES = 128
ROW_TILE = 128
EXPERT_ROWS = 256
HIST_PAD = 32
DMA_UNROLL = 8
VMEM_LIMIT = 56 * 1024 * 1024


def _params(semantics, vmem=VMEM_LIMIT):
    return pltpu.CompilerParams(dimension_semantics=semantics, vmem_limit_bytes=vmem)


def _rms(x, g):
    return x * lax.rsqrt(jnp.mean(x * x, axis=-1, keepdims=True) + EPS) * g


def _silu(x):
    return x * jax.nn.sigmoid(x)


def _log_sigmoid(z):
    return jnp.minimum(z, 0.0) - jnp.log(1.0 + jnp.exp(-jnp.abs(z)))


def _pick(i, n_first, first_ref, second_ref):
    return jnp.where(i < n_first, first_ref[...], second_ref[...])


def _split_maps(n_first):
    first = lambda i: (jnp.minimum(i, n_first - 1), 0)
    second = lambda i: (jnp.maximum(i - n_first, 0), 0)
    return first, second


def _pack_pairs(x):
    half = x.shape[1] // 2
    hi = lax.bitcast_convert_type(x[:, :half].astype(BF16).astype(F32), U32)
    lo = lax.bitcast_convert_type(x[:, half:].astype(BF16).astype(F32), U32)
    return hi | (lo >> 16)


def _unpack_pairs(p):
    hi = lax.bitcast_convert_type(p & jnp.uint32(0xFFFF0000), F32)
    lo = lax.bitcast_convert_type(p << 16, F32)
    return hi, lo


def _inproj_kernel(xp_ref, xs_ref, g1_ref, w_ref, wlr2_ref, blr2_ref,
                   u_ref, q_ref, k_ref, v_ref, g_ref, la_ref, *, n_first, c_conv, qk, vv, dk):
    i = pl.program_id(0)
    x = _pick(i, n_first, xp_ref, xs_ref)
    h = _rms(x, g1_ref[...]).astype(BF16)

    def mm(lo, width):
        return jnp.dot(h, w_ref[:, lo:lo + width], preferred_element_type=F32)

    a = mm(0, c_conv)
    a_gate = mm(c_conv, c_conv)
    u_ref[...] = a * jax.nn.sigmoid(a_gate)
    off = 2 * c_conv
    q_ref[...] = mm(off, qk) * (dk ** -0.5)
    k_ref[...] = mm(off + qk, qk)
    v_ref[...] = mm(off + 2 * qk, vv)
    g_ref[...] = mm(off + 2 * qk + vv, vv)
    lr = mm(off + 2 * qk + 2 * vv, LANES)
    z = jnp.dot(lr, wlr2_ref[...], precision=HIGHEST, preferred_element_type=F32) + blr2_ref[...]
    la_ref[...] = _log_sigmoid(z) * (1.0 / GATE_TEMP)


def _inproj(xp, xs, g1, w_pad, wlr2_pad, blr2, *, c_conv, qk, vv, dk):
    n_p, d = xp.shape
    n_s = xs.shape[0]
    n_all = n_p + n_s
    n_first = n_p // ROW_TILE
    grid = (n_all // ROW_TILE,)
    first, second = _split_maps(n_first)
    const = lambda i: (0, 0)
    row = lambda i: (i, 0)
    widths = (c_conv, qk, qk, vv, vv, qk)
    return pl.pallas_call(
        functools.partial(_inproj_kernel, n_first=n_first, c_conv=c_conv, qk=qk, vv=vv, dk=dk),
        grid=grid,
        in_specs=[
            pl.BlockSpec((ROW_TILE, d), first),
            pl.BlockSpec((ROW_TILE, d), second),
            pl.BlockSpec((1, d), const),
            pl.BlockSpec(w_pad.shape, const, pipeline_mode=pl.Buffered(1)),
            pl.BlockSpec(wlr2_pad.shape, const),
            pl.BlockSpec((1, qk), const),
        ],
        out_specs=[pl.BlockSpec((ROW_TILE, w), row) for w in widths],
        out_shape=[jax.ShapeDtypeStruct((n_all, w), F32) for w in widths],
        compiler_params=_params(("arbitrary",)),
        name="inproj",
    )(xp, xs, g1, w_pad, wlr2_pad, blr2)


def _conv_kernel(u_ref, hist_ref, w_ref, b_ref, lg_ref, lb_ref, c_ref, win, cbuf, *, tt, width):
    i = pl.program_id(1)

    @pl.when(i == 0)
    def _():
        win[0:HIST_PAD, :] = hist_ref[0]

    @pl.when(i > 0)
    def _():
        win[0:HIST_PAD, :] = win[tt:tt + HIST_PAD, :]

    win[HIST_PAD:HIST_PAD + tt, :] = u_ref[...]
    lead = HIST_PAD - (width - 1)
    n_ch = u_ref.shape[1]
    for cb in range(n_ch // LANES):
        cs = slice(cb * LANES, (cb + 1) * LANES)
        acc = jnp.broadcast_to(b_ref[:, cs], (tt, LANES))
        for j in range(width):
            acc = acc + w_ref[j:j + 1, cs] * win[lead + j:lead + j + tt, cs]
        cbuf[:, cs] = acc
    c = cbuf[...]
    mu = jnp.mean(c, axis=-1, keepdims=True)
    xc = c - mu
    y = xc * lax.rsqrt(jnp.mean(xc * xc, axis=-1, keepdims=True) + EPS) * lg_ref[...] + lb_ref[...]
    c_ref[...] = _silu(y).astype(c_ref.dtype)


def _conv(u_all, hist_pad, w_dw, b_dw, ln_g, ln_b, *, row0, bsz, t, tt):
    n_ch = u_all.shape[1]
    width = w_dw.shape[0]
    nt = t // tt
    blk0 = row0 // tt
    const = lambda b, i: (0, 0)
    return pl.pallas_call(
        functools.partial(_conv_kernel, tt=tt, width=width),
        grid=(bsz, nt),
        in_specs=[
            pl.BlockSpec((tt, n_ch), lambda b, i: (blk0 + b * nt + i, 0)),
            pl.BlockSpec((1, HIST_PAD, n_ch), lambda b, i: (b, 0, 0)),
            pl.BlockSpec(w_dw.shape, const),
            pl.BlockSpec((1, n_ch), const),
            pl.BlockSpec((1, n_ch), const),
            pl.BlockSpec((1, n_ch), const),
        ],
        out_specs=pl.BlockSpec((tt, n_ch), lambda b, i: (b * nt + i, 0)),
        out_shape=jax.ShapeDtypeStruct((bsz * t, n_ch), BF16),
        scratch_shapes=[pltpu.VMEM((tt + HIST_PAD, n_ch), F32), pltpu.VMEM((tt, n_ch), F32)],
        compiler_params=_params(("arbitrary", "arbitrary")),
        name="conv",
    )(u_all, hist_pad, w_dw, b_dw, ln_g, ln_b)


def _gla_kernel(q_ref, k_ref, v_ref, g_ref, la_ref, s0_ref, gn_ref, o_ref, sout_ref, state,
                *, chunk, heads, dk, dv):
    i = pl.program_id(1)

    @pl.when(i == 0)
    def _():
        state[...] = s0_ref[0]

    la = la_ref[...]
    r = lax.broadcasted_iota(I32, (chunk, chunk), 0)
    c = lax.broadcasted_iota(I32, (chunk, chunk), 1)
    causal = c <= r
    b = jnp.dot(causal.astype(F32), la, precision=HIGHEST, preferred_element_type=F32)
    b_end = b[chunk - 1:chunk, :]
    q_in = (q_ref[...] * jnp.exp(b)).astype(BF16)
    k_in = (k_ref[...] * jnp.exp(-b)).astype(BF16)
    k_out = (k_ref[...] * jnp.exp(b_end - b)).astype(BF16)
    decay_row = jnp.exp(b_end)
    eye = lax.broadcasted_iota(I32, (dk, dk), 0) == lax.broadcasted_iota(I32, (dk, dk), 1)
    for h in range(heads):
        ks = slice(h * dk, (h + 1) * dk)
        vs = slice(h * dv, (h + 1) * dv)
        vh = v_ref[:, vs].astype(BF16)
        att = lax.dot_general(q_in[:, ks], k_in[:, ks], (((1,), (1,)), ((), ())), preferred_element_type=F32)
        att = jnp.where(causal, att, 0.0).astype(BF16)
        s_h = state[h]
        o = jnp.dot(att, vh, preferred_element_type=F32)
        o = o + jnp.dot(q_in[:, ks], s_h.astype(BF16), preferred_element_type=F32)
        decay_col = jnp.sum(jnp.where(eye, jnp.broadcast_to(decay_row[:, ks], (dk, dk)), 0.0), axis=1, keepdims=True)
        state[h] = decay_col * s_h + lax.dot_general(k_out[:, ks], vh, (((0,), (0,)), ((), ())),
                                                     preferred_element_type=F32)
        o = o * lax.rsqrt(jnp.mean(o * o, axis=-1, keepdims=True) + EPS) * gn_ref[...]
        o_ref[:, vs] = (o * _silu(g_ref[:, vs])).astype(o_ref.dtype)

    @pl.when(i == pl.num_programs(1) - 1)
    def _():
        sout_ref[0] = state[...]


def _gla(q_all, k_all, v_all, g_all, la_all, s0, gn, *, row0, bsz, t, chunk):
    heads, dk, dv = s0.shape[1:]
    nt = t // chunk
    blk0 = row0 // chunk
    rows = lambda b, i: (blk0 + b * nt + i, 0)
    return pl.pallas_call(
        functools.partial(_gla_kernel, chunk=chunk, heads=heads, dk=dk, dv=dv),
        grid=(bsz, nt),
        in_specs=[
            pl.BlockSpec((chunk, heads * dk), rows),
            pl.BlockSpec((chunk, heads * dk), rows),
            pl.BlockSpec((chunk, heads * dv), rows),
            pl.BlockSpec((chunk, heads * dv), rows),
            pl.BlockSpec((chunk, heads * dk), rows),
            pl.BlockSpec((1, heads, dk, dv), lambda b, i: (b, 0, 0, 0)),
            pl.BlockSpec((1, dv), lambda b, i: (0, 0)),
        ],
        out_specs=[
            pl.BlockSpec((chunk, heads * dv), lambda b, i: (b * nt + i, 0)),
            pl.BlockSpec((1, heads, dk, dv), lambda b, i: (b, 0, 0, 0)),
        ],
        out_shape=[
            jax.ShapeDtypeStruct((bsz * t, heads * dv), BF16),
            jax.ShapeDtypeStruct((bsz, heads, dk, dv), F32),
        ],
        scratch_shapes=[pltpu.VMEM((heads, dk, dv), F32)],
        compiler_params=_params(("arbitrary", "arbitrary")),
        name="gla",
    )(q_all, k_all, v_all, g_all, la_all, s0, gn)


def _outproj_kernel(xp_ref, xs_ref, cp_ref, cs_ref, op_ref, os_ref, w_ref, g2_ref, wr_ref, br_ref,
                    x1_ref, h2p_ref, ids_ref, gates_ref, *, n_first, c_conv, n_groups, per_group):
    i = pl.program_id(0)
    x = _pick(i, n_first, xp_ref, xs_ref)
    cc = _pick(i, n_first, cp_ref, cs_ref)
    oo = _pick(i, n_first, op_ref, os_ref)
    mix = jnp.dot(cc, w_ref[0:c_conv, :], preferred_element_type=F32)
    mix = mix + jnp.dot(oo, w_ref[c_conv:, :], preferred_element_type=F32)
    x1 = x + mix
    x1_ref[...] = x1
    h2 = _rms(x1, g2_ref[...])
    h2p_ref[...] = _pack_pairs(h2)
    logits = lax.dot_general(wr_ref[...], h2, (((1,), (1,)), ((), ())), precision=HIGHEST,
                             preferred_element_type=F32) + br_ref[...]
    tm = x.shape[0]
    n_exp = n_groups * per_group
    lc = logits[0:n_groups, :]
    mc = jnp.max(lc, axis=0, keepdims=True)
    p_group = 1.0 / jnp.sum(jnp.exp(lc - mc), axis=0, keepdims=True)
    rows_c = lax.broadcasted_iota(I32, (n_groups, tm), 0)
    g_idx = jnp.min(jnp.where(lc == mc, rows_c, n_groups), axis=0, keepdims=True)
    lf = logits[n_groups:n_groups + n_exp, :]
    rows_f = lax.broadcasted_iota(I32, (n_exp, tm), 0)
    in_group = (rows_f >= g_idx * per_group) & (rows_f < (g_idx + 1) * per_group)
    neg = jnp.float32(-jnp.inf)
    l1 = jnp.where(in_group, lf, neg)
    m1 = jnp.max(l1, axis=0, keepdims=True)
    e1 = jnp.min(jnp.where(l1 == m1, rows_f, n_exp), axis=0, keepdims=True)
    l2 = jnp.where(rows_f == e1, neg, l1)
    m2 = jnp.max(l2, axis=0, keepdims=True)
    e2 = jnp.min(jnp.where(l2 == m2, rows_f, n_exp), axis=0, keepdims=True)
    r2 = jnp.exp(m2 - m1)
    w1 = 1.0 / (1.0 + r2)
    row8 = lax.broadcasted_iota(I32, (8, tm), 0)
    ids_ref[...] = jnp.where(row8 == 0, e1, jnp.where(row8 == 1, e2, 0))
    gates_ref[...] = jnp.where(row8 == 0, p_group * w1, jnp.where(row8 == 1, p_group * (r2 * w1), 0.0))


def _outproj(xp, xs, cp, cs, op, os_, w_out, g2, wr, br, *, n_groups, per_group):
    n_p, d = xp.shape
    n_all = n_p + xs.shape[0]
    n_first = n_p // ROW_TILE
    c_conv = cp.shape[1]
    vv = op.shape[1]
    first, second = _split_maps(n_first)
    const = lambda i: (0, 0)
    row = lambda i: (i, 0)
    col = lambda i: (0, i)
    return pl.pallas_call(
        functools.partial(_outproj_kernel, n_first=n_first, c_conv=c_conv, n_groups=n_groups, per_group=per_group),
        grid=(n_all // ROW_TILE,),
        in_specs=[
            pl.BlockSpec((ROW_TILE, d), first), pl.BlockSpec((ROW_TILE, d), second),
            pl.BlockSpec((ROW_TILE, c_conv), first), pl.BlockSpec((ROW_TILE, c_conv), second),
            pl.BlockSpec((ROW_TILE, vv), first), pl.BlockSpec((ROW_TILE, vv), second),
            pl.BlockSpec(w_out.shape, const, pipeline_mode=pl.Buffered(1)),
            pl.BlockSpec((1, d), const),
            pl.BlockSpec(wr.shape, const),
            pl.BlockSpec(br.shape, const),
        ],
        out_specs=[
            pl.BlockSpec((ROW_TILE, d), row), pl.BlockSpec((ROW_TILE, d // 2), row),
            pl.BlockSpec((8, ROW_TILE), col), pl.BlockSpec((8, ROW_TILE), col),
        ],
        out_shape=[
            jax.ShapeDtypeStruct((n_all, d), F32), jax.ShapeDtypeStruct((n_all, d // 2), U32),
            jax.ShapeDtypeStruct((8, n_all), I32), jax.ShapeDtypeStruct((8, n_all), F32),
        ],
        compiler_params=_params(("arbitrary",)),
        name="outproj",
    )(xp, xs, cp, cs, op, os_, w_out, g2, wr, br)


def _plan_kernel(ids_ref, dest_ref, blk_ref, *, n_exp, n_blk_lanes):
    n_tiles = ids_ref.shape[1] // LANES
    rows = lax.broadcasted_iota(I32, (n_exp, LANES), 0)

    def tile(j):
        return pl.ds(pl.multiple_of(j * LANES, LANES), LANES)

    def one_hots(j):
        ids = ids_ref[:, tile(j)]
        return (rows == ids[0:1, :]).astype(F32), (rows == ids[1:2, :]).astype(F32)

    def count_body(j, acc):
        oh0, oh1 = one_hots(j)
        return acc + oh0 + oh1

    counts = jnp.sum(lax.fori_loop(0, n_tiles, count_body, jnp.zeros((n_exp, LANES), F32)), axis=1, keepdims=True)
    padded = jnp.floor((counts + (EXPERT_ROWS - 1)) * (1.0 / EXPERT_ROWS)) * EXPERT_ROWS
    er = lax.broadcasted_iota(I32, (n_exp, n_exp), 0)
    ec = lax.broadcasted_iota(I32, (n_exp, n_exp), 1)
    start = jnp.dot((ec < er).astype(F32), jnp.broadcast_to(padded, (n_exp, LANES)), precision=HIGHEST,
                    preferred_element_type=F32)[:, 0:1]
    lane = lax.broadcasted_iota(I32, (n_exp, n_blk_lanes), 1)
    on_diag = lane == lax.broadcasted_iota(I32, (n_exp, n_blk_lanes), 0)
    first_blk = jnp.sum(jnp.where(on_diag, start, 0.0), axis=0, keepdims=True)
    total = jnp.sum(padded, axis=0, keepdims=True)
    first_blk = jnp.where(lane[0:1, :] == n_exp, total, first_blk) * (1.0 / EXPERT_ROWS)
    blk_ref[...] = jnp.broadcast_to(first_blk, (8, n_blk_lanes)).astype(I32)

    tr = lax.broadcasted_iota(I32, (LANES, LANES), 0)
    tc = lax.broadcasted_iota(I32, (LANES, LANES), 1)
    strictly_earlier = (tr < tc).astype(BF16)
    row8 = lax.broadcasted_iota(I32, (8, LANES), 0)

    def dest_body(j, carry):
        oh0, oh1 = one_hots(j)
        both = oh0 + oh1
        earlier = jnp.dot(both.astype(BF16), strictly_earlier, preferred_element_type=F32)
        base = earlier + carry
        d0 = jnp.sum(oh0 * base, axis=0, keepdims=True)
        d1 = jnp.sum(oh1 * base, axis=0, keepdims=True)
        dest_ref[:, tile(j)] = jnp.where(row8 == 0, d0, jnp.where(row8 == 1, d1, 0.0)).astype(I32)
        return carry + jnp.sum(both, axis=1, keepdims=True)

    lax.fori_loop(0, n_tiles, dest_body, start)


def _plan(ids, *, n_exp, n_blk_lanes):
    n_all = ids.shape[1]
    return pl.pallas_call(
        functools.partial(_plan_kernel, n_exp=n_exp, n_blk_lanes=n_blk_lanes),
        out_shape=[jax.ShapeDtypeStruct((8, n_all), I32), jax.ShapeDtypeStruct((8, n_blk_lanes), I32)],
        compiler_params=pltpu.CompilerParams(vmem_limit_bytes=VMEM_LIMIT),
        name="plan",
    )(ids)


def _invert_kernel(dest_ref, tok_ref, *, n_tok):
    rows = tok_ref.shape[0]

    def init(j, carry):
        for r in range(DMA_UNROLL):
            tok_ref[j * DMA_UNROLL + r] = 0
        return carry

    lax.fori_loop(0, rows // DMA_UNROLL, init, 0)

    def scatter(j, carry):
        for r in range(DMA_UNROLL):
            t = j * DMA_UNROLL + r
            tok_ref[dest_ref[t]] = t
            tok_ref[dest_ref[n_tok + t]] = t
        return carry

    lax.fori_loop(0, n_tok // DMA_UNROLL, scatter, 0)


def _invert(dest_flat, rows_pad, n_tok):
    return pl.pallas_call(
        functools.partial(_invert_kernel, n_tok=n_tok),
        in_specs=[pl.BlockSpec(memory_space=pltpu.SMEM)],
        out_specs=pl.BlockSpec(memory_space=pltpu.SMEM),
        out_shape=jax.ShapeDtypeStruct((rows_pad,), I32),
        name="invert",
    )(dest_flat)


def _row_copy(src, s, dst, d, sem):
    return pltpu.make_async_copy(src.at[pl.ds(s, 1)], dst.at[pl.ds(d, 1)], sem)


def _experts_kernel(first_blk_ref, tok_ref, h2p_ref, wg_ref, wu_ref, wd_ref, ysp_ref,
                    xbuf, ybuf, wg_f32, wu_f32, wd_f32, wg_bf, wu_bf, wd_bf, gsems, ysems, wsems, *, n_exp):
    e = pl.program_id(0)
    tb = xbuf.shape[1]
    n_blocks = ysp_ref.shape[0] // tb
    b_lo = first_blk_ref[e]
    b_hi = first_blk_ref[e + 1]
    n_total = first_blk_ref[n_exp]

    def gather_wait(slot):
        pltpu.make_async_copy(h2p_ref.at[pl.ds(0, tb)], xbuf.at[slot], gsems.at[slot]).wait()

    def writeback(blk, slot):
        rows = pl.ds(pl.multiple_of(blk * tb, tb), tb)
        return pltpu.make_async_copy(ybuf.at[slot], ysp_ref.at[rows], ysems.at[slot])

    @pl.when(e == 0)
    def _():
        def body(j, carry):
            for r in range(DMA_UNROLL):
                row = j * DMA_UNROLL + r
                _row_copy(h2p_ref, tok_ref[row], xbuf.at[0], row, gsems.at[0]).start()
            return carry

        lax.fori_loop(0, tb // DMA_UNROLL, body, 0)

    def weight_copies(ex, slot):
        return (pltpu.make_async_copy(wg_ref.at[ex], wg_f32.at[slot], wsems.at[slot]),
                pltpu.make_async_copy(wu_ref.at[ex], wu_f32.at[slot], wsems.at[slot]),
                pltpu.make_async_copy(wd_ref.at[ex], wd_f32.at[slot], wsems.at[slot]))

    @pl.when(e == 0)
    def _():
        for cp in weight_copies(0, 0):
            cp.start(priority=1)

    @pl.when(e + 1 < n_exp)
    def _():
        for cp in weight_copies(e + 1, (e + 1) % 2):
            cp.start(priority=1)

    wslot = e % 2
    for cp in weight_copies(e, wslot):
        cp.wait()
    wg_bf[...] = wg_f32[wslot].astype(BF16)
    wu_bf[...] = wu_f32[wslot].astype(BF16)
    wd_bf[...] = wd_f32[wslot].astype(BF16)

    def block(b, carry):
        slot = b % 2

        @pl.when(b >= 2)
        def _():
            writeback(b, slot).wait()

        gather_wait(slot)
        nxt = jnp.minimum(b + 1, n_total - 1) * tb
        for row in range(tb):
            _row_copy(h2p_ref, tok_ref[nxt + row], xbuf.at[1 - slot], row, gsems.at[1 - slot]).start()
        hi, lo = _unpack_pairs(xbuf[slot])
        x = jnp.concatenate([hi.astype(BF16), lo.astype(BF16)], axis=1)
        hg = jnp.dot(x, wg_bf[...], preferred_element_type=F32)
        hu = jnp.dot(x, wu_bf[...], preferred_element_type=F32)
        hb = (_silu(hg) * hu).astype(BF16)
        ybuf[slot] = _pack_pairs(jnp.dot(hb, wd_bf[...], preferred_element_type=F32))
        writeback(b, slot).start()
        return carry

    lax.fori_loop(b_lo, b_hi, block, 0)

    @pl.when(e == n_exp - 1)
    def _():
        gather_wait(n_total % 2)

        @pl.when(n_total >= 2)
        def _():
            writeback(0, n_total % 2).wait()

        writeback(0, (n_total + 1) % 2).wait()
        ybuf[0] = jnp.zeros(ybuf.shape[1:], ybuf.dtype)

        def zero(blk, carry):
            writeback(blk, 0).start()
            return carry

        lax.fori_loop(n_total, n_blocks, zero, 0)

        def zero_wait(blk, carry):
            writeback(0, 0).wait()
            return carry

        lax.fori_loop(n_total, n_blocks, zero_wait, 0)


def _experts(first_blk, row_tok, h2p, w_gate, w_up, w_down):
    rows_pad = row_tok.shape[0]
    half = h2p.shape[1]
    n_exp, d, ff = w_gate.shape
    grid_spec = pltpu.PrefetchScalarGridSpec(
        num_scalar_prefetch=2,
        grid=(n_exp,),
        in_specs=[
            pl.BlockSpec(memory_space=pl.ANY),
            pl.BlockSpec(memory_space=pl.ANY),
            pl.BlockSpec(memory_space=pl.ANY),
            pl.BlockSpec(memory_space=pl.ANY),
        ],
        out_specs=pl.BlockSpec(memory_space=pl.ANY),
        scratch_shapes=[
            pltpu.VMEM((2, EXPERT_ROWS, half), U32), pltpu.VMEM((2, EXPERT_ROWS, half), U32),
            pltpu.VMEM((2, d, ff), F32), pltpu.VMEM((2, d, ff), F32), pltpu.VMEM((2, ff, d), F32),
            pltpu.VMEM((d, ff), BF16), pltpu.VMEM((d, ff), BF16), pltpu.VMEM((ff, d), BF16),
            pltpu.SemaphoreType.DMA((2,)), pltpu.SemaphoreType.DMA((2,)), pltpu.SemaphoreType.DMA((2,)),
        ],
    )
    return pl.pallas_call(
        functools.partial(_experts_kernel, n_exp=n_exp),
        grid_spec=grid_spec,
        out_shape=jax.ShapeDtypeStruct((rows_pad, half), U32),
        compiler_params=_params(("arbitrary",)),
        name="experts",
    )(first_blk, row_tok, h2p, w_gate, w_up, w_down)


def _combine_kernel(dest_ref, dest_next_ref, gates_ref, x1_ref, ysp_ref, gf_ref, yp_ref, ysmp_ref,
                    buf0, buf1, sems, *, n_first):
    i = pl.program_id(0)
    n = pl.num_programs(0)
    tm = x1_ref.shape[0]
    slot = i % 2

    def gather(d_ref, s):
        def body(j, carry):
            for r in range(DMA_UNROLL):
                t = j * DMA_UNROLL + r
                _row_copy(ysp_ref, d_ref[0, t], buf0.at[s], t, sems.at[s]).start()
                _row_copy(ysp_ref, d_ref[1, t], buf1.at[s], t, sems.at[s]).start()
            return carry

        lax.fori_loop(0, tm // DMA_UNROLL, body, 0)

    @pl.when(i == 0)
    def _():
        gather(dest_ref, 0)

    @pl.when(i + 1 < n)
    def _():
        gather(dest_next_ref, 1 - slot)

    pltpu.make_async_copy(ysp_ref.at[pl.ds(0, tm)], buf0.at[slot], sems.at[slot]).wait()
    pltpu.make_async_copy(ysp_ref.at[pl.ds(0, tm)], buf1.at[slot], sems.at[slot]).wait()
    hi0, lo0 = _unpack_pairs(buf0[slot])
    hi1, lo1 = _unpack_pairs(buf1[slot])
    g0 = gates_ref[:, 0:1]
    g1 = gates_ref[:, 1:2]
    moe = jnp.concatenate([g0 * hi0 + g1 * hi1, g0 * lo0 + g1 * lo1], axis=1)
    y = _rms(x1_ref[...] + moe, gf_ref[...])

    @pl.when(i < n_first)
    def _():
        yp_ref[...] = y

    @pl.when(i >= n_first)
    def _():
        ysmp_ref[...] = y


def _combine(dest, gates_t, x1, ysp, gf, *, n_p):
    n_all, d = x1.shape
    half = ysp.shape[1]
    n_first = n_p // ROW_TILE
    n_tiles = n_all // ROW_TILE
    first, second = _split_maps(n_first)
    return pl.pallas_call(
        functools.partial(_combine_kernel, n_first=n_first),
        grid=(n_tiles,),
        in_specs=[
            pl.BlockSpec((8, ROW_TILE), lambda i: (0, i), memory_space=pltpu.SMEM),
            pl.BlockSpec((8, ROW_TILE), lambda i: (0, jnp.minimum(i + 1, n_tiles - 1)), memory_space=pltpu.SMEM),
            pl.BlockSpec((ROW_TILE, 8), lambda i: (i, 0)),
            pl.BlockSpec((ROW_TILE, d), lambda i: (i, 0)),
            pl.BlockSpec(memory_space=pl.ANY),
            pl.BlockSpec((1, d), lambda i: (0, 0)),
        ],
        out_specs=[pl.BlockSpec((ROW_TILE, d), first), pl.BlockSpec((ROW_TILE, d), second)],
        out_shape=[jax.ShapeDtypeStruct((n_p, d), F32), jax.ShapeDtypeStruct((n_all - n_p, d), F32)],
        scratch_shapes=[pltpu.VMEM((2, ROW_TILE, half), U32), pltpu.VMEM((2, ROW_TILE, half), U32),
                        pltpu.SemaphoreType.DMA((2,))],
        compiler_params=_params(("arbitrary",)),
        name="combine",
    )(dest, dest, gates_t, x1, ysp, gf)


def _chunk_for(t):
    return 64 if t % 64 == 0 else t


def kernel(x_prompt, x_sample, cache_conv, state_gla, norm1_g, w_in, w_lr2, b_lr2, w_dw, b_dw, conv_ln_g, conv_ln_b, gla_norm_g, w_out, norm2_g, w_router_coarse, b_router_coarse, w_router_fine, b_router_fine, w_exp_gate, w_exp_up, w_exp_down, norm_f_g):
    assert norm1_g.shape[0] == 1, "single trunk layer"
    bp, tp, d = x_prompt.shape
    bs, ts, _ = x_sample.shape
    heads, dk, dv = state_gla.shape[2:]
    c_conv = w_dw.shape[2]
    width = w_dw.shape[1]
    rank = w_lr2.shape[1]
    qk, vv = heads * dk, heads * dv
    n_groups, _, per_group = w_router_fine.shape[1:]
    n_exp = n_groups * per_group
    n_p, n_s = bp * tp, bs * ts
    n_all = n_p + n_s
    assert n_p % ROW_TILE == 0 and n_s % ROW_TILE == 0 and width - 1 <= HIST_PAD

    xp = x_prompt.reshape(n_p, d)
    xs = x_sample.reshape(n_s, d)
    row = lambda a: a.reshape(1, -1)

    w_in_pad = jnp.pad(w_in[0], ((0, 0), (0, LANES - rank))).astype(BF16)
    w_lr2_pad = jnp.pad(w_lr2[0], ((0, LANES - rank), (0, 0)))
    u, q, k, v, g, la = _inproj(xp, xs, row(norm1_g[0]), w_in_pad, w_lr2_pad, row(b_lr2[0]),
                                c_conv=c_conv, qk=qk, vv=vv, dk=dk)

    hist_p = jnp.zeros((bp, HIST_PAD, c_conv), F32)
    hist_s = jnp.pad(cache_conv[0], ((0, 0), (HIST_PAD - (width - 1), 0), (0, 0)))
    conv_args = (w_dw[0], row(b_dw[0]), row(conv_ln_g[0]), row(conv_ln_b[0]))
    c_p = _conv(u, hist_p, *conv_args, row0=0, bsz=bp, t=tp, tt=_chunk_for(tp))
    c_s = _conv(u, hist_s, *conv_args, row0=n_p, bsz=bs, t=ts, tt=_chunk_for(ts))

    gn = row(gla_norm_g[0])
    s0_p = jnp.zeros((bp, heads, dk, dv), F32)
    o_p, gla_p = _gla(q, k, v, g, la, s0_p, gn, row0=0, bsz=bp, t=tp, chunk=_chunk_for(tp))
    o_s, gla_s = _gla(q, k, v, g, la, state_gla[0], gn, row0=n_p, bsz=bs, t=ts, chunk=_chunk_for(ts))

    wr = jnp.concatenate([w_router_coarse[0].T,
                          jnp.transpose(w_router_fine[0], (0, 2, 1)).reshape(n_exp, d)], axis=0)
    br = jnp.concatenate([b_router_coarse[0], b_router_fine[0].reshape(n_exp)])
    r_rows = -(-(n_groups + n_exp) // 8) * 8
    wr = jnp.pad(wr, ((0, r_rows - wr.shape[0]), (0, 0)))
    br = jnp.pad(br, (0, r_rows - br.shape[0])).reshape(r_rows, 1)
    x1, h2p, ids, gates = _outproj(xp, xs, c_p, c_s, o_p, o_s, w_out[0].astype(BF16), row(norm2_g[0]), wr, br,
                                   n_groups=n_groups, per_group=per_group)

    n_blocks = (2 * n_all) // EXPERT_ROWS + n_exp
    n_blk_lanes = -(-n_blocks // LANES) * LANES
    dest, blk = _plan(ids, n_exp=n_exp, n_blk_lanes=n_blk_lanes)
    row_tok = _invert(dest[0:2].reshape(-1), n_blocks * EXPERT_ROWS, n_all)
    ysp = _experts(blk[0, :n_exp + 1], row_tok, h2p, w_exp_gate[0], w_exp_up[0], w_exp_down[0])
    y_p, y_s = _combine(dest, gates.T, x1, ysp, row(norm_f_g), n_p=n_p)

    keep = width - 1
    u_p = u[:n_p].reshape(bp, tp, c_conv)[:, tp - keep:]
    u_s = u[n_p:].reshape(bs, ts, c_conv)
    conv_sample = jnp.concatenate([cache_conv[0], u_s], axis=1)[:, -keep:]
    return (y_p.reshape(bp, tp, d), y_s.reshape(bs, ts, d), u_p[None], gla_p[None],
            conv_sample[None], gla_s[None])
```

```python
import functools

import jax
import jax.numpy as jnp
from jax import lax
from jax.experimental import pallas as pl
from jax.experimental.pallas import tpu as pltpu

F32 = jnp.float32
BF16 = jnp.bfloat16
I32 = jnp.int32
U32 = jnp.uint32
EPS = 1e-6
GATE_TEMP = 16.0
HIGHEST = lax.Precision.HIGHEST

LANES = 128
ROW_TILE = 128
EXPERT_ROWS = 256
HIST_PAD = 32
DMA_UNROLL = 8
VMEM_LIMIT = 56 * 1024 * 1024


def _params(semantics, vmem=VMEM_LIMIT):
    return pltpu.CompilerParams(dimension_semantics=semantics, vmem_limit_bytes=vmem)


def _rms(x, g):
    return x * lax.rsqrt(jnp.mean(x * x, axis=-1, keepdims=True) + EPS) * g


def _silu(x):
    return x * jax.nn.sigmoid(x)


def _log_sigmoid(z):
    return jnp.minimum(z, 0.0) - jnp.log(1.0 + jnp.exp(-jnp.abs(z)))


def _pick(i, n_first, first_ref, second_ref):
    return jnp.where(i < n_first, first_ref[...], second_ref[...])


def _split_maps(n_first):
    first = lambda i: (jnp.minimum(i, n_first - 1), 0)
    second = lambda i: (jnp.maximum(i - n_first, 0), 0)
    return first, second


def _pack_pairs(x):
    half = x.shape[1] // 2
    hi = lax.bitcast_convert_type(x[:, :half].astype(BF16).astype(F32), U32)
    lo = lax.bitcast_convert_type(x[:, half:].astype(BF16).astype(F32), U32)
    return hi | (lo >> 16)


def _store_row_tiles(ref, lead, packed):
    for j in range(packed.shape[1] // LANES):
        ref[lead + (slice(None), j, slice(None))] = packed[:, j * LANES:(j + 1) * LANES]


def _load_row_tiles(ref, lead):
    n_tiles = ref.shape[-2]
    return jnp.concatenate([ref[lead + (slice(None), j, slice(None))] for j in range(n_tiles)], axis=1)


def _unpack_pairs(p):
    hi = lax.bitcast_convert_type(p & jnp.uint32(0xFFFF0000), F32)
    lo = lax.bitcast_convert_type(p << 16, F32)
    return hi, lo


def _inproj_kernel(xp_ref, xs_ref, g1_ref, w_ref, wlr2_ref, blr2_ref,
                   u_ref, q_ref, k_ref, v_ref, g_ref, la_ref, *, n_first, c_conv, qk, vv, dk):
    i = pl.program_id(0)
    x = _pick(i, n_first, xp_ref, xs_ref)
    h = _rms(x, g1_ref[...]).astype(BF16)

    def mm(lo, width):
        return jnp.dot(h, w_ref[:, lo:lo + width], preferred_element_type=F32)

    a = mm(0, c_conv)
    a_gate = mm(c_conv, c_conv)
    u_ref[...] = a * jax.nn.sigmoid(a_gate)
    off = 2 * c_conv
    q_ref[...] = mm(off, qk) * (dk ** -0.5)
    k_ref[...] = mm(off + qk, qk)
    v_ref[...] = mm(off + 2 * qk, vv)
    g_ref[...] = mm(off + 2 * qk + vv, vv)
    lr = mm(off + 2 * qk + 2 * vv, LANES)
    z = jnp.dot(lr, wlr2_ref[...], precision=HIGHEST, preferred_element_type=F32) + blr2_ref[...]
    la_ref[...] = _log_sigmoid(z) * (1.0 / GATE_TEMP)


def _inproj(xp, xs, g1, w_pad, wlr2_pad, blr2, *, c_conv, qk, vv, dk):
    n_p, d = xp.shape
    n_s = xs.shape[0]
    n_all = n_p + n_s
    n_first = n_p // ROW_TILE
    grid = (n_all // ROW_TILE,)
    first, second = _split_maps(n_first)
    const = lambda i: (0, 0)
    row = lambda i: (i, 0)
    widths = (c_conv, qk, qk, vv, vv, qk)
    return pl.pallas_call(
        functools.partial(_inproj_kernel, n_first=n_first, c_conv=c_conv, qk=qk, vv=vv, dk=dk),
        grid=grid,
        in_specs=[
            pl.BlockSpec((ROW_TILE, d), first),
            pl.BlockSpec((ROW_TILE, d), second),
            pl.BlockSpec((1, d), const),
            pl.BlockSpec(w_pad.shape, const, pipeline_mode=pl.Buffered(1)),
            pl.BlockSpec(wlr2_pad.shape, const),
            pl.BlockSpec((1, qk), const),
        ],
        out_specs=[pl.BlockSpec((ROW_TILE, w), row) for w in widths],
        out_shape=[jax.ShapeDtypeStruct((n_all, w), F32) for w in widths],
        compiler_params=_params(("arbitrary",)),
        name="inproj",
    )(xp, xs, g1, w_pad, wlr2_pad, blr2)


def _conv_kernel(u_ref, hist_ref, w_ref, b_ref, lg_ref, lb_ref, c_ref, win, cbuf, *, tt, width):
    i = pl.program_id(1)

    @pl.when(i == 0)
    def _():
        win[0:HIST_PAD, :] = hist_ref[0]

    @pl.when(i > 0)
    def _():
        win[0:HIST_PAD, :] = win[tt:tt + HIST_PAD, :]

    win[HIST_PAD:HIST_PAD + tt, :] = u_ref[...]
    lead = HIST_PAD - (width - 1)
    n_ch = u_ref.shape[1]
    for cb in range(n_ch // LANES):
        cs = slice(cb * LANES, (cb + 1) * LANES)
        acc = jnp.broadcast_to(b_ref[:, cs], (tt, LANES))
        for j in range(width):
            acc = acc + w_ref[j:j + 1, cs] * win[lead + j:lead + j + tt, cs]
        cbuf[:, cs] = acc
    c = cbuf[...]
    mu = jnp.mean(c, axis=-1, keepdims=True)
    xc = c - mu
    y = xc * lax.rsqrt(jnp.mean(xc * xc, axis=-1, keepdims=True) + EPS) * lg_ref[...] + lb_ref[...]
    c_ref[...] = _silu(y).astype(c_ref.dtype)


def _conv(u_all, hist_pad, w_dw, b_dw, ln_g, ln_b, *, row0, bsz, t, tt):
    n_ch = u_all.shape[1]
    width = w_dw.shape[0]
    nt = t // tt
    blk0 = row0 // tt
    const = lambda b, i: (0, 0)
    return pl.pallas_call(
        functools.partial(_conv_kernel, tt=tt, width=width),
        grid=(bsz, nt),
        in_specs=[
            pl.BlockSpec((tt, n_ch), lambda b, i: (blk0 + b * nt + i, 0)),
            pl.BlockSpec((1, HIST_PAD, n_ch), lambda b, i: (b, 0, 0)),
            pl.BlockSpec(w_dw.shape, const),
            pl.BlockSpec((1, n_ch), const),
            pl.BlockSpec((1, n_ch), const),
            pl.BlockSpec((1, n_ch), const),
        ],
        out_specs=pl.BlockSpec((tt, n_ch), lambda b, i: (b * nt + i, 0)),
        out_shape=jax.ShapeDtypeStruct((bsz * t, n_ch), BF16),
        scratch_shapes=[pltpu.VMEM((tt + HIST_PAD, n_ch), F32), pltpu.VMEM((tt, n_ch), F32)],
        compiler_params=_params(("arbitrary", "arbitrary")),
        name="conv",
    )(u_all, hist_pad, w_dw, b_dw, ln_g, ln_b)


def _gla_kernel(q_ref, k_ref, v_ref, g_ref, la_ref, s0_ref, gn_ref, o_ref, sout_ref, state,
                *, chunk, heads, dk, dv):
    i = pl.program_id(1)

    @pl.when(i == 0)
    def _():
        state[...] = s0_ref[0]

    la = la_ref[...]
    r = lax.broadcasted_iota(I32, (chunk, chunk), 0)
    c = lax.broadcasted_iota(I32, (chunk, chunk), 1)
    causal = c <= r
    b = jnp.dot(causal.astype(F32), la, precision=HIGHEST, preferred_element_type=F32)
    b_end = b[chunk - 1:chunk, :]
    q_in = (q_ref[...] * jnp.exp(b)).astype(BF16)
    k_in = (k_ref[...] * jnp.exp(-b)).astype(BF16)
    k_out = (k_ref[...] * jnp.exp(b_end - b)).astype(BF16)
    decay_row = jnp.exp(b_end)
    eye = lax.broadcasted_iota(I32, (dk, dk), 0) == lax.broadcasted_iota(I32, (dk, dk), 1)
    for h in range(heads):
        ks = slice(h * dk, (h + 1) * dk)
        vs = slice(h * dv, (h + 1) * dv)
        vh = v_ref[:, vs].astype(BF16)
        att = lax.dot_general(q_in[:, ks], k_in[:, ks], (((1,), (1,)), ((), ())), preferred_element_type=F32)
        att = jnp.where(causal, att, 0.0).astype(BF16)
        s_h = state[h]
        o = jnp.dot(att, vh, preferred_element_type=F32)
        o = o + jnp.dot(q_in[:, ks], s_h.astype(BF16), preferred_element_type=F32)
        decay_col = jnp.sum(jnp.where(eye, jnp.broadcast_to(decay_row[:, ks], (dk, dk)), 0.0), axis=1, keepdims=True)
        state[h] = decay_col * s_h + lax.dot_general(k_out[:, ks], vh, (((0,), (0,)), ((), ())),
                                                     preferred_element_type=F32)
        o = o * lax.rsqrt(jnp.mean(o * o, axis=-1, keepdims=True) + EPS) * gn_ref[...]
        o_ref[:, vs] = (o * _silu(g_ref[:, vs])).astype(o_ref.dtype)

    @pl.when(i == pl.num_programs(1) - 1)
    def _():
        sout_ref[0] = state[...]


def _gla(q_all, k_all, v_all, g_all, la_all, s0, gn, *, row0, bsz, t, chunk):
    heads, dk, dv = s0.shape[1:]
    nt = t // chunk
    blk0 = row0 // chunk
    rows = lambda b, i: (blk0 + b * nt + i, 0)
    return pl.pallas_call(
        functools.partial(_gla_kernel, chunk=chunk, heads=heads, dk=dk, dv=dv),
        grid=(bsz, nt),
        in_specs=[
            pl.BlockSpec((chunk, heads * dk), rows),
            pl.BlockSpec((chunk, heads * dk), rows),
            pl.BlockSpec((chunk, heads * dv), rows),
            pl.BlockSpec((chunk, heads * dv), rows),
            pl.BlockSpec((chunk, heads * dk), rows),
            pl.BlockSpec((1, heads, dk, dv), lambda b, i: (b, 0, 0, 0)),
            pl.BlockSpec((1, dv), lambda b, i: (0, 0)),
        ],
        out_specs=[
            pl.BlockSpec((chunk, heads * dv), lambda b, i: (b * nt + i, 0)),
            pl.BlockSpec((1, heads, dk, dv), lambda b, i: (b, 0, 0, 0)),
        ],
        out_shape=[
            jax.ShapeDtypeStruct((bsz * t, heads * dv), BF16),
            jax.ShapeDtypeStruct((bsz, heads, dk, dv), F32),
        ],
        scratch_shapes=[pltpu.VMEM((heads, dk, dv), F32)],
        compiler_params=_params(("arbitrary", "arbitrary")),
        name="gla",
    )(q_all, k_all, v_all, g_all, la_all, s0, gn)


def _outproj_kernel(xp_ref, xs_ref, cp_ref, cs_ref, op_ref, os_ref, w_ref, g2_ref, wr_ref, br_ref,
                    x1_ref, h2p_ref, ids_ref, gates_ref, *, n_first, c_conv, n_groups, per_group):
    i = pl.program_id(0)
    x = _pick(i, n_first, xp_ref, xs_ref)
    cc = _pick(i, n_first, cp_ref, cs_ref)
    oo = _pick(i, n_first, op_ref, os_ref)
    mix = jnp.dot(cc, w_ref[0:c_conv, :], preferred_element_type=F32)
    mix = mix + jnp.dot(oo, w_ref[c_conv:, :], preferred_element_type=F32)
    x1 = x + mix
    x1_ref[...] = x1
    h2 = _rms(x1, g2_ref[...])
    _store_row_tiles(h2p_ref, (), _pack_pairs(h2))
    logits = lax.dot_general(wr_ref[...], h2, (((1,), (1,)), ((), ())), precision=HIGHEST,
                             preferred_element_type=F32) + br_ref[...]
    tm = x.shape[0]
    n_exp = n_groups * per_group
    lc = logits[0:n_groups, :]
    mc = jnp.max(lc, axis=0, keepdims=True)
    p_group = 1.0 / jnp.sum(jnp.exp(lc - mc), axis=0, keepdims=True)
    rows_c = lax.broadcasted_iota(I32, (n_groups, tm), 0)
    g_idx = jnp.min(jnp.where(lc == mc, rows_c, n_groups), axis=0, keepdims=True)
    lf = logits[n_groups:n_groups + n_exp, :]
    rows_f = lax.broadcasted_iota(I32, (n_exp, tm), 0)
    in_group = (rows_f >= g_idx * per_group) & (rows_f < (g_idx + 1) * per_group)
    neg = jnp.float32(-jnp.inf)
    l1 = jnp.where(in_group, lf, neg)
    m1 = jnp.max(l1, axis=0, keepdims=True)
    e1 = jnp.min(jnp.where(l1 == m1, rows_f, n_exp), axis=0, keepdims=True)
    l2 = jnp.where(rows_f == e1, neg, l1)
    m2 = jnp.max(l2, axis=0, keepdims=True)
    e2 = jnp.min(jnp.where(l2 == m2, rows_f, n_exp), axis=0, keepdims=True)
    r2 = jnp.exp(m2 - m1)
    w1 = 1.0 / (1.0 + r2)
    row8 = lax.broadcasted_iota(I32, (8, tm), 0)
    ids_ref[...] = jnp.where(row8 == 0, e1, jnp.where(row8 == 1, e2, 0))
    gates_ref[...] = jnp.where(row8 == 0, p_group * w1, jnp.where(row8 == 1, p_group * (r2 * w1), 0.0))


def _outproj(xp, xs, cp, cs, op, os_, w_out, g2, wr, br, *, n_groups, per_group):
    n_p, d = xp.shape
    n_all = n_p + xs.shape[0]
    n_first = n_p // ROW_TILE
    c_conv = cp.shape[1]
    vv = op.shape[1]
    first, second = _split_maps(n_first)
    const = lambda i: (0, 0)
    row = lambda i: (i, 0)
    col = lambda i: (0, i)
    return pl.pallas_call(
        functools.partial(_outproj_kernel, n_first=n_first, c_conv=c_conv, n_groups=n_groups, per_group=per_group),
        grid=(n_all // ROW_TILE,),
        in_specs=[
            pl.BlockSpec((ROW_TILE, d), first), pl.BlockSpec((ROW_TILE, d), second),
            pl.BlockSpec((ROW_TILE, c_conv), first), pl.BlockSpec((ROW_TILE, c_conv), second),
            pl.BlockSpec((ROW_TILE, vv), first), pl.BlockSpec((ROW_TILE, vv), second),
            pl.BlockSpec(w_out.shape, const, pipeline_mode=pl.Buffered(1)),
            pl.BlockSpec((1, d), const),
            pl.BlockSpec(wr.shape, const),
            pl.BlockSpec(br.shape, const),
        ],
        out_specs=[
            pl.BlockSpec((ROW_TILE, d), row), pl.BlockSpec((ROW_TILE, d // 2 // LANES, LANES), lambda i: (i, 0, 0)),
            pl.BlockSpec((8, ROW_TILE), col), pl.BlockSpec((8, ROW_TILE), col),
        ],
        out_shape=[
            jax.ShapeDtypeStruct((n_all, d), F32), jax.ShapeDtypeStruct((n_all, d // 2 // LANES, LANES), U32),
            jax.ShapeDtypeStruct((8, n_all), I32), jax.ShapeDtypeStruct((8, n_all), F32),
        ],
        compiler_params=_params(("arbitrary",)),
        name="outproj",
    )(xp, xs, cp, cs, op, os_, w_out, g2, wr, br)


def _plan_kernel(ids_ref, dest_ref, blk_ref, *, n_exp, n_blk_lanes):
    n_tiles = ids_ref.shape[1] // LANES
    rows = lax.broadcasted_iota(I32, (n_exp, LANES), 0)

    def tile(j):
        return pl.ds(pl.multiple_of(j * LANES, LANES), LANES)

    def one_hots(j):
        ids = ids_ref[:, tile(j)]
        return (rows == ids[0:1, :]).astype(F32), (rows == ids[1:2, :]).astype(F32)

    def count_body(j, acc):
        oh0, oh1 = one_hots(j)
        return acc + oh0 + oh1

    counts = jnp.sum(lax.fori_loop(0, n_tiles, count_body, jnp.zeros((n_exp, LANES), F32)), axis=1, keepdims=True)
    padded = jnp.floor((counts + (EXPERT_ROWS - 1)) * (1.0 / EXPERT_ROWS)) * EXPERT_ROWS
    er = lax.broadcasted_iota(I32, (n_exp, n_exp), 0)
    ec = lax.broadcasted_iota(I32, (n_exp, n_exp), 1)
    start = jnp.dot((ec < er).astype(F32), jnp.broadcast_to(padded, (n_exp, LANES)), precision=HIGHEST,
                    preferred_element_type=F32)[:, 0:1]
    lane = lax.broadcasted_iota(I32, (n_exp, n_blk_lanes), 1)
    on_diag = lane == lax.broadcasted_iota(I32, (n_exp, n_blk_lanes), 0)
    first_blk = jnp.sum(jnp.where(on_diag, start, 0.0), axis=0, keepdims=True)
    total = jnp.sum(padded, axis=0, keepdims=True)
    first_blk = jnp.where(lane[0:1, :] == n_exp, total, first_blk) * (1.0 / EXPERT_ROWS)
    blk_ref[...] = jnp.broadcast_to(first_blk, (8, n_blk_lanes)).astype(I32)

    tr = lax.broadcasted_iota(I32, (LANES, LANES), 0)
    tc = lax.broadcasted_iota(I32, (LANES, LANES), 1)
    strictly_earlier = (tr < tc).astype(BF16)
    row8 = lax.broadcasted_iota(I32, (8, LANES), 0)

    def dest_body(j, carry):
        oh0, oh1 = one_hots(j)
        both = oh0 + oh1
        earlier = jnp.dot(both.astype(BF16), strictly_earlier, preferred_element_type=F32)
        base = earlier + carry
        d0 = jnp.sum(oh0 * base, axis=0, keepdims=True)
        d1 = jnp.sum(oh1 * base, axis=0, keepdims=True)
        dest_ref[:, tile(j)] = jnp.where(row8 == 0, d0, jnp.where(row8 == 1, d1, 0.0)).astype(I32)
        return carry + jnp.sum(both, axis=1, keepdims=True)

    lax.fori_loop(0, n_tiles, dest_body, start)


def _plan(ids, *, n_exp, n_blk_lanes):
    n_all = ids.shape[1]
    return pl.pallas_call(
        functools.partial(_plan_kernel, n_exp=n_exp, n_blk_lanes=n_blk_lanes),
        out_shape=[jax.ShapeDtypeStruct((8, n_all), I32), jax.ShapeDtypeStruct((8, n_blk_lanes), I32)],
        compiler_params=pltpu.CompilerParams(vmem_limit_bytes=VMEM_LIMIT),
        name="plan",
    )(ids)


def _invert_kernel(dest_ref, tok_ref, *, n_tok):
    rows = tok_ref.shape[0]

    def init(j, carry):
        for r in range(DMA_UNROLL):
            tok_ref[j * DMA_UNROLL + r] = 0
        return carry

    lax.fori_loop(0, rows // DMA_UNROLL, init, 0)

    def scatter(j, carry):
        for r in range(DMA_UNROLL):
            t = j * DMA_UNROLL + r
            tok_ref[dest_ref[t]] = t
            tok_ref[dest_ref[n_tok + t]] = t
        return carry

    lax.fori_loop(0, n_tok // DMA_UNROLL, scatter, 0)


def _invert(dest_flat, rows_pad, n_tok):
    return pl.pallas_call(
        functools.partial(_invert_kernel, n_tok=n_tok),
        in_specs=[pl.BlockSpec(memory_space=pltpu.SMEM)],
        out_specs=pl.BlockSpec(memory_space=pltpu.SMEM),
        out_shape=jax.ShapeDtypeStruct((rows_pad,), I32),
        name="invert",
    )(dest_flat)


def _row_copy(src, s, dst, d, sem):
    return pltpu.make_async_copy(src.at[s], dst.at[d], sem)


def _experts_kernel(first_blk_ref, tok_ref, h2p_ref, wg_ref, wu_ref, wd_ref, ysp_ref,
                    xbuf, ybuf, wg_f32, wu_f32, wd_f32, wg_bf, wu_bf, wd_bf, gsems, ysems, wsems, *, n_exp):
    e = pl.program_id(0)
    tb = xbuf.shape[1]
    n_blocks = ysp_ref.shape[0] // tb
    b_lo = first_blk_ref[e]
    b_hi = first_blk_ref[e + 1]
    n_total = first_blk_ref[n_exp]

    def gather_wait(slot):
        pltpu.make_async_copy(h2p_ref.at[pl.ds(0, tb)], xbuf.at[slot], gsems.at[slot]).wait()

    def writeback(blk, slot):
        rows = pl.ds(pl.multiple_of(blk * tb, tb), tb)
        return pltpu.make_async_copy(ybuf.at[slot], ysp_ref.at[rows], ysems.at[slot])

    @pl.when(e == 0)
    def _():
        def body(j, carry):
            for r in range(DMA_UNROLL):
                row = j * DMA_UNROLL + r
                _row_copy(h2p_ref, tok_ref[row], xbuf.at[0], row, gsems.at[0]).start()
            return carry

        lax.fori_loop(0, tb // DMA_UNROLL, body, 0)

    def weight_copies(ex, slot):
        return (pltpu.make_async_copy(wg_ref.at[ex], wg_f32.at[slot], wsems.at[slot]),
                pltpu.make_async_copy(wu_ref.at[ex], wu_f32.at[slot], wsems.at[slot]),
                pltpu.make_async_copy(wd_ref.at[ex], wd_f32.at[slot], wsems.at[slot]))

    @pl.when(e == 0)
    def _():
        for cp in weight_copies(0, 0):
            cp.start(priority=1)

    @pl.when(e + 1 < n_exp)
    def _():
        for cp in weight_copies(e + 1, (e + 1) % 2):
            cp.start(priority=1)

    wslot = e % 2
    for cp in weight_copies(e, wslot):
        cp.wait()
    wg_bf[...] = wg_f32[wslot].astype(BF16)
    wu_bf[...] = wu_f32[wslot].astype(BF16)
    wd_bf[...] = wd_f32[wslot].astype(BF16)

    def block(b, carry):
        slot = b % 2

        @pl.when(b >= 2)
        def _():
            writeback(b, slot).wait()

        gather_wait(slot)
        nxt = jnp.minimum(b + 1, n_total - 1) * tb
        for row in range(tb):
            _row_copy(h2p_ref, tok_ref[nxt + row], xbuf.at[1 - slot], row, gsems.at[1 - slot]).start()
        hi, lo = _unpack_pairs(_load_row_tiles(xbuf, (slot,)))
        x = jnp.concatenate([hi.astype(BF16), lo.astype(BF16)], axis=1)
        hg = jnp.dot(x, wg_bf[...], preferred_element_type=F32)
        hu = jnp.dot(x, wu_bf[...], preferred_element_type=F32)
        hb = (_silu(hg) * hu).astype(BF16)
        _store_row_tiles(ybuf, (slot,), _pack_pairs(jnp.dot(hb, wd_bf[...], preferred_element_type=F32)))
        writeback(b, slot).start()
        return carry

    lax.fori_loop(b_lo, b_hi, block, 0)

    @pl.when(e == n_exp - 1)
    def _():
        gather_wait(n_total % 2)

        @pl.when(n_total >= 2)
        def _():
            writeback(0, n_total % 2).wait()

        writeback(0, (n_total + 1) % 2).wait()
        ybuf[0] = jnp.zeros(ybuf.shape[1:], ybuf.dtype)

        def zero(blk, carry):
            writeback(blk, 0).start()
            return carry

        lax.fori_loop(n_total, n_blocks, zero, 0)

        def zero_wait(blk, carry):
            writeback(0, 0).wait()
            return carry

        lax.fori_loop(n_total, n_blocks, zero_wait, 0)


def _experts(first_blk, row_tok, h2p, w_gate, w_up, w_down):
    rows_pad = row_tok.shape[0]
    tile = h2p.shape[1:]
    n_exp, d, ff = w_gate.shape
    grid_spec = pltpu.PrefetchScalarGridSpec(
        num_scalar_prefetch=2,
        grid=(n_exp,),
        in_specs=[
            pl.BlockSpec(memory_space=pl.ANY),
            pl.BlockSpec(memory_space=pl.ANY),
            pl.BlockSpec(memory_space=pl.ANY),
            pl.BlockSpec(memory_space=pl.ANY),
        ],
        out_specs=pl.BlockSpec(memory_space=pl.ANY),
        scratch_shapes=[
            pltpu.VMEM((2, EXPERT_ROWS) + tile, U32), pltpu.VMEM((2, EXPERT_ROWS) + tile, U32),
            pltpu.VMEM((2, d, ff), F32), pltpu.VMEM((2, d, ff), F32), pltpu.VMEM((2, ff, d), F32),
            pltpu.VMEM((d, ff), BF16), pltpu.VMEM((d, ff), BF16), pltpu.VMEM((ff, d), BF16),
            pltpu.SemaphoreType.DMA((2,)), pltpu.SemaphoreType.DMA((2,)), pltpu.SemaphoreType.DMA((2,)),
        ],
    )
    return pl.pallas_call(
        functools.partial(_experts_kernel, n_exp=n_exp),
        grid_spec=grid_spec,
        out_shape=jax.ShapeDtypeStruct((rows_pad,) + tile, U32),
        compiler_params=_params(("arbitrary",)),
        name="experts",
    )(first_blk, row_tok, h2p, w_gate, w_up, w_down)


def _combine_kernel(dest_ref, dest_next_ref, gates_ref, x1_ref, ysp_ref, gf_ref, yp_ref, ysmp_ref,
                    buf0, buf1, sems, *, n_first):
    i = pl.program_id(0)
    n = pl.num_programs(0)
    tm = x1_ref.shape[0]
    slot = i % 2

    def gather(d_ref, s):
        def body(j, carry):
            for r in range(DMA_UNROLL):
                t = j * DMA_UNROLL + r
                _row_copy(ysp_ref, d_ref[0, t], buf0.at[s], t, sems.at[s]).start()
                _row_copy(ysp_ref, d_ref[1, t], buf1.at[s], t, sems.at[s]).start()
            return carry

        lax.fori_loop(0, tm // DMA_UNROLL, body, 0)

    @pl.when(i == 0)
    def _():
        gather(dest_ref, 0)

    @pl.when(i + 1 < n)
    def _():
        gather(dest_next_ref, 1 - slot)

    pltpu.make_async_copy(ysp_ref.at[pl.ds(0, tm)], buf0.at[slot], sems.at[slot]).wait()
    pltpu.make_async_copy(ysp_ref.at[pl.ds(0, tm)], buf1.at[slot], sems.at[slot]).wait()
    hi0, lo0 = _unpack_pairs(_load_row_tiles(buf0, (slot,)))
    hi1, lo1 = _unpack_pairs(_load_row_tiles(buf1, (slot,)))
    g0 = gates_ref[:, 0:1]
    g1 = gates_ref[:, 1:2]
    moe = jnp.concatenate([g0 * hi0 + g1 * hi1, g0 * lo0 + g1 * lo1], axis=1)
    y = _rms(x1_ref[...] + moe, gf_ref[...])

    @pl.when(i < n_first)
    def _():
        yp_ref[...] = y

    @pl.when(i >= n_first)
    def _():
        ysmp_ref[...] = y


def _combine(dest, gates_t, x1, ysp, gf, *, n_p):
    n_all, d = x1.shape
    tile = ysp.shape[1:]
    n_first = n_p // ROW_TILE
    n_tiles = n_all // ROW_TILE
    first, second = _split_maps(n_first)
    return pl.pallas_call(
        functools.partial(_combine_kernel, n_first=n_first),
        grid=(n_tiles,),
        in_specs=[
            pl.BlockSpec((8, ROW_TILE), lambda i: (0, i), memory_space=pltpu.SMEM),
            pl.BlockSpec((8, ROW_TILE), lambda i: (0, jnp.minimum(i + 1, n_tiles - 1)), memory_space=pltpu.SMEM),
            pl.BlockSpec((ROW_TILE, 8), lambda i: (i, 0)),
            pl.BlockSpec((ROW_TILE, d), lambda i: (i, 0)),
            pl.BlockSpec(memory_space=pl.ANY),
            pl.BlockSpec((1, d), lambda i: (0, 0)),
        ],
        out_specs=[pl.BlockSpec((ROW_TILE, d), first), pl.BlockSpec((ROW_TILE, d), second)],
        out_shape=[jax.ShapeDtypeStruct((n_p, d), F32), jax.ShapeDtypeStruct((n_all - n_p, d), F32)],
        scratch_shapes=[pltpu.VMEM((2, ROW_TILE) + tile, U32), pltpu.VMEM((2, ROW_TILE) + tile, U32),
                        pltpu.SemaphoreType.DMA((2,))],
        compiler_params=_params(("arbitrary",)),
        name="combine",
    )(dest, dest, gates_t, x1, ysp, gf)


def _chunk_for(t):
    return 64 if t % 64 == 0 else t


def kernel(x_prompt, x_sample, cache_conv, state_gla, norm1_g, w_in, w_lr2, b_lr2, w_dw, b_dw, conv_ln_g, conv_ln_b, gla_norm_g, w_out, norm2_g, w_router_coarse, b_router_coarse, w_router_fine, b_router_fine, w_exp_gate, w_exp_up, w_exp_down, norm_f_g):
    assert norm1_g.shape[0] == 1, "single trunk layer"
    bp, tp, d = x_prompt.shape
    bs, ts, _ = x_sample.shape
    heads, dk, dv = state_gla.shape[2:]
    c_conv = w_dw.shape[2]
    width = w_dw.shape[1]
    rank = w_lr2.shape[1]
    qk, vv = heads * dk, heads * dv
    n_groups, _, per_group = w_router_fine.shape[1:]
    n_exp = n_groups * per_group
    n_p, n_s = bp * tp, bs * ts
    n_all = n_p + n_s
    assert n_p % ROW_TILE == 0 and n_s % ROW_TILE == 0 and width - 1 <= HIST_PAD

    xp = x_prompt.reshape(n_p, d)
    xs = x_sample.reshape(n_s, d)
    row = lambda a: a.reshape(1, -1)

    w_in_pad = jnp.pad(w_in[0], ((0, 0), (0, LANES - rank))).astype(BF16)
    w_lr2_pad = jnp.pad(w_lr2[0], ((0, LANES - rank), (0, 0)))
    u, q, k, v, g, la = _inproj(xp, xs, row(norm1_g[0]), w_in_pad, w_lr2_pad, row(b_lr2[0]),
                                c_conv=c_conv, qk=qk, vv=vv, dk=dk)

    hist_p = jnp.zeros((bp, HIST_PAD, c_conv), F32)
    hist_s = jnp.pad(cache_conv[0], ((0, 0), (HIST_PAD - (width - 1), 0), (0, 0)))
    conv_args = (w_dw[0], row(b_dw[0]), row(conv_ln_g[0]), row(conv_ln_b[0]))
    c_p = _conv(u, hist_p, *conv_args, row0=0, bsz=bp, t=tp, tt=_chunk_for(tp))
    c_s = _conv(u, hist_s, *conv_args, row0=n_p, bsz=bs, t=ts, tt=_chunk_for(ts))

    gn = row(gla_norm_g[0])
    s0_p = jnp.zeros((bp, heads, dk, dv), F32)
    o_p, gla_p = _gla(q, k, v, g, la, s0_p, gn, row0=0, bsz=bp, t=tp, chunk=_chunk_for(tp))
    o_s, gla_s = _gla(q, k, v, g, la, state_gla[0], gn, row0=n_p, bsz=bs, t=ts, chunk=_chunk_for(ts))

    wr = jnp.concatenate([w_router_coarse[0].T,
                          jnp.transpose(w_router_fine[0], (0, 2, 1)).reshape(n_exp, d)], axis=0)
    br = jnp.concatenate([b_router_coarse[0], b_router_fine[0].reshape(n_exp)])
    r_rows = -(-(n_groups + n_exp) // 8) * 8
    wr = jnp.pad(wr, ((0, r_rows - wr.shape[0]), (0, 0)))
    br = jnp.pad(br, (0, r_rows - br.shape[0])).reshape(r_rows, 1)
    x1, h2p, ids, gates = _outproj(xp, xs, c_p, c_s, o_p, o_s, w_out[0].astype(BF16), row(norm2_g[0]), wr, br,
                                   n_groups=n_groups, per_group=per_group)

    n_blocks = (2 * n_all) // EXPERT_ROWS + n_exp
    n_blk_lanes = -(-n_blocks // LANES) * LANES
    dest, blk = _plan(ids, n_exp=n_exp, n_blk_lanes=n_blk_lanes)
    row_tok = _invert(dest[0:2].reshape(-1), n_blocks * EXPERT_ROWS, n_all)
    ysp = _experts(blk[0, :n_exp + 1], row_tok, h2p, w_exp_gate[0], w_exp_up[0], w_exp_down[0])
    y_p, y_s = _combine(dest, gates.T, x1, ysp, row(norm_f_g), n_p=n_p)

    keep = width - 1
    u_p = u[:n_p].reshape(bp, tp, c_conv)[:, tp - keep:]
    u_s = u[n_p:].reshape(bs, ts, c_conv)
    conv_sample = jnp.concatenate([cache_conv[0], u_s], axis=1)[:, -keep:]
    return (y_p.reshape(bp, tp, d), y_s.reshape(bs, ts, d), u_p[None], gla_p[None],
            conv_sample[None], gla_s[None])
```

```python
import functools

import jax
import jax.numpy as jnp
from jax import lax
from jax.experimental import pallas as pl
from jax.experimental.pallas import tpu as pltpu

F32 = jnp.float32
BF16 = jnp.bfloat16
I32 = jnp.int32
U32 = jnp.uint32
EPS = 1e-6
GATE_TEMP = 16.0
HIGHEST = lax.Precision.HIGHEST

LANES = 128
ROW_TILE = 128
EXPERT_ROWS = 256
HIST_PAD = 32
DMA_UNROLL = 8
VMEM_LIMIT = 56 * 1024 * 1024


def _params(semantics, vmem=VMEM_LIMIT):
    return pltpu.CompilerParams(dimension_semantics=semantics, vmem_limit_bytes=vmem)


def _rms(x, g):
    return x * lax.rsqrt(jnp.mean(x * x, axis=-1, keepdims=True) + EPS) * g


def _silu(x):
    return x * jax.nn.sigmoid(x)


def _log_sigmoid(z):
    return jnp.minimum(z, 0.0) - jnp.log(1.0 + jnp.exp(-jnp.abs(z)))


def _pick(i, n_first, first_ref, second_ref):
    return jnp.where(i < n_first, first_ref[...], second_ref[...])


def _split_maps(n_first):
    first = lambda i: (jnp.minimum(i, n_first - 1), 0)
    second = lambda i: (jnp.maximum(i - n_first, 0), 0)
    return first, second


def _pack_pairs(x):
    half = x.shape[1] // 2
    hi = lax.bitcast_convert_type(x[:, :half].astype(BF16).astype(F32), U32)
    lo = lax.bitcast_convert_type(x[:, half:].astype(BF16).astype(F32), U32)
    return hi | (lo >> 16)


def _unpack_pairs(p):
    hi = lax.bitcast_convert_type(p & jnp.uint32(0xFFFF0000), F32)
    lo = lax.bitcast_convert_type(p << 16, F32)
    return hi, lo


def _store_row_tiles(ref, lead, packed):
    for j in range(packed.shape[1] // LANES):
        ref[lead + (slice(None), j, slice(None))] = packed[:, j * LANES:(j + 1) * LANES]


def _load_row_tiles(ref, lead):
    n_tiles = ref.shape[-2]
    return jnp.concatenate([ref[lead + (slice(None), j, slice(None))] for j in range(n_tiles)], axis=1)


def _row_copy(src, s, dst, d, sem):
    return pltpu.make_async_copy(src.at[s], dst.at[d], sem)


def _inproj_kernel(xp_ref, xs_ref, g1_ref, w_ref, wlr2_ref, blr2_ref,
                   u_ref, q_ref, k_ref, v_ref, g_ref, la_ref, *, n_first, c_conv, qk, vv, dk):
    i = pl.program_id(0)
    x = _pick(i, n_first, xp_ref, xs_ref)
    h = _rms(x, g1_ref[...]).astype(BF16)

    def mm(lo, width):
        return jnp.dot(h, w_ref[:, lo:lo + width], preferred_element_type=F32)

    a = mm(0, c_conv)
    a_gate = mm(c_conv, c_conv)
    u_ref[...] = a * jax.nn.sigmoid(a_gate)
    off = 2 * c_conv
    q_ref[...] = mm(off, qk) * (dk ** -0.5)
    k_ref[...] = mm(off + qk, qk)
    v_ref[...] = mm(off + 2 * qk, vv)
    g_ref[...] = mm(off + 2 * qk + vv, vv)
    lr = mm(off + 2 * qk + 2 * vv, LANES)
    z = jnp.dot(lr, wlr2_ref[...], precision=HIGHEST, preferred_element_type=F32) + blr2_ref[...]
    la_ref[...] = _log_sigmoid(z) * (1.0 / GATE_TEMP)


def _inproj(xp, xs, g1, w_pad, wlr2_pad, blr2, *, c_conv, qk, vv, dk):
    n_p, d = xp.shape
    n_s = xs.shape[0]
    n_all = n_p + n_s
    n_first = n_p // ROW_TILE
    grid = (n_all // ROW_TILE,)
    first, second = _split_maps(n_first)
    const = lambda i: (0, 0)
    row = lambda i: (i, 0)
    widths = (c_conv, qk, qk, vv, vv, qk)
    return pl.pallas_call(
        functools.partial(_inproj_kernel, n_first=n_first, c_conv=c_conv, qk=qk, vv=vv, dk=dk),
        grid=grid,
        in_specs=[
            pl.BlockSpec((ROW_TILE, d), first),
            pl.BlockSpec((ROW_TILE, d), second),
            pl.BlockSpec((1, d), const),
            pl.BlockSpec(w_pad.shape, const, pipeline_mode=pl.Buffered(1)),
            pl.BlockSpec(wlr2_pad.shape, const),
            pl.BlockSpec((1, qk), const),
        ],
        out_specs=[pl.BlockSpec((ROW_TILE, w), row) for w in widths],
        out_shape=[jax.ShapeDtypeStruct((n_all, w), F32) for w in widths],
        compiler_params=_params(("arbitrary",)),
        name="inproj",
    )(xp, xs, g1, w_pad, wlr2_pad, blr2)


def _conv_kernel(u_ref, hist_ref, w_ref, b_ref, lg_ref, lb_ref, c_ref, win, cbuf, *, tt, width):
    i = pl.program_id(1)

    @pl.when(i == 0)
    def _():
        win[0:HIST_PAD, :] = hist_ref[0]

    @pl.when(i > 0)
    def _():
        win[0:HIST_PAD, :] = win[tt:tt + HIST_PAD, :]

    win[HIST_PAD:HIST_PAD + tt, :] = u_ref[...]
    lead = HIST_PAD - (width - 1)
    n_ch = u_ref.shape[1]
    for cb in range(n_ch // LANES):
        cs = slice(cb * LANES, (cb + 1) * LANES)
        acc = jnp.broadcast_to(b_ref[:, cs], (tt, LANES))
        for j in range(width):
            acc = acc + w_ref[j:j + 1, cs] * win[lead + j:lead + j + tt, cs]
        cbuf[:, cs] = acc
    c = cbuf[...]
    mu = jnp.mean(c, axis=-1, keepdims=True)
    xc = c - mu
    y = xc * lax.rsqrt(jnp.mean(xc * xc, axis=-1, keepdims=True) + EPS) * lg_ref[...] + lb_ref[...]
    c_ref[...] = _silu(y).astype(c_ref.dtype)


def _conv(u_all, hist_pad, w_dw, b_dw, ln_g, ln_b, *, row0, bsz, t, tt):
    n_ch = u_all.shape[1]
    width = w_dw.shape[0]
    nt = t // tt
    blk0 = row0 // tt
    const = lambda b, i: (0, 0)
    return pl.pallas_call(
        functools.partial(_conv_kernel, tt=tt, width=width),
        grid=(bsz, nt),
        in_specs=[
            pl.BlockSpec((tt, n_ch), lambda b, i: (blk0 + b * nt + i, 0)),
            pl.BlockSpec((1, HIST_PAD, n_ch), lambda b, i: (b, 0, 0)),
            pl.BlockSpec(w_dw.shape, const),
            pl.BlockSpec((1, n_ch), const),
            pl.BlockSpec((1, n_ch), const),
            pl.BlockSpec((1, n_ch), const),
        ],
        out_specs=pl.BlockSpec((tt, n_ch), lambda b, i: (b * nt + i, 0)),
        out_shape=jax.ShapeDtypeStruct((bsz * t, n_ch), BF16),
        scratch_shapes=[pltpu.VMEM((tt + HIST_PAD, n_ch), F32), pltpu.VMEM((tt, n_ch), F32)],
        compiler_params=_params(("arbitrary", "arbitrary")),
        name="conv",
    )(u_all, hist_pad, w_dw, b_dw, ln_g, ln_b)


def _gla_kernel(q_ref, k_ref, v_ref, g_ref, la_ref, s0_ref, gn_ref, o_ref, sout_ref, state,
                *, chunk, heads, dk, dv):
    i = pl.program_id(1)

    @pl.when(i == 0)
    def _():
        state[...] = s0_ref[0]

    la = la_ref[...]
    r = lax.broadcasted_iota(I32, (chunk, chunk), 0)
    c = lax.broadcasted_iota(I32, (chunk, chunk), 1)
    causal = c <= r
    b = jnp.dot(causal.astype(F32), la, precision=HIGHEST, preferred_element_type=F32)
    b_end = b[chunk - 1:chunk, :]
    q_in = (q_ref[...] * jnp.exp(b)).astype(BF16)
    k_in = (k_ref[...] * jnp.exp(-b)).astype(BF16)
    k_out = (k_ref[...] * jnp.exp(b_end - b)).astype(BF16)
    decay_row = jnp.exp(b_end)
    eye = lax.broadcasted_iota(I32, (dk, dk), 0) == lax.broadcasted_iota(I32, (dk, dk), 1)
    for h in range(heads):
        ks = slice(h * dk, (h + 1) * dk)
        vs = slice(h * dv, (h + 1) * dv)
        vh = v_ref[:, vs].astype(BF16)
        att = lax.dot_general(q_in[:, ks], k_in[:, ks], (((1,), (1,)), ((), ())), preferred_element_type=F32)
        att = jnp.where(causal, att, 0.0).astype(BF16)
        s_h = state[h]
        o = jnp.dot(att, vh, preferred_element_type=F32)
        o = o + jnp.dot(q_in[:, ks], s_h.astype(BF16), preferred_element_type=F32)
        decay_col = jnp.sum(jnp.where(eye, jnp.broadcast_to(decay_row[:, ks], (dk, dk)), 0.0), axis=1, keepdims=True)
        state[h] = decay_col * s_h + lax.dot_general(k_out[:, ks], vh, (((0,), (0,)), ((), ())),
                                                     preferred_element_type=F32)
        o = o * lax.rsqrt(jnp.mean(o * o, axis=-1, keepdims=True) + EPS) * gn_ref[...]
        o_ref[:, vs] = (o * _silu(g_ref[:, vs])).astype(o_ref.dtype)

    @pl.when(i == pl.num_programs(1) - 1)
    def _():
        sout_ref[0] = state[...]


def _gla(q_all, k_all, v_all, g_all, la_all, s0, gn, *, row0, bsz, t, chunk):
    heads, dk, dv = s0.shape[1:]
    nt = t // chunk
    blk0 = row0 // chunk
    rows = lambda b, i: (blk0 + b * nt + i, 0)
    return pl.pallas_call(
        functools.partial(_gla_kernel, chunk=chunk, heads=heads, dk=dk, dv=dv),
        grid=(bsz, nt),
        in_specs=[
            pl.BlockSpec((chunk, heads * dk), rows),
            pl.BlockSpec((chunk, heads * dk), rows),
            pl.BlockSpec((chunk, heads * dv), rows),
            pl.BlockSpec((chunk, heads * dv), rows),
            pl.BlockSpec((chunk, heads * dk), rows),
            pl.BlockSpec((1, heads, dk, dv), lambda b, i: (b, 0, 0, 0)),
            pl.BlockSpec((1, dv), lambda b, i: (0, 0)),
        ],
        out_specs=[
            pl.BlockSpec((chunk, heads * dv), lambda b, i: (b * nt + i, 0)),
            pl.BlockSpec((1, heads, dk, dv), lambda b, i: (b, 0, 0, 0)),
        ],
        out_shape=[
            jax.ShapeDtypeStruct((bsz * t, heads * dv), BF16),
            jax.ShapeDtypeStruct((bsz, heads, dk, dv), F32),
        ],
        scratch_shapes=[pltpu.VMEM((heads, dk, dv), F32)],
        compiler_params=_params(("arbitrary", "arbitrary")),
        name="gla",
    )(q_all, k_all, v_all, g_all, la_all, s0, gn)


def _outproj_kernel(xp_ref, xs_ref, cp_ref, cs_ref, op_ref, os_ref, w_ref, g2_ref, wr_ref, br_ref,
                    x1_ref, dest_ref, gates_ref, pages_ref, counts_ref, sorted_ref,
                    stage, dest_vmem, dest_smem, cnt_s, page_s, npage_s, table_s, scatter_sems, dest_sems,
                    *, n_first, c_conv, n_groups, per_group, trash_row):
    i = pl.program_id(0)
    n_steps = pl.num_programs(0)
    tm = x1_ref.shape[0]
    n_exp = n_groups * per_group
    slot = i % 2
    prev = 1 - slot

    def dest_copy(s):
        return pltpu.make_async_copy(dest_vmem.at[s], dest_smem.at[s], dest_sems.at[s])

    def scatter_wait(s):
        for _ in range(2):
            pltpu.make_async_copy(stage.at[s], sorted_ref.at[pl.ds(0, tm)], scatter_sems.at[s]).wait()

    @pl.when(i == 0)
    def _():
        cnt_s[...] = jnp.zeros_like(cnt_s)
        page_s[...] = jnp.zeros_like(page_s)
        npage_s[...] = jnp.zeros_like(npage_s)
        table_s[...] = jnp.zeros_like(table_s)
        stage[1] = jnp.zeros(stage.shape[1:], stage.dtype)

        def fill(t, carry):
            dest_smem[1, 0, t] = trash_row + t
            dest_smem[1, 1, t] = trash_row + tm + t
            return carry

        lax.fori_loop(0, tm, fill, 0)

    @pl.when(i >= 1)
    def _():
        dest_copy(prev).wait()
        scatter_wait(slot)

    for t in range(tm):
        _row_copy(stage.at[prev], t, sorted_ref, dest_smem[prev, 0, t], scatter_sems.at[prev]).start()
        _row_copy(stage.at[prev], t, sorted_ref, dest_smem[prev, 1, t], scatter_sems.at[prev]).start()

    x = _pick(i, n_first, xp_ref, xs_ref)
    cc = _pick(i, n_first, cp_ref, cs_ref)
    oo = _pick(i, n_first, op_ref, os_ref)
    mix = jnp.dot(cc, w_ref[0:c_conv, :], preferred_element_type=F32)
    mix = mix + jnp.dot(oo, w_ref[c_conv:, :], preferred_element_type=F32)
    x1 = x + mix
    x1_ref[...] = x1
    h2 = _rms(x1, g2_ref[...])
    _store_row_tiles(stage, (slot,), _pack_pairs(h2))
    logits = lax.dot_general(wr_ref[...], h2, (((1,), (1,)), ((), ())), precision=HIGHEST,
                             preferred_element_type=F32) + br_ref[...]
    lc = logits[0:n_groups, :]
    mc = jnp.max(lc, axis=0, keepdims=True)
    p_group = 1.0 / jnp.sum(jnp.exp(lc - mc), axis=0, keepdims=True)
    rows_c = lax.broadcasted_iota(I32, (n_groups, tm), 0)
    g_idx = jnp.min(jnp.where(lc == mc, rows_c, n_groups), axis=0, keepdims=True)
    lf = logits[n_groups:n_groups + n_exp, :]
    rows_f = lax.broadcasted_iota(I32, (n_exp, tm), 0)
    in_group = (rows_f >= g_idx * per_group) & (rows_f < (g_idx + 1) * per_group)
    neg = jnp.float32(-jnp.inf)
    l1 = jnp.where(in_group, lf, neg)
    m1 = jnp.max(l1, axis=0, keepdims=True)
    e1 = jnp.min(jnp.where(l1 == m1, rows_f, n_exp), axis=0, keepdims=True)
    l2 = jnp.where(rows_f == e1, neg, l1)
    m2 = jnp.max(l2, axis=0, keepdims=True)
    e2 = jnp.min(jnp.where(l2 == m2, rows_f, n_exp), axis=0, keepdims=True)
    r2 = jnp.exp(m2 - m1)
    w1 = 1.0 / (1.0 + r2)
    row8 = lax.broadcasted_iota(I32, (8, tm), 0)
    gates_ref[...] = jnp.where(row8 == 0, p_group * w1, jnp.where(row8 == 1, p_group * (r2 * w1), 0.0))

    oh0 = (rows_f == e1).astype(F32)
    oh1 = (rows_f == e2).astype(F32)
    both = oh0 + oh1
    tr = lax.broadcasted_iota(I32, (tm, tm), 0)
    tc = lax.broadcasted_iota(I32, (tm, tm), 1)
    earlier = jnp.dot(both.astype(BF16), (tr < tc).astype(BF16), preferred_element_type=F32)
    cnt = cnt_s[...]
    rank_base = earlier + cnt
    tile_cnt = jnp.sum(both, axis=1, keepdims=True)
    page_rows = float(EXPERT_ROWS)
    k0 = jnp.floor(cnt * (1.0 / page_rows))
    new_cnt = cnt + tile_cnt
    limit = (k0 + 1.0) * page_rows
    need_a = ((cnt == k0 * page_rows) & (tile_cnt > 0.0)).astype(F32)
    need_b = (new_cnt > limit).astype(F32)
    need = need_a + need_b
    er = lax.broadcasted_iota(I32, (n_exp, n_exp), 0)
    ec = lax.broadcasted_iota(I32, (n_exp, n_exp), 1)
    before = jnp.dot((ec < er).astype(BF16), jnp.broadcast_to(need, (n_exp, LANES)).astype(BF16),
                     preferred_element_type=F32)[:, 0:1]
    base = npage_s[...] + before
    page_a = jnp.where(need_a > 0.0, base, page_s[...])
    page_b = base + need_a
    npage_s[...] = npage_s[...] + jnp.sum(need, axis=0, keepdims=True)
    lane = lax.broadcasted_iota(I32, table_s.shape, 1).astype(F32)
    table = jnp.where((lane == k0) & (need_a > 0.0), page_a, table_s[...])
    table_s[...] = jnp.where((lane == k0 + 1.0) & (need_b > 0.0), page_b, table)
    cnt_s[...] = new_cnt
    page_s[...] = jnp.where(jnp.floor(new_cnt * (1.0 / page_rows)) == k0, page_a, page_b)

    def dest_rows(oh):
        rank = jnp.sum(oh * rank_base, axis=0, keepdims=True)
        lim = jnp.sum(oh * limit, axis=0, keepdims=True)
        pa = jnp.sum(oh * page_a, axis=0, keepdims=True)
        pb = jnp.sum(oh * page_b, axis=0, keepdims=True)
        within = rank - jnp.floor(rank * (1.0 / page_rows)) * page_rows
        return jnp.where(rank < lim, pa, pb) * page_rows + within

    dest = jnp.where(row8 == 0, dest_rows(oh0), jnp.where(row8 == 1, dest_rows(oh1), 0.0)).astype(I32)
    dest_ref[...] = dest
    dest_vmem[slot] = dest
    dest_copy(slot).start()

    @pl.when(i == n_steps - 1)
    def _():
        pages_ref[...] = table_s[...].astype(I32)
        counts_ref[...] = jnp.broadcast_to(cnt_s[...], counts_ref.shape).astype(I32)
        dest_copy(slot).wait()
        scatter_wait(prev)

        def last(j, carry):
            for r in range(DMA_UNROLL):
                t = j * DMA_UNROLL + r
                _row_copy(stage.at[slot], t, sorted_ref, dest_smem[slot, 0, t], scatter_sems.at[slot]).start()
                _row_copy(stage.at[slot], t, sorted_ref, dest_smem[slot, 1, t], scatter_sems.at[slot]).start()
            return carry

        lax.fori_loop(0, tm // DMA_UNROLL, last, 0)
        scatter_wait(slot)


def _outproj(xp, xs, cp, cs, op, os_, w_out, g2, wr, br, *, n_groups, per_group, n_pages):
    n_p, d = xp.shape
    n_all = n_p + xs.shape[0]
    n_first = n_p // ROW_TILE
    n_exp = n_groups * per_group
    c_conv = cp.shape[1]
    vv = op.shape[1]
    tile = (d // 2 // LANES, LANES)
    first, second = _split_maps(n_first)
    const = lambda i: (0, 0)
    row = lambda i: (i, 0)
    col = lambda i: (0, i)
    rows_sorted = (n_pages + 1) * EXPERT_ROWS
    assert 2 * ROW_TILE <= EXPERT_ROWS
    return pl.pallas_call(
        functools.partial(_outproj_kernel, n_first=n_first, c_conv=c_conv, n_groups=n_groups, per_group=per_group,
                          trash_row=n_pages * EXPERT_ROWS),
        grid=(n_all // ROW_TILE,),
        in_specs=[
            pl.BlockSpec((ROW_TILE, d), first), pl.BlockSpec((ROW_TILE, d), second),
            pl.BlockSpec((ROW_TILE, c_conv), first), pl.BlockSpec((ROW_TILE, c_conv), second),
            pl.BlockSpec((ROW_TILE, vv), first), pl.BlockSpec((ROW_TILE, vv), second),
            pl.BlockSpec(w_out.shape, const, pipeline_mode=pl.Buffered(1)),
            pl.BlockSpec((1, d), const),
            pl.BlockSpec(wr.shape, const),
            pl.BlockSpec(br.shape, const),
        ],
        out_specs=[
            pl.BlockSpec((ROW_TILE, d), row),
            pl.BlockSpec((8, ROW_TILE), col), pl.BlockSpec((8, ROW_TILE), col),
            pl.BlockSpec((n_exp, LANES), const), pl.BlockSpec((n_exp, LANES), const),
            pl.BlockSpec(memory_space=pl.ANY),
        ],
        out_shape=[
            jax.ShapeDtypeStruct((n_all, d), F32),
            jax.ShapeDtypeStruct((8, n_all), I32), jax.ShapeDtypeStruct((8, n_all), F32),
            jax.ShapeDtypeStruct((n_exp, LANES), I32), jax.ShapeDtypeStruct((n_exp, LANES), I32),
            jax.ShapeDtypeStruct((rows_sorted,) + tile, U32),
        ],
        scratch_shapes=[
            pltpu.VMEM((2, ROW_TILE) + tile, U32),
            pltpu.VMEM((2, 8, ROW_TILE), I32), pltpu.SMEM((2, 8, ROW_TILE), I32),
            pltpu.VMEM((n_exp, 1), F32), pltpu.VMEM((n_exp, 1), F32), pltpu.VMEM((1, 1), F32),
            pltpu.VMEM((n_exp, LANES), F32),
            pltpu.SemaphoreType.DMA((2,)), pltpu.SemaphoreType.DMA((2,)),
        ],
        compiler_params=_params(("arbitrary",)),
        name="outproj",
    )(xp, xs, cp, cs, op, os_, w_out, g2, wr, br)


def _experts_kernel(cnt_ref, pages_ref, xs_ref, wg_ref, wu_ref, wd_ref, ysp_ref,
                    xbuf, ybuf, wg_f32, wu_f32, wd_f32, wg_bf, wu_bf, wd_bf, first_blk, page_seq,
                    gsems, ysems, wsems, *, n_exp, table_lanes):
    e = pl.program_id(0)
    tb = xbuf.shape[1]
    n_pages = page_seq.shape[0]

    def n_pages_of(ex):
        return (cnt_ref[ex] + (tb - 1)) // tb

    def page_rows(blk):
        return pl.ds(pl.multiple_of(page_seq[blk] * tb, tb), tb)

    def fetch(blk, slot):
        return pltpu.make_async_copy(xs_ref.at[page_rows(blk)], xbuf.at[slot], gsems.at[slot])

    def writeback(blk, slot):
        return pltpu.make_async_copy(ybuf.at[slot], ysp_ref.at[page_rows(blk)], ysems.at[slot])

    def weight_copies(ex, slot):
        return (pltpu.make_async_copy(wg_ref.at[ex], wg_f32.at[slot], wsems.at[slot]),
                pltpu.make_async_copy(wu_ref.at[ex], wu_f32.at[slot], wsems.at[slot]),
                pltpu.make_async_copy(wd_ref.at[ex], wd_f32.at[slot], wsems.at[slot]))

    @pl.when(e == 0)
    def _():
        for cp in weight_copies(0, 0):
            cp.start(priority=1)

        def per_expert(ex, blk):
            first_blk[ex] = blk

            def per_page(j, carry):
                page_seq[blk + j] = pages_ref[ex * table_lanes + j]
                return carry

            lax.fori_loop(0, n_pages_of(ex), per_page, 0)
            return blk + n_pages_of(ex)

        first_blk[n_exp] = lax.fori_loop(0, n_exp, per_expert, 0)
        fetch(0, 0).start()

    @pl.when(e + 1 < n_exp)
    def _():
        for cp in weight_copies(e + 1, (e + 1) % 2):
            cp.start(priority=1)

    b_lo = first_blk[e]
    b_hi = first_blk[e + 1]
    n_total = first_blk[n_exp]
    wslot = e % 2
    for cp in weight_copies(e, wslot):
        cp.wait()
    wg_bf[...] = wg_f32[wslot].astype(BF16)
    wu_bf[...] = wu_f32[wslot].astype(BF16)
    wd_bf[...] = wd_f32[wslot].astype(BF16)
    row_id = lax.broadcasted_iota(I32, (tb, 1), 0)

    def block(b, carry):
        slot = b % 2

        @pl.when(b >= 2)
        def _():
            writeback(b, slot).wait()

        fetch(b, slot).wait()
        fetch(jnp.minimum(b + 1, n_total - 1), 1 - slot).start()
        valid = cnt_ref[e] - (b - b_lo) * tb
        words = jnp.where(row_id < valid, _load_row_tiles(xbuf, (slot,)), jnp.uint32(0))
        hi, lo = _unpack_pairs(words)
        x = jnp.concatenate([hi.astype(BF16), lo.astype(BF16)], axis=1)
        hg = jnp.dot(x, wg_bf[...], preferred_element_type=F32)
        hu = jnp.dot(x, wu_bf[...], preferred_element_type=F32)
        hb = (_silu(hg) * hu).astype(BF16)
        _store_row_tiles(ybuf, (slot,), _pack_pairs(jnp.dot(hb, wd_bf[...], preferred_element_type=F32)))
        writeback(b, slot).start()
        return carry

    lax.fori_loop(b_lo, b_hi, block, 0)

    @pl.when(e == n_exp - 1)
    def _():
        fetch(0, n_total % 2).wait()

        @pl.when(n_total >= 2)
        def _():
            writeback(0, n_total % 2).wait()

        writeback(0, (n_total + 1) % 2).wait()
        ybuf[0] = jnp.zeros(ybuf.shape[1:], ybuf.dtype)

        def spare(blk):
            return pltpu.make_async_copy(ybuf.at[0], ysp_ref.at[pl.ds(pl.multiple_of(blk * tb, tb), tb)], ysems.at[0])

        def zero(blk, carry):
            spare(blk).start()
            return carry

        lax.fori_loop(n_total, n_pages, zero, 0)

        def zero_wait(blk, carry):
            spare(0).wait()
            return carry

        lax.fori_loop(n_total, n_pages, zero_wait, 0)


def _experts(counts, pages_flat, xs_sorted, w_gate, w_up, w_down, *, n_pages, table_lanes):
    tile = xs_sorted.shape[1:]
    n_exp, d, ff = w_gate.shape
    anyspec = pl.BlockSpec(memory_space=pl.ANY)
    grid_spec = pltpu.PrefetchScalarGridSpec(
        num_scalar_prefetch=2,
        grid=(n_exp,),
        in_specs=[anyspec, anyspec, anyspec, anyspec],
        out_specs=anyspec,
        scratch_shapes=[
            pltpu.VMEM((2, EXPERT_ROWS) + tile, U32), pltpu.VMEM((2, EXPERT_ROWS) + tile, U32),
            pltpu.VMEM((2, d, ff), F32), pltpu.VMEM((2, d, ff), F32), pltpu.VMEM((2, ff, d), F32),
            pltpu.VMEM((d, ff), BF16), pltpu.VMEM((d, ff), BF16), pltpu.VMEM((ff, d), BF16),
            pltpu.SMEM((n_exp + 1,), I32), pltpu.SMEM((n_pages,), I32),
            pltpu.SemaphoreType.DMA((2,)), pltpu.SemaphoreType.DMA((2,)), pltpu.SemaphoreType.DMA((2,)),
        ],
    )
    return pl.pallas_call(
        functools.partial(_experts_kernel, n_exp=n_exp, table_lanes=table_lanes),
        grid_spec=grid_spec,
        out_shape=jax.ShapeDtypeStruct((n_pages * EXPERT_ROWS,) + tile, U32),
        compiler_params=_params(("arbitrary",)),
        name="experts",
    )(counts, pages_flat, xs_sorted, w_gate, w_up, w_down)


def _combine_kernel(dest_ref, dest_next_ref, gates_ref, x1_ref, ysp_ref, gf_ref, yp_ref, ysmp_ref,
                    buf0, buf1, sems, *, n_first):
    i = pl.program_id(0)
    n = pl.num_programs(0)
    tm = x1_ref.shape[0]
    slot = i % 2

    def gather(d_ref, s):
        def body(j, carry):
            for r in range(DMA_UNROLL):
                t = j * DMA_UNROLL + r
                _row_copy(ysp_ref, d_ref[0, t], buf0.at[s], t, sems.at[s]).start()
                _row_copy(ysp_ref, d_ref[1, t], buf1.at[s], t, sems.at[s]).start()
            return carry

        lax.fori_loop(0, tm // DMA_UNROLL, body, 0)

    @pl.when(i == 0)
    def _():
        gather(dest_ref, 0)

    @pl.when(i + 1 < n)
    def _():
        gather(dest_next_ref, 1 - slot)

    pltpu.make_async_copy(ysp_ref.at[pl.ds(0, tm)], buf0.at[slot], sems.at[slot]).wait()
    pltpu.make_async_copy(ysp_ref.at[pl.ds(0, tm)], buf1.at[slot], sems.at[slot]).wait()
    hi0, lo0 = _unpack_pairs(_load_row_tiles(buf0, (slot,)))
    hi1, lo1 = _unpack_pairs(_load_row_tiles(buf1, (slot,)))
    g0 = gates_ref[:, 0:1]
    g1 = gates_ref[:, 1:2]
    moe = jnp.concatenate([g0 * hi0 + g1 * hi1, g0 * lo0 + g1 * lo1], axis=1)
    y = _rms(x1_ref[...] + moe, gf_ref[...])

    @pl.when(i < n_first)
    def _():
        yp_ref[...] = y

    @pl.when(i >= n_first)
    def _():
        ysmp_ref[...] = y


def _combine(dest, gates_t, x1, ysp, gf, *, n_p):
    n_all, d = x1.shape
    tile = ysp.shape[1:]
    n_first = n_p // ROW_TILE
    n_tiles = n_all // ROW_TILE
    first, second = _split_maps(n_first)
    return pl.pallas_call(
        functools.partial(_combine_kernel, n_first=n_first),
        grid=(n_tiles,),
        in_specs=[
            pl.BlockSpec((8, ROW_TILE), lambda i: (0, i), memory_space=pltpu.SMEM),
            pl.BlockSpec((8, ROW_TILE), lambda i: (0, jnp.minimum(i + 1, n_tiles - 1)), memory_space=pltpu.SMEM),
            pl.BlockSpec((ROW_TILE, 8), lambda i: (i, 0)),
            pl.BlockSpec((ROW_TILE, d), lambda i: (i, 0)),
            pl.BlockSpec(memory_space=pl.ANY),
            pl.BlockSpec((1, d), lambda i: (0, 0)),
        ],
        out_specs=[pl.BlockSpec((ROW_TILE, d), first), pl.BlockSpec((ROW_TILE, d), second)],
        out_shape=[jax.ShapeDtypeStruct((n_p, d), F32), jax.ShapeDtypeStruct((n_all - n_p, d), F32)],
        scratch_shapes=[pltpu.VMEM((2, ROW_TILE) + tile, U32), pltpu.VMEM((2, ROW_TILE) + tile, U32),
                        pltpu.SemaphoreType.DMA((2,))],
        compiler_params=_params(("arbitrary",)),
        name="combine",
    )(dest, dest, gates_t, x1, ysp, gf)


def _chunk_for(t):
    return 64 if t % 64 == 0 else t


def kernel(x_prompt, x_sample, cache_conv, state_gla, norm1_g, w_in, w_lr2, b_lr2, w_dw, b_dw, conv_ln_g, conv_ln_b, gla_norm_g, w_out, norm2_g, w_router_coarse, b_router_coarse, w_router_fine, b_router_fine, w_exp_gate, w_exp_up, w_exp_down, norm_f_g):
    assert norm1_g.shape[0] == 1, "single trunk layer"
    bp, tp, d = x_prompt.shape
    bs, ts, _ = x_sample.shape
    heads, dk, dv = state_gla.shape[2:]
    c_conv = w_dw.shape[2]
    width = w_dw.shape[1]
    rank = w_lr2.shape[1]
    qk, vv = heads * dk, heads * dv
    n_groups, _, per_group = w_router_fine.shape[1:]
    n_exp = n_groups * per_group
    n_p, n_s = bp * tp, bs * ts
    n_all = n_p + n_s
    assert n_p % ROW_TILE == 0 and n_s % ROW_TILE == 0 and width - 1 <= HIST_PAD

    xp = x_prompt.reshape(n_p, d)
    xs = x_sample.reshape(n_s, d)
    row = lambda a: a.reshape(1, -1)

    w_in_pad = jnp.pad(w_in[0], ((0, 0), (0, LANES - rank))).astype(BF16)
    w_lr2_pad = jnp.pad(w_lr2[0], ((0, LANES - rank), (0, 0)))
    u, q, k, v, g, la = _inproj(xp, xs, row(norm1_g[0]), w_in_pad, w_lr2_pad, row(b_lr2[0]),
                                c_conv=c_conv, qk=qk, vv=vv, dk=dk)

    hist_p = jnp.zeros((bp, HIST_PAD, c_conv), F32)
    hist_s = jnp.pad(cache_conv[0], ((0, 0), (HIST_PAD - (width - 1), 0), (0, 0)))
    conv_args = (w_dw[0], row(b_dw[0]), row(conv_ln_g[0]), row(conv_ln_b[0]))
    c_p = _conv(u, hist_p, *conv_args, row0=0, bsz=bp, t=tp, tt=_chunk_for(tp))
    c_s = _conv(u, hist_s, *conv_args, row0=n_p, bsz=bs, t=ts, tt=_chunk_for(ts))

    gn = row(gla_norm_g[0])
    s0_p = jnp.zeros((bp, heads, dk, dv), F32)
    o_p, gla_p = _gla(q, k, v, g, la, s0_p, gn, row0=0, bsz=bp, t=tp, chunk=_chunk_for(tp))
    o_s, gla_s = _gla(q, k, v, g, la, state_gla[0], gn, row0=n_p, bsz=bs, t=ts, chunk=_chunk_for(ts))

    wr = jnp.concatenate([w_router_coarse[0].T,
                          jnp.transpose(w_router_fine[0], (0, 2, 1)).reshape(n_exp, d)], axis=0)
    br = jnp.concatenate([b_router_coarse[0], b_router_fine[0].reshape(n_exp)])
    r_rows = -(-(n_groups + n_exp) // 8) * 8
    wr = jnp.pad(wr, ((0, r_rows - wr.shape[0]), (0, 0)))
    br = jnp.pad(br, (0, r_rows - br.shape[0])).reshape(r_rows, 1)
    n_pages = (2 * n_all) // EXPERT_ROWS + n_exp
    assert n_all // EXPERT_ROWS + 2 <= LANES, "page table row must hold one expert's pages"
    x1, dest, gates, pages, counts, xs_sorted = _outproj(
        xp, xs, c_p, c_s, o_p, o_s, w_out[0].astype(BF16), row(norm2_g[0]), wr, br,
        n_groups=n_groups, per_group=per_group, n_pages=n_pages)
    ysp = _experts(counts[:, 0], pages.reshape(-1), xs_sorted, w_exp_gate[0], w_exp_up[0], w_exp_down[0],
                   n_pages=n_pages, table_lanes=LANES)
    y_p, y_s = _combine(dest, gates.T, x1, ysp, row(norm_f_g), n_p=n_p)

    keep = width - 1
    u_p = u[:n_p].reshape(bp, tp, c_conv)[:, tp - keep:]
    u_s = u[n_p:].reshape(bs, ts, c_conv)
    conv_sample = jnp.concatenate([cache_conv[0], u_s], axis=1)[:, -keep:]
    return (y_p.reshape(bp, tp, d), y_s.reshape(bs, ts, d), u_p[None], gla_p[None],
            conv_sample[None], gla_s[None])
```

```python
import functools

import jax
import jax.numpy as jnp
from jax import lax
from jax.experimental import pallas as pl
from jax.experimental.pallas import tpu as pltpu

F32 = jnp.float32
BF16 = jnp.bfloat16
I32 = jnp.int32
U32 = jnp.uint32
EPS = 1e-6
GATE_TEMP = 16.0
HIGHEST = lax.Precision.HIGHEST

LANES = 128
ROW_TILE = 256
EXPERT_ROWS = 256
HIST_PAD = 32
DMA_UNROLL = 8
VMEM_LIMIT = 56 * 1024 * 1024


def _params(semantics, vmem=VMEM_LIMIT):
    return pltpu.CompilerParams(dimension_semantics=semantics, vmem_limit_bytes=vmem)


def _rms(x, g):
    return x * lax.rsqrt(jnp.mean(x * x, axis=-1, keepdims=True) + EPS) * g


def _silu(x):
    return x * jax.nn.sigmoid(x)


def _log_sigmoid(z):
    return jnp.minimum(z, 0.0) - jnp.log(1.0 + jnp.exp(-jnp.abs(z)))


def _pick(i, n_first, first_ref, second_ref):
    return jnp.where(i < n_first, first_ref[...], second_ref[...])


def _split_maps(n_first):
    first = lambda i: (jnp.minimum(i, n_first - 1), 0)
    second = lambda i: (jnp.maximum(i - n_first, 0), 0)
    return first, second


def _pack_pairs(x):
    half = x.shape[1] // 2
    hi = lax.bitcast_convert_type(x[:, :half].astype(BF16).astype(F32), U32)
    lo = lax.bitcast_convert_type(x[:, half:].astype(BF16).astype(F32), U32)
    return hi | (lo >> 16)


def _unpack_pairs(p):
    hi = lax.bitcast_convert_type(p & jnp.uint32(0xFFFF0000), F32)
    lo = lax.bitcast_convert_type(p << 16, F32)
    return hi, lo


def _row_copy(src, s, dst, d, sem):
    return pltpu.make_async_copy(src.at[pl.ds(s, 1)], dst.at[pl.ds(d, 1)], sem)


def _inproj_kernel(xp_ref, xs_ref, g1_ref, w_ref, wlr2_ref, blr2_ref,
                   u_ref, q_ref, k_ref, v_ref, g_ref, la_ref, *, n_first, c_conv, qk, vv, dk):
    i = pl.program_id(0)
    x = _pick(i, n_first, xp_ref, xs_ref)
    h = _rms(x, g1_ref[...]).astype(BF16)

    def mm(lo, width):
        return jnp.dot(h, w_ref[:, lo:lo + width], preferred_element_type=F32)

    a = mm(0, c_conv)
    a_gate = mm(c_conv, c_conv)
    u_ref[...] = a * jax.nn.sigmoid(a_gate)
    off = 2 * c_conv
    q_ref[...] = mm(off, qk) * (dk ** -0.5)
    k_ref[...] = mm(off + qk, qk)
    v_ref[...] = mm(off + 2 * qk, vv)
    g_ref[...] = mm(off + 2 * qk + vv, vv)
    lr = mm(off + 2 * qk + 2 * vv, LANES)
    z = jnp.dot(lr, wlr2_ref[...], precision=HIGHEST, preferred_element_type=F32) + blr2_ref[...]
    la_ref[...] = _log_sigmoid(z) * (1.0 / GATE_TEMP)


def _inproj(xp, xs, g1, w_pad, wlr2_pad, blr2, *, c_conv, qk, vv, dk):
    n_p, d = xp.shape
    n_s = xs.shape[0]
    n_all = n_p + n_s
    n_first = n_p // ROW_TILE
    grid = (n_all // ROW_TILE,)
    first, second = _split_maps(n_first)
    const = lambda i: (0, 0)
    row = lambda i: (i, 0)
    widths = (c_conv, qk, qk, vv, vv, qk)
    return pl.pallas_call(
        functools.partial(_inproj_kernel, n_first=n_first, c_conv=c_conv, qk=qk, vv=vv, dk=dk),
        grid=grid,
        in_specs=[
            pl.BlockSpec((ROW_TILE, d), first),
            pl.BlockSpec((ROW_TILE, d), second),
            pl.BlockSpec((1, d), const),
            pl.BlockSpec(w_pad.shape, const, pipeline_mode=pl.Buffered(1)),
            pl.BlockSpec(wlr2_pad.shape, const),
            pl.BlockSpec((1, qk), const),
        ],
        out_specs=[pl.BlockSpec((ROW_TILE, w), row) for w in widths],
        out_shape=[jax.ShapeDtypeStruct((n_all, w), F32) for w in widths],
        compiler_params=_params(("arbitrary",)),
        name="inproj",
    )(xp, xs, g1, w_pad, wlr2_pad, blr2)


def _conv_kernel(u_ref, hist_ref, w_ref, b_ref, lg_ref, lb_ref, c_ref, win, cbuf, *, tt, width):
    i = pl.program_id(1)

    @pl.when(i == 0)
    def _():
        win[0:HIST_PAD, :] = hist_ref[0]

    @pl.when(i > 0)
    def _():
        win[0:HIST_PAD, :] = win[tt:tt + HIST_PAD, :]

    win[HIST_PAD:HIST_PAD + tt, :] = u_ref[...]
    lead = HIST_PAD - (width - 1)
    n_ch = u_ref.shape[1]
    for cb in range(n_ch // LANES):
        cs = slice(cb * LANES, (cb + 1) * LANES)
        acc = jnp.broadcast_to(b_ref[:, cs], (tt, LANES))
        for j in range(width):
            acc = acc + w_ref[j:j + 1, cs] * win[lead + j:lead + j + tt, cs]
        cbuf[:, cs] = acc
    c = cbuf[...]
    mu = jnp.mean(c, axis=-1, keepdims=True)
    xc = c - mu
    y = xc * lax.rsqrt(jnp.mean(xc * xc, axis=-1, keepdims=True) + EPS) * lg_ref[...] + lb_ref[...]
    c_ref[...] = _silu(y).astype(c_ref.dtype)


def _conv(u_all, hist_pad, w_dw, b_dw, ln_g, ln_b, *, row0, bsz, t, tt):
    n_ch = u_all.shape[1]
    width = w_dw.shape[0]
    nt = t // tt
    blk0 = row0 // tt
    const = lambda b, i: (0, 0)
    return pl.pallas_call(
        functools.partial(_conv_kernel, tt=tt, width=width),
        grid=(bsz, nt),
        in_specs=[
            pl.BlockSpec((tt, n_ch), lambda b, i: (blk0 + b * nt + i, 0)),
            pl.BlockSpec((1, HIST_PAD, n_ch), lambda b, i: (b, 0, 0)),
            pl.BlockSpec(w_dw.shape, const),
            pl.BlockSpec((1, n_ch), const),
            pl.BlockSpec((1, n_ch), const),
            pl.BlockSpec((1, n_ch), const),
        ],
        out_specs=pl.BlockSpec((tt, n_ch), lambda b, i: (b * nt + i, 0)),
        out_shape=jax.ShapeDtypeStruct((bsz * t, n_ch), BF16),
        scratch_shapes=[pltpu.VMEM((tt + HIST_PAD, n_ch), F32), pltpu.VMEM((tt, n_ch), F32)],
        compiler_params=_params(("arbitrary", "arbitrary")),
        name="conv",
    )(u_all, hist_pad, w_dw, b_dw, ln_g, ln_b)


def _gla_kernel(q_ref, k_ref, v_ref, g_ref, la_ref, s0_ref, gn_ref, o_ref, sout_ref, state,
                *, chunk, heads, dk, dv):
    i = pl.program_id(1)

    @pl.when(i == 0)
    def _():
        state[...] = s0_ref[0]

    la = la_ref[...]
    r = lax.broadcasted_iota(I32, (chunk, chunk), 0)
    c = lax.broadcasted_iota(I32, (chunk, chunk), 1)
    causal = c <= r
    b = jnp.dot(causal.astype(F32), la, precision=HIGHEST, preferred_element_type=F32)
    b_end = b[chunk - 1:chunk, :]
    q_in = (q_ref[...] * jnp.exp(b)).astype(BF16)
    k_in = (k_ref[...] * jnp.exp(-b)).astype(BF16)
    k_out = (k_ref[...] * jnp.exp(b_end - b)).astype(BF16)
    decay_row = jnp.exp(b_end)
    eye = lax.broadcasted_iota(I32, (dk, dk), 0) == lax.broadcasted_iota(I32, (dk, dk), 1)
    for h in range(heads):
        ks = slice(h * dk, (h + 1) * dk)
        vs = slice(h * dv, (h + 1) * dv)
        vh = v_ref[:, vs].astype(BF16)
        att = lax.dot_general(q_in[:, ks], k_in[:, ks], (((1,), (1,)), ((), ())), preferred_element_type=F32)
        att = jnp.where(causal, att, 0.0).astype(BF16)
        s_h = state[h]
        o = jnp.dot(att, vh, preferred_element_type=F32)
        o = o + jnp.dot(q_in[:, ks], s_h.astype(BF16), preferred_element_type=F32)
        decay_col = jnp.sum(jnp.where(eye, jnp.broadcast_to(decay_row[:, ks], (dk, dk)), 0.0), axis=1, keepdims=True)
        state[h] = decay_col * s_h + lax.dot_general(k_out[:, ks], vh, (((0,), (0,)), ((), ())),
                                                     preferred_element_type=F32)
        o = o * lax.rsqrt(jnp.mean(o * o, axis=-1, keepdims=True) + EPS) * gn_ref[...]
        o_ref[:, vs] = (o * _silu(g_ref[:, vs])).astype(o_ref.dtype)

    @pl.when(i == pl.num_programs(1) - 1)
    def _():
        sout_ref[0] = state[...]


def _gla(q_all, k_all, v_all, g_all, la_all, s0, gn, *, row0, bsz, t, chunk):
    heads, dk, dv = s0.shape[1:]
    nt = t // chunk
    blk0 = row0 // chunk
    rows = lambda b, i: (blk0 + b * nt + i, 0)
    return pl.pallas_call(
        functools.partial(_gla_kernel, chunk=chunk, heads=heads, dk=dk, dv=dv),
        grid=(bsz, nt),
        in_specs=[
            pl.BlockSpec((chunk, heads * dk), rows),
            pl.BlockSpec((chunk, heads * dk), rows),
            pl.BlockSpec((chunk, heads * dv), rows),
            pl.BlockSpec((chunk, heads * dv), rows),
            pl.BlockSpec((chunk, heads * dk), rows),
            pl.BlockSpec((1, heads, dk, dv), lambda b, i: (b, 0, 0, 0)),
            pl.BlockSpec((1, dv), lambda b, i: (0, 0)),
        ],
        out_specs=[
            pl.BlockSpec((chunk, heads * dv), lambda b, i: (b * nt + i, 0)),
            pl.BlockSpec((1, heads, dk, dv), lambda b, i: (b, 0, 0, 0)),
        ],
        out_shape=[
            jax.ShapeDtypeStruct((bsz * t, heads * dv), BF16),
            jax.ShapeDtypeStruct((bsz, heads, dk, dv), F32),
        ],
        scratch_shapes=[pltpu.VMEM((heads, dk, dv), F32)],
        compiler_params=_params(("arbitrary", "arbitrary")),
        name="gla",
    )(q_all, k_all, v_all, g_all, la_all, s0, gn)


def _outproj_kernel(xp_ref, xs_ref, cp_ref, cs_ref, op_ref, os_ref, w_ref, g2_ref, wr_ref, br_ref,
                    x1_ref, dest_ref, gates_ref, pages_ref, counts_ref, sorted_ref,
                    stage, dest_vmem, dest_smem, cnt_s, page_s, npage_s, table_s, scatter_sems, dest_sems,
                    *, n_first, c_conv, n_groups, per_group, trash_row):
    i = pl.program_id(0)
    n_steps = pl.num_programs(0)
    tm = x1_ref.shape[0]
    n_exp = n_groups * per_group
    slot = i % 2
    prev = 1 - slot

    def dest_copy(s):
        return pltpu.make_async_copy(dest_vmem.at[s], dest_smem.at[s], dest_sems.at[s])

    def scatter_wait(s):
        for _ in range(2):
            pltpu.make_async_copy(stage.at[s], sorted_ref.at[pl.ds(0, tm)], scatter_sems.at[s]).wait()

    @pl.when(i == 0)
    def _():
        cnt_s[...] = jnp.zeros_like(cnt_s)
        page_s[...] = jnp.zeros_like(page_s)
        npage_s[...] = jnp.zeros_like(npage_s)
        table_s[...] = jnp.zeros_like(table_s)
        stage[1] = jnp.zeros(stage.shape[1:], stage.dtype)

        def fill(t, carry):
            dest_smem[1, 0, t] = trash_row + t
            dest_smem[1, 1, t] = trash_row + tm + t
            return carry

        lax.fori_loop(0, tm, fill, 0)

    @pl.when(i >= 1)
    def _():
        dest_copy(prev).wait()
        scatter_wait(slot)

    for t in range(tm):
        _row_copy(stage.at[prev], t, sorted_ref, dest_smem[prev, 0, t], scatter_sems.at[prev]).start()
        _row_copy(stage.at[prev], t, sorted_ref, dest_smem[prev, 1, t], scatter_sems.at[prev]).start()

    x = _pick(i, n_first, xp_ref, xs_ref)
    cc = _pick(i, n_first, cp_ref, cs_ref)
    oo = _pick(i, n_first, op_ref, os_ref)
    mix = jnp.dot(cc, w_ref[0:c_conv, :], preferred_element_type=F32)
    mix = mix + jnp.dot(oo, w_ref[c_conv:, :], preferred_element_type=F32)
    x1 = x + mix
    x1_ref[...] = x1
    h2 = _rms(x1, g2_ref[...])
    stage[slot] = _pack_pairs(h2)
    h_hi = h2.astype(BF16)
    h_lo = (h2 - h_hi.astype(F32)).astype(BF16)
    parts = lax.dot_general(wr_ref[...], jnp.concatenate([h_hi, h_lo], axis=0), (((1,), (1,)), ((), ())),
                            preferred_element_type=F32)
    n_r = br_ref.shape[0]
    logits = (parts[0:n_r, 0:tm] + parts[0:n_r, tm:] + parts[n_r:, 0:tm] + parts[n_r:, tm:]) + br_ref[...]
    lc = logits[0:n_groups, :]
    mc = jnp.max(lc, axis=0, keepdims=True)
    p_group = 1.0 / jnp.sum(jnp.exp(lc - mc), axis=0, keepdims=True)
    rows_c = lax.broadcasted_iota(I32, (n_groups, tm), 0)
    g_idx = jnp.min(jnp.where(lc == mc, rows_c, n_groups), axis=0, keepdims=True)
    lf = logits[n_groups:n_groups + n_exp, :]
    rows_f = lax.broadcasted_iota(I32, (n_exp, tm), 0)
    in_group = (rows_f >= g_idx * per_group) & (rows_f < (g_idx + 1) * per_group)
    neg = jnp.float32(-jnp.inf)
    l1 = jnp.where(in_group, lf, neg)
    m1 = jnp.max(l1, axis=0, keepdims=True)
    e1 = jnp.min(jnp.where(l1 == m1, rows_f, n_exp), axis=0, keepdims=True)
    l2 = jnp.where(rows_f == e1, neg, l1)
    m2 = jnp.max(l2, axis=0, keepdims=True)
    e2 = jnp.min(jnp.where(l2 == m2, rows_f, n_exp), axis=0, keepdims=True)
    r2 = jnp.exp(m2 - m1)
    w1 = 1.0 / (1.0 + r2)
    row8 = lax.broadcasted_iota(I32, (8, tm), 0)
    gates_ref[...] = jnp.where(row8 == 0, p_group * w1, jnp.where(row8 == 1, p_group * (r2 * w1), 0.0))

    oh0 = (rows_f == e1).astype(F32)
    oh1 = (rows_f == e2).astype(F32)
    both = oh0 + oh1
    tr = lax.broadcasted_iota(I32, (tm, tm), 0)
    tc = lax.broadcasted_iota(I32, (tm, tm), 1)
    earlier = jnp.dot(both.astype(BF16), (tr < tc).astype(BF16), preferred_element_type=F32)
    cnt = cnt_s[...]
    rank_base = earlier + cnt
    tile_cnt = jnp.sum(both, axis=1, keepdims=True)
    page_rows = float(EXPERT_ROWS)
    k0 = jnp.floor(cnt * (1.0 / page_rows))
    new_cnt = cnt + tile_cnt
    limit = (k0 + 1.0) * page_rows
    need_a = ((cnt == k0 * page_rows) & (tile_cnt > 0.0)).astype(F32)
    need_b = (new_cnt > limit).astype(F32)
    need = need_a + need_b
    er = lax.broadcasted_iota(I32, (n_exp, n_exp), 0)
    ec = lax.broadcasted_iota(I32, (n_exp, n_exp), 1)
    before = jnp.dot((ec < er).astype(BF16), jnp.broadcast_to(need, (n_exp, LANES)).astype(BF16),
                     preferred_element_type=F32)[:, 0:1]
    base = npage_s[...] + before
    page_a = jnp.where(need_a > 0.0, base, page_s[...])
    page_b = base + need_a
    npage_s[...] = npage_s[...] + jnp.sum(need, axis=0, keepdims=True)
    lane = lax.broadcasted_iota(I32, table_s.shape, 1).astype(F32)
    table = jnp.where((lane == k0) & (need_a > 0.0), page_a, table_s[...])
    table_s[...] = jnp.where((lane == k0 + 1.0) & (need_b > 0.0), page_b, table)
    cnt_s[...] = new_cnt
    page_s[...] = jnp.where(jnp.floor(new_cnt * (1.0 / page_rows)) == k0, page_a, page_b)

    def dest_rows(oh):
        rank = jnp.sum(oh * rank_base, axis=0, keepdims=True)
        lim = jnp.sum(oh * limit, axis=0, keepdims=True)
        pa = jnp.sum(oh * page_a, axis=0, keepdims=True)
        pb = jnp.sum(oh * page_b, axis=0, keepdims=True)
        within = rank - jnp.floor(rank * (1.0 / page_rows)) * page_rows
        return jnp.where(rank < lim, pa, pb) * page_rows + within

    dest = jnp.where(row8 == 0, dest_rows(oh0), jnp.where(row8 == 1, dest_rows(oh1), 0.0)).astype(I32)
    dest_ref[...] = dest
    dest_vmem[slot] = dest
    dest_copy(slot).start()

    @pl.when(i == n_steps - 1)
    def _():
        pages_ref[...] = table_s[...].astype(I32)
        counts_ref[...] = jnp.broadcast_to(cnt_s[...], counts_ref.shape).astype(I32)
        dest_copy(slot).wait()
        scatter_wait(prev)

        def last(j, carry):
            for r in range(DMA_UNROLL):
                t = j * DMA_UNROLL + r
                _row_copy(stage.at[slot], t, sorted_ref, dest_smem[slot, 0, t], scatter_sems.at[slot]).start()
                _row_copy(stage.at[slot], t, sorted_ref, dest_smem[slot, 1, t], scatter_sems.at[slot]).start()
            return carry

        lax.fori_loop(0, tm // DMA_UNROLL, last, 0)
        scatter_wait(slot)


def _outproj(xp, xs, cp, cs, op, os_, w_out, g2, wr, br, *, n_groups, per_group, n_pages):
    n_p, d = xp.shape
    n_all = n_p + xs.shape[0]
    n_first = n_p // ROW_TILE
    n_exp = n_groups * per_group
    c_conv = cp.shape[1]
    vv = op.shape[1]
    tile = (d // 2,)
    first, second = _split_maps(n_first)
    const = lambda i: (0, 0)
    row = lambda i: (i, 0)
    col = lambda i: (0, i)
    rows_sorted = n_pages * EXPERT_ROWS + 2 * ROW_TILE
    assert ROW_TILE <= EXPERT_ROWS, "a tile may open at most two pages per expert"
    return pl.pallas_call(
        functools.partial(_outproj_kernel, n_first=n_first, c_conv=c_conv, n_groups=n_groups, per_group=per_group,
                          trash_row=n_pages * EXPERT_ROWS),
        grid=(n_all // ROW_TILE,),
        in_specs=[
            pl.BlockSpec((ROW_TILE, d), first), pl.BlockSpec((ROW_TILE, d), second),
            pl.BlockSpec((ROW_TILE, c_conv), first), pl.BlockSpec((ROW_TILE, c_conv), second),
            pl.BlockSpec((ROW_TILE, vv), first), pl.BlockSpec((ROW_TILE, vv), second),
            pl.BlockSpec(w_out.shape, const, pipeline_mode=pl.Buffered(1)),
            pl.BlockSpec((1, d), const),
            pl.BlockSpec(wr.shape, const),
            pl.BlockSpec(br.shape, const),
        ],
        out_specs=[
            pl.BlockSpec((ROW_TILE, d), row),
            pl.BlockSpec((8, ROW_TILE), col), pl.BlockSpec((8, ROW_TILE), col),
            pl.BlockSpec((n_exp, LANES), const), pl.BlockSpec((n_exp, LANES), const),
            pl.BlockSpec(memory_space=pl.ANY),
        ],
        out_shape=[
            jax.ShapeDtypeStruct((n_all, d), F32),
            jax.ShapeDtypeStruct((8, n_all), I32), jax.ShapeDtypeStruct((8, n_all), F32),
            jax.ShapeDtypeStruct((n_exp, LANES), I32), jax.ShapeDtypeStruct((n_exp, LANES), I32),
            jax.ShapeDtypeStruct((rows_sorted,) + tile, U32),
        ],
        scratch_shapes=[
            pltpu.VMEM((2, ROW_TILE) + tile, U32),
            pltpu.VMEM((2, 8, ROW_TILE), I32), pltpu.SMEM((2, 8, ROW_TILE), I32),
            pltpu.VMEM((n_exp, 1), F32), pltpu.VMEM((n_exp, 1), F32), pltpu.VMEM((1, 1), F32),
            pltpu.VMEM((n_exp, LANES), F32),
            pltpu.SemaphoreType.DMA((2,)), pltpu.SemaphoreType.DMA((2,)),
        ],
        compiler_params=_params(("arbitrary",)),
        name="outproj",
    )(xp, xs, cp, cs, op, os_, w_out, g2, wr, br)


def _experts_kernel(cnt_ref, pages_ref, xs_ref, wg_ref, wu_ref, wd_ref, ysp_ref,
                    xbuf, ybuf, wg_f32, wu_f32, wd_f32, wg_bf, wu_bf, wd_bf, first_blk, page_seq,
                    gsems, ysems, wsems, *, n_exp, table_lanes):
    e = pl.program_id(0)
    tb = xbuf.shape[1]
    n_pages = page_seq.shape[0]

    def n_pages_of(ex):
        return (cnt_ref[ex] + (tb - 1)) // tb

    def page_rows(blk):
        return pl.ds(pl.multiple_of(page_seq[blk] * tb, tb), tb)

    def fetch(blk, slot):
        return pltpu.make_async_copy(xs_ref.at[page_rows(blk)], xbuf.at[slot], gsems.at[slot])

    def writeback(blk, slot):
        return pltpu.make_async_copy(ybuf.at[slot], ysp_ref.at[page_rows(blk)], ysems.at[slot])

    def weight_copies(ex, slot):
        return (pltpu.make_async_copy(wg_ref.at[ex], wg_f32.at[slot], wsems.at[slot]),
                pltpu.make_async_copy(wu_ref.at[ex], wu_f32.at[slot], wsems.at[slot]),
                pltpu.make_async_copy(wd_ref.at[ex], wd_f32.at[slot], wsems.at[slot]))

    @pl.when(e == 0)
    def _():
        for cp in weight_copies(0, 0):
            cp.start(priority=1)

        def per_expert(ex, blk):
            first_blk[ex] = blk

            def per_page(j, carry):
                page_seq[blk + j] = pages_ref[ex * table_lanes + j]
                return carry

            lax.fori_loop(0, n_pages_of(ex), per_page, 0)
            return blk + n_pages_of(ex)

        first_blk[n_exp] = lax.fori_loop(0, n_exp, per_expert, 0)
        fetch(0, 0).start()

    @pl.when(e + 1 < n_exp)
    def _():
        for cp in weight_copies(e + 1, (e + 1) % 2):
            cp.start(priority=1)

    b_lo = first_blk[e]
    b_hi = first_blk[e + 1]
    n_total = first_blk[n_exp]
    wslot = e % 2
    for cp in weight_copies(e, wslot):
        cp.wait()
    wg_bf[...] = wg_f32[wslot].astype(BF16)
    wu_bf[...] = wu_f32[wslot].astype(BF16)
    wd_bf[...] = wd_f32[wslot].astype(BF16)
    row_id = lax.broadcasted_iota(I32, (tb, 1), 0)

    def block(b, carry):
        slot = b % 2

        @pl.when(b >= 2)
        def _():
            writeback(b, slot).wait()

        fetch(b, slot).wait()
        fetch(jnp.minimum(b + 1, n_total - 1), 1 - slot).start()
        valid = cnt_ref[e] - (b - b_lo) * tb
        words = jnp.where(row_id < valid, xbuf[slot], jnp.uint32(0))
        hi, lo = _unpack_pairs(words)
        x = jnp.concatenate([hi.astype(BF16), lo.astype(BF16)], axis=1)
        hg = jnp.dot(x, wg_bf[...], preferred_element_type=F32)
        hu = jnp.dot(x, wu_bf[...], preferred_element_type=F32)
        hb = (_silu(hg) * hu).astype(BF16)
        ybuf[slot] = _pack_pairs(jnp.dot(hb, wd_bf[...], preferred_element_type=F32))
        writeback(b, slot).start()
        return carry

    lax.fori_loop(b_lo, b_hi, block, 0)

    @pl.when(e == n_exp - 1)
    def _():
        fetch(0, n_total % 2).wait()

        @pl.when(n_total >= 2)
        def _():
            writeback(0, n_total % 2).wait()

        writeback(0, (n_total + 1) % 2).wait()
        ybuf[0] = jnp.zeros(ybuf.shape[1:], ybuf.dtype)

        def spare(blk):
            return pltpu.make_async_copy(ybuf.at[0], ysp_ref.at[pl.ds(pl.multiple_of(blk * tb, tb), tb)], ysems.at[0])

        def zero(blk, carry):
            spare(blk).start()
            return carry

        lax.fori_loop(n_total, n_pages, zero, 0)

        def zero_wait(blk, carry):
            spare(0).wait()
            return carry

        lax.fori_loop(n_total, n_pages, zero_wait, 0)


def _experts(counts, pages_flat, xs_sorted, w_gate, w_up, w_down, *, n_pages, table_lanes):
    tile = xs_sorted.shape[1:]
    n_exp, d, ff = w_gate.shape
    anyspec = pl.BlockSpec(memory_space=pl.ANY)
    grid_spec = pltpu.PrefetchScalarGridSpec(
        num_scalar_prefetch=2,
        grid=(n_exp,),
        in_specs=[anyspec, anyspec, anyspec, anyspec],
        out_specs=anyspec,
        scratch_shapes=[
            pltpu.VMEM((2, EXPERT_ROWS) + tile, U32), pltpu.VMEM((2, EXPERT_ROWS) + tile, U32),
            pltpu.VMEM((2, d, ff), F32), pltpu.VMEM((2, d, ff), F32), pltpu.VMEM((2, ff, d), F32),
            pltpu.VMEM((d, ff), BF16), pltpu.VMEM((d, ff), BF16), pltpu.VMEM((ff, d), BF16),
            pltpu.SMEM((n_exp + 1,), I32), pltpu.SMEM((n_pages,), I32),
            pltpu.SemaphoreType.DMA((2,)), pltpu.SemaphoreType.DMA((2,)), pltpu.SemaphoreType.DMA((2,)),
        ],
    )
    return pl.pallas_call(
        functools.partial(_experts_kernel, n_exp=n_exp, table_lanes=table_lanes),
        grid_spec=grid_spec,
        out_shape=jax.ShapeDtypeStruct((n_pages * EXPERT_ROWS,) + tile, U32),
        compiler_params=_params(("arbitrary",)),
        name="experts",
    )(counts, pages_flat, xs_sorted, w_gate, w_up, w_down)


def _combine_kernel(dest_ref, dest_next_ref, gates_ref, x1_ref, ysp_ref, gf_ref, yp_ref, ysmp_ref,
                    buf0, buf1, sems, *, n_first):
    i = pl.program_id(0)
    n = pl.num_programs(0)
    tm = x1_ref.shape[0]
    slot = i % 2

    def gather(d_ref, s):
        def body(j, carry):
            for r in range(DMA_UNROLL):
                t = j * DMA_UNROLL + r
                _row_copy(ysp_ref, d_ref[0, t], buf0.at[s], t, sems.at[s]).start()
                _row_copy(ysp_ref, d_ref[1, t], buf1.at[s], t, sems.at[s]).start()
            return carry

        lax.fori_loop(0, tm // DMA_UNROLL, body, 0)

    @pl.when(i == 0)
    def _():
        gather(dest_ref, 0)

    @pl.when(i + 1 < n)
    def _():
        gather(dest_next_ref, 1 - slot)

    pltpu.make_async_copy(ysp_ref.at[pl.ds(0, tm)], buf0.at[slot], sems.at[slot]).wait()
    pltpu.make_async_copy(ysp_ref.at[pl.ds(0, tm)], buf1.at[slot], sems.at[slot]).wait()
    hi0, lo0 = _unpack_pairs(buf0[slot])
    hi1, lo1 = _unpack_pairs(buf1[slot])
    g0 = gates_ref[:, 0:1]
    g1 = gates_ref[:, 1:2]
    moe = jnp.concatenate([g0 * hi0 + g1 * hi1, g0 * lo0 + g1 * lo1], axis=1)
    y = _rms(x1_ref[...] + moe, gf_ref[...])

    @pl.when(i < n_first)
    def _():
        yp_ref[...] = y

    @pl.when(i >= n_first)
    def _():
        ysmp_ref[...] = y


def _combine(dest, gates_t, x1, ysp, gf, *, n_p):
    n_all, d = x1.shape
    tile = ysp.shape[1:]
    n_first = n_p // ROW_TILE
    n_tiles = n_all // ROW_TILE
    first, second = _split_maps(n_first)
    return pl.pallas_call(
        functools.partial(_combine_kernel, n_first=n_first),
        grid=(n_tiles,),
        in_specs=[
            pl.BlockSpec((8, ROW_TILE), lambda i: (0, i), memory_space=pltpu.SMEM),
            pl.BlockSpec((8, ROW_TILE), lambda i: (0, jnp.minimum(i + 1, n_tiles - 1)), memory_space=pltpu.SMEM),
            pl.BlockSpec((ROW_TILE, 8), lambda i: (i, 0)),
            pl.BlockSpec((ROW_TILE, d), lambda i: (i, 0)),
            pl.BlockSpec(memory_space=pl.ANY),
            pl.BlockSpec((1, d), lambda i: (0, 0)),
        ],
        out_specs=[pl.BlockSpec((ROW_TILE, d), first), pl.BlockSpec((ROW_TILE, d), second)],
        out_shape=[jax.ShapeDtypeStruct((n_p, d), F32), jax.ShapeDtypeStruct((n_all - n_p, d), F32)],
        scratch_shapes=[pltpu.VMEM((2, ROW_TILE) + tile, U32), pltpu.VMEM((2, ROW_TILE) + tile, U32),
                        pltpu.SemaphoreType.DMA((2,))],
        compiler_params=_params(("arbitrary",)),
        name="combine",
    )(dest, dest, gates_t, x1, ysp, gf)


def _chunk_for(t):
    return 64 if t % 64 == 0 else t


def kernel(x_prompt, x_sample, cache_conv, state_gla, norm1_g, w_in, w_lr2, b_lr2, w_dw, b_dw, conv_ln_g, conv_ln_b, gla_norm_g, w_out, norm2_g, w_router_coarse, b_router_coarse, w_router_fine, b_router_fine, w_exp_gate, w_exp_up, w_exp_down, norm_f_g):
    assert norm1_g.shape[0] == 1, "single trunk layer"
    bp, tp, d = x_prompt.shape
    bs, ts, _ = x_sample.shape
    heads, dk, dv = state_gla.shape[2:]
    c_conv = w_dw.shape[2]
    width = w_dw.shape[1]
    rank = w_lr2.shape[1]
    qk, vv = heads * dk, heads * dv
    n_groups, _, per_group = w_router_fine.shape[1:]
    n_exp = n_groups * per_group
    n_p, n_s = bp * tp, bs * ts
    n_s_pad = -(-n_s // ROW_TILE) * ROW_TILE
    pad_rows = lambda a: jnp.pad(a, ((0, n_s_pad - n_s), (0, 0)))
    n_all = n_p + n_s_pad
    assert n_p % ROW_TILE == 0 and width - 1 <= HIST_PAD

    xp = x_prompt.reshape(n_p, d)
    xs = pad_rows(x_sample.reshape(n_s, d))
    row = lambda a: a.reshape(1, -1)

    w_in_pad = jnp.pad(w_in[0], ((0, 0), (0, LANES - rank))).astype(BF16)
    w_lr2_pad = jnp.pad(w_lr2[0], ((0, LANES - rank), (0, 0)))
    u, q, k, v, g, la = _inproj(xp, xs, row(norm1_g[0]), w_in_pad, w_lr2_pad, row(b_lr2[0]),
                                c_conv=c_conv, qk=qk, vv=vv, dk=dk)

    hist_p = jnp.zeros((bp, HIST_PAD, c_conv), F32)
    hist_s = jnp.pad(cache_conv[0], ((0, 0), (HIST_PAD - (width - 1), 0), (0, 0)))
    conv_args = (w_dw[0], row(b_dw[0]), row(conv_ln_g[0]), row(conv_ln_b[0]))
    c_p = _conv(u, hist_p, *conv_args, row0=0, bsz=bp, t=tp, tt=_chunk_for(tp))
    c_s = pad_rows(_conv(u, hist_s, *conv_args, row0=n_p, bsz=bs, t=ts, tt=_chunk_for(ts)))

    gn = row(gla_norm_g[0])
    s0_p = jnp.zeros((bp, heads, dk, dv), F32)
    o_p, gla_p = _gla(q, k, v, g, la, s0_p, gn, row0=0, bsz=bp, t=tp, chunk=_chunk_for(tp))
    o_s, gla_s = _gla(q, k, v, g, la, state_gla[0], gn, row0=n_p, bsz=bs, t=ts, chunk=_chunk_for(ts))
    o_s = pad_rows(o_s)

    wr = jnp.concatenate([w_router_coarse[0].T,
                          jnp.transpose(w_router_fine[0], (0, 2, 1)).reshape(n_exp, d)], axis=0)
    br = jnp.concatenate([b_router_coarse[0], b_router_fine[0].reshape(n_exp)])
    r_rows = -(-(n_groups + n_exp) // 8) * 8
    wr = jnp.pad(wr, ((0, r_rows - wr.shape[0]), (0, 0)))
    br = jnp.pad(br, (0, r_rows - br.shape[0])).reshape(r_rows, 1)
    wr_hi = wr.astype(BF16)
    wr = jnp.concatenate([wr_hi, (wr - wr_hi.astype(F32)).astype(BF16)], axis=0)
    n_pages = (2 * n_all) // EXPERT_ROWS + n_exp
    assert n_all // EXPERT_ROWS + 2 <= LANES, "page table row must hold one expert's pages"
    x1, dest, gates, pages, counts, xs_sorted = _outproj(
        xp, xs, c_p, c_s, o_p, o_s, w_out[0].astype(BF16), row(norm2_g[0]), wr, br,
        n_groups=n_groups, per_group=per_group, n_pages=n_pages)
    ysp = _experts(counts[:, 0], pages.reshape(-1), xs_sorted, w_exp_gate[0], w_exp_up[0], w_exp_down[0],
                   n_pages=n_pages, table_lanes=LANES)
    y_p, y_s = _combine(dest, gates.T, x1, ysp, row(norm_f_g), n_p=n_p)

    keep = width - 1
    u_p = u[:n_p].reshape(bp, tp, c_conv)[:, tp - keep:]
    u_s = u[n_p:n_p + n_s].reshape(bs, ts, c_conv)
    y_s = y_s[:n_s]
    conv_sample = jnp.concatenate([cache_conv[0], u_s], axis=1)[:, -keep:]
    return (y_p.reshape(bp, tp, d), y_s.reshape(bs, ts, d), u_p[None], gla_p[None],
            conv_sample[None], gla_s[None])
```

```python
import functools

import jax
import jax.numpy as jnp
from jax import lax
from jax.experimental import pallas as pl
from jax.experimental.pallas import tpu as pltpu

F32 = jnp.float32
BF16 = jnp.bfloat16
I32 = jnp.int32
U32 = jnp.uint32
EPS = 1e-6
GATE_TEMP = 16.0
HIGHEST = lax.Precision.HIGHEST

LANES = 128
ROW_TILE = 256
EXPERT_ROWS = 256
HIST_PAD = 32
DMA_UNROLL = 8
VMEM_LIMIT = 56 * 1024 * 1024


def _params(semantics, vmem=VMEM_LIMIT):
    return pltpu.CompilerParams(dimension_semantics=semantics, vmem_limit_bytes=vmem)


def _rms(x, g):
    return x * lax.rsqrt(jnp.mean(x * x, axis=-1, keepdims=True) + EPS) * g


def _silu(x):
    return x * jax.nn.sigmoid(x)


def _log_sigmoid(z):
    return jnp.minimum(z, 0.0) - jnp.log(1.0 + jnp.exp(-jnp.abs(z)))


def _pick(i, n_first, first_ref, second_ref):
    return jnp.where(i < n_first, first_ref[...], second_ref[...])


def _split_maps(n_first):
    first = lambda i: (jnp.minimum(i, n_first - 1), 0)
    second = lambda i: (jnp.maximum(i - n_first, 0), 0)
    return first, second


def _pack_pairs(x):
    half = x.shape[1] // 2
    hi = lax.bitcast_convert_type(x[:, :half].astype(BF16).astype(F32), U32)
    lo = lax.bitcast_convert_type(x[:, half:].astype(BF16).astype(F32), U32)
    return hi | (lo >> 16)


def _unpack_pairs(p):
    hi = lax.bitcast_convert_type(p & jnp.uint32(0xFFFF0000), F32)
    lo = lax.bitcast_convert_type(p << 16, F32)
    return hi, lo


def _row_copy(src, s, dst, d, sem):
    return pltpu.make_async_copy(src.at[pl.ds(s, 1)], dst.at[pl.ds(d, 1)], sem)


def _inproj_kernel(xp_ref, xs_ref, g1_ref, w_ref, wlr2_ref, blr2_ref,
                   u_ref, q_ref, k_ref, v_ref, g_ref, la_ref, *, n_first, c_conv, qk, vv, dk):
    i = pl.program_id(0)
    x = _pick(i, n_first, xp_ref, xs_ref)
    h = _rms(x, g1_ref[...]).astype(BF16)

    def mm(lo, width):
        return jnp.dot(h, w_ref[:, lo:lo + width], preferred_element_type=F32)

    a = mm(0, c_conv)
    a_gate = mm(c_conv, c_conv)
    u_ref[...] = a * jax.nn.sigmoid(a_gate)
    off = 2 * c_conv
    q_ref[...] = mm(off, qk) * (dk ** -0.5)
    k_ref[...] = mm(off + qk, qk)
    v_ref[...] = mm(off + 2 * qk, vv)
    g_ref[...] = mm(off + 2 * qk + vv, vv)
    lr = mm(off + 2 * qk + 2 * vv, LANES)
    z = jnp.dot(lr, wlr2_ref[...], precision=HIGHEST, preferred_element_type=F32) + blr2_ref[...]
    la_ref[...] = _log_sigmoid(z) * (1.0 / GATE_TEMP)


def _inproj(xp, xs, g1, w_pad, wlr2_pad, blr2, *, c_conv, qk, vv, dk):
    n_p, d = xp.shape
    n_s = xs.shape[0]
    n_all = n_p + n_s
    n_first = n_p // ROW_TILE
    grid = (n_all // ROW_TILE,)
    first, second = _split_maps(n_first)
    const = lambda i: (0, 0)
    row = lambda i: (i, 0)
    widths = (c_conv, qk, qk, vv, vv, qk)
    return pl.pallas_call(
        functools.partial(_inproj_kernel, n_first=n_first, c_conv=c_conv, qk=qk, vv=vv, dk=dk),
        grid=grid,
        in_specs=[
            pl.BlockSpec((ROW_TILE, d), first),
            pl.BlockSpec((ROW_TILE, d), second),
            pl.BlockSpec((1, d), const),
            pl.BlockSpec(w_pad.shape, const, pipeline_mode=pl.Buffered(1)),
            pl.BlockSpec(wlr2_pad.shape, const),
            pl.BlockSpec((1, qk), const),
        ],
        out_specs=[pl.BlockSpec((ROW_TILE, w), row) for w in widths],
        out_shape=[jax.ShapeDtypeStruct((n_all, w), F32) for w in widths],
        compiler_params=_params(("arbitrary",)),
        name="inproj",
    )(xp, xs, g1, w_pad, wlr2_pad, blr2)


def _conv_kernel(u_ref, hist_ref, w_ref, b_ref, lg_ref, lb_ref, c_ref, win, cbuf, *, tt, width):
    i = pl.program_id(1)

    @pl.when(i == 0)
    def _():
        win[0:HIST_PAD, :] = hist_ref[0]

    @pl.when(i > 0)
    def _():
        win[0:HIST_PAD, :] = win[tt:tt + HIST_PAD, :]

    win[HIST_PAD:HIST_PAD + tt, :] = u_ref[...]
    lead = HIST_PAD - (width - 1)
    n_ch = u_ref.shape[1]
    for cb in range(n_ch // LANES):
        cs = slice(cb * LANES, (cb + 1) * LANES)
        acc = jnp.broadcast_to(b_ref[:, cs], (tt, LANES))
        for j in range(width):
            acc = acc + w_ref[j:j + 1, cs] * win[lead + j:lead + j + tt, cs]
        cbuf[:, cs] = acc
    c = cbuf[...]
    mu = jnp.mean(c, axis=-1, keepdims=True)
    xc = c - mu
    y = xc * lax.rsqrt(jnp.mean(xc * xc, axis=-1, keepdims=True) + EPS) * lg_ref[...] + lb_ref[...]
    c_ref[...] = _silu(y).astype(c_ref.dtype)


def _conv(u_all, hist_pad, w_dw, b_dw, ln_g, ln_b, *, row0, bsz, t, tt):
    n_ch = u_all.shape[1]
    width = w_dw.shape[0]
    nt = t // tt
    blk0 = row0 // tt
    const = lambda b, i: (0, 0)
    return pl.pallas_call(
        functools.partial(_conv_kernel, tt=tt, width=width),
        grid=(bsz, nt),
        in_specs=[
            pl.BlockSpec((tt, n_ch), lambda b, i: (blk0 + b * nt + i, 0)),
            pl.BlockSpec((1, HIST_PAD, n_ch), lambda b, i: (b, 0, 0)),
            pl.BlockSpec(w_dw.shape, const),
            pl.BlockSpec((1, n_ch), const),
            pl.BlockSpec((1, n_ch), const),
            pl.BlockSpec((1, n_ch), const),
        ],
        out_specs=pl.BlockSpec((tt, n_ch), lambda b, i: (b * nt + i, 0)),
        out_shape=jax.ShapeDtypeStruct((bsz * t, n_ch), BF16),
        scratch_shapes=[pltpu.VMEM((tt + HIST_PAD, n_ch), F32), pltpu.VMEM((tt, n_ch), F32)],
        compiler_params=_params(("arbitrary", "arbitrary")),
        name="conv",
    )(u_all, hist_pad, w_dw, b_dw, ln_g, ln_b)


def _gla_kernel(q_ref, k_ref, v_ref, g_ref, la_ref, s0_ref, gn_ref, o_ref, sout_ref, state,
                *, chunk, heads, dk, dv):
    i = pl.program_id(1)

    @pl.when(i == 0)
    def _():
        state[...] = s0_ref[0]

    la = la_ref[...]
    r = lax.broadcasted_iota(I32, (chunk, chunk), 0)
    c = lax.broadcasted_iota(I32, (chunk, chunk), 1)
    causal = c <= r
    b = jnp.dot(causal.astype(F32), la, precision=HIGHEST, preferred_element_type=F32)
    b_end = b[chunk - 1:chunk, :]
    q_in = (q_ref[...] * jnp.exp(b)).astype(BF16)
    k_in = (k_ref[...] * jnp.exp(-b)).astype(BF16)
    k_out = (k_ref[...] * jnp.exp(b_end - b)).astype(BF16)
    decay_row = jnp.exp(b_end)
    eye = lax.broadcasted_iota(I32, (dk, dk), 0) == lax.broadcasted_iota(I32, (dk, dk), 1)
    for h in range(heads):
        ks = slice(h * dk, (h + 1) * dk)
        vs = slice(h * dv, (h + 1) * dv)
        vh = v_ref[:, vs].astype(BF16)
        att = lax.dot_general(q_in[:, ks], k_in[:, ks], (((1,), (1,)), ((), ())), preferred_element_type=F32)
        att = jnp.where(causal, att, 0.0).astype(BF16)
        s_h = state[h]
        o = jnp.dot(att, vh, preferred_element_type=F32)
        o = o + jnp.dot(q_in[:, ks], s_h.astype(BF16), preferred_element_type=F32)
        decay_col = jnp.sum(jnp.where(eye, jnp.broadcast_to(decay_row[:, ks], (dk, dk)), 0.0), axis=1, keepdims=True)
        state[h] = decay_col * s_h + lax.dot_general(k_out[:, ks], vh, (((0,), (0,)), ((), ())),
                                                     preferred_element_type=F32)
        o = o * lax.rsqrt(jnp.mean(o * o, axis=-1, keepdims=True) + EPS) * gn_ref[...]
        o_ref[:, vs] = (o * _silu(g_ref[:, vs])).astype(o_ref.dtype)

    @pl.when(i == pl.num_programs(1) - 1)
    def _():
        sout_ref[0] = state[...]


def _gla(q_all, k_all, v_all, g_all, la_all, s0, gn, *, row0, bsz, t, chunk):
    heads, dk, dv = s0.shape[1:]
    nt = t // chunk
    blk0 = row0 // chunk
    rows = lambda b, i: (blk0 + b * nt + i, 0)
    return pl.pallas_call(
        functools.partial(_gla_kernel, chunk=chunk, heads=heads, dk=dk, dv=dv),
        grid=(bsz, nt),
        in_specs=[
            pl.BlockSpec((chunk, heads * dk), rows),
            pl.BlockSpec((chunk, heads * dk), rows),
            pl.BlockSpec((chunk, heads * dv), rows),
            pl.BlockSpec((chunk, heads * dv), rows),
            pl.BlockSpec((chunk, heads * dk), rows),
            pl.BlockSpec((1, heads, dk, dv), lambda b, i: (b, 0, 0, 0)),
            pl.BlockSpec((1, dv), lambda b, i: (0, 0)),
        ],
        out_specs=[
            pl.BlockSpec((chunk, heads * dv), lambda b, i: (b * nt + i, 0)),
            pl.BlockSpec((1, heads, dk, dv), lambda b, i: (b, 0, 0, 0)),
        ],
        out_shape=[
            jax.ShapeDtypeStruct((bsz * t, heads * dv), BF16),
            jax.ShapeDtypeStruct((bsz, heads, dk, dv), F32),
        ],
        scratch_shapes=[pltpu.VMEM((heads, dk, dv), F32)],
        compiler_params=_params(("arbitrary", "arbitrary")),
        name="gla",
    )(q_all, k_all, v_all, g_all, la_all, s0, gn)


def _mixer_kernel(x_ref, hist_ref, s0_ref, g1_ref, w_ref, wlr_ref, wlr2_ref, blr2_ref,
                  wdw_ref, bdw_ref, lg_ref, lb_ref, gn_ref,
                  c_ref, o_ref, tail_ref, sout_ref, win, cbuf, state,
                  *, n_seq, seq_rows, chunk, heads, dk, dv, c_conv, width):
    i = pl.program_id(1)
    qk, vv = heads * dk, heads * dv

    @pl.when(i == 0)
    def _():
        for s in range(n_seq):
            win[s, 0:HIST_PAD, :] = hist_ref[s]
        state[...] = s0_ref[...]

    @pl.when(i > 0)
    def _():
        for s in range(n_seq):
            win[s, 0:HIST_PAD, :] = win[s, seq_rows:seq_rows + HIST_PAD, :]

    h = _rms(x_ref[...], g1_ref[...]).astype(BF16)

    def mm(lo, n):
        return jnp.dot(h, w_ref[:, lo:lo + n], preferred_element_type=F32)

    u = mm(0, c_conv) * jax.nn.sigmoid(mm(c_conv, c_conv))
    lead = HIST_PAD - (width - 1)
    for s in range(n_seq):
        r0 = s * seq_rows
        win[s, HIST_PAD:HIST_PAD + seq_rows, :] = u[r0:r0 + seq_rows, :]
        for cb in range(c_conv // LANES):
            cs = slice(cb * LANES, (cb + 1) * LANES)
            for t0 in range(0, seq_rows, chunk):
                acc = jnp.broadcast_to(bdw_ref[:, cs], (chunk, LANES))
                for j in range(width):
                    acc = acc + wdw_ref[j:j + 1, cs] * win[s, t0 + lead + j:t0 + lead + j + chunk, cs]
                cbuf[r0 + t0:r0 + t0 + chunk, cs] = acc
    cv = cbuf[...]
    mu = jnp.mean(cv, axis=-1, keepdims=True)
    xc = cv - mu
    cn = xc * lax.rsqrt(jnp.mean(xc * xc, axis=-1, keepdims=True) + EPS) * lg_ref[...] + lb_ref[...]
    c_ref[...] = _silu(cn).astype(c_ref.dtype)

    off = 2 * c_conv
    q = mm(off, qk) * (dk ** -0.5)
    k = mm(off + qk, qk)
    v = mm(off + 2 * qk, vv).astype(BF16)
    g = mm(off + 2 * qk + vv, vv)
    lr = jnp.dot(h, wlr_ref[...], preferred_element_type=F32)
    z = jnp.dot(lr, wlr2_ref[...], precision=HIGHEST, preferred_element_type=F32) + blr2_ref[...]
    la = _log_sigmoid(z) * (1.0 / GATE_TEMP)
    r = lax.broadcasted_iota(I32, (chunk, chunk), 0)
    c = lax.broadcasted_iota(I32, (chunk, chunk), 1)
    causal = c <= r
    tril = causal.astype(F32)
    eye = lax.broadcasted_iota(I32, (dk, dk), 0) == lax.broadcasted_iota(I32, (dk, dk), 1)
    for s in range(n_seq):
        for t0 in range(0, seq_rows, chunk):
            rows = slice(s * seq_rows + t0, s * seq_rows + t0 + chunk)
            b = jnp.dot(tril, la[rows, :], precision=HIGHEST, preferred_element_type=F32)
            b_end = b[chunk - 1:chunk, :]
            q_in = (q[rows, :] * jnp.exp(b)).astype(BF16)
            k_in = (k[rows, :] * jnp.exp(-b)).astype(BF16)
            k_out = (k[rows, :] * jnp.exp(b_end - b)).astype(BF16)
            decay_row = jnp.exp(b_end)
            for hd in range(heads):
                ks = slice(hd * dk, (hd + 1) * dk)
                vs = slice(hd * dv, (hd + 1) * dv)
                vh = v[rows, vs]
                att = lax.dot_general(q_in[:, ks], k_in[:, ks], (((1,), (1,)), ((), ())), preferred_element_type=F32)
                att = jnp.where(causal, att, 0.0).astype(BF16)
                s_h = state[s, hd]
                o = jnp.dot(att, vh, preferred_element_type=F32)
                o = o + jnp.dot(q_in[:, ks], s_h.astype(BF16), preferred_element_type=F32)
                decay_col = jnp.sum(jnp.where(eye, jnp.broadcast_to(decay_row[:, ks], (dk, dk)), 0.0),
                                    axis=1, keepdims=True)
                state[s, hd] = decay_col * s_h + lax.dot_general(k_out[:, ks], vh, (((0,), (0,)), ((), ())),
                                                                 preferred_element_type=F32)
                o = o * lax.rsqrt(jnp.mean(o * o, axis=-1, keepdims=True) + EPS) * gn_ref[...]
                o_ref[rows, vs] = (o * _silu(g[rows, vs])).astype(o_ref.dtype)

    @pl.when(i == pl.num_programs(1) - 1)
    def _():
        for s in range(n_seq):
            tail_ref[s] = win[s, seq_rows:seq_rows + HIST_PAD, :]
        sout_ref[...] = state[...]


def _mixer(x2d, hist_pad, s0, g1, w_main, w_lr, wlr2_pad, blr2, w_dw, b_dw, ln_g, ln_b, gn, *, n_seq, seq_rows):
    bsz, heads, dk, dv = s0.shape
    d = x2d.shape[1]
    t = x2d.shape[0] // bsz
    c_conv = w_dw.shape[1]
    width = w_dw.shape[0]
    qk, vv = heads * dk, heads * dv
    assert (n_seq == 1 and t % seq_rows == 0) or (seq_rows == t and bsz % n_seq == 0)
    nt = t // seq_rows
    rows = n_seq * seq_rows
    chunk = 64 if seq_rows % 64 == 0 else seq_rows
    const = lambda b, i: (0, 0)
    tile = lambda b, i: (b * nt + i, 0)
    per_seq3 = lambda b, i: (b, 0, 0)
    per_seq4 = lambda b, i: (b, 0, 0, 0)
    return pl.pallas_call(
        functools.partial(_mixer_kernel, n_seq=n_seq, seq_rows=seq_rows, chunk=chunk, heads=heads, dk=dk, dv=dv,
                          c_conv=c_conv, width=width),
        grid=(bsz // n_seq, nt),
        in_specs=[
            pl.BlockSpec((rows, d), tile),
            pl.BlockSpec((n_seq, HIST_PAD, c_conv), per_seq3),
            pl.BlockSpec((n_seq, heads, dk, dv), per_seq4),
            pl.BlockSpec((1, d), const),
            pl.BlockSpec(w_main.shape, const, pipeline_mode=pl.Buffered(1)),
            pl.BlockSpec(w_lr.shape, const),
            pl.BlockSpec(wlr2_pad.shape, const),
            pl.BlockSpec((1, qk), const),
            pl.BlockSpec(w_dw.shape, const),
            pl.BlockSpec((1, c_conv), const), pl.BlockSpec((1, c_conv), const), pl.BlockSpec((1, c_conv), const),
            pl.BlockSpec((1, dv), const),
        ],
        out_specs=[
            pl.BlockSpec((rows, c_conv), tile), pl.BlockSpec((rows, vv), tile),
            pl.BlockSpec((n_seq, HIST_PAD, c_conv), per_seq3),
            pl.BlockSpec((n_seq, heads, dk, dv), per_seq4),
        ],
        out_shape=[
            jax.ShapeDtypeStruct((bsz * t, c_conv), BF16), jax.ShapeDtypeStruct((bsz * t, vv), BF16),
            jax.ShapeDtypeStruct((bsz, HIST_PAD, c_conv), F32),
            jax.ShapeDtypeStruct((bsz, heads, dk, dv), F32),
        ],
        scratch_shapes=[
            pltpu.VMEM((n_seq, seq_rows + HIST_PAD, c_conv), F32), pltpu.VMEM((rows, c_conv), F32),
            pltpu.VMEM((n_seq, heads, dk, dv), F32),
        ],
        compiler_params=_params(("arbitrary", "arbitrary")),
        name="mixer",
    )(x2d, hist_pad, s0, g1, w_main, w_lr, wlr2_pad, blr2, w_dw, b_dw, ln_g, ln_b, gn)


def _outproj_kernel(xp_ref, xs_ref, cp_ref, cs_ref, op_ref, os_ref, w_ref, g2_ref, wr_ref, br_ref,
                    x1_ref, dest_ref, gates_ref, pages_ref, counts_ref, sorted_ref,
                    stage, dest_vmem, dest_smem, cnt_s, page_s, npage_s, table_s, scatter_sems, dest_sems,
                    *, n_first, c_conv, n_groups, per_group, trash_row):
    i = pl.program_id(0)
    n_steps = pl.num_programs(0)
    tm = x1_ref.shape[0]
    n_exp = n_groups * per_group
    slot = i % 2
    prev = 1 - slot

    def dest_copy(s):
        return pltpu.make_async_copy(dest_vmem.at[s], dest_smem.at[s], dest_sems.at[s])

    def scatter_wait(s):
        for _ in range(2):
            pltpu.make_async_copy(stage.at[s], sorted_ref.at[pl.ds(0, tm)], scatter_sems.at[s]).wait()

    @pl.when(i == 0)
    def _():
        cnt_s[...] = jnp.zeros_like(cnt_s)
        page_s[...] = jnp.zeros_like(page_s)
        npage_s[...] = jnp.zeros_like(npage_s)
        table_s[...] = jnp.zeros_like(table_s)
        stage[1] = jnp.zeros(stage.shape[1:], stage.dtype)

        def fill(t, carry):
            dest_smem[1, 0, t] = trash_row + t
            dest_smem[1, 1, t] = trash_row + tm + t
            return carry

        lax.fori_loop(0, tm, fill, 0)

    @pl.when(i >= 1)
    def _():
        dest_copy(prev).wait()
        scatter_wait(slot)

    for t in range(tm):
        _row_copy(stage.at[prev], t, sorted_ref, dest_smem[prev, 0, t], scatter_sems.at[prev]).start()
        _row_copy(stage.at[prev], t, sorted_ref, dest_smem[prev, 1, t], scatter_sems.at[prev]).start()

    x = _pick(i, n_first, xp_ref, xs_ref)
    cc = _pick(i, n_first, cp_ref, cs_ref)
    oo = _pick(i, n_first, op_ref, os_ref)
    mix = jnp.dot(cc, w_ref[0:c_conv, :], preferred_element_type=F32)
    mix = mix + jnp.dot(oo, w_ref[c_conv:, :], preferred_element_type=F32)
    x1 = x + mix
    x1_ref[...] = x1
    h2 = _rms(x1, g2_ref[...])
    stage[slot] = _pack_pairs(h2)
    h_hi = h2.astype(BF16)
    h_lo = (h2 - h_hi.astype(F32)).astype(BF16)
    parts = lax.dot_general(wr_ref[...], jnp.concatenate([h_hi, h_lo], axis=0), (((1,), (1,)), ((), ())),
                            preferred_element_type=F32)
    n_r = br_ref.shape[0]
    logits = (parts[0:n_r, 0:tm] + parts[0:n_r, tm:] + parts[n_r:, 0:tm] + parts[n_r:, tm:]) + br_ref[...]
    lc = logits[0:n_groups, :]
    mc = jnp.max(lc, axis=0, keepdims=True)
    p_group = 1.0 / jnp.sum(jnp.exp(lc - mc), axis=0, keepdims=True)
    rows_c = lax.broadcasted_iota(I32, (n_groups, tm), 0)
    g_idx = jnp.min(jnp.where(lc == mc, rows_c, n_groups), axis=0, keepdims=True)
    lf = logits[n_groups:n_groups + n_exp, :]
    rows_f = lax.broadcasted_iota(I32, (n_exp, tm), 0)
    in_group = (rows_f >= g_idx * per_group) & (rows_f < (g_idx + 1) * per_group)
    neg = jnp.float32(-jnp.inf)
    l1 = jnp.where(in_group, lf, neg)
    m1 = jnp.max(l1, axis=0, keepdims=True)
    e1 = jnp.min(jnp.where(l1 == m1, rows_f, n_exp), axis=0, keepdims=True)
    l2 = jnp.where(rows_f == e1, neg, l1)
    m2 = jnp.max(l2, axis=0, keepdims=True)
    e2 = jnp.min(jnp.where(l2 == m2, rows_f, n_exp), axis=0, keepdims=True)
    r2 = jnp.exp(m2 - m1)
    w1 = 1.0 / (1.0 + r2)
    row8 = lax.broadcasted_iota(I32, (8, tm), 0)
    gates_ref[...] = jnp.where(row8 == 0, p_group * w1, jnp.where(row8 == 1, p_group * (r2 * w1), 0.0))

    oh0 = (rows_f == e1).astype(F32)
    oh1 = (rows_f == e2).astype(F32)
    both = oh0 + oh1
    tr = lax.broadcasted_iota(I32, (tm, tm), 0)
    tc = lax.broadcasted_iota(I32, (tm, tm), 1)
    earlier = jnp.dot(both.astype(BF16), (tr < tc).astype(BF16), preferred_element_type=F32)
    cnt = cnt_s[...]
    rank_base = earlier + cnt
    tile_cnt = jnp.sum(both, axis=1, keepdims=True)
    page_rows = float(EXPERT_ROWS)
    k0 = jnp.floor(cnt * (1.0 / page_rows))
    new_cnt = cnt + tile_cnt
    limit = (k0 + 1.0) * page_rows
    need_a = ((cnt == k0 * page_rows) & (tile_cnt > 0.0)).astype(F32)
    need_b = (new_cnt > limit).astype(F32)
    need = need_a + need_b
    er = lax.broadcasted_iota(I32, (n_exp, n_exp), 0)
    ec = lax.broadcasted_iota(I32, (n_exp, n_exp), 1)
    before = jnp.dot((ec < er).astype(BF16), jnp.broadcast_to(need, (n_exp, LANES)).astype(BF16),
                     preferred_element_type=F32)[:, 0:1]
    base = npage_s[...] + before
    page_a = jnp.where(need_a > 0.0, base, page_s[...])
    page_b = base + need_a
    npage_s[...] = npage_s[...] + jnp.sum(need, axis=0, keepdims=True)
    lane = lax.broadcasted_iota(I32, table_s.shape, 1).astype(F32)
    table = jnp.where((lane == k0) & (need_a > 0.0), page_a, table_s[...])
    table_s[...] = jnp.where((lane == k0 + 1.0) & (need_b > 0.0), page_b, table)
    cnt_s[...] = new_cnt
    page_s[...] = jnp.where(jnp.floor(new_cnt * (1.0 / page_rows)) == k0, page_a, page_b)

    def dest_rows(oh):
        rank = jnp.sum(oh * rank_base, axis=0, keepdims=True)
        lim = jnp.sum(oh * limit, axis=0, keepdims=True)
        pa = jnp.sum(oh * page_a, axis=0, keepdims=True)
        pb = jnp.sum(oh * page_b, axis=0, keepdims=True)
        within = rank - jnp.floor(rank * (1.0 / page_rows)) * page_rows
        return jnp.where(rank < lim, pa, pb) * page_rows + within

    dest = jnp.where(row8 == 0, dest_rows(oh0), jnp.where(row8 == 1, dest_rows(oh1), 0.0)).astype(I32)
    dest_ref[...] = dest
    dest_vmem[slot] = dest
    dest_copy(slot).start()

    @pl.when(i == n_steps - 1)
    def _():
        pages_ref[...] = table_s[...].astype(I32)
        counts_ref[...] = jnp.broadcast_to(cnt_s[...], counts_ref.shape).astype(I32)
        dest_copy(slot).wait()
        scatter_wait(prev)

        def last(j, carry):
            for r in range(DMA_UNROLL):
                t = j * DMA_UNROLL + r
                _row_copy(stage.at[slot], t, sorted_ref, dest_smem[slot, 0, t], scatter_sems.at[slot]).start()
                _row_copy(stage.at[slot], t, sorted_ref, dest_smem[slot, 1, t], scatter_sems.at[slot]).start()
            return carry

        lax.fori_loop(0, tm // DMA_UNROLL, last, 0)
        scatter_wait(slot)


def _outproj(xp, xs, cp, cs, op, os_, w_out, g2, wr, br, *, n_groups, per_group, n_pages):
    n_p, d = xp.shape
    n_all = n_p + xs.shape[0]
    n_first = n_p // ROW_TILE
    n_exp = n_groups * per_group
    c_conv = cp.shape[1]
    vv = op.shape[1]
    tile = (d // 2,)
    first, second = _split_maps(n_first)
    const = lambda i: (0, 0)
    row = lambda i: (i, 0)
    col = lambda i: (0, i)
    rows_sorted = n_pages * EXPERT_ROWS + 2 * ROW_TILE
    assert ROW_TILE <= EXPERT_ROWS, "a tile may open at most two pages per expert"
    return pl.pallas_call(
        functools.partial(_outproj_kernel, n_first=n_first, c_conv=c_conv, n_groups=n_groups, per_group=per_group,
                          trash_row=n_pages * EXPERT_ROWS),
        grid=(n_all // ROW_TILE,),
        in_specs=[
            pl.BlockSpec((ROW_TILE, d), first), pl.BlockSpec((ROW_TILE, d), second),
            pl.BlockSpec((ROW_TILE, c_conv), first), pl.BlockSpec((ROW_TILE, c_conv), second),
            pl.BlockSpec((ROW_TILE, vv), first), pl.BlockSpec((ROW_TILE, vv), second),
            pl.BlockSpec(w_out.shape, const, pipeline_mode=pl.Buffered(1)),
            pl.BlockSpec((1, d), const),
            pl.BlockSpec(wr.shape, const),
            pl.BlockSpec(br.shape, const),
        ],
        out_specs=[
            pl.BlockSpec((ROW_TILE, d), row),
            pl.BlockSpec((8, ROW_TILE), col), pl.BlockSpec((8, ROW_TILE), col),
            pl.BlockSpec((n_exp, LANES), const), pl.BlockSpec((n_exp, LANES), const),
            pl.BlockSpec(memory_space=pl.ANY),
        ],
        out_shape=[
            jax.ShapeDtypeStruct((n_all, d), F32),
            jax.ShapeDtypeStruct((8, n_all), I32), jax.ShapeDtypeStruct((8, n_all), F32),
            jax.ShapeDtypeStruct((n_exp, LANES), I32), jax.ShapeDtypeStruct((n_exp, LANES), I32),
            jax.ShapeDtypeStruct((rows_sorted,) + tile, U32),
        ],
        scratch_shapes=[
            pltpu.VMEM((2, ROW_TILE) + tile, U32),
            pltpu.VMEM((2, 8, ROW_TILE), I32), pltpu.SMEM((2, 8, ROW_TILE), I32),
            pltpu.VMEM((n_exp, 1), F32), pltpu.VMEM((n_exp, 1), F32), pltpu.VMEM((1, 1), F32),
            pltpu.VMEM((n_exp, LANES), F32),
            pltpu.SemaphoreType.DMA((2,)), pltpu.SemaphoreType.DMA((2,)),
        ],
        compiler_params=_params(("arbitrary",)),
        name="outproj",
    )(xp, xs, cp, cs, op, os_, w_out, g2, wr, br)


def _experts_kernel(cnt_ref, pages_ref, xs_ref, wg_ref, wu_ref, wd_ref, ysp_ref,
                    xbuf, ybuf, wg_f32, wu_f32, wd_f32, wg_bf, wu_bf, wd_bf, first_blk, page_seq,
                    gsems, ysems, wsems, *, n_exp, table_lanes):
    e = pl.program_id(0)
    tb = xbuf.shape[1]
    n_pages = page_seq.shape[0]

    def n_pages_of(ex):
        return (cnt_ref[ex] + (tb - 1)) // tb

    def page_rows(blk):
        return pl.ds(pl.multiple_of(page_seq[blk] * tb, tb), tb)

    def fetch(blk, slot):
        return pltpu.make_async_copy(xs_ref.at[page_rows(blk)], xbuf.at[slot], gsems.at[slot])

    def writeback(blk, slot):
        return pltpu.make_async_copy(ybuf.at[slot], ysp_ref.at[page_rows(blk)], ysems.at[slot])

    def weight_copies(ex, slot):
        return (pltpu.make_async_copy(wg_ref.at[ex], wg_f32.at[slot], wsems.at[slot]),
                pltpu.make_async_copy(wu_ref.at[ex], wu_f32.at[slot], wsems.at[slot]),
                pltpu.make_async_copy(wd_ref.at[ex], wd_f32.at[slot], wsems.at[slot]))

    @pl.when(e == 0)
    def _():
        for cp in weight_copies(0, 0):
            cp.start(priority=1)

        def per_expert(ex, blk):
            first_blk[ex] = blk

            def per_page(j, carry):
                page_seq[blk + j] = pages_ref[ex * table_lanes + j]
                return carry

            lax.fori_loop(0, n_pages_of(ex), per_page, 0)
            return blk + n_pages_of(ex)

        first_blk[n_exp] = lax.fori_loop(0, n_exp, per_expert, 0)
        fetch(0, 0).start()

    @pl.when(e + 1 < n_exp)
    def _():
        for cp in weight_copies(e + 1, (e + 1) % 2):
            cp.start(priority=1)

    b_lo = first_blk[e]
    b_hi = first_blk[e + 1]
    n_total = first_blk[n_exp]
    wslot = e % 2
    for cp in weight_copies(e, wslot):
        cp.wait()
    wg_bf[...] = wg_f32[wslot].astype(BF16)
    wu_bf[...] = wu_f32[wslot].astype(BF16)
    wd_bf[...] = wd_f32[wslot].astype(BF16)
    row_id = lax.broadcasted_iota(I32, (tb, 1), 0)

    def block(b, carry):
        slot = b % 2

        @pl.when(b >= 2)
        def _():
            writeback(b, slot).wait()

        fetch(b, slot).wait()
        fetch(jnp.minimum(b + 1, n_total - 1), 1 - slot).start()
        valid = cnt_ref[e] - (b - b_lo) * tb
        words = jnp.where(row_id < valid, xbuf[slot], jnp.uint32(0))
        hi, lo = _unpack_pairs(words)
        x = jnp.concatenate([hi.astype(BF16), lo.astype(BF16)], axis=1)
        hg = jnp.dot(x, wg_bf[...], preferred_element_type=F32)
        hu = jnp.dot(x, wu_bf[...], preferred_element_type=F32)
        hb = (_silu(hg) * hu).astype(BF16)
        ybuf[slot] = _pack_pairs(jnp.dot(hb, wd_bf[...], preferred_element_type=F32))
        writeback(b, slot).start()
        return carry

    lax.fori_loop(b_lo, b_hi, block, 0)

    @pl.when(e == n_exp - 1)
    def _():
        fetch(0, n_total % 2).wait()

        @pl.when(n_total >= 2)
        def _():
            writeback(0, n_total % 2).wait()

        writeback(0, (n_total + 1) % 2).wait()
        ybuf[0] = jnp.zeros(ybuf.shape[1:], ybuf.dtype)

        def spare(blk):
            return pltpu.make_async_copy(ybuf.at[0], ysp_ref.at[pl.ds(pl.multiple_of(blk * tb, tb), tb)], ysems.at[0])

        def zero(blk, carry):
            spare(blk).start()
            return carry

        lax.fori_loop(n_total, n_pages, zero, 0)

        def zero_wait(blk, carry):
            spare(0).wait()
            return carry

        lax.fori_loop(n_total, n_pages, zero_wait, 0)


def _experts(counts, pages_flat, xs_sorted, w_gate, w_up, w_down, *, n_pages, table_lanes):
    tile = xs_sorted.shape[1:]
    n_exp, d, ff = w_gate.shape
    anyspec = pl.BlockSpec(memory_space=pl.ANY)
    grid_spec = pltpu.PrefetchScalarGridSpec(
        num_scalar_prefetch=2,
        grid=(n_exp,),
        in_specs=[anyspec, anyspec, anyspec, anyspec],
        out_specs=anyspec,
        scratch_shapes=[
            pltpu.VMEM((2, EXPERT_ROWS) + tile, U32), pltpu.VMEM((2, EXPERT_ROWS) + tile, U32),
            pltpu.VMEM((2, d, ff), F32), pltpu.VMEM((2, d, ff), F32), pltpu.VMEM((2, ff, d), F32),
            pltpu.VMEM((d, ff), BF16), pltpu.VMEM((d, ff), BF16), pltpu.VMEM((ff, d), BF16),
            pltpu.SMEM((n_exp + 1,), I32), pltpu.SMEM((n_pages,), I32),
            pltpu.SemaphoreType.DMA((2,)), pltpu.SemaphoreType.DMA((2,)), pltpu.SemaphoreType.DMA((2,)),
        ],
    )
    return pl.pallas_call(
        functools.partial(_experts_kernel, n_exp=n_exp, table_lanes=table_lanes),
        grid_spec=grid_spec,
        out_shape=jax.ShapeDtypeStruct((n_pages * EXPERT_ROWS,) + tile, U32),
        compiler_params=_params(("arbitrary",)),
        name="experts",
    )(counts, pages_flat, xs_sorted, w_gate, w_up, w_down)


def _combine_kernel(dest_ref, dest_next_ref, gates_ref, x1_ref, ysp_ref, gf_ref, yp_ref, ysmp_ref,
                    buf0, buf1, sems, *, n_first):
    i = pl.program_id(0)
    n = pl.num_programs(0)
    tm = x1_ref.shape[0]
    slot = i % 2

    def gather(d_ref, s):
        def body(j, carry):
            for r in range(DMA_UNROLL):
                t = j * DMA_UNROLL + r
                _row_copy(ysp_ref, d_ref[0, t], buf0.at[s], t, sems.at[s]).start()
                _row_copy(ysp_ref, d_ref[1, t], buf1.at[s], t, sems.at[s]).start()
            return carry

        lax.fori_loop(0, tm // DMA_UNROLL, body, 0)

    @pl.when(i == 0)
    def _():
        gather(dest_ref, 0)

    @pl.when(i + 1 < n)
    def _():
        gather(dest_next_ref, 1 - slot)

    pltpu.make_async_copy(ysp_ref.at[pl.ds(0, tm)], buf0.at[slot], sems.at[slot]).wait()
    pltpu.make_async_copy(ysp_ref.at[pl.ds(0, tm)], buf1.at[slot], sems.at[slot]).wait()
    hi0, lo0 = _unpack_pairs(buf0[slot])
    hi1, lo1 = _unpack_pairs(buf1[slot])
    g0 = gates_ref[:, 0:1]
    g1 = gates_ref[:, 1:2]
    moe = jnp.concatenate([g0 * hi0 + g1 * hi1, g0 * lo0 + g1 * lo1], axis=1)
    y = _rms(x1_ref[...] + moe, gf_ref[...])

    @pl.when(i < n_first)
    def _():
        yp_ref[...] = y

    @pl.when(i >= n_first)
    def _():
        ysmp_ref[...] = y


def _combine(dest, gates_t, x1, ysp, gf, *, n_p):
    n_all, d = x1.shape
    tile = ysp.shape[1:]
    n_first = n_p // ROW_TILE
    n_tiles = n_all // ROW_TILE
    first, second = _split_maps(n_first)
    return pl.pallas_call(
        functools.partial(_combine_kernel, n_first=n_first),
        grid=(n_tiles,),
        in_specs=[
            pl.BlockSpec((8, ROW_TILE), lambda i: (0, i), memory_space=pltpu.SMEM),
            pl.BlockSpec((8, ROW_TILE), lambda i: (0, jnp.minimum(i + 1, n_tiles - 1)), memory_space=pltpu.SMEM),
            pl.BlockSpec((ROW_TILE, 8), lambda i: (i, 0)),
            pl.BlockSpec((ROW_TILE, d), lambda i: (i, 0)),
            pl.BlockSpec(memory_space=pl.ANY),
            pl.BlockSpec((1, d), lambda i: (0, 0)),
        ],
        out_specs=[pl.BlockSpec((ROW_TILE, d), first), pl.BlockSpec((ROW_TILE, d), second)],
        out_shape=[jax.ShapeDtypeStruct((n_p, d), F32), jax.ShapeDtypeStruct((n_all - n_p, d), F32)],
        scratch_shapes=[pltpu.VMEM((2, ROW_TILE) + tile, U32), pltpu.VMEM((2, ROW_TILE) + tile, U32),
                        pltpu.SemaphoreType.DMA((2,))],
        compiler_params=_params(("arbitrary",)),
        name="combine",
    )(dest, dest, gates_t, x1, ysp, gf)


def _chunk_for(t):
    return 64 if t % 64 == 0 else t


def kernel(x_prompt, x_sample, cache_conv, state_gla, norm1_g, w_in, w_lr2, b_lr2, w_dw, b_dw, conv_ln_g, conv_ln_b, gla_norm_g, w_out, norm2_g, w_router_coarse, b_router_coarse, w_router_fine, b_router_fine, w_exp_gate, w_exp_up, w_exp_down, norm_f_g):
    assert norm1_g.shape[0] == 1, "single trunk layer"
    bp, tp, d = x_prompt.shape
    bs, ts, _ = x_sample.shape
    heads, dk, dv = state_gla.shape[2:]
    c_conv = w_dw.shape[2]
    width = w_dw.shape[1]
    rank = w_lr2.shape[1]
    qk, vv = heads * dk, heads * dv
    n_groups, _, per_group = w_router_fine.shape[1:]
    n_exp = n_groups * per_group
    n_p, n_s = bp * tp, bs * ts
    n_s_pad = -(-n_s // ROW_TILE) * ROW_TILE
    pad_rows = lambda a: jnp.pad(a, ((0, n_s_pad - n_s), (0, 0)))
    n_all = n_p + n_s_pad
    assert n_p % ROW_TILE == 0 and width - 1 <= HIST_PAD

    xp = x_prompt.reshape(n_p, d)
    xs = pad_rows(x_sample.reshape(n_s, d))
    row = lambda a: a.reshape(1, -1)

    n_main = 2 * c_conv + 2 * qk + 2 * vv
    w_main = w_in[0][:, :n_main].astype(BF16)
    w_lr = jnp.pad(w_in[0][:, n_main:], ((0, 0), (0, LANES - rank))).astype(BF16)
    w_lr2_pad = jnp.pad(w_lr2[0], ((0, LANES - rank), (0, 0)))
    mixer_args = (row(norm1_g[0]), w_main, w_lr, w_lr2_pad, row(b_lr2[0]),
                  w_dw[0], row(b_dw[0]), row(conv_ln_g[0]), row(conv_ln_b[0]), row(gla_norm_g[0]))
    hist_p = jnp.zeros((bp, HIST_PAD, c_conv), F32)
    hist_s = jnp.pad(cache_conv[0], ((0, 0), (HIST_PAD - (width - 1), 0), (0, 0)))
    s0_p = jnp.zeros((bp, heads, dk, dv), F32)
    c_p, o_p, tail_p, gla_p = _mixer(xp, hist_p, s0_p, *mixer_args, n_seq=1, seq_rows=ROW_TILE)
    c_s, o_s, tail_s, gla_s = _mixer(x_sample.reshape(n_s, d), hist_s, state_gla[0], *mixer_args,
                                     n_seq=bs, seq_rows=ts)
    c_s, o_s = pad_rows(c_s), pad_rows(o_s)

    wr = jnp.concatenate([w_router_coarse[0].T,
                          jnp.transpose(w_router_fine[0], (0, 2, 1)).reshape(n_exp, d)], axis=0)
    br = jnp.concatenate([b_router_coarse[0], b_router_fine[0].reshape(n_exp)])
    r_rows = -(-(n_groups + n_exp) // 8) * 8
    wr = jnp.pad(wr, ((0, r_rows - wr.shape[0]), (0, 0)))
    br = jnp.pad(br, (0, r_rows - br.shape[0])).reshape(r_rows, 1)
    wr_hi = wr.astype(BF16)
    wr = jnp.concatenate([wr_hi, (wr - wr_hi.astype(F32)).astype(BF16)], axis=0)
    n_pages = (2 * n_all) // EXPERT_ROWS + n_exp
    assert n_all // EXPERT_ROWS + 2 <= LANES, "page table row must hold one expert's pages"
    x1, dest, gates, pages, counts, xs_sorted = _outproj(
        xp, xs, c_p, c_s, o_p, o_s, w_out[0].astype(BF16), row(norm2_g[0]), wr, br,
        n_groups=n_groups, per_group=per_group, n_pages=n_pages)
    ysp = _experts(counts[:, 0], pages.reshape(-1), xs_sorted, w_exp_gate[0], w_exp_up[0], w_exp_down[0],
                   n_pages=n_pages, table_lanes=LANES)
    y_p, y_s = _combine(dest, gates.T, x1, ysp, row(norm_f_g), n_p=n_p)

    lead = HIST_PAD - (width - 1)
    return (y_p.reshape(bp, tp, d), y_s[:n_s].reshape(bs, ts, d), tail_p[:, lead:][None], gla_p[None],
            tail_s[:, lead:][None], gla_s[None])
```

```python
import functools

import jax
import jax.numpy as jnp
from jax import lax
from jax.experimental import pallas as pl
from jax.experimental.pallas import tpu as pltpu

F32 = jnp.float32
BF16 = jnp.bfloat16
I32 = jnp.int32
U32 = jnp.uint32
EPS = 1e-6
GATE_TEMP = 16.0
HIGHEST = lax.Precision.HIGHEST

LANES = 128
ROW_TILE = 256
EXPERT_ROWS = 256
HIST_PAD = 32
DMA_UNROLL = 8
VMEM_LIMIT = 56 * 1024 * 1024


def _params(semantics, vmem=VMEM_LIMIT):
    return pltpu.CompilerParams(dimension_semantics=semantics, vmem_limit_bytes=vmem)


def _rms(x, g):
    return x * lax.rsqrt(jnp.mean(x * x, axis=-1, keepdims=True) + EPS) * g


def _silu(x):
    return x * jax.nn.sigmoid(x)


def _log_sigmoid(z):
    return jnp.minimum(z, 0.0) - jnp.log(1.0 + jnp.exp(-jnp.abs(z)))


def _pick(i, n_first, first_ref, second_ref):
    return jnp.where(i < n_first, first_ref[...], second_ref[...])


def _split_maps(n_first):
    first = lambda i: (jnp.minimum(i, n_first - 1), 0)
    second = lambda i: (jnp.maximum(i - n_first, 0), 0)
    return first, second


def _pack_pairs(x):
    half = x.shape[1] // 2
    hi = lax.bitcast_convert_type(x[:, :half].astype(BF16).astype(F32), U32)
    lo = lax.bitcast_convert_type(x[:, half:].astype(BF16).astype(F32), U32)
    return hi | (lo >> 16)


def _unpack_pairs(p):
    hi = lax.bitcast_convert_type(p & jnp.uint32(0xFFFF0000), F32)
    lo = lax.bitcast_convert_type(p << 16, F32)
    return hi, lo


def _row_copy(src, s, dst, d, sem):
    return pltpu.make_async_copy(src.at[pl.ds(s, 1)], dst.at[pl.ds(d, 1)], sem)


def _inproj_kernel(xp_ref, xs_ref, g1_ref, w_ref, wlr2_ref, blr2_ref,
                   u_ref, q_ref, k_ref, v_ref, g_ref, la_ref, *, n_first, c_conv, qk, vv, dk):
    i = pl.program_id(0)
    x = _pick(i, n_first, xp_ref, xs_ref)
    h = _rms(x, g1_ref[...]).astype(BF16)

    def mm(lo, width):
        return jnp.dot(h, w_ref[:, lo:lo + width], preferred_element_type=F32)

    a = mm(0, c_conv)
    a_gate = mm(c_conv, c_conv)
    u_ref[...] = a * jax.nn.sigmoid(a_gate)
    off = 2 * c_conv
    q_ref[...] = mm(off, qk) * (dk ** -0.5)
    k_ref[...] = mm(off + qk, qk)
    v_ref[...] = mm(off + 2 * qk, vv)
    g_ref[...] = mm(off + 2 * qk + vv, vv)
    lr = mm(off + 2 * qk + 2 * vv, LANES)
    z = jnp.dot(lr, wlr2_ref[...], precision=HIGHEST, preferred_element_type=F32) + blr2_ref[...]
    la_ref[...] = _log_sigmoid(z) * (1.0 / GATE_TEMP)


def _inproj(xp, xs, g1, w_pad, wlr2_pad, blr2, *, c_conv, qk, vv, dk):
    n_p, d = xp.shape
    n_s = xs.shape[0]
    n_all = n_p + n_s
    n_first = n_p // ROW_TILE
    grid = (n_all // ROW_TILE,)
    first, second = _split_maps(n_first)
    const = lambda i: (0, 0)
    row = lambda i: (i, 0)
    widths = (c_conv, qk, qk, vv, vv, qk)
    return pl.pallas_call(
        functools.partial(_inproj_kernel, n_first=n_first, c_conv=c_conv, qk=qk, vv=vv, dk=dk),
        grid=grid,
        in_specs=[
            pl.BlockSpec((ROW_TILE, d), first),
            pl.BlockSpec((ROW_TILE, d), second),
            pl.BlockSpec((1, d), const),
            pl.BlockSpec(w_pad.shape, const, pipeline_mode=pl.Buffered(1)),
            pl.BlockSpec(wlr2_pad.shape, const),
            pl.BlockSpec((1, qk), const),
        ],
        out_specs=[pl.BlockSpec((ROW_TILE, w), row) for w in widths],
        out_shape=[jax.ShapeDtypeStruct((n_all, w), F32) for w in widths],
        compiler_params=_params(("arbitrary",)),
        name="inproj",
    )(xp, xs, g1, w_pad, wlr2_pad, blr2)


def _conv_kernel(u_ref, hist_ref, w_ref, b_ref, lg_ref, lb_ref, c_ref, win, cbuf, *, tt, width):
    i = pl.program_id(1)

    @pl.when(i == 0)
    def _():
        win[0:HIST_PAD, :] = hist_ref[0]

    @pl.when(i > 0)
    def _():
        win[0:HIST_PAD, :] = win[tt:tt + HIST_PAD, :]

    win[HIST_PAD:HIST_PAD + tt, :] = u_ref[...]
    lead = HIST_PAD - (width - 1)
    n_ch = u_ref.shape[1]
    for cb in range(n_ch // LANES):
        cs = slice(cb * LANES, (cb + 1) * LANES)
        acc = jnp.broadcast_to(b_ref[:, cs], (tt, LANES))
        for j in range(width):
            acc = acc + w_ref[j:j + 1, cs] * win[lead + j:lead + j + tt, cs]
        cbuf[:, cs] = acc
    c = cbuf[...]
    mu = jnp.mean(c, axis=-1, keepdims=True)
    xc = c - mu
    y = xc * lax.rsqrt(jnp.mean(xc * xc, axis=-1, keepdims=True) + EPS) * lg_ref[...] + lb_ref[...]
    c_ref[...] = _silu(y).astype(c_ref.dtype)


def _conv(u_all, hist_pad, w_dw, b_dw, ln_g, ln_b, *, row0, bsz, t, tt):
    n_ch = u_all.shape[1]
    width = w_dw.shape[0]
    nt = t // tt
    blk0 = row0 // tt
    const = lambda b, i: (0, 0)
    return pl.pallas_call(
        functools.partial(_conv_kernel, tt=tt, width=width),
        grid=(bsz, nt),
        in_specs=[
            pl.BlockSpec((tt, n_ch), lambda b, i: (blk0 + b * nt + i, 0)),
            pl.BlockSpec((1, HIST_PAD, n_ch), lambda b, i: (b, 0, 0)),
            pl.BlockSpec(w_dw.shape, const),
            pl.BlockSpec((1, n_ch), const),
            pl.BlockSpec((1, n_ch), const),
            pl.BlockSpec((1, n_ch), const),
        ],
        out_specs=pl.BlockSpec((tt, n_ch), lambda b, i: (b * nt + i, 0)),
        out_shape=jax.ShapeDtypeStruct((bsz * t, n_ch), BF16),
        scratch_shapes=[pltpu.VMEM((tt + HIST_PAD, n_ch), F32), pltpu.VMEM((tt, n_ch), F32)],
        compiler_params=_params(("arbitrary", "arbitrary")),
        name="conv",
    )(u_all, hist_pad, w_dw, b_dw, ln_g, ln_b)


def _gla_kernel(q_ref, k_ref, v_ref, g_ref, la_ref, s0_ref, gn_ref, o_ref, sout_ref, state,
                *, chunk, heads, dk, dv):
    i = pl.program_id(1)

    @pl.when(i == 0)
    def _():
        state[...] = s0_ref[0]

    la = la_ref[...]
    r = lax.broadcasted_iota(I32, (chunk, chunk), 0)
    c = lax.broadcasted_iota(I32, (chunk, chunk), 1)
    causal = c <= r
    b = jnp.dot(causal.astype(F32), la, precision=HIGHEST, preferred_element_type=F32)
    b_end = b[chunk - 1:chunk, :]
    q_in = (q_ref[...] * jnp.exp(b)).astype(BF16)
    k_in = (k_ref[...] * jnp.exp(-b)).astype(BF16)
    k_out = (k_ref[...] * jnp.exp(b_end - b)).astype(BF16)
    decay_row = jnp.exp(b_end)
    eye = lax.broadcasted_iota(I32, (dk, dk), 0) == lax.broadcasted_iota(I32, (dk, dk), 1)
    for h in range(heads):
        ks = slice(h * dk, (h + 1) * dk)
        vs = slice(h * dv, (h + 1) * dv)
        vh = v_ref[:, vs].astype(BF16)
        att = lax.dot_general(q_in[:, ks], k_in[:, ks], (((1,), (1,)), ((), ())), preferred_element_type=F32)
        att = jnp.where(causal, att, 0.0).astype(BF16)
        s_h = state[h]
        o = jnp.dot(att, vh, preferred_element_type=F32)
        o = o + jnp.dot(q_in[:, ks], s_h.astype(BF16), preferred_element_type=F32)
        decay_col = jnp.sum(jnp.where(eye, jnp.broadcast_to(decay_row[:, ks], (dk, dk)), 0.0), axis=1, keepdims=True)
        state[h] = decay_col * s_h + lax.dot_general(k_out[:, ks], vh, (((0,), (0,)), ((), ())),
                                                     preferred_element_type=F32)
        o = o * lax.rsqrt(jnp.mean(o * o, axis=-1, keepdims=True) + EPS) * gn_ref[...]
        o_ref[:, vs] = (o * _silu(g_ref[:, vs])).astype(o_ref.dtype)

    @pl.when(i == pl.num_programs(1) - 1)
    def _():
        sout_ref[0] = state[...]


def _gla(q_all, k_all, v_all, g_all, la_all, s0, gn, *, row0, bsz, t, chunk):
    heads, dk, dv = s0.shape[1:]
    nt = t // chunk
    blk0 = row0 // chunk
    rows = lambda b, i: (blk0 + b * nt + i, 0)
    return pl.pallas_call(
        functools.partial(_gla_kernel, chunk=chunk, heads=heads, dk=dk, dv=dv),
        grid=(bsz, nt),
        in_specs=[
            pl.BlockSpec((chunk, heads * dk), rows),
            pl.BlockSpec((chunk, heads * dk), rows),
            pl.BlockSpec((chunk, heads * dv), rows),
            pl.BlockSpec((chunk, heads * dv), rows),
            pl.BlockSpec((chunk, heads * dk), rows),
            pl.BlockSpec((1, heads, dk, dv), lambda b, i: (b, 0, 0, 0)),
            pl.BlockSpec((1, dv), lambda b, i: (0, 0)),
        ],
        out_specs=[
            pl.BlockSpec((chunk, heads * dv), lambda b, i: (b * nt + i, 0)),
            pl.BlockSpec((1, heads, dk, dv), lambda b, i: (b, 0, 0, 0)),
        ],
        out_shape=[
            jax.ShapeDtypeStruct((bsz * t, heads * dv), BF16),
            jax.ShapeDtypeStruct((bsz, heads, dk, dv), F32),
        ],
        scratch_shapes=[pltpu.VMEM((heads, dk, dv), F32)],
        compiler_params=_params(("arbitrary", "arbitrary")),
        name="gla",
    )(q_all, k_all, v_all, g_all, la_all, s0, gn)


def _mixer_kernel(x_ref, hist_ref, s0_ref, g1_ref, w_ref, wlr2_ref, blr2_ref,
                  wdw_ref, bdw_ref, lg_ref, lb_ref, gn_ref,
                  c_ref, o_ref, tail_ref, sout_ref, win, shifted, cbuf, state,
                  *, n_seq, seq_rows, chunk, heads, dk, dv, c_conv, width):
    i = pl.program_id(1)
    qk, vv = heads * dk, heads * dv

    @pl.when(i == 0)
    def _():
        for s in range(n_seq):
            win[s, 0:HIST_PAD, :] = hist_ref[s]
        state[...] = s0_ref[...]

    @pl.when(i > 0)
    def _():
        for s in range(n_seq):
            win[s, 0:HIST_PAD, :] = win[s, seq_rows:seq_rows + HIST_PAD, :]

    h = _rms(x_ref[...], g1_ref[...]).astype(BF16)

    def mm(lo, n):
        return jnp.dot(h, w_ref[:, lo:lo + n], preferred_element_type=F32)

    u = mm(0, c_conv) * jax.nn.sigmoid(mm(c_conv, c_conv))
    lead = HIST_PAD - (width - 1)
    for s in range(n_seq):
        r0 = s * seq_rows
        win[s, HIST_PAD:HIST_PAD + seq_rows, :] = u[r0:r0 + seq_rows, :]
        span = shifted.shape[1]
        for cb in range(c_conv // LANES):
            cs = slice(cb * LANES, (cb + 1) * LANES)
            for r in range(1, 8):
                shifted[r - 1, :, cs] = win[s, r:r + span, cs]
            for t0 in range(0, seq_rows, chunk):
                acc = jnp.broadcast_to(bdw_ref[:, cs], (chunk, LANES))
                for j in range(width):
                    r, a8 = (lead + j) % 8, (lead + j) // 8 * 8
                    if r == 0:
                        tap = win[s, t0 + a8:t0 + a8 + chunk, cs]
                    else:
                        tap = shifted[r - 1, t0 + a8:t0 + a8 + chunk, cs]
                    acc = acc + wdw_ref[j:j + 1, cs] * tap
                cbuf[r0 + t0:r0 + t0 + chunk, cs] = acc
    cv = cbuf[...]
    mu = jnp.mean(cv, axis=-1, keepdims=True)
    xc = cv - mu
    cn = xc * lax.rsqrt(jnp.mean(xc * xc, axis=-1, keepdims=True) + EPS) * lg_ref[...] + lb_ref[...]
    c_ref[...] = _silu(cn).astype(c_ref.dtype)

    off = 2 * c_conv
    q = mm(off, qk) * (dk ** -0.5)
    k = mm(off + qk, qk)
    v = mm(off + 2 * qk, vv).astype(BF16)
    g = mm(off + 2 * qk + vv, vv)
    lr = mm(off + 2 * qk + 2 * vv, wlr2_ref.shape[0])
    z = jnp.dot(lr, wlr2_ref[...], precision=HIGHEST, preferred_element_type=F32) + blr2_ref[...]
    la = _log_sigmoid(z) * (1.0 / GATE_TEMP)
    r = lax.broadcasted_iota(I32, (chunk, chunk), 0)
    c = lax.broadcasted_iota(I32, (chunk, chunk), 1)
    causal = c <= r
    tril = causal.astype(F32)
    eye = lax.broadcasted_iota(I32, (dk, dk), 0) == lax.broadcasted_iota(I32, (dk, dk), 1)
    for s in range(n_seq):
        for t0 in range(0, seq_rows, chunk):
            rows = slice(s * seq_rows + t0, s * seq_rows + t0 + chunk)
            b = jnp.dot(tril, la[rows, :], precision=HIGHEST, preferred_element_type=F32)
            b_end = b[chunk - 1:chunk, :]
            q_in = (q[rows, :] * jnp.exp(b)).astype(BF16)
            k_in = (k[rows, :] * jnp.exp(-b)).astype(BF16)
            k_out = (k[rows, :] * jnp.exp(b_end - b)).astype(BF16)
            decay_row = jnp.exp(b_end)
            for hd in range(heads):
                ks = slice(hd * dk, (hd + 1) * dk)
                vs = slice(hd * dv, (hd + 1) * dv)
                vh = v[rows, vs]
                att = lax.dot_general(q_in[:, ks], k_in[:, ks], (((1,), (1,)), ((), ())), preferred_element_type=F32)
                att = jnp.where(causal, att, 0.0).astype(BF16)
                s_h = state[s, hd]
                o = jnp.dot(att, vh, preferred_element_type=F32)
                o = o + jnp.dot(q_in[:, ks], s_h.astype(BF16), preferred_element_type=F32)
                decay_col = jnp.sum(jnp.where(eye, jnp.broadcast_to(decay_row[:, ks], (dk, dk)), 0.0),
                                    axis=1, keepdims=True)
                state[s, hd] = decay_col * s_h + lax.dot_general(k_out[:, ks], vh, (((0,), (0,)), ((), ())),
                                                                 preferred_element_type=F32)
                o = o * lax.rsqrt(jnp.mean(o * o, axis=-1, keepdims=True) + EPS) * gn_ref[...]
                o_ref[rows, vs] = (o * _silu(g[rows, vs])).astype(o_ref.dtype)

    @pl.when(i == pl.num_programs(1) - 1)
    def _():
        for s in range(n_seq):
            tail_ref[s] = win[s, seq_rows:seq_rows + HIST_PAD, :]
        sout_ref[...] = state[...]


def _mixer(x2d, hist_pad, s0, g1, w_in, w_lr2, blr2, w_dw, b_dw, ln_g, ln_b, gn, *, n_seq, seq_rows):
    bsz, heads, dk, dv = s0.shape
    d = x2d.shape[1]
    t = x2d.shape[0] // bsz
    c_conv = w_dw.shape[1]
    width = w_dw.shape[0]
    qk, vv = heads * dk, heads * dv
    assert (n_seq == 1 and t % seq_rows == 0) or (seq_rows == t and bsz % n_seq == 0)
    nt = t // seq_rows
    rows = n_seq * seq_rows
    chunk = 64 if seq_rows % 64 == 0 else seq_rows
    const = lambda b, i: (0, 0)
    tile = lambda b, i: (b * nt + i, 0)
    per_seq3 = lambda b, i: (b, 0, 0)
    per_seq4 = lambda b, i: (b, 0, 0, 0)
    return pl.pallas_call(
        functools.partial(_mixer_kernel, n_seq=n_seq, seq_rows=seq_rows, chunk=chunk, heads=heads, dk=dk, dv=dv,
                          c_conv=c_conv, width=width),
        grid=(bsz // n_seq, nt),
        in_specs=[
            pl.BlockSpec((rows, d), tile),
            pl.BlockSpec((n_seq, HIST_PAD, c_conv), per_seq3),
            pl.BlockSpec((n_seq, heads, dk, dv), per_seq4),
            pl.BlockSpec((1, d), const),
            pl.BlockSpec(w_in.shape, const, pipeline_mode=pl.Buffered(1)),
            pl.BlockSpec(w_lr2.shape, const),
            pl.BlockSpec((1, qk), const),
            pl.BlockSpec(w_dw.shape, const),
            pl.BlockSpec((1, c_conv), const), pl.BlockSpec((1, c_conv), const), pl.BlockSpec((1, c_conv), const),
            pl.BlockSpec((1, dv), const),
        ],
        out_specs=[
            pl.BlockSpec((rows, c_conv), tile), pl.BlockSpec((rows, vv), tile),
            pl.BlockSpec((n_seq, HIST_PAD, c_conv), per_seq3),
            pl.BlockSpec((n_seq, heads, dk, dv), per_seq4),
        ],
        out_shape=[
            jax.ShapeDtypeStruct((bsz * t, c_conv), BF16), jax.ShapeDtypeStruct((bsz * t, vv), BF16),
            jax.ShapeDtypeStruct((bsz, HIST_PAD, c_conv), F32),
            jax.ShapeDtypeStruct((bsz, heads, dk, dv), F32),
        ],
        scratch_shapes=[
            pltpu.VMEM((n_seq, seq_rows + HIST_PAD, c_conv), F32),
            pltpu.VMEM((7, seq_rows + HIST_PAD - 8, c_conv), F32),
            pltpu.VMEM((rows, c_conv), F32),
            pltpu.VMEM((n_seq, heads, dk, dv), F32),
        ],
        compiler_params=_params(("arbitrary", "arbitrary")),
        name="mixer",
    )(x2d, hist_pad, s0, g1, w_in, w_lr2, blr2, w_dw, b_dw, ln_g, ln_b, gn)


def _outproj_kernel(xp_ref, xs_ref, cp_ref, cs_ref, op_ref, os_ref, w_ref, g2_ref, wr_ref, br_ref,
                    x1_ref, dest_ref, gates_ref, pages_ref, counts_ref, sorted_ref,
                    stage, dest_vmem, dest_smem, cnt_s, page_s, npage_s, table_s, scatter_sems, dest_sems,
                    *, n_first, c_conv, n_groups, per_group, trash_row):
    i = pl.program_id(0)
    n_steps = pl.num_programs(0)
    tm = x1_ref.shape[0]
    n_exp = n_groups * per_group
    slot = i % 2
    prev = 1 - slot

    def dest_copy(s):
        return pltpu.make_async_copy(dest_vmem.at[s], dest_smem.at[s], dest_sems.at[s])

    def scatter_wait(s):
        for _ in range(2):
            pltpu.make_async_copy(stage.at[s], sorted_ref.at[pl.ds(0, tm)], scatter_sems.at[s]).wait()

    @pl.when(i == 0)
    def _():
        cnt_s[...] = jnp.zeros_like(cnt_s)
        page_s[...] = jnp.zeros_like(page_s)
        npage_s[...] = jnp.zeros_like(npage_s)
        table_s[...] = jnp.zeros_like(table_s)
        stage[1] = jnp.zeros(stage.shape[1:], stage.dtype)

        def fill(t, carry):
            dest_smem[1, 0, t] = trash_row + t
            dest_smem[1, 1, t] = trash_row + tm + t
            return carry

        lax.fori_loop(0, tm, fill, 0)

    @pl.when(i >= 1)
    def _():
        dest_copy(prev).wait()
        scatter_wait(slot)

    for t in range(tm):
        _row_copy(stage.at[prev], t, sorted_ref, dest_smem[prev, 0, t], scatter_sems.at[prev]).start()
        _row_copy(stage.at[prev], t, sorted_ref, dest_smem[prev, 1, t], scatter_sems.at[prev]).start()

    x = _pick(i, n_first, xp_ref, xs_ref)
    cc = _pick(i, n_first, cp_ref, cs_ref)
    oo = _pick(i, n_first, op_ref, os_ref)
    mix = jnp.dot(cc, w_ref[0:c_conv, :], preferred_element_type=F32)
    mix = mix + jnp.dot(oo, w_ref[c_conv:, :], preferred_element_type=F32)
    x1 = x + mix
    x1_ref[...] = x1
    h2 = _rms(x1, g2_ref[...])
    stage[slot] = _pack_pairs(h2)
    h_hi = h2.astype(BF16)
    h_lo = (h2 - h_hi.astype(F32)).astype(BF16)
    parts = lax.dot_general(wr_ref[...], jnp.concatenate([h_hi, h_lo], axis=0), (((1,), (1,)), ((), ())),
                            preferred_element_type=F32)
    n_r = br_ref.shape[0]
    logits = (parts[0:n_r, 0:tm] + parts[0:n_r, tm:] + parts[n_r:, 0:tm] + parts[n_r:, tm:]) + br_ref[...]
    lc = logits[0:n_groups, :]
    mc = jnp.max(lc, axis=0, keepdims=True)
    p_group = 1.0 / jnp.sum(jnp.exp(lc - mc), axis=0, keepdims=True)
    rows_c = lax.broadcasted_iota(I32, (n_groups, tm), 0)
    g_idx = jnp.min(jnp.where(lc == mc, rows_c, n_groups), axis=0, keepdims=True)
    lf = logits[n_groups:n_groups + n_exp, :]
    rows_f = lax.broadcasted_iota(I32, (n_exp, tm), 0)
    in_group = (rows_f >= g_idx * per_group) & (rows_f < (g_idx + 1) * per_group)
    neg = jnp.float32(-jnp.inf)
    l1 = jnp.where(in_group, lf, neg)
    m1 = jnp.max(l1, axis=0, keepdims=True)
    e1 = jnp.min(jnp.where(l1 == m1, rows_f, n_exp), axis=0, keepdims=True)
    l2 = jnp.where(rows_f == e1, neg, l1)
    m2 = jnp.max(l2, axis=0, keepdims=True)
    e2 = jnp.min(jnp.where(l2 == m2, rows_f, n_exp), axis=0, keepdims=True)
    r2 = jnp.exp(m2 - m1)
    w1 = 1.0 / (1.0 + r2)
    row8 = lax.broadcasted_iota(I32, (8, tm), 0)
    gates_ref[...] = jnp.where(row8 == 0, p_group * w1, jnp.where(row8 == 1, p_group * (r2 * w1), 0.0))

    oh0 = (rows_f == e1).astype(F32)
    oh1 = (rows_f == e2).astype(F32)
    both = oh0 + oh1
    tr = lax.broadcasted_iota(I32, (tm, tm), 0)
    tc = lax.broadcasted_iota(I32, (tm, tm), 1)
    earlier = jnp.dot(both.astype(BF16), (tr < tc).astype(BF16), preferred_element_type=F32)
    cnt = cnt_s[...]
    rank_base = earlier + cnt
    tile_cnt = jnp.sum(both, axis=1, keepdims=True)
    page_rows = float(EXPERT_ROWS)
    k0 = jnp.floor(cnt * (1.0 / page_rows))
    new_cnt = cnt + tile_cnt
    limit = (k0 + 1.0) * page_rows
    need_a = ((cnt == k0 * page_rows) & (tile_cnt > 0.0)).astype(F32)
    need_b = (new_cnt > limit).astype(F32)
    need = need_a + need_b
    er = lax.broadcasted_iota(I32, (n_exp, n_exp), 0)
    ec = lax.broadcasted_iota(I32, (n_exp, n_exp), 1)
    before = jnp.dot((ec < er).astype(BF16), jnp.broadcast_to(need, (n_exp, LANES)).astype(BF16),
                     preferred_element_type=F32)[:, 0:1]
    base = npage_s[...] + before
    page_a = jnp.where(need_a > 0.0, base, page_s[...])
    page_b = base + need_a
    npage_s[...] = npage_s[...] + jnp.sum(need, axis=0, keepdims=True)
    lane = lax.broadcasted_iota(I32, table_s.shape, 1).astype(F32)
    table = jnp.where((lane == k0) & (need_a > 0.0), page_a, table_s[...])
    table_s[...] = jnp.where((lane == k0 + 1.0) & (need_b > 0.0), page_b, table)
    cnt_s[...] = new_cnt
    page_s[...] = jnp.where(jnp.floor(new_cnt * (1.0 / page_rows)) == k0, page_a, page_b)

    def dest_rows(oh):
        rank = jnp.sum(oh * rank_base, axis=0, keepdims=True)
        lim = jnp.sum(oh * limit, axis=0, keepdims=True)
        pa = jnp.sum(oh * page_a, axis=0, keepdims=True)
        pb = jnp.sum(oh * page_b, axis=0, keepdims=True)
        within = rank - jnp.floor(rank * (1.0 / page_rows)) * page_rows
        return jnp.where(rank < lim, pa, pb) * page_rows + within

    dest = jnp.where(row8 == 0, dest_rows(oh0), jnp.where(row8 == 1, dest_rows(oh1), 0.0)).astype(I32)
    dest_ref[...] = dest
    dest_vmem[slot] = dest
    dest_copy(slot).start()

    @pl.when(i == n_steps - 1)
    def _():
        pages_ref[...] = table_s[...].astype(I32)
        counts_ref[...] = jnp.broadcast_to(cnt_s[...], counts_ref.shape).astype(I32)
        dest_copy(slot).wait()
        scatter_wait(prev)

        def last(j, carry):
            for r in range(DMA_UNROLL):
                t = j * DMA_UNROLL + r
                _row_copy(stage.at[slot], t, sorted_ref, dest_smem[slot, 0, t], scatter_sems.at[slot]).start()
                _row_copy(stage.at[slot], t, sorted_ref, dest_smem[slot, 1, t], scatter_sems.at[slot]).start()
            return carry

        lax.fori_loop(0, tm // DMA_UNROLL, last, 0)
        scatter_wait(slot)


def _outproj(xp, xs, cp, cs, op, os_, w_out, g2, wr, br, *, n_groups, per_group, n_pages):
    n_p, d = xp.shape
    n_all = n_p + xs.shape[0]
    n_first = n_p // ROW_TILE
    n_exp = n_groups * per_group
    c_conv = cp.shape[1]
    vv = op.shape[1]
    tile = (d // 2,)
    first, second = _split_maps(n_first)
    const = lambda i: (0, 0)
    row = lambda i: (i, 0)
    col = lambda i: (0, i)
    rows_sorted = n_pages * EXPERT_ROWS + 2 * ROW_TILE
    assert ROW_TILE <= EXPERT_ROWS, "a tile may open at most two pages per expert"
    return pl.pallas_call(
        functools.partial(_outproj_kernel, n_first=n_first, c_conv=c_conv, n_groups=n_groups, per_group=per_group,
                          trash_row=n_pages * EXPERT_ROWS),
        grid=(n_all // ROW_TILE,),
        in_specs=[
            pl.BlockSpec((ROW_TILE, d), first), pl.BlockSpec((ROW_TILE, d), second),
            pl.BlockSpec((ROW_TILE, c_conv), first), pl.BlockSpec((ROW_TILE, c_conv), second),
            pl.BlockSpec((ROW_TILE, vv), first), pl.BlockSpec((ROW_TILE, vv), second),
            pl.BlockSpec(w_out.shape, const, pipeline_mode=pl.Buffered(1)),
            pl.BlockSpec((1, d), const),
            pl.BlockSpec(wr.shape, const),
            pl.BlockSpec(br.shape, const),
        ],
        out_specs=[
            pl.BlockSpec((ROW_TILE, d), row),
            pl.BlockSpec((8, ROW_TILE), col), pl.BlockSpec((8, ROW_TILE), col),
            pl.BlockSpec((n_exp, LANES), const), pl.BlockSpec((n_exp, LANES), const),
            pl.BlockSpec(memory_space=pl.ANY),
        ],
        out_shape=[
            jax.ShapeDtypeStruct((n_all, d), F32),
            jax.ShapeDtypeStruct((8, n_all), I32), jax.ShapeDtypeStruct((8, n_all), F32),
            jax.ShapeDtypeStruct((n_exp, LANES), I32), jax.ShapeDtypeStruct((n_exp, LANES), I32),
            jax.ShapeDtypeStruct((rows_sorted,) + tile, U32),
        ],
        scratch_shapes=[
            pltpu.VMEM((2, ROW_TILE) + tile, U32),
            pltpu.VMEM((2, 8, ROW_TILE), I32), pltpu.SMEM((2, 8, ROW_TILE), I32),
            pltpu.VMEM((n_exp, 1), F32), pltpu.VMEM((n_exp, 1), F32), pltpu.VMEM((1, 1), F32),
            pltpu.VMEM((n_exp, LANES), F32),
            pltpu.SemaphoreType.DMA((2,)), pltpu.SemaphoreType.DMA((2,)),
        ],
        compiler_params=_params(("arbitrary",)),
        name="outproj",
    )(xp, xs, cp, cs, op, os_, w_out, g2, wr, br)


def _experts_kernel(cnt_ref, pages_ref, xs_ref, wg_ref, wu_ref, wd_ref, ysp_ref,
                    xbuf, ybuf, wg_f32, wu_f32, wd_f32, wg_bf, wu_bf, wd_bf, first_blk, page_seq,
                    gsems, ysems, wsems, *, n_exp, table_lanes):
    e = pl.program_id(0)
    tb = xbuf.shape[1]
    n_pages = page_seq.shape[0]

    def n_pages_of(ex):
        return (cnt_ref[ex] + (tb - 1)) // tb

    def page_rows(blk):
        return pl.ds(pl.multiple_of(page_seq[blk] * tb, tb), tb)

    def fetch(blk, slot):
        return pltpu.make_async_copy(xs_ref.at[page_rows(blk)], xbuf.at[slot], gsems.at[slot])

    def writeback(blk, slot):
        return pltpu.make_async_copy(ybuf.at[slot], ysp_ref.at[page_rows(blk)], ysems.at[slot])

    def weight_copies(ex, slot):
        return (pltpu.make_async_copy(wg_ref.at[ex], wg_f32.at[slot], wsems.at[slot]),
                pltpu.make_async_copy(wu_ref.at[ex], wu_f32.at[slot], wsems.at[slot]),
                pltpu.make_async_copy(wd_ref.at[ex], wd_f32.at[slot], wsems.at[slot]))

    @pl.when(e == 0)
    def _():
        for cp in weight_copies(0, 0):
            cp.start(priority=1)

        def per_expert(ex, blk):
            first_blk[ex] = blk

            def per_page(j, carry):
                page_seq[blk + j] = pages_ref[ex * table_lanes + j]
                return carry

            lax.fori_loop(0, n_pages_of(ex), per_page, 0)
            return blk + n_pages_of(ex)

        first_blk[n_exp] = lax.fori_loop(0, n_exp, per_expert, 0)
        fetch(0, 0).start()

    @pl.when(e + 1 < n_exp)
    def _():
        for cp in weight_copies(e + 1, (e + 1) % 2):
            cp.start(priority=1)

    b_lo = first_blk[e]
    b_hi = first_blk[e + 1]
    n_total = first_blk[n_exp]
    wslot = e % 2
    for cp in weight_copies(e, wslot):
        cp.wait()
    wg_bf[...] = wg_f32[wslot].astype(BF16)
    wu_bf[...] = wu_f32[wslot].astype(BF16)
    wd_bf[...] = wd_f32[wslot].astype(BF16)
    row_id = lax.broadcasted_iota(I32, (tb, 1), 0)

    def block(b, carry):
        slot = b % 2

        @pl.when(b >= 2)
        def _():
            writeback(b, slot).wait()

        fetch(b, slot).wait()
        fetch(jnp.minimum(b + 1, n_total - 1), 1 - slot).start()
        valid = cnt_ref[e] - (b - b_lo) * tb
        words = jnp.where(row_id < valid, xbuf[slot], jnp.uint32(0))
        hi, lo = _unpack_pairs(words)
        x = jnp.concatenate([hi.astype(BF16), lo.astype(BF16)], axis=1)
        hg = jnp.dot(x, wg_bf[...], preferred_element_type=F32)
        hu = jnp.dot(x, wu_bf[...], preferred_element_type=F32)
        hb = (_silu(hg) * hu).astype(BF16)
        ybuf[slot] = _pack_pairs(jnp.dot(hb, wd_bf[...], preferred_element_type=F32))
        writeback(b, slot).start()
        return carry

    lax.fori_loop(b_lo, b_hi, block, 0)

    @pl.when(e == n_exp - 1)
    def _():
        fetch(0, n_total % 2).wait()

        @pl.when(n_total >= 2)
        def _():
            writeback(0, n_total % 2).wait()

        writeback(0, (n_total + 1) % 2).wait()
        ybuf[0] = jnp.zeros(ybuf.shape[1:], ybuf.dtype)

        def spare(blk):
            return pltpu.make_async_copy(ybuf.at[0], ysp_ref.at[pl.ds(pl.multiple_of(blk * tb, tb), tb)], ysems.at[0])

        def zero(blk, carry):
            spare(blk).start()
            return carry

        lax.fori_loop(n_total, n_pages, zero, 0)

        def zero_wait(blk, carry):
            spare(0).wait()
            return carry

        lax.fori_loop(n_total, n_pages, zero_wait, 0)


def _experts(counts, pages_flat, xs_sorted, w_gate, w_up, w_down, *, n_pages, table_lanes):
    tile = xs_sorted.shape[1:]
    n_exp, d, ff = w_gate.shape
    anyspec = pl.BlockSpec(memory_space=pl.ANY)
    grid_spec = pltpu.PrefetchScalarGridSpec(
        num_scalar_prefetch=2,
        grid=(n_exp,),
        in_specs=[anyspec, anyspec, anyspec, anyspec],
        out_specs=anyspec,
        scratch_shapes=[
            pltpu.VMEM((2, EXPERT_ROWS) + tile, U32), pltpu.VMEM((2, EXPERT_ROWS) + tile, U32),
            pltpu.VMEM((2, d, ff), F32), pltpu.VMEM((2, d, ff), F32), pltpu.VMEM((2, ff, d), F32),
            pltpu.VMEM((d, ff), BF16), pltpu.VMEM((d, ff), BF16), pltpu.VMEM((ff, d), BF16),
            pltpu.SMEM((n_exp + 1,), I32), pltpu.SMEM((n_pages,), I32),
            pltpu.SemaphoreType.DMA((2,)), pltpu.SemaphoreType.DMA((2,)), pltpu.SemaphoreType.DMA((2,)),
        ],
    )
    return pl.pallas_call(
        functools.partial(_experts_kernel, n_exp=n_exp, table_lanes=table_lanes),
        grid_spec=grid_spec,
        out_shape=jax.ShapeDtypeStruct((n_pages * EXPERT_ROWS,) + tile, U32),
        compiler_params=_params(("arbitrary",)),
        name="experts",
    )(counts, pages_flat, xs_sorted, w_gate, w_up, w_down)


def _combine_kernel(dest_ref, dest_next_ref, gates_ref, x1_ref, ysp_ref, gf_ref, yp_ref, ysmp_ref,
                    buf0, buf1, sems, *, n_first):
    i = pl.program_id(0)
    n = pl.num_programs(0)
    tm = x1_ref.shape[0]
    slot = i % 2

    def gather(d_ref, s):
        def body(j, carry):
            for r in range(DMA_UNROLL):
                t = j * DMA_UNROLL + r
                _row_copy(ysp_ref, d_ref[0, t], buf0.at[s], t, sems.at[s]).start(priority=0)
                _row_copy(ysp_ref, d_ref[1, t], buf1.at[s], t, sems.at[s]).start(priority=1)
            return carry

        lax.fori_loop(0, tm // DMA_UNROLL, body, 0)

    @pl.when(i == 0)
    def _():
        gather(dest_ref, 0)

    @pl.when(i + 1 < n)
    def _():
        gather(dest_next_ref, 1 - slot)

    pltpu.make_async_copy(ysp_ref.at[pl.ds(0, tm)], buf0.at[slot], sems.at[slot]).wait()
    pltpu.make_async_copy(ysp_ref.at[pl.ds(0, tm)], buf1.at[slot], sems.at[slot]).wait()
    hi0, lo0 = _unpack_pairs(buf0[slot])
    hi1, lo1 = _unpack_pairs(buf1[slot])
    g0 = gates_ref[:, 0:1]
    g1 = gates_ref[:, 1:2]
    moe = jnp.concatenate([g0 * hi0 + g1 * hi1, g0 * lo0 + g1 * lo1], axis=1)
    y = _rms(x1_ref[...] + moe, gf_ref[...])

    @pl.when(i < n_first)
    def _():
        yp_ref[...] = y

    @pl.when(i >= n_first)
    def _():
        ysmp_ref[...] = y


def _combine(dest, gates_t, x1, ysp, gf, *, n_p):
    n_all, d = x1.shape
    tile = ysp.shape[1:]
    n_first = n_p // ROW_TILE
    n_tiles = n_all // ROW_TILE
    first, second = _split_maps(n_first)
    return pl.pallas_call(
        functools.partial(_combine_kernel, n_first=n_first),
        grid=(n_tiles,),
        in_specs=[
            pl.BlockSpec((8, ROW_TILE), lambda i: (0, i), memory_space=pltpu.SMEM),
            pl.BlockSpec((8, ROW_TILE), lambda i: (0, jnp.minimum(i + 1, n_tiles - 1)), memory_space=pltpu.SMEM),
            pl.BlockSpec((ROW_TILE, 8), lambda i: (i, 0)),
            pl.BlockSpec((ROW_TILE, d), lambda i: (i, 0)),
            pl.BlockSpec(memory_space=pl.ANY),
            pl.BlockSpec((1, d), lambda i: (0, 0)),
        ],
        out_specs=[pl.BlockSpec((ROW_TILE, d), first), pl.BlockSpec((ROW_TILE, d), second)],
        out_shape=[jax.ShapeDtypeStruct((n_p, d), F32), jax.ShapeDtypeStruct((n_all - n_p, d), F32)],
        scratch_shapes=[pltpu.VMEM((2, ROW_TILE) + tile, U32), pltpu.VMEM((2, ROW_TILE) + tile, U32),
                        pltpu.SemaphoreType.DMA((2,))],
        compiler_params=_params(("arbitrary",)),
        name="combine",
    )(dest, dest, gates_t, x1, ysp, gf)


def _chunk_for(t):
    return 64 if t % 64 == 0 else t


def kernel(x_prompt, x_sample, cache_conv, state_gla, norm1_g, w_in, w_lr2, b_lr2, w_dw, b_dw, conv_ln_g, conv_ln_b, gla_norm_g, w_out, norm2_g, w_router_coarse, b_router_coarse, w_router_fine, b_router_fine, w_exp_gate, w_exp_up, w_exp_down, norm_f_g):
    assert norm1_g.shape[0] == 1, "single trunk layer"
    bp, tp, d = x_prompt.shape
    bs, ts, _ = x_sample.shape
    heads, dk, dv = state_gla.shape[2:]
    c_conv = w_dw.shape[2]
    width = w_dw.shape[1]
    rank = w_lr2.shape[1]
    qk, vv = heads * dk, heads * dv
    n_groups, _, per_group = w_router_fine.shape[1:]
    n_exp = n_groups * per_group
    n_p, n_s = bp * tp, bs * ts
    n_s_pad = -(-n_s // ROW_TILE) * ROW_TILE
    pad_rows = lambda a: jnp.pad(a, ((0, n_s_pad - n_s), (0, 0)))
    n_all = n_p + n_s_pad
    assert n_p % ROW_TILE == 0 and width - 1 <= HIST_PAD

    xp = x_prompt.reshape(n_p, d)
    xs = pad_rows(x_sample.reshape(n_s, d))
    row = lambda a: a.reshape(1, -1)

    mixer_args = (row(norm1_g[0]), w_in[0].astype(BF16), w_lr2[0], row(b_lr2[0]),
                  w_dw[0], row(b_dw[0]), row(conv_ln_g[0]), row(conv_ln_b[0]), row(gla_norm_g[0]))
    hist_p = jnp.zeros((bp, HIST_PAD, c_conv), F32)
    hist_s = jnp.pad(cache_conv[0], ((0, 0), (HIST_PAD - (width - 1), 0), (0, 0)))
    s0_p = jnp.zeros((bp, heads, dk, dv), F32)
    c_p, o_p, tail_p, gla_p = _mixer(xp, hist_p, s0_p, *mixer_args, n_seq=1, seq_rows=ROW_TILE)
    c_s, o_s, tail_s, gla_s = _mixer(x_sample.reshape(n_s, d), hist_s, state_gla[0], *mixer_args,
                                     n_seq=bs, seq_rows=ts)
    c_s, o_s = pad_rows(c_s), pad_rows(o_s)

    wr = jnp.concatenate([w_router_coarse[0].T,
                          jnp.transpose(w_router_fine[0], (0, 2, 1)).reshape(n_exp, d)], axis=0)
    br = jnp.concatenate([b_router_coarse[0], b_router_fine[0].reshape(n_exp)])
    r_rows = -(-(n_groups + n_exp) // 8) * 8
    wr = jnp.pad(wr, ((0, r_rows - wr.shape[0]), (0, 0)))
    br = jnp.pad(br, (0, r_rows - br.shape[0])).reshape(r_rows, 1)
    wr_hi = wr.astype(BF16)
    wr = jnp.concatenate([wr_hi, (wr - wr_hi.astype(F32)).astype(BF16)], axis=0)
    n_pages = (2 * n_all) // EXPERT_ROWS + n_exp
    assert n_all // EXPERT_ROWS + 2 <= LANES, "page table row must hold one expert's pages"
    x1, dest, gates, pages, counts, xs_sorted = _outproj(
        xp, xs, c_p, c_s, o_p, o_s, w_out[0].astype(BF16), row(norm2_g[0]), wr, br,
        n_groups=n_groups, per_group=per_group, n_pages=n_pages)
    ysp = _experts(counts[:, 0], pages.reshape(-1), xs_sorted, w_exp_gate[0], w_exp_up[0], w_exp_down[0],
                   n_pages=n_pages, table_lanes=LANES)
    y_p, y_s = _combine(dest, gates.T, x1, ysp, row(norm_f_g), n_p=n_p)

    lead = HIST_PAD - (width - 1)
    return (y_p.reshape(bp, tp, d), y_s[:n_s].reshape(bs, ts, d), tail_p[:, lead:][None], gla_p[None],
            tail_s[:, lead:][None], gla_s[None])
```

```python
import functools

import jax
import jax.numpy as jnp
from jax import lax
from jax.experimental import pallas as pl
from jax.experimental.pallas import tpu as pltpu

F32 = jnp.float32
BF16 = jnp.bfloat16
I32 = jnp.int32
U32 = jnp.uint32
EPS = 1e-6
GATE_TEMP = 16.0
HIGHEST = lax.Precision.HIGHEST

LANES = 128
ROW_TILE = 256
EXPERT_ROWS = 256
HIST_PAD = 32
DMA_UNROLL = 8
VMEM_LIMIT = 56 * 1024 * 1024


def _params(semantics, vmem=VMEM_LIMIT):
    return pltpu.CompilerParams(dimension_semantics=semantics, vmem_limit_bytes=vmem)


def _rms(x, g):
    return x * lax.rsqrt(jnp.mean(x * x, axis=-1, keepdims=True) + EPS) * g


def _silu(x):
    return x * jax.nn.sigmoid(x)


def _log_sigmoid(z):
    return jnp.minimum(z, 0.0) - jnp.log(1.0 + jnp.exp(-jnp.abs(z)))


def _pick(i, n_first, first_ref, second_ref):
    return jnp.where(i < n_first, first_ref[...], second_ref[...])


def _split_maps(n_first):
    first = lambda i: (jnp.minimum(i, n_first - 1), 0)
    second = lambda i: (jnp.maximum(i - n_first, 0), 0)
    return first, second


def _pack_pairs(x):
    half = x.shape[1] // 2
    hi = lax.bitcast_convert_type(x[:, :half].astype(BF16).astype(F32), U32)
    lo = lax.bitcast_convert_type(x[:, half:].astype(BF16).astype(F32), U32)
    return hi | (lo >> 16)


def _unpack_pairs(p):
    hi = lax.bitcast_convert_type(p & jnp.uint32(0xFFFF0000), F32)
    lo = lax.bitcast_convert_type(p << 16, F32)
    return hi, lo


def _row_copy(src, s, dst, d, sem):
    return pltpu.make_async_copy(src.at[pl.ds(s, 1)], dst.at[pl.ds(d, 1)], sem)


def _inproj_kernel(xp_ref, xs_ref, g1_ref, w_ref, wlr2_ref, blr2_ref,
                   u_ref, q_ref, k_ref, v_ref, g_ref, la_ref, *, n_first, c_conv, qk, vv, dk):
    i = pl.program_id(0)
    x = _pick(i, n_first, xp_ref, xs_ref)
    h = _rms(x, g1_ref[...]).astype(BF16)

    def mm(lo, width):
        return jnp.dot(h, w_ref[:, lo:lo + width], preferred_element_type=F32)

    a = mm(0, c_conv)
    a_gate = mm(c_conv, c_conv)
    u_ref[...] = a * jax.nn.sigmoid(a_gate)
    off = 2 * c_conv
    q_ref[...] = mm(off, qk) * (dk ** -0.5)
    k_ref[...] = mm(off + qk, qk)
    v_ref[...] = mm(off + 2 * qk, vv)
    g_ref[...] = mm(off + 2 * qk + vv, vv)
    lr = mm(off + 2 * qk + 2 * vv, LANES)
    z = jnp.dot(lr, wlr2_ref[...], precision=HIGHEST, preferred_element_type=F32) + blr2_ref[...]
    la_ref[...] = _log_sigmoid(z) * (1.0 / GATE_TEMP)


def _inproj(xp, xs, g1, w_pad, wlr2_pad, blr2, *, c_conv, qk, vv, dk):
    n_p, d = xp.shape
    n_s = xs.shape[0]
    n_all = n_p + n_s
    n_first = n_p // ROW_TILE
    grid = (n_all // ROW_TILE,)
    first, second = _split_maps(n_first)
    const = lambda i: (0, 0)
    row = lambda i: (i, 0)
    widths = (c_conv, qk, qk, vv, vv, qk)
    return pl.pallas_call(
        functools.partial(_inproj_kernel, n_first=n_first, c_conv=c_conv, qk=qk, vv=vv, dk=dk),
        grid=grid,
        in_specs=[
            pl.BlockSpec((ROW_TILE, d), first),
            pl.BlockSpec((ROW_TILE, d), second),
            pl.BlockSpec((1, d), const),
            pl.BlockSpec(w_pad.shape, const, pipeline_mode=pl.Buffered(1)),
            pl.BlockSpec(wlr2_pad.shape, const),
            pl.BlockSpec((1, qk), const),
        ],
        out_specs=[pl.BlockSpec((ROW_TILE, w), row) for w in widths],
        out_shape=[jax.ShapeDtypeStruct((n_all, w), F32) for w in widths],
        compiler_params=_params(("arbitrary",)),
        name="inproj",
    )(xp, xs, g1, w_pad, wlr2_pad, blr2)


def _conv_kernel(u_ref, hist_ref, w_ref, b_ref, lg_ref, lb_ref, c_ref, win, cbuf, *, tt, width):
    i = pl.program_id(1)

    @pl.when(i == 0)
    def _():
        win[0:HIST_PAD, :] = hist_ref[0]

    @pl.when(i > 0)
    def _():
        win[0:HIST_PAD, :] = win[tt:tt + HIST_PAD, :]

    win[HIST_PAD:HIST_PAD + tt, :] = u_ref[...]
    lead = HIST_PAD - (width - 1)
    n_ch = u_ref.shape[1]
    for cb in range(n_ch // LANES):
        cs = slice(cb * LANES, (cb + 1) * LANES)
        acc = jnp.broadcast_to(b_ref[:, cs], (tt, LANES))
        for j in range(width):
            acc = acc + w_ref[j:j + 1, cs] * win[lead + j:lead + j + tt, cs]
        cbuf[:, cs] = acc
    c = cbuf[...]
    mu = jnp.mean(c, axis=-1, keepdims=True)
    xc = c - mu
    y = xc * lax.rsqrt(jnp.mean(xc * xc, axis=-1, keepdims=True) + EPS) * lg_ref[...] + lb_ref[...]
    c_ref[...] = _silu(y).astype(c_ref.dtype)


def _conv(u_all, hist_pad, w_dw, b_dw, ln_g, ln_b, *, row0, bsz, t, tt):
    n_ch = u_all.shape[1]
    width = w_dw.shape[0]
    nt = t // tt
    blk0 = row0 // tt
    const = lambda b, i: (0, 0)
    return pl.pallas_call(
        functools.partial(_conv_kernel, tt=tt, width=width),
        grid=(bsz, nt),
        in_specs=[
            pl.BlockSpec((tt, n_ch), lambda b, i: (blk0 + b * nt + i, 0)),
            pl.BlockSpec((1, HIST_PAD, n_ch), lambda b, i: (b, 0, 0)),
            pl.BlockSpec(w_dw.shape, const),
            pl.BlockSpec((1, n_ch), const),
            pl.BlockSpec((1, n_ch), const),
            pl.BlockSpec((1, n_ch), const),
        ],
        out_specs=pl.BlockSpec((tt, n_ch), lambda b, i: (b * nt + i, 0)),
        out_shape=jax.ShapeDtypeStruct((bsz * t, n_ch), BF16),
        scratch_shapes=[pltpu.VMEM((tt + HIST_PAD, n_ch), F32), pltpu.VMEM((tt, n_ch), F32)],
        compiler_params=_params(("arbitrary", "arbitrary")),
        name="conv",
    )(u_all, hist_pad, w_dw, b_dw, ln_g, ln_b)


def _gla_kernel(q_ref, k_ref, v_ref, g_ref, la_ref, s0_ref, gn_ref, o_ref, sout_ref, state,
                *, chunk, heads, dk, dv):
    i = pl.program_id(1)

    @pl.when(i == 0)
    def _():
        state[...] = s0_ref[0]

    la = la_ref[...]
    r = lax.broadcasted_iota(I32, (chunk, chunk), 0)
    c = lax.broadcasted_iota(I32, (chunk, chunk), 1)
    causal = c <= r
    b = jnp.dot(causal.astype(F32), la, precision=HIGHEST, preferred_element_type=F32)
    b_end = b[chunk - 1:chunk, :]
    q_in = (q_ref[...] * jnp.exp(b)).astype(BF16)
    k_in = (k_ref[...] * jnp.exp(-b)).astype(BF16)
    k_out = (k_ref[...] * jnp.exp(b_end - b)).astype(BF16)
    decay_row = jnp.exp(b_end)
    eye = lax.broadcasted_iota(I32, (dk, dk), 0) == lax.broadcasted_iota(I32, (dk, dk), 1)
    for h in range(heads):
        ks = slice(h * dk, (h + 1) * dk)
        vs = slice(h * dv, (h + 1) * dv)
        vh = v_ref[:, vs].astype(BF16)
        att = lax.dot_general(q_in[:, ks], k_in[:, ks], (((1,), (1,)), ((), ())), preferred_element_type=F32)
        att = jnp.where(causal, att, 0.0).astype(BF16)
        s_h = state[h]
        o = jnp.dot(att, vh, preferred_element_type=F32)
        o = o + jnp.dot(q_in[:, ks], s_h.astype(BF16), preferred_element_type=F32)
        decay_col = jnp.sum(jnp.where(eye, jnp.broadcast_to(decay_row[:, ks], (dk, dk)), 0.0), axis=1, keepdims=True)
        state[h] = decay_col * s_h + lax.dot_general(k_out[:, ks], vh, (((0,), (0,)), ((), ())),
                                                     preferred_element_type=F32)
        o = o * lax.rsqrt(jnp.mean(o * o, axis=-1, keepdims=True) + EPS) * gn_ref[...]
        o_ref[:, vs] = (o * _silu(g_ref[:, vs])).astype(o_ref.dtype)

    @pl.when(i == pl.num_programs(1) - 1)
    def _():
        sout_ref[0] = state[...]


def _gla(q_all, k_all, v_all, g_all, la_all, s0, gn, *, row0, bsz, t, chunk):
    heads, dk, dv = s0.shape[1:]
    nt = t // chunk
    blk0 = row0 // chunk
    rows = lambda b, i: (blk0 + b * nt + i, 0)
    return pl.pallas_call(
        functools.partial(_gla_kernel, chunk=chunk, heads=heads, dk=dk, dv=dv),
        grid=(bsz, nt),
        in_specs=[
            pl.BlockSpec((chunk, heads * dk), rows),
            pl.BlockSpec((chunk, heads * dk), rows),
            pl.BlockSpec((chunk, heads * dv), rows),
            pl.BlockSpec((chunk, heads * dv), rows),
            pl.BlockSpec((chunk, heads * dk), rows),
            pl.BlockSpec((1, heads, dk, dv), lambda b, i: (b, 0, 0, 0)),
            pl.BlockSpec((1, dv), lambda b, i: (0, 0)),
        ],
        out_specs=[
            pl.BlockSpec((chunk, heads * dv), lambda b, i: (b * nt + i, 0)),
            pl.BlockSpec((1, heads, dk, dv), lambda b, i: (b, 0, 0, 0)),
        ],
        out_shape=[
            jax.ShapeDtypeStruct((bsz * t, heads * dv), BF16),
            jax.ShapeDtypeStruct((bsz, heads, dk, dv), F32),
        ],
        scratch_shapes=[pltpu.VMEM((heads, dk, dv), F32)],
        compiler_params=_params(("arbitrary", "arbitrary")),
        name="gla",
    )(q_all, k_all, v_all, g_all, la_all, s0, gn)


def _mixer_kernel(x_ref, hist_ref, s0_ref, g1_ref, w_ref, wlr2_ref, blr2_ref,
                  wdw_ref, bdw_ref, lg_ref, lb_ref, gn_ref,
                  c_ref, o_ref, tail_ref, sout_ref, pu, pq, pk, pv, pg, pla, win, shifted, cbuf, state,
                  *, n_seq, seq_rows, chunk, heads, dk, dv, c_conv, width):
    i = pl.program_id(1)
    qk, vv = heads * dk, heads * dv

    @pl.when(i == 0)
    def _():
        for buf in (pu, pq, pk, pv, pg, pla):
            buf[...] = jnp.zeros_like(buf)
        win[...] = jnp.zeros_like(win)
        state[...] = jnp.zeros_like(state)

    @pl.when(i == 1)
    def _():
        for s in range(n_seq):
            win[s, 0:HIST_PAD, :] = hist_ref[s]
        state[...] = s0_ref[...]

    @pl.when(i > 1)
    def _():
        for s in range(n_seq):
            win[s, 0:HIST_PAD, :] = win[s, seq_rows:seq_rows + HIST_PAD, :]

    h = _rms(x_ref[...], g1_ref[...]).astype(BF16)

    def mm(lo, n):
        return jnp.dot(h, w_ref[:, lo:lo + n], preferred_element_type=F32)

    off = 2 * c_conv
    new = {}

    def proj_u(j, n):
        lo = j * n
        new["u", j] = mm(lo, n) * jax.nn.sigmoid(mm(c_conv + lo, n))

    def proj(name, lo, n, scale=None, dtype=F32):
        val = mm(lo, n)
        new[name] = (val if scale is None else val * scale).astype(dtype)

    def proj_la():
        lr = mm(off + 2 * qk + 2 * vv, wlr2_ref.shape[0])
        z = jnp.dot(lr, wlr2_ref[...], precision=HIGHEST, preferred_element_type=F32) + blr2_ref[...]
        new["la"] = _log_sigmoid(z) * (1.0 / GATE_TEMP)

    half_c, half_v = c_conv // 2, vv // 2
    stage1 = [
        functools.partial(proj_u, 0, half_c), functools.partial(proj_u, 1, half_c),
        functools.partial(proj, "q", off, qk, dk ** -0.5), functools.partial(proj, "k", off + qk, qk),
        functools.partial(proj, ("v", 0), off + 2 * qk, half_v, None, BF16),
        functools.partial(proj, ("v", 1), off + 2 * qk + half_v, half_v, None, BF16),
        functools.partial(proj, ("g", 0), off + 2 * qk + vv, half_v),
        functools.partial(proj, ("g", 1), off + 2 * qk + vv + half_v, half_v),
        proj_la,
    ]

    lead = HIST_PAD - (width - 1)
    span = shifted.shape[1]

    def conv_fill(s):
        r0 = s * seq_rows
        win[s, HIST_PAD:HIST_PAD + seq_rows, :] = pu[r0:r0 + seq_rows, :]

    def conv_block(s, cb):
        r0 = s * seq_rows
        cs = slice(cb * LANES, (cb + 1) * LANES)
        for r in range(1, 8):
            shifted[r - 1] = win[s, r:r + span, cs]
        for t0 in range(0, seq_rows, chunk):
            acc = jnp.broadcast_to(bdw_ref[:, cs], (chunk, LANES))
            for j in range(width):
                r, a8 = (lead + j) % 8, (lead + j) // 8 * 8
                if r == 0:
                    tap = win[s, t0 + a8:t0 + a8 + chunk, cs]
                else:
                    tap = shifted[r - 1, t0 + a8:t0 + a8 + chunk, :]
                acc = acc + wdw_ref[j:j + 1, cs] * tap
            cbuf[r0 + t0:r0 + t0 + chunk, cs] = acc

    def conv_norm():
        cv = cbuf[...]
        mu = jnp.mean(cv, axis=-1, keepdims=True)
        xc = cv - mu
        cn = xc * lax.rsqrt(jnp.mean(xc * xc, axis=-1, keepdims=True) + EPS) * lg_ref[...] + lb_ref[...]
        c_ref[...] = _silu(cn).astype(c_ref.dtype)

    r = lax.broadcasted_iota(I32, (chunk, chunk), 0)
    c = lax.broadcasted_iota(I32, (chunk, chunk), 1)
    causal = c <= r
    tril = causal.astype(F32)
    eye = lax.broadcasted_iota(I32, (dk, dk), 0) == lax.broadcasted_iota(I32, (dk, dk), 1)
    gla = {}

    def gla_prep(s, t0):
        rows = slice(s * seq_rows + t0, s * seq_rows + t0 + chunk)
        b = jnp.dot(tril, pla[rows, :], precision=HIGHEST, preferred_element_type=F32)
        b_end = b[chunk - 1:chunk, :]
        k_c = pk[rows, :]
        gla[s, t0] = ((pq[rows, :] * jnp.exp(b)).astype(BF16), (k_c * jnp.exp(-b)).astype(BF16),
                      (k_c * jnp.exp(b_end - b)).astype(BF16), jnp.exp(b_end))

    def gla_head(s, t0, hd):
        rows = slice(s * seq_rows + t0, s * seq_rows + t0 + chunk)
        q_in, k_in, k_out, decay_row = gla[s, t0]
        ks = slice(hd * dk, (hd + 1) * dk)
        vs = slice(hd * dv, (hd + 1) * dv)
        vh = pv[rows, vs]
        att = lax.dot_general(q_in[:, ks], k_in[:, ks], (((1,), (1,)), ((), ())), preferred_element_type=F32)
        att = jnp.where(causal, att, 0.0).astype(BF16)
        s_h = state[s, hd]
        o = jnp.dot(att, vh, preferred_element_type=F32)
        o = o + jnp.dot(q_in[:, ks], s_h.astype(BF16), preferred_element_type=F32)
        decay_col = jnp.sum(jnp.where(eye, jnp.broadcast_to(decay_row[:, ks], (dk, dk)), 0.0), axis=1, keepdims=True)
        state[s, hd] = decay_col * s_h + lax.dot_general(k_out[:, ks], vh, (((0,), (0,)), ((), ())),
                                                         preferred_element_type=F32)
        o = o * lax.rsqrt(jnp.mean(o * o, axis=-1, keepdims=True) + EPS) * gn_ref[...]
        o_ref[rows, vs] = (o * _silu(pg[rows, vs])).astype(o_ref.dtype)

    stage2 = []
    for s in range(n_seq):
        stage2.append(functools.partial(conv_fill, s))
        stage2 += [functools.partial(conv_block, s, cb) for cb in range(c_conv // LANES)]
    stage2.append(conv_norm)
    for s in range(n_seq):
        for t0 in range(0, seq_rows, chunk):
            stage2.append(functools.partial(gla_prep, s, t0))
            stage2 += [functools.partial(gla_head, s, t0, hd) for hd in range(heads)]

    per = -(-len(stage2) // len(stage1))
    for n, piece in enumerate(stage1):
        piece()
        for other in stage2[n * per:(n + 1) * per]:
            other()

    pu[...] = jnp.concatenate([new["u", 0], new["u", 1]], axis=1)
    pq[...] = new["q"]
    pk[...] = new["k"]
    pv[...] = jnp.concatenate([new["v", 0], new["v", 1]], axis=1)
    pg[...] = jnp.concatenate([new["g", 0], new["g", 1]], axis=1)
    pla[...] = new["la"]

    @pl.when(i == pl.num_programs(1) - 1)
    def _():
        for s in range(n_seq):
            tail_ref[s] = win[s, seq_rows:seq_rows + HIST_PAD, :]
        sout_ref[...] = state[...]


def _mixer(x2d, hist_pad, s0, g1, w_in, w_lr2, blr2, w_dw, b_dw, ln_g, ln_b, gn, *, n_seq, seq_rows):
    bsz, heads, dk, dv = s0.shape
    d = x2d.shape[1]
    t = x2d.shape[0] // bsz
    c_conv = w_dw.shape[1]
    width = w_dw.shape[0]
    qk, vv = heads * dk, heads * dv
    assert (n_seq == 1 and t % seq_rows == 0) or (seq_rows == t and bsz % n_seq == 0)
    nt = t // seq_rows
    rows = n_seq * seq_rows
    span = seq_rows + HIST_PAD - 8
    chunk = 64 if seq_rows % 64 == 0 else seq_rows
    const = lambda b, i: (0, 0)
    tile_in = lambda b, i: (b * nt + jnp.minimum(i, nt - 1), 0)
    tile_out = lambda b, i: (b * nt + jnp.maximum(i - 1, 0), 0)
    per_seq3 = lambda b, i: (b, 0, 0)
    per_seq4 = lambda b, i: (b, 0, 0, 0)
    return pl.pallas_call(
        functools.partial(_mixer_kernel, n_seq=n_seq, seq_rows=seq_rows, chunk=chunk, heads=heads, dk=dk, dv=dv,
                          c_conv=c_conv, width=width),
        grid=(bsz // n_seq, nt + 1),
        in_specs=[
            pl.BlockSpec((rows, d), tile_in),
            pl.BlockSpec((n_seq, HIST_PAD, c_conv), per_seq3),
            pl.BlockSpec((n_seq, heads, dk, dv), per_seq4),
            pl.BlockSpec((1, d), const),
            pl.BlockSpec(w_in.shape, const, pipeline_mode=pl.Buffered(1)),
            pl.BlockSpec(w_lr2.shape, const),
            pl.BlockSpec((1, qk), const),
            pl.BlockSpec(w_dw.shape, const),
            pl.BlockSpec((1, c_conv), const), pl.BlockSpec((1, c_conv), const), pl.BlockSpec((1, c_conv), const),
            pl.BlockSpec((1, dv), const),
        ],
        out_specs=[
            pl.BlockSpec((rows, c_conv), tile_out), pl.BlockSpec((rows, vv), tile_out),
            pl.BlockSpec((n_seq, HIST_PAD, c_conv), per_seq3),
            pl.BlockSpec((n_seq, heads, dk, dv), per_seq4),
        ],
        out_shape=[
            jax.ShapeDtypeStruct((bsz * t, c_conv), BF16), jax.ShapeDtypeStruct((bsz * t, vv), BF16),
            jax.ShapeDtypeStruct((bsz, HIST_PAD, c_conv), F32),
            jax.ShapeDtypeStruct((bsz, heads, dk, dv), F32),
        ],
        scratch_shapes=[
            pltpu.VMEM((rows, c_conv), F32), pltpu.VMEM((rows, qk), F32), pltpu.VMEM((rows, qk), F32),
            pltpu.VMEM((rows, vv), BF16), pltpu.VMEM((rows, vv), F32), pltpu.VMEM((rows, qk), F32),
            pltpu.VMEM((n_seq, seq_rows + HIST_PAD, c_conv), F32),
            pltpu.VMEM((7, span, LANES), F32),
            pltpu.VMEM((rows, c_conv), F32),
            pltpu.VMEM((n_seq, heads, dk, dv), F32),
        ],
        compiler_params=_params(("arbitrary", "arbitrary")),
        name="mixer",
    )(x2d, hist_pad, s0, g1, w_in, w_lr2, blr2, w_dw, b_dw, ln_g, ln_b, gn)


GLA_SUB = 16


def _front_kernel(x_ref, hist_ref, s0_ref, g1_ref, w_ref, wlr2_ref, blr2_ref,
                  wdw_ref, bdw_ref, lg_ref, lb_ref, gn_ref,
                  c_ref, o_ref, tail_ref, sout_ref, win, shifted, cbuf, state,
                  *, n_seq, seq_rows, chunk, heads, dk, dv, c_conv, width):
    i = pl.program_id(1)
    qk, vv = heads * dk, heads * dv

    @pl.when(i == 0)
    def _():
        for s in range(n_seq):
            win[s, 0:HIST_PAD, :] = hist_ref[s]
        state[...] = s0_ref[...]

    @pl.when(i > 0)
    def _():
        for s in range(n_seq):
            win[s, 0:HIST_PAD, :] = win[s, seq_rows:seq_rows + HIST_PAD, :]

    h = _rms(x_ref[...], g1_ref[...]).astype(BF16)

    def mm(lo, n):
        return jnp.dot(h, w_ref[:, lo:lo + n], preferred_element_type=F32)

    u = mm(0, c_conv) * jax.nn.sigmoid(mm(c_conv, c_conv))
    lead = HIST_PAD - (width - 1)
    span = shifted.shape[1]
    for s in range(n_seq):
        r0 = s * seq_rows
        win[s, HIST_PAD:HIST_PAD + seq_rows, :] = u[r0:r0 + seq_rows, :]
        for cb in range(c_conv // LANES):
            cs = slice(cb * LANES, (cb + 1) * LANES)
            for r in range(1, 8):
                shifted[r - 1] = win[s, r:r + span, cs]
            for t0 in range(0, seq_rows, chunk):
                acc = jnp.broadcast_to(bdw_ref[:, cs], (chunk, LANES))
                for j in range(width):
                    r, a8 = (lead + j) % 8, (lead + j) // 8 * 8
                    if r == 0:
                        tap = win[s, t0 + a8:t0 + a8 + chunk, cs]
                    else:
                        tap = shifted[r - 1, t0 + a8:t0 + a8 + chunk, :]
                    acc = acc + wdw_ref[j:j + 1, cs] * tap
                cbuf[r0 + t0:r0 + t0 + chunk, cs] = acc
    cv = cbuf[...]
    mu = jnp.mean(cv, axis=-1, keepdims=True)
    xc = cv - mu
    cn = xc * lax.rsqrt(jnp.mean(xc * xc, axis=-1, keepdims=True) + EPS) * lg_ref[...] + lb_ref[...]
    c_ref[...] = _silu(cn).astype(c_ref.dtype)

    off = 2 * c_conv
    q = mm(off, qk) * (dk ** -0.5)
    k = mm(off + qk, qk)
    v = mm(off + 2 * qk, vv).astype(BF16)
    g = mm(off + 2 * qk + vv, vv)
    lr = mm(off + 2 * qk + 2 * vv, wlr2_ref.shape[0])
    z = jnp.dot(lr, wlr2_ref[...], precision=HIGHEST, preferred_element_type=F32) + blr2_ref[...]
    la = _log_sigmoid(z) * (1.0 / GATE_TEMP)

    n_sub = chunk // GLA_SUB
    r = lax.broadcasted_iota(I32, (chunk, chunk), 0)
    c = lax.broadcasted_iota(I32, (chunk, chunk), 1)
    causal = c <= r
    local_sum = (causal & ((r // GLA_SUB) == (c // GLA_SUB))).astype(BF16)
    sub_rows = [slice(j * GLA_SUB, (j + 1) * GLA_SUB) for j in range(n_sub)]
    eye = lax.broadcasted_iota(I32, (dk, dk), 0) == lax.broadcasted_iota(I32, (dk, dk), 1)
    for s in range(n_seq):
        for t0 in range(0, seq_rows, chunk):
            rows = slice(s * seq_rows + t0, s * seq_rows + t0 + chunk)
            la_1 = la[rows, :].astype(BF16)
            rest = la[rows, :] - la_1.astype(F32)
            la_2 = rest.astype(BF16)
            la_3 = (rest - la_2.astype(F32)).astype(BF16)
            sums = jnp.dot(local_sum, jnp.concatenate([la_1, la_2, la_3], axis=1), preferred_element_type=F32)
            local = sums[:, 0:qk] + sums[:, qk:2 * qk] + sums[:, 2 * qk:3 * qk]
            bases = [jnp.zeros((1, qk), F32)]
            for j in range(1, n_sub):
                bases.append(bases[-1] + local[j * GLA_SUB - 1:j * GLA_SUB, :])
            base = jnp.concatenate([jnp.broadcast_to(bs, (GLA_SUB, qk)) for bs in bases], axis=0)
            b = base + local
            b_end = b[chunk - 1:chunk, :]
            q_c, k_c = q[rows, :], k[rows, :]
            q_loc = q_c * jnp.exp(local)
            k_loc = k_c * jnp.exp(-local)
            q_in = (q_c * jnp.exp(b)).astype(BF16)
            k_out = (k_c * jnp.exp(b_end - b)).astype(BF16)
            decay_row = jnp.exp(b_end)
            zero_rows = jnp.zeros((GLA_SUB, qk), F32)
            q_parts, k_parts = [], []
            for j in range(n_sub):
                q_parts.append(jnp.concatenate(
                    [q_loc[sub_rows[m], :] * jnp.exp(jnp.minimum(bases[m] - bases[j], 0.0)) for m in range(n_sub)],
                    axis=0).astype(BF16))
                k_parts.append(jnp.concatenate(
                    [k_loc[sub_rows[m], :] if m == j else zero_rows for m in range(n_sub)], axis=0).astype(BF16))
            for hd in range(heads):
                ks = slice(hd * dk, (hd + 1) * dk)
                vs = slice(hd * dv, (hd + 1) * dv)
                vh = v[rows, vs]
                q_cat = jnp.concatenate([p[:, ks] for p in q_parts], axis=1)
                k_cat = jnp.concatenate([p[:, ks] for p in k_parts], axis=1)
                att = lax.dot_general(q_cat, k_cat, (((1,), (1,)), ((), ())), preferred_element_type=F32)
                att = jnp.where(causal, att, 0.0).astype(BF16)
                s_h = state[s, hd]
                o = jnp.dot(att, vh, preferred_element_type=F32)
                o = o + jnp.dot(q_in[:, ks], s_h.astype(BF16), preferred_element_type=F32)
                decay_col = jnp.sum(jnp.where(eye, jnp.broadcast_to(decay_row[:, ks], (dk, dk)), 0.0),
                                    axis=1, keepdims=True)
                state[s, hd] = decay_col * s_h + lax.dot_general(k_out[:, ks], vh, (((0,), (0,)), ((), ())),
                                                                 preferred_element_type=F32)
                o = o * lax.rsqrt(jnp.mean(o * o, axis=-1, keepdims=True) + EPS) * gn_ref[...]
                o_ref[rows, vs] = (o * _silu(g[rows, vs])).astype(o_ref.dtype)

    @pl.when(i == pl.num_programs(1) - 1)
    def _():
        for s in range(n_seq):
            tail_ref[s] = win[s, seq_rows:seq_rows + HIST_PAD, :]
        sout_ref[...] = state[...]


def _front(x2d, hist_pad, s0, g1, w_in, w_lr2, blr2, w_dw, b_dw, ln_g, ln_b, gn, *, n_seq, seq_rows):
    bsz, heads, dk, dv = s0.shape
    d = x2d.shape[1]
    t = x2d.shape[0] // bsz
    c_conv = w_dw.shape[1]
    width = w_dw.shape[0]
    qk, vv = heads * dk, heads * dv
    assert (n_seq == 1 and t % seq_rows == 0) or (seq_rows == t and bsz % n_seq == 0)
    nt = t // seq_rows
    rows = n_seq * seq_rows
    chunk = 64 if seq_rows % 64 == 0 else seq_rows
    assert chunk % GLA_SUB == 0
    const = lambda b, i: (0, 0)
    tile = lambda b, i: (b * nt + i, 0)
    per_seq3 = lambda b, i: (b, 0, 0)
    per_seq4 = lambda b, i: (b, 0, 0, 0)
    return pl.pallas_call(
        functools.partial(_front_kernel, n_seq=n_seq, seq_rows=seq_rows, chunk=chunk, heads=heads, dk=dk, dv=dv,
                          c_conv=c_conv, width=width),
        grid=(bsz // n_seq, nt),
        in_specs=[
            pl.BlockSpec((rows, d), tile),
            pl.BlockSpec((n_seq, HIST_PAD, c_conv), per_seq3),
            pl.BlockSpec((n_seq, heads, dk, dv), per_seq4),
            pl.BlockSpec((1, d), const),
            pl.BlockSpec(w_in.shape, const, pipeline_mode=pl.Buffered(1)),
            pl.BlockSpec(w_lr2.shape, const),
            pl.BlockSpec((1, qk), const),
            pl.BlockSpec(w_dw.shape, const),
            pl.BlockSpec((1, c_conv), const), pl.BlockSpec((1, c_conv), const), pl.BlockSpec((1, c_conv), const),
            pl.BlockSpec((1, dv), const),
        ],
        out_specs=[
            pl.BlockSpec((rows, c_conv), tile), pl.BlockSpec((rows, vv), tile),
            pl.BlockSpec((n_seq, HIST_PAD, c_conv), per_seq3),
            pl.BlockSpec((n_seq, heads, dk, dv), per_seq4),
        ],
        out_shape=[
            jax.ShapeDtypeStruct((bsz * t, c_conv), BF16), jax.ShapeDtypeStruct((bsz * t, vv), BF16),
            jax.ShapeDtypeStruct((bsz, HIST_PAD, c_conv), F32),
            jax.ShapeDtypeStruct((bsz, heads, dk, dv), F32),
        ],
        scratch_shapes=[
            pltpu.VMEM((n_seq, seq_rows + HIST_PAD, c_conv), F32),
            pltpu.VMEM((7, seq_rows + HIST_PAD - 8, LANES), F32),
            pltpu.VMEM((rows, c_conv), F32),
            pltpu.VMEM((n_seq, heads, dk, dv), F32),
        ],
        compiler_params=_params(("arbitrary", "arbitrary")),
        name="front",
    )(x2d, hist_pad, s0, g1, w_in, w_lr2, blr2, w_dw, b_dw, ln_g, ln_b, gn)


def _outproj_kernel(xp_ref, xs_ref, cp_ref, cs_ref, op_ref, os_ref, w_ref, g2_ref, wr_ref, br_ref,
                    x1_ref, dest_ref, gates_ref, pages_ref, counts_ref, sorted_ref,
                    stage, dest_vmem, dest_smem, cnt_s, page_s, npage_s, table_s, scatter_sems, dest_sems,
                    *, n_first, c_conv, n_groups, per_group, trash_row):
    i = pl.program_id(0)
    n_steps = pl.num_programs(0)
    tm = x1_ref.shape[0]
    n_exp = n_groups * per_group
    slot = i % 2
    prev = 1 - slot

    def dest_copy(s):
        return pltpu.make_async_copy(dest_vmem.at[s], dest_smem.at[s], dest_sems.at[s])

    def scatter_wait(s):
        for _ in range(2):
            pltpu.make_async_copy(stage.at[s], sorted_ref.at[pl.ds(0, tm)], scatter_sems.at[s]).wait()

    @pl.when(i == 0)
    def _():
        cnt_s[...] = jnp.zeros_like(cnt_s)
        page_s[...] = jnp.zeros_like(page_s)
        npage_s[...] = jnp.zeros_like(npage_s)
        table_s[...] = jnp.zeros_like(table_s)
        stage[1] = jnp.zeros(stage.shape[1:], stage.dtype)

        def fill(t, carry):
            dest_smem[1, 0, t] = trash_row + t
            dest_smem[1, 1, t] = trash_row + tm + t
            return carry

        lax.fori_loop(0, tm, fill, 0)

    @pl.when(i >= 1)
    def _():
        dest_copy(prev).wait()
        scatter_wait(slot)

    for t in range(tm):
        _row_copy(stage.at[prev], t, sorted_ref, dest_smem[prev, 0, t], scatter_sems.at[prev]).start()
        _row_copy(stage.at[prev], t, sorted_ref, dest_smem[prev, 1, t], scatter_sems.at[prev]).start()

    x = _pick(i, n_first, xp_ref, xs_ref)
    cc = _pick(i, n_first, cp_ref, cs_ref)
    oo = _pick(i, n_first, op_ref, os_ref)
    mix = jnp.dot(cc, w_ref[0:c_conv, :], preferred_element_type=F32)
    mix = mix + jnp.dot(oo, w_ref[c_conv:, :], preferred_element_type=F32)
    x1 = x + mix
    x1_ref[...] = x1
    h2 = _rms(x1, g2_ref[...])
    stage[slot] = _pack_pairs(h2)
    h_hi = h2.astype(BF16)
    h_lo = (h2 - h_hi.astype(F32)).astype(BF16)
    parts = lax.dot_general(wr_ref[...], jnp.concatenate([h_hi, h_lo], axis=0), (((1,), (1,)), ((), ())),
                            preferred_element_type=F32)
    n_r = br_ref.shape[0]
    logits = (parts[0:n_r, 0:tm] + parts[0:n_r, tm:] + parts[n_r:, 0:tm] + parts[n_r:, tm:]) + br_ref[...]
    lc = logits[0:n_groups, :]
    mc = jnp.max(lc, axis=0, keepdims=True)
    p_group = 1.0 / jnp.sum(jnp.exp(lc - mc), axis=0, keepdims=True)
    rows_c = lax.broadcasted_iota(I32, (n_groups, tm), 0)
    g_idx = jnp.min(jnp.where(lc == mc, rows_c, n_groups), axis=0, keepdims=True)
    lf = logits[n_groups:n_groups + n_exp, :]
    rows_f = lax.broadcasted_iota(I32, (n_exp, tm), 0)
    in_group = (rows_f >= g_idx * per_group) & (rows_f < (g_idx + 1) * per_group)
    neg = jnp.float32(-jnp.inf)
    l1 = jnp.where(in_group, lf, neg)
    m1 = jnp.max(l1, axis=0, keepdims=True)
    e1 = jnp.min(jnp.where(l1 == m1, rows_f, n_exp), axis=0, keepdims=True)
    l2 = jnp.where(rows_f == e1, neg, l1)
    m2 = jnp.max(l2, axis=0, keepdims=True)
    e2 = jnp.min(jnp.where(l2 == m2, rows_f, n_exp), axis=0, keepdims=True)
    r2 = jnp.exp(m2 - m1)
    w1 = 1.0 / (1.0 + r2)
    row8 = lax.broadcasted_iota(I32, (8, tm), 0)
    gates_ref[...] = jnp.where(row8 == 0, p_group * w1, jnp.where(row8 == 1, p_group * (r2 * w1), 0.0))

    oh0 = (rows_f == e1).astype(F32)
    oh1 = (rows_f == e2).astype(F32)
    both = oh0 + oh1
    tr = lax.broadcasted_iota(I32, (tm, tm), 0)
    tc = lax.broadcasted_iota(I32, (tm, tm), 1)
    earlier = jnp.dot(both.astype(BF16), (tr < tc).astype(BF16), preferred_element_type=F32)
    cnt = cnt_s[...]
    rank_base = earlier + cnt
    tile_cnt = jnp.sum(both, axis=1, keepdims=True)
    page_rows = float(EXPERT_ROWS)
    k0 = jnp.floor(cnt * (1.0 / page_rows))
    new_cnt = cnt + tile_cnt
    limit = (k0 + 1.0) * page_rows
    need_a = ((cnt == k0 * page_rows) & (tile_cnt > 0.0)).astype(F32)
    need_b = (new_cnt > limit).astype(F32)
    need = need_a + need_b
    er = lax.broadcasted_iota(I32, (n_exp, n_exp), 0)
    ec = lax.broadcasted_iota(I32, (n_exp, n_exp), 1)
    before = jnp.dot((ec < er).astype(BF16), jnp.broadcast_to(need, (n_exp, LANES)).astype(BF16),
                     preferred_element_type=F32)[:, 0:1]
    base = npage_s[...] + before
    page_a = jnp.where(need_a > 0.0, base, page_s[...])
    page_b = base + need_a
    npage_s[...] = npage_s[...] + jnp.sum(need, axis=0, keepdims=True)
    lane = lax.broadcasted_iota(I32, table_s.shape, 1).astype(F32)
    table = jnp.where((lane == k0) & (need_a > 0.0), page_a, table_s[...])
    table_s[...] = jnp.where((lane == k0 + 1.0) & (need_b > 0.0), page_b, table)
    cnt_s[...] = new_cnt
    page_s[...] = jnp.where(jnp.floor(new_cnt * (1.0 / page_rows)) == k0, page_a, page_b)

    def dest_rows(oh):
        rank = jnp.sum(oh * rank_base, axis=0, keepdims=True)
        lim = jnp.sum(oh * limit, axis=0, keepdims=True)
        pa = jnp.sum(oh * page_a, axis=0, keepdims=True)
        pb = jnp.sum(oh * page_b, axis=0, keepdims=True)
        within = rank - jnp.floor(rank * (1.0 / page_rows)) * page_rows
        return jnp.where(rank < lim, pa, pb) * page_rows + within

    dest = jnp.where(row8 == 0, dest_rows(oh0), jnp.where(row8 == 1, dest_rows(oh1), 0.0)).astype(I32)
    dest_ref[...] = dest
    dest_vmem[slot] = dest
    dest_copy(slot).start()

    @pl.when(i == n_steps - 1)
    def _():
        pages_ref[...] = table_s[...].astype(I32)
        counts_ref[...] = jnp.broadcast_to(cnt_s[...], counts_ref.shape).astype(I32)
        dest_copy(slot).wait()
        scatter_wait(prev)

        def last(j, carry):
            for r in range(DMA_UNROLL):
                t = j * DMA_UNROLL + r
                _row_copy(stage.at[slot], t, sorted_ref, dest_smem[slot, 0, t], scatter_sems.at[slot]).start()
                _row_copy(stage.at[slot], t, sorted_ref, dest_smem[slot, 1, t], scatter_sems.at[slot]).start()
            return carry

        lax.fori_loop(0, tm // DMA_UNROLL, last, 0)
        scatter_wait(slot)


def _outproj(xp, xs, cp, cs, op, os_, w_out, g2, wr, br, *, n_groups, per_group, n_pages):
    n_p, d = xp.shape
    n_all = n_p + xs.shape[0]
    n_first = n_p // ROW_TILE
    n_exp = n_groups * per_group
    c_conv = cp.shape[1]
    vv = op.shape[1]
    tile = (d // 2,)
    first, second = _split_maps(n_first)
    const = lambda i: (0, 0)
    row = lambda i: (i, 0)
    col = lambda i: (0, i)
    rows_sorted = n_pages * EXPERT_ROWS + 2 * ROW_TILE
    assert ROW_TILE <= EXPERT_ROWS, "a tile may open at most two pages per expert"
    return pl.pallas_call(
        functools.partial(_outproj_kernel, n_first=n_first, c_conv=c_conv, n_groups=n_groups, per_group=per_group,
                          trash_row=n_pages * EXPERT_ROWS),
        grid=(n_all // ROW_TILE,),
        in_specs=[
            pl.BlockSpec((ROW_TILE, d), first), pl.BlockSpec((ROW_TILE, d), second),
            pl.BlockSpec((ROW_TILE, c_conv), first), pl.BlockSpec((ROW_TILE, c_conv), second),
            pl.BlockSpec((ROW_TILE, vv), first), pl.BlockSpec((ROW_TILE, vv), second),
            pl.BlockSpec(w_out.shape, const, pipeline_mode=pl.Buffered(1)),
            pl.BlockSpec((1, d), const),
            pl.BlockSpec(wr.shape, const),
            pl.BlockSpec(br.shape, const),
        ],
        out_specs=[
            pl.BlockSpec((ROW_TILE, d), row),
            pl.BlockSpec((8, ROW_TILE), col), pl.BlockSpec((8, ROW_TILE), col),
            pl.BlockSpec((n_exp, LANES), const), pl.BlockSpec((n_exp, LANES), const),
            pl.BlockSpec(memory_space=pl.ANY),
        ],
        out_shape=[
            jax.ShapeDtypeStruct((n_all, d), F32),
            jax.ShapeDtypeStruct((8, n_all), I32), jax.ShapeDtypeStruct((8, n_all), F32),
            jax.ShapeDtypeStruct((n_exp, LANES), I32), jax.ShapeDtypeStruct((n_exp, LANES), I32),
            jax.ShapeDtypeStruct((rows_sorted,) + tile, U32),
        ],
        scratch_shapes=[
            pltpu.VMEM((2, ROW_TILE) + tile, U32),
            pltpu.VMEM((2, 8, ROW_TILE), I32), pltpu.SMEM((2, 8, ROW_TILE), I32),
            pltpu.VMEM((n_exp, 1), F32), pltpu.VMEM((n_exp, 1), F32), pltpu.VMEM((1, 1), F32),
            pltpu.VMEM((n_exp, LANES), F32),
            pltpu.SemaphoreType.DMA((2,)), pltpu.SemaphoreType.DMA((2,)),
        ],
        compiler_params=_params(("arbitrary",)),
        name="outproj",
    )(xp, xs, cp, cs, op, os_, w_out, g2, wr, br)


def _experts_kernel(cnt_ref, pages_ref, xs_ref, wg_ref, wu_ref, wd_ref, ysp_ref,
                    xbuf, ybuf, wg_f32, wu_f32, wd_f32, wg_bf, wu_bf, wd_bf, first_blk, page_seq,
                    gsems, ysems, wsems, *, n_exp, table_lanes):
    e = pl.program_id(0)
    tb = xbuf.shape[1]
    n_pages = page_seq.shape[0]

    def n_pages_of(ex):
        return (cnt_ref[ex] + (tb - 1)) // tb

    def page_rows(blk):
        return pl.ds(pl.multiple_of(page_seq[blk] * tb, tb), tb)

    def fetch(blk, slot):
        return pltpu.make_async_copy(xs_ref.at[page_rows(blk)], xbuf.at[slot], gsems.at[slot])

    def writeback(blk, slot):
        return pltpu.make_async_copy(ybuf.at[slot], ysp_ref.at[page_rows(blk)], ysems.at[slot])

    def weight_copies(ex, slot):
        return (pltpu.make_async_copy(wg_ref.at[ex], wg_f32.at[slot], wsems.at[slot]),
                pltpu.make_async_copy(wu_ref.at[ex], wu_f32.at[slot], wsems.at[slot]),
                pltpu.make_async_copy(wd_ref.at[ex], wd_f32.at[slot], wsems.at[slot]))

    @pl.when(e == 0)
    def _():
        for cp in weight_copies(0, 0):
            cp.start(priority=1)

        def per_expert(ex, blk):
            first_blk[ex] = blk

            def per_page(j, carry):
                page_seq[blk + j] = pages_ref[ex * table_lanes + j]
                return carry

            lax.fori_loop(0, n_pages_of(ex), per_page, 0)
            return blk + n_pages_of(ex)

        first_blk[n_exp] = lax.fori_loop(0, n_exp, per_expert, 0)
        fetch(0, 0).start()

    @pl.when(e + 1 < n_exp)
    def _():
        for cp in weight_copies(e + 1, (e + 1) % 2):
            cp.start(priority=1)

    b_lo = first_blk[e]
    b_hi = first_blk[e + 1]
    n_total = first_blk[n_exp]
    wslot = e % 2
    for cp in weight_copies(e, wslot):
        cp.wait()
    wg_bf[...] = wg_f32[wslot].astype(BF16)
    wu_bf[...] = wu_f32[wslot].astype(BF16)
    wd_bf[...] = wd_f32[wslot].astype(BF16)
    row_id = lax.broadcasted_iota(I32, (tb, 1), 0)

    def block(b, carry):
        slot = b % 2

        @pl.when(b >= 2)
        def _():
            writeback(b, slot).wait()

        fetch(b, slot).wait()
        fetch(jnp.minimum(b + 1, n_total - 1), 1 - slot).start()
        valid = cnt_ref[e] - (b - b_lo) * tb
        words = jnp.where(row_id < valid, xbuf[slot], jnp.uint32(0))
        hi, lo = _unpack_pairs(words)
        x = jnp.concatenate([hi.astype(BF16), lo.astype(BF16)], axis=1)
        hg = jnp.dot(x, wg_bf[...], preferred_element_type=F32)
        hu = jnp.dot(x, wu_bf[...], preferred_element_type=F32)
        hb = (_silu(hg) * hu).astype(BF16)
        ybuf[slot] = _pack_pairs(jnp.dot(hb, wd_bf[...], preferred_element_type=F32))
        writeback(b, slot).start()
        return carry

    lax.fori_loop(b_lo, b_hi, block, 0)

    @pl.when(e == n_exp - 1)
    def _():
        fetch(0, n_total % 2).wait()

        @pl.when(n_total >= 2)
        def _():
            writeback(0, n_total % 2).wait()

        writeback(0, (n_total + 1) % 2).wait()
        ybuf[0] = jnp.zeros(ybuf.shape[1:], ybuf.dtype)

        def spare(blk):
            return pltpu.make_async_copy(ybuf.at[0], ysp_ref.at[pl.ds(pl.multiple_of(blk * tb, tb), tb)], ysems.at[0])

        def zero(blk, carry):
            spare(blk).start()
            return carry

        lax.fori_loop(n_total, n_pages, zero, 0)

        def zero_wait(blk, carry):
            spare(0).wait()
            return carry

        lax.fori_loop(n_total, n_pages, zero_wait, 0)


def _experts(counts, pages_flat, xs_sorted, w_gate, w_up, w_down, *, n_pages, table_lanes):
    tile = xs_sorted.shape[1:]
    n_exp, d, ff = w_gate.shape
    anyspec = pl.BlockSpec(memory_space=pl.ANY)
    grid_spec = pltpu.PrefetchScalarGridSpec(
        num_scalar_prefetch=2,
        grid=(n_exp,),
        in_specs=[anyspec, anyspec, anyspec, anyspec],
        out_specs=anyspec,
        scratch_shapes=[
            pltpu.VMEM((2, EXPERT_ROWS) + tile, U32), pltpu.VMEM((2, EXPERT_ROWS) + tile, U32),
            pltpu.VMEM((2, d, ff), F32), pltpu.VMEM((2, d, ff), F32), pltpu.VMEM((2, ff, d), F32),
            pltpu.VMEM((d, ff), BF16), pltpu.VMEM((d, ff), BF16), pltpu.VMEM((ff, d), BF16),
            pltpu.SMEM((n_exp + 1,), I32), pltpu.SMEM((n_pages,), I32),
            pltpu.SemaphoreType.DMA((2,)), pltpu.SemaphoreType.DMA((2,)), pltpu.SemaphoreType.DMA((2,)),
        ],
    )
    return pl.pallas_call(
        functools.partial(_experts_kernel, n_exp=n_exp, table_lanes=table_lanes),
        grid_spec=grid_spec,
        out_shape=jax.ShapeDtypeStruct((n_pages * EXPERT_ROWS,) + tile, U32),
        compiler_params=_params(("arbitrary",)),
        name="experts",
    )(counts, pages_flat, xs_sorted, w_gate, w_up, w_down)


def _combine_kernel(dest_ref, dest_next_ref, gates_ref, x1_ref, ysp_ref, gf_ref, yp_ref, ysmp_ref,
                    buf0, buf1, sems, *, n_first):
    i = pl.program_id(0)
    n = pl.num_programs(0)
    tm = x1_ref.shape[0]
    slot = i % 2

    def gather(d_ref, s):
        def body(j, carry):
            for r in range(DMA_UNROLL):
                t = j * DMA_UNROLL + r
                _row_copy(ysp_ref, d_ref[0, t], buf0.at[s], t, sems.at[s]).start(priority=0)
                _row_copy(ysp_ref, d_ref[1, t], buf1.at[s], t, sems.at[s]).start(priority=1)
            return carry

        lax.fori_loop(0, tm // DMA_UNROLL, body, 0)

    @pl.when(i == 0)
    def _():
        gather(dest_ref, 0)

    @pl.when(i + 1 < n)
    def _():
        gather(dest_next_ref, 1 - slot)

    pltpu.make_async_copy(ysp_ref.at[pl.ds(0, tm)], buf0.at[slot], sems.at[slot]).wait()
    pltpu.make_async_copy(ysp_ref.at[pl.ds(0, tm)], buf1.at[slot], sems.at[slot]).wait()
    hi0, lo0 = _unpack_pairs(buf0[slot])
    hi1, lo1 = _unpack_pairs(buf1[slot])
    g0 = gates_ref[:, 0:1]
    g1 = gates_ref[:, 1:2]
    moe = jnp.concatenate([g0 * hi0 + g1 * hi1, g0 * lo0 + g1 * lo1], axis=1)
    y = _rms(x1_ref[...] + moe, gf_ref[...])

    @pl.when(i < n_first)
    def _():
        yp_ref[...] = y

    @pl.when(i >= n_first)
    def _():
        ysmp_ref[...] = y


def _combine(dest, gates_t, x1, ysp, gf, *, n_p):
    n_all, d = x1.shape
    tile = ysp.shape[1:]
    n_first = n_p // ROW_TILE
    n_tiles = n_all // ROW_TILE
    first, second = _split_maps(n_first)
    return pl.pallas_call(
        functools.partial(_combine_kernel, n_first=n_first),
        grid=(n_tiles,),
        in_specs=[
            pl.BlockSpec((8, ROW_TILE), lambda i: (0, i), memory_space=pltpu.SMEM),
            pl.BlockSpec((8, ROW_TILE), lambda i: (0, jnp.minimum(i + 1, n_tiles - 1)), memory_space=pltpu.SMEM),
            pl.BlockSpec((ROW_TILE, 8), lambda i: (i, 0)),
            pl.BlockSpec((ROW_TILE, d), lambda i: (i, 0)),
            pl.BlockSpec(memory_space=pl.ANY),
            pl.BlockSpec((1, d), lambda i: (0, 0)),
        ],
        out_specs=[pl.BlockSpec((ROW_TILE, d), first), pl.BlockSpec((ROW_TILE, d), second)],
        out_shape=[jax.ShapeDtypeStruct((n_p, d), F32), jax.ShapeDtypeStruct((n_all - n_p, d), F32)],
        scratch_shapes=[pltpu.VMEM((2, ROW_TILE) + tile, U32), pltpu.VMEM((2, ROW_TILE) + tile, U32),
                        pltpu.SemaphoreType.DMA((2,))],
        compiler_params=_params(("arbitrary",)),
        name="combine",
    )(dest, dest, gates_t, x1, ysp, gf)


def _chunk_for(t):
    return 64 if t % 64 == 0 else t


def kernel(x_prompt, x_sample, cache_conv, state_gla, norm1_g, w_in, w_lr2, b_lr2, w_dw, b_dw, conv_ln_g, conv_ln_b, gla_norm_g, w_out, norm2_g, w_router_coarse, b_router_coarse, w_router_fine, b_router_fine, w_exp_gate, w_exp_up, w_exp_down, norm_f_g):
    assert norm1_g.shape[0] == 1, "single trunk layer"
    bp, tp, d = x_prompt.shape
    bs, ts, _ = x_sample.shape
    heads, dk, dv = state_gla.shape[2:]
    c_conv = w_dw.shape[2]
    width = w_dw.shape[1]
    rank = w_lr2.shape[1]
    qk, vv = heads * dk, heads * dv
    n_groups, _, per_group = w_router_fine.shape[1:]
    n_exp = n_groups * per_group
    n_p, n_s = bp * tp, bs * ts
    n_s_pad = -(-n_s // ROW_TILE) * ROW_TILE
    pad_rows = lambda a: jnp.pad(a, ((0, n_s_pad - n_s), (0, 0)))
    n_all = n_p + n_s_pad
    assert n_p % ROW_TILE == 0 and width - 1 <= HIST_PAD

    xp = x_prompt.reshape(n_p, d)
    xs = pad_rows(x_sample.reshape(n_s, d))
    row = lambda a: a.reshape(1, -1)

    mixer_args = (row(norm1_g[0]), w_in[0].astype(BF16), w_lr2[0], row(b_lr2[0]),
                  w_dw[0], row(b_dw[0]), row(conv_ln_g[0]), row(conv_ln_b[0]), row(gla_norm_g[0]))
    hist_p = jnp.zeros((bp, HIST_PAD, c_conv), F32)
    hist_s = jnp.pad(cache_conv[0], ((0, 0), (HIST_PAD - (width - 1), 0), (0, 0)))
    s0_p = jnp.zeros((bp, heads, dk, dv), F32)
    c_p, o_p, tail_p, gla_p = _front(xp, hist_p, s0_p, *mixer_args, n_seq=1, seq_rows=ROW_TILE)
    c_s, o_s, tail_s, gla_s = _front(x_sample.reshape(n_s, d), hist_s, state_gla[0], *mixer_args,
                                     n_seq=bs, seq_rows=ts)
    c_s, o_s = pad_rows(c_s), pad_rows(o_s)

    wr = jnp.concatenate([w_router_coarse[0].T,
                          jnp.transpose(w_router_fine[0], (0, 2, 1)).reshape(n_exp, d)], axis=0)
    br = jnp.concatenate([b_router_coarse[0], b_router_fine[0].reshape(n_exp)])
    r_rows = -(-(n_groups + n_exp) // 8) * 8
    wr = jnp.pad(wr, ((0, r_rows - wr.shape[0]), (0, 0)))
    br = jnp.pad(br, (0, r_rows - br.shape[0])).reshape(r_rows, 1)
    wr_hi = wr.astype(BF16)
    wr = jnp.concatenate([wr_hi, (wr - wr_hi.astype(F32)).astype(BF16)], axis=0)
    n_pages = (2 * n_all) // EXPERT_ROWS + n_exp
    assert n_all // EXPERT_ROWS + 2 <= LANES, "page table row must hold one expert's pages"
    x1, dest, gates, pages, counts, xs_sorted = _outproj(
        xp, xs, c_p, c_s, o_p, o_s, w_out[0].astype(BF16), row(norm2_g[0]), wr, br,
        n_groups=n_groups, per_group=per_group, n_pages=n_pages)
    ysp = _experts(counts[:, 0], pages.reshape(-1), xs_sorted, w_exp_gate[0], w_exp_up[0], w_exp_down[0],
                   n_pages=n_pages, table_lanes=LANES)
    y_p, y_s = _combine(dest, gates.T, x1, ysp, row(norm_f_g), n_p=n_p)

    lead = HIST_PAD - (width - 1)
    return (y_p.reshape(bp, tp, d), y_s[:n_s].reshape(bs, ts, d), tail_p[:, lead:][None], gla_p[None],
            tail_s[:, lead:][None], gla_s[None])
```

```python
import functools

import jax
import jax.numpy as jnp
from jax import lax
from jax.experimental import pallas as pl
from jax.experimental.pallas import tpu as pltpu

F32 = jnp.float32
BF16 = jnp.bfloat16
I32 = jnp.int32
U32 = jnp.uint32
EPS = 1e-6
GATE_TEMP = 16.0
HIGHEST = lax.Precision.HIGHEST

LANES = 128
ROW_TILE = 256
EXPERT_ROWS = 256
HIST_PAD = 32
DMA_UNROLL = 8
VMEM_LIMIT = 56 * 1024 * 1024


def _params(semantics, vmem=VMEM_LIMIT):
    return pltpu.CompilerParams(dimension_semantics=semantics, vmem_limit_bytes=vmem)


def _rms(x, g):
    return x * lax.rsqrt(jnp.mean(x * x, axis=-1, keepdims=True) + EPS) * g


def _silu(x):
    return x * jax.nn.sigmoid(x)


def _log_sigmoid(z):
    return jnp.minimum(z, 0.0) - jnp.log(1.0 + jnp.exp(-jnp.abs(z)))


def _pick(i, n_first, first_ref, second_ref):
    return jnp.where(i < n_first, first_ref[...], second_ref[...])


def _split_maps(n_first):
    first = lambda i: (jnp.minimum(i, n_first - 1), 0)
    second = lambda i: (jnp.maximum(i - n_first, 0), 0)
    return first, second


def _pack_pairs(x):
    half = x.shape[1] // 2
    hi = lax.bitcast_convert_type(x[:, :half].astype(BF16).astype(F32), U32)
    lo = lax.bitcast_convert_type(x[:, half:].astype(BF16).astype(F32), U32)
    return hi | (lo >> 16)


def _unpack_pairs(p):
    hi = lax.bitcast_convert_type(p & jnp.uint32(0xFFFF0000), F32)
    lo = lax.bitcast_convert_type(p << 16, F32)
    return hi, lo


def _row_copy(src, s, dst, d, sem):
    return pltpu.make_async_copy(src.at[pl.ds(s, 1)], dst.at[pl.ds(d, 1)], sem)


def _inproj_kernel(xp_ref, xs_ref, g1_ref, w_ref, wlr2_ref, blr2_ref,
                   u_ref, q_ref, k_ref, v_ref, g_ref, la_ref, *, n_first, c_conv, qk, vv, dk):
    i = pl.program_id(0)
    x = _pick(i, n_first, xp_ref, xs_ref)
    h = _rms(x, g1_ref[...]).astype(BF16)

    def mm(lo, width):
        return jnp.dot(h, w_ref[:, lo:lo + width], preferred_element_type=F32)

    a = mm(0, c_conv)
    a_gate = mm(c_conv, c_conv)
    u_ref[...] = a * jax.nn.sigmoid(a_gate)
    off = 2 * c_conv
    q_ref[...] = mm(off, qk) * (dk ** -0.5)
    k_ref[...] = mm(off + qk, qk)
    v_ref[...] = mm(off + 2 * qk, vv)
    g_ref[...] = mm(off + 2 * qk + vv, vv)
    lr = mm(off + 2 * qk + 2 * vv, LANES)
    z = jnp.dot(lr, wlr2_ref[...], precision=HIGHEST, preferred_element_type=F32) + blr2_ref[...]
    la_ref[...] = _log_sigmoid(z) * (1.0 / GATE_TEMP)


def _inproj(xp, xs, g1, w_pad, wlr2_pad, blr2, *, c_conv, qk, vv, dk):
    n_p, d = xp.shape
    n_s = xs.shape[0]
    n_all = n_p + n_s
    n_first = n_p // ROW_TILE
    grid = (n_all // ROW_TILE,)
    first, second = _split_maps(n_first)
    const = lambda i: (0, 0)
    row = lambda i: (i, 0)
    widths = (c_conv, qk, qk, vv, vv, qk)
    return pl.pallas_call(
        functools.partial(_inproj_kernel, n_first=n_first, c_conv=c_conv, qk=qk, vv=vv, dk=dk),
        grid=grid,
        in_specs=[
            pl.BlockSpec((ROW_TILE, d), first),
            pl.BlockSpec((ROW_TILE, d), second),
            pl.BlockSpec((1, d), const),
            pl.BlockSpec(w_pad.shape, const, pipeline_mode=pl.Buffered(1)),
            pl.BlockSpec(wlr2_pad.shape, const),
            pl.BlockSpec((1, qk), const),
        ],
        out_specs=[pl.BlockSpec((ROW_TILE, w), row) for w in widths],
        out_shape=[jax.ShapeDtypeStruct((n_all, w), F32) for w in widths],
        compiler_params=_params(("arbitrary",)),
        name="inproj",
    )(xp, xs, g1, w_pad, wlr2_pad, blr2)


def _conv_kernel(u_ref, hist_ref, w_ref, b_ref, lg_ref, lb_ref, c_ref, win, cbuf, *, tt, width):
    i = pl.program_id(1)

    @pl.when(i == 0)
    def _():
        win[0:HIST_PAD, :] = hist_ref[0]

    @pl.when(i > 0)
    def _():
        win[0:HIST_PAD, :] = win[tt:tt + HIST_PAD, :]

    win[HIST_PAD:HIST_PAD + tt, :] = u_ref[...]
    lead = HIST_PAD - (width - 1)
    n_ch = u_ref.shape[1]
    for cb in range(n_ch // LANES):
        cs = slice(cb * LANES, (cb + 1) * LANES)
        acc = jnp.broadcast_to(b_ref[:, cs], (tt, LANES))
        for j in range(width):
            acc = acc + w_ref[j:j + 1, cs] * win[lead + j:lead + j + tt, cs]
        cbuf[:, cs] = acc
    c = cbuf[...]
    mu = jnp.mean(c, axis=-1, keepdims=True)
    xc = c - mu
    y = xc * lax.rsqrt(jnp.mean(xc * xc, axis=-1, keepdims=True) + EPS) * lg_ref[...] + lb_ref[...]
    c_ref[...] = _silu(y).astype(c_ref.dtype)


def _conv(u_all, hist_pad, w_dw, b_dw, ln_g, ln_b, *, row0, bsz, t, tt):
    n_ch = u_all.shape[1]
    width = w_dw.shape[0]
    nt = t // tt
    blk0 = row0 // tt
    const = lambda b, i: (0, 0)
    return pl.pallas_call(
        functools.partial(_conv_kernel, tt=tt, width=width),
        grid=(bsz, nt),
        in_specs=[
            pl.BlockSpec((tt, n_ch), lambda b, i: (blk0 + b * nt + i, 0)),
            pl.BlockSpec((1, HIST_PAD, n_ch), lambda b, i: (b, 0, 0)),
            pl.BlockSpec(w_dw.shape, const),
            pl.BlockSpec((1, n_ch), const),
            pl.BlockSpec((1, n_ch), const),
            pl.BlockSpec((1, n_ch), const),
        ],
        out_specs=pl.BlockSpec((tt, n_ch), lambda b, i: (b * nt + i, 0)),
        out_shape=jax.ShapeDtypeStruct((bsz * t, n_ch), BF16),
        scratch_shapes=[pltpu.VMEM((tt + HIST_PAD, n_ch), F32), pltpu.VMEM((tt, n_ch), F32)],
        compiler_params=_params(("arbitrary", "arbitrary")),
        name="conv",
    )(u_all, hist_pad, w_dw, b_dw, ln_g, ln_b)


def _gla_kernel(q_ref, k_ref, v_ref, g_ref, la_ref, s0_ref, gn_ref, o_ref, sout_ref, state,
                *, chunk, heads, dk, dv):
    i = pl.program_id(1)

    @pl.when(i == 0)
    def _():
        state[...] = s0_ref[0]

    la = la_ref[...]
    r = lax.broadcasted_iota(I32, (chunk, chunk), 0)
    c = lax.broadcasted_iota(I32, (chunk, chunk), 1)
    causal = c <= r
    b = jnp.dot(causal.astype(F32), la, precision=HIGHEST, preferred_element_type=F32)
    b_end = b[chunk - 1:chunk, :]
    q_in = (q_ref[...] * jnp.exp(b)).astype(BF16)
    k_in = (k_ref[...] * jnp.exp(-b)).astype(BF16)
    k_out = (k_ref[...] * jnp.exp(b_end - b)).astype(BF16)
    decay_row = jnp.exp(b_end)
    eye = lax.broadcasted_iota(I32, (dk, dk), 0) == lax.broadcasted_iota(I32, (dk, dk), 1)
    for h in range(heads):
        ks = slice(h * dk, (h + 1) * dk)
        vs = slice(h * dv, (h + 1) * dv)
        vh = v_ref[:, vs].astype(BF16)
        att = lax.dot_general(q_in[:, ks], k_in[:, ks], (((1,), (1,)), ((), ())), preferred_element_type=F32)
        att = jnp.where(causal, att, 0.0).astype(BF16)
        s_h = state[h]
        o = jnp.dot(att, vh, preferred_element_type=F32)
        o = o + jnp.dot(q_in[:, ks], s_h.astype(BF16), preferred_element_type=F32)
        decay_col = jnp.sum(jnp.where(eye, jnp.broadcast_to(decay_row[:, ks], (dk, dk)), 0.0), axis=1, keepdims=True)
        state[h] = decay_col * s_h + lax.dot_general(k_out[:, ks], vh, (((0,), (0,)), ((), ())),
                                                     preferred_element_type=F32)
        o = o * lax.rsqrt(jnp.mean(o * o, axis=-1, keepdims=True) + EPS) * gn_ref[...]
        o_ref[:, vs] = (o * _silu(g_ref[:, vs])).astype(o_ref.dtype)

    @pl.when(i == pl.num_programs(1) - 1)
    def _():
        sout_ref[0] = state[...]


def _gla(q_all, k_all, v_all, g_all, la_all, s0, gn, *, row0, bsz, t, chunk):
    heads, dk, dv = s0.shape[1:]
    nt = t // chunk
    blk0 = row0 // chunk
    rows = lambda b, i: (blk0 + b * nt + i, 0)
    return pl.pallas_call(
        functools.partial(_gla_kernel, chunk=chunk, heads=heads, dk=dk, dv=dv),
        grid=(bsz, nt),
        in_specs=[
            pl.BlockSpec((chunk, heads * dk), rows),
            pl.BlockSpec((chunk, heads * dk), rows),
            pl.BlockSpec((chunk, heads * dv), rows),
            pl.BlockSpec((chunk, heads * dv), rows),
            pl.BlockSpec((chunk, heads * dk), rows),
            pl.BlockSpec((1, heads, dk, dv), lambda b, i: (b, 0, 0, 0)),
            pl.BlockSpec((1, dv), lambda b, i: (0, 0)),
        ],
        out_specs=[
            pl.BlockSpec((chunk, heads * dv), lambda b, i: (b * nt + i, 0)),
            pl.BlockSpec((1, heads, dk, dv), lambda b, i: (b, 0, 0, 0)),
        ],
        out_shape=[
            jax.ShapeDtypeStruct((bsz * t, heads * dv), BF16),
            jax.ShapeDtypeStruct((bsz, heads, dk, dv), F32),
        ],
        scratch_shapes=[pltpu.VMEM((heads, dk, dv), F32)],
        compiler_params=_params(("arbitrary", "arbitrary")),
        name="gla",
    )(q_all, k_all, v_all, g_all, la_all, s0, gn)


def _mixer_kernel(x_ref, hist_ref, s0_ref, g1_ref, w_ref, wlr2_ref, blr2_ref,
                  wdw_ref, bdw_ref, lg_ref, lb_ref, gn_ref,
                  c_ref, o_ref, tail_ref, sout_ref, pu, pq, pk, pv, pg, pla, win, shifted, cbuf, state,
                  *, n_seq, seq_rows, chunk, heads, dk, dv, c_conv, width):
    i = pl.program_id(1)
    qk, vv = heads * dk, heads * dv

    @pl.when(i == 0)
    def _():
        for buf in (pu, pq, pk, pv, pg, pla):
            buf[...] = jnp.zeros_like(buf)
        win[...] = jnp.zeros_like(win)
        state[...] = jnp.zeros_like(state)

    @pl.when(i == 1)
    def _():
        for s in range(n_seq):
            win[s, 0:HIST_PAD, :] = hist_ref[s]
        state[...] = s0_ref[...]

    @pl.when(i > 1)
    def _():
        for s in range(n_seq):
            win[s, 0:HIST_PAD, :] = win[s, seq_rows:seq_rows + HIST_PAD, :]

    h = _rms(x_ref[...], g1_ref[...]).astype(BF16)

    def mm(lo, n):
        return jnp.dot(h, w_ref[:, lo:lo + n], preferred_element_type=F32)

    off = 2 * c_conv
    new = {}

    def proj_u(j, n):
        lo = j * n
        new["u", j] = mm(lo, n) * jax.nn.sigmoid(mm(c_conv + lo, n))

    def proj(name, lo, n, scale=None, dtype=F32):
        val = mm(lo, n)
        new[name] = (val if scale is None else val * scale).astype(dtype)

    def proj_la():
        lr = mm(off + 2 * qk + 2 * vv, wlr2_ref.shape[0])
        z = jnp.dot(lr, wlr2_ref[...], precision=HIGHEST, preferred_element_type=F32) + blr2_ref[...]
        new["la"] = _log_sigmoid(z) * (1.0 / GATE_TEMP)

    half_c, half_v = c_conv // 2, vv // 2
    stage1 = [
        functools.partial(proj_u, 0, half_c), functools.partial(proj_u, 1, half_c),
        functools.partial(proj, "q", off, qk, dk ** -0.5), functools.partial(proj, "k", off + qk, qk),
        functools.partial(proj, ("v", 0), off + 2 * qk, half_v, None, BF16),
        functools.partial(proj, ("v", 1), off + 2 * qk + half_v, half_v, None, BF16),
        functools.partial(proj, ("g", 0), off + 2 * qk + vv, half_v),
        functools.partial(proj, ("g", 1), off + 2 * qk + vv + half_v, half_v),
        proj_la,
    ]

    lead = HIST_PAD - (width - 1)
    span = shifted.shape[1]

    def conv_fill(s):
        r0 = s * seq_rows
        win[s, HIST_PAD:HIST_PAD + seq_rows, :] = pu[r0:r0 + seq_rows, :]

    def conv_block(s, cb):
        r0 = s * seq_rows
        cs = slice(cb * LANES, (cb + 1) * LANES)
        for r in range(1, 8):
            shifted[r - 1] = win[s, r:r + span, cs]
        for t0 in range(0, seq_rows, chunk):
            acc = jnp.broadcast_to(bdw_ref[:, cs], (chunk, LANES))
            for j in range(width):
                r, a8 = (lead + j) % 8, (lead + j) // 8 * 8
                if r == 0:
                    tap = win[s, t0 + a8:t0 + a8 + chunk, cs]
                else:
                    tap = shifted[r - 1, t0 + a8:t0 + a8 + chunk, :]
                acc = acc + wdw_ref[j:j + 1, cs] * tap
            cbuf[r0 + t0:r0 + t0 + chunk, cs] = acc

    def conv_norm():
        cv = cbuf[...]
        mu = jnp.mean(cv, axis=-1, keepdims=True)
        xc = cv - mu
        cn = xc * lax.rsqrt(jnp.mean(xc * xc, axis=-1, keepdims=True) + EPS) * lg_ref[...] + lb_ref[...]
        c_ref[...] = _silu(cn).astype(c_ref.dtype)

    r = lax.broadcasted_iota(I32, (chunk, chunk), 0)
    c = lax.broadcasted_iota(I32, (chunk, chunk), 1)
    causal = c <= r
    tril = causal.astype(F32)
    eye = lax.broadcasted_iota(I32, (dk, dk), 0) == lax.broadcasted_iota(I32, (dk, dk), 1)
    gla = {}

    def gla_prep(s, t0):
        rows = slice(s * seq_rows + t0, s * seq_rows + t0 + chunk)
        b = jnp.dot(tril, pla[rows, :], precision=HIGHEST, preferred_element_type=F32)
        b_end = b[chunk - 1:chunk, :]
        k_c = pk[rows, :]
        gla[s, t0] = ((pq[rows, :] * jnp.exp(b)).astype(BF16), (k_c * jnp.exp(-b)).astype(BF16),
                      (k_c * jnp.exp(b_end - b)).astype(BF16), jnp.exp(b_end))

    def gla_head(s, t0, hd):
        rows = slice(s * seq_rows + t0, s * seq_rows + t0 + chunk)
        q_in, k_in, k_out, decay_row = gla[s, t0]
        ks = slice(hd * dk, (hd + 1) * dk)
        vs = slice(hd * dv, (hd + 1) * dv)
        vh = pv[rows, vs]
        att = lax.dot_general(q_in[:, ks], k_in[:, ks], (((1,), (1,)), ((), ())), preferred_element_type=F32)
        att = jnp.where(causal, att, 0.0).astype(BF16)
        s_h = state[s, hd]
        o = jnp.dot(att, vh, preferred_element_type=F32)
        o = o + jnp.dot(q_in[:, ks], s_h.astype(BF16), preferred_element_type=F32)
        decay_col = jnp.sum(jnp.where(eye, jnp.broadcast_to(decay_row[:, ks], (dk, dk)), 0.0), axis=1, keepdims=True)
        state[s, hd] = decay_col * s_h + lax.dot_general(k_out[:, ks], vh, (((0,), (0,)), ((), ())),
                                                         preferred_element_type=F32)
        o = o * lax.rsqrt(jnp.mean(o * o, axis=-1, keepdims=True) + EPS) * gn_ref[...]
        o_ref[rows, vs] = (o * _silu(pg[rows, vs])).astype(o_ref.dtype)

    stage2 = []
    for s in range(n_seq):
        stage2.append(functools.partial(conv_fill, s))
        stage2 += [functools.partial(conv_block, s, cb) for cb in range(c_conv // LANES)]
    stage2.append(conv_norm)
    for s in range(n_seq):
        for t0 in range(0, seq_rows, chunk):
            stage2.append(functools.partial(gla_prep, s, t0))
            stage2 += [functools.partial(gla_head, s, t0, hd) for hd in range(heads)]

    per = -(-len(stage2) // len(stage1))
    for n, piece in enumerate(stage1):
        piece()
        for other in stage2[n * per:(n + 1) * per]:
            other()

    pu[...] = jnp.concatenate([new["u", 0], new["u", 1]], axis=1)
    pq[...] = new["q"]
    pk[...] = new["k"]
    pv[...] = jnp.concatenate([new["v", 0], new["v", 1]], axis=1)
    pg[...] = jnp.concatenate([new["g", 0], new["g", 1]], axis=1)
    pla[...] = new["la"]

    @pl.when(i == pl.num_programs(1) - 1)
    def _():
        for s in range(n_seq):
            tail_ref[s] = win[s, seq_rows:seq_rows + HIST_PAD, :]
        sout_ref[...] = state[...]


def _mixer(x2d, hist_pad, s0, g1, w_in, w_lr2, blr2, w_dw, b_dw, ln_g, ln_b, gn, *, n_seq, seq_rows):
    bsz, heads, dk, dv = s0.shape
    d = x2d.shape[1]
    t = x2d.shape[0] // bsz
    c_conv = w_dw.shape[1]
    width = w_dw.shape[0]
    qk, vv = heads * dk, heads * dv
    assert (n_seq == 1 and t % seq_rows == 0) or (seq_rows == t and bsz % n_seq == 0)
    nt = t // seq_rows
    rows = n_seq * seq_rows
    span = seq_rows + HIST_PAD - 8
    chunk = 64 if seq_rows % 64 == 0 else seq_rows
    const = lambda b, i: (0, 0)
    tile_in = lambda b, i: (b * nt + jnp.minimum(i, nt - 1), 0)
    tile_out = lambda b, i: (b * nt + jnp.maximum(i - 1, 0), 0)
    per_seq3 = lambda b, i: (b, 0, 0)
    per_seq4 = lambda b, i: (b, 0, 0, 0)
    return pl.pallas_call(
        functools.partial(_mixer_kernel, n_seq=n_seq, seq_rows=seq_rows, chunk=chunk, heads=heads, dk=dk, dv=dv,
                          c_conv=c_conv, width=width),
        grid=(bsz // n_seq, nt + 1),
        in_specs=[
            pl.BlockSpec((rows, d), tile_in),
            pl.BlockSpec((n_seq, HIST_PAD, c_conv), per_seq3),
            pl.BlockSpec((n_seq, heads, dk, dv), per_seq4),
            pl.BlockSpec((1, d), const),
            pl.BlockSpec(w_in.shape, const, pipeline_mode=pl.Buffered(1)),
            pl.BlockSpec(w_lr2.shape, const),
            pl.BlockSpec((1, qk), const),
            pl.BlockSpec(w_dw.shape, const),
            pl.BlockSpec((1, c_conv), const), pl.BlockSpec((1, c_conv), const), pl.BlockSpec((1, c_conv), const),
            pl.BlockSpec((1, dv), const),
        ],
        out_specs=[
            pl.BlockSpec((rows, c_conv), tile_out), pl.BlockSpec((rows, vv), tile_out),
            pl.BlockSpec((n_seq, HIST_PAD, c_conv), per_seq3),
            pl.BlockSpec((n_seq, heads, dk, dv), per_seq4),
        ],
        out_shape=[
            jax.ShapeDtypeStruct((bsz * t, c_conv), BF16), jax.ShapeDtypeStruct((bsz * t, vv), BF16),
            jax.ShapeDtypeStruct((bsz, HIST_PAD, c_conv), F32),
            jax.ShapeDtypeStruct((bsz, heads, dk, dv), F32),
        ],
        scratch_shapes=[
            pltpu.VMEM((rows, c_conv), F32), pltpu.VMEM((rows, qk), F32), pltpu.VMEM((rows, qk), F32),
            pltpu.VMEM((rows, vv), BF16), pltpu.VMEM((rows, vv), F32), pltpu.VMEM((rows, qk), F32),
            pltpu.VMEM((n_seq, seq_rows + HIST_PAD, c_conv), F32),
            pltpu.VMEM((7, span, LANES), F32),
            pltpu.VMEM((rows, c_conv), F32),
            pltpu.VMEM((n_seq, heads, dk, dv), F32),
        ],
        compiler_params=_params(("arbitrary", "arbitrary")),
        name="mixer",
    )(x2d, hist_pad, s0, g1, w_in, w_lr2, blr2, w_dw, b_dw, ln_g, ln_b, gn)


GLA_SUB = 16


def _zero_fill_step(zero_ref, zbuf, zsem, n_chunks, per_step):
    step = pl.program_id(0) * pl.num_programs(1) + pl.program_id(1)
    last = pl.num_programs(0) * pl.num_programs(1) - 1
    rows = zbuf.shape[0]

    def copy(idx):
        return pltpu.make_async_copy(zbuf, zero_ref.at[pl.ds(pl.multiple_of(idx * rows, rows), rows)], zsem)

    @pl.when(step == 0)
    def _():
        zbuf[...] = jnp.zeros_like(zbuf)

    for p in range(per_step):
        earlier = (step - 1) * per_step + p

        @pl.when((step > 0) & (earlier < n_chunks))
        def _():
            copy(earlier).wait()

    for p in range(per_step):
        idx = step * per_step + p

        @pl.when(idx < n_chunks)
        def _():
            copy(idx).start()

        @pl.when((step == last) & (idx < n_chunks))
        def _():
            copy(idx).wait()


def _front_kernel(x_ref, hist_ref, s0_ref, g1_ref, w_ref, wlr2_ref, blr2_ref,
                  wdw_ref, bdw_ref, lg_ref, lb_ref, gn_ref,
                  c_ref, o_ref, tail_ref, sout_ref, *rest,
                  n_seq, seq_rows, chunk, heads, dk, dv, c_conv, width, zero_chunks):
    i = pl.program_id(1)
    qk, vv = heads * dk, heads * dv
    if zero_chunks:
        zero_ref, win, shifted, cbuf, state, zbuf, zsem = rest
        _zero_fill_step(zero_ref, zbuf, zsem, *zero_chunks)
    else:
        win, shifted, cbuf, state = rest

    @pl.when(i == 0)
    def _():
        for s in range(n_seq):
            win[s, 0:HIST_PAD, :] = hist_ref[s]
        state[...] = s0_ref[...]

    @pl.when(i > 0)
    def _():
        for s in range(n_seq):
            win[s, 0:HIST_PAD, :] = win[s, seq_rows:seq_rows + HIST_PAD, :]

    h = _rms(x_ref[...], g1_ref[...]).astype(BF16)

    def mm(lo, n):
        return jnp.dot(h, w_ref[:, lo:lo + n], preferred_element_type=F32)

    u = mm(0, c_conv) * jax.nn.sigmoid(mm(c_conv, c_conv))
    lead = HIST_PAD - (width - 1)
    span = shifted.shape[1]
    for s in range(n_seq):
        r0 = s * seq_rows
        win[s, HIST_PAD:HIST_PAD + seq_rows, :] = u[r0:r0 + seq_rows, :]
        for cb in range(c_conv // LANES):
            cs = slice(cb * LANES, (cb + 1) * LANES)
            for r in range(1, 8):
                shifted[r - 1] = win[s, r:r + span, cs]
            for t0 in range(0, seq_rows, chunk):
                acc = jnp.broadcast_to(bdw_ref[:, cs], (chunk, LANES))
                for j in range(width):
                    r, a8 = (lead + j) % 8, (lead + j) // 8 * 8
                    if r == 0:
                        tap = win[s, t0 + a8:t0 + a8 + chunk, cs]
                    else:
                        tap = shifted[r - 1, t0 + a8:t0 + a8 + chunk, :]
                    acc = acc + wdw_ref[j:j + 1, cs] * tap
                cbuf[r0 + t0:r0 + t0 + chunk, cs] = acc
    cv = cbuf[...]
    mu = jnp.mean(cv, axis=-1, keepdims=True)
    xc = cv - mu
    cn = xc * lax.rsqrt(jnp.mean(xc * xc, axis=-1, keepdims=True) + EPS) * lg_ref[...] + lb_ref[...]
    c_ref[...] = _silu(cn).astype(c_ref.dtype)

    off = 2 * c_conv
    q = mm(off, qk) * (dk ** -0.5)
    k = mm(off + qk, qk)
    v = mm(off + 2 * qk, vv).astype(BF16)
    g = mm(off + 2 * qk + vv, vv)
    lr = mm(off + 2 * qk + 2 * vv, wlr2_ref.shape[0])
    z = jnp.dot(lr, wlr2_ref[...], precision=HIGHEST, preferred_element_type=F32) + blr2_ref[...]
    la = _log_sigmoid(z) * (1.0 / GATE_TEMP)

    n_sub = chunk // GLA_SUB
    r = lax.broadcasted_iota(I32, (chunk, chunk), 0)
    c = lax.broadcasted_iota(I32, (chunk, chunk), 1)
    causal = c <= r
    local_sum = (causal & ((r // GLA_SUB) == (c // GLA_SUB))).astype(BF16)
    sub_rows = [slice(j * GLA_SUB, (j + 1) * GLA_SUB) for j in range(n_sub)]
    eye = lax.broadcasted_iota(I32, (dk, dk), 0) == lax.broadcasted_iota(I32, (dk, dk), 1)
    for s in range(n_seq):
        for t0 in range(0, seq_rows, chunk):
            rows = slice(s * seq_rows + t0, s * seq_rows + t0 + chunk)
            la_1 = la[rows, :].astype(BF16)
            rest = la[rows, :] - la_1.astype(F32)
            la_2 = rest.astype(BF16)
            la_3 = (rest - la_2.astype(F32)).astype(BF16)
            sums = jnp.dot(local_sum, jnp.concatenate([la_1, la_2, la_3], axis=1), preferred_element_type=F32)
            local = sums[:, 0:qk] + sums[:, qk:2 * qk] + sums[:, 2 * qk:3 * qk]
            bases = [jnp.zeros((1, qk), F32)]
            for j in range(1, n_sub):
                bases.append(bases[-1] + local[j * GLA_SUB - 1:j * GLA_SUB, :])
            base = jnp.concatenate([jnp.broadcast_to(bs, (GLA_SUB, qk)) for bs in bases], axis=0)
            b = base + local
            b_end = b[chunk - 1:chunk, :]
            q_c, k_c = q[rows, :], k[rows, :]
            q_loc = q_c * jnp.exp(local)
            k_loc = k_c * jnp.exp(-local)
            q_in = (q_c * jnp.exp(b)).astype(BF16)
            k_out = (k_c * jnp.exp(b_end - b)).astype(BF16)
            decay_row = jnp.exp(b_end)
            zero_rows = jnp.zeros((GLA_SUB, qk), F32)
            q_parts, k_parts = [], []
            for j in range(n_sub):
                q_parts.append(jnp.concatenate(
                    [q_loc[sub_rows[m], :] * jnp.exp(jnp.minimum(bases[m] - bases[j], 0.0)) for m in range(n_sub)],
                    axis=0).astype(BF16))
                k_parts.append(jnp.concatenate(
                    [k_loc[sub_rows[m], :] if m == j else zero_rows for m in range(n_sub)], axis=0).astype(BF16))
            for hd in range(heads):
                ks = slice(hd * dk, (hd + 1) * dk)
                vs = slice(hd * dv, (hd + 1) * dv)
                vh = v[rows, vs]
                q_cat = jnp.concatenate([p[:, ks] for p in q_parts], axis=1)
                k_cat = jnp.concatenate([p[:, ks] for p in k_parts], axis=1)
                att = lax.dot_general(q_cat, k_cat, (((1,), (1,)), ((), ())), preferred_element_type=F32)
                att = jnp.where(causal, att, 0.0).astype(BF16)
                s_h = state[s, hd]
                o = jnp.dot(att, vh, preferred_element_type=F32)
                o = o + jnp.dot(q_in[:, ks], s_h.astype(BF16), preferred_element_type=F32)
                decay_col = jnp.sum(jnp.where(eye, jnp.broadcast_to(decay_row[:, ks], (dk, dk)), 0.0),
                                    axis=1, keepdims=True)
                state[s, hd] = decay_col * s_h + lax.dot_general(k_out[:, ks], vh, (((0,), (0,)), ((), ())),
                                                                 preferred_element_type=F32)
                o = o * lax.rsqrt(jnp.mean(o * o, axis=-1, keepdims=True) + EPS) * gn_ref[...]
                o_ref[rows, vs] = (o * _silu(g[rows, vs])).astype(o_ref.dtype)

    @pl.when(i == pl.num_programs(1) - 1)
    def _():
        for s in range(n_seq):
            tail_ref[s] = win[s, seq_rows:seq_rows + HIST_PAD, :]
        sout_ref[...] = state[...]


def _front(x2d, hist_pad, s0, g1, w_in, w_lr2, blr2, w_dw, b_dw, ln_g, ln_b, gn, *, n_seq, seq_rows,
           zero_shape=None):
    bsz, heads, dk, dv = s0.shape
    d = x2d.shape[1]
    t = x2d.shape[0] // bsz
    c_conv = w_dw.shape[1]
    width = w_dw.shape[0]
    qk, vv = heads * dk, heads * dv
    assert (n_seq == 1 and t % seq_rows == 0) or (seq_rows == t and bsz % n_seq == 0)
    nt = t // seq_rows
    rows = n_seq * seq_rows
    chunk = 64 if seq_rows % 64 == 0 else seq_rows
    assert chunk % GLA_SUB == 0
    const = lambda b, i: (0, 0)
    tile = lambda b, i: (b * nt + i, 0)
    per_seq3 = lambda b, i: (b, 0, 0)
    per_seq4 = lambda b, i: (b, 0, 0, 0)
    extra_out_specs, extra_out_shape, extra_scratch, zero_chunks = [], [], [], None
    if zero_shape is not None:
        assert zero_shape[0] % EXPERT_ROWS == 0
        n_chunks = zero_shape[0] // EXPERT_ROWS
        zero_chunks = (n_chunks, -(-n_chunks // ((bsz // n_seq) * nt)))
        extra_out_specs = [pl.BlockSpec(memory_space=pl.ANY)]
        extra_out_shape = [jax.ShapeDtypeStruct(zero_shape, U32)]
        extra_scratch = [pltpu.VMEM((EXPERT_ROWS, zero_shape[1]), U32), pltpu.SemaphoreType.DMA(())]
    return pl.pallas_call(
        functools.partial(_front_kernel, n_seq=n_seq, seq_rows=seq_rows, chunk=chunk, heads=heads, dk=dk, dv=dv,
                          c_conv=c_conv, width=width, zero_chunks=zero_chunks),
        grid=(bsz // n_seq, nt),
        in_specs=[
            pl.BlockSpec((rows, d), tile),
            pl.BlockSpec((n_seq, HIST_PAD, c_conv), per_seq3),
            pl.BlockSpec((n_seq, heads, dk, dv), per_seq4),
            pl.BlockSpec((1, d), const),
            pl.BlockSpec(w_in.shape, const, pipeline_mode=pl.Buffered(1)),
            pl.BlockSpec(w_lr2.shape, const),
            pl.BlockSpec((1, qk), const),
            pl.BlockSpec(w_dw.shape, const),
            pl.BlockSpec((1, c_conv), const), pl.BlockSpec((1, c_conv), const), pl.BlockSpec((1, c_conv), const),
            pl.BlockSpec((1, dv), const),
        ],
        out_specs=[
            pl.BlockSpec((rows, c_conv), tile), pl.BlockSpec((rows, vv), tile),
            pl.BlockSpec((n_seq, HIST_PAD, c_conv), per_seq3),
            pl.BlockSpec((n_seq, heads, dk, dv), per_seq4),
        ] + extra_out_specs,
        out_shape=[
            jax.ShapeDtypeStruct((bsz * t, c_conv), BF16), jax.ShapeDtypeStruct((bsz * t, vv), BF16),
            jax.ShapeDtypeStruct((bsz, HIST_PAD, c_conv), F32),
            jax.ShapeDtypeStruct((bsz, heads, dk, dv), F32),
        ] + extra_out_shape,
        scratch_shapes=[
            pltpu.VMEM((n_seq, seq_rows + HIST_PAD, c_conv), F32),
            pltpu.VMEM((7, seq_rows + HIST_PAD - 8, LANES), F32),
            pltpu.VMEM((rows, c_conv), F32),
            pltpu.VMEM((n_seq, heads, dk, dv), F32),
        ] + extra_scratch,
        compiler_params=_params(("arbitrary", "arbitrary")),
        name="front",
    )(x2d, hist_pad, s0, g1, w_in, w_lr2, blr2, w_dw, b_dw, ln_g, ln_b, gn)


def _outproj_kernel(xp_ref, xs_ref, cp_ref, cs_ref, op_ref, os_ref, w_ref, g2_ref, wr_ref, br_ref, zeroed_ref,
                    x1_ref, dest_ref, gates_ref, pages_ref, counts_ref, sorted_ref,
                    stage, dest_vmem, dest_smem, cnt_s, page_s, npage_s, table_s, scatter_sems, dest_sems,
                    *, n_first, c_conv, n_groups, per_group, trash_row):
    i = pl.program_id(0)
    n_steps = pl.num_programs(0)
    tm = x1_ref.shape[0]
    n_exp = n_groups * per_group
    slot = i % 2
    prev = 1 - slot

    def dest_copy(s):
        return pltpu.make_async_copy(dest_vmem.at[s], dest_smem.at[s], dest_sems.at[s])

    def scatter_wait(s):
        for _ in range(2):
            pltpu.make_async_copy(stage.at[s], sorted_ref.at[pl.ds(0, tm)], scatter_sems.at[s]).wait()

    @pl.when(i == 0)
    def _():
        cnt_s[...] = jnp.zeros_like(cnt_s)
        page_s[...] = jnp.zeros_like(page_s)
        npage_s[...] = jnp.zeros_like(npage_s)
        table_s[...] = jnp.zeros_like(table_s)
        stage[1] = jnp.zeros(stage.shape[1:], stage.dtype)

        def fill(t, carry):
            dest_smem[1, 0, t] = trash_row + t
            dest_smem[1, 1, t] = trash_row + tm + t
            return carry

        lax.fori_loop(0, tm, fill, 0)

    @pl.when(i >= 1)
    def _():
        dest_copy(prev).wait()
        scatter_wait(slot)

    for t in range(tm):
        _row_copy(stage.at[prev], t, sorted_ref, dest_smem[prev, 0, t], scatter_sems.at[prev]).start()
        _row_copy(stage.at[prev], t, sorted_ref, dest_smem[prev, 1, t], scatter_sems.at[prev]).start()

    x = _pick(i, n_first, xp_ref, xs_ref)
    cc = _pick(i, n_first, cp_ref, cs_ref)
    oo = _pick(i, n_first, op_ref, os_ref)
    del zeroed_ref
    mix = jnp.dot(jnp.concatenate([cc, oo], axis=1), w_ref[...], preferred_element_type=F32)
    x1 = x + mix
    x1_ref[...] = x1
    h2 = _rms(x1, g2_ref[...])
    stage[slot] = _pack_pairs(h2)
    h_hi = h2.astype(BF16)
    h_lo = (h2 - h_hi.astype(F32)).astype(BF16)
    parts = lax.dot_general(wr_ref[...], jnp.concatenate([h_hi, h_lo], axis=0), (((1,), (1,)), ((), ())),
                            preferred_element_type=F32)
    n_r = br_ref.shape[0]
    logits = (parts[0:n_r, 0:tm] + parts[0:n_r, tm:] + parts[n_r:, 0:tm] + parts[n_r:, tm:]) + br_ref[...]
    lc = logits[0:n_groups, :]
    mc = jnp.max(lc, axis=0, keepdims=True)
    p_group = 1.0 / jnp.sum(jnp.exp(lc - mc), axis=0, keepdims=True)
    rows_c = lax.broadcasted_iota(I32, (n_groups, tm), 0)
    g_idx = jnp.min(jnp.where(lc == mc, rows_c, n_groups), axis=0, keepdims=True)
    lf = logits[n_groups:n_groups + n_exp, :]
    rows_f = lax.broadcasted_iota(I32, (n_exp, tm), 0)
    in_group = (rows_f >= g_idx * per_group) & (rows_f < (g_idx + 1) * per_group)
    neg = jnp.float32(-jnp.inf)
    l1 = jnp.where(in_group, lf, neg)
    m1 = jnp.max(l1, axis=0, keepdims=True)
    e1 = jnp.min(jnp.where(l1 == m1, rows_f, n_exp), axis=0, keepdims=True)
    l2 = jnp.where(rows_f == e1, neg, l1)
    m2 = jnp.max(l2, axis=0, keepdims=True)
    e2 = jnp.min(jnp.where(l2 == m2, rows_f, n_exp), axis=0, keepdims=True)
    r2 = jnp.exp(m2 - m1)
    w1 = 1.0 / (1.0 + r2)
    row8 = lax.broadcasted_iota(I32, (8, tm), 0)
    gates_ref[...] = jnp.where(row8 == 0, p_group * w1, jnp.where(row8 == 1, p_group * (r2 * w1), 0.0))

    oh0 = (rows_f == e1).astype(F32)
    oh1 = (rows_f == e2).astype(F32)
    both = oh0 + oh1
    tr = lax.broadcasted_iota(I32, (tm, tm), 0)
    tc = lax.broadcasted_iota(I32, (tm, tm), 1)
    earlier = jnp.dot(both.astype(BF16), (tr < tc).astype(BF16), preferred_element_type=F32)
    cnt = cnt_s[...]
    rank_base = earlier + cnt
    tile_cnt = jnp.sum(both, axis=1, keepdims=True)
    page_rows = float(EXPERT_ROWS)
    k0 = jnp.floor(cnt * (1.0 / page_rows))
    new_cnt = cnt + tile_cnt
    limit = (k0 + 1.0) * page_rows
    need_a = ((cnt == k0 * page_rows) & (tile_cnt > 0.0)).astype(F32)
    need_b = (new_cnt > limit).astype(F32)
    need = need_a + need_b
    er = lax.broadcasted_iota(I32, (n_exp, n_exp), 0)
    ec = lax.broadcasted_iota(I32, (n_exp, n_exp), 1)
    before = jnp.dot((ec < er).astype(BF16), jnp.broadcast_to(need, (n_exp, LANES)).astype(BF16),
                     preferred_element_type=F32)[:, 0:1]
    base = npage_s[...] + before
    page_a = jnp.where(need_a > 0.0, base, page_s[...])
    page_b = base + need_a
    npage_s[...] = npage_s[...] + jnp.sum(need, axis=0, keepdims=True)
    lane = lax.broadcasted_iota(I32, table_s.shape, 1).astype(F32)
    table = jnp.where((lane == k0) & (need_a > 0.0), page_a, table_s[...])
    table_s[...] = jnp.where((lane == k0 + 1.0) & (need_b > 0.0), page_b, table)
    cnt_s[...] = new_cnt
    page_s[...] = jnp.where(jnp.floor(new_cnt * (1.0 / page_rows)) == k0, page_a, page_b)

    def dest_rows(oh):
        rank = jnp.sum(oh * rank_base, axis=0, keepdims=True)
        lim = jnp.sum(oh * limit, axis=0, keepdims=True)
        pa = jnp.sum(oh * page_a, axis=0, keepdims=True)
        pb = jnp.sum(oh * page_b, axis=0, keepdims=True)
        within = rank - jnp.floor(rank * (1.0 / page_rows)) * page_rows
        return jnp.where(rank < lim, pa, pb) * page_rows + within

    dest = jnp.where(row8 == 0, dest_rows(oh0), jnp.where(row8 == 1, dest_rows(oh1), 0.0)).astype(I32)
    dest_ref[...] = dest
    dest_vmem[slot] = dest
    dest_copy(slot).start()

    @pl.when(i == n_steps - 1)
    def _():
        pages_ref[...] = table_s[...].astype(I32)
        counts_ref[...] = jnp.broadcast_to(cnt_s[...], counts_ref.shape).astype(I32)
        dest_copy(slot).wait()
        scatter_wait(prev)

        def last(j, carry):
            for r in range(DMA_UNROLL):
                t = j * DMA_UNROLL + r
                _row_copy(stage.at[slot], t, sorted_ref, dest_smem[slot, 0, t], scatter_sems.at[slot]).start()
                _row_copy(stage.at[slot], t, sorted_ref, dest_smem[slot, 1, t], scatter_sems.at[slot]).start()
            return carry

        lax.fori_loop(0, tm // DMA_UNROLL, last, 0)
        scatter_wait(slot)


def _outproj(xp, xs, cp, cs, op, os_, w_out, g2, wr, br, zeroed, *, n_groups, per_group, n_pages):
    n_p, d = xp.shape
    n_all = n_p + xs.shape[0]
    n_first = n_p // ROW_TILE
    n_exp = n_groups * per_group
    c_conv = cp.shape[1]
    vv = op.shape[1]
    tile = (d // 2,)
    first, second = _split_maps(n_first)
    const = lambda i: (0, 0)
    row = lambda i: (i, 0)
    col = lambda i: (0, i)
    rows_sorted = zeroed.shape[0]
    assert rows_sorted >= n_pages * EXPERT_ROWS + 2 * ROW_TILE
    assert ROW_TILE <= EXPERT_ROWS, "a tile may open at most two pages per expert"
    return pl.pallas_call(
        functools.partial(_outproj_kernel, n_first=n_first, c_conv=c_conv, n_groups=n_groups, per_group=per_group,
                          trash_row=n_pages * EXPERT_ROWS),
        grid=(n_all // ROW_TILE,),
        in_specs=[
            pl.BlockSpec((ROW_TILE, d), first), pl.BlockSpec((ROW_TILE, d), second),
            pl.BlockSpec((ROW_TILE, c_conv), first), pl.BlockSpec((ROW_TILE, c_conv), second),
            pl.BlockSpec((ROW_TILE, vv), first), pl.BlockSpec((ROW_TILE, vv), second),
            pl.BlockSpec(w_out.shape, const, pipeline_mode=pl.Buffered(1)),
            pl.BlockSpec((1, d), const),
            pl.BlockSpec(wr.shape, const),
            pl.BlockSpec(br.shape, const),
            pl.BlockSpec(memory_space=pl.ANY),
        ],
        out_specs=[
            pl.BlockSpec((ROW_TILE, d), row),
            pl.BlockSpec((8, ROW_TILE), col), pl.BlockSpec((8, ROW_TILE), col),
            pl.BlockSpec((n_exp, LANES), const), pl.BlockSpec((n_exp, LANES), const),
            pl.BlockSpec(memory_space=pl.ANY),
        ],
        out_shape=[
            jax.ShapeDtypeStruct((n_all, d), F32),
            jax.ShapeDtypeStruct((8, n_all), I32), jax.ShapeDtypeStruct((8, n_all), F32),
            jax.ShapeDtypeStruct((n_exp, LANES), I32), jax.ShapeDtypeStruct((n_exp, LANES), I32),
            jax.ShapeDtypeStruct((rows_sorted,) + tile, U32),
        ],
        scratch_shapes=[
            pltpu.VMEM((2, ROW_TILE) + tile, U32),
            pltpu.VMEM((2, 8, ROW_TILE), I32), pltpu.SMEM((2, 8, ROW_TILE), I32),
            pltpu.VMEM((n_exp, 1), F32), pltpu.VMEM((n_exp, 1), F32), pltpu.VMEM((1, 1), F32),
            pltpu.VMEM((n_exp, LANES), F32),
            pltpu.SemaphoreType.DMA((2,)), pltpu.SemaphoreType.DMA((2,)),
        ],
        compiler_params=_params(("arbitrary",)),
        input_output_aliases={10: 5},
        name="outproj",
    )(xp, xs, cp, cs, op, os_, w_out, g2, wr, br, zeroed)


def _experts_kernel(cnt_ref, pages_ref, xs_ref, wg_ref, wu_ref, wd_ref, ysp_ref,
                    xbuf, ybuf, wg_f32, wu_f32, wd_f32, wg_bf, wu_bf, wd_bf, first_blk, page_seq,
                    gsems, ysems, wsems, *, n_exp, table_lanes):
    e = pl.program_id(0)
    tb = xbuf.shape[1]
    n_pages = page_seq.shape[0]

    def n_pages_of(ex):
        return (cnt_ref[ex] + (tb - 1)) // tb

    def page_rows(blk):
        return pl.ds(pl.multiple_of(page_seq[blk] * tb, tb), tb)

    def fetch(blk, slot):
        return pltpu.make_async_copy(xs_ref.at[page_rows(blk)], xbuf.at[slot], gsems.at[slot])

    def writeback(blk, slot):
        return pltpu.make_async_copy(ybuf.at[slot], ysp_ref.at[page_rows(blk)], ysems.at[slot])

    def weight_copies(ex, slot):
        return (pltpu.make_async_copy(wg_ref.at[ex], wg_f32.at[slot], wsems.at[slot]),
                pltpu.make_async_copy(wu_ref.at[ex], wu_f32.at[slot], wsems.at[slot]),
                pltpu.make_async_copy(wd_ref.at[ex], wd_f32.at[slot], wsems.at[slot]))

    @pl.when(e == 0)
    def _():
        for cp in weight_copies(0, 0):
            cp.start(priority=1)

        def per_expert(ex, blk):
            first_blk[ex] = blk

            def per_page(j, carry):
                page_seq[blk + j] = pages_ref[ex * table_lanes + j]
                return carry

            lax.fori_loop(0, n_pages_of(ex), per_page, 0)
            return blk + n_pages_of(ex)

        first_blk[n_exp] = lax.fori_loop(0, n_exp, per_expert, 0)
        fetch(0, 0).start()

    @pl.when(e + 1 < n_exp)
    def _():
        for cp in weight_copies(e + 1, (e + 1) % 2):
            cp.start(priority=1)

    b_lo = first_blk[e]
    b_hi = first_blk[e + 1]
    n_total = first_blk[n_exp]
    wslot = e % 2
    for cp in weight_copies(e, wslot):
        cp.wait()
    wg_bf[...] = wg_f32[wslot].astype(BF16)
    wu_bf[...] = wu_f32[wslot].astype(BF16)
    wd_bf[...] = wd_f32[wslot].astype(BF16)

    def block(b, carry):
        slot = b % 2

        @pl.when(b >= 2)
        def _():
            writeback(b, slot).wait()

        fetch(b, slot).wait()
        fetch(jnp.minimum(b + 1, n_total - 1), 1 - slot).start()
        hi, lo = _unpack_pairs(xbuf[slot])
        x = jnp.concatenate([hi.astype(BF16), lo.astype(BF16)], axis=1)
        hg = jnp.dot(x, wg_bf[...], preferred_element_type=F32)
        hu = jnp.dot(x, wu_bf[...], preferred_element_type=F32)
        hb = (_silu(hg) * hu).astype(BF16)
        ybuf[slot] = _pack_pairs(jnp.dot(hb, wd_bf[...], preferred_element_type=F32))
        writeback(b, slot).start()
        return carry

    lax.fori_loop(b_lo, b_hi, block, 0)

    @pl.when(e == n_exp - 1)
    def _():
        fetch(0, n_total % 2).wait()

        @pl.when(n_total >= 2)
        def _():
            writeback(0, n_total % 2).wait()

        writeback(0, (n_total + 1) % 2).wait()
        ybuf[0] = jnp.zeros(ybuf.shape[1:], ybuf.dtype)

        def spare(blk):
            return pltpu.make_async_copy(ybuf.at[0], ysp_ref.at[pl.ds(pl.multiple_of(blk * tb, tb), tb)], ysems.at[0])

        def zero(blk, carry):
            spare(blk).start()
            return carry

        lax.fori_loop(n_total, n_pages, zero, 0)

        def zero_wait(blk, carry):
            spare(0).wait()
            return carry

        lax.fori_loop(n_total, n_pages, zero_wait, 0)


def _experts(counts, pages_flat, xs_sorted, w_gate, w_up, w_down, *, n_pages, table_lanes):
    tile = xs_sorted.shape[1:]
    n_exp, d, ff = w_gate.shape
    anyspec = pl.BlockSpec(memory_space=pl.ANY)
    grid_spec = pltpu.PrefetchScalarGridSpec(
        num_scalar_prefetch=2,
        grid=(n_exp,),
        in_specs=[anyspec, anyspec, anyspec, anyspec],
        out_specs=anyspec,
        scratch_shapes=[
            pltpu.VMEM((2, EXPERT_ROWS) + tile, U32), pltpu.VMEM((2, EXPERT_ROWS) + tile, U32),
            pltpu.VMEM((2, d, ff), F32), pltpu.VMEM((2, d, ff), F32), pltpu.VMEM((2, ff, d), F32),
            pltpu.VMEM((d, ff), BF16), pltpu.VMEM((d, ff), BF16), pltpu.VMEM((ff, d), BF16),
            pltpu.SMEM((n_exp + 1,), I32), pltpu.SMEM((n_pages,), I32),
            pltpu.SemaphoreType.DMA((2,)), pltpu.SemaphoreType.DMA((2,)), pltpu.SemaphoreType.DMA((2,)),
        ],
    )
    return pl.pallas_call(
        functools.partial(_experts_kernel, n_exp=n_exp, table_lanes=table_lanes),
        grid_spec=grid_spec,
        out_shape=jax.ShapeDtypeStruct((n_pages * EXPERT_ROWS,) + tile, U32),
        compiler_params=_params(("arbitrary",)),
        name="experts",
    )(counts, pages_flat, xs_sorted, w_gate, w_up, w_down)


def _combine_kernel(dest_ref, dest_next_ref, gates_ref, x1_ref, ysp_ref, gf_ref, yp_ref, ysmp_ref,
                    buf0, buf1, sems, *, n_first):
    i = pl.program_id(0)
    n = pl.num_programs(0)
    tm = x1_ref.shape[0]
    slot = i % 2

    def gather(d_ref, s):
        def body(j, carry):
            for r in range(DMA_UNROLL):
                t = j * DMA_UNROLL + r
                _row_copy(ysp_ref, d_ref[0, t], buf0.at[s], t, sems.at[s]).start(priority=0)
                _row_copy(ysp_ref, d_ref[1, t], buf1.at[s], t, sems.at[s]).start(priority=1)
            return carry

        lax.fori_loop(0, tm // DMA_UNROLL, body, 0)

    @pl.when(i == 0)
    def _():
        gather(dest_ref, 0)

    @pl.when(i + 1 < n)
    def _():
        gather(dest_next_ref, 1 - slot)

    pltpu.make_async_copy(ysp_ref.at[pl.ds(0, tm)], buf0.at[slot], sems.at[slot]).wait()
    pltpu.make_async_copy(ysp_ref.at[pl.ds(0, tm)], buf1.at[slot], sems.at[slot]).wait()
    hi0, lo0 = _unpack_pairs(buf0[slot])
    hi1, lo1 = _unpack_pairs(buf1[slot])
    g0 = gates_ref[:, 0:1]
    g1 = gates_ref[:, 1:2]
    moe = jnp.concatenate([g0 * hi0 + g1 * hi1, g0 * lo0 + g1 * lo1], axis=1)
    y = _rms(x1_ref[...] + moe, gf_ref[...])

    @pl.when(i < n_first)
    def _():
        yp_ref[...] = y

    @pl.when(i >= n_first)
    def _():
        ysmp_ref[...] = y


def _combine(dest, gates_t, x1, ysp, gf, *, n_p):
    n_all, d = x1.shape
    tile = ysp.shape[1:]
    n_first = n_p // ROW_TILE
    n_tiles = n_all // ROW_TILE
    first, second = _split_maps(n_first)
    return pl.pallas_call(
        functools.partial(_combine_kernel, n_first=n_first),
        grid=(n_tiles,),
        in_specs=[
            pl.BlockSpec((8, ROW_TILE), lambda i: (0, i), memory_space=pltpu.SMEM),
            pl.BlockSpec((8, ROW_TILE), lambda i: (0, jnp.minimum(i + 1, n_tiles - 1)), memory_space=pltpu.SMEM),
            pl.BlockSpec((ROW_TILE, 8), lambda i: (i, 0)),
            pl.BlockSpec((ROW_TILE, d), lambda i: (i, 0)),
            pl.BlockSpec(memory_space=pl.ANY),
            pl.BlockSpec((1, d), lambda i: (0, 0)),
        ],
        out_specs=[pl.BlockSpec((ROW_TILE, d), first), pl.BlockSpec((ROW_TILE, d), second)],
        out_shape=[jax.ShapeDtypeStruct((n_p, d), F32), jax.ShapeDtypeStruct((n_all - n_p, d), F32)],
        scratch_shapes=[pltpu.VMEM((2, ROW_TILE) + tile, U32), pltpu.VMEM((2, ROW_TILE) + tile, U32),
                        pltpu.SemaphoreType.DMA((2,))],
        compiler_params=_params(("arbitrary",)),
        name="combine",
    )(dest, dest, gates_t, x1, ysp, gf)


def _chunk_for(t):
    return 64 if t % 64 == 0 else t


def kernel(x_prompt, x_sample, cache_conv, state_gla, norm1_g, w_in, w_lr2, b_lr2, w_dw, b_dw, conv_ln_g, conv_ln_b, gla_norm_g, w_out, norm2_g, w_router_coarse, b_router_coarse, w_router_fine, b_router_fine, w_exp_gate, w_exp_up, w_exp_down, norm_f_g):
    assert norm1_g.shape[0] == 1, "single trunk layer"
    bp, tp, d = x_prompt.shape
    bs, ts, _ = x_sample.shape
    heads, dk, dv = state_gla.shape[2:]
    c_conv = w_dw.shape[2]
    width = w_dw.shape[1]
    rank = w_lr2.shape[1]
    qk, vv = heads * dk, heads * dv
    n_groups, _, per_group = w_router_fine.shape[1:]
    n_exp = n_groups * per_group
    n_p, n_s = bp * tp, bs * ts
    n_s_pad = -(-n_s // ROW_TILE) * ROW_TILE
    pad_rows = lambda a: jnp.pad(a, ((0, n_s_pad - n_s), (0, 0)))
    n_all = n_p + n_s_pad
    assert n_p % ROW_TILE == 0 and width - 1 <= HIST_PAD

    xp = x_prompt.reshape(n_p, d)
    xs = pad_rows(x_sample.reshape(n_s, d))
    row = lambda a: a.reshape(1, -1)

    mixer_args = (row(norm1_g[0]), w_in[0].astype(BF16), w_lr2[0], row(b_lr2[0]),
                  w_dw[0], row(b_dw[0]), row(conv_ln_g[0]), row(conv_ln_b[0]), row(gla_norm_g[0]))
    hist_p = jnp.zeros((bp, HIST_PAD, c_conv), F32)
    hist_s = jnp.pad(cache_conv[0], ((0, 0), (HIST_PAD - (width - 1), 0), (0, 0)))
    s0_p = jnp.zeros((bp, heads, dk, dv), F32)
    n_pages = (2 * n_all) // EXPERT_ROWS + n_exp
    sorted_rows = n_pages * EXPERT_ROWS + -(-2 * ROW_TILE // EXPERT_ROWS) * EXPERT_ROWS
    c_p, o_p, tail_p, gla_p, zeroed = _front(xp, hist_p, s0_p, *mixer_args, n_seq=1, seq_rows=ROW_TILE,
                                             zero_shape=(sorted_rows, d // 2))
    c_s, o_s, tail_s, gla_s = _front(x_sample.reshape(n_s, d), hist_s, state_gla[0], *mixer_args,
                                     n_seq=bs, seq_rows=ts)
    c_s, o_s = pad_rows(c_s), pad_rows(o_s)

    wr = jnp.concatenate([w_router_coarse[0].T,
                          jnp.transpose(w_router_fine[0], (0, 2, 1)).reshape(n_exp, d)], axis=0)
    br = jnp.concatenate([b_router_coarse[0], b_router_fine[0].reshape(n_exp)])
    r_rows = -(-(n_groups + n_exp) // 8) * 8
    wr = jnp.pad(wr, ((0, r_rows - wr.shape[0]), (0, 0)))
    br = jnp.pad(br, (0, r_rows - br.shape[0])).reshape(r_rows, 1)
    wr_hi = wr.astype(BF16)
    wr = jnp.concatenate([wr_hi, (wr - wr_hi.astype(F32)).astype(BF16)], axis=0)
    n_pages = (2 * n_all) // EXPERT_ROWS + n_exp
    assert n_all // EXPERT_ROWS + 2 <= LANES, "page table row must hold one expert's pages"
    x1, dest, gates, pages, counts, xs_sorted = _outproj(
        xp, xs, c_p, c_s, o_p, o_s, w_out[0].astype(BF16), row(norm2_g[0]), wr, br, zeroed,
        n_groups=n_groups, per_group=per_group, n_pages=n_pages)
    ysp = _experts(counts[:, 0], pages.reshape(-1), xs_sorted, w_exp_gate[0], w_exp_up[0], w_exp_down[0],
                   n_pages=n_pages, table_lanes=LANES)
    y_p, y_s = _combine(dest, gates.T, x1, ysp, row(norm_f_g), n_p=n_p)

    lead = HIST_PAD - (width - 1)
    return (y_p.reshape(bp, tp, d), y_s[:n_s].reshape(bs, ts, d), tail_p[:, lead:][None], gla_p[None],
            tail_s[:, lead:][None], gla_s[None])
```

```python
import functools

import jax
import jax.numpy as jnp
from jax import lax
from jax.experimental import pallas as pl
from jax.experimental.pallas import tpu as pltpu

F32 = jnp.float32
BF16 = jnp.bfloat16
I32 = jnp.int32
U32 = jnp.uint32
EPS = 1e-6
GATE_TEMP = 16.0
HIGHEST = lax.Precision.HIGHEST

LANES = 128
ROW_TILE = 256
EXPERT_ROWS = 256
HIST_PAD = 32
DMA_UNROLL = 8
VMEM_LIMIT = 56 * 1024 * 1024


def _params(semantics, vmem=VMEM_LIMIT):
    return pltpu.CompilerParams(dimension_semantics=semantics, vmem_limit_bytes=vmem)


def _rms(x, g):
    return x * lax.rsqrt(jnp.mean(x * x, axis=-1, keepdims=True) + EPS) * g


def _silu(x):
    return x * jax.nn.sigmoid(x)


def _log_sigmoid(z):
    return jnp.minimum(z, 0.0) - jnp.log(1.0 + jnp.exp(-jnp.abs(z)))


def _pick(i, n_first, first_ref, second_ref):
    return jnp.where(i < n_first, first_ref[...], second_ref[...])


def _split_maps(n_first):
    first = lambda i: (jnp.minimum(i, n_first - 1), 0)
    second = lambda i: (jnp.maximum(i - n_first, 0), 0)
    return first, second


def _pack_pairs(x):
    half = x.shape[1] // 2
    hi = lax.bitcast_convert_type(x[:, :half].astype(BF16).astype(F32), U32)
    lo = lax.bitcast_convert_type(x[:, half:].astype(BF16).astype(F32), U32)
    return hi | (lo >> 16)


def _unpack_pairs(p):
    hi = lax.bitcast_convert_type(p & jnp.uint32(0xFFFF0000), F32)
    lo = lax.bitcast_convert_type(p << 16, F32)
    return hi, lo


def _row_copy(src, s, dst, d, sem):
    return pltpu.make_async_copy(src.at[pl.ds(s, 1)], dst.at[pl.ds(d, 1)], sem)


def _inproj_kernel(xp_ref, xs_ref, g1_ref, w_ref, wlr2_ref, blr2_ref,
                   u_ref, q_ref, k_ref, v_ref, g_ref, la_ref, *, n_first, c_conv, qk, vv, dk):
    i = pl.program_id(0)
    x = _pick(i, n_first, xp_ref, xs_ref)
    h = _rms(x, g1_ref[...]).astype(BF16)

    def mm(lo, width):
        return jnp.dot(h, w_ref[:, lo:lo + width], preferred_element_type=F32)

    a = mm(0, c_conv)
    a_gate = mm(c_conv, c_conv)
    u_ref[...] = a * jax.nn.sigmoid(a_gate)
    off = 2 * c_conv
    q_ref[...] = mm(off, qk) * (dk ** -0.5)
    k_ref[...] = mm(off + qk, qk)
    v_ref[...] = mm(off + 2 * qk, vv)
    g_ref[...] = mm(off + 2 * qk + vv, vv)
    lr = mm(off + 2 * qk + 2 * vv, LANES)
    z = jnp.dot(lr, wlr2_ref[...], precision=HIGHEST, preferred_element_type=F32) + blr2_ref[...]
    la_ref[...] = _log_sigmoid(z) * (1.0 / GATE_TEMP)


def _inproj(xp, xs, g1, w_pad, wlr2_pad, blr2, *, c_conv, qk, vv, dk):
    n_p, d = xp.shape
    n_s = xs.shape[0]
    n_all = n_p + n_s
    n_first = n_p // ROW_TILE
    grid = (n_all // ROW_TILE,)
    first, second = _split_maps(n_first)
    const = lambda i: (0, 0)
    row = lambda i: (i, 0)
    widths = (c_conv, qk, qk, vv, vv, qk)
    return pl.pallas_call(
        functools.partial(_inproj_kernel, n_first=n_first, c_conv=c_conv, qk=qk, vv=vv, dk=dk),
        grid=grid,
        in_specs=[
            pl.BlockSpec((ROW_TILE, d), first),
            pl.BlockSpec((ROW_TILE, d), second),
            pl.BlockSpec((1, d), const),
            pl.BlockSpec(w_pad.shape, const, pipeline_mode=pl.Buffered(1)),
            pl.BlockSpec(wlr2_pad.shape, const),
            pl.BlockSpec((1, qk), const),
        ],
        out_specs=[pl.BlockSpec((ROW_TILE, w), row) for w in widths],
        out_shape=[jax.ShapeDtypeStruct((n_all, w), F32) for w in widths],
        compiler_params=_params(("arbitrary",)),
        name="inproj",
    )(xp, xs, g1, w_pad, wlr2_pad, blr2)


def _conv_kernel(u_ref, hist_ref, w_ref, b_ref, lg_ref, lb_ref, c_ref, win, cbuf, *, tt, width):
    i = pl.program_id(1)

    @pl.when(i == 0)
    def _():
        win[0:HIST_PAD, :] = hist_ref[0]

    @pl.when(i > 0)
    def _():
        win[0:HIST_PAD, :] = win[tt:tt + HIST_PAD, :]

    win[HIST_PAD:HIST_PAD + tt, :] = u_ref[...]
    lead = HIST_PAD - (width - 1)
    n_ch = u_ref.shape[1]
    for cb in range(n_ch // LANES):
        cs = slice(cb * LANES, (cb + 1) * LANES)
        acc = jnp.broadcast_to(b_ref[:, cs], (tt, LANES))
        for j in range(width):
            acc = acc + w_ref[j:j + 1, cs] * win[lead + j:lead + j + tt, cs]
        cbuf[:, cs] = acc
    c = cbuf[...]
    mu = jnp.mean(c, axis=-1, keepdims=True)
    xc = c - mu
    y = xc * lax.rsqrt(jnp.mean(xc * xc, axis=-1, keepdims=True) + EPS) * lg_ref[...] + lb_ref[...]
    c_ref[...] = _silu(y).astype(c_ref.dtype)


def _conv(u_all, hist_pad, w_dw, b_dw, ln_g, ln_b, *, row0, bsz, t, tt):
    n_ch = u_all.shape[1]
    width = w_dw.shape[0]
    nt = t // tt
    blk0 = row0 // tt
    const = lambda b, i: (0, 0)
    return pl.pallas_call(
        functools.partial(_conv_kernel, tt=tt, width=width),
        grid=(bsz, nt),
        in_specs=[
            pl.BlockSpec((tt, n_ch), lambda b, i: (blk0 + b * nt + i, 0)),
            pl.BlockSpec((1, HIST_PAD, n_ch), lambda b, i: (b, 0, 0)),
            pl.BlockSpec(w_dw.shape, const),
            pl.BlockSpec((1, n_ch), const),
            pl.BlockSpec((1, n_ch), const),
            pl.BlockSpec((1, n_ch), const),
        ],
        out_specs=pl.BlockSpec((tt, n_ch), lambda b, i: (b * nt + i, 0)),
        out_shape=jax.ShapeDtypeStruct((bsz * t, n_ch), BF16),
        scratch_shapes=[pltpu.VMEM((tt + HIST_PAD, n_ch), F32), pltpu.VMEM((tt, n_ch), F32)],
        compiler_params=_params(("arbitrary", "arbitrary")),
        name="conv",
    )(u_all, hist_pad, w_dw, b_dw, ln_g, ln_b)


def _gla_kernel(q_ref, k_ref, v_ref, g_ref, la_ref, s0_ref, gn_ref, o_ref, sout_ref, state,
                *, chunk, heads, dk, dv):
    i = pl.program_id(1)

    @pl.when(i == 0)
    def _():
        state[...] = s0_ref[0]

    la = la_ref[...]
    r = lax.broadcasted_iota(I32, (chunk, chunk), 0)
    c = lax.broadcasted_iota(I32, (chunk, chunk), 1)
    causal = c <= r
    b = jnp.dot(causal.astype(F32), la, precision=HIGHEST, preferred_element_type=F32)
    b_end = b[chunk - 1:chunk, :]
    q_in = (q_ref[...] * jnp.exp(b)).astype(BF16)
    k_in = (k_ref[...] * jnp.exp(-b)).astype(BF16)
    k_out = (k_ref[...] * jnp.exp(b_end - b)).astype(BF16)
    decay_row = jnp.exp(b_end)
    eye = lax.broadcasted_iota(I32, (dk, dk), 0) == lax.broadcasted_iota(I32, (dk, dk), 1)
    for h in range(heads):
        ks = slice(h * dk, (h + 1) * dk)
        vs = slice(h * dv, (h + 1) * dv)
        vh = v_ref[:, vs].astype(BF16)
        att = lax.dot_general(q_in[:, ks], k_in[:, ks], (((1,), (1,)), ((), ())), preferred_element_type=F32)
        att = jnp.where(causal, att, 0.0).astype(BF16)
        s_h = state[h]
        o = jnp.dot(att, vh, preferred_element_type=F32)
        o = o + jnp.dot(q_in[:, ks], s_h.astype(BF16), preferred_element_type=F32)
        decay_col = jnp.sum(jnp.where(eye, jnp.broadcast_to(decay_row[:, ks], (dk, dk)), 0.0), axis=1, keepdims=True)
        state[h] = decay_col * s_h + lax.dot_general(k_out[:, ks], vh, (((0,), (0,)), ((), ())),
                                                     preferred_element_type=F32)
        o = o * lax.rsqrt(jnp.mean(o * o, axis=-1, keepdims=True) + EPS) * gn_ref[...]
        o_ref[:, vs] = (o * _silu(g_ref[:, vs])).astype(o_ref.dtype)

    @pl.when(i == pl.num_programs(1) - 1)
    def _():
        sout_ref[0] = state[...]


def _gla(q_all, k_all, v_all, g_all, la_all, s0, gn, *, row0, bsz, t, chunk):
    heads, dk, dv = s0.shape[1:]
    nt = t // chunk
    blk0 = row0 // chunk
    rows = lambda b, i: (blk0 + b * nt + i, 0)
    return pl.pallas_call(
        functools.partial(_gla_kernel, chunk=chunk, heads=heads, dk=dk, dv=dv),
        grid=(bsz, nt),
        in_specs=[
            pl.BlockSpec((chunk, heads * dk), rows),
            pl.BlockSpec((chunk, heads * dk), rows),
            pl.BlockSpec((chunk, heads * dv), rows),
            pl.BlockSpec((chunk, heads * dv), rows),
            pl.BlockSpec((chunk, heads * dk), rows),
            pl.BlockSpec((1, heads, dk, dv), lambda b, i: (b, 0, 0, 0)),
            pl.BlockSpec((1, dv), lambda b, i: (0, 0)),
        ],
        out_specs=[
            pl.BlockSpec((chunk, heads * dv), lambda b, i: (b * nt + i, 0)),
            pl.BlockSpec((1, heads, dk, dv), lambda b, i: (b, 0, 0, 0)),
        ],
        out_shape=[
            jax.ShapeDtypeStruct((bsz * t, heads * dv), BF16),
            jax.ShapeDtypeStruct((bsz, heads, dk, dv), F32),
        ],
        scratch_shapes=[pltpu.VMEM((heads, dk, dv), F32)],
        compiler_params=_params(("arbitrary", "arbitrary")),
        name="gla",
    )(q_all, k_all, v_all, g_all, la_all, s0, gn)


def _mixer_kernel(x_ref, hist_ref, s0_ref, g1_ref, w_ref, wlr2_ref, blr2_ref,
                  wdw_ref, bdw_ref, lg_ref, lb_ref, gn_ref,
                  c_ref, o_ref, tail_ref, sout_ref, pu, pq, pk, pv, pg, pla, win, shifted, cbuf, state,
                  *, n_seq, seq_rows, chunk, heads, dk, dv, c_conv, width):
    i = pl.program_id(1)
    qk, vv = heads * dk, heads * dv

    @pl.when(i == 0)
    def _():
        for buf in (pu, pq, pk, pv, pg, pla):
            buf[...] = jnp.zeros_like(buf)
        win[...] = jnp.zeros_like(win)
        state[...] = jnp.zeros_like(state)

    @pl.when(i == 1)
    def _():
        for s in range(n_seq):
            win[s, 0:HIST_PAD, :] = hist_ref[s]
        state[...] = s0_ref[...]

    @pl.when(i > 1)
    def _():
        for s in range(n_seq):
            win[s, 0:HIST_PAD, :] = win[s, seq_rows:seq_rows + HIST_PAD, :]

    h = _rms(x_ref[...], g1_ref[...]).astype(BF16)

    def mm(lo, n):
        return jnp.dot(h, w_ref[:, lo:lo + n], preferred_element_type=F32)

    off = 2 * c_conv
    new = {}

    def proj_u(j, n):
        lo = j * n
        new["u", j] = mm(lo, n) * jax.nn.sigmoid(mm(c_conv + lo, n))

    def proj(name, lo, n, scale=None, dtype=F32):
        val = mm(lo, n)
        new[name] = (val if scale is None else val * scale).astype(dtype)

    def proj_la():
        lr = mm(off + 2 * qk + 2 * vv, wlr2_ref.shape[0])
        z = jnp.dot(lr, wlr2_ref[...], precision=HIGHEST, preferred_element_type=F32) + blr2_ref[...]
        new["la"] = _log_sigmoid(z) * (1.0 / GATE_TEMP)

    half_c, half_v = c_conv // 2, vv // 2
    stage1 = [
        functools.partial(proj_u, 0, half_c), functools.partial(proj_u, 1, half_c),
        functools.partial(proj, "q", off, qk, dk ** -0.5), functools.partial(proj, "k", off + qk, qk),
        functools.partial(proj, ("v", 0), off + 2 * qk, half_v, None, BF16),
        functools.partial(proj, ("v", 1), off + 2 * qk + half_v, half_v, None, BF16),
        functools.partial(proj, ("g", 0), off + 2 * qk + vv, half_v),
        functools.partial(proj, ("g", 1), off + 2 * qk + vv + half_v, half_v),
        proj_la,
    ]

    lead = HIST_PAD - (width - 1)
    span = shifted.shape[1]

    def conv_fill(s):
        r0 = s * seq_rows
        win[s, HIST_PAD:HIST_PAD + seq_rows, :] = pu[r0:r0 + seq_rows, :]

    def conv_block(s, cb):
        r0 = s * seq_rows
        cs = slice(cb * LANES, (cb + 1) * LANES)
        for r in range(1, 8):
            shifted[r - 1] = win[s, r:r + span, cs]
        for t0 in range(0, seq_rows, chunk):
            acc = jnp.broadcast_to(bdw_ref[:, cs], (chunk, LANES))
            for j in range(width):
                r, a8 = (lead + j) % 8, (lead + j) // 8 * 8
                if r == 0:
                    tap = win[s, t0 + a8:t0 + a8 + chunk, cs]
                else:
                    tap = shifted[r - 1, t0 + a8:t0 + a8 + chunk, :]
                acc = acc + wdw_ref[j:j + 1, cs] * tap
            cbuf[r0 + t0:r0 + t0 + chunk, cs] = acc

    def conv_norm():
        cv = cbuf[...]
        mu = jnp.mean(cv, axis=-1, keepdims=True)
        xc = cv - mu
        cn = xc * lax.rsqrt(jnp.mean(xc * xc, axis=-1, keepdims=True) + EPS) * lg_ref[...] + lb_ref[...]
        c_ref[...] = _silu(cn).astype(c_ref.dtype)

    r = lax.broadcasted_iota(I32, (chunk, chunk), 0)
    c = lax.broadcasted_iota(I32, (chunk, chunk), 1)
    causal = c <= r
    tril = causal.astype(F32)
    eye = lax.broadcasted_iota(I32, (dk, dk), 0) == lax.broadcasted_iota(I32, (dk, dk), 1)
    gla = {}

    def gla_prep(s, t0):
        rows = slice(s * seq_rows + t0, s * seq_rows + t0 + chunk)
        b = jnp.dot(tril, pla[rows, :], precision=HIGHEST, preferred_element_type=F32)
        b_end = b[chunk - 1:chunk, :]
        k_c = pk[rows, :]
        gla[s, t0] = ((pq[rows, :] * jnp.exp(b)).astype(BF16), (k_c * jnp.exp(-b)).astype(BF16),
                      (k_c * jnp.exp(b_end - b)).astype(BF16), jnp.exp(b_end))

    def gla_head(s, t0, hd):
        rows = slice(s * seq_rows + t0, s * seq_rows + t0 + chunk)
        q_in, k_in, k_out, decay_row = gla[s, t0]
        ks = slice(hd * dk, (hd + 1) * dk)
        vs = slice(hd * dv, (hd + 1) * dv)
        vh = pv[rows, vs]
        att = lax.dot_general(q_in[:, ks], k_in[:, ks], (((1,), (1,)), ((), ())), preferred_element_type=F32)
        att = jnp.where(causal, att, 0.0).astype(BF16)
        s_h = state[s, hd]
        o = jnp.dot(att, vh, preferred_element_type=F32)
        o = o + jnp.dot(q_in[:, ks], s_h.astype(BF16), preferred_element_type=F32)
        decay_col = jnp.sum(jnp.where(eye, jnp.broadcast_to(decay_row[:, ks], (dk, dk)), 0.0), axis=1, keepdims=True)
        state[s, hd] = decay_col * s_h + lax.dot_general(k_out[:, ks], vh, (((0,), (0,)), ((), ())),
                                                         preferred_element_type=F32)
        o = o * lax.rsqrt(jnp.mean(o * o, axis=-1, keepdims=True) + EPS) * gn_ref[...]
        o_ref[rows, vs] = (o * _silu(pg[rows, vs])).astype(o_ref.dtype)

    stage2 = []
    for s in range(n_seq):
        stage2.append(functools.partial(conv_fill, s))
        stage2 += [functools.partial(conv_block, s, cb) for cb in range(c_conv // LANES)]
    stage2.append(conv_norm)
    for s in range(n_seq):
        for t0 in range(0, seq_rows, chunk):
            stage2.append(functools.partial(gla_prep, s, t0))
            stage2 += [functools.partial(gla_head, s, t0, hd) for hd in range(heads)]

    per = -(-len(stage2) // len(stage1))
    for n, piece in enumerate(stage1):
        piece()
        for other in stage2[n * per:(n + 1) * per]:
            other()

    pu[...] = jnp.concatenate([new["u", 0], new["u", 1]], axis=1)
    pq[...] = new["q"]
    pk[...] = new["k"]
    pv[...] = jnp.concatenate([new["v", 0], new["v", 1]], axis=1)
    pg[...] = jnp.concatenate([new["g", 0], new["g", 1]], axis=1)
    pla[...] = new["la"]

    @pl.when(i == pl.num_programs(1) - 1)
    def _():
        for s in range(n_seq):
            tail_ref[s] = win[s, seq_rows:seq_rows + HIST_PAD, :]
        sout_ref[...] = state[...]


def _mixer(x2d, hist_pad, s0, g1, w_in, w_lr2, blr2, w_dw, b_dw, ln_g, ln_b, gn, *, n_seq, seq_rows):
    bsz, heads, dk, dv = s0.shape
    d = x2d.shape[1]
    t = x2d.shape[0] // bsz
    c_conv = w_dw.shape[1]
    width = w_dw.shape[0]
    qk, vv = heads * dk, heads * dv
    assert (n_seq == 1 and t % seq_rows == 0) or (seq_rows == t and bsz % n_seq == 0)
    nt = t // seq_rows
    rows = n_seq * seq_rows
    span = seq_rows + HIST_PAD - 8
    chunk = 64 if seq_rows % 64 == 0 else seq_rows
    const = lambda b, i: (0, 0)
    tile_in = lambda b, i: (b * nt + jnp.minimum(i, nt - 1), 0)
    tile_out = lambda b, i: (b * nt + jnp.maximum(i - 1, 0), 0)
    per_seq3 = lambda b, i: (b, 0, 0)
    per_seq4 = lambda b, i: (b, 0, 0, 0)
    return pl.pallas_call(
        functools.partial(_mixer_kernel, n_seq=n_seq, seq_rows=seq_rows, chunk=chunk, heads=heads, dk=dk, dv=dv,
                          c_conv=c_conv, width=width),
        grid=(bsz // n_seq, nt + 1),
        in_specs=[
            pl.BlockSpec((rows, d), tile_in),
            pl.BlockSpec((n_seq, HIST_PAD, c_conv), per_seq3),
            pl.BlockSpec((n_seq, heads, dk, dv), per_seq4),
            pl.BlockSpec((1, d), const),
            pl.BlockSpec(w_in.shape, const, pipeline_mode=pl.Buffered(1)),
            pl.BlockSpec(w_lr2.shape, const),
            pl.BlockSpec((1, qk), const),
            pl.BlockSpec(w_dw.shape, const),
            pl.BlockSpec((1, c_conv), const), pl.BlockSpec((1, c_conv), const), pl.BlockSpec((1, c_conv), const),
            pl.BlockSpec((1, dv), const),
        ],
        out_specs=[
            pl.BlockSpec((rows, c_conv), tile_out), pl.BlockSpec((rows, vv), tile_out),
            pl.BlockSpec((n_seq, HIST_PAD, c_conv), per_seq3),
            pl.BlockSpec((n_seq, heads, dk, dv), per_seq4),
        ],
        out_shape=[
            jax.ShapeDtypeStruct((bsz * t, c_conv), BF16), jax.ShapeDtypeStruct((bsz * t, vv), BF16),
            jax.ShapeDtypeStruct((bsz, HIST_PAD, c_conv), F32),
            jax.ShapeDtypeStruct((bsz, heads, dk, dv), F32),
        ],
        scratch_shapes=[
            pltpu.VMEM((rows, c_conv), F32), pltpu.VMEM((rows, qk), F32), pltpu.VMEM((rows, qk), F32),
            pltpu.VMEM((rows, vv), BF16), pltpu.VMEM((rows, vv), F32), pltpu.VMEM((rows, qk), F32),
            pltpu.VMEM((n_seq, seq_rows + HIST_PAD, c_conv), F32),
            pltpu.VMEM((7, span, LANES), F32),
            pltpu.VMEM((rows, c_conv), F32),
            pltpu.VMEM((n_seq, heads, dk, dv), F32),
        ],
        compiler_params=_params(("arbitrary", "arbitrary")),
        name="mixer",
    )(x2d, hist_pad, s0, g1, w_in, w_lr2, blr2, w_dw, b_dw, ln_g, ln_b, gn)


GLA_SUB = 16
GLA_CHUNK = 128
CONV_ROWS = 64


def _zero_fill_step(zero_ref, zbuf, zsem, n_chunks, per_step):
    step = pl.program_id(0) * pl.num_programs(1) + pl.program_id(1)
    last = pl.num_programs(0) * pl.num_programs(1) - 1
    rows = zbuf.shape[0]

    def copy(idx):
        return pltpu.make_async_copy(zbuf, zero_ref.at[pl.ds(pl.multiple_of(idx * rows, rows), rows)], zsem)

    @pl.when(step == 0)
    def _():
        zbuf[...] = jnp.zeros_like(zbuf)

    for p in range(per_step):
        earlier = (step - 1) * per_step + p

        @pl.when((step > 0) & (earlier < n_chunks))
        def _():
            copy(earlier).wait()

    for p in range(per_step):
        idx = step * per_step + p

        @pl.when(idx < n_chunks)
        def _():
            copy(idx).start()

        @pl.when((step == last) & (idx < n_chunks))
        def _():
            copy(idx).wait()


def _front_kernel(x_ref, hist_ref, s0_ref, g1_ref, w_ref, wlr2_ref, blr2_ref,
                  wdw_ref, bdw_ref, lg_ref, lb_ref, gn_ref,
                  c_ref, o_ref, tail_ref, sout_ref, *rest,
                  n_seq, seq_rows, chunk, heads, dk, dv, c_conv, width, zero_chunks):
    i = pl.program_id(1)
    qk, vv = heads * dk, heads * dv
    if zero_chunks:
        zero_ref, win, shifted, cbuf, state, zbuf, zsem = rest
        _zero_fill_step(zero_ref, zbuf, zsem, *zero_chunks)
    else:
        win, shifted, cbuf, state = rest

    @pl.when(i == 0)
    def _():
        for s in range(n_seq):
            win[s, 0:HIST_PAD, :] = hist_ref[s]
        state[...] = s0_ref[...]

    @pl.when(i > 0)
    def _():
        for s in range(n_seq):
            win[s, 0:HIST_PAD, :] = win[s, seq_rows:seq_rows + HIST_PAD, :]

    h = _rms(x_ref[...], g1_ref[...]).astype(BF16)

    def mm(lo, n):
        return jnp.dot(h, w_ref[:, lo:lo + n], preferred_element_type=F32)

    u = mm(0, c_conv) * jax.nn.sigmoid(mm(c_conv, c_conv))
    lead = HIST_PAD - (width - 1)
    span = shifted.shape[1]
    conv_rows = min(seq_rows, CONV_ROWS)
    for s in range(n_seq):
        r0 = s * seq_rows
        win[s, HIST_PAD:HIST_PAD + seq_rows, :] = u[r0:r0 + seq_rows, :]
        for cb in range(c_conv // LANES):
            cs = slice(cb * LANES, (cb + 1) * LANES)
            for r in range(1, 8):
                shifted[r - 1] = win[s, r:r + span, cs]
            for t0 in range(0, seq_rows, conv_rows):
                acc = jnp.broadcast_to(bdw_ref[:, cs], (conv_rows, LANES))
                for j in range(width):
                    r, a8 = (lead + j) % 8, (lead + j) // 8 * 8
                    if r == 0:
                        tap = win[s, t0 + a8:t0 + a8 + conv_rows, cs]
                    else:
                        tap = shifted[r - 1, t0 + a8:t0 + a8 + conv_rows, :]
                    acc = acc + wdw_ref[j:j + 1, cs] * tap
                cbuf[r0 + t0:r0 + t0 + conv_rows, cs] = acc
    cv = cbuf[...]
    mu = jnp.mean(cv, axis=-1, keepdims=True)
    xc = cv - mu
    cn = xc * lax.rsqrt(jnp.mean(xc * xc, axis=-1, keepdims=True) + EPS) * lg_ref[...] + lb_ref[...]
    c_ref[...] = _silu(cn).astype(c_ref.dtype)

    off = 2 * c_conv
    q = mm(off, qk) * (dk ** -0.5)
    k = mm(off + qk, qk)
    v = mm(off + 2 * qk, vv).astype(BF16)
    g = mm(off + 2 * qk + vv, vv)
    lr = mm(off + 2 * qk + 2 * vv, wlr2_ref.shape[0])
    z = jnp.dot(lr, wlr2_ref[...], precision=HIGHEST, preferred_element_type=F32) + blr2_ref[...]
    la = _log_sigmoid(z) * (1.0 / GATE_TEMP)

    n_sub = chunk // GLA_SUB
    r = lax.broadcasted_iota(I32, (chunk, chunk), 0)
    c = lax.broadcasted_iota(I32, (chunk, chunk), 1)
    causal = c <= r
    local_sum = (causal & ((r // GLA_SUB) == (c // GLA_SUB))).astype(BF16)
    sub_rows = [slice(j * GLA_SUB, (j + 1) * GLA_SUB) for j in range(n_sub)]
    eye = lax.broadcasted_iota(I32, (dk, dk), 0) == lax.broadcasted_iota(I32, (dk, dk), 1)
    for s in range(n_seq):
        for t0 in range(0, seq_rows, chunk):
            rows = slice(s * seq_rows + t0, s * seq_rows + t0 + chunk)
            la_1 = la[rows, :].astype(BF16)
            rest = la[rows, :] - la_1.astype(F32)
            la_2 = rest.astype(BF16)
            la_3 = (rest - la_2.astype(F32)).astype(BF16)
            sums = jnp.dot(local_sum, jnp.concatenate([la_1, la_2, la_3], axis=1), preferred_element_type=F32)
            local = sums[:, 0:qk] + sums[:, qk:2 * qk] + sums[:, 2 * qk:3 * qk]
            bases = [jnp.zeros((1, qk), F32)]
            for j in range(1, n_sub):
                bases.append(bases[-1] + local[j * GLA_SUB - 1:j * GLA_SUB, :])
            base = jnp.concatenate([jnp.broadcast_to(bs, (GLA_SUB, qk)) for bs in bases], axis=0)
            b = base + local
            b_end = b[chunk - 1:chunk, :]
            q_c, k_c = q[rows, :], k[rows, :]
            q_loc = q_c * jnp.exp(local)
            k_loc = k_c * jnp.exp(-local)
            q_in = (q_c * jnp.exp(b)).astype(BF16)
            k_out = (k_c * jnp.exp(b_end - b)).astype(BF16)
            decay_row = jnp.exp(b_end)
            zero_rows = jnp.zeros((GLA_SUB, qk), F32)
            q_parts, k_parts = [], []
            for j in range(n_sub):
                q_parts.append(jnp.concatenate(
                    [q_loc[sub_rows[m], :] * jnp.exp(jnp.minimum(bases[m] - bases[j], 0.0)) for m in range(n_sub)],
                    axis=0).astype(BF16))
                k_parts.append(jnp.concatenate(
                    [k_loc[sub_rows[m], :] if m == j else zero_rows for m in range(n_sub)], axis=0).astype(BF16))
            for hd in range(heads):
                ks = slice(hd * dk, (hd + 1) * dk)
                vs = slice(hd * dv, (hd + 1) * dv)
                vh = v[rows, vs]
                q_cat = jnp.concatenate([p[:, ks] for p in q_parts], axis=1)
                k_cat = jnp.concatenate([p[:, ks] for p in k_parts], axis=1)
                att = lax.dot_general(q_cat, k_cat, (((1,), (1,)), ((), ())), preferred_element_type=F32)
                att = jnp.where(causal, att, 0.0).astype(BF16)
                s_h = state[s, hd]
                o = jnp.dot(att, vh, preferred_element_type=F32)
                o = o + jnp.dot(q_in[:, ks], s_h.astype(BF16), preferred_element_type=F32)
                decay_col = jnp.sum(jnp.where(eye, jnp.broadcast_to(decay_row[:, ks], (dk, dk)), 0.0),
                                    axis=1, keepdims=True)
                state[s, hd] = decay_col * s_h + lax.dot_general(k_out[:, ks], vh, (((0,), (0,)), ((), ())),
                                                                 preferred_element_type=F32)
                o = o * lax.rsqrt(jnp.mean(o * o, axis=-1, keepdims=True) + EPS) * gn_ref[...]
                o_ref[rows, vs] = (o * _silu(g[rows, vs])).astype(o_ref.dtype)

    @pl.when(i == pl.num_programs(1) - 1)
    def _():
        for s in range(n_seq):
            tail_ref[s] = win[s, seq_rows:seq_rows + HIST_PAD, :]
        sout_ref[...] = state[...]


def _front(x2d, hist_pad, s0, g1, w_in, w_lr2, blr2, w_dw, b_dw, ln_g, ln_b, gn, *, n_seq, seq_rows,
           zero_shape=None):
    bsz, heads, dk, dv = s0.shape
    d = x2d.shape[1]
    t = x2d.shape[0] // bsz
    c_conv = w_dw.shape[1]
    width = w_dw.shape[0]
    qk, vv = heads * dk, heads * dv
    assert (n_seq == 1 and t % seq_rows == 0) or (seq_rows == t and bsz % n_seq == 0)
    nt = t // seq_rows
    rows = n_seq * seq_rows
    chunk = GLA_CHUNK if seq_rows % GLA_CHUNK == 0 else seq_rows
    assert chunk % GLA_SUB == 0 and seq_rows % min(seq_rows, CONV_ROWS) == 0
    const = lambda b, i: (0, 0)
    tile = lambda b, i: (b * nt + i, 0)
    per_seq3 = lambda b, i: (b, 0, 0)
    per_seq4 = lambda b, i: (b, 0, 0, 0)
    extra_out_specs, extra_out_shape, extra_scratch, zero_chunks = [], [], [], None
    if zero_shape is not None:
        assert zero_shape[0] % EXPERT_ROWS == 0
        n_chunks = zero_shape[0] // EXPERT_ROWS
        zero_chunks = (n_chunks, -(-n_chunks // ((bsz // n_seq) * nt)))
        extra_out_specs = [pl.BlockSpec(memory_space=pl.ANY)]
        extra_out_shape = [jax.ShapeDtypeStruct(zero_shape, U32)]
        extra_scratch = [pltpu.VMEM((EXPERT_ROWS, zero_shape[1]), U32), pltpu.SemaphoreType.DMA(())]
    return pl.pallas_call(
        functools.partial(_front_kernel, n_seq=n_seq, seq_rows=seq_rows, chunk=chunk, heads=heads, dk=dk, dv=dv,
                          c_conv=c_conv, width=width, zero_chunks=zero_chunks),
        grid=(bsz // n_seq, nt),
        in_specs=[
            pl.BlockSpec((rows, d), tile),
            pl.BlockSpec((n_seq, HIST_PAD, c_conv), per_seq3),
            pl.BlockSpec((n_seq, heads, dk, dv), per_seq4),
            pl.BlockSpec((1, d), const),
            pl.BlockSpec(w_in.shape, const, pipeline_mode=pl.Buffered(1)),
            pl.BlockSpec(w_lr2.shape, const),
            pl.BlockSpec((1, qk), const),
            pl.BlockSpec(w_dw.shape, const),
            pl.BlockSpec((1, c_conv), const), pl.BlockSpec((1, c_conv), const), pl.BlockSpec((1, c_conv), const),
            pl.BlockSpec((1, dv), const),
        ],
        out_specs=[
            pl.BlockSpec((rows, c_conv), tile), pl.BlockSpec((rows, vv), tile),
            pl.BlockSpec((n_seq, HIST_PAD, c_conv), per_seq3),
            pl.BlockSpec((n_seq, heads, dk, dv), per_seq4),
        ] + extra_out_specs,
        out_shape=[
            jax.ShapeDtypeStruct((bsz * t, c_conv), BF16), jax.ShapeDtypeStruct((bsz * t, vv), BF16),
            jax.ShapeDtypeStruct((bsz, HIST_PAD, c_conv), F32),
            jax.ShapeDtypeStruct((bsz, heads, dk, dv), F32),
        ] + extra_out_shape,
        scratch_shapes=[
            pltpu.VMEM((n_seq, seq_rows + HIST_PAD, c_conv), F32),
            pltpu.VMEM((7, seq_rows + HIST_PAD - 8, LANES), F32),
            pltpu.VMEM((rows, c_conv), F32),
            pltpu.VMEM((n_seq, heads, dk, dv), F32),
        ] + extra_scratch,
        compiler_params=_params(("arbitrary", "arbitrary")),
        name="front",
    )(x2d, hist_pad, s0, g1, w_in, w_lr2, blr2, w_dw, b_dw, ln_g, ln_b, gn)


def _outproj_kernel(xp_ref, xs_ref, cp_ref, cs_ref, op_ref, os_ref, w_ref, g2_ref, wr_ref, br_ref, zeroed_ref,
                    x1_ref, dest_ref, gates_ref, pages_ref, counts_ref, sorted_ref,
                    stage, dest_vmem, dest_smem, cnt_s, page_s, npage_s, table_s, scatter_sems, dest_sems,
                    *, n_first, c_conv, n_groups, per_group, trash_row):
    i = pl.program_id(0)
    n_steps = pl.num_programs(0)
    tm = x1_ref.shape[0]
    n_exp = n_groups * per_group
    slot = i % 2
    prev = 1 - slot

    def dest_copy(s):
        return pltpu.make_async_copy(dest_vmem.at[s], dest_smem.at[s], dest_sems.at[s])

    def scatter_wait(s):
        for _ in range(2):
            pltpu.make_async_copy(stage.at[s], sorted_ref.at[pl.ds(0, tm)], scatter_sems.at[s]).wait()

    @pl.when(i == 0)
    def _():
        cnt_s[...] = jnp.zeros_like(cnt_s)
        page_s[...] = jnp.zeros_like(page_s)
        npage_s[...] = jnp.zeros_like(npage_s)
        table_s[...] = jnp.zeros_like(table_s)
        stage[1] = jnp.zeros(stage.shape[1:], stage.dtype)

        def fill(t, carry):
            dest_smem[1, 0, t] = trash_row + t
            dest_smem[1, 1, t] = trash_row + tm + t
            return carry

        lax.fori_loop(0, tm, fill, 0)

    @pl.when(i >= 1)
    def _():
        dest_copy(prev).wait()
        scatter_wait(slot)

    for t in range(tm):
        _row_copy(stage.at[prev], t, sorted_ref, dest_smem[prev, 0, t], scatter_sems.at[prev]).start()
        _row_copy(stage.at[prev], t, sorted_ref, dest_smem[prev, 1, t], scatter_sems.at[prev]).start()

    x = _pick(i, n_first, xp_ref, xs_ref)
    cc = _pick(i, n_first, cp_ref, cs_ref)
    oo = _pick(i, n_first, op_ref, os_ref)
    del zeroed_ref
    mix = jnp.dot(jnp.concatenate([cc, oo], axis=1), w_ref[...], preferred_element_type=F32)
    x1 = x + mix
    x1_ref[...] = x1
    h2 = _rms(x1, g2_ref[...])
    stage[slot] = _pack_pairs(h2)
    h_hi = h2.astype(BF16)
    h_lo = (h2 - h_hi.astype(F32)).astype(BF16)
    parts = lax.dot_general(wr_ref[...], jnp.concatenate([h_hi, h_lo], axis=0), (((1,), (1,)), ((), ())),
                            preferred_element_type=F32)
    n_r = br_ref.shape[0]
    logits = (parts[0:n_r, 0:tm] + parts[0:n_r, tm:] + parts[n_r:, 0:tm] + parts[n_r:, tm:]) + br_ref[...]
    lc = logits[0:n_groups, :]
    mc = jnp.max(lc, axis=0, keepdims=True)
    p_group = 1.0 / jnp.sum(jnp.exp(lc - mc), axis=0, keepdims=True)
    rows_c = lax.broadcasted_iota(I32, (n_groups, tm), 0)
    g_idx = jnp.min(jnp.where(lc == mc, rows_c, n_groups), axis=0, keepdims=True)
    lf = logits[n_groups:n_groups + n_exp, :]
    rows_f = lax.broadcasted_iota(I32, (n_exp, tm), 0)
    in_group = (rows_f >= g_idx * per_group) & (rows_f < (g_idx + 1) * per_group)
    neg = jnp.float32(-jnp.inf)
    l1 = jnp.where(in_group, lf, neg)
    m1 = jnp.max(l1, axis=0, keepdims=True)
    e1 = jnp.min(jnp.where(l1 == m1, rows_f, n_exp), axis=0, keepdims=True)
    l2 = jnp.where(rows_f == e1, neg, l1)
    m2 = jnp.max(l2, axis=0, keepdims=True)
    e2 = jnp.min(jnp.where(l2 == m2, rows_f, n_exp), axis=0, keepdims=True)
    r2 = jnp.exp(m2 - m1)
    w1 = 1.0 / (1.0 + r2)
    row8 = lax.broadcasted_iota(I32, (8, tm), 0)
    gates_ref[...] = jnp.where(row8 == 0, p_group * w1, jnp.where(row8 == 1, p_group * (r2 * w1), 0.0))

    oh0 = (rows_f == e1).astype(F32)
    oh1 = (rows_f == e2).astype(F32)
    both = oh0 + oh1
    tr = lax.broadcasted_iota(I32, (tm, tm), 0)
    tc = lax.broadcasted_iota(I32, (tm, tm), 1)
    earlier = jnp.dot(both.astype(BF16), (tr < tc).astype(BF16), preferred_element_type=F32)
    cnt = cnt_s[...]
    rank_base = earlier + cnt
    tile_cnt = jnp.sum(both, axis=1, keepdims=True)
    page_rows = float(EXPERT_ROWS)
    k0 = jnp.floor(cnt * (1.0 / page_rows))
    new_cnt = cnt + tile_cnt
    limit = (k0 + 1.0) * page_rows
    need_a = ((cnt == k0 * page_rows) & (tile_cnt > 0.0)).astype(F32)
    need_b = (new_cnt > limit).astype(F32)
    need = need_a + need_b
    er = lax.broadcasted_iota(I32, (n_exp, n_exp), 0)
    ec = lax.broadcasted_iota(I32, (n_exp, n_exp), 1)
    before = jnp.dot((ec < er).astype(BF16), jnp.broadcast_to(need, (n_exp, LANES)).astype(BF16),
                     preferred_element_type=F32)[:, 0:1]
    base = npage_s[...] + before
    page_a = jnp.where(need_a > 0.0, base, page_s[...])
    page_b = base + need_a
    npage_s[...] = npage_s[...] + jnp.sum(need, axis=0, keepdims=True)
    lane = lax.broadcasted_iota(I32, table_s.shape, 1).astype(F32)
    table = jnp.where((lane == k0) & (need_a > 0.0), page_a, table_s[...])
    table_s[...] = jnp.where((lane == k0 + 1.0) & (need_b > 0.0), page_b, table)
    cnt_s[...] = new_cnt
    page_s[...] = jnp.where(jnp.floor(new_cnt * (1.0 / page_rows)) == k0, page_a, page_b)

    def dest_rows(oh):
        rank = jnp.sum(oh * rank_base, axis=0, keepdims=True)
        lim = jnp.sum(oh * limit, axis=0, keepdims=True)
        pa = jnp.sum(oh * page_a, axis=0, keepdims=True)
        pb = jnp.sum(oh * page_b, axis=0, keepdims=True)
        within = rank - jnp.floor(rank * (1.0 / page_rows)) * page_rows
        return jnp.where(rank < lim, pa, pb) * page_rows + within

    dest = jnp.where(row8 == 0, dest_rows(oh0), jnp.where(row8 == 1, dest_rows(oh1), 0.0)).astype(I32)
    dest_ref[...] = dest
    dest_vmem[slot] = dest
    dest_copy(slot).start()

    @pl.when(i == n_steps - 1)
    def _():
        pages_ref[...] = table_s[...].astype(I32)
        counts_ref[...] = jnp.broadcast_to(cnt_s[...], counts_ref.shape).astype(I32)
        dest_copy(slot).wait()
        scatter_wait(prev)

        def last(j, carry):
            for r in range(DMA_UNROLL):
                t = j * DMA_UNROLL + r
                _row_copy(stage.at[slot], t, sorted_ref, dest_smem[slot, 0, t], scatter_sems.at[slot]).start()
                _row_copy(stage.at[slot], t, sorted_ref, dest_smem[slot, 1, t], scatter_sems.at[slot]).start()
            return carry

        lax.fori_loop(0, tm // DMA_UNROLL, last, 0)
        scatter_wait(slot)


def _outproj(xp, xs, cp, cs, op, os_, w_out, g2, wr, br, zeroed, *, n_groups, per_group, n_pages):
    n_p, d = xp.shape
    n_all = n_p + xs.shape[0]
    n_first = n_p // ROW_TILE
    n_exp = n_groups * per_group
    c_conv = cp.shape[1]
    vv = op.shape[1]
    tile = (d // 2,)
    first, second = _split_maps(n_first)
    const = lambda i: (0, 0)
    row = lambda i: (i, 0)
    col = lambda i: (0, i)
    rows_sorted = zeroed.shape[0]
    assert rows_sorted >= n_pages * EXPERT_ROWS + 2 * ROW_TILE
    assert ROW_TILE <= EXPERT_ROWS, "a tile may open at most two pages per expert"
    return pl.pallas_call(
        functools.partial(_outproj_kernel, n_first=n_first, c_conv=c_conv, n_groups=n_groups, per_group=per_group,
                          trash_row=n_pages * EXPERT_ROWS),
        grid=(n_all // ROW_TILE,),
        in_specs=[
            pl.BlockSpec((ROW_TILE, d), first), pl.BlockSpec((ROW_TILE, d), second),
            pl.BlockSpec((ROW_TILE, c_conv), first), pl.BlockSpec((ROW_TILE, c_conv), second),
            pl.BlockSpec((ROW_TILE, vv), first), pl.BlockSpec((ROW_TILE, vv), second),
            pl.BlockSpec(w_out.shape, const, pipeline_mode=pl.Buffered(1)),
            pl.BlockSpec((1, d), const),
            pl.BlockSpec(wr.shape, const),
            pl.BlockSpec(br.shape, const),
            pl.BlockSpec(memory_space=pl.ANY),
        ],
        out_specs=[
            pl.BlockSpec((ROW_TILE, d), row),
            pl.BlockSpec((8, ROW_TILE), col), pl.BlockSpec((8, ROW_TILE), col),
            pl.BlockSpec((n_exp, LANES), const), pl.BlockSpec((n_exp, LANES), const),
            pl.BlockSpec(memory_space=pl.ANY),
        ],
        out_shape=[
            jax.ShapeDtypeStruct((n_all, d), F32),
            jax.ShapeDtypeStruct((8, n_all), I32), jax.ShapeDtypeStruct((8, n_all), F32),
            jax.ShapeDtypeStruct((n_exp, LANES), I32), jax.ShapeDtypeStruct((n_exp, LANES), I32),
            jax.ShapeDtypeStruct((rows_sorted,) + tile, U32),
        ],
        scratch_shapes=[
            pltpu.VMEM((2, ROW_TILE) + tile, U32),
            pltpu.VMEM((2, 8, ROW_TILE), I32), pltpu.SMEM((2, 8, ROW_TILE), I32),
            pltpu.VMEM((n_exp, 1), F32), pltpu.VMEM((n_exp, 1), F32), pltpu.VMEM((1, 1), F32),
            pltpu.VMEM((n_exp, LANES), F32),
            pltpu.SemaphoreType.DMA((2,)), pltpu.SemaphoreType.DMA((2,)),
        ],
        compiler_params=_params(("arbitrary",)),
        input_output_aliases={10: 5},
        name="outproj",
    )(xp, xs, cp, cs, op, os_, w_out, g2, wr, br, zeroed)


def _experts_kernel(cnt_ref, pages_ref, xs_ref, wg_ref, wu_ref, wd_ref, ysp_ref,
                    xbuf, ybuf, wg_f32, wu_f32, wd_f32, wg_bf, wu_bf, wd_bf, first_blk, page_seq,
                    gsems, ysems, wsems, *, n_exp, table_lanes):
    e = pl.program_id(0)
    tb = xbuf.shape[1]
    n_pages = page_seq.shape[0]

    def n_pages_of(ex):
        return (cnt_ref[ex] + (tb - 1)) // tb

    def page_rows(blk):
        return pl.ds(pl.multiple_of(page_seq[blk] * tb, tb), tb)

    def fetch(blk, slot):
        return pltpu.make_async_copy(xs_ref.at[page_rows(blk)], xbuf.at[slot], gsems.at[slot])

    def writeback(blk, slot):
        return pltpu.make_async_copy(ybuf.at[slot], ysp_ref.at[page_rows(blk)], ysems.at[slot])

    def weight_copies(ex, slot):
        return (pltpu.make_async_copy(wg_ref.at[ex], wg_f32.at[slot], wsems.at[slot]),
                pltpu.make_async_copy(wu_ref.at[ex], wu_f32.at[slot], wsems.at[slot]),
                pltpu.make_async_copy(wd_ref.at[ex], wd_f32.at[slot], wsems.at[slot]))

    @pl.when(e == 0)
    def _():
        for cp in weight_copies(0, 0):
            cp.start(priority=1)

        def per_expert(ex, blk):
            first_blk[ex] = blk

            def per_page(j, carry):
                page_seq[blk + j] = pages_ref[ex * table_lanes + j]
                return carry

            lax.fori_loop(0, n_pages_of(ex), per_page, 0)
            return blk + n_pages_of(ex)

        first_blk[n_exp] = lax.fori_loop(0, n_exp, per_expert, 0)
        fetch(0, 0).start()

    @pl.when(e + 1 < n_exp)
    def _():
        for cp in weight_copies(e + 1, (e + 1) % 2):
            cp.start(priority=1)

    b_lo = first_blk[e]
    b_hi = first_blk[e + 1]
    n_total = first_blk[n_exp]
    wslot = e % 2
    for cp in weight_copies(e, wslot):
        cp.wait()
    wg_bf[...] = wg_f32[wslot].astype(BF16)
    wu_bf[...] = wu_f32[wslot].astype(BF16)
    wd_bf[...] = wd_f32[wslot].astype(BF16)

    def block(b, carry):
        slot = b % 2

        @pl.when(b >= 2)
        def _():
            writeback(b, slot).wait()

        fetch(b, slot).wait()
        fetch(jnp.minimum(b + 1, n_total - 1), 1 - slot).start()
        hi, lo = _unpack_pairs(xbuf[slot])
        x = jnp.concatenate([hi.astype(BF16), lo.astype(BF16)], axis=1)
        hg = jnp.dot(x, wg_bf[...], preferred_element_type=F32)
        hu = jnp.dot(x, wu_bf[...], preferred_element_type=F32)
        hb = (_silu(hg) * hu).astype(BF16)
        ybuf[slot] = _pack_pairs(jnp.dot(hb, wd_bf[...], preferred_element_type=F32))
        writeback(b, slot).start()
        return carry

    lax.fori_loop(b_lo, b_hi, block, 0)

    @pl.when(e == n_exp - 1)
    def _():
        fetch(0, n_total % 2).wait()

        @pl.when(n_total >= 2)
        def _():
            writeback(0, n_total % 2).wait()

        writeback(0, (n_total + 1) % 2).wait()
        ybuf[0] = jnp.zeros(ybuf.shape[1:], ybuf.dtype)

        def spare(blk):
            return pltpu.make_async_copy(ybuf.at[0], ysp_ref.at[pl.ds(pl.multiple_of(blk * tb, tb), tb)], ysems.at[0])

        def zero(blk, carry):
            spare(blk).start()
            return carry

        lax.fori_loop(n_total, n_pages, zero, 0)

        def zero_wait(blk, carry):
            spare(0).wait()
            return carry

        lax.fori_loop(n_total, n_pages, zero_wait, 0)


def _experts(counts, pages_flat, xs_sorted, w_gate, w_up, w_down, *, n_pages, table_lanes):
    tile = xs_sorted.shape[1:]
    n_exp, d, ff = w_gate.shape
    anyspec = pl.BlockSpec(memory_space=pl.ANY)
    grid_spec = pltpu.PrefetchScalarGridSpec(
        num_scalar_prefetch=2,
        grid=(n_exp,),
        in_specs=[anyspec, anyspec, anyspec, anyspec],
        out_specs=anyspec,
        scratch_shapes=[
            pltpu.VMEM((2, EXPERT_ROWS) + tile, U32), pltpu.VMEM((2, EXPERT_ROWS) + tile, U32),
            pltpu.VMEM((2, d, ff), F32), pltpu.VMEM((2, d, ff), F32), pltpu.VMEM((2, ff, d), F32),
            pltpu.VMEM((d, ff), BF16), pltpu.VMEM((d, ff), BF16), pltpu.VMEM((ff, d), BF16),
            pltpu.SMEM((n_exp + 1,), I32), pltpu.SMEM((n_pages,), I32),
            pltpu.SemaphoreType.DMA((2,)), pltpu.SemaphoreType.DMA((2,)), pltpu.SemaphoreType.DMA((2,)),
        ],
    )
    return pl.pallas_call(
        functools.partial(_experts_kernel, n_exp=n_exp, table_lanes=table_lanes),
        grid_spec=grid_spec,
        out_shape=jax.ShapeDtypeStruct((n_pages * EXPERT_ROWS,) + tile, U32),
        compiler_params=_params(("arbitrary",)),
        name="experts",
    )(counts, pages_flat, xs_sorted, w_gate, w_up, w_down)


def _combine_kernel(dest_ref, dest_next_ref, gates_ref, x1_ref, ysp_ref, gf_ref, yp_ref, ysmp_ref,
                    buf0, buf1, sems, *, n_first):
    i = pl.program_id(0)
    n = pl.num_programs(0)
    tm = x1_ref.shape[0]
    slot = i % 2

    def gather(d_ref, s):
        def body(j, carry):
            for r in range(DMA_UNROLL):
                t = j * DMA_UNROLL + r
                _row_copy(ysp_ref, d_ref[0, t], buf0.at[s], t, sems.at[s]).start(priority=0)
                _row_copy(ysp_ref, d_ref[1, t], buf1.at[s], t, sems.at[s]).start(priority=1)
            return carry

        lax.fori_loop(0, tm // DMA_UNROLL, body, 0)

    @pl.when(i == 0)
    def _():
        gather(dest_ref, 0)

    @pl.when(i + 1 < n)
    def _():
        gather(dest_next_ref, 1 - slot)

    pltpu.make_async_copy(ysp_ref.at[pl.ds(0, tm)], buf0.at[slot], sems.at[slot]).wait()
    pltpu.make_async_copy(ysp_ref.at[pl.ds(0, tm)], buf1.at[slot], sems.at[slot]).wait()
    hi0, lo0 = _unpack_pairs(buf0[slot])
    hi1, lo1 = _unpack_pairs(buf1[slot])
    g0 = gates_ref[:, 0:1]
    g1 = gates_ref[:, 1:2]
    moe = jnp.concatenate([g0 * hi0 + g1 * hi1, g0 * lo0 + g1 * lo1], axis=1)
    y = _rms(x1_ref[...] + moe, gf_ref[...])

    @pl.when(i < n_first)
    def _():
        yp_ref[...] = y

    @pl.when(i >= n_first)
    def _():
        ysmp_ref[...] = y


def _combine(dest, gates_t, x1, ysp, gf, *, n_p):
    n_all, d = x1.shape
    tile = ysp.shape[1:]
    n_first = n_p // ROW_TILE
    n_tiles = n_all // ROW_TILE
    first, second = _split_maps(n_first)
    return pl.pallas_call(
        functools.partial(_combine_kernel, n_first=n_first),
        grid=(n_tiles,),
        in_specs=[
            pl.BlockSpec((8, ROW_TILE), lambda i: (0, i), memory_space=pltpu.SMEM),
            pl.BlockSpec((8, ROW_TILE), lambda i: (0, jnp.minimum(i + 1, n_tiles - 1)), memory_space=pltpu.SMEM),
            pl.BlockSpec((ROW_TILE, 8), lambda i: (i, 0)),
            pl.BlockSpec((ROW_TILE, d), lambda i: (i, 0)),
            pl.BlockSpec(memory_space=pl.ANY),
            pl.BlockSpec((1, d), lambda i: (0, 0)),
        ],
        out_specs=[pl.BlockSpec((ROW_TILE, d), first), pl.BlockSpec((ROW_TILE, d), second)],
        out_shape=[jax.ShapeDtypeStruct((n_p, d), F32), jax.ShapeDtypeStruct((n_all - n_p, d), F32)],
        scratch_shapes=[pltpu.VMEM((2, ROW_TILE) + tile, U32), pltpu.VMEM((2, ROW_TILE) + tile, U32),
                        pltpu.SemaphoreType.DMA((2,))],
        compiler_params=_params(("arbitrary",)),
        name="combine",
    )(dest, dest, gates_t, x1, ysp, gf)


def _chunk_for(t):
    return 64 if t % 64 == 0 else t


def kernel(x_prompt, x_sample, cache_conv, state_gla, norm1_g, w_in, w_lr2, b_lr2, w_dw, b_dw, conv_ln_g, conv_ln_b, gla_norm_g, w_out, norm2_g, w_router_coarse, b_router_coarse, w_router_fine, b_router_fine, w_exp_gate, w_exp_up, w_exp_down, norm_f_g):
    assert norm1_g.shape[0] == 1, "single trunk layer"
    bp, tp, d = x_prompt.shape
    bs, ts, _ = x_sample.shape
    heads, dk, dv = state_gla.shape[2:]
    c_conv = w_dw.shape[2]
    width = w_dw.shape[1]
    rank = w_lr2.shape[1]
    qk, vv = heads * dk, heads * dv
    n_groups, _, per_group = w_router_fine.shape[1:]
    n_exp = n_groups * per_group
    n_p, n_s = bp * tp, bs * ts
    n_s_pad = -(-n_s // ROW_TILE) * ROW_TILE
    pad_rows = lambda a: jnp.pad(a, ((0, n_s_pad - n_s), (0, 0)))
    n_all = n_p + n_s_pad
    assert n_p % ROW_TILE == 0 and width - 1 <= HIST_PAD

    xp = x_prompt.reshape(n_p, d)
    xs = pad_rows(x_sample.reshape(n_s, d))
    row = lambda a: a.reshape(1, -1)

    mixer_args = (row(norm1_g[0]), w_in[0].astype(BF16), w_lr2[0], row(b_lr2[0]),
                  w_dw[0], row(b_dw[0]), row(conv_ln_g[0]), row(conv_ln_b[0]), row(gla_norm_g[0]))
    hist_p = jnp.zeros((bp, HIST_PAD, c_conv), F32)
    hist_s = jnp.pad(cache_conv[0], ((0, 0), (HIST_PAD - (width - 1), 0), (0, 0)))
    s0_p = jnp.zeros((bp, heads, dk, dv), F32)
    n_pages = (2 * n_all) // EXPERT_ROWS + n_exp
    sorted_rows = n_pages * EXPERT_ROWS + -(-2 * ROW_TILE // EXPERT_ROWS) * EXPERT_ROWS
    c_p, o_p, tail_p, gla_p, zeroed = _front(xp, hist_p, s0_p, *mixer_args, n_seq=1, seq_rows=ROW_TILE,
                                             zero_shape=(sorted_rows, d // 2))
    c_s, o_s, tail_s, gla_s = _front(x_sample.reshape(n_s, d), hist_s, state_gla[0], *mixer_args,
                                     n_seq=bs, seq_rows=ts)
    c_s, o_s = pad_rows(c_s), pad_rows(o_s)

    wr = jnp.concatenate([w_router_coarse[0].T,
                          jnp.transpose(w_router_fine[0], (0, 2, 1)).reshape(n_exp, d)], axis=0)
    br = jnp.concatenate([b_router_coarse[0], b_router_fine[0].reshape(n_exp)])
    r_rows = -(-(n_groups + n_exp) // 8) * 8
    wr = jnp.pad(wr, ((0, r_rows - wr.shape[0]), (0, 0)))
    br = jnp.pad(br, (0, r_rows - br.shape[0])).reshape(r_rows, 1)
    wr_hi = wr.astype(BF16)
    wr = jnp.concatenate([wr_hi, (wr - wr_hi.astype(F32)).astype(BF16)], axis=0)
    n_pages = (2 * n_all) // EXPERT_ROWS + n_exp
    assert n_all // EXPERT_ROWS + 2 <= LANES, "page table row must hold one expert's pages"
    x1, dest, gates, pages, counts, xs_sorted = _outproj(
        xp, xs, c_p, c_s, o_p, o_s, w_out[0].astype(BF16), row(norm2_g[0]), wr, br, zeroed,
        n_groups=n_groups, per_group=per_group, n_pages=n_pages)
    ysp = _experts(counts[:, 0], pages.reshape(-1), xs_sorted, w_exp_gate[0], w_exp_up[0], w_exp_down[0],
                   n_pages=n_pages, table_lanes=LANES)
    y_p, y_s = _combine(dest, gates.T, x1, ysp, row(norm_f_g), n_p=n_p)

    lead = HIST_PAD - (width - 1)
    return (y_p.reshape(bp, tp, d), y_s[:n_s].reshape(bs, ts, d), tail_p[:, lead:][None], gla_p[None],
            tail_s[:, lead:][None], gla_s[None])
```

```python
import functools

import jax
import jax.numpy as jnp
from jax import lax
from jax.experimental import pallas as pl
from jax.experimental.pallas import tpu as pltpu

F32 = jnp.float32
BF16 = jnp.bfloat16
I32 = jnp.int32
U32 = jnp.uint32
EPS = 1e-6
GATE_TEMP = 16.0
HIGHEST = lax.Precision.HIGHEST

LANES = 128
ROW_TILE = 256
EXPERT_ROWS = 256
HIST_PAD = 32
DMA_UNROLL = 8
WEIGHT_SLOTS = 3
VMEM_LIMIT = 56 * 1024 * 1024


def _params(semantics, vmem=VMEM_LIMIT):
    return pltpu.CompilerParams(dimension_semantics=semantics, vmem_limit_bytes=vmem)


def _rms(x, g):
    return x * lax.rsqrt(jnp.mean(x * x, axis=-1, keepdims=True) + EPS) * g


def _silu(x):
    return x * jax.nn.sigmoid(x)


def _log_sigmoid(z):
    return jnp.minimum(z, 0.0) - jnp.log(1.0 + jnp.exp(-jnp.abs(z)))


def _pick(i, n_first, first_ref, second_ref):
    return jnp.where(i < n_first, first_ref[...], second_ref[...])


def _split_maps(n_first):
    first = lambda i: (jnp.minimum(i, n_first - 1), 0)
    second = lambda i: (jnp.maximum(i - n_first, 0), 0)
    return first, second


def _pack_pairs(x):
    half = x.shape[1] // 2
    hi = lax.bitcast_convert_type(x[:, :half].astype(BF16).astype(F32), U32)
    lo = lax.bitcast_convert_type(x[:, half:].astype(BF16).astype(F32), U32)
    return hi | (lo >> 16)


def _unpack_pairs(p):
    hi = lax.bitcast_convert_type(p & jnp.uint32(0xFFFF0000), F32)
    lo = lax.bitcast_convert_type(p << 16, F32)
    return hi, lo


def _row_copy(src, s, dst, d, sem):
    return pltpu.make_async_copy(src.at[pl.ds(s, 1)], dst.at[pl.ds(d, 1)], sem)


def _inproj_kernel(xp_ref, xs_ref, g1_ref, w_ref, wlr2_ref, blr2_ref,
                   u_ref, q_ref, k_ref, v_ref, g_ref, la_ref, *, n_first, c_conv, qk, vv, dk):
    i = pl.program_id(0)
    x = _pick(i, n_first, xp_ref, xs_ref)
    h = _rms(x, g1_ref[...]).astype(BF16)

    def mm(lo, width):
        return jnp.dot(h, w_ref[:, lo:lo + width], preferred_element_type=F32)

    a = mm(0, c_conv)
    a_gate = mm(c_conv, c_conv)
    u_ref[...] = a * jax.nn.sigmoid(a_gate)
    off = 2 * c_conv
    q_ref[...] = mm(off, qk) * (dk ** -0.5)
    k_ref[...] = mm(off + qk, qk)
    v_ref[...] = mm(off + 2 * qk, vv)
    g_ref[...] = mm(off + 2 * qk + vv, vv)
    lr = mm(off + 2 * qk + 2 * vv, LANES)
    z = jnp.dot(lr, wlr2_ref[...], precision=HIGHEST, preferred_element_type=F32) + blr2_ref[...]
    la_ref[...] = _log_sigmoid(z) * (1.0 / GATE_TEMP)


def _inproj(xp, xs, g1, w_pad, wlr2_pad, blr2, *, c_conv, qk, vv, dk):
    n_p, d = xp.shape
    n_s = xs.shape[0]
    n_all = n_p + n_s
    n_first = n_p // ROW_TILE
    grid = (n_all // ROW_TILE,)
    first, second = _split_maps(n_first)
    const = lambda i: (0, 0)
    row = lambda i: (i, 0)
    widths = (c_conv, qk, qk, vv, vv, qk)
    return pl.pallas_call(
        functools.partial(_inproj_kernel, n_first=n_first, c_conv=c_conv, qk=qk, vv=vv, dk=dk),
        grid=grid,
        in_specs=[
            pl.BlockSpec((ROW_TILE, d), first),
            pl.BlockSpec((ROW_TILE, d), second),
            pl.BlockSpec((1, d), const),
            pl.BlockSpec(w_pad.shape, const, pipeline_mode=pl.Buffered(1)),
            pl.BlockSpec(wlr2_pad.shape, const),
            pl.BlockSpec((1, qk), const),
        ],
        out_specs=[pl.BlockSpec((ROW_TILE, w), row) for w in widths],
        out_shape=[jax.ShapeDtypeStruct((n_all, w), F32) for w in widths],
        compiler_params=_params(("arbitrary",)),
        name="inproj",
    )(xp, xs, g1, w_pad, wlr2_pad, blr2)


def _conv_kernel(u_ref, hist_ref, w_ref, b_ref, lg_ref, lb_ref, c_ref, win, cbuf, *, tt, width):
    i = pl.program_id(1)

    @pl.when(i == 0)
    def _():
        win[0:HIST_PAD, :] = hist_ref[0]

    @pl.when(i > 0)
    def _():
        win[0:HIST_PAD, :] = win[tt:tt + HIST_PAD, :]

    win[HIST_PAD:HIST_PAD + tt, :] = u_ref[...]
    lead = HIST_PAD - (width - 1)
    n_ch = u_ref.shape[1]
    for cb in range(n_ch // LANES):
        cs = slice(cb * LANES, (cb + 1) * LANES)
        acc = jnp.broadcast_to(b_ref[:, cs], (tt, LANES))
        for j in range(width):
            acc = acc + w_ref[j:j + 1, cs] * win[lead + j:lead + j + tt, cs]
        cbuf[:, cs] = acc
    c = cbuf[...]
    mu = jnp.mean(c, axis=-1, keepdims=True)
    xc = c - mu
    y = xc * lax.rsqrt(jnp.mean(xc * xc, axis=-1, keepdims=True) + EPS) * lg_ref[...] + lb_ref[...]
    c_ref[...] = _silu(y).astype(c_ref.dtype)


def _conv(u_all, hist_pad, w_dw, b_dw, ln_g, ln_b, *, row0, bsz, t, tt):
    n_ch = u_all.shape[1]
    width = w_dw.shape[0]
    nt = t // tt
    blk0 = row0 // tt
    const = lambda b, i: (0, 0)
    return pl.pallas_call(
        functools.partial(_conv_kernel, tt=tt, width=width),
        grid=(bsz, nt),
        in_specs=[
            pl.BlockSpec((tt, n_ch), lambda b, i: (blk0 + b * nt + i, 0)),
            pl.BlockSpec((1, HIST_PAD, n_ch), lambda b, i: (b, 0, 0)),
            pl.BlockSpec(w_dw.shape, const),
            pl.BlockSpec((1, n_ch), const),
            pl.BlockSpec((1, n_ch), const),
            pl.BlockSpec((1, n_ch), const),
        ],
        out_specs=pl.BlockSpec((tt, n_ch), lambda b, i: (b * nt + i, 0)),
        out_shape=jax.ShapeDtypeStruct((bsz * t, n_ch), BF16),
        scratch_shapes=[pltpu.VMEM((tt + HIST_PAD, n_ch), F32), pltpu.VMEM((tt, n_ch), F32)],
        compiler_params=_params(("arbitrary", "arbitrary")),
        name="conv",
    )(u_all, hist_pad, w_dw, b_dw, ln_g, ln_b)


def _gla_kernel(q_ref, k_ref, v_ref, g_ref, la_ref, s0_ref, gn_ref, o_ref, sout_ref, state,
                *, chunk, heads, dk, dv):
    i = pl.program_id(1)

    @pl.when(i == 0)
    def _():
        state[...] = s0_ref[0]

    la = la_ref[...]
    r = lax.broadcasted_iota(I32, (chunk, chunk), 0)
    c = lax.broadcasted_iota(I32, (chunk, chunk), 1)
    causal = c <= r
    b = jnp.dot(causal.astype(F32), la, precision=HIGHEST, preferred_element_type=F32)
    b_end = b[chunk - 1:chunk, :]
    q_in = (q_ref[...] * jnp.exp(b)).astype(BF16)
    k_in = (k_ref[...] * jnp.exp(-b)).astype(BF16)
    k_out = (k_ref[...] * jnp.exp(b_end - b)).astype(BF16)
    decay_row = jnp.exp(b_end)
    eye = lax.broadcasted_iota(I32, (dk, dk), 0) == lax.broadcasted_iota(I32, (dk, dk), 1)
    for h in range(heads):
        ks = slice(h * dk, (h + 1) * dk)
        vs = slice(h * dv, (h + 1) * dv)
        vh = v_ref[:, vs].astype(BF16)
        att = lax.dot_general(q_in[:, ks], k_in[:, ks], (((1,), (1,)), ((), ())), preferred_element_type=F32)
        att = jnp.where(causal, att, 0.0).astype(BF16)
        s_h = state[h]
        o = jnp.dot(att, vh, preferred_element_type=F32)
        o = o + jnp.dot(q_in[:, ks], s_h.astype(BF16), preferred_element_type=F32)
        decay_col = jnp.sum(jnp.where(eye, jnp.broadcast_to(decay_row[:, ks], (dk, dk)), 0.0), axis=1, keepdims=True)
        state[h] = decay_col * s_h + lax.dot_general(k_out[:, ks], vh, (((0,), (0,)), ((), ())),
                                                     preferred_element_type=F32)
        o = o * lax.rsqrt(jnp.mean(o * o, axis=-1, keepdims=True) + EPS) * gn_ref[...]
        o_ref[:, vs] = (o * _silu(g_ref[:, vs])).astype(o_ref.dtype)

    @pl.when(i == pl.num_programs(1) - 1)
    def _():
        sout_ref[0] = state[...]


def _gla(q_all, k_all, v_all, g_all, la_all, s0, gn, *, row0, bsz, t, chunk):
    heads, dk, dv = s0.shape[1:]
    nt = t // chunk
    blk0 = row0 // chunk
    rows = lambda b, i: (blk0 + b * nt + i, 0)
    return pl.pallas_call(
        functools.partial(_gla_kernel, chunk=chunk, heads=heads, dk=dk, dv=dv),
        grid=(bsz, nt),
        in_specs=[
            pl.BlockSpec((chunk, heads * dk), rows),
            pl.BlockSpec((chunk, heads * dk), rows),
            pl.BlockSpec((chunk, heads * dv), rows),
            pl.BlockSpec((chunk, heads * dv), rows),
            pl.BlockSpec((chunk, heads * dk), rows),
            pl.BlockSpec((1, heads, dk, dv), lambda b, i: (b, 0, 0, 0)),
            pl.BlockSpec((1, dv), lambda b, i: (0, 0)),
        ],
        out_specs=[
            pl.BlockSpec((chunk, heads * dv), lambda b, i: (b * nt + i, 0)),
            pl.BlockSpec((1, heads, dk, dv), lambda b, i: (b, 0, 0, 0)),
        ],
        out_shape=[
            jax.ShapeDtypeStruct((bsz * t, heads * dv), BF16),
            jax.ShapeDtypeStruct((bsz, heads, dk, dv), F32),
        ],
        scratch_shapes=[pltpu.VMEM((heads, dk, dv), F32)],
        compiler_params=_params(("arbitrary", "arbitrary")),
        name="gla",
    )(q_all, k_all, v_all, g_all, la_all, s0, gn)


def _mixer_kernel(x_ref, hist_ref, s0_ref, g1_ref, w_ref, wlr2_ref, blr2_ref,
                  wdw_ref, bdw_ref, lg_ref, lb_ref, gn_ref,
                  c_ref, o_ref, tail_ref, sout_ref, pu, pq, pk, pv, pg, pla, win, shifted, cbuf, state,
                  *, n_seq, seq_rows, chunk, heads, dk, dv, c_conv, width):
    i = pl.program_id(1)
    qk, vv = heads * dk, heads * dv

    @pl.when(i == 0)
    def _():
        for buf in (pu, pq, pk, pv, pg, pla):
            buf[...] = jnp.zeros_like(buf)
        win[...] = jnp.zeros_like(win)
        state[...] = jnp.zeros_like(state)

    @pl.when(i == 1)
    def _():
        for s in range(n_seq):
            win[s, 0:HIST_PAD, :] = hist_ref[s]
        state[...] = s0_ref[...]

    @pl.when(i > 1)
    def _():
        for s in range(n_seq):
            win[s, 0:HIST_PAD, :] = win[s, seq_rows:seq_rows + HIST_PAD, :]

    h = _rms(x_ref[...], g1_ref[...]).astype(BF16)

    def mm(lo, n):
        return jnp.dot(h, w_ref[:, lo:lo + n], preferred_element_type=F32)

    off = 2 * c_conv
    new = {}

    def proj_u(j, n):
        lo = j * n
        new["u", j] = mm(lo, n) * jax.nn.sigmoid(mm(c_conv + lo, n))

    def proj(name, lo, n, scale=None, dtype=F32):
        val = mm(lo, n)
        new[name] = (val if scale is None else val * scale).astype(dtype)

    def proj_la():
        lr = mm(off + 2 * qk + 2 * vv, wlr2_ref.shape[0])
        z = jnp.dot(lr, wlr2_ref[...], precision=HIGHEST, preferred_element_type=F32) + blr2_ref[...]
        new["la"] = _log_sigmoid(z) * (1.0 / GATE_TEMP)

    half_c, half_v = c_conv // 2, vv // 2
    stage1 = [
        functools.partial(proj_u, 0, half_c), functools.partial(proj_u, 1, half_c),
        functools.partial(proj, "q", off, qk, dk ** -0.5), functools.partial(proj, "k", off + qk, qk),
        functools.partial(proj, ("v", 0), off + 2 * qk, half_v, None, BF16),
        functools.partial(proj, ("v", 1), off + 2 * qk + half_v, half_v, None, BF16),
        functools.partial(proj, ("g", 0), off + 2 * qk + vv, half_v),
        functools.partial(proj, ("g", 1), off + 2 * qk + vv + half_v, half_v),
        proj_la,
    ]

    lead = HIST_PAD - (width - 1)
    span = shifted.shape[1]

    def conv_fill(s):
        r0 = s * seq_rows
        win[s, HIST_PAD:HIST_PAD + seq_rows, :] = pu[r0:r0 + seq_rows, :]

    def conv_block(s, cb):
        r0 = s * seq_rows
        cs = slice(cb * LANES, (cb + 1) * LANES)
        for r in range(1, 8):
            shifted[r - 1] = win[s, r:r + span, cs]
        for t0 in range(0, seq_rows, chunk):
            acc = jnp.broadcast_to(bdw_ref[:, cs], (chunk, LANES))
            for j in range(width):
                r, a8 = (lead + j) % 8, (lead + j) // 8 * 8
                if r == 0:
                    tap = win[s, t0 + a8:t0 + a8 + chunk, cs]
                else:
                    tap = shifted[r - 1, t0 + a8:t0 + a8 + chunk, :]
                acc = acc + wdw_ref[j:j + 1, cs] * tap
            cbuf[r0 + t0:r0 + t0 + chunk, cs] = acc

    def conv_norm():
        cv = cbuf[...]
        mu = jnp.mean(cv, axis=-1, keepdims=True)
        xc = cv - mu
        cn = xc * lax.rsqrt(jnp.mean(xc * xc, axis=-1, keepdims=True) + EPS) * lg_ref[...] + lb_ref[...]
        c_ref[...] = _silu(cn).astype(c_ref.dtype)

    r = lax.broadcasted_iota(I32, (chunk, chunk), 0)
    c = lax.broadcasted_iota(I32, (chunk, chunk), 1)
    causal = c <= r
    tril = causal.astype(F32)
    eye = lax.broadcasted_iota(I32, (dk, dk), 0) == lax.broadcasted_iota(I32, (dk, dk), 1)
    gla = {}

    def gla_prep(s, t0):
        rows = slice(s * seq_rows + t0, s * seq_rows + t0 + chunk)
        b = jnp.dot(tril, pla[rows, :], precision=HIGHEST, preferred_element_type=F32)
        b_end = b[chunk - 1:chunk, :]
        k_c = pk[rows, :]
        gla[s, t0] = ((pq[rows, :] * jnp.exp(b)).astype(BF16), (k_c * jnp.exp(-b)).astype(BF16),
                      (k_c * jnp.exp(b_end - b)).astype(BF16), jnp.exp(b_end))

    def gla_head(s, t0, hd):
        rows = slice(s * seq_rows + t0, s * seq_rows + t0 + chunk)
        q_in, k_in, k_out, decay_row = gla[s, t0]
        ks = slice(hd * dk, (hd + 1) * dk)
        vs = slice(hd * dv, (hd + 1) * dv)
        vh = pv[rows, vs]
        att = lax.dot_general(q_in[:, ks], k_in[:, ks], (((1,), (1,)), ((), ())), preferred_element_type=F32)
        att = jnp.where(causal, att, 0.0).astype(BF16)
        s_h = state[s, hd]
        o = jnp.dot(att, vh, preferred_element_type=F32)
        o = o + jnp.dot(q_in[:, ks], s_h.astype(BF16), preferred_element_type=F32)
        decay_col = jnp.sum(jnp.where(eye, jnp.broadcast_to(decay_row[:, ks], (dk, dk)), 0.0), axis=1, keepdims=True)
        state[s, hd] = decay_col * s_h + lax.dot_general(k_out[:, ks], vh, (((0,), (0,)), ((), ())),
                                                         preferred_element_type=F32)
        o = o * lax.rsqrt(jnp.mean(o * o, axis=-1, keepdims=True) + EPS) * gn_ref[...]
        o_ref[rows, vs] = (o * _silu(pg[rows, vs])).astype(o_ref.dtype)

    stage2 = []
    for s in range(n_seq):
        stage2.append(functools.partial(conv_fill, s))
        stage2 += [functools.partial(conv_block, s, cb) for cb in range(c_conv // LANES)]
    stage2.append(conv_norm)
    for s in range(n_seq):
        for t0 in range(0, seq_rows, chunk):
            stage2.append(functools.partial(gla_prep, s, t0))
            stage2 += [functools.partial(gla_head, s, t0, hd) for hd in range(heads)]

    per = -(-len(stage2) // len(stage1))
    for n, piece in enumerate(stage1):
        piece()
        for other in stage2[n * per:(n + 1) * per]:
            other()

    pu[...] = jnp.concatenate([new["u", 0], new["u", 1]], axis=1)
    pq[...] = new["q"]
    pk[...] = new["k"]
    pv[...] = jnp.concatenate([new["v", 0], new["v", 1]], axis=1)
    pg[...] = jnp.concatenate([new["g", 0], new["g", 1]], axis=1)
    pla[...] = new["la"]

    @pl.when(i == pl.num_programs(1) - 1)
    def _():
        for s in range(n_seq):
            tail_ref[s] = win[s, seq_rows:seq_rows + HIST_PAD, :]
        sout_ref[...] = state[...]


def _mixer(x2d, hist_pad, s0, g1, w_in, w_lr2, blr2, w_dw, b_dw, ln_g, ln_b, gn, *, n_seq, seq_rows):
    bsz, heads, dk, dv = s0.shape
    d = x2d.shape[1]
    t = x2d.shape[0] // bsz
    c_conv = w_dw.shape[1]
    width = w_dw.shape[0]
    qk, vv = heads * dk, heads * dv
    assert (n_seq == 1 and t % seq_rows == 0) or (seq_rows == t and bsz % n_seq == 0)
    nt = t // seq_rows
    rows = n_seq * seq_rows
    span = seq_rows + HIST_PAD - 8
    chunk = 64 if seq_rows % 64 == 0 else seq_rows
    const = lambda b, i: (0, 0)
    tile_in = lambda b, i: (b * nt + jnp.minimum(i, nt - 1), 0)
    tile_out = lambda b, i: (b * nt + jnp.maximum(i - 1, 0), 0)
    per_seq3 = lambda b, i: (b, 0, 0)
    per_seq4 = lambda b, i: (b, 0, 0, 0)
    return pl.pallas_call(
        functools.partial(_mixer_kernel, n_seq=n_seq, seq_rows=seq_rows, chunk=chunk, heads=heads, dk=dk, dv=dv,
                          c_conv=c_conv, width=width),
        grid=(bsz // n_seq, nt + 1),
        in_specs=[
            pl.BlockSpec((rows, d), tile_in),
            pl.BlockSpec((n_seq, HIST_PAD, c_conv), per_seq3),
            pl.BlockSpec((n_seq, heads, dk, dv), per_seq4),
            pl.BlockSpec((1, d), const),
            pl.BlockSpec(w_in.shape, const, pipeline_mode=pl.Buffered(1)),
            pl.BlockSpec(w_lr2.shape, const),
            pl.BlockSpec((1, qk), const),
            pl.BlockSpec(w_dw.shape, const),
            pl.BlockSpec((1, c_conv), const), pl.BlockSpec((1, c_conv), const), pl.BlockSpec((1, c_conv), const),
            pl.BlockSpec((1, dv), const),
        ],
        out_specs=[
            pl.BlockSpec((rows, c_conv), tile_out), pl.BlockSpec((rows, vv), tile_out),
            pl.BlockSpec((n_seq, HIST_PAD, c_conv), per_seq3),
            pl.BlockSpec((n_seq, heads, dk, dv), per_seq4),
        ],
        out_shape=[
            jax.ShapeDtypeStruct((bsz * t, c_conv), BF16), jax.ShapeDtypeStruct((bsz * t, vv), BF16),
            jax.ShapeDtypeStruct((bsz, HIST_PAD, c_conv), F32),
            jax.ShapeDtypeStruct((bsz, heads, dk, dv), F32),
        ],
        scratch_shapes=[
            pltpu.VMEM((rows, c_conv), F32), pltpu.VMEM((rows, qk), F32), pltpu.VMEM((rows, qk), F32),
            pltpu.VMEM((rows, vv), BF16), pltpu.VMEM((rows, vv), F32), pltpu.VMEM((rows, qk), F32),
            pltpu.VMEM((n_seq, seq_rows + HIST_PAD, c_conv), F32),
            pltpu.VMEM((7, span, LANES), F32),
            pltpu.VMEM((rows, c_conv), F32),
            pltpu.VMEM((n_seq, heads, dk, dv), F32),
        ],
        compiler_params=_params(("arbitrary", "arbitrary")),
        name="mixer",
    )(x2d, hist_pad, s0, g1, w_in, w_lr2, blr2, w_dw, b_dw, ln_g, ln_b, gn)


GLA_SUB = 16
GLA_CHUNK = 128
CONV_ROWS = 64


def _zero_fill_step(zero_ref, zbuf, zsem, n_chunks, per_step):
    step = pl.program_id(0) * pl.num_programs(1) + pl.program_id(1)
    last = pl.num_programs(0) * pl.num_programs(1) - 1
    rows = zbuf.shape[0]

    def copy(idx):
        return pltpu.make_async_copy(zbuf, zero_ref.at[pl.ds(pl.multiple_of(idx * rows, rows), rows)], zsem)

    @pl.when(step == 0)
    def _():
        zbuf[...] = jnp.zeros_like(zbuf)

    for p in range(per_step):
        earlier = (step - 1) * per_step + p

        @pl.when((step > 0) & (earlier < n_chunks))
        def _():
            copy(earlier).wait()

    for p in range(per_step):
        idx = step * per_step + p

        @pl.when(idx < n_chunks)
        def _():
            copy(idx).start()

        @pl.when((step == last) & (idx < n_chunks))
        def _():
            copy(idx).wait()


def _front_kernel(x_ref, hist_ref, s0_ref, g1_ref, w_ref, wlr2_ref, blr2_ref,
                  wdw_ref, bdw_ref, lg_ref, lb_ref, gn_ref,
                  c_ref, o_ref, tail_ref, sout_ref, *rest,
                  n_seq, seq_rows, chunk, heads, dk, dv, c_conv, width, zero_chunks):
    i = pl.program_id(1)
    qk, vv = heads * dk, heads * dv
    if zero_chunks:
        zero_ref, win, shifted, cbuf, state, zbuf, zsem = rest
        _zero_fill_step(zero_ref, zbuf, zsem, *zero_chunks)
    else:
        win, shifted, cbuf, state = rest

    @pl.when(i == 0)
    def _():
        for s in range(n_seq):
            win[s, 0:HIST_PAD, :] = hist_ref[s]
        state[...] = s0_ref[...]

    @pl.when(i > 0)
    def _():
        for s in range(n_seq):
            win[s, 0:HIST_PAD, :] = win[s, seq_rows:seq_rows + HIST_PAD, :]

    h = _rms(x_ref[...], g1_ref[...]).astype(BF16)

    def mm(lo, n):
        return jnp.dot(h, w_ref[:, lo:lo + n], preferred_element_type=F32)

    u = mm(0, c_conv) * jax.nn.sigmoid(mm(c_conv, c_conv))
    lead = HIST_PAD - (width - 1)
    span = shifted.shape[1]
    conv_rows = min(seq_rows, CONV_ROWS)
    for s in range(n_seq):
        r0 = s * seq_rows
        win[s, HIST_PAD:HIST_PAD + seq_rows, :] = u[r0:r0 + seq_rows, :]
        for cb in range(c_conv // LANES):
            cs = slice(cb * LANES, (cb + 1) * LANES)
            for r in range(1, 8):
                shifted[r - 1] = win[s, r:r + span, cs]
            for t0 in range(0, seq_rows, conv_rows):
                acc = jnp.broadcast_to(bdw_ref[:, cs], (conv_rows, LANES))
                for j in range(width):
                    r, a8 = (lead + j) % 8, (lead + j) // 8 * 8
                    if r == 0:
                        tap = win[s, t0 + a8:t0 + a8 + conv_rows, cs]
                    else:
                        tap = shifted[r - 1, t0 + a8:t0 + a8 + conv_rows, :]
                    acc = acc + wdw_ref[j:j + 1, cs] * tap
                cbuf[r0 + t0:r0 + t0 + conv_rows, cs] = acc
    cv = cbuf[...]
    mu = jnp.mean(cv, axis=-1, keepdims=True)
    xc = cv - mu
    cn = xc * lax.rsqrt(jnp.mean(xc * xc, axis=-1, keepdims=True) + EPS) * lg_ref[...] + lb_ref[...]
    c_ref[...] = _silu(cn).astype(c_ref.dtype)

    off = 2 * c_conv
    q = mm(off, qk) * (dk ** -0.5)
    k = mm(off + qk, qk)
    v = mm(off + 2 * qk, vv).astype(BF16)
    g = mm(off + 2 * qk + vv, vv)
    lr = mm(off + 2 * qk + 2 * vv, wlr2_ref.shape[0])
    z = jnp.dot(lr, wlr2_ref[...], precision=HIGHEST, preferred_element_type=F32) + blr2_ref[...]
    la = _log_sigmoid(z) * (1.0 / GATE_TEMP)

    n_sub = chunk // GLA_SUB
    r = lax.broadcasted_iota(I32, (chunk, chunk), 0)
    c = lax.broadcasted_iota(I32, (chunk, chunk), 1)
    causal = c <= r
    local_sum = (causal & ((r // GLA_SUB) == (c // GLA_SUB))).astype(BF16)
    sub_rows = [slice(j * GLA_SUB, (j + 1) * GLA_SUB) for j in range(n_sub)]
    eye = lax.broadcasted_iota(I32, (dk, dk), 0) == lax.broadcasted_iota(I32, (dk, dk), 1)
    for s in range(n_seq):
        for t0 in range(0, seq_rows, chunk):
            rows = slice(s * seq_rows + t0, s * seq_rows + t0 + chunk)
            la_1 = la[rows, :].astype(BF16)
            rest = la[rows, :] - la_1.astype(F32)
            la_2 = rest.astype(BF16)
            la_3 = (rest - la_2.astype(F32)).astype(BF16)
            sums = jnp.dot(local_sum, jnp.concatenate([la_1, la_2, la_3], axis=1), preferred_element_type=F32)
            local = sums[:, 0:qk] + sums[:, qk:2 * qk] + sums[:, 2 * qk:3 * qk]
            bases = [jnp.zeros((1, qk), F32)]
            for j in range(1, n_sub):
                bases.append(bases[-1] + local[j * GLA_SUB - 1:j * GLA_SUB, :])
            base = jnp.concatenate([jnp.broadcast_to(bs, (GLA_SUB, qk)) for bs in bases], axis=0)
            b = base + local
            b_end = b[chunk - 1:chunk, :]
            q_c, k_c = q[rows, :], k[rows, :]
            q_loc = q_c * jnp.exp(local)
            k_loc = k_c * jnp.exp(-local)
            q_in = (q_c * jnp.exp(b)).astype(BF16)
            k_out = (k_c * jnp.exp(b_end - b)).astype(BF16)
            decay_row = jnp.exp(b_end)
            zero_rows = jnp.zeros((GLA_SUB, qk), F32)
            q_parts, k_parts = [], []
            for j in range(n_sub):
                q_parts.append(jnp.concatenate(
                    [q_loc[sub_rows[m], :] * jnp.exp(jnp.minimum(bases[m] - bases[j], 0.0)) for m in range(n_sub)],
                    axis=0).astype(BF16))
                k_parts.append(jnp.concatenate(
                    [k_loc[sub_rows[m], :] if m == j else zero_rows for m in range(n_sub)], axis=0).astype(BF16))
            for hd in range(heads):
                ks = slice(hd * dk, (hd + 1) * dk)
                vs = slice(hd * dv, (hd + 1) * dv)
                vh = v[rows, vs]
                q_cat = jnp.concatenate([p[:, ks] for p in q_parts], axis=1)
                k_cat = jnp.concatenate([p[:, ks] for p in k_parts], axis=1)
                att = lax.dot_general(q_cat, k_cat, (((1,), (1,)), ((), ())), preferred_element_type=F32)
                att = jnp.where(causal, att, 0.0).astype(BF16)
                s_h = state[s, hd]
                o = jnp.dot(att, vh, preferred_element_type=F32)
                o = o + jnp.dot(q_in[:, ks], s_h.astype(BF16), preferred_element_type=F32)
                decay_col = jnp.sum(jnp.where(eye, jnp.broadcast_to(decay_row[:, ks], (dk, dk)), 0.0),
                                    axis=1, keepdims=True)
                state[s, hd] = decay_col * s_h + lax.dot_general(k_out[:, ks], vh, (((0,), (0,)), ((), ())),
                                                                 preferred_element_type=F32)
                o = o * lax.rsqrt(jnp.mean(o * o, axis=-1, keepdims=True) + EPS) * gn_ref[...]
                o_ref[rows, vs] = (o * _silu(g[rows, vs])).astype(o_ref.dtype)

    @pl.when(i == pl.num_programs(1) - 1)
    def _():
        for s in range(n_seq):
            tail_ref[s] = win[s, seq_rows:seq_rows + HIST_PAD, :]
        sout_ref[...] = state[...]


def _front(x2d, hist_pad, s0, g1, w_in, w_lr2, blr2, w_dw, b_dw, ln_g, ln_b, gn, *, n_seq, seq_rows,
           zero_shape=None):
    bsz, heads, dk, dv = s0.shape
    d = x2d.shape[1]
    t = x2d.shape[0] // bsz
    c_conv = w_dw.shape[1]
    width = w_dw.shape[0]
    qk, vv = heads * dk, heads * dv
    assert (n_seq == 1 and t % seq_rows == 0) or (seq_rows == t and bsz % n_seq == 0)
    nt = t // seq_rows
    rows = n_seq * seq_rows
    chunk = GLA_CHUNK if seq_rows % GLA_CHUNK == 0 else seq_rows
    assert chunk % GLA_SUB == 0 and seq_rows % min(seq_rows, CONV_ROWS) == 0
    const = lambda b, i: (0, 0)
    tile = lambda b, i: (b * nt + i, 0)
    per_seq3 = lambda b, i: (b, 0, 0)
    per_seq4 = lambda b, i: (b, 0, 0, 0)
    extra_out_specs, extra_out_shape, extra_scratch, zero_chunks = [], [], [], None
    if zero_shape is not None:
        assert zero_shape[0] % EXPERT_ROWS == 0
        n_chunks = zero_shape[0] // EXPERT_ROWS
        zero_chunks = (n_chunks, -(-n_chunks // ((bsz // n_seq) * nt)))
        extra_out_specs = [pl.BlockSpec(memory_space=pl.ANY)]
        extra_out_shape = [jax.ShapeDtypeStruct(zero_shape, U32)]
        extra_scratch = [pltpu.VMEM((EXPERT_ROWS, zero_shape[1]), U32), pltpu.SemaphoreType.DMA(())]
    return pl.pallas_call(
        functools.partial(_front_kernel, n_seq=n_seq, seq_rows=seq_rows, chunk=chunk, heads=heads, dk=dk, dv=dv,
                          c_conv=c_conv, width=width, zero_chunks=zero_chunks),
        grid=(bsz // n_seq, nt),
        in_specs=[
            pl.BlockSpec((rows, d), tile),
            pl.BlockSpec((n_seq, HIST_PAD, c_conv), per_seq3),
            pl.BlockSpec((n_seq, heads, dk, dv), per_seq4),
            pl.BlockSpec((1, d), const),
            pl.BlockSpec(w_in.shape, const, pipeline_mode=pl.Buffered(1)),
            pl.BlockSpec(w_lr2.shape, const),
            pl.BlockSpec((1, qk), const),
            pl.BlockSpec(w_dw.shape, const),
            pl.BlockSpec((1, c_conv), const), pl.BlockSpec((1, c_conv), const), pl.BlockSpec((1, c_conv), const),
            pl.BlockSpec((1, dv), const),
        ],
        out_specs=[
            pl.BlockSpec((rows, c_conv), tile), pl.BlockSpec((rows, vv), tile),
            pl.BlockSpec((n_seq, HIST_PAD, c_conv), per_seq3),
            pl.BlockSpec((n_seq, heads, dk, dv), per_seq4),
        ] + extra_out_specs,
        out_shape=[
            jax.ShapeDtypeStruct((bsz * t, c_conv), BF16), jax.ShapeDtypeStruct((bsz * t, vv), BF16),
            jax.ShapeDtypeStruct((bsz, HIST_PAD, c_conv), F32),
            jax.ShapeDtypeStruct((bsz, heads, dk, dv), F32),
        ] + extra_out_shape,
        scratch_shapes=[
            pltpu.VMEM((n_seq, seq_rows + HIST_PAD, c_conv), F32),
            pltpu.VMEM((7, seq_rows + HIST_PAD - 8, LANES), F32),
            pltpu.VMEM((rows, c_conv), F32),
            pltpu.VMEM((n_seq, heads, dk, dv), F32),
        ] + extra_scratch,
        compiler_params=_params(("arbitrary", "arbitrary")),
        name="front",
    )(x2d, hist_pad, s0, g1, w_in, w_lr2, blr2, w_dw, b_dw, ln_g, ln_b, gn)


def _outproj_kernel(xp_ref, xs_ref, cp_ref, cs_ref, op_ref, os_ref, w_ref, g2_ref, wr_ref, br_ref, zeroed_ref,
                    x1_ref, dest_ref, gates_ref, pages_ref, counts_ref, sorted_ref,
                    stage, dest_vmem, dest_smem, cnt_s, page_s, npage_s, table_s, scatter_sems, dest_sems,
                    *, n_first, c_conv, n_groups, per_group, trash_row):
    i = pl.program_id(0)
    n_steps = pl.num_programs(0)
    tm = x1_ref.shape[0]
    n_exp = n_groups * per_group
    slot = i % 2
    prev = 1 - slot

    def dest_copy(s):
        return pltpu.make_async_copy(dest_vmem.at[s], dest_smem.at[s], dest_sems.at[s])

    def scatter_wait(s):
        for _ in range(2):
            pltpu.make_async_copy(stage.at[s], sorted_ref.at[pl.ds(0, tm)], scatter_sems.at[s]).wait()

    @pl.when(i == 0)
    def _():
        cnt_s[...] = jnp.zeros_like(cnt_s)
        page_s[...] = jnp.zeros_like(page_s)
        npage_s[...] = jnp.zeros_like(npage_s)
        table_s[...] = jnp.zeros_like(table_s)
        stage[1] = jnp.zeros(stage.shape[1:], stage.dtype)

        def fill(t, carry):
            dest_smem[1, 0, t] = trash_row + t
            dest_smem[1, 1, t] = trash_row + tm + t
            return carry

        lax.fori_loop(0, tm, fill, 0)

    @pl.when(i >= 1)
    def _():
        dest_copy(prev).wait()
        scatter_wait(slot)

    for t in range(tm):
        _row_copy(stage.at[prev], t, sorted_ref, dest_smem[prev, 0, t], scatter_sems.at[prev]).start()
        _row_copy(stage.at[prev], t, sorted_ref, dest_smem[prev, 1, t], scatter_sems.at[prev]).start()

    x = _pick(i, n_first, xp_ref, xs_ref)
    cc = _pick(i, n_first, cp_ref, cs_ref)
    oo = _pick(i, n_first, op_ref, os_ref)
    del zeroed_ref
    mix = jnp.dot(jnp.concatenate([cc, oo], axis=1), w_ref[...], preferred_element_type=F32)
    x1 = x + mix
    x1_ref[...] = x1
    h2 = _rms(x1, g2_ref[...])
    stage[slot] = _pack_pairs(h2)
    h_hi = h2.astype(BF16)
    h_lo = (h2 - h_hi.astype(F32)).astype(BF16)
    parts = lax.dot_general(wr_ref[...], jnp.concatenate([h_hi, h_lo], axis=0), (((1,), (1,)), ((), ())),
                            preferred_element_type=F32)
    n_r = br_ref.shape[0]
    logits = (parts[0:n_r, 0:tm] + parts[0:n_r, tm:] + parts[n_r:, 0:tm] + parts[n_r:, tm:]) + br_ref[...]
    lc = logits[0:n_groups, :]
    mc = jnp.max(lc, axis=0, keepdims=True)
    p_group = 1.0 / jnp.sum(jnp.exp(lc - mc), axis=0, keepdims=True)
    rows_c = lax.broadcasted_iota(I32, (n_groups, tm), 0)
    g_idx = jnp.min(jnp.where(lc == mc, rows_c, n_groups), axis=0, keepdims=True)
    lf = logits[n_groups:n_groups + n_exp, :]
    rows_f = lax.broadcasted_iota(I32, (n_exp, tm), 0)
    in_group = (rows_f >= g_idx * per_group) & (rows_f < (g_idx + 1) * per_group)
    neg = jnp.float32(-jnp.inf)
    l1 = jnp.where(in_group, lf, neg)
    m1 = jnp.max(l1, axis=0, keepdims=True)
    e1 = jnp.min(jnp.where(l1 == m1, rows_f, n_exp), axis=0, keepdims=True)
    l2 = jnp.where(rows_f == e1, neg, l1)
    m2 = jnp.max(l2, axis=0, keepdims=True)
    e2 = jnp.min(jnp.where(l2 == m2, rows_f, n_exp), axis=0, keepdims=True)
    r2 = jnp.exp(m2 - m1)
    w1 = 1.0 / (1.0 + r2)
    row8 = lax.broadcasted_iota(I32, (8, tm), 0)
    gates_ref[...] = jnp.where(row8 == 0, p_group * w1, jnp.where(row8 == 1, p_group * (r2 * w1), 0.0))

    oh0 = (rows_f == e1).astype(F32)
    oh1 = (rows_f == e2).astype(F32)
    both = oh0 + oh1
    tr = lax.broadcasted_iota(I32, (tm, tm), 0)
    tc = lax.broadcasted_iota(I32, (tm, tm), 1)
    earlier = jnp.dot(both.astype(BF16), (tr < tc).astype(BF16), preferred_element_type=F32)
    cnt = cnt_s[...]
    rank_base = earlier + cnt
    tile_cnt = jnp.sum(both, axis=1, keepdims=True)
    page_rows = float(EXPERT_ROWS)
    k0 = jnp.floor(cnt * (1.0 / page_rows))
    new_cnt = cnt + tile_cnt
    limit = (k0 + 1.0) * page_rows
    need_a = ((cnt == k0 * page_rows) & (tile_cnt > 0.0)).astype(F32)
    need_b = (new_cnt > limit).astype(F32)
    need = need_a + need_b
    er = lax.broadcasted_iota(I32, (n_exp, n_exp), 0)
    ec = lax.broadcasted_iota(I32, (n_exp, n_exp), 1)
    before = jnp.dot((ec < er).astype(BF16), jnp.broadcast_to(need, (n_exp, LANES)).astype(BF16),
                     preferred_element_type=F32)[:, 0:1]
    base = npage_s[...] + before
    page_a = jnp.where(need_a > 0.0, base, page_s[...])
    page_b = base + need_a
    npage_s[...] = npage_s[...] + jnp.sum(need, axis=0, keepdims=True)
    lane = lax.broadcasted_iota(I32, table_s.shape, 1).astype(F32)
    table = jnp.where((lane == k0) & (need_a > 0.0), page_a, table_s[...])
    table_s[...] = jnp.where((lane == k0 + 1.0) & (need_b > 0.0), page_b, table)
    cnt_s[...] = new_cnt
    page_s[...] = jnp.where(jnp.floor(new_cnt * (1.0 / page_rows)) == k0, page_a, page_b)

    def dest_rows(oh):
        rank = jnp.sum(oh * rank_base, axis=0, keepdims=True)
        lim = jnp.sum(oh * limit, axis=0, keepdims=True)
        pa = jnp.sum(oh * page_a, axis=0, keepdims=True)
        pb = jnp.sum(oh * page_b, axis=0, keepdims=True)
        within = rank - jnp.floor(rank * (1.0 / page_rows)) * page_rows
        return jnp.where(rank < lim, pa, pb) * page_rows + within

    dest = jnp.where(row8 == 0, dest_rows(oh0), jnp.where(row8 == 1, dest_rows(oh1), 0.0)).astype(I32)
    dest_ref[...] = dest
    dest_vmem[slot] = dest
    dest_copy(slot).start()

    @pl.when(i == n_steps - 1)
    def _():
        pages_ref[...] = table_s[...].astype(I32)
        counts_ref[...] = jnp.broadcast_to(cnt_s[...], counts_ref.shape).astype(I32)
        dest_copy(slot).wait()
        scatter_wait(prev)

        def last(j, carry):
            for r in range(DMA_UNROLL):
                t = j * DMA_UNROLL + r
                _row_copy(stage.at[slot], t, sorted_ref, dest_smem[slot, 0, t], scatter_sems.at[slot]).start()
                _row_copy(stage.at[slot], t, sorted_ref, dest_smem[slot, 1, t], scatter_sems.at[slot]).start()
            return carry

        lax.fori_loop(0, tm // DMA_UNROLL, last, 0)
        scatter_wait(slot)


def _outproj(xp, xs, cp, cs, op, os_, w_out, g2, wr, br, zeroed, *, n_groups, per_group, n_pages):
    n_p, d = xp.shape
    n_all = n_p + xs.shape[0]
    n_first = n_p // ROW_TILE
    n_exp = n_groups * per_group
    c_conv = cp.shape[1]
    vv = op.shape[1]
    tile = (d // 2,)
    first, second = _split_maps(n_first)
    const = lambda i: (0, 0)
    row = lambda i: (i, 0)
    col = lambda i: (0, i)
    rows_sorted = zeroed.shape[0]
    assert rows_sorted >= n_pages * EXPERT_ROWS + 2 * ROW_TILE
    assert ROW_TILE <= EXPERT_ROWS, "a tile may open at most two pages per expert"
    return pl.pallas_call(
        functools.partial(_outproj_kernel, n_first=n_first, c_conv=c_conv, n_groups=n_groups, per_group=per_group,
                          trash_row=n_pages * EXPERT_ROWS),
        grid=(n_all // ROW_TILE,),
        in_specs=[
            pl.BlockSpec((ROW_TILE, d), first), pl.BlockSpec((ROW_TILE, d), second),
            pl.BlockSpec((ROW_TILE, c_conv), first), pl.BlockSpec((ROW_TILE, c_conv), second),
            pl.BlockSpec((ROW_TILE, vv), first), pl.BlockSpec((ROW_TILE, vv), second),
            pl.BlockSpec(w_out.shape, const, pipeline_mode=pl.Buffered(1)),
            pl.BlockSpec((1, d), const),
            pl.BlockSpec(wr.shape, const),
            pl.BlockSpec(br.shape, const),
            pl.BlockSpec(memory_space=pl.ANY),
        ],
        out_specs=[
            pl.BlockSpec((ROW_TILE, d), row),
            pl.BlockSpec((8, ROW_TILE), col), pl.BlockSpec((8, ROW_TILE), col),
            pl.BlockSpec((n_exp, LANES), const), pl.BlockSpec((n_exp, LANES), const),
            pl.BlockSpec(memory_space=pl.ANY),
        ],
        out_shape=[
            jax.ShapeDtypeStruct((n_all, d), F32),
            jax.ShapeDtypeStruct((8, n_all), I32), jax.ShapeDtypeStruct((8, n_all), F32),
            jax.ShapeDtypeStruct((n_exp, LANES), I32), jax.ShapeDtypeStruct((n_exp, LANES), I32),
            jax.ShapeDtypeStruct((rows_sorted,) + tile, U32),
        ],
        scratch_shapes=[
            pltpu.VMEM((2, ROW_TILE) + tile, U32),
            pltpu.VMEM((2, 8, ROW_TILE), I32), pltpu.SMEM((2, 8, ROW_TILE), I32),
            pltpu.VMEM((n_exp, 1), F32), pltpu.VMEM((n_exp, 1), F32), pltpu.VMEM((1, 1), F32),
            pltpu.VMEM((n_exp, LANES), F32),
            pltpu.SemaphoreType.DMA((2,)), pltpu.SemaphoreType.DMA((2,)),
        ],
        compiler_params=_params(("arbitrary",)),
        input_output_aliases={10: 5},
        name="outproj",
    )(xp, xs, cp, cs, op, os_, w_out, g2, wr, br, zeroed)


def _experts_kernel(cnt_ref, pages_ref, xs_ref, wg_ref, wu_ref, wd_ref, ysp_ref,
                    xbuf, ybuf, wg_f32, wu_f32, wd_f32, wg_bf, wu_bf, wd_bf, first_blk, page_seq,
                    gsems, ysems, wsems, *, n_exp, table_lanes):
    e = pl.program_id(0)
    tb = xbuf.shape[1]
    n_pages = page_seq.shape[0]

    def n_pages_of(ex):
        return (cnt_ref[ex] + (tb - 1)) // tb

    def page_rows(blk):
        return pl.ds(pl.multiple_of(page_seq[blk] * tb, tb), tb)

    def fetch(blk, slot):
        return pltpu.make_async_copy(xs_ref.at[page_rows(blk)], xbuf.at[slot], gsems.at[slot])

    def writeback(blk, slot):
        return pltpu.make_async_copy(ybuf.at[slot], ysp_ref.at[page_rows(blk)], ysems.at[slot])

    def weight_copies(ex, slot):
        return (pltpu.make_async_copy(wg_ref.at[ex], wg_f32.at[slot], wsems.at[slot]),
                pltpu.make_async_copy(wu_ref.at[ex], wu_f32.at[slot], wsems.at[slot]),
                pltpu.make_async_copy(wd_ref.at[ex], wd_f32.at[slot], wsems.at[slot]))

    @pl.when(e == 0)
    def _():
        for ahead in range(min(WEIGHT_SLOTS - 1, n_exp)):
            for cp in weight_copies(ahead, ahead):
                cp.start(priority=1)

        def per_expert(ex, blk):
            first_blk[ex] = blk

            def per_page(j, carry):
                page_seq[blk + j] = pages_ref[ex * table_lanes + j]
                return carry

            lax.fori_loop(0, n_pages_of(ex), per_page, 0)
            return blk + n_pages_of(ex)

        first_blk[n_exp] = lax.fori_loop(0, n_exp, per_expert, 0)
        fetch(0, 0).start()

    @pl.when(e + WEIGHT_SLOTS - 1 < n_exp)
    def _():
        for cp in weight_copies(e + WEIGHT_SLOTS - 1, (e + WEIGHT_SLOTS - 1) % WEIGHT_SLOTS):
            cp.start(priority=1)

    b_lo = first_blk[e]
    b_hi = first_blk[e + 1]
    n_total = first_blk[n_exp]
    wslot = e % WEIGHT_SLOTS
    for cp in weight_copies(e, wslot):
        cp.wait()
    wg_bf[...] = wg_f32[wslot].astype(BF16)
    wu_bf[...] = wu_f32[wslot].astype(BF16)
    wd_bf[...] = wd_f32[wslot].astype(BF16)

    def block(b, carry):
        slot = b % 2

        @pl.when(b >= 2)
        def _():
            writeback(b, slot).wait()

        fetch(b, slot).wait()
        fetch(jnp.minimum(b + 1, n_total - 1), 1 - slot).start()
        hi, lo = _unpack_pairs(xbuf[slot])
        x = jnp.concatenate([hi.astype(BF16), lo.astype(BF16)], axis=1)
        hg = jnp.dot(x, wg_bf[...], preferred_element_type=F32)
        hu = jnp.dot(x, wu_bf[...], preferred_element_type=F32)
        hb = (_silu(hg) * hu).astype(BF16)
        ybuf[slot] = _pack_pairs(jnp.dot(hb, wd_bf[...], preferred_element_type=F32))
        writeback(b, slot).start()
        return carry

    lax.fori_loop(b_lo, b_hi, block, 0)

    @pl.when(e == n_exp - 1)
    def _():
        fetch(0, n_total % 2).wait()

        @pl.when(n_total >= 2)
        def _():
            writeback(0, n_total % 2).wait()

        writeback(0, (n_total + 1) % 2).wait()
        ybuf[0] = jnp.zeros(ybuf.shape[1:], ybuf.dtype)

        def spare(blk):
            return pltpu.make_async_copy(ybuf.at[0], ysp_ref.at[pl.ds(pl.multiple_of(blk * tb, tb), tb)], ysems.at[0])

        def zero(blk, carry):
            spare(blk).start()
            return carry

        lax.fori_loop(n_total, n_pages, zero, 0)

        def zero_wait(blk, carry):
            spare(0).wait()
            return carry

        lax.fori_loop(n_total, n_pages, zero_wait, 0)


def _experts(counts, pages_flat, xs_sorted, w_gate, w_up, w_down, *, n_pages, table_lanes):
    tile = xs_sorted.shape[1:]
    n_exp, d, ff = w_gate.shape
    anyspec = pl.BlockSpec(memory_space=pl.ANY)
    grid_spec = pltpu.PrefetchScalarGridSpec(
        num_scalar_prefetch=2,
        grid=(n_exp,),
        in_specs=[anyspec, anyspec, anyspec, anyspec],
        out_specs=anyspec,
        scratch_shapes=[
            pltpu.VMEM((2, EXPERT_ROWS) + tile, U32), pltpu.VMEM((2, EXPERT_ROWS) + tile, U32),
            pltpu.VMEM((WEIGHT_SLOTS, d, ff), F32), pltpu.VMEM((WEIGHT_SLOTS, d, ff), F32),
            pltpu.VMEM((WEIGHT_SLOTS, ff, d), F32),
            pltpu.VMEM((d, ff), BF16), pltpu.VMEM((d, ff), BF16), pltpu.VMEM((ff, d), BF16),
            pltpu.SMEM((n_exp + 1,), I32), pltpu.SMEM((n_pages,), I32),
            pltpu.SemaphoreType.DMA((2,)), pltpu.SemaphoreType.DMA((2,)), pltpu.SemaphoreType.DMA((WEIGHT_SLOTS,)),
        ],
    )
    return pl.pallas_call(
        functools.partial(_experts_kernel, n_exp=n_exp, table_lanes=table_lanes),
        grid_spec=grid_spec,
        out_shape=jax.ShapeDtypeStruct((n_pages * EXPERT_ROWS,) + tile, U32),
        compiler_params=_params(("arbitrary",)),
        name="experts",
    )(counts, pages_flat, xs_sorted, w_gate, w_up, w_down)


def _combine_kernel(dest_ref, dest_next_ref, gates_ref, x1_ref, ysp_ref, gf_ref, yp_ref, ysmp_ref,
                    buf0, buf1, sems, *, n_first):
    i = pl.program_id(0)
    n = pl.num_programs(0)
    tm = x1_ref.shape[0]
    slot = i % 2

    def gather(d_ref, s):
        def body(j, carry):
            for r in range(DMA_UNROLL):
                t = j * DMA_UNROLL + r
                _row_copy(ysp_ref, d_ref[0, t], buf0.at[s], t, sems.at[s]).start(priority=0)
                _row_copy(ysp_ref, d_ref[1, t], buf1.at[s], t, sems.at[s]).start(priority=1)
            return carry

        lax.fori_loop(0, tm // DMA_UNROLL, body, 0)

    @pl.when(i == 0)
    def _():
        gather(dest_ref, 0)

    @pl.when(i + 1 < n)
    def _():
        gather(dest_next_ref, 1 - slot)

    pltpu.make_async_copy(ysp_ref.at[pl.ds(0, tm)], buf0.at[slot], sems.at[slot]).wait()
    pltpu.make_async_copy(ysp_ref.at[pl.ds(0, tm)], buf1.at[slot], sems.at[slot]).wait()
    hi0, lo0 = _unpack_pairs(buf0[slot])
    hi1, lo1 = _unpack_pairs(buf1[slot])
    g0 = gates_ref[:, 0:1]
    g1 = gates_ref[:, 1:2]
    moe = jnp.concatenate([g0 * hi0 + g1 * hi1, g0 * lo0 + g1 * lo1], axis=1)
    y = _rms(x1_ref[...] + moe, gf_ref[...])

    @pl.when(i < n_first)
    def _():
        yp_ref[...] = y

    @pl.when(i >= n_first)
    def _():
        ysmp_ref[...] = y


def _combine(dest, gates_t, x1, ysp, gf, *, n_p):
    n_all, d = x1.shape
    tile = ysp.shape[1:]
    n_first = n_p // ROW_TILE
    n_tiles = n_all // ROW_TILE
    first, second = _split_maps(n_first)
    return pl.pallas_call(
        functools.partial(_combine_kernel, n_first=n_first),
        grid=(n_tiles,),
        in_specs=[
            pl.BlockSpec((8, ROW_TILE), lambda i: (0, i), memory_space=pltpu.SMEM),
            pl.BlockSpec((8, ROW_TILE), lambda i: (0, jnp.minimum(i + 1, n_tiles - 1)), memory_space=pltpu.SMEM),
            pl.BlockSpec((ROW_TILE, 8), lambda i: (i, 0)),
            pl.BlockSpec((ROW_TILE, d), lambda i: (i, 0)),
            pl.BlockSpec(memory_space=pl.ANY),
            pl.BlockSpec((1, d), lambda i: (0, 0)),
        ],
        out_specs=[pl.BlockSpec((ROW_TILE, d), first), pl.BlockSpec((ROW_TILE, d), second)],
        out_shape=[jax.ShapeDtypeStruct((n_p, d), F32), jax.ShapeDtypeStruct((n_all - n_p, d), F32)],
        scratch_shapes=[pltpu.VMEM((2, ROW_TILE) + tile, U32), pltpu.VMEM((2, ROW_TILE) + tile, U32),
                        pltpu.SemaphoreType.DMA((2,))],
        compiler_params=_params(("arbitrary",)),
        name="combine",
    )(dest, dest, gates_t, x1, ysp, gf)


def _chunk_for(t):
    return 64 if t % 64 == 0 else t


def kernel(x_prompt, x_sample, cache_conv, state_gla, norm1_g, w_in, w_lr2, b_lr2, w_dw, b_dw, conv_ln_g, conv_ln_b, gla_norm_g, w_out, norm2_g, w_router_coarse, b_router_coarse, w_router_fine, b_router_fine, w_exp_gate, w_exp_up, w_exp_down, norm_f_g):
    assert norm1_g.shape[0] == 1, "single trunk layer"
    bp, tp, d = x_prompt.shape
    bs, ts, _ = x_sample.shape
    heads, dk, dv = state_gla.shape[2:]
    c_conv = w_dw.shape[2]
    width = w_dw.shape[1]
    rank = w_lr2.shape[1]
    qk, vv = heads * dk, heads * dv
    n_groups, _, per_group = w_router_fine.shape[1:]
    n_exp = n_groups * per_group
    n_p, n_s = bp * tp, bs * ts
    n_s_pad = -(-n_s // ROW_TILE) * ROW_TILE
    pad_rows = lambda a: jnp.pad(a, ((0, n_s_pad - n_s), (0, 0)))
    n_all = n_p + n_s_pad
    assert n_p % ROW_TILE == 0 and width - 1 <= HIST_PAD

    xp = x_prompt.reshape(n_p, d)
    xs = pad_rows(x_sample.reshape(n_s, d))
    row = lambda a: a.reshape(1, -1)

    mixer_args = (row(norm1_g[0]), w_in[0].astype(BF16), w_lr2[0], row(b_lr2[0]),
                  w_dw[0], row(b_dw[0]), row(conv_ln_g[0]), row(conv_ln_b[0]), row(gla_norm_g[0]))
    hist_p = jnp.zeros((bp, HIST_PAD, c_conv), F32)
    hist_s = jnp.pad(cache_conv[0], ((0, 0), (HIST_PAD - (width - 1), 0), (0, 0)))
    s0_p = jnp.zeros((bp, heads, dk, dv), F32)
    n_pages = (2 * n_all) // EXPERT_ROWS + n_exp
    sorted_rows = n_pages * EXPERT_ROWS + -(-2 * ROW_TILE // EXPERT_ROWS) * EXPERT_ROWS
    c_p, o_p, tail_p, gla_p, zeroed = _front(xp, hist_p, s0_p, *mixer_args, n_seq=1, seq_rows=ROW_TILE,
                                             zero_shape=(sorted_rows, d // 2))
    c_s, o_s, tail_s, gla_s = _front(x_sample.reshape(n_s, d), hist_s, state_gla[0], *mixer_args,
                                     n_seq=bs, seq_rows=ts)
    c_s, o_s = pad_rows(c_s), pad_rows(o_s)

    wr = jnp.concatenate([w_router_coarse[0].T,
                          jnp.transpose(w_router_fine[0], (0, 2, 1)).reshape(n_exp, d)], axis=0)
    br = jnp.concatenate([b_router_coarse[0], b_router_fine[0].reshape(n_exp)])
    r_rows = -(-(n_groups + n_exp) // 8) * 8
    wr = jnp.pad(wr, ((0, r_rows - wr.shape[0]), (0, 0)))
    br = jnp.pad(br, (0, r_rows - br.shape[0])).reshape(r_rows, 1)
    wr_hi = wr.astype(BF16)
    wr = jnp.concatenate([wr_hi, (wr - wr_hi.astype(F32)).astype(BF16)], axis=0)
    n_pages = (2 * n_all) // EXPERT_ROWS + n_exp
    assert n_all // EXPERT_ROWS + 2 <= LANES, "page table row must hold one expert's pages"
    x1, dest, gates, pages, counts, xs_sorted = _outproj(
        xp, xs, c_p, c_s, o_p, o_s, w_out[0].astype(BF16), row(norm2_g[0]), wr, br, zeroed,
        n_groups=n_groups, per_group=per_group, n_pages=n_pages)
    ysp = _experts(counts[:, 0], pages.reshape(-1), xs_sorted, w_exp_gate[0], w_exp_up[0], w_exp_down[0],
                   n_pages=n_pages, table_lanes=LANES)
    y_p, y_s = _combine(dest, gates.T, x1, ysp, row(norm_f_g), n_p=n_p)

    lead = HIST_PAD - (width - 1)
    return (y_p.reshape(bp, tp, d), y_s[:n_s].reshape(bs, ts, d), tail_p[:, lead:][None], gla_p[None],
            tail_s[:, lead:][None], gla_s[None])
```

```python
import functools

import jax
import jax.numpy as jnp
from jax import lax
from jax.experimental import pallas as pl
from jax.experimental.pallas import tpu as pltpu

F32 = jnp.float32
BF16 = jnp.bfloat16
I32 = jnp.int32
U32 = jnp.uint32
EPS = 1e-6
GATE_TEMP = 16.0
HIGHEST = lax.Precision.HIGHEST

LANES = 128
ROW_TILE = 256
EXPERT_ROWS = 256
HIST_PAD = 32
DMA_UNROLL = 8
WEIGHT_SLOTS = 3
VMEM_LIMIT = 56 * 1024 * 1024


def _params(semantics, vmem=VMEM_LIMIT):
    return pltpu.CompilerParams(dimension_semantics=semantics, vmem_limit_bytes=vmem)


def _rms(x, g):
    return x * lax.rsqrt(jnp.mean(x * x, axis=-1, keepdims=True) + EPS) * g


def _silu(x):
    return x * jax.nn.sigmoid(x)


def _log_sigmoid(z):
    return jnp.minimum(z, 0.0) - jnp.log(1.0 + jnp.exp(-jnp.abs(z)))


def _pick(i, n_first, first_ref, second_ref):
    return jnp.where(i < n_first, first_ref[...], second_ref[...])


def _split_maps(n_first):
    first = lambda i: (jnp.minimum(i, n_first - 1), 0)
    second = lambda i: (jnp.maximum(i - n_first, 0), 0)
    return first, second


def _pack_pairs(x):
    half = x.shape[1] // 2
    hi = lax.bitcast_convert_type(x[:, :half].astype(BF16).astype(F32), U32)
    lo = lax.bitcast_convert_type(x[:, half:].astype(BF16).astype(F32), U32)
    return hi | (lo >> 16)


def _unpack_pairs(p):
    hi = lax.bitcast_convert_type(p & jnp.uint32(0xFFFF0000), F32)
    lo = lax.bitcast_convert_type(p << 16, F32)
    return hi, lo


def _row_copy(src, s, dst, d, sem):
    return pltpu.make_async_copy(src.at[pl.ds(s, 1)], dst.at[pl.ds(d, 1)], sem)


def _inproj_kernel(xp_ref, xs_ref, g1_ref, w_ref, wlr2_ref, blr2_ref,
                   u_ref, q_ref, k_ref, v_ref, g_ref, la_ref, *, n_first, c_conv, qk, vv, dk):
    i = pl.program_id(0)
    x = _pick(i, n_first, xp_ref, xs_ref)
    h = _rms(x, g1_ref[...]).astype(BF16)

    def mm(lo, width):
        return jnp.dot(h, w_ref[:, lo:lo + width], preferred_element_type=F32)

    a = mm(0, c_conv)
    a_gate = mm(c_conv, c_conv)
    u_ref[...] = a * jax.nn.sigmoid(a_gate)
    off = 2 * c_conv
    q_ref[...] = mm(off, qk) * (dk ** -0.5)
    k_ref[...] = mm(off + qk, qk)
    v_ref[...] = mm(off + 2 * qk, vv)
    g_ref[...] = mm(off + 2 * qk + vv, vv)
    lr = mm(off + 2 * qk + 2 * vv, LANES)
    z = jnp.dot(lr, wlr2_ref[...], precision=HIGHEST, preferred_element_type=F32) + blr2_ref[...]
    la_ref[...] = _log_sigmoid(z) * (1.0 / GATE_TEMP)


def _inproj(xp, xs, g1, w_pad, wlr2_pad, blr2, *, c_conv, qk, vv, dk):
    n_p, d = xp.shape
    n_s = xs.shape[0]
    n_all = n_p + n_s
    n_first = n_p // ROW_TILE
    grid = (n_all // ROW_TILE,)
    first, second = _split_maps(n_first)
    const = lambda i: (0, 0)
    row = lambda i: (i, 0)
    widths = (c_conv, qk, qk, vv, vv, qk)
    return pl.pallas_call(
        functools.partial(_inproj_kernel, n_first=n_first, c_conv=c_conv, qk=qk, vv=vv, dk=dk),
        grid=grid,
        in_specs=[
            pl.BlockSpec((ROW_TILE, d), first),
            pl.BlockSpec((ROW_TILE, d), second),
            pl.BlockSpec((1, d), const),
            pl.BlockSpec(w_pad.shape, const, pipeline_mode=pl.Buffered(1)),
            pl.BlockSpec(wlr2_pad.shape, const),
            pl.BlockSpec((1, qk), const),
        ],
        out_specs=[pl.BlockSpec((ROW_TILE, w), row) for w in widths],
        out_shape=[jax.ShapeDtypeStruct((n_all, w), F32) for w in widths],
        compiler_params=_params(("arbitrary",)),
        name="inproj",
    )(xp, xs, g1, w_pad, wlr2_pad, blr2)


def _conv_kernel(u_ref, hist_ref, w_ref, b_ref, lg_ref, lb_ref, c_ref, win, cbuf, *, tt, width):
    i = pl.program_id(1)

    @pl.when(i == 0)
    def _():
        win[0:HIST_PAD, :] = hist_ref[0]

    @pl.when(i > 0)
    def _():
        win[0:HIST_PAD, :] = win[tt:tt + HIST_PAD, :]

    win[HIST_PAD:HIST_PAD + tt, :] = u_ref[...]
    lead = HIST_PAD - (width - 1)
    n_ch = u_ref.shape[1]
    for cb in range(n_ch // LANES):
        cs = slice(cb * LANES, (cb + 1) * LANES)
        acc = jnp.broadcast_to(b_ref[:, cs], (tt, LANES))
        for j in range(width):
            acc = acc + w_ref[j:j + 1, cs] * win[lead + j:lead + j + tt, cs]
        cbuf[:, cs] = acc
    c = cbuf[...]
    mu = jnp.mean(c, axis=-1, keepdims=True)
    xc = c - mu
    y = xc * lax.rsqrt(jnp.mean(xc * xc, axis=-1, keepdims=True) + EPS) * lg_ref[...] + lb_ref[...]
    c_ref[...] = _silu(y).astype(c_ref.dtype)


def _conv(u_all, hist_pad, w_dw, b_dw, ln_g, ln_b, *, row0, bsz, t, tt):
    n_ch = u_all.shape[1]
    width = w_dw.shape[0]
    nt = t // tt
    blk0 = row0 // tt
    const = lambda b, i: (0, 0)
    return pl.pallas_call(
        functools.partial(_conv_kernel, tt=tt, width=width),
        grid=(bsz, nt),
        in_specs=[
            pl.BlockSpec((tt, n_ch), lambda b, i: (blk0 + b * nt + i, 0)),
            pl.BlockSpec((1, HIST_PAD, n_ch), lambda b, i: (b, 0, 0)),
            pl.BlockSpec(w_dw.shape, const),
            pl.BlockSpec((1, n_ch), const),
            pl.BlockSpec((1, n_ch), const),
            pl.BlockSpec((1, n_ch), const),
        ],
        out_specs=pl.BlockSpec((tt, n_ch), lambda b, i: (b * nt + i, 0)),
        out_shape=jax.ShapeDtypeStruct((bsz * t, n_ch), BF16),
        scratch_shapes=[pltpu.VMEM((tt + HIST_PAD, n_ch), F32), pltpu.VMEM((tt, n_ch), F32)],
        compiler_params=_params(("arbitrary", "arbitrary")),
        name="conv",
    )(u_all, hist_pad, w_dw, b_dw, ln_g, ln_b)


def _gla_kernel(q_ref, k_ref, v_ref, g_ref, la_ref, s0_ref, gn_ref, o_ref, sout_ref, state,
                *, chunk, heads, dk, dv):
    i = pl.program_id(1)

    @pl.when(i == 0)
    def _():
        state[...] = s0_ref[0]

    la = la_ref[...]
    r = lax.broadcasted_iota(I32, (chunk, chunk), 0)
    c = lax.broadcasted_iota(I32, (chunk, chunk), 1)
    causal = c <= r
    b = jnp.dot(causal.astype(F32), la, precision=HIGHEST, preferred_element_type=F32)
    b_end = b[chunk - 1:chunk, :]
    q_in = (q_ref[...] * jnp.exp(b)).astype(BF16)
    k_in = (k_ref[...] * jnp.exp(-b)).astype(BF16)
    k_out = (k_ref[...] * jnp.exp(b_end - b)).astype(BF16)
    decay_row = jnp.exp(b_end)
    eye = lax.broadcasted_iota(I32, (dk, dk), 0) == lax.broadcasted_iota(I32, (dk, dk), 1)
    for h in range(heads):
        ks = slice(h * dk, (h + 1) * dk)
        vs = slice(h * dv, (h + 1) * dv)
        vh = v_ref[:, vs].astype(BF16)
        att = lax.dot_general(q_in[:, ks], k_in[:, ks], (((1,), (1,)), ((), ())), preferred_element_type=F32)
        att = jnp.where(causal, att, 0.0).astype(BF16)
        s_h = state[h]
        o = jnp.dot(att, vh, preferred_element_type=F32)
        o = o + jnp.dot(q_in[:, ks], s_h.astype(BF16), preferred_element_type=F32)
        decay_col = jnp.sum(jnp.where(eye, jnp.broadcast_to(decay_row[:, ks], (dk, dk)), 0.0), axis=1, keepdims=True)
        state[h] = decay_col * s_h + lax.dot_general(k_out[:, ks], vh, (((0,), (0,)), ((), ())),
                                                     preferred_element_type=F32)
        o = o * lax.rsqrt(jnp.mean(o * o, axis=-1, keepdims=True) + EPS) * gn_ref[...]
        o_ref[:, vs] = (o * _silu(g_ref[:, vs])).astype(o_ref.dtype)

    @pl.when(i == pl.num_programs(1) - 1)
    def _():
        sout_ref[0] = state[...]


def _gla(q_all, k_all, v_all, g_all, la_all, s0, gn, *, row0, bsz, t, chunk):
    heads, dk, dv = s0.shape[1:]
    nt = t // chunk
    blk0 = row0 // chunk
    rows = lambda b, i: (blk0 + b * nt + i, 0)
    return pl.pallas_call(
        functools.partial(_gla_kernel, chunk=chunk, heads=heads, dk=dk, dv=dv),
        grid=(bsz, nt),
        in_specs=[
            pl.BlockSpec((chunk, heads * dk), rows),
            pl.BlockSpec((chunk, heads * dk), rows),
            pl.BlockSpec((chunk, heads * dv), rows),
            pl.BlockSpec((chunk, heads * dv), rows),
            pl.BlockSpec((chunk, heads * dk), rows),
            pl.BlockSpec((1, heads, dk, dv), lambda b, i: (b, 0, 0, 0)),
            pl.BlockSpec((1, dv), lambda b, i: (0, 0)),
        ],
        out_specs=[
            pl.BlockSpec((chunk, heads * dv), lambda b, i: (b * nt + i, 0)),
            pl.BlockSpec((1, heads, dk, dv), lambda b, i: (b, 0, 0, 0)),
        ],
        out_shape=[
            jax.ShapeDtypeStruct((bsz * t, heads * dv), BF16),
            jax.ShapeDtypeStruct((bsz, heads, dk, dv), F32),
        ],
        scratch_shapes=[pltpu.VMEM((heads, dk, dv), F32)],
        compiler_params=_params(("arbitrary", "arbitrary")),
        name="gla",
    )(q_all, k_all, v_all, g_all, la_all, s0, gn)


def _mixer_kernel(x_ref, hist_ref, s0_ref, g1_ref, w_ref, wlr2_ref, blr2_ref,
                  wdw_ref, bdw_ref, lg_ref, lb_ref, gn_ref,
                  c_ref, o_ref, tail_ref, sout_ref, pu, pq, pk, pv, pg, pla, win, shifted, cbuf, state,
                  *, n_seq, seq_rows, chunk, heads, dk, dv, c_conv, width):
    i = pl.program_id(1)
    qk, vv = heads * dk, heads * dv

    @pl.when(i == 0)
    def _():
        for buf in (pu, pq, pk, pv, pg, pla):
            buf[...] = jnp.zeros_like(buf)
        win[...] = jnp.zeros_like(win)
        state[...] = jnp.zeros_like(state)

    @pl.when(i == 1)
    def _():
        for s in range(n_seq):
            win[s, 0:HIST_PAD, :] = hist_ref[s]
        state[...] = s0_ref[...]

    @pl.when(i > 1)
    def _():
        for s in range(n_seq):
            win[s, 0:HIST_PAD, :] = win[s, seq_rows:seq_rows + HIST_PAD, :]

    h = _rms(x_ref[...], g1_ref[...]).astype(BF16)

    def mm(lo, n):
        return jnp.dot(h, w_ref[:, lo:lo + n], preferred_element_type=F32)

    off = 2 * c_conv
    new = {}

    def proj_u(j, n):
        lo = j * n
        new["u", j] = mm(lo, n) * jax.nn.sigmoid(mm(c_conv + lo, n))

    def proj(name, lo, n, scale=None, dtype=F32):
        val = mm(lo, n)
        new[name] = (val if scale is None else val * scale).astype(dtype)

    def proj_la():
        lr = mm(off + 2 * qk + 2 * vv, wlr2_ref.shape[0])
        z = jnp.dot(lr, wlr2_ref[...], precision=HIGHEST, preferred_element_type=F32) + blr2_ref[...]
        new["la"] = _log_sigmoid(z) * (1.0 / GATE_TEMP)

    half_c, half_v = c_conv // 2, vv // 2
    stage1 = [
        functools.partial(proj_u, 0, half_c), functools.partial(proj_u, 1, half_c),
        functools.partial(proj, "q", off, qk, dk ** -0.5), functools.partial(proj, "k", off + qk, qk),
        functools.partial(proj, ("v", 0), off + 2 * qk, half_v, None, BF16),
        functools.partial(proj, ("v", 1), off + 2 * qk + half_v, half_v, None, BF16),
        functools.partial(proj, ("g", 0), off + 2 * qk + vv, half_v),
        functools.partial(proj, ("g", 1), off + 2 * qk + vv + half_v, half_v),
        proj_la,
    ]

    lead = HIST_PAD - (width - 1)
    span = shifted.shape[1]

    def conv_fill(s):
        r0 = s * seq_rows
        win[s, HIST_PAD:HIST_PAD + seq_rows, :] = pu[r0:r0 + seq_rows, :]

    def conv_block(s, cb):
        r0 = s * seq_rows
        cs = slice(cb * LANES, (cb + 1) * LANES)
        for r in range(1, 8):
            shifted[r - 1] = win[s, r:r + span, cs]
        for t0 in range(0, seq_rows, chunk):
            acc = jnp.broadcast_to(bdw_ref[:, cs], (chunk, LANES))
            for j in range(width):
                r, a8 = (lead + j) % 8, (lead + j) // 8 * 8
                if r == 0:
                    tap = win[s, t0 + a8:t0 + a8 + chunk, cs]
                else:
                    tap = shifted[r - 1, t0 + a8:t0 + a8 + chunk, :]
                acc = acc + wdw_ref[j:j + 1, cs] * tap
            cbuf[r0 + t0:r0 + t0 + chunk, cs] = acc

    def conv_norm():
        cv = cbuf[...]
        mu = jnp.mean(cv, axis=-1, keepdims=True)
        xc = cv - mu
        cn = xc * lax.rsqrt(jnp.mean(xc * xc, axis=-1, keepdims=True) + EPS) * lg_ref[...] + lb_ref[...]
        c_ref[...] = _silu(cn).astype(c_ref.dtype)

    r = lax.broadcasted_iota(I32, (chunk, chunk), 0)
    c = lax.broadcasted_iota(I32, (chunk, chunk), 1)
    causal = c <= r
    tril = causal.astype(F32)
    eye = lax.broadcasted_iota(I32, (dk, dk), 0) == lax.broadcasted_iota(I32, (dk, dk), 1)
    gla = {}

    def gla_prep(s, t0):
        rows = slice(s * seq_rows + t0, s * seq_rows + t0 + chunk)
        b = jnp.dot(tril, pla[rows, :], precision=HIGHEST, preferred_element_type=F32)
        b_end = b[chunk - 1:chunk, :]
        k_c = pk[rows, :]
        gla[s, t0] = ((pq[rows, :] * jnp.exp(b)).astype(BF16), (k_c * jnp.exp(-b)).astype(BF16),
                      (k_c * jnp.exp(b_end - b)).astype(BF16), jnp.exp(b_end))

    def gla_head(s, t0, hd):
        rows = slice(s * seq_rows + t0, s * seq_rows + t0 + chunk)
        q_in, k_in, k_out, decay_row = gla[s, t0]
        ks = slice(hd * dk, (hd + 1) * dk)
        vs = slice(hd * dv, (hd + 1) * dv)
        vh = pv[rows, vs]
        att = lax.dot_general(q_in[:, ks], k_in[:, ks], (((1,), (1,)), ((), ())), preferred_element_type=F32)
        att = jnp.where(causal, att, 0.0).astype(BF16)
        s_h = state[s, hd]
        o = jnp.dot(att, vh, preferred_element_type=F32)
        o = o + jnp.dot(q_in[:, ks], s_h.astype(BF16), preferred_element_type=F32)
        decay_col = jnp.sum(jnp.where(eye, jnp.broadcast_to(decay_row[:, ks], (dk, dk)), 0.0), axis=1, keepdims=True)
        state[s, hd] = decay_col * s_h + lax.dot_general(k_out[:, ks], vh, (((0,), (0,)), ((), ())),
                                                         preferred_element_type=F32)
        o = o * lax.rsqrt(jnp.mean(o * o, axis=-1, keepdims=True) + EPS) * gn_ref[...]
        o_ref[rows, vs] = (o * _silu(pg[rows, vs])).astype(o_ref.dtype)

    stage2 = []
    for s in range(n_seq):
        stage2.append(functools.partial(conv_fill, s))
        stage2 += [functools.partial(conv_block, s, cb) for cb in range(c_conv // LANES)]
    stage2.append(conv_norm)
    for s in range(n_seq):
        for t0 in range(0, seq_rows, chunk):
            stage2.append(functools.partial(gla_prep, s, t0))
            stage2 += [functools.partial(gla_head, s, t0, hd) for hd in range(heads)]

    per = -(-len(stage2) // len(stage1))
    for n, piece in enumerate(stage1):
        piece()
        for other in stage2[n * per:(n + 1) * per]:
            other()

    pu[...] = jnp.concatenate([new["u", 0], new["u", 1]], axis=1)
    pq[...] = new["q"]
    pk[...] = new["k"]
    pv[...] = jnp.concatenate([new["v", 0], new["v", 1]], axis=1)
    pg[...] = jnp.concatenate([new["g", 0], new["g", 1]], axis=1)
    pla[...] = new["la"]

    @pl.when(i == pl.num_programs(1) - 1)
    def _():
        for s in range(n_seq):
            tail_ref[s] = win[s, seq_rows:seq_rows + HIST_PAD, :]
        sout_ref[...] = state[...]


def _mixer(x2d, hist_pad, s0, g1, w_in, w_lr2, blr2, w_dw, b_dw, ln_g, ln_b, gn, *, n_seq, seq_rows):
    bsz, heads, dk, dv = s0.shape
    d = x2d.shape[1]
    t = x2d.shape[0] // bsz
    c_conv = w_dw.shape[1]
    width = w_dw.shape[0]
    qk, vv = heads * dk, heads * dv
    assert (n_seq == 1 and t % seq_rows == 0) or (seq_rows == t and bsz % n_seq == 0)
    nt = t // seq_rows
    rows = n_seq * seq_rows
    span = seq_rows + HIST_PAD - 8
    chunk = 64 if seq_rows % 64 == 0 else seq_rows
    const = lambda b, i: (0, 0)
    tile_in = lambda b, i: (b * nt + jnp.minimum(i, nt - 1), 0)
    tile_out = lambda b, i: (b * nt + jnp.maximum(i - 1, 0), 0)
    per_seq3 = lambda b, i: (b, 0, 0)
    per_seq4 = lambda b, i: (b, 0, 0, 0)
    return pl.pallas_call(
        functools.partial(_mixer_kernel, n_seq=n_seq, seq_rows=seq_rows, chunk=chunk, heads=heads, dk=dk, dv=dv,
                          c_conv=c_conv, width=width),
        grid=(bsz // n_seq, nt + 1),
        in_specs=[
            pl.BlockSpec((rows, d), tile_in),
            pl.BlockSpec((n_seq, HIST_PAD, c_conv), per_seq3),
            pl.BlockSpec((n_seq, heads, dk, dv), per_seq4),
            pl.BlockSpec((1, d), const),
            pl.BlockSpec(w_in.shape, const, pipeline_mode=pl.Buffered(1)),
            pl.BlockSpec(w_lr2.shape, const),
            pl.BlockSpec((1, qk), const),
            pl.BlockSpec(w_dw.shape, const),
            pl.BlockSpec((1, c_conv), const), pl.BlockSpec((1, c_conv), const), pl.BlockSpec((1, c_conv), const),
            pl.BlockSpec((1, dv), const),
        ],
        out_specs=[
            pl.BlockSpec((rows, c_conv), tile_out), pl.BlockSpec((rows, vv), tile_out),
            pl.BlockSpec((n_seq, HIST_PAD, c_conv), per_seq3),
            pl.BlockSpec((n_seq, heads, dk, dv), per_seq4),
        ],
        out_shape=[
            jax.ShapeDtypeStruct((bsz * t, c_conv), BF16), jax.ShapeDtypeStruct((bsz * t, vv), BF16),
            jax.ShapeDtypeStruct((bsz, HIST_PAD, c_conv), F32),
            jax.ShapeDtypeStruct((bsz, heads, dk, dv), F32),
        ],
        scratch_shapes=[
            pltpu.VMEM((rows, c_conv), F32), pltpu.VMEM((rows, qk), F32), pltpu.VMEM((rows, qk), F32),
            pltpu.VMEM((rows, vv), BF16), pltpu.VMEM((rows, vv), F32), pltpu.VMEM((rows, qk), F32),
            pltpu.VMEM((n_seq, seq_rows + HIST_PAD, c_conv), F32),
            pltpu.VMEM((7, span, LANES), F32),
            pltpu.VMEM((rows, c_conv), F32),
            pltpu.VMEM((n_seq, heads, dk, dv), F32),
        ],
        compiler_params=_params(("arbitrary", "arbitrary")),
        name="mixer",
    )(x2d, hist_pad, s0, g1, w_in, w_lr2, blr2, w_dw, b_dw, ln_g, ln_b, gn)


GLA_SUB = 16
GLA_CHUNK = 128
CONV_ROWS = 64


def _zero_fill_step(zero_ref, zbuf, zsem, n_chunks, per_step):
    step = pl.program_id(0) * pl.num_programs(1) + pl.program_id(1)
    last = pl.num_programs(0) * pl.num_programs(1) - 1
    rows = zbuf.shape[0]

    def copy(idx):
        return pltpu.make_async_copy(zbuf, zero_ref.at[pl.ds(pl.multiple_of(idx * rows, rows), rows)], zsem)

    @pl.when(step == 0)
    def _():
        zbuf[...] = jnp.zeros_like(zbuf)

    for p in range(per_step):
        earlier = (step - 1) * per_step + p

        @pl.when((step > 0) & (earlier < n_chunks))
        def _():
            copy(earlier).wait()

    for p in range(per_step):
        idx = step * per_step + p

        @pl.when(idx < n_chunks)
        def _():
            copy(idx).start()

        @pl.when((step == last) & (idx < n_chunks))
        def _():
            copy(idx).wait()


def _front_kernel(x_ref, hist_ref, s0_ref, g1_ref, w_ref, wlr2_ref, blr2_ref,
                  wdw_ref, bdw_ref, lg_ref, lb_ref, gn_ref,
                  c_ref, o_ref, tail_ref, sout_ref, *rest,
                  n_seq, seq_rows, chunk, heads, dk, dv, c_conv, width, zero_chunks):
    i = pl.program_id(1)
    qk, vv = heads * dk, heads * dv
    if zero_chunks:
        zero_ref, win, shifted, cbuf, state, zbuf, zsem = rest
        _zero_fill_step(zero_ref, zbuf, zsem, *zero_chunks)
    else:
        win, shifted, cbuf, state = rest

    @pl.when(i == 0)
    def _():
        for s in range(n_seq):
            win[s, 0:HIST_PAD, :] = hist_ref[s]
        state[...] = s0_ref[...]

    @pl.when(i > 0)
    def _():
        for s in range(n_seq):
            win[s, 0:HIST_PAD, :] = win[s, seq_rows:seq_rows + HIST_PAD, :]

    h = _rms(x_ref[...], g1_ref[...]).astype(BF16)

    def mm(lo, n):
        return jnp.dot(h, w_ref[:, lo:lo + n], preferred_element_type=F32)

    u = mm(0, c_conv) * jax.nn.sigmoid(mm(c_conv, c_conv))
    lead = HIST_PAD - (width - 1)
    span = shifted.shape[1]
    conv_rows = min(seq_rows, CONV_ROWS)
    for s in range(n_seq):
        r0 = s * seq_rows
        win[s, HIST_PAD:HIST_PAD + seq_rows, :] = u[r0:r0 + seq_rows, :]
        for cb in range(c_conv // LANES):
            cs = slice(cb * LANES, (cb + 1) * LANES)
            for r in range(1, 8):
                shifted[r - 1] = win[s, r:r + span, cs]
            for t0 in range(0, seq_rows, conv_rows):
                acc = jnp.broadcast_to(bdw_ref[:, cs], (conv_rows, LANES))
                for j in range(width):
                    r, a8 = (lead + j) % 8, (lead + j) // 8 * 8
                    if r == 0:
                        tap = win[s, t0 + a8:t0 + a8 + conv_rows, cs]
                    else:
                        tap = shifted[r - 1, t0 + a8:t0 + a8 + conv_rows, :]
                    acc = acc + wdw_ref[j:j + 1, cs] * tap
                cbuf[r0 + t0:r0 + t0 + conv_rows, cs] = acc
    cv = cbuf[...]
    mu = jnp.mean(cv, axis=-1, keepdims=True)
    xc = cv - mu
    cn = xc * lax.rsqrt(jnp.mean(xc * xc, axis=-1, keepdims=True) + EPS) * lg_ref[...] + lb_ref[...]
    c_ref[...] = _silu(cn).astype(c_ref.dtype)

    off = 2 * c_conv
    q = mm(off, qk) * (dk ** -0.5)
    k = mm(off + qk, qk)
    v = mm(off + 2 * qk, vv).astype(BF16)
    g = mm(off + 2 * qk + vv, vv)
    lr = mm(off + 2 * qk + 2 * vv, wlr2_ref.shape[0])
    z = jnp.dot(lr, wlr2_ref[...], precision=HIGHEST, preferred_element_type=F32) + blr2_ref[...]
    la = _log_sigmoid(z) * (1.0 / GATE_TEMP)

    n_sub = chunk // GLA_SUB
    r = lax.broadcasted_iota(I32, (chunk, chunk), 0)
    c = lax.broadcasted_iota(I32, (chunk, chunk), 1)
    causal = c <= r
    local_sum = (causal & ((r // GLA_SUB) == (c // GLA_SUB))).astype(BF16)
    sub_rows = [slice(j * GLA_SUB, (j + 1) * GLA_SUB) for j in range(n_sub)]
    eye = lax.broadcasted_iota(I32, (dk, dk), 0) == lax.broadcasted_iota(I32, (dk, dk), 1)
    for s in range(n_seq):
        for t0 in range(0, seq_rows, chunk):
            rows = slice(s * seq_rows + t0, s * seq_rows + t0 + chunk)
            la_1 = la[rows, :].astype(BF16)
            rest = la[rows, :] - la_1.astype(F32)
            la_2 = rest.astype(BF16)
            la_3 = (rest - la_2.astype(F32)).astype(BF16)
            sums = jnp.dot(local_sum, jnp.concatenate([la_1, la_2, la_3], axis=1), preferred_element_type=F32)
            local = sums[:, 0:qk] + sums[:, qk:2 * qk] + sums[:, 2 * qk:3 * qk]
            bases = [jnp.zeros((1, qk), F32)]
            for j in range(1, n_sub):
                bases.append(bases[-1] + local[j * GLA_SUB - 1:j * GLA_SUB, :])
            base = jnp.concatenate([jnp.broadcast_to(bs, (GLA_SUB, qk)) for bs in bases], axis=0)
            b = base + local
            b_end = b[chunk - 1:chunk, :]
            q_c, k_c = q[rows, :], k[rows, :]
            q_loc = q_c * jnp.exp(local)
            k_loc = k_c * jnp.exp(-local)
            q_in = (q_c * jnp.exp(b)).astype(BF16)
            k_out = (k_c * jnp.exp(b_end - b)).astype(BF16)
            decay_row = jnp.exp(b_end)
            zero_rows = jnp.zeros((GLA_SUB, qk), F32)
            q_parts, k_parts = [], []
            for j in range(n_sub):
                q_parts.append(jnp.concatenate(
                    [q_loc[sub_rows[m], :] * jnp.exp(jnp.minimum(bases[m] - bases[j], 0.0)) for m in range(n_sub)],
                    axis=0).astype(BF16))
                k_parts.append(jnp.concatenate(
                    [k_loc[sub_rows[m], :] if m == j else zero_rows for m in range(n_sub)], axis=0).astype(BF16))
            for hd in range(heads):
                ks = slice(hd * dk, (hd + 1) * dk)
                vs = slice(hd * dv, (hd + 1) * dv)
                vh = v[rows, vs]
                q_cat = jnp.concatenate([p[:, ks] for p in q_parts], axis=1)
                k_cat = jnp.concatenate([p[:, ks] for p in k_parts], axis=1)
                att = lax.dot_general(q_cat, k_cat, (((1,), (1,)), ((), ())), preferred_element_type=F32)
                att = jnp.where(causal, att, 0.0).astype(BF16)
                s_h = state[s, hd]
                o = jnp.dot(att, vh, preferred_element_type=F32)
                o = o + jnp.dot(q_in[:, ks], s_h.astype(BF16), preferred_element_type=F32)
                decay_col = jnp.sum(jnp.where(eye, jnp.broadcast_to(decay_row[:, ks], (dk, dk)), 0.0),
                                    axis=1, keepdims=True)
                state[s, hd] = decay_col * s_h + lax.dot_general(k_out[:, ks], vh, (((0,), (0,)), ((), ())),
                                                                 preferred_element_type=F32)
                o = o * lax.rsqrt(jnp.mean(o * o, axis=-1, keepdims=True) + EPS) * gn_ref[...]
                o_ref[rows, vs] = (o * _silu(g[rows, vs])).astype(o_ref.dtype)

    @pl.when(i == pl.num_programs(1) - 1)
    def _():
        for s in range(n_seq):
            tail_ref[s] = win[s, seq_rows:seq_rows + HIST_PAD, :]
        sout_ref[...] = state[...]


def _front(x2d, hist_pad, s0, g1, w_in, w_lr2, blr2, w_dw, b_dw, ln_g, ln_b, gn, *, n_seq, seq_rows,
           zero_shape=None):
    bsz, heads, dk, dv = s0.shape
    d = x2d.shape[1]
    t = x2d.shape[0] // bsz
    c_conv = w_dw.shape[1]
    width = w_dw.shape[0]
    qk, vv = heads * dk, heads * dv
    assert (n_seq == 1 and t % seq_rows == 0) or (seq_rows == t and bsz % n_seq == 0)
    nt = t // seq_rows
    rows = n_seq * seq_rows
    chunk = GLA_CHUNK if seq_rows % GLA_CHUNK == 0 else seq_rows
    assert chunk % GLA_SUB == 0 and seq_rows % min(seq_rows, CONV_ROWS) == 0
    const = lambda b, i: (0, 0)
    tile = lambda b, i: (b * nt + i, 0)
    per_seq3 = lambda b, i: (b, 0, 0)
    per_seq4 = lambda b, i: (b, 0, 0, 0)
    extra_out_specs, extra_out_shape, extra_scratch, zero_chunks = [], [], [], None
    if zero_shape is not None:
        assert zero_shape[0] % EXPERT_ROWS == 0
        n_chunks = zero_shape[0] // EXPERT_ROWS
        zero_chunks = (n_chunks, -(-n_chunks // ((bsz // n_seq) * nt)))
        extra_out_specs = [pl.BlockSpec(memory_space=pl.ANY)]
        extra_out_shape = [jax.ShapeDtypeStruct(zero_shape, U32)]
        extra_scratch = [pltpu.VMEM((EXPERT_ROWS, zero_shape[1]), U32), pltpu.SemaphoreType.DMA(())]
    return pl.pallas_call(
        functools.partial(_front_kernel, n_seq=n_seq, seq_rows=seq_rows, chunk=chunk, heads=heads, dk=dk, dv=dv,
                          c_conv=c_conv, width=width, zero_chunks=zero_chunks),
        grid=(bsz // n_seq, nt),
        in_specs=[
            pl.BlockSpec((rows, d), tile),
            pl.BlockSpec((n_seq, HIST_PAD, c_conv), per_seq3),
            pl.BlockSpec((n_seq, heads, dk, dv), per_seq4),
            pl.BlockSpec((1, d), const),
            pl.BlockSpec(w_in.shape, const, pipeline_mode=pl.Buffered(1)),
            pl.BlockSpec(w_lr2.shape, const),
            pl.BlockSpec((1, qk), const),
            pl.BlockSpec(w_dw.shape, const),
            pl.BlockSpec((1, c_conv), const), pl.BlockSpec((1, c_conv), const), pl.BlockSpec((1, c_conv), const),
            pl.BlockSpec((1, dv), const),
        ],
        out_specs=[
            pl.BlockSpec((rows, c_conv), tile), pl.BlockSpec((rows, vv), tile),
            pl.BlockSpec((n_seq, HIST_PAD, c_conv), per_seq3),
            pl.BlockSpec((n_seq, heads, dk, dv), per_seq4),
        ] + extra_out_specs,
        out_shape=[
            jax.ShapeDtypeStruct((bsz * t, c_conv), BF16), jax.ShapeDtypeStruct((bsz * t, vv), BF16),
            jax.ShapeDtypeStruct((bsz, HIST_PAD, c_conv), F32),
            jax.ShapeDtypeStruct((bsz, heads, dk, dv), F32),
        ] + extra_out_shape,
        scratch_shapes=[
            pltpu.VMEM((n_seq, seq_rows + HIST_PAD, c_conv), F32),
            pltpu.VMEM((7, seq_rows + HIST_PAD - 8, LANES), F32),
            pltpu.VMEM((rows, c_conv), F32),
            pltpu.VMEM((n_seq, heads, dk, dv), F32),
        ] + extra_scratch,
        compiler_params=_params(("arbitrary", "arbitrary")),
        name="front",
    )(x2d, hist_pad, s0, g1, w_in, w_lr2, blr2, w_dw, b_dw, ln_g, ln_b, gn)


def _outproj_kernel(xp_ref, xs_ref, cp_ref, cs_ref, op_ref, os_ref, w_ref, g2_ref, wr_ref, br_ref, zeroed_ref,
                    x1_ref, dest_ref, gates_ref, pages_ref, counts_ref, sorted_ref,
                    stage, dest_vmem, dest_smem, cnt_s, page_s, npage_s, table_s, scatter_sems, dest_sems,
                    *, n_first, c_conv, n_groups, per_group, trash_row):
    i = pl.program_id(0)
    n_steps = pl.num_programs(0)
    tm = x1_ref.shape[0]
    n_exp = n_groups * per_group
    slot = i % 2
    prev = 1 - slot

    def dest_copy(s):
        return pltpu.make_async_copy(dest_vmem.at[s], dest_smem.at[s], dest_sems.at[s])

    def scatter_wait(s):
        for _ in range(2):
            pltpu.make_async_copy(stage.at[s], sorted_ref.at[pl.ds(0, tm)], scatter_sems.at[s]).wait()

    @pl.when(i == 0)
    def _():
        cnt_s[...] = jnp.zeros_like(cnt_s)
        page_s[...] = jnp.zeros_like(page_s)
        npage_s[...] = jnp.zeros_like(npage_s)
        table_s[...] = jnp.zeros_like(table_s)
        stage[1] = jnp.zeros(stage.shape[1:], stage.dtype)

        def fill(t, carry):
            dest_smem[1, 0, t] = trash_row + t
            dest_smem[1, 1, t] = trash_row + tm + t
            return carry

        lax.fori_loop(0, tm, fill, 0)

    @pl.when(i >= 1)
    def _():
        dest_copy(prev).wait()
        scatter_wait(slot)

    for t in range(tm):
        _row_copy(stage.at[prev], t, sorted_ref, dest_smem[prev, 0, t], scatter_sems.at[prev]).start()
        _row_copy(stage.at[prev], t, sorted_ref, dest_smem[prev, 1, t], scatter_sems.at[prev]).start()

    x = _pick(i, n_first, xp_ref, xs_ref)
    cc = _pick(i, n_first, cp_ref, cs_ref)
    oo = _pick(i, n_first, op_ref, os_ref)
    del zeroed_ref
    mix = jnp.dot(jnp.concatenate([cc, oo], axis=1), w_ref[...], preferred_element_type=F32)
    x1 = x + mix
    x1_ref[...] = x1
    h2 = _rms(x1, g2_ref[...])
    stage[slot] = _pack_pairs(h2)
    h_hi = h2.astype(BF16)
    h_lo = (h2 - h_hi.astype(F32)).astype(BF16)
    parts = lax.dot_general(wr_ref[...], jnp.concatenate([h_hi, h_lo], axis=0), (((1,), (1,)), ((), ())),
                            preferred_element_type=F32)
    n_r = br_ref.shape[0]
    logits = (parts[0:n_r, 0:tm] + parts[0:n_r, tm:] + parts[n_r:, 0:tm] + parts[n_r:, tm:]) + br_ref[...]
    lc = logits[0:n_groups, :]
    mc = jnp.max(lc, axis=0, keepdims=True)
    p_group = 1.0 / jnp.sum(jnp.exp(lc - mc), axis=0, keepdims=True)
    rows_c = lax.broadcasted_iota(I32, (n_groups, tm), 0)
    g_idx = jnp.min(jnp.where(lc == mc, rows_c, n_groups), axis=0, keepdims=True)
    lf = logits[n_groups:n_groups + n_exp, :]
    rows_f = lax.broadcasted_iota(I32, (n_exp, tm), 0)
    in_group = (rows_f >= g_idx * per_group) & (rows_f < (g_idx + 1) * per_group)
    neg = jnp.float32(-jnp.inf)
    l1 = jnp.where(in_group, lf, neg)
    m1 = jnp.max(l1, axis=0, keepdims=True)
    e1 = jnp.min(jnp.where(l1 == m1, rows_f, n_exp), axis=0, keepdims=True)
    l2 = jnp.where(rows_f == e1, neg, l1)
    m2 = jnp.max(l2, axis=0, keepdims=True)
    e2 = jnp.min(jnp.where(l2 == m2, rows_f, n_exp), axis=0, keepdims=True)
    r2 = jnp.exp(m2 - m1)
    w1 = 1.0 / (1.0 + r2)
    row8 = lax.broadcasted_iota(I32, (8, tm), 0)
    gates_ref[...] = jnp.where(row8 == 0, p_group * w1, jnp.where(row8 == 1, p_group * (r2 * w1), 0.0))

    oh0 = (rows_f == e1).astype(F32)
    oh1 = (rows_f == e2).astype(F32)
    both = oh0 + oh1
    tr = lax.broadcasted_iota(I32, (tm, tm), 0)
    tc = lax.broadcasted_iota(I32, (tm, tm), 1)
    earlier = jnp.dot(both.astype(BF16), (tr < tc).astype(BF16), preferred_element_type=F32)
    cnt = cnt_s[...]
    rank_base = earlier + cnt
    tile_cnt = jnp.sum(both, axis=1, keepdims=True)
    page_rows = float(EXPERT_ROWS)
    k0 = jnp.floor(cnt * (1.0 / page_rows))
    new_cnt = cnt + tile_cnt
    limit = (k0 + 1.0) * page_rows
    need_a = ((cnt == k0 * page_rows) & (tile_cnt > 0.0)).astype(F32)
    need_b = (new_cnt > limit).astype(F32)
    need = need_a + need_b
    er = lax.broadcasted_iota(I32, (n_exp, n_exp), 0)
    ec = lax.broadcasted_iota(I32, (n_exp, n_exp), 1)
    before = jnp.dot((ec < er).astype(BF16), jnp.broadcast_to(need, (n_exp, LANES)).astype(BF16),
                     preferred_element_type=F32)[:, 0:1]
    base = npage_s[...] + before
    page_a = jnp.where(need_a > 0.0, base, page_s[...])
    page_b = base + need_a
    npage_s[...] = npage_s[...] + jnp.sum(need, axis=0, keepdims=True)
    lane = lax.broadcasted_iota(I32, table_s.shape, 1).astype(F32)
    table = jnp.where((lane == k0) & (need_a > 0.0), page_a, table_s[...])
    table_s[...] = jnp.where((lane == k0 + 1.0) & (need_b > 0.0), page_b, table)
    cnt_s[...] = new_cnt
    page_s[...] = jnp.where(jnp.floor(new_cnt * (1.0 / page_rows)) == k0, page_a, page_b)

    def dest_rows(oh):
        rank = jnp.sum(oh * rank_base, axis=0, keepdims=True)
        lim = jnp.sum(oh * limit, axis=0, keepdims=True)
        pa = jnp.sum(oh * page_a, axis=0, keepdims=True)
        pb = jnp.sum(oh * page_b, axis=0, keepdims=True)
        within = rank - jnp.floor(rank * (1.0 / page_rows)) * page_rows
        return jnp.where(rank < lim, pa, pb) * page_rows + within

    dest = jnp.where(row8 == 0, dest_rows(oh0), jnp.where(row8 == 1, dest_rows(oh1), 0.0)).astype(I32)
    dest_ref[...] = dest
    dest_vmem[slot] = dest
    dest_copy(slot).start()

    @pl.when(i == n_steps - 1)
    def _():
        pages_ref[...] = table_s[...].astype(I32)
        counts_ref[...] = jnp.broadcast_to(cnt_s[...], counts_ref.shape).astype(I32)
        dest_copy(slot).wait()
        scatter_wait(prev)

        def last(j, carry):
            for r in range(DMA_UNROLL):
                t = j * DMA_UNROLL + r
                _row_copy(stage.at[slot], t, sorted_ref, dest_smem[slot, 0, t], scatter_sems.at[slot]).start()
                _row_copy(stage.at[slot], t, sorted_ref, dest_smem[slot, 1, t], scatter_sems.at[slot]).start()
            return carry

        lax.fori_loop(0, tm // DMA_UNROLL, last, 0)
        scatter_wait(slot)


def _outproj(xp, xs, cp, cs, op, os_, w_out, g2, wr, br, zeroed, *, n_groups, per_group, n_pages):
    n_p, d = xp.shape
    n_all = n_p + xs.shape[0]
    n_first = n_p // ROW_TILE
    n_exp = n_groups * per_group
    c_conv = cp.shape[1]
    vv = op.shape[1]
    tile = (d // 2,)
    first, second = _split_maps(n_first)
    const = lambda i: (0, 0)
    row = lambda i: (i, 0)
    col = lambda i: (0, i)
    rows_sorted = zeroed.shape[0]
    assert rows_sorted >= n_pages * EXPERT_ROWS + 2 * ROW_TILE
    assert ROW_TILE <= EXPERT_ROWS, "a tile may open at most two pages per expert"
    return pl.pallas_call(
        functools.partial(_outproj_kernel, n_first=n_first, c_conv=c_conv, n_groups=n_groups, per_group=per_group,
                          trash_row=n_pages * EXPERT_ROWS),
        grid=(n_all // ROW_TILE,),
        in_specs=[
            pl.BlockSpec((ROW_TILE, d), first), pl.BlockSpec((ROW_TILE, d), second),
            pl.BlockSpec((ROW_TILE, c_conv), first), pl.BlockSpec((ROW_TILE, c_conv), second),
            pl.BlockSpec((ROW_TILE, vv), first), pl.BlockSpec((ROW_TILE, vv), second),
            pl.BlockSpec(w_out.shape, const, pipeline_mode=pl.Buffered(1)),
            pl.BlockSpec((1, d), const),
            pl.BlockSpec(wr.shape, const),
            pl.BlockSpec(br.shape, const),
            pl.BlockSpec(memory_space=pl.ANY),
        ],
        out_specs=[
            pl.BlockSpec((ROW_TILE, d), row),
            pl.BlockSpec((8, ROW_TILE), col), pl.BlockSpec((8, ROW_TILE), col),
            pl.BlockSpec((n_exp, LANES), const), pl.BlockSpec((n_exp, LANES), const),
            pl.BlockSpec(memory_space=pl.ANY),
        ],
        out_shape=[
            jax.ShapeDtypeStruct((n_all, d), F32),
            jax.ShapeDtypeStruct((8, n_all), I32), jax.ShapeDtypeStruct((8, n_all), F32),
            jax.ShapeDtypeStruct((n_exp, LANES), I32), jax.ShapeDtypeStruct((n_exp, LANES), I32),
            jax.ShapeDtypeStruct((rows_sorted,) + tile, U32),
        ],
        scratch_shapes=[
            pltpu.VMEM((2, ROW_TILE) + tile, U32),
            pltpu.VMEM((2, 8, ROW_TILE), I32), pltpu.SMEM((2, 8, ROW_TILE), I32),
            pltpu.VMEM((n_exp, 1), F32), pltpu.VMEM((n_exp, 1), F32), pltpu.VMEM((1, 1), F32),
            pltpu.VMEM((n_exp, LANES), F32),
            pltpu.SemaphoreType.DMA((2,)), pltpu.SemaphoreType.DMA((2,)),
        ],
        compiler_params=_params(("arbitrary",)),
        input_output_aliases={10: 5},
        name="outproj",
    )(xp, xs, cp, cs, op, os_, w_out, g2, wr, br, zeroed)


def _experts_kernel(cnt_ref, pages_ref, xs_ref, wg_ref, wu_ref, wd_ref, ysp_ref,
                    xbuf, ybuf, wg_f32, wu_f32, wd_f32, wg_bf, wu_bf, wd_bf, first_blk, page_seq,
                    gsems, ysems, wsems, *, n_exp, table_lanes):
    e = pl.program_id(0)
    tb = xbuf.shape[1]
    n_pages = page_seq.shape[0]

    def n_pages_of(ex):
        return (cnt_ref[ex] + (tb - 1)) // tb

    def page_rows(blk):
        return pl.ds(pl.multiple_of(page_seq[blk] * tb, tb), tb)

    def fetch(blk, slot):
        return pltpu.make_async_copy(xs_ref.at[page_rows(blk)], xbuf.at[slot], gsems.at[slot])

    def writeback(blk, slot):
        return pltpu.make_async_copy(ybuf.at[slot], ysp_ref.at[page_rows(blk)], ysems.at[slot])

    def weight_copies(ex, slot):
        return (pltpu.make_async_copy(wg_ref.at[ex], wg_f32.at[slot], wsems.at[slot]),
                pltpu.make_async_copy(wu_ref.at[ex], wu_f32.at[slot], wsems.at[slot]),
                pltpu.make_async_copy(wd_ref.at[ex], wd_f32.at[slot], wsems.at[slot]))

    @pl.when(e == 0)
    def _():
        for ahead in range(min(WEIGHT_SLOTS - 1, n_exp)):
            for cp in weight_copies(ahead, ahead):
                cp.start(priority=1)

        def per_expert(ex, blk):
            first_blk[ex] = blk

            def per_page(j, carry):
                page_seq[blk + j] = pages_ref[ex * table_lanes + j]
                return carry

            lax.fori_loop(0, n_pages_of(ex), per_page, 0)
            return blk + n_pages_of(ex)

        first_blk[n_exp] = lax.fori_loop(0, n_exp, per_expert, 0)
        fetch(0, 0).start()

    @pl.when(e + WEIGHT_SLOTS - 1 < n_exp)
    def _():
        for cp in weight_copies(e + WEIGHT_SLOTS - 1, (e + WEIGHT_SLOTS - 1) % WEIGHT_SLOTS):
            cp.start(priority=1)

    b_lo = first_blk[e]
    b_hi = first_blk[e + 1]
    n_total = first_blk[n_exp]
    wslot = e % WEIGHT_SLOTS
    for cp in weight_copies(e, wslot):
        cp.wait()
    wg_bf[...] = wg_f32[wslot].astype(BF16)
    wu_bf[...] = wu_f32[wslot].astype(BF16)
    wd_bf[...] = wd_f32[wslot].astype(BF16)

    def block(b, carry):
        slot = b % 2

        @pl.when(b >= 2)
        def _():
            writeback(b, slot).wait()

        fetch(b, slot).wait()
        fetch(jnp.minimum(b + 1, n_total - 1), 1 - slot).start()

        def swiglu(rows):
            hi, lo = _unpack_pairs(xbuf[slot, rows, :])
            x = jnp.concatenate([hi.astype(BF16), lo.astype(BF16)], axis=1)
            hg = jnp.dot(x, wg_bf[...], preferred_element_type=F32)
            hu = jnp.dot(x, wu_bf[...], preferred_element_type=F32)
            hb = (_silu(hg) * hu).astype(BF16)
            ybuf[slot, rows, :] = _pack_pairs(jnp.dot(hb, wd_bf[...], preferred_element_type=F32))

        valid = cnt_ref[e] - (b - b_lo) * tb
        half = tb // 2

        @pl.when(valid > half)
        def _():
            swiglu(slice(0, tb))

        @pl.when(valid <= half)
        def _():
            swiglu(slice(0, half))
            ybuf[slot, half:tb, :] = jnp.zeros((tb - half,) + ybuf.shape[2:], ybuf.dtype)

        writeback(b, slot).start()
        return carry

    lax.fori_loop(b_lo, b_hi, block, 0)

    @pl.when(e == n_exp - 1)
    def _():
        fetch(0, n_total % 2).wait()

        @pl.when(n_total >= 2)
        def _():
            writeback(0, n_total % 2).wait()

        writeback(0, (n_total + 1) % 2).wait()
        ybuf[0] = jnp.zeros(ybuf.shape[1:], ybuf.dtype)

        def spare(blk):
            return pltpu.make_async_copy(ybuf.at[0], ysp_ref.at[pl.ds(pl.multiple_of(blk * tb, tb), tb)], ysems.at[0])

        def zero(blk, carry):
            spare(blk).start()
            return carry

        lax.fori_loop(n_total, n_pages, zero, 0)

        def zero_wait(blk, carry):
            spare(0).wait()
            return carry

        lax.fori_loop(n_total, n_pages, zero_wait, 0)


def _experts(counts, pages_flat, xs_sorted, w_gate, w_up, w_down, *, n_pages, table_lanes):
    tile = xs_sorted.shape[1:]
    n_exp, d, ff = w_gate.shape
    anyspec = pl.BlockSpec(memory_space=pl.ANY)
    grid_spec = pltpu.PrefetchScalarGridSpec(
        num_scalar_prefetch=2,
        grid=(n_exp,),
        in_specs=[anyspec, anyspec, anyspec, anyspec],
        out_specs=anyspec,
        scratch_shapes=[
            pltpu.VMEM((2, EXPERT_ROWS) + tile, U32), pltpu.VMEM((2, EXPERT_ROWS) + tile, U32),
            pltpu.VMEM((WEIGHT_SLOTS, d, ff), F32), pltpu.VMEM((WEIGHT_SLOTS, d, ff), F32),
            pltpu.VMEM((WEIGHT_SLOTS, ff, d), F32),
            pltpu.VMEM((d, ff), BF16), pltpu.VMEM((d, ff), BF16), pltpu.VMEM((ff, d), BF16),
            pltpu.SMEM((n_exp + 1,), I32), pltpu.SMEM((n_pages,), I32),
            pltpu.SemaphoreType.DMA((2,)), pltpu.SemaphoreType.DMA((2,)), pltpu.SemaphoreType.DMA((WEIGHT_SLOTS,)),
        ],
    )
    return pl.pallas_call(
        functools.partial(_experts_kernel, n_exp=n_exp, table_lanes=table_lanes),
        grid_spec=grid_spec,
        out_shape=jax.ShapeDtypeStruct((n_pages * EXPERT_ROWS,) + tile, U32),
        compiler_params=_params(("arbitrary",)),
        name="experts",
    )(counts, pages_flat, xs_sorted, w_gate, w_up, w_down)


def _combine_kernel(dest_ref, dest_next_ref, gates_ref, x1_ref, ysp_ref, gf_ref, yp_ref, ysmp_ref,
                    buf0, buf1, sems, *, n_first):
    i = pl.program_id(0)
    n = pl.num_programs(0)
    tm = x1_ref.shape[0]
    slot = i % 2

    def gather(d_ref, s):
        def body(j, carry):
            for r in range(DMA_UNROLL):
                t = j * DMA_UNROLL + r
                _row_copy(ysp_ref, d_ref[0, t], buf0.at[s], t, sems.at[s]).start(priority=0)
                _row_copy(ysp_ref, d_ref[1, t], buf1.at[s], t, sems.at[s]).start(priority=1)
            return carry

        lax.fori_loop(0, tm // DMA_UNROLL, body, 0)

    @pl.when(i == 0)
    def _():
        gather(dest_ref, 0)

    @pl.when(i + 1 < n)
    def _():
        gather(dest_next_ref, 1 - slot)

    pltpu.make_async_copy(ysp_ref.at[pl.ds(0, tm)], buf0.at[slot], sems.at[slot]).wait()
    pltpu.make_async_copy(ysp_ref.at[pl.ds(0, tm)], buf1.at[slot], sems.at[slot]).wait()
    hi0, lo0 = _unpack_pairs(buf0[slot])
    hi1, lo1 = _unpack_pairs(buf1[slot])
    g0 = gates_ref[:, 0:1]
    g1 = gates_ref[:, 1:2]
    moe = jnp.concatenate([g0 * hi0 + g1 * hi1, g0 * lo0 + g1 * lo1], axis=1)
    y = _rms(x1_ref[...] + moe, gf_ref[...])

    @pl.when(i < n_first)
    def _():
        yp_ref[...] = y

    @pl.when(i >= n_first)
    def _():
        ysmp_ref[...] = y


def _combine(dest, gates_t, x1, ysp, gf, *, n_p):
    n_all, d = x1.shape
    tile = ysp.shape[1:]
    n_first = n_p // ROW_TILE
    n_tiles = n_all // ROW_TILE
    first, second = _split_maps(n_first)
    return pl.pallas_call(
        functools.partial(_combine_kernel, n_first=n_first),
        grid=(n_tiles,),
        in_specs=[
            pl.BlockSpec((8, ROW_TILE), lambda i: (0, i), memory_space=pltpu.SMEM),
            pl.BlockSpec((8, ROW_TILE), lambda i: (0, jnp.minimum(i + 1, n_tiles - 1)), memory_space=pltpu.SMEM),
            pl.BlockSpec((ROW_TILE, 8), lambda i: (i, 0)),
            pl.BlockSpec((ROW_TILE, d), lambda i: (i, 0)),
            pl.BlockSpec(memory_space=pl.ANY),
            pl.BlockSpec((1, d), lambda i: (0, 0)),
        ],
        out_specs=[pl.BlockSpec((ROW_TILE, d), first), pl.BlockSpec((ROW_TILE, d), second)],
        out_shape=[jax.ShapeDtypeStruct((n_p, d), F32), jax.ShapeDtypeStruct((n_all - n_p, d), F32)],
        scratch_shapes=[pltpu.VMEM((2, ROW_TILE) + tile, U32), pltpu.VMEM((2, ROW_TILE) + tile, U32),
                        pltpu.SemaphoreType.DMA((2,))],
        compiler_params=_params(("arbitrary",)),
        name="combine",
    )(dest, dest, gates_t, x1, ysp, gf)


def _chunk_for(t):
    return 64 if t % 64 == 0 else t


def kernel(x_prompt, x_sample, cache_conv, state_gla, norm1_g, w_in, w_lr2, b_lr2, w_dw, b_dw, conv_ln_g, conv_ln_b, gla_norm_g, w_out, norm2_g, w_router_coarse, b_router_coarse, w_router_fine, b_router_fine, w_exp_gate, w_exp_up, w_exp_down, norm_f_g):
    assert norm1_g.shape[0] == 1, "single trunk layer"
    bp, tp, d = x_prompt.shape
    bs, ts, _ = x_sample.shape
    heads, dk, dv = state_gla.shape[2:]
    c_conv = w_dw.shape[2]
    width = w_dw.shape[1]
    rank = w_lr2.shape[1]
    qk, vv = heads * dk, heads * dv
    n_groups, _, per_group = w_router_fine.shape[1:]
    n_exp = n_groups * per_group
    n_p, n_s = bp * tp, bs * ts
    n_s_pad = -(-n_s // ROW_TILE) * ROW_TILE
    pad_rows = lambda a: jnp.pad(a, ((0, n_s_pad - n_s), (0, 0)))
    n_all = n_p + n_s_pad
    assert n_p % ROW_TILE == 0 and width - 1 <= HIST_PAD

    xp = x_prompt.reshape(n_p, d)
    xs = pad_rows(x_sample.reshape(n_s, d))
    row = lambda a: a.reshape(1, -1)

    mixer_args = (row(norm1_g[0]), w_in[0].astype(BF16), w_lr2[0], row(b_lr2[0]),
                  w_dw[0], row(b_dw[0]), row(conv_ln_g[0]), row(conv_ln_b[0]), row(gla_norm_g[0]))
    hist_p = jnp.zeros((bp, HIST_PAD, c_conv), F32)
    hist_s = jnp.pad(cache_conv[0], ((0, 0), (HIST_PAD - (width - 1), 0), (0, 0)))
    s0_p = jnp.zeros((bp, heads, dk, dv), F32)
    n_pages = (2 * n_all) // EXPERT_ROWS + n_exp
    sorted_rows = n_pages * EXPERT_ROWS + -(-2 * ROW_TILE // EXPERT_ROWS) * EXPERT_ROWS
    c_p, o_p, tail_p, gla_p, zeroed = _front(xp, hist_p, s0_p, *mixer_args, n_seq=1, seq_rows=ROW_TILE,
                                             zero_shape=(sorted_rows, d // 2))
    c_s, o_s, tail_s, gla_s = _front(x_sample.reshape(n_s, d), hist_s, state_gla[0], *mixer_args,
                                     n_seq=bs, seq_rows=ts)
    c_s, o_s = pad_rows(c_s), pad_rows(o_s)

    wr = jnp.concatenate([w_router_coarse[0].T,
                          jnp.transpose(w_router_fine[0], (0, 2, 1)).reshape(n_exp, d)], axis=0)
    br = jnp.concatenate([b_router_coarse[0], b_router_fine[0].reshape(n_exp)])
    r_rows = -(-(n_groups + n_exp) // 8) * 8
    wr = jnp.pad(wr, ((0, r_rows - wr.shape[0]), (0, 0)))
    br = jnp.pad(br, (0, r_rows - br.shape[0])).reshape(r_rows, 1)
    wr_hi = wr.astype(BF16)
    wr = jnp.concatenate([wr_hi, (wr - wr_hi.astype(F32)).astype(BF16)], axis=0)
    n_pages = (2 * n_all) // EXPERT_ROWS + n_exp
    assert n_all // EXPERT_ROWS + 2 <= LANES, "page table row must hold one expert's pages"
    x1, dest, gates, pages, counts, xs_sorted = _outproj(
        xp, xs, c_p, c_s, o_p, o_s, w_out[0].astype(BF16), row(norm2_g[0]), wr, br, zeroed,
        n_groups=n_groups, per_group=per_group, n_pages=n_pages)
    ysp = _experts(counts[:, 0], pages.reshape(-1), xs_sorted, w_exp_gate[0], w_exp_up[0], w_exp_down[0],
                   n_pages=n_pages, table_lanes=LANES)
    y_p, y_s = _combine(dest, gates.T, x1, ysp, row(norm_f_g), n_p=n_p)

    lead = HIST_PAD - (width - 1)
    return (y_p.reshape(bp, tp, d), y_s[:n_s].reshape(bs, ts, d), tail_p[:, lead:][None], gla_p[None],
            tail_s[:, lead:][None], gla_s[None])
```

```python
import functools

import jax
import jax.numpy as jnp
from jax import lax
from jax.experimental import pallas as pl
from jax.experimental.pallas import tpu as pltpu

F32 = jnp.float32
BF16 = jnp.bfloat16
I32 = jnp.int32
U32 = jnp.uint32
EPS = 1e-6
GATE_TEMP = 16.0
HIGHEST = lax.Precision.HIGHEST

LANES = 128
ROW_TILE = 256
EXPERT_ROWS = 256
HIST_PAD = 32
DMA_UNROLL = 8
WEIGHT_SLOTS = 3
VMEM_LIMIT = 56 * 1024 * 1024


def _params(semantics, vmem=VMEM_LIMIT):
    return pltpu.CompilerParams(dimension_semantics=semantics, vmem_limit_bytes=vmem)


def _rms(x, g):
    return x * lax.rsqrt(jnp.mean(x * x, axis=-1, keepdims=True) + EPS) * g


def _silu(x):
    return x * jax.nn.sigmoid(x)


def _log_sigmoid(z):
    return jnp.minimum(z, 0.0) - jnp.log(1.0 + jnp.exp(-jnp.abs(z)))


def _pick(i, n_first, first_ref, second_ref):
    return jnp.where(i < n_first, first_ref[...], second_ref[...])


def _split_maps(n_first):
    first = lambda i: (jnp.minimum(i, n_first - 1), 0)
    second = lambda i: (jnp.maximum(i - n_first, 0), 0)
    return first, second


def _pack_pairs(x):
    half = x.shape[1] // 2
    hi = lax.bitcast_convert_type(x[:, :half].astype(BF16).astype(F32), U32)
    lo = lax.bitcast_convert_type(x[:, half:].astype(BF16).astype(F32), U32)
    return hi | (lo >> 16)


def _unpack_pairs(p):
    hi = lax.bitcast_convert_type(p & jnp.uint32(0xFFFF0000), F32)
    lo = lax.bitcast_convert_type(p << 16, F32)
    return hi, lo


def _row_copy(src, s, dst, d, sem):
    return pltpu.make_async_copy(src.at[pl.ds(s, 1)], dst.at[pl.ds(d, 1)], sem)


def _inproj_kernel(xp_ref, xs_ref, g1_ref, w_ref, wlr2_ref, blr2_ref,
                   u_ref, q_ref, k_ref, v_ref, g_ref, la_ref, *, n_first, c_conv, qk, vv, dk):
    i = pl.program_id(0)
    x = _pick(i, n_first, xp_ref, xs_ref)
    h = _rms(x, g1_ref[...]).astype(BF16)

    def mm(lo, width):
        return jnp.dot(h, w_ref[:, lo:lo + width], preferred_element_type=F32)

    a = mm(0, c_conv)
    a_gate = mm(c_conv, c_conv)
    u_ref[...] = a * jax.nn.sigmoid(a_gate)
    off = 2 * c_conv
    q_ref[...] = mm(off, qk) * (dk ** -0.5)
    k_ref[...] = mm(off + qk, qk)
    v_ref[...] = mm(off + 2 * qk, vv)
    g_ref[...] = mm(off + 2 * qk + vv, vv)
    lr = mm(off + 2 * qk + 2 * vv, LANES)
    z = jnp.dot(lr, wlr2_ref[...], precision=HIGHEST, preferred_element_type=F32) + blr2_ref[...]
    la_ref[...] = _log_sigmoid(z) * (1.0 / GATE_TEMP)


def _inproj(xp, xs, g1, w_pad, wlr2_pad, blr2, *, c_conv, qk, vv, dk):
    n_p, d = xp.shape
    n_s = xs.shape[0]
    n_all = n_p + n_s
    n_first = n_p // ROW_TILE
    grid = (n_all // ROW_TILE,)
    first, second = _split_maps(n_first)
    const = lambda i: (0, 0)
    row = lambda i: (i, 0)
    widths = (c_conv, qk, qk, vv, vv, qk)
    return pl.pallas_call(
        functools.partial(_inproj_kernel, n_first=n_first, c_conv=c_conv, qk=qk, vv=vv, dk=dk),
        grid=grid,
        in_specs=[
            pl.BlockSpec((ROW_TILE, d), first),
            pl.BlockSpec((ROW_TILE, d), second),
            pl.BlockSpec((1, d), const),
            pl.BlockSpec(w_pad.shape, const, pipeline_mode=pl.Buffered(1)),
            pl.BlockSpec(wlr2_pad.shape, const),
            pl.BlockSpec((1, qk), const),
        ],
        out_specs=[pl.BlockSpec((ROW_TILE, w), row) for w in widths],
        out_shape=[jax.ShapeDtypeStruct((n_all, w), F32) for w in widths],
        compiler_params=_params(("arbitrary",)),
        name="inproj",
    )(xp, xs, g1, w_pad, wlr2_pad, blr2)


def _conv_kernel(u_ref, hist_ref, w_ref, b_ref, lg_ref, lb_ref, c_ref, win, cbuf, *, tt, width):
    i = pl.program_id(1)

    @pl.when(i == 0)
    def _():
        win[0:HIST_PAD, :] = hist_ref[0]

    @pl.when(i > 0)
    def _():
        win[0:HIST_PAD, :] = win[tt:tt + HIST_PAD, :]

    win[HIST_PAD:HIST_PAD + tt, :] = u_ref[...]
    lead = HIST_PAD - (width - 1)
    n_ch = u_ref.shape[1]
    for cb in range(n_ch // LANES):
        cs = slice(cb * LANES, (cb + 1) * LANES)
        acc = jnp.broadcast_to(b_ref[:, cs], (tt, LANES))
        for j in range(width):
            acc = acc + w_ref[j:j + 1, cs] * win[lead + j:lead + j + tt, cs]
        cbuf[:, cs] = acc
    c = cbuf[...]
    mu = jnp.mean(c, axis=-1, keepdims=True)
    xc = c - mu
    y = xc * lax.rsqrt(jnp.mean(xc * xc, axis=-1, keepdims=True) + EPS) * lg_ref[...] + lb_ref[...]
    c_ref[...] = _silu(y).astype(c_ref.dtype)


def _conv(u_all, hist_pad, w_dw, b_dw, ln_g, ln_b, *, row0, bsz, t, tt):
    n_ch = u_all.shape[1]
    width = w_dw.shape[0]
    nt = t // tt
    blk0 = row0 // tt
    const = lambda b, i: (0, 0)
    return pl.pallas_call(
        functools.partial(_conv_kernel, tt=tt, width=width),
        grid=(bsz, nt),
        in_specs=[
            pl.BlockSpec((tt, n_ch), lambda b, i: (blk0 + b * nt + i, 0)),
            pl.BlockSpec((1, HIST_PAD, n_ch), lambda b, i: (b, 0, 0)),
            pl.BlockSpec(w_dw.shape, const),
            pl.BlockSpec((1, n_ch), const),
            pl.BlockSpec((1, n_ch), const),
            pl.BlockSpec((1, n_ch), const),
        ],
        out_specs=pl.BlockSpec((tt, n_ch), lambda b, i: (b * nt + i, 0)),
        out_shape=jax.ShapeDtypeStruct((bsz * t, n_ch), BF16),
        scratch_shapes=[pltpu.VMEM((tt + HIST_PAD, n_ch), F32), pltpu.VMEM((tt, n_ch), F32)],
        compiler_params=_params(("arbitrary", "arbitrary")),
        name="conv",
    )(u_all, hist_pad, w_dw, b_dw, ln_g, ln_b)


def _gla_kernel(q_ref, k_ref, v_ref, g_ref, la_ref, s0_ref, gn_ref, o_ref, sout_ref, state,
                *, chunk, heads, dk, dv):
    i = pl.program_id(1)

    @pl.when(i == 0)
    def _():
        state[...] = s0_ref[0]

    la = la_ref[...]
    r = lax.broadcasted_iota(I32, (chunk, chunk), 0)
    c = lax.broadcasted_iota(I32, (chunk, chunk), 1)
    causal = c <= r
    b = jnp.dot(causal.astype(F32), la, precision=HIGHEST, preferred_element_type=F32)
    b_end = b[chunk - 1:chunk, :]
    q_in = (q_ref[...] * jnp.exp(b)).astype(BF16)
    k_in = (k_ref[...] * jnp.exp(-b)).astype(BF16)
    k_out = (k_ref[...] * jnp.exp(b_end - b)).astype(BF16)
    decay_row = jnp.exp(b_end)
    eye = lax.broadcasted_iota(I32, (dk, dk), 0) == lax.broadcasted_iota(I32, (dk, dk), 1)
    for h in range(heads):
        ks = slice(h * dk, (h + 1) * dk)
        vs = slice(h * dv, (h + 1) * dv)
        vh = v_ref[:, vs].astype(BF16)
        att = lax.dot_general(q_in[:, ks], k_in[:, ks], (((1,), (1,)), ((), ())), preferred_element_type=F32)
        att = jnp.where(causal, att, 0.0).astype(BF16)
        s_h = state[h]
        o = jnp.dot(att, vh, preferred_element_type=F32)
        o = o + jnp.dot(q_in[:, ks], s_h.astype(BF16), preferred_element_type=F32)
        decay_col = jnp.sum(jnp.where(eye, jnp.broadcast_to(decay_row[:, ks], (dk, dk)), 0.0), axis=1, keepdims=True)
        state[h] = decay_col * s_h + lax.dot_general(k_out[:, ks], vh, (((0,), (0,)), ((), ())),
                                                     preferred_element_type=F32)
        o = o * lax.rsqrt(jnp.mean(o * o, axis=-1, keepdims=True) + EPS) * gn_ref[...]
        o_ref[:, vs] = (o * _silu(g_ref[:, vs])).astype(o_ref.dtype)

    @pl.when(i == pl.num_programs(1) - 1)
    def _():
        sout_ref[0] = state[...]


def _gla(q_all, k_all, v_all, g_all, la_all, s0, gn, *, row0, bsz, t, chunk):
    heads, dk, dv = s0.shape[1:]
    nt = t // chunk
    blk0 = row0 // chunk
    rows = lambda b, i: (blk0 + b * nt + i, 0)
    return pl.pallas_call(
        functools.partial(_gla_kernel, chunk=chunk, heads=heads, dk=dk, dv=dv),
        grid=(bsz, nt),
        in_specs=[
            pl.BlockSpec((chunk, heads * dk), rows),
            pl.BlockSpec((chunk, heads * dk), rows),
            pl.BlockSpec((chunk, heads * dv), rows),
            pl.BlockSpec((chunk, heads * dv), rows),
            pl.BlockSpec((chunk, heads * dk), rows),
            pl.BlockSpec((1, heads, dk, dv), lambda b, i: (b, 0, 0, 0)),
            pl.BlockSpec((1, dv), lambda b, i: (0, 0)),
        ],
        out_specs=[
            pl.BlockSpec((chunk, heads * dv), lambda b, i: (b * nt + i, 0)),
            pl.BlockSpec((1, heads, dk, dv), lambda b, i: (b, 0, 0, 0)),
        ],
        out_shape=[
            jax.ShapeDtypeStruct((bsz * t, heads * dv), BF16),
            jax.ShapeDtypeStruct((bsz, heads, dk, dv), F32),
        ],
        scratch_shapes=[pltpu.VMEM((heads, dk, dv), F32)],
        compiler_params=_params(("arbitrary", "arbitrary")),
        name="gla",
    )(q_all, k_all, v_all, g_all, la_all, s0, gn)


def _mixer_kernel(x_ref, hist_ref, s0_ref, g1_ref, w_ref, wlr2_ref, blr2_ref,
                  wdw_ref, bdw_ref, lg_ref, lb_ref, gn_ref,
                  c_ref, o_ref, tail_ref, sout_ref, pu, pq, pk, pv, pg, pla, win, shifted, cbuf, state,
                  *, n_seq, seq_rows, chunk, heads, dk, dv, c_conv, width):
    i = pl.program_id(1)
    qk, vv = heads * dk, heads * dv

    @pl.when(i == 0)
    def _():
        for buf in (pu, pq, pk, pv, pg, pla):
            buf[...] = jnp.zeros_like(buf)
        win[...] = jnp.zeros_like(win)
        state[...] = jnp.zeros_like(state)

    @pl.when(i == 1)
    def _():
        for s in range(n_seq):
            win[s, 0:HIST_PAD, :] = hist_ref[s]
        state[...] = s0_ref[...]

    @pl.when(i > 1)
    def _():
        for s in range(n_seq):
            win[s, 0:HIST_PAD, :] = win[s, seq_rows:seq_rows + HIST_PAD, :]

    h = _rms(x_ref[...], g1_ref[...]).astype(BF16)

    def mm(lo, n):
        return jnp.dot(h, w_ref[:, lo:lo + n], preferred_element_type=F32)

    off = 2 * c_conv
    new = {}

    def proj_u(j, n):
        lo = j * n
        new["u", j] = mm(lo, n) * jax.nn.sigmoid(mm(c_conv + lo, n))

    def proj(name, lo, n, scale=None, dtype=F32):
        val = mm(lo, n)
        new[name] = (val if scale is None else val * scale).astype(dtype)

    def proj_la():
        lr = mm(off + 2 * qk + 2 * vv, wlr2_ref.shape[0])
        z = jnp.dot(lr, wlr2_ref[...], precision=HIGHEST, preferred_element_type=F32) + blr2_ref[...]
        new["la"] = _log_sigmoid(z) * (1.0 / GATE_TEMP)

    half_c, half_v = c_conv // 2, vv // 2
    stage1 = [
        functools.partial(proj_u, 0, half_c), functools.partial(proj_u, 1, half_c),
        functools.partial(proj, "q", off, qk, dk ** -0.5), functools.partial(proj, "k", off + qk, qk),
        functools.partial(proj, ("v", 0), off + 2 * qk, half_v, None, BF16),
        functools.partial(proj, ("v", 1), off + 2 * qk + half_v, half_v, None, BF16),
        functools.partial(proj, ("g", 0), off + 2 * qk + vv, half_v),
        functools.partial(proj, ("g", 1), off + 2 * qk + vv + half_v, half_v),
        proj_la,
    ]

    lead = HIST_PAD - (width - 1)
    span = shifted.shape[1]

    def conv_fill(s):
        r0 = s * seq_rows
        win[s, HIST_PAD:HIST_PAD + seq_rows, :] = pu[r0:r0 + seq_rows, :]

    def conv_block(s, cb):
        r0 = s * seq_rows
        cs = slice(cb * LANES, (cb + 1) * LANES)
        for r in range(1, 8):
            shifted[r - 1] = win[s, r:r + span, cs]
        for t0 in range(0, seq_rows, chunk):
            acc = jnp.broadcast_to(bdw_ref[:, cs], (chunk, LANES))
            for j in range(width):
                r, a8 = (lead + j) % 8, (lead + j) // 8 * 8
                if r == 0:
                    tap = win[s, t0 + a8:t0 + a8 + chunk, cs]
                else:
                    tap = shifted[r - 1, t0 + a8:t0 + a8 + chunk, :]
                acc = acc + wdw_ref[j:j + 1, cs] * tap
            cbuf[r0 + t0:r0 + t0 + chunk, cs] = acc

    def conv_norm():
        cv = cbuf[...]
        mu = jnp.mean(cv, axis=-1, keepdims=True)
        xc = cv - mu
        cn = xc * lax.rsqrt(jnp.mean(xc * xc, axis=-1, keepdims=True) + EPS) * lg_ref[...] + lb_ref[...]
        c_ref[...] = _silu(cn).astype(c_ref.dtype)

    r = lax.broadcasted_iota(I32, (chunk, chunk), 0)
    c = lax.broadcasted_iota(I32, (chunk, chunk), 1)
    causal = c <= r
    tril = causal.astype(F32)
    eye = lax.broadcasted_iota(I32, (dk, dk), 0) == lax.broadcasted_iota(I32, (dk, dk), 1)
    gla = {}

    def gla_prep(s, t0):
        rows = slice(s * seq_rows + t0, s * seq_rows + t0 + chunk)
        b = jnp.dot(tril, pla[rows, :], precision=HIGHEST, preferred_element_type=F32)
        b_end = b[chunk - 1:chunk, :]
        k_c = pk[rows, :]
        gla[s, t0] = ((pq[rows, :] * jnp.exp(b)).astype(BF16), (k_c * jnp.exp(-b)).astype(BF16),
                      (k_c * jnp.exp(b_end - b)).astype(BF16), jnp.exp(b_end))

    def gla_head(s, t0, hd):
        rows = slice(s * seq_rows + t0, s * seq_rows + t0 + chunk)
        q_in, k_in, k_out, decay_row = gla[s, t0]
        ks = slice(hd * dk, (hd + 1) * dk)
        vs = slice(hd * dv, (hd + 1) * dv)
        vh = pv[rows, vs]
        att = lax.dot_general(q_in[:, ks], k_in[:, ks], (((1,), (1,)), ((), ())), preferred_element_type=F32)
        att = jnp.where(causal, att, 0.0).astype(BF16)
        s_h = state[s, hd]
        o = jnp.dot(att, vh, preferred_element_type=F32)
        o = o + jnp.dot(q_in[:, ks], s_h.astype(BF16), preferred_element_type=F32)
        decay_col = jnp.sum(jnp.where(eye, jnp.broadcast_to(decay_row[:, ks], (dk, dk)), 0.0), axis=1, keepdims=True)
        state[s, hd] = decay_col * s_h + lax.dot_general(k_out[:, ks], vh, (((0,), (0,)), ((), ())),
                                                         preferred_element_type=F32)
        o = o * lax.rsqrt(jnp.mean(o * o, axis=-1, keepdims=True) + EPS) * gn_ref[...]
        o_ref[rows, vs] = (o * _silu(pg[rows, vs])).astype(o_ref.dtype)

    stage2 = []
    for s in range(n_seq):
        stage2.append(functools.partial(conv_fill, s))
        stage2 += [functools.partial(conv_block, s, cb) for cb in range(c_conv // LANES)]
    stage2.append(conv_norm)
    for s in range(n_seq):
        for t0 in range(0, seq_rows, chunk):
            stage2.append(functools.partial(gla_prep, s, t0))
            stage2 += [functools.partial(gla_head, s, t0, hd) for hd in range(heads)]

    per = -(-len(stage2) // len(stage1))
    for n, piece in enumerate(stage1):
        piece()
        for other in stage2[n * per:(n + 1) * per]:
            other()

    pu[...] = jnp.concatenate([new["u", 0], new["u", 1]], axis=1)
    pq[...] = new["q"]
    pk[...] = new["k"]
    pv[...] = jnp.concatenate([new["v", 0], new["v", 1]], axis=1)
    pg[...] = jnp.concatenate([new["g", 0], new["g", 1]], axis=1)
    pla[...] = new["la"]

    @pl.when(i == pl.num_programs(1) - 1)
    def _():
        for s in range(n_seq):
            tail_ref[s] = win[s, seq_rows:seq_rows + HIST_PAD, :]
        sout_ref[...] = state[...]


def _mixer(x2d, hist_pad, s0, g1, w_in, w_lr2, blr2, w_dw, b_dw, ln_g, ln_b, gn, *, n_seq, seq_rows):
    bsz, heads, dk, dv = s0.shape
    d = x2d.shape[1]
    t = x2d.shape[0] // bsz
    c_conv = w_dw.shape[1]
    width = w_dw.shape[0]
    qk, vv = heads * dk, heads * dv
    assert (n_seq == 1 and t % seq_rows == 0) or (seq_rows == t and bsz % n_seq == 0)
    nt = t // seq_rows
    rows = n_seq * seq_rows
    span = seq_rows + HIST_PAD - 8
    chunk = 64 if seq_rows % 64 == 0 else seq_rows
    const = lambda b, i: (0, 0)
    tile_in = lambda b, i: (b * nt + jnp.minimum(i, nt - 1), 0)
    tile_out = lambda b, i: (b * nt + jnp.maximum(i - 1, 0), 0)
    per_seq3 = lambda b, i: (b, 0, 0)
    per_seq4 = lambda b, i: (b, 0, 0, 0)
    return pl.pallas_call(
        functools.partial(_mixer_kernel, n_seq=n_seq, seq_rows=seq_rows, chunk=chunk, heads=heads, dk=dk, dv=dv,
                          c_conv=c_conv, width=width),
        grid=(bsz // n_seq, nt + 1),
        in_specs=[
            pl.BlockSpec((rows, d), tile_in),
            pl.BlockSpec((n_seq, HIST_PAD, c_conv), per_seq3),
            pl.BlockSpec((n_seq, heads, dk, dv), per_seq4),
            pl.BlockSpec((1, d), const),
            pl.BlockSpec(w_in.shape, const, pipeline_mode=pl.Buffered(1)),
            pl.BlockSpec(w_lr2.shape, const),
            pl.BlockSpec((1, qk), const),
            pl.BlockSpec(w_dw.shape, const),
            pl.BlockSpec((1, c_conv), const), pl.BlockSpec((1, c_conv), const), pl.BlockSpec((1, c_conv), const),
            pl.BlockSpec((1, dv), const),
        ],
        out_specs=[
            pl.BlockSpec((rows, c_conv), tile_out), pl.BlockSpec((rows, vv), tile_out),
            pl.BlockSpec((n_seq, HIST_PAD, c_conv), per_seq3),
            pl.BlockSpec((n_seq, heads, dk, dv), per_seq4),
        ],
        out_shape=[
            jax.ShapeDtypeStruct((bsz * t, c_conv), BF16), jax.ShapeDtypeStruct((bsz * t, vv), BF16),
            jax.ShapeDtypeStruct((bsz, HIST_PAD, c_conv), F32),
            jax.ShapeDtypeStruct((bsz, heads, dk, dv), F32),
        ],
        scratch_shapes=[
            pltpu.VMEM((rows, c_conv), F32), pltpu.VMEM((rows, qk), F32), pltpu.VMEM((rows, qk), F32),
            pltpu.VMEM((rows, vv), BF16), pltpu.VMEM((rows, vv), F32), pltpu.VMEM((rows, qk), F32),
            pltpu.VMEM((n_seq, seq_rows + HIST_PAD, c_conv), F32),
            pltpu.VMEM((7, span, LANES), F32),
            pltpu.VMEM((rows, c_conv), F32),
            pltpu.VMEM((n_seq, heads, dk, dv), F32),
        ],
        compiler_params=_params(("arbitrary", "arbitrary")),
        name="mixer",
    )(x2d, hist_pad, s0, g1, w_in, w_lr2, blr2, w_dw, b_dw, ln_g, ln_b, gn)


GLA_SUB = 16
GLA_CHUNK = 128
CONV_ROWS = 64


def _zero_fill_step(zero_ref, zbuf, zsem, n_chunks, per_step):
    step = pl.program_id(0) * pl.num_programs(1) + pl.program_id(1)
    last = pl.num_programs(0) * pl.num_programs(1) - 1
    rows = zbuf.shape[0]

    def copy(idx):
        return pltpu.make_async_copy(zbuf, zero_ref.at[pl.ds(pl.multiple_of(idx * rows, rows), rows)], zsem)

    @pl.when(step == 0)
    def _():
        zbuf[...] = jnp.zeros_like(zbuf)

    for p in range(per_step):
        earlier = (step - 1) * per_step + p

        @pl.when((step > 0) & (earlier < n_chunks))
        def _():
            copy(earlier).wait()

    for p in range(per_step):
        idx = step * per_step + p

        @pl.when(idx < n_chunks)
        def _():
            copy(idx).start()

        @pl.when((step == last) & (idx < n_chunks))
        def _():
            copy(idx).wait()


def _front_kernel(x_ref, hist_ref, s0_ref, g1_ref, w_ref, wlr2_ref, blr2_ref,
                  wdw_ref, bdw_ref, lg_ref, lb_ref, gn_ref,
                  c_ref, o_ref, tail_ref, sout_ref, *rest,
                  n_seq, seq_rows, chunk, heads, dk, dv, c_conv, width, zero_chunks):
    i = pl.program_id(1)
    qk, vv = heads * dk, heads * dv
    if zero_chunks:
        zero_ref, win, shifted, cbuf, state, zbuf, zsem = rest
        _zero_fill_step(zero_ref, zbuf, zsem, *zero_chunks)
    else:
        win, shifted, cbuf, state = rest

    @pl.when(i == 0)
    def _():
        for s in range(n_seq):
            win[s, 0:HIST_PAD, :] = hist_ref[s]
        state[...] = s0_ref[...]

    @pl.when(i > 0)
    def _():
        for s in range(n_seq):
            win[s, 0:HIST_PAD, :] = win[s, seq_rows:seq_rows + HIST_PAD, :]

    h = _rms(x_ref[...], g1_ref[...]).astype(BF16)

    def mm(lo, n):
        return jnp.dot(h, w_ref[:, lo:lo + n], preferred_element_type=F32)

    u = mm(0, c_conv) * jax.nn.sigmoid(mm(c_conv, c_conv))
    lead = HIST_PAD - (width - 1)
    span = shifted.shape[1]
    conv_rows = min(seq_rows, CONV_ROWS)
    for s in range(n_seq):
        r0 = s * seq_rows
        win[s, HIST_PAD:HIST_PAD + seq_rows, :] = u[r0:r0 + seq_rows, :]
        for cb in range(c_conv // LANES):
            cs = slice(cb * LANES, (cb + 1) * LANES)
            for r in range(1, 8):
                shifted[r - 1] = win[s, r:r + span, cs]
            for t0 in range(0, seq_rows, conv_rows):
                acc = jnp.broadcast_to(bdw_ref[:, cs], (conv_rows, LANES))
                for j in range(width):
                    r, a8 = (lead + j) % 8, (lead + j) // 8 * 8
                    if r == 0:
                        tap = win[s, t0 + a8:t0 + a8 + conv_rows, cs]
                    else:
                        tap = shifted[r - 1, t0 + a8:t0 + a8 + conv_rows, :]
                    acc = acc + wdw_ref[j:j + 1, cs] * tap
                cbuf[r0 + t0:r0 + t0 + conv_rows, cs] = acc
    cv = cbuf[...]
    mu = jnp.mean(cv, axis=-1, keepdims=True)
    xc = cv - mu
    cn = xc * lax.rsqrt(jnp.mean(xc * xc, axis=-1, keepdims=True) + EPS) * lg_ref[...] + lb_ref[...]
    c_ref[...] = _silu(cn).astype(c_ref.dtype)

    off = 2 * c_conv
    q = mm(off, qk) * (dk ** -0.5)
    k = mm(off + qk, qk)
    v = mm(off + 2 * qk, vv).astype(BF16)
    g = mm(off + 2 * qk + vv, vv)
    lr = mm(off + 2 * qk + 2 * vv, wlr2_ref.shape[0])
    z = jnp.dot(lr, wlr2_ref[...], precision=HIGHEST, preferred_element_type=F32) + blr2_ref[...]
    la = _log_sigmoid(z) * (1.0 / GATE_TEMP)

    n_sub = chunk // GLA_SUB
    r = lax.broadcasted_iota(I32, (chunk, chunk), 0)
    c = lax.broadcasted_iota(I32, (chunk, chunk), 1)
    causal = c <= r
    local_sum = (causal & ((r // GLA_SUB) == (c // GLA_SUB))).astype(BF16)
    sub_rows = [slice(j * GLA_SUB, (j + 1) * GLA_SUB) for j in range(n_sub)]
    eye = lax.broadcasted_iota(I32, (dk, dk), 0) == lax.broadcasted_iota(I32, (dk, dk), 1)
    for s in range(n_seq):
        for t0 in range(0, seq_rows, chunk):
            rows = slice(s * seq_rows + t0, s * seq_rows + t0 + chunk)
            la_1 = la[rows, :].astype(BF16)
            rest = la[rows, :] - la_1.astype(F32)
            la_2 = rest.astype(BF16)
            la_3 = (rest - la_2.astype(F32)).astype(BF16)
            sums = jnp.dot(local_sum, jnp.concatenate([la_1, la_2, la_3], axis=1), preferred_element_type=F32)
            local = sums[:, 0:qk] + sums[:, qk:2 * qk] + sums[:, 2 * qk:3 * qk]
            bases = [jnp.zeros((1, qk), F32)]
            for j in range(1, n_sub):
                bases.append(bases[-1] + local[j * GLA_SUB - 1:j * GLA_SUB, :])
            base = jnp.concatenate([jnp.broadcast_to(bs, (GLA_SUB, qk)) for bs in bases], axis=0)
            b = base + local
            b_end = b[chunk - 1:chunk, :]
            q_c, k_c = q[rows, :], k[rows, :]
            q_loc = q_c * jnp.exp(local)
            k_loc = k_c * jnp.exp(-local)
            q_in = (q_c * jnp.exp(b)).astype(BF16)
            k_out = (k_c * jnp.exp(b_end - b)).astype(BF16)
            decay_row = jnp.exp(b_end)
            zero_rows = jnp.zeros((GLA_SUB, qk), F32)
            q_parts, k_parts = [], []
            for j in range(n_sub):
                q_parts.append(jnp.concatenate(
                    [q_loc[sub_rows[m], :] * jnp.exp(jnp.minimum(bases[m] - bases[j], 0.0)) for m in range(n_sub)],
                    axis=0).astype(BF16))
                k_parts.append(jnp.concatenate(
                    [k_loc[sub_rows[m], :] if m == j else zero_rows for m in range(n_sub)], axis=0).astype(BF16))
            for hd in range(heads):
                ks = slice(hd * dk, (hd + 1) * dk)
                vs = slice(hd * dv, (hd + 1) * dv)
                vh = v[rows, vs]
                q_cat = jnp.concatenate([p[:, ks] for p in q_parts], axis=1)
                k_cat = jnp.concatenate([p[:, ks] for p in k_parts], axis=1)
                att = lax.dot_general(q_cat, k_cat, (((1,), (1,)), ((), ())), preferred_element_type=F32)
                att = jnp.where(causal, att, 0.0).astype(BF16)
                s_h = state[s, hd]
                o = jnp.dot(att, vh, preferred_element_type=F32)
                o = o + jnp.dot(q_in[:, ks], s_h.astype(BF16), preferred_element_type=F32)
                decay_col = jnp.sum(jnp.where(eye, jnp.broadcast_to(decay_row[:, ks], (dk, dk)), 0.0),
                                    axis=1, keepdims=True)
                state[s, hd] = decay_col * s_h + lax.dot_general(k_out[:, ks], vh, (((0,), (0,)), ((), ())),
                                                                 preferred_element_type=F32)
                o = o * lax.rsqrt(jnp.mean(o * o, axis=-1, keepdims=True) + EPS) * gn_ref[...]
                o_ref[rows, vs] = (o * _silu(g[rows, vs])).astype(o_ref.dtype)

    @pl.when(i == pl.num_programs(1) - 1)
    def _():
        for s in range(n_seq):
            tail_ref[s] = win[s, seq_rows:seq_rows + HIST_PAD, :]
        sout_ref[...] = state[...]


def _front(x2d, hist_pad, s0, g1, w_in, w_lr2, blr2, w_dw, b_dw, ln_g, ln_b, gn, *, n_seq, seq_rows,
           zero_shape=None):
    bsz, heads, dk, dv = s0.shape
    d = x2d.shape[1]
    t = x2d.shape[0] // bsz
    c_conv = w_dw.shape[1]
    width = w_dw.shape[0]
    qk, vv = heads * dk, heads * dv
    assert (n_seq == 1 and t % seq_rows == 0) or (seq_rows == t and bsz % n_seq == 0)
    nt = t // seq_rows
    rows = n_seq * seq_rows
    chunk = GLA_CHUNK if seq_rows % GLA_CHUNK == 0 else seq_rows
    assert chunk % GLA_SUB == 0 and seq_rows % min(seq_rows, CONV_ROWS) == 0
    const = lambda b, i: (0, 0)
    tile = lambda b, i: (b * nt + i, 0)
    per_seq3 = lambda b, i: (b, 0, 0)
    per_seq4 = lambda b, i: (b, 0, 0, 0)
    extra_out_specs, extra_out_shape, extra_scratch, zero_chunks = [], [], [], None
    if zero_shape is not None:
        assert zero_shape[0] % EXPERT_ROWS == 0
        n_chunks = zero_shape[0] // EXPERT_ROWS
        zero_chunks = (n_chunks, -(-n_chunks // ((bsz // n_seq) * nt)))
        extra_out_specs = [pl.BlockSpec(memory_space=pl.ANY)]
        extra_out_shape = [jax.ShapeDtypeStruct(zero_shape, U32)]
        extra_scratch = [pltpu.VMEM((EXPERT_ROWS, zero_shape[1]), U32), pltpu.SemaphoreType.DMA(())]
    return pl.pallas_call(
        functools.partial(_front_kernel, n_seq=n_seq, seq_rows=seq_rows, chunk=chunk, heads=heads, dk=dk, dv=dv,
                          c_conv=c_conv, width=width, zero_chunks=zero_chunks),
        grid=(bsz // n_seq, nt),
        in_specs=[
            pl.BlockSpec((rows, d), tile),
            pl.BlockSpec((n_seq, HIST_PAD, c_conv), per_seq3),
            pl.BlockSpec((n_seq, heads, dk, dv), per_seq4),
            pl.BlockSpec((1, d), const),
            pl.BlockSpec(w_in.shape, const, pipeline_mode=pl.Buffered(1)),
            pl.BlockSpec(w_lr2.shape, const),
            pl.BlockSpec((1, qk), const),
            pl.BlockSpec(w_dw.shape, const),
            pl.BlockSpec((1, c_conv), const), pl.BlockSpec((1, c_conv), const), pl.BlockSpec((1, c_conv), const),
            pl.BlockSpec((1, dv), const),
        ],
        out_specs=[
            pl.BlockSpec((rows, c_conv), tile), pl.BlockSpec((rows, vv), tile),
            pl.BlockSpec((n_seq, HIST_PAD, c_conv), per_seq3),
            pl.BlockSpec((n_seq, heads, dk, dv), per_seq4),
        ] + extra_out_specs,
        out_shape=[
            jax.ShapeDtypeStruct((bsz * t, c_conv), BF16), jax.ShapeDtypeStruct((bsz * t, vv), BF16),
            jax.ShapeDtypeStruct((bsz, HIST_PAD, c_conv), F32),
            jax.ShapeDtypeStruct((bsz, heads, dk, dv), F32),
        ] + extra_out_shape,
        scratch_shapes=[
            pltpu.VMEM((n_seq, seq_rows + HIST_PAD, c_conv), F32),
            pltpu.VMEM((7, seq_rows + HIST_PAD - 8, LANES), F32),
            pltpu.VMEM((rows, c_conv), F32),
            pltpu.VMEM((n_seq, heads, dk, dv), F32),
        ] + extra_scratch,
        compiler_params=_params(("arbitrary", "arbitrary")),
        name="front",
    )(x2d, hist_pad, s0, g1, w_in, w_lr2, blr2, w_dw, b_dw, ln_g, ln_b, gn)


def _outproj_kernel(xp_ref, xs_ref, cp_ref, cs_ref, op_ref, os_ref, w_ref, g2_ref, wr_ref, br_ref, zeroed_ref,
                    x1_ref, dest_ref, gates_ref, pages_ref, counts_ref, sorted_ref,
                    stage, dest_vmem, dest_smem, cnt_s, page_s, npage_s, table_s, scatter_sems, dest_sems,
                    *, n_first, c_conv, n_groups, per_group, trash_row):
    i = pl.program_id(0)
    n_steps = pl.num_programs(0)
    tm = x1_ref.shape[0]
    n_exp = n_groups * per_group
    slot = i % 2
    prev = 1 - slot

    def dest_copy(s):
        return pltpu.make_async_copy(dest_vmem.at[s], dest_smem.at[s], dest_sems.at[s])

    def scatter_wait(s):
        for _ in range(2):
            pltpu.make_async_copy(stage.at[s], sorted_ref.at[pl.ds(0, tm)], scatter_sems.at[s]).wait()

    @pl.when(i == 0)
    def _():
        cnt_s[...] = jnp.zeros_like(cnt_s)
        page_s[...] = jnp.zeros_like(page_s)
        npage_s[...] = jnp.zeros_like(npage_s)
        table_s[...] = jnp.zeros_like(table_s)
        stage[1] = jnp.zeros(stage.shape[1:], stage.dtype)

        def fill(t, carry):
            dest_smem[1, 0, t] = trash_row + t
            dest_smem[1, 1, t] = trash_row + tm + t
            return carry

        lax.fori_loop(0, tm, fill, 0)

    @pl.when(i >= 1)
    def _():
        dest_copy(prev).wait()
        scatter_wait(slot)

    for t in range(tm):
        _row_copy(stage.at[prev], t, sorted_ref, dest_smem[prev, 0, t], scatter_sems.at[prev]).start()
        _row_copy(stage.at[prev], t, sorted_ref, dest_smem[prev, 1, t], scatter_sems.at[prev]).start()

    x = _pick(i, n_first, xp_ref, xs_ref)
    cc = _pick(i, n_first, cp_ref, cs_ref)
    oo = _pick(i, n_first, op_ref, os_ref)
    del zeroed_ref
    mix = jnp.dot(jnp.concatenate([cc, oo], axis=1), w_ref[...], preferred_element_type=F32)
    x1 = x + mix
    x1_ref[...] = x1
    h2 = _rms(x1, g2_ref[...])
    stage[slot] = _pack_pairs(h2)
    h_hi = h2.astype(BF16)
    h_lo = (h2 - h_hi.astype(F32)).astype(BF16)
    parts = lax.dot_general(wr_ref[...], jnp.concatenate([h_hi, h_lo], axis=0), (((1,), (1,)), ((), ())),
                            preferred_element_type=F32)
    n_r = br_ref.shape[0]
    logits = (parts[0:n_r, 0:tm] + parts[0:n_r, tm:] + parts[n_r:, 0:tm] + parts[n_r:, tm:]) + br_ref[...]
    lc = logits[0:n_groups, :]
    mc = jnp.max(lc, axis=0, keepdims=True)
    p_group = 1.0 / jnp.sum(jnp.exp(lc - mc), axis=0, keepdims=True)
    rows_c = lax.broadcasted_iota(I32, (n_groups, tm), 0)
    g_idx = jnp.min(jnp.where(lc == mc, rows_c, n_groups), axis=0, keepdims=True)
    lf = logits[n_groups:n_groups + n_exp, :]
    rows_f = lax.broadcasted_iota(I32, (n_exp, tm), 0)
    in_group = (rows_f >= g_idx * per_group) & (rows_f < (g_idx + 1) * per_group)
    neg = jnp.float32(-jnp.inf)
    l1 = jnp.where(in_group, lf, neg)
    m1 = jnp.max(l1, axis=0, keepdims=True)
    e1 = jnp.min(jnp.where(l1 == m1, rows_f, n_exp), axis=0, keepdims=True)
    l2 = jnp.where(rows_f == e1, neg, l1)
    m2 = jnp.max(l2, axis=0, keepdims=True)
    e2 = jnp.min(jnp.where(l2 == m2, rows_f, n_exp), axis=0, keepdims=True)
    r2 = jnp.exp(m2 - m1)
    w1 = 1.0 / (1.0 + r2)
    row8 = lax.broadcasted_iota(I32, (8, tm), 0)
    gates_ref[...] = jnp.where(row8 == 0, p_group * w1, jnp.where(row8 == 1, p_group * (r2 * w1), 0.0))

    oh0 = (rows_f == e1).astype(F32)
    oh1 = (rows_f == e2).astype(F32)
    both = oh0 + oh1
    tr = lax.broadcasted_iota(I32, (tm, tm), 0)
    tc = lax.broadcasted_iota(I32, (tm, tm), 1)
    earlier = jnp.dot(both.astype(BF16), (tr < tc).astype(BF16), preferred_element_type=F32)
    cnt = cnt_s[...]
    rank_base = earlier + cnt
    tile_cnt = jnp.sum(both, axis=1, keepdims=True)
    page_rows = float(EXPERT_ROWS)
    k0 = jnp.floor(cnt * (1.0 / page_rows))
    new_cnt = cnt + tile_cnt
    limit = (k0 + 1.0) * page_rows
    need_a = ((cnt == k0 * page_rows) & (tile_cnt > 0.0)).astype(F32)
    need_b = (new_cnt > limit).astype(F32)
    need = need_a + need_b
    er = lax.broadcasted_iota(I32, (n_exp, n_exp), 0)
    ec = lax.broadcasted_iota(I32, (n_exp, n_exp), 1)
    before = jnp.dot((ec < er).astype(BF16), jnp.broadcast_to(need, (n_exp, LANES)).astype(BF16),
                     preferred_element_type=F32)[:, 0:1]
    base = npage_s[...] + before
    page_a = jnp.where(need_a > 0.0, base, page_s[...])
    page_b = base + need_a
    npage_s[...] = npage_s[...] + jnp.sum(need, axis=0, keepdims=True)
    lane = lax.broadcasted_iota(I32, table_s.shape, 1).astype(F32)
    table = jnp.where((lane == k0) & (need_a > 0.0), page_a, table_s[...])
    table_s[...] = jnp.where((lane == k0 + 1.0) & (need_b > 0.0), page_b, table)
    cnt_s[...] = new_cnt
    page_s[...] = jnp.where(jnp.floor(new_cnt * (1.0 / page_rows)) == k0, page_a, page_b)

    def dest_rows(oh):
        rank = jnp.sum(oh * rank_base, axis=0, keepdims=True)
        lim = jnp.sum(oh * limit, axis=0, keepdims=True)
        pa = jnp.sum(oh * page_a, axis=0, keepdims=True)
        pb = jnp.sum(oh * page_b, axis=0, keepdims=True)
        within = rank - jnp.floor(rank * (1.0 / page_rows)) * page_rows
        return jnp.where(rank < lim, pa, pb) * page_rows + within

    dest = jnp.where(row8 == 0, dest_rows(oh0), jnp.where(row8 == 1, dest_rows(oh1), 0.0)).astype(I32)
    dest_ref[...] = dest
    dest_vmem[slot] = dest
    dest_copy(slot).start()

    @pl.when(i == n_steps - 1)
    def _():
        pages_ref[...] = table_s[...].astype(I32)
        counts_ref[...] = jnp.broadcast_to(cnt_s[...], counts_ref.shape).astype(I32)
        dest_copy(slot).wait()
        scatter_wait(prev)

        def last(j, carry):
            for r in range(DMA_UNROLL):
                t = j * DMA_UNROLL + r
                _row_copy(stage.at[slot], t, sorted_ref, dest_smem[slot, 0, t], scatter_sems.at[slot]).start()
                _row_copy(stage.at[slot], t, sorted_ref, dest_smem[slot, 1, t], scatter_sems.at[slot]).start()
            return carry

        lax.fori_loop(0, tm // DMA_UNROLL, last, 0)
        scatter_wait(slot)


def _outproj(xp, xs, cp, cs, op, os_, w_out, g2, wr, br, zeroed, *, n_groups, per_group, n_pages):
    n_p, d = xp.shape
    n_all = n_p + xs.shape[0]
    n_first = n_p // ROW_TILE
    n_exp = n_groups * per_group
    c_conv = cp.shape[1]
    vv = op.shape[1]
    tile = (d // 2,)
    first, second = _split_maps(n_first)
    const = lambda i: (0, 0)
    row = lambda i: (i, 0)
    col = lambda i: (0, i)
    rows_sorted = zeroed.shape[0]
    assert rows_sorted >= n_pages * EXPERT_ROWS + 2 * ROW_TILE
    assert ROW_TILE <= EXPERT_ROWS, "a tile may open at most two pages per expert"
    return pl.pallas_call(
        functools.partial(_outproj_kernel, n_first=n_first, c_conv=c_conv, n_groups=n_groups, per_group=per_group,
                          trash_row=n_pages * EXPERT_ROWS),
        grid=(n_all // ROW_TILE,),
        in_specs=[
            pl.BlockSpec((ROW_TILE, d), first), pl.BlockSpec((ROW_TILE, d), second),
            pl.BlockSpec((ROW_TILE, c_conv), first), pl.BlockSpec((ROW_TILE, c_conv), second),
            pl.BlockSpec((ROW_TILE, vv), first), pl.BlockSpec((ROW_TILE, vv), second),
            pl.BlockSpec(w_out.shape, const, pipeline_mode=pl.Buffered(1)),
            pl.BlockSpec((1, d), const),
            pl.BlockSpec(wr.shape, const),
            pl.BlockSpec(br.shape, const),
            pl.BlockSpec(memory_space=pl.ANY),
        ],
        out_specs=[
            pl.BlockSpec((ROW_TILE, d), row),
            pl.BlockSpec((8, ROW_TILE), col), pl.BlockSpec((8, ROW_TILE), col),
            pl.BlockSpec((n_exp, LANES), const), pl.BlockSpec((n_exp, LANES), const),
            pl.BlockSpec(memory_space=pl.ANY),
        ],
        out_shape=[
            jax.ShapeDtypeStruct((n_all, d), F32),
            jax.ShapeDtypeStruct((8, n_all), I32), jax.ShapeDtypeStruct((8, n_all), F32),
            jax.ShapeDtypeStruct((n_exp, LANES), I32), jax.ShapeDtypeStruct((n_exp, LANES), I32),
            jax.ShapeDtypeStruct((rows_sorted,) + tile, U32),
        ],
        scratch_shapes=[
            pltpu.VMEM((2, ROW_TILE) + tile, U32),
            pltpu.VMEM((2, 8, ROW_TILE), I32), pltpu.SMEM((2, 8, ROW_TILE), I32),
            pltpu.VMEM((n_exp, 1), F32), pltpu.VMEM((n_exp, 1), F32), pltpu.VMEM((1, 1), F32),
            pltpu.VMEM((n_exp, LANES), F32),
            pltpu.SemaphoreType.DMA((2,)), pltpu.SemaphoreType.DMA((2,)),
        ],
        compiler_params=_params(("arbitrary",)),
        input_output_aliases={10: 5},
        name="outproj",
    )(xp, xs, cp, cs, op, os_, w_out, g2, wr, br, zeroed)


def _experts_kernel(cnt_ref, pages_ref, xs_ref, wg_ref, wu_ref, wd_ref, ysp_ref,
                    xbuf, ybuf, wg_f32, wu_f32, wd_f32, wg_bf, wu_bf, wd_bf, first_blk, page_seq,
                    gsems, ysems, wsems, *, n_exp, table_lanes):
    e = pl.program_id(0)
    tb = xbuf.shape[1]
    n_pages = page_seq.shape[0]

    def n_pages_of(ex):
        return (cnt_ref[ex] + (tb - 1)) // tb

    def page_rows(blk):
        return pl.ds(pl.multiple_of(page_seq[blk] * tb, tb), tb)

    def fetch(blk, slot):
        return pltpu.make_async_copy(xs_ref.at[page_rows(blk)], xbuf.at[slot], gsems.at[slot])

    def writeback(blk, slot):
        return pltpu.make_async_copy(ybuf.at[slot], ysp_ref.at[page_rows(blk)], ysems.at[slot])

    def weight_copies(ex, slot):
        return (pltpu.make_async_copy(wg_ref.at[ex], wg_f32.at[slot], wsems.at[slot]),
                pltpu.make_async_copy(wu_ref.at[ex], wu_f32.at[slot], wsems.at[slot]),
                pltpu.make_async_copy(wd_ref.at[ex], wd_f32.at[slot], wsems.at[slot]))

    @pl.when(e == 0)
    def _():
        for ahead in range(min(WEIGHT_SLOTS - 1, n_exp)):
            for cp in weight_copies(ahead, ahead):
                cp.start(priority=1)

        def per_expert(ex, blk):
            first_blk[ex] = blk

            def per_page(j, carry):
                page_seq[blk + j] = pages_ref[ex * table_lanes + j]
                return carry

            lax.fori_loop(0, n_pages_of(ex), per_page, 0)
            return blk + n_pages_of(ex)

        first_blk[n_exp] = lax.fori_loop(0, n_exp, per_expert, 0)
        fetch(0, 0).start()

    @pl.when(e + WEIGHT_SLOTS - 1 < n_exp)
    def _():
        for cp in weight_copies(e + WEIGHT_SLOTS - 1, (e + WEIGHT_SLOTS - 1) % WEIGHT_SLOTS):
            cp.start(priority=1)

    b_lo = first_blk[e]
    b_hi = first_blk[e + 1]
    n_total = first_blk[n_exp]
    wslot = e % WEIGHT_SLOTS
    for cp in weight_copies(e, wslot):
        cp.wait()
    wg_bf[...] = wg_f32[wslot].astype(BF16)
    wu_bf[...] = wu_f32[wslot].astype(BF16)
    wd_bf[...] = wd_f32[wslot].astype(BF16)

    def block(b, carry):
        slot = b % 2

        @pl.when(b >= 2)
        def _():
            writeback(b, slot).wait()

        fetch(b, slot).wait()
        fetch(jnp.minimum(b + 1, n_total - 1), 1 - slot).start()

        def swiglu(rows):
            hi, lo = _unpack_pairs(xbuf[slot, rows, :])
            x = jnp.concatenate([hi.astype(BF16), lo.astype(BF16)], axis=1)
            hg = jnp.dot(x, wg_bf[...], preferred_element_type=F32)
            hu = jnp.dot(x, wu_bf[...], preferred_element_type=F32)
            hb = (_silu(hg) * hu).astype(BF16)
            ybuf[slot, rows, :] = _pack_pairs(jnp.dot(hb, wd_bf[...], preferred_element_type=F32))

        valid = cnt_ref[e] - (b - b_lo) * tb
        half = tb // 2

        @pl.when(valid > half)
        def _():
            swiglu(slice(0, tb))

        @pl.when(valid <= half)
        def _():
            swiglu(slice(0, half))
            ybuf[slot, half:tb, :] = jnp.zeros((tb - half,) + ybuf.shape[2:], ybuf.dtype)

        writeback(b, slot).start()
        return carry

    lax.fori_loop(b_lo, b_hi, block, 0)

    @pl.when(e == n_exp - 1)
    def _():
        fetch(0, n_total % 2).wait()

        @pl.when(n_total >= 2)
        def _():
            writeback(0, n_total % 2).wait()

        writeback(0, (n_total + 1) % 2).wait()
        ybuf[0] = jnp.zeros(ybuf.shape[1:], ybuf.dtype)

        def spare(blk):
            return pltpu.make_async_copy(ybuf.at[0], ysp_ref.at[pl.ds(pl.multiple_of(blk * tb, tb), tb)], ysems.at[0])

        def zero(blk, carry):
            spare(blk).start()
            return carry

        lax.fori_loop(n_total, n_pages, zero, 0)

        def zero_wait(blk, carry):
            spare(0).wait()
            return carry

        lax.fori_loop(n_total, n_pages, zero_wait, 0)


def _experts(counts, pages_flat, xs_sorted, w_gate, w_up, w_down, *, n_pages, table_lanes):
    tile = xs_sorted.shape[1:]
    n_exp, d, ff = w_gate.shape
    anyspec = pl.BlockSpec(memory_space=pl.ANY)
    grid_spec = pltpu.PrefetchScalarGridSpec(
        num_scalar_prefetch=2,
        grid=(n_exp,),
        in_specs=[anyspec, anyspec, anyspec, anyspec],
        out_specs=anyspec,
        scratch_shapes=[
            pltpu.VMEM((2, EXPERT_ROWS) + tile, U32), pltpu.VMEM((2, EXPERT_ROWS) + tile, U32),
            pltpu.VMEM((WEIGHT_SLOTS, d, ff), F32), pltpu.VMEM((WEIGHT_SLOTS, d, ff), F32),
            pltpu.VMEM((WEIGHT_SLOTS, ff, d), F32),
            pltpu.VMEM((d, ff), BF16), pltpu.VMEM((d, ff), BF16), pltpu.VMEM((ff, d), BF16),
            pltpu.SMEM((n_exp + 1,), I32), pltpu.SMEM((n_pages,), I32),
            pltpu.SemaphoreType.DMA((2,)), pltpu.SemaphoreType.DMA((2,)), pltpu.SemaphoreType.DMA((WEIGHT_SLOTS,)),
        ],
    )
    return pl.pallas_call(
        functools.partial(_experts_kernel, n_exp=n_exp, table_lanes=table_lanes),
        grid_spec=grid_spec,
        out_shape=jax.ShapeDtypeStruct((n_pages * EXPERT_ROWS,) + tile, U32),
        compiler_params=_params(("arbitrary",)),
        name="experts",
    )(counts, pages_flat, xs_sorted, w_gate, w_up, w_down)


def _combine_kernel(dest_ref, dest_next_ref, gates_ref, x1_ref, ysp_ref, gf_ref, yp_ref, ysmp_ref,
                    buf0, buf1, sems, *, n_first):
    i = pl.program_id(0)
    n = pl.num_programs(0)
    tm = x1_ref.shape[0]
    slot = i % 2

    def gather(d_ref, s):
        def body(j, carry):
            for r in range(DMA_UNROLL):
                t = j * DMA_UNROLL + r
                _row_copy(ysp_ref, d_ref[0, t], buf0.at[s], t, sems.at[s]).start(priority=0)
                _row_copy(ysp_ref, d_ref[1, t], buf1.at[s], t, sems.at[s]).start(priority=1)
            return carry

        lax.fori_loop(0, tm // DMA_UNROLL, body, 0)

    def gather_wait(s):
        pltpu.make_async_copy(ysp_ref.at[pl.ds(0, tm)], buf0.at[s], sems.at[s]).wait()
        pltpu.make_async_copy(ysp_ref.at[pl.ds(0, tm)], buf1.at[s], sems.at[s]).wait()

    @pl.when(i == 0)
    def _():
        gather(dest_ref, 0)

    for t in range(tm):
        _row_copy(ysp_ref, dest_next_ref[0, t], buf0.at[1 - slot], t, sems.at[1 - slot]).start(priority=0)
        _row_copy(ysp_ref, dest_next_ref[1, t], buf1.at[1 - slot], t, sems.at[1 - slot]).start(priority=1)

    gather_wait(slot)
    hi0, lo0 = _unpack_pairs(buf0[slot])
    hi1, lo1 = _unpack_pairs(buf1[slot])
    g0 = gates_ref[:, 0:1]
    g1 = gates_ref[:, 1:2]
    moe = jnp.concatenate([g0 * hi0 + g1 * hi1, g0 * lo0 + g1 * lo1], axis=1)
    y = _rms(x1_ref[...] + moe, gf_ref[...])

    @pl.when(i < n_first)
    def _():
        yp_ref[...] = y

    @pl.when(i >= n_first)
    def _():
        ysmp_ref[...] = y

    @pl.when(i == n - 1)
    def _():
        gather_wait(1 - slot)


def _combine(dest, gates_t, x1, ysp, gf, *, n_p):
    n_all, d = x1.shape
    tile = ysp.shape[1:]
    n_first = n_p // ROW_TILE
    n_tiles = n_all // ROW_TILE
    first, second = _split_maps(n_first)
    return pl.pallas_call(
        functools.partial(_combine_kernel, n_first=n_first),
        grid=(n_tiles,),
        in_specs=[
            pl.BlockSpec((8, ROW_TILE), lambda i: (0, i), memory_space=pltpu.SMEM),
            pl.BlockSpec((8, ROW_TILE), lambda i: (0, jnp.minimum(i + 1, n_tiles - 1)), memory_space=pltpu.SMEM),
            pl.BlockSpec((ROW_TILE, 8), lambda i: (i, 0)),
            pl.BlockSpec((ROW_TILE, d), lambda i: (i, 0)),
            pl.BlockSpec(memory_space=pl.ANY),
            pl.BlockSpec((1, d), lambda i: (0, 0)),
        ],
        out_specs=[pl.BlockSpec((ROW_TILE, d), first), pl.BlockSpec((ROW_TILE, d), second)],
        out_shape=[jax.ShapeDtypeStruct((n_p, d), F32), jax.ShapeDtypeStruct((n_all - n_p, d), F32)],
        scratch_shapes=[pltpu.VMEM((2, ROW_TILE) + tile, U32), pltpu.VMEM((2, ROW_TILE) + tile, U32),
                        pltpu.SemaphoreType.DMA((2,))],
        compiler_params=_params(("arbitrary",)),
        name="combine",
    )(dest, dest, gates_t, x1, ysp, gf)


def _chunk_for(t):
    return 64 if t % 64 == 0 else t


def kernel(x_prompt, x_sample, cache_conv, state_gla, norm1_g, w_in, w_lr2, b_lr2, w_dw, b_dw, conv_ln_g, conv_ln_b, gla_norm_g, w_out, norm2_g, w_router_coarse, b_router_coarse, w_router_fine, b_router_fine, w_exp_gate, w_exp_up, w_exp_down, norm_f_g):
    assert norm1_g.shape[0] == 1, "single trunk layer"
    bp, tp, d = x_prompt.shape
    bs, ts, _ = x_sample.shape
    heads, dk, dv = state_gla.shape[2:]
    c_conv = w_dw.shape[2]
    width = w_dw.shape[1]
    rank = w_lr2.shape[1]
    qk, vv = heads * dk, heads * dv
    n_groups, _, per_group = w_router_fine.shape[1:]
    n_exp = n_groups * per_group
    n_p, n_s = bp * tp, bs * ts
    n_s_pad = -(-n_s // ROW_TILE) * ROW_TILE
    pad_rows = lambda a: jnp.pad(a, ((0, n_s_pad - n_s), (0, 0)))
    n_all = n_p + n_s_pad
    assert n_p % ROW_TILE == 0 and width - 1 <= HIST_PAD

    xp = x_prompt.reshape(n_p, d)
    xs = pad_rows(x_sample.reshape(n_s, d))
    row = lambda a: a.reshape(1, -1)

    mixer_args = (row(norm1_g[0]), w_in[0].astype(BF16), w_lr2[0], row(b_lr2[0]),
                  w_dw[0], row(b_dw[0]), row(conv_ln_g[0]), row(conv_ln_b[0]), row(gla_norm_g[0]))
    hist_p = jnp.zeros((bp, HIST_PAD, c_conv), F32)
    hist_s = jnp.pad(cache_conv[0], ((0, 0), (HIST_PAD - (width - 1), 0), (0, 0)))
    s0_p = jnp.zeros((bp, heads, dk, dv), F32)
    n_pages = (2 * n_all) // EXPERT_ROWS + n_exp
    sorted_rows = n_pages * EXPERT_ROWS + -(-2 * ROW_TILE // EXPERT_ROWS) * EXPERT_ROWS
    c_p, o_p, tail_p, gla_p, zeroed = _front(xp, hist_p, s0_p, *mixer_args, n_seq=1, seq_rows=ROW_TILE,
                                             zero_shape=(sorted_rows, d // 2))
    c_s, o_s, tail_s, gla_s = _front(x_sample.reshape(n_s, d), hist_s, state_gla[0], *mixer_args,
                                     n_seq=bs, seq_rows=ts)
    c_s, o_s = pad_rows(c_s), pad_rows(o_s)

    wr = jnp.concatenate([w_router_coarse[0].T,
                          jnp.transpose(w_router_fine[0], (0, 2, 1)).reshape(n_exp, d)], axis=0)
    br = jnp.concatenate([b_router_coarse[0], b_router_fine[0].reshape(n_exp)])
    r_rows = -(-(n_groups + n_exp) // 8) * 8
    wr = jnp.pad(wr, ((0, r_rows - wr.shape[0]), (0, 0)))
    br = jnp.pad(br, (0, r_rows - br.shape[0])).reshape(r_rows, 1)
    wr_hi = wr.astype(BF16)
    wr = jnp.concatenate([wr_hi, (wr - wr_hi.astype(F32)).astype(BF16)], axis=0)
    n_pages = (2 * n_all) // EXPERT_ROWS + n_exp
    assert n_all // EXPERT_ROWS + 2 <= LANES, "page table row must hold one expert's pages"
    x1, dest, gates, pages, counts, xs_sorted = _outproj(
        xp, xs, c_p, c_s, o_p, o_s, w_out[0].astype(BF16), row(norm2_g[0]), wr, br, zeroed,
        n_groups=n_groups, per_group=per_group, n_pages=n_pages)
    ysp = _experts(counts[:, 0], pages.reshape(-1), xs_sorted, w_exp_gate[0], w_exp_up[0], w_exp_down[0],
                   n_pages=n_pages, table_lanes=LANES)
    y_p, y_s = _combine(dest, gates.T, x1, ysp, row(norm_f_g), n_p=n_p)

    lead = HIST_PAD - (width - 1)
    return (y_p.reshape(bp, tp, d), y_s[:n_s].reshape(bs, ts, d), tail_p[:, lead:][None], gla_p[None],
            tail_s[:, lead:][None], gla_s[None])
```

```python
import functools

import jax
import jax.numpy as jnp
from jax import lax
from jax.experimental import pallas as pl
from jax.experimental.pallas import tpu as pltpu

F32 = jnp.float32
BF16 = jnp.bfloat16
I32 = jnp.int32
U32 = jnp.uint32
EPS = 1e-6
GATE_TEMP = 16.0
HIGHEST = lax.Precision.HIGHEST

LANES = 128
ROW_TILE = 256
EXPERT_ROWS = 256
HIST_PAD = 32
DMA_UNROLL = 8
COMBINE_ROWS = 32
WEIGHT_SLOTS = 3
VMEM_LIMIT = 56 * 1024 * 1024


def _params(semantics, vmem=VMEM_LIMIT):
    return pltpu.CompilerParams(dimension_semantics=semantics, vmem_limit_bytes=vmem)


def _rms(x, g):
    return x * lax.rsqrt(jnp.mean(x * x, axis=-1, keepdims=True) + EPS) * g


def _silu(x):
    return x * jax.nn.sigmoid(x)


def _log_sigmoid(z):
    return jnp.minimum(z, 0.0) - jnp.log(1.0 + jnp.exp(-jnp.abs(z)))


def _pick(i, n_first, first_ref, second_ref):
    return jnp.where(i < n_first, first_ref[...], second_ref[...])


def _split_maps(n_first):
    first = lambda i: (jnp.minimum(i, n_first - 1), 0)
    second = lambda i: (jnp.maximum(i - n_first, 0), 0)
    return first, second


def _pack_pairs(x):
    half = x.shape[1] // 2
    hi = lax.bitcast_convert_type(x[:, :half].astype(BF16).astype(F32), U32)
    lo = lax.bitcast_convert_type(x[:, half:].astype(BF16).astype(F32), U32)
    return hi | (lo >> 16)


def _unpack_pairs(p):
    hi = lax.bitcast_convert_type(p & jnp.uint32(0xFFFF0000), F32)
    lo = lax.bitcast_convert_type(p << 16, F32)
    return hi, lo


def _row_copy(src, s, dst, d, sem):
    return pltpu.make_async_copy(src.at[pl.ds(s, 1)], dst.at[pl.ds(d, 1)], sem)


def _inproj_kernel(xp_ref, xs_ref, g1_ref, w_ref, wlr2_ref, blr2_ref,
                   u_ref, q_ref, k_ref, v_ref, g_ref, la_ref, *, n_first, c_conv, qk, vv, dk):
    i = pl.program_id(0)
    x = _pick(i, n_first, xp_ref, xs_ref)
    h = _rms(x, g1_ref[...]).astype(BF16)

    def mm(lo, width):
        return jnp.dot(h, w_ref[:, lo:lo + width], preferred_element_type=F32)

    a = mm(0, c_conv)
    a_gate = mm(c_conv, c_conv)
    u_ref[...] = a * jax.nn.sigmoid(a_gate)
    off = 2 * c_conv
    q_ref[...] = mm(off, qk) * (dk ** -0.5)
    k_ref[...] = mm(off + qk, qk)
    v_ref[...] = mm(off + 2 * qk, vv)
    g_ref[...] = mm(off + 2 * qk + vv, vv)
    lr = mm(off + 2 * qk + 2 * vv, LANES)
    z = jnp.dot(lr, wlr2_ref[...], precision=HIGHEST, preferred_element_type=F32) + blr2_ref[...]
    la_ref[...] = _log_sigmoid(z) * (1.0 / GATE_TEMP)


def _inproj(xp, xs, g1, w_pad, wlr2_pad, blr2, *, c_conv, qk, vv, dk):
    n_p, d = xp.shape
    n_s = xs.shape[0]
    n_all = n_p + n_s
    n_first = n_p // ROW_TILE
    grid = (n_all // ROW_TILE,)
    first, second = _split_maps(n_first)
    const = lambda i: (0, 0)
    row = lambda i: (i, 0)
    widths = (c_conv, qk, qk, vv, vv, qk)
    return pl.pallas_call(
        functools.partial(_inproj_kernel, n_first=n_first, c_conv=c_conv, qk=qk, vv=vv, dk=dk),
        grid=grid,
        in_specs=[
            pl.BlockSpec((ROW_TILE, d), first),
            pl.BlockSpec((ROW_TILE, d), second),
            pl.BlockSpec((1, d), const),
            pl.BlockSpec(w_pad.shape, const, pipeline_mode=pl.Buffered(1)),
            pl.BlockSpec(wlr2_pad.shape, const),
            pl.BlockSpec((1, qk), const),
        ],
        out_specs=[pl.BlockSpec((ROW_TILE, w), row) for w in widths],
        out_shape=[jax.ShapeDtypeStruct((n_all, w), F32) for w in widths],
        compiler_params=_params(("arbitrary",)),
        name="inproj",
    )(xp, xs, g1, w_pad, wlr2_pad, blr2)


def _conv_kernel(u_ref, hist_ref, w_ref, b_ref, lg_ref, lb_ref, c_ref, win, cbuf, *, tt, width):
    i = pl.program_id(1)

    @pl.when(i == 0)
    def _():
        win[0:HIST_PAD, :] = hist_ref[0]

    @pl.when(i > 0)
    def _():
        win[0:HIST_PAD, :] = win[tt:tt + HIST_PAD, :]

    win[HIST_PAD:HIST_PAD + tt, :] = u_ref[...]
    lead = HIST_PAD - (width - 1)
    n_ch = u_ref.shape[1]
    for cb in range(n_ch // LANES):
        cs = slice(cb * LANES, (cb + 1) * LANES)
        acc = jnp.broadcast_to(b_ref[:, cs], (tt, LANES))
        for j in range(width):
            acc = acc + w_ref[j:j + 1, cs] * win[lead + j:lead + j + tt, cs]
        cbuf[:, cs] = acc
    c = cbuf[...]
    mu = jnp.mean(c, axis=-1, keepdims=True)
    xc = c - mu
    y = xc * lax.rsqrt(jnp.mean(xc * xc, axis=-1, keepdims=True) + EPS) * lg_ref[...] + lb_ref[...]
    c_ref[...] = _silu(y).astype(c_ref.dtype)


def _conv(u_all, hist_pad, w_dw, b_dw, ln_g, ln_b, *, row0, bsz, t, tt):
    n_ch = u_all.shape[1]
    width = w_dw.shape[0]
    nt = t // tt
    blk0 = row0 // tt
    const = lambda b, i: (0, 0)
    return pl.pallas_call(
        functools.partial(_conv_kernel, tt=tt, width=width),
        grid=(bsz, nt),
        in_specs=[
            pl.BlockSpec((tt, n_ch), lambda b, i: (blk0 + b * nt + i, 0)),
            pl.BlockSpec((1, HIST_PAD, n_ch), lambda b, i: (b, 0, 0)),
            pl.BlockSpec(w_dw.shape, const),
            pl.BlockSpec((1, n_ch), const),
            pl.BlockSpec((1, n_ch), const),
            pl.BlockSpec((1, n_ch), const),
        ],
        out_specs=pl.BlockSpec((tt, n_ch), lambda b, i: (b * nt + i, 0)),
        out_shape=jax.ShapeDtypeStruct((bsz * t, n_ch), BF16),
        scratch_shapes=[pltpu.VMEM((tt + HIST_PAD, n_ch), F32), pltpu.VMEM((tt, n_ch), F32)],
        compiler_params=_params(("arbitrary", "arbitrary")),
        name="conv",
    )(u_all, hist_pad, w_dw, b_dw, ln_g, ln_b)


def _gla_kernel(q_ref, k_ref, v_ref, g_ref, la_ref, s0_ref, gn_ref, o_ref, sout_ref, state,
                *, chunk, heads, dk, dv):
    i = pl.program_id(1)

    @pl.when(i == 0)
    def _():
        state[...] = s0_ref[0]

    la = la_ref[...]
    r = lax.broadcasted_iota(I32, (chunk, chunk), 0)
    c = lax.broadcasted_iota(I32, (chunk, chunk), 1)
    causal = c <= r
    b = jnp.dot(causal.astype(F32), la, precision=HIGHEST, preferred_element_type=F32)
    b_end = b[chunk - 1:chunk, :]
    q_in = (q_ref[...] * jnp.exp(b)).astype(BF16)
    k_in = (k_ref[...] * jnp.exp(-b)).astype(BF16)
    k_out = (k_ref[...] * jnp.exp(b_end - b)).astype(BF16)
    decay_row = jnp.exp(b_end)
    eye = lax.broadcasted_iota(I32, (dk, dk), 0) == lax.broadcasted_iota(I32, (dk, dk), 1)
    for h in range(heads):
        ks = slice(h * dk, (h + 1) * dk)
        vs = slice(h * dv, (h + 1) * dv)
        vh = v_ref[:, vs].astype(BF16)
        att = lax.dot_general(q_in[:, ks], k_in[:, ks], (((1,), (1,)), ((), ())), preferred_element_type=F32)
        att = jnp.where(causal, att, 0.0).astype(BF16)
        s_h = state[h]
        o = jnp.dot(att, vh, preferred_element_type=F32)
        o = o + jnp.dot(q_in[:, ks], s_h.astype(BF16), preferred_element_type=F32)
        decay_col = jnp.sum(jnp.where(eye, jnp.broadcast_to(decay_row[:, ks], (dk, dk)), 0.0), axis=1, keepdims=True)
        state[h] = decay_col * s_h + lax.dot_general(k_out[:, ks], vh, (((0,), (0,)), ((), ())),
                                                     preferred_element_type=F32)
        o = o * lax.rsqrt(jnp.mean(o * o, axis=-1, keepdims=True) + EPS) * gn_ref[...]
        o_ref[:, vs] = (o * _silu(g_ref[:, vs])).astype(o_ref.dtype)

    @pl.when(i == pl.num_programs(1) - 1)
    def _():
        sout_ref[0] = state[...]


def _gla(q_all, k_all, v_all, g_all, la_all, s0, gn, *, row0, bsz, t, chunk):
    heads, dk, dv = s0.shape[1:]
    nt = t // chunk
    blk0 = row0 // chunk
    rows = lambda b, i: (blk0 + b * nt + i, 0)
    return pl.pallas_call(
        functools.partial(_gla_kernel, chunk=chunk, heads=heads, dk=dk, dv=dv),
        grid=(bsz, nt),
        in_specs=[
            pl.BlockSpec((chunk, heads * dk), rows),
            pl.BlockSpec((chunk, heads * dk), rows),
            pl.BlockSpec((chunk, heads * dv), rows),
            pl.BlockSpec((chunk, heads * dv), rows),
            pl.BlockSpec((chunk, heads * dk), rows),
            pl.BlockSpec((1, heads, dk, dv), lambda b, i: (b, 0, 0, 0)),
            pl.BlockSpec((1, dv), lambda b, i: (0, 0)),
        ],
        out_specs=[
            pl.BlockSpec((chunk, heads * dv), lambda b, i: (b * nt + i, 0)),
            pl.BlockSpec((1, heads, dk, dv), lambda b, i: (b, 0, 0, 0)),
        ],
        out_shape=[
            jax.ShapeDtypeStruct((bsz * t, heads * dv), BF16),
            jax.ShapeDtypeStruct((bsz, heads, dk, dv), F32),
        ],
        scratch_shapes=[pltpu.VMEM((heads, dk, dv), F32)],
        compiler_params=_params(("arbitrary", "arbitrary")),
        name="gla",
    )(q_all, k_all, v_all, g_all, la_all, s0, gn)


def _mixer_kernel(x_ref, hist_ref, s0_ref, g1_ref, w_ref, wlr2_ref, blr2_ref,
                  wdw_ref, bdw_ref, lg_ref, lb_ref, gn_ref,
                  c_ref, o_ref, tail_ref, sout_ref, pu, pq, pk, pv, pg, pla, win, shifted, cbuf, state,
                  *, n_seq, seq_rows, chunk, heads, dk, dv, c_conv, width):
    i = pl.program_id(1)
    qk, vv = heads * dk, heads * dv

    @pl.when(i == 0)
    def _():
        for buf in (pu, pq, pk, pv, pg, pla):
            buf[...] = jnp.zeros_like(buf)
        win[...] = jnp.zeros_like(win)
        state[...] = jnp.zeros_like(state)

    @pl.when(i == 1)
    def _():
        for s in range(n_seq):
            win[s, 0:HIST_PAD, :] = hist_ref[s]
        state[...] = s0_ref[...]

    @pl.when(i > 1)
    def _():
        for s in range(n_seq):
            win[s, 0:HIST_PAD, :] = win[s, seq_rows:seq_rows + HIST_PAD, :]

    h = _rms(x_ref[...], g1_ref[...]).astype(BF16)

    def mm(lo, n):
        return jnp.dot(h, w_ref[:, lo:lo + n], preferred_element_type=F32)

    off = 2 * c_conv
    new = {}

    def proj_u(j, n):
        lo = j * n
        new["u", j] = mm(lo, n) * jax.nn.sigmoid(mm(c_conv + lo, n))

    def proj(name, lo, n, scale=None, dtype=F32):
        val = mm(lo, n)
        new[name] = (val if scale is None else val * scale).astype(dtype)

    def proj_la():
        lr = mm(off + 2 * qk + 2 * vv, wlr2_ref.shape[0])
        z = jnp.dot(lr, wlr2_ref[...], precision=HIGHEST, preferred_element_type=F32) + blr2_ref[...]
        new["la"] = _log_sigmoid(z) * (1.0 / GATE_TEMP)

    half_c, half_v = c_conv // 2, vv // 2
    stage1 = [
        functools.partial(proj_u, 0, half_c), functools.partial(proj_u, 1, half_c),
        functools.partial(proj, "q", off, qk, dk ** -0.5), functools.partial(proj, "k", off + qk, qk),
        functools.partial(proj, ("v", 0), off + 2 * qk, half_v, None, BF16),
        functools.partial(proj, ("v", 1), off + 2 * qk + half_v, half_v, None, BF16),
        functools.partial(proj, ("g", 0), off + 2 * qk + vv, half_v),
        functools.partial(proj, ("g", 1), off + 2 * qk + vv + half_v, half_v),
        proj_la,
    ]

    lead = HIST_PAD - (width - 1)
    span = shifted.shape[1]

    def conv_fill(s):
        r0 = s * seq_rows
        win[s, HIST_PAD:HIST_PAD + seq_rows, :] = pu[r0:r0 + seq_rows, :]

    def conv_block(s, cb):
        r0 = s * seq_rows
        cs = slice(cb * LANES, (cb + 1) * LANES)
        for r in range(1, 8):
            shifted[r - 1] = win[s, r:r + span, cs]
        for t0 in range(0, seq_rows, chunk):
            acc = jnp.broadcast_to(bdw_ref[:, cs], (chunk, LANES))
            for j in range(width):
                r, a8 = (lead + j) % 8, (lead + j) // 8 * 8
                if r == 0:
                    tap = win[s, t0 + a8:t0 + a8 + chunk, cs]
                else:
                    tap = shifted[r - 1, t0 + a8:t0 + a8 + chunk, :]
                acc = acc + wdw_ref[j:j + 1, cs] * tap
            cbuf[r0 + t0:r0 + t0 + chunk, cs] = acc

    def conv_norm():
        cv = cbuf[...]
        mu = jnp.mean(cv, axis=-1, keepdims=True)
        xc = cv - mu
        cn = xc * lax.rsqrt(jnp.mean(xc * xc, axis=-1, keepdims=True) + EPS) * lg_ref[...] + lb_ref[...]
        c_ref[...] = _silu(cn).astype(c_ref.dtype)

    r = lax.broadcasted_iota(I32, (chunk, chunk), 0)
    c = lax.broadcasted_iota(I32, (chunk, chunk), 1)
    causal = c <= r
    tril = causal.astype(F32)
    eye = lax.broadcasted_iota(I32, (dk, dk), 0) == lax.broadcasted_iota(I32, (dk, dk), 1)
    gla = {}

    def gla_prep(s, t0):
        rows = slice(s * seq_rows + t0, s * seq_rows + t0 + chunk)
        b = jnp.dot(tril, pla[rows, :], precision=HIGHEST, preferred_element_type=F32)
        b_end = b[chunk - 1:chunk, :]
        k_c = pk[rows, :]
        gla[s, t0] = ((pq[rows, :] * jnp.exp(b)).astype(BF16), (k_c * jnp.exp(-b)).astype(BF16),
                      (k_c * jnp.exp(b_end - b)).astype(BF16), jnp.exp(b_end))

    def gla_head(s, t0, hd):
        rows = slice(s * seq_rows + t0, s * seq_rows + t0 + chunk)
        q_in, k_in, k_out, decay_row = gla[s, t0]
        ks = slice(hd * dk, (hd + 1) * dk)
        vs = slice(hd * dv, (hd + 1) * dv)
        vh = pv[rows, vs]
        att = lax.dot_general(q_in[:, ks], k_in[:, ks], (((1,), (1,)), ((), ())), preferred_element_type=F32)
        att = jnp.where(causal, att, 0.0).astype(BF16)
        s_h = state[s, hd]
        o = jnp.dot(att, vh, preferred_element_type=F32)
        o = o + jnp.dot(q_in[:, ks], s_h.astype(BF16), preferred_element_type=F32)
        decay_col = jnp.sum(jnp.where(eye, jnp.broadcast_to(decay_row[:, ks], (dk, dk)), 0.0), axis=1, keepdims=True)
        state[s, hd] = decay_col * s_h + lax.dot_general(k_out[:, ks], vh, (((0,), (0,)), ((), ())),
                                                         preferred_element_type=F32)
        o = o * lax.rsqrt(jnp.mean(o * o, axis=-1, keepdims=True) + EPS) * gn_ref[...]
        o_ref[rows, vs] = (o * _silu(pg[rows, vs])).astype(o_ref.dtype)

    stage2 = []
    for s in range(n_seq):
        stage2.append(functools.partial(conv_fill, s))
        stage2 += [functools.partial(conv_block, s, cb) for cb in range(c_conv // LANES)]
    stage2.append(conv_norm)
    for s in range(n_seq):
        for t0 in range(0, seq_rows, chunk):
            stage2.append(functools.partial(gla_prep, s, t0))
            stage2 += [functools.partial(gla_head, s, t0, hd) for hd in range(heads)]

    per = -(-len(stage2) // len(stage1))
    for n, piece in enumerate(stage1):
        piece()
        for other in stage2[n * per:(n + 1) * per]:
            other()

    pu[...] = jnp.concatenate([new["u", 0], new["u", 1]], axis=1)
    pq[...] = new["q"]
    pk[...] = new["k"]
    pv[...] = jnp.concatenate([new["v", 0], new["v", 1]], axis=1)
    pg[...] = jnp.concatenate([new["g", 0], new["g", 1]], axis=1)
    pla[...] = new["la"]

    @pl.when(i == pl.num_programs(1) - 1)
    def _():
        for s in range(n_seq):
            tail_ref[s] = win[s, seq_rows:seq_rows + HIST_PAD, :]
        sout_ref[...] = state[...]


def _mixer(x2d, hist_pad, s0, g1, w_in, w_lr2, blr2, w_dw, b_dw, ln_g, ln_b, gn, *, n_seq, seq_rows):
    bsz, heads, dk, dv = s0.shape
    d = x2d.shape[1]
    t = x2d.shape[0] // bsz
    c_conv = w_dw.shape[1]
    width = w_dw.shape[0]
    qk, vv = heads * dk, heads * dv
    assert (n_seq == 1 and t % seq_rows == 0) or (seq_rows == t and bsz % n_seq == 0)
    nt = t // seq_rows
    rows = n_seq * seq_rows
    span = seq_rows + HIST_PAD - 8
    chunk = 64 if seq_rows % 64 == 0 else seq_rows
    const = lambda b, i: (0, 0)
    tile_in = lambda b, i: (b * nt + jnp.minimum(i, nt - 1), 0)
    tile_out = lambda b, i: (b * nt + jnp.maximum(i - 1, 0), 0)
    per_seq3 = lambda b, i: (b, 0, 0)
    per_seq4 = lambda b, i: (b, 0, 0, 0)
    return pl.pallas_call(
        functools.partial(_mixer_kernel, n_seq=n_seq, seq_rows=seq_rows, chunk=chunk, heads=heads, dk=dk, dv=dv,
                          c_conv=c_conv, width=width),
        grid=(bsz // n_seq, nt + 1),
        in_specs=[
            pl.BlockSpec((rows, d), tile_in),
            pl.BlockSpec((n_seq, HIST_PAD, c_conv), per_seq3),
            pl.BlockSpec((n_seq, heads, dk, dv), per_seq4),
            pl.BlockSpec((1, d), const),
            pl.BlockSpec(w_in.shape, const, pipeline_mode=pl.Buffered(1)),
            pl.BlockSpec(w_lr2.shape, const),
            pl.BlockSpec((1, qk), const),
            pl.BlockSpec(w_dw.shape, const),
            pl.BlockSpec((1, c_conv), const), pl.BlockSpec((1, c_conv), const), pl.BlockSpec((1, c_conv), const),
            pl.BlockSpec((1, dv), const),
        ],
        out_specs=[
            pl.BlockSpec((rows, c_conv), tile_out), pl.BlockSpec((rows, vv), tile_out),
            pl.BlockSpec((n_seq, HIST_PAD, c_conv), per_seq3),
            pl.BlockSpec((n_seq, heads, dk, dv), per_seq4),
        ],
        out_shape=[
            jax.ShapeDtypeStruct((bsz * t, c_conv), BF16), jax.ShapeDtypeStruct((bsz * t, vv), BF16),
            jax.ShapeDtypeStruct((bsz, HIST_PAD, c_conv), F32),
            jax.ShapeDtypeStruct((bsz, heads, dk, dv), F32),
        ],
        scratch_shapes=[
            pltpu.VMEM((rows, c_conv), F32), pltpu.VMEM((rows, qk), F32), pltpu.VMEM((rows, qk), F32),
            pltpu.VMEM((rows, vv), BF16), pltpu.VMEM((rows, vv), F32), pltpu.VMEM((rows, qk), F32),
            pltpu.VMEM((n_seq, seq_rows + HIST_PAD, c_conv), F32),
            pltpu.VMEM((7, span, LANES), F32),
            pltpu.VMEM((rows, c_conv), F32),
            pltpu.VMEM((n_seq, heads, dk, dv), F32),
        ],
        compiler_params=_params(("arbitrary", "arbitrary")),
        name="mixer",
    )(x2d, hist_pad, s0, g1, w_in, w_lr2, blr2, w_dw, b_dw, ln_g, ln_b, gn)


GLA_SUB = 16
GLA_CHUNK = 128
CONV_ROWS = 64


def _zero_fill_step(zero_ref, zbuf, zsem, n_chunks, per_step):
    step = pl.program_id(0) * pl.num_programs(1) + pl.program_id(1)
    last = pl.num_programs(0) * pl.num_programs(1) - 1
    rows = zbuf.shape[0]

    def copy(idx):
        return pltpu.make_async_copy(zbuf, zero_ref.at[pl.ds(pl.multiple_of(idx * rows, rows), rows)], zsem)

    @pl.when(step == 0)
    def _():
        zbuf[...] = jnp.zeros_like(zbuf)

    for p in range(per_step):
        earlier = (step - 1) * per_step + p

        @pl.when((step > 0) & (earlier < n_chunks))
        def _():
            copy(earlier).wait()

    for p in range(per_step):
        idx = step * per_step + p

        @pl.when(idx < n_chunks)
        def _():
            copy(idx).start()

        @pl.when((step == last) & (idx < n_chunks))
        def _():
            copy(idx).wait()


def _front_kernel(x_ref, hist_ref, s0_ref, g1_ref, w_ref, wlr2_ref, blr2_ref,
                  wdw_ref, bdw_ref, lg_ref, lb_ref, gn_ref,
                  c_ref, o_ref, tail_ref, sout_ref, *rest,
                  n_seq, seq_rows, chunk, heads, dk, dv, c_conv, width, zero_chunks):
    i = pl.program_id(1)
    qk, vv = heads * dk, heads * dv
    if zero_chunks:
        zero_ref, win, shifted, cbuf, state, zbuf, zsem = rest
        _zero_fill_step(zero_ref, zbuf, zsem, *zero_chunks)
    else:
        win, shifted, cbuf, state = rest

    @pl.when(i == 0)
    def _():
        for s in range(n_seq):
            win[s, 0:HIST_PAD, :] = hist_ref[s]
        state[...] = s0_ref[...]

    @pl.when(i > 0)
    def _():
        for s in range(n_seq):
            win[s, 0:HIST_PAD, :] = win[s, seq_rows:seq_rows + HIST_PAD, :]

    h = _rms(x_ref[...], g1_ref[...]).astype(BF16)

    def mm(lo, n):
        return jnp.dot(h, w_ref[:, lo:lo + n], preferred_element_type=F32)

    u = mm(0, c_conv) * jax.nn.sigmoid(mm(c_conv, c_conv))
    lead = HIST_PAD - (width - 1)
    span = shifted.shape[1]
    conv_rows = min(seq_rows, CONV_ROWS)
    for s in range(n_seq):
        r0 = s * seq_rows
        win[s, HIST_PAD:HIST_PAD + seq_rows, :] = u[r0:r0 + seq_rows, :]
        for cb in range(c_conv // LANES):
            cs = slice(cb * LANES, (cb + 1) * LANES)
            for r in range(1, 8):
                shifted[r - 1] = win[s, r:r + span, cs]
            for t0 in range(0, seq_rows, conv_rows):
                acc = jnp.broadcast_to(bdw_ref[:, cs], (conv_rows, LANES))
                for j in range(width):
                    r, a8 = (lead + j) % 8, (lead + j) // 8 * 8
                    if r == 0:
                        tap = win[s, t0 + a8:t0 + a8 + conv_rows, cs]
                    else:
                        tap = shifted[r - 1, t0 + a8:t0 + a8 + conv_rows, :]
                    acc = acc + wdw_ref[j:j + 1, cs] * tap
                cbuf[r0 + t0:r0 + t0 + conv_rows, cs] = acc
    cv = cbuf[...]
    mu = jnp.mean(cv, axis=-1, keepdims=True)
    xc = cv - mu
    cn = xc * lax.rsqrt(jnp.mean(xc * xc, axis=-1, keepdims=True) + EPS) * lg_ref[...] + lb_ref[...]
    c_ref[...] = _silu(cn).astype(c_ref.dtype)

    off = 2 * c_conv
    q = mm(off, qk) * (dk ** -0.5)
    k = mm(off + qk, qk)
    v = mm(off + 2 * qk, vv).astype(BF16)
    g = mm(off + 2 * qk + vv, vv)
    lr = mm(off + 2 * qk + 2 * vv, wlr2_ref.shape[0])
    z = jnp.dot(lr, wlr2_ref[...], precision=HIGHEST, preferred_element_type=F32) + blr2_ref[...]
    la = _log_sigmoid(z) * (1.0 / GATE_TEMP)

    n_sub = chunk // GLA_SUB
    r = lax.broadcasted_iota(I32, (chunk, chunk), 0)
    c = lax.broadcasted_iota(I32, (chunk, chunk), 1)
    causal = c <= r
    local_sum = (causal & ((r // GLA_SUB) == (c // GLA_SUB))).astype(BF16)
    sub_rows = [slice(j * GLA_SUB, (j + 1) * GLA_SUB) for j in range(n_sub)]
    eye = lax.broadcasted_iota(I32, (dk, dk), 0) == lax.broadcasted_iota(I32, (dk, dk), 1)
    for s in range(n_seq):
        for t0 in range(0, seq_rows, chunk):
            rows = slice(s * seq_rows + t0, s * seq_rows + t0 + chunk)
            la_1 = la[rows, :].astype(BF16)
            rest = la[rows, :] - la_1.astype(F32)
            la_2 = rest.astype(BF16)
            la_3 = (rest - la_2.astype(F32)).astype(BF16)
            sums = jnp.dot(local_sum, jnp.concatenate([la_1, la_2, la_3], axis=1), preferred_element_type=F32)
            local = sums[:, 0:qk] + sums[:, qk:2 * qk] + sums[:, 2 * qk:3 * qk]
            bases = [jnp.zeros((1, qk), F32)]
            for j in range(1, n_sub):
                bases.append(bases[-1] + local[j * GLA_SUB - 1:j * GLA_SUB, :])
            base = jnp.concatenate([jnp.broadcast_to(bs, (GLA_SUB, qk)) for bs in bases], axis=0)
            b = base + local
            b_end = b[chunk - 1:chunk, :]
            q_c, k_c = q[rows, :], k[rows, :]
            q_loc = q_c * jnp.exp(local)
            k_loc = k_c * jnp.exp(-local)
            q_in = (q_c * jnp.exp(b)).astype(BF16)
            k_out = (k_c * jnp.exp(b_end - b)).astype(BF16)
            decay_row = jnp.exp(b_end)
            zero_rows = jnp.zeros((GLA_SUB, qk), F32)
            q_parts, k_parts = [], []
            for j in range(n_sub):
                q_parts.append(jnp.concatenate(
                    [q_loc[sub_rows[m], :] * jnp.exp(jnp.minimum(bases[m] - bases[j], 0.0)) for m in range(n_sub)],
                    axis=0).astype(BF16))
                k_parts.append(jnp.concatenate(
                    [k_loc[sub_rows[m], :] if m == j else zero_rows for m in range(n_sub)], axis=0).astype(BF16))
            for hd in range(heads):
                ks = slice(hd * dk, (hd + 1) * dk)
                vs = slice(hd * dv, (hd + 1) * dv)
                vh = v[rows, vs]
                q_cat = jnp.concatenate([p[:, ks] for p in q_parts], axis=1)
                k_cat = jnp.concatenate([p[:, ks] for p in k_parts], axis=1)
                att = lax.dot_general(q_cat, k_cat, (((1,), (1,)), ((), ())), preferred_element_type=F32)
                att = jnp.where(causal, att, 0.0).astype(BF16)
                s_h = state[s, hd]
                o = jnp.dot(att, vh, preferred_element_type=F32)
                o = o + jnp.dot(q_in[:, ks], s_h.astype(BF16), preferred_element_type=F32)
                decay_col = jnp.sum(jnp.where(eye, jnp.broadcast_to(decay_row[:, ks], (dk, dk)), 0.0),
                                    axis=1, keepdims=True)
                state[s, hd] = decay_col * s_h + lax.dot_general(k_out[:, ks], vh, (((0,), (0,)), ((), ())),
                                                                 preferred_element_type=F32)
                o = o * lax.rsqrt(jnp.mean(o * o, axis=-1, keepdims=True) + EPS) * gn_ref[...]
                o_ref[rows, vs] = (o * _silu(g[rows, vs])).astype(o_ref.dtype)

    @pl.when(i == pl.num_programs(1) - 1)
    def _():
        for s in range(n_seq):
            tail_ref[s] = win[s, seq_rows:seq_rows + HIST_PAD, :]
        sout_ref[...] = state[...]


def _front(x2d, hist_pad, s0, g1, w_in, w_lr2, blr2, w_dw, b_dw, ln_g, ln_b, gn, *, n_seq, seq_rows,
           zero_shape=None):
    bsz, heads, dk, dv = s0.shape
    d = x2d.shape[1]
    t = x2d.shape[0] // bsz
    c_conv = w_dw.shape[1]
    width = w_dw.shape[0]
    qk, vv = heads * dk, heads * dv
    assert (n_seq == 1 and t % seq_rows == 0) or (seq_rows == t and bsz % n_seq == 0)
    nt = t // seq_rows
    rows = n_seq * seq_rows
    chunk = GLA_CHUNK if seq_rows % GLA_CHUNK == 0 else seq_rows
    assert chunk % GLA_SUB == 0 and seq_rows % min(seq_rows, CONV_ROWS) == 0
    const = lambda b, i: (0, 0)
    tile = lambda b, i: (b * nt + i, 0)
    per_seq3 = lambda b, i: (b, 0, 0)
    per_seq4 = lambda b, i: (b, 0, 0, 0)
    extra_out_specs, extra_out_shape, extra_scratch, zero_chunks = [], [], [], None
    if zero_shape is not None:
        assert zero_shape[0] % EXPERT_ROWS == 0
        n_chunks = zero_shape[0] // EXPERT_ROWS
        zero_chunks = (n_chunks, -(-n_chunks // ((bsz // n_seq) * nt)))
        extra_out_specs = [pl.BlockSpec(memory_space=pl.ANY)]
        extra_out_shape = [jax.ShapeDtypeStruct(zero_shape, U32)]
        extra_scratch = [pltpu.VMEM((EXPERT_ROWS, zero_shape[1]), U32), pltpu.SemaphoreType.DMA(())]
    return pl.pallas_call(
        functools.partial(_front_kernel, n_seq=n_seq, seq_rows=seq_rows, chunk=chunk, heads=heads, dk=dk, dv=dv,
                          c_conv=c_conv, width=width, zero_chunks=zero_chunks),
        grid=(bsz // n_seq, nt),
        in_specs=[
            pl.BlockSpec((rows, d), tile),
            pl.BlockSpec((n_seq, HIST_PAD, c_conv), per_seq3),
            pl.BlockSpec((n_seq, heads, dk, dv), per_seq4),
            pl.BlockSpec((1, d), const),
            pl.BlockSpec(w_in.shape, const, pipeline_mode=pl.Buffered(1)),
            pl.BlockSpec(w_lr2.shape, const),
            pl.BlockSpec((1, qk), const),
            pl.BlockSpec(w_dw.shape, const),
            pl.BlockSpec((1, c_conv), const), pl.BlockSpec((1, c_conv), const), pl.BlockSpec((1, c_conv), const),
            pl.BlockSpec((1, dv), const),
        ],
        out_specs=[
            pl.BlockSpec((rows, c_conv), tile), pl.BlockSpec((rows, vv), tile),
            pl.BlockSpec((n_seq, HIST_PAD, c_conv), per_seq3),
            pl.BlockSpec((n_seq, heads, dk, dv), per_seq4),
        ] + extra_out_specs,
        out_shape=[
            jax.ShapeDtypeStruct((bsz * t, c_conv), BF16), jax.ShapeDtypeStruct((bsz * t, vv), BF16),
            jax.ShapeDtypeStruct((bsz, HIST_PAD, c_conv), F32),
            jax.ShapeDtypeStruct((bsz, heads, dk, dv), F32),
        ] + extra_out_shape,
        scratch_shapes=[
            pltpu.VMEM((n_seq, seq_rows + HIST_PAD, c_conv), F32),
            pltpu.VMEM((7, seq_rows + HIST_PAD - 8, LANES), F32),
            pltpu.VMEM((rows, c_conv), F32),
            pltpu.VMEM((n_seq, heads, dk, dv), F32),
        ] + extra_scratch,
        compiler_params=_params(("arbitrary", "arbitrary")),
        name="front",
    )(x2d, hist_pad, s0, g1, w_in, w_lr2, blr2, w_dw, b_dw, ln_g, ln_b, gn)


def _outproj_kernel(xp_ref, xs_ref, cp_ref, cs_ref, op_ref, os_ref, w_ref, g2_ref, wr_ref, br_ref, zeroed_ref,
                    x1_ref, dest_ref, gates_ref, pages_ref, counts_ref, sorted_ref,
                    stage, dest_vmem, dest_smem, cnt_s, page_s, npage_s, table_s, scatter_sems, dest_sems,
                    *, n_first, c_conv, n_groups, per_group, trash_row):
    i = pl.program_id(0)
    n_steps = pl.num_programs(0)
    tm = x1_ref.shape[0]
    n_exp = n_groups * per_group
    slot = i % 2
    prev = 1 - slot

    def dest_copy(s):
        return pltpu.make_async_copy(dest_vmem.at[s], dest_smem.at[s], dest_sems.at[s])

    def scatter_wait(s):
        for _ in range(2):
            pltpu.make_async_copy(stage.at[s], sorted_ref.at[pl.ds(0, tm)], scatter_sems.at[s]).wait()

    @pl.when(i == 0)
    def _():
        cnt_s[...] = jnp.zeros_like(cnt_s)
        page_s[...] = jnp.zeros_like(page_s)
        npage_s[...] = jnp.zeros_like(npage_s)
        table_s[...] = jnp.zeros_like(table_s)
        stage[1] = jnp.zeros(stage.shape[1:], stage.dtype)

        def fill(t, carry):
            dest_smem[1, 0, t] = trash_row + t
            dest_smem[1, 1, t] = trash_row + tm + t
            return carry

        lax.fori_loop(0, tm, fill, 0)

    @pl.when(i >= 1)
    def _():
        dest_copy(prev).wait()
        scatter_wait(slot)

    for t in range(tm):
        _row_copy(stage.at[prev], t, sorted_ref, dest_smem[prev, 0, t], scatter_sems.at[prev]).start()
        _row_copy(stage.at[prev], t, sorted_ref, dest_smem[prev, 1, t], scatter_sems.at[prev]).start()

    x = _pick(i, n_first, xp_ref, xs_ref)
    cc = _pick(i, n_first, cp_ref, cs_ref)
    oo = _pick(i, n_first, op_ref, os_ref)
    del zeroed_ref
    mix = jnp.dot(jnp.concatenate([cc, oo], axis=1), w_ref[...], preferred_element_type=F32)
    x1 = x + mix
    x1_ref[...] = x1
    h2 = _rms(x1, g2_ref[...])
    stage[slot] = _pack_pairs(h2)
    h_hi = h2.astype(BF16)
    h_lo = (h2 - h_hi.astype(F32)).astype(BF16)
    parts = lax.dot_general(wr_ref[...], jnp.concatenate([h_hi, h_lo], axis=0), (((1,), (1,)), ((), ())),
                            preferred_element_type=F32)
    n_r = br_ref.shape[0]
    logits = (parts[0:n_r, 0:tm] + parts[0:n_r, tm:] + parts[n_r:, 0:tm] + parts[n_r:, tm:]) + br_ref[...]
    lc = logits[0:n_groups, :]
    mc = jnp.max(lc, axis=0, keepdims=True)
    p_group = 1.0 / jnp.sum(jnp.exp(lc - mc), axis=0, keepdims=True)
    rows_c = lax.broadcasted_iota(I32, (n_groups, tm), 0)
    g_idx = jnp.min(jnp.where(lc == mc, rows_c, n_groups), axis=0, keepdims=True)
    lf = logits[n_groups:n_groups + n_exp, :]
    rows_f = lax.broadcasted_iota(I32, (n_exp, tm), 0)
    in_group = (rows_f >= g_idx * per_group) & (rows_f < (g_idx + 1) * per_group)
    neg = jnp.float32(-jnp.inf)
    l1 = jnp.where(in_group, lf, neg)
    m1 = jnp.max(l1, axis=0, keepdims=True)
    e1 = jnp.min(jnp.where(l1 == m1, rows_f, n_exp), axis=0, keepdims=True)
    l2 = jnp.where(rows_f == e1, neg, l1)
    m2 = jnp.max(l2, axis=0, keepdims=True)
    e2 = jnp.min(jnp.where(l2 == m2, rows_f, n_exp), axis=0, keepdims=True)
    r2 = jnp.exp(m2 - m1)
    w1 = 1.0 / (1.0 + r2)
    row8 = lax.broadcasted_iota(I32, (8, tm), 0)
    gates_ref[...] = jnp.where(row8 == 0, p_group * w1, jnp.where(row8 == 1, p_group * (r2 * w1), 0.0))

    oh0 = (rows_f == e1).astype(F32)
    oh1 = (rows_f == e2).astype(F32)
    both = oh0 + oh1
    tr = lax.broadcasted_iota(I32, (tm, tm), 0)
    tc = lax.broadcasted_iota(I32, (tm, tm), 1)
    earlier = jnp.dot(both.astype(BF16), (tr < tc).astype(BF16), preferred_element_type=F32)
    cnt = cnt_s[...]
    rank_base = earlier + cnt
    tile_cnt = jnp.sum(both, axis=1, keepdims=True)
    page_rows = float(EXPERT_ROWS)
    k0 = jnp.floor(cnt * (1.0 / page_rows))
    new_cnt = cnt + tile_cnt
    limit = (k0 + 1.0) * page_rows
    need_a = ((cnt == k0 * page_rows) & (tile_cnt > 0.0)).astype(F32)
    need_b = (new_cnt > limit).astype(F32)
    need = need_a + need_b
    er = lax.broadcasted_iota(I32, (n_exp, n_exp), 0)
    ec = lax.broadcasted_iota(I32, (n_exp, n_exp), 1)
    before = jnp.dot((ec < er).astype(BF16), jnp.broadcast_to(need, (n_exp, LANES)).astype(BF16),
                     preferred_element_type=F32)[:, 0:1]
    base = npage_s[...] + before
    page_a = jnp.where(need_a > 0.0, base, page_s[...])
    page_b = base + need_a
    npage_s[...] = npage_s[...] + jnp.sum(need, axis=0, keepdims=True)
    lane = lax.broadcasted_iota(I32, table_s.shape, 1).astype(F32)
    table = jnp.where((lane == k0) & (need_a > 0.0), page_a, table_s[...])
    table_s[...] = jnp.where((lane == k0 + 1.0) & (need_b > 0.0), page_b, table)
    cnt_s[...] = new_cnt
    page_s[...] = jnp.where(jnp.floor(new_cnt * (1.0 / page_rows)) == k0, page_a, page_b)

    def dest_rows(oh):
        rank = jnp.sum(oh * rank_base, axis=0, keepdims=True)
        lim = jnp.sum(oh * limit, axis=0, keepdims=True)
        pa = jnp.sum(oh * page_a, axis=0, keepdims=True)
        pb = jnp.sum(oh * page_b, axis=0, keepdims=True)
        within = rank - jnp.floor(rank * (1.0 / page_rows)) * page_rows
        return jnp.where(rank < lim, pa, pb) * page_rows + within

    dest = jnp.where(row8 == 0, dest_rows(oh0), jnp.where(row8 == 1, dest_rows(oh1), 0.0)).astype(I32)
    dest_ref[...] = dest
    dest_vmem[slot] = dest
    dest_copy(slot).start()

    @pl.when(i == n_steps - 1)
    def _():
        pages_ref[...] = table_s[...].astype(I32)
        counts_ref[...] = jnp.broadcast_to(cnt_s[...], counts_ref.shape).astype(I32)
        dest_copy(slot).wait()
        scatter_wait(prev)

        def last(j, carry):
            for r in range(DMA_UNROLL):
                t = j * DMA_UNROLL + r
                _row_copy(stage.at[slot], t, sorted_ref, dest_smem[slot, 0, t], scatter_sems.at[slot]).start()
                _row_copy(stage.at[slot], t, sorted_ref, dest_smem[slot, 1, t], scatter_sems.at[slot]).start()
            return carry

        lax.fori_loop(0, tm // DMA_UNROLL, last, 0)
        scatter_wait(slot)


def _outproj(xp, xs, cp, cs, op, os_, w_out, g2, wr, br, zeroed, *, n_groups, per_group, n_pages):
    n_p, d = xp.shape
    n_all = n_p + xs.shape[0]
    n_first = n_p // ROW_TILE
    n_exp = n_groups * per_group
    c_conv = cp.shape[1]
    vv = op.shape[1]
    tile = (d // 2,)
    first, second = _split_maps(n_first)
    const = lambda i: (0, 0)
    row = lambda i: (i, 0)
    col = lambda i: (0, i)
    rows_sorted = zeroed.shape[0]
    assert rows_sorted >= n_pages * EXPERT_ROWS + 2 * ROW_TILE
    assert ROW_TILE <= EXPERT_ROWS, "a tile may open at most two pages per expert"
    return pl.pallas_call(
        functools.partial(_outproj_kernel, n_first=n_first, c_conv=c_conv, n_groups=n_groups, per_group=per_group,
                          trash_row=n_pages * EXPERT_ROWS),
        grid=(n_all // ROW_TILE,),
        in_specs=[
            pl.BlockSpec((ROW_TILE, d), first), pl.BlockSpec((ROW_TILE, d), second),
            pl.BlockSpec((ROW_TILE, c_conv), first), pl.BlockSpec((ROW_TILE, c_conv), second),
            pl.BlockSpec((ROW_TILE, vv), first), pl.BlockSpec((ROW_TILE, vv), second),
            pl.BlockSpec(w_out.shape, const, pipeline_mode=pl.Buffered(1)),
            pl.BlockSpec((1, d), const),
            pl.BlockSpec(wr.shape, const),
            pl.BlockSpec(br.shape, const),
            pl.BlockSpec(memory_space=pl.ANY),
        ],
        out_specs=[
            pl.BlockSpec((ROW_TILE, d), row),
            pl.BlockSpec((8, ROW_TILE), col), pl.BlockSpec((8, ROW_TILE), col),
            pl.BlockSpec((n_exp, LANES), const), pl.BlockSpec((n_exp, LANES), const),
            pl.BlockSpec(memory_space=pl.ANY),
        ],
        out_shape=[
            jax.ShapeDtypeStruct((n_all, d), F32),
            jax.ShapeDtypeStruct((8, n_all), I32), jax.ShapeDtypeStruct((8, n_all), F32),
            jax.ShapeDtypeStruct((n_exp, LANES), I32), jax.ShapeDtypeStruct((n_exp, LANES), I32),
            jax.ShapeDtypeStruct((rows_sorted,) + tile, U32),
        ],
        scratch_shapes=[
            pltpu.VMEM((2, ROW_TILE) + tile, U32),
            pltpu.VMEM((2, 8, ROW_TILE), I32), pltpu.SMEM((2, 8, ROW_TILE), I32),
            pltpu.VMEM((n_exp, 1), F32), pltpu.VMEM((n_exp, 1), F32), pltpu.VMEM((1, 1), F32),
            pltpu.VMEM((n_exp, LANES), F32),
            pltpu.SemaphoreType.DMA((2,)), pltpu.SemaphoreType.DMA((2,)),
        ],
        compiler_params=_params(("arbitrary",)),
        input_output_aliases={10: 5},
        name="outproj",
    )(xp, xs, cp, cs, op, os_, w_out, g2, wr, br, zeroed)


def _experts_kernel(cnt_ref, pages_ref, xs_ref, wg_ref, wu_ref, wd_ref, ysp_ref,
                    xbuf, ybuf, wg_f32, wu_f32, wd_f32, wg_bf, wu_bf, wd_bf, first_blk, page_seq,
                    gsems, ysems, wsems, *, n_exp, table_lanes):
    e = pl.program_id(0)
    tb = xbuf.shape[1]
    n_pages = page_seq.shape[0]

    def n_pages_of(ex):
        return (cnt_ref[ex] + (tb - 1)) // tb

    def page_rows(blk):
        return pl.ds(pl.multiple_of(page_seq[blk] * tb, tb), tb)

    def fetch(blk, slot):
        return pltpu.make_async_copy(xs_ref.at[page_rows(blk)], xbuf.at[slot], gsems.at[slot])

    def writeback(blk, slot):
        return pltpu.make_async_copy(ybuf.at[slot], ysp_ref.at[page_rows(blk)], ysems.at[slot])

    def weight_copies(ex, slot):
        return (pltpu.make_async_copy(wg_ref.at[ex], wg_f32.at[slot], wsems.at[slot]),
                pltpu.make_async_copy(wu_ref.at[ex], wu_f32.at[slot], wsems.at[slot]),
                pltpu.make_async_copy(wd_ref.at[ex], wd_f32.at[slot], wsems.at[slot]))

    @pl.when(e == 0)
    def _():
        for ahead in range(min(WEIGHT_SLOTS - 1, n_exp)):
            for cp in weight_copies(ahead, ahead):
                cp.start(priority=1)

        def per_expert(ex, blk):
            first_blk[ex] = blk

            def per_page(j, carry):
                page_seq[blk + j] = pages_ref[ex * table_lanes + j]
                return carry

            lax.fori_loop(0, n_pages_of(ex), per_page, 0)
            return blk + n_pages_of(ex)

        first_blk[n_exp] = lax.fori_loop(0, n_exp, per_expert, 0)
        fetch(0, 0).start()

    @pl.when(e + WEIGHT_SLOTS - 1 < n_exp)
    def _():
        for cp in weight_copies(e + WEIGHT_SLOTS - 1, (e + WEIGHT_SLOTS - 1) % WEIGHT_SLOTS):
            cp.start(priority=1)

    b_lo = first_blk[e]
    b_hi = first_blk[e + 1]
    n_total = first_blk[n_exp]
    wslot = e % WEIGHT_SLOTS
    for cp in weight_copies(e, wslot):
        cp.wait()
    wg_bf[...] = wg_f32[wslot].astype(BF16)
    wu_bf[...] = wu_f32[wslot].astype(BF16)
    wd_bf[...] = wd_f32[wslot].astype(BF16)

    def block(b, carry):
        slot = b % 2

        @pl.when(b >= 2)
        def _():
            writeback(b, slot).wait()

        fetch(b, slot).wait()
        fetch(jnp.minimum(b + 1, n_total - 1), 1 - slot).start()

        def swiglu(rows):
            hi, lo = _unpack_pairs(xbuf[slot, rows, :])
            x = jnp.concatenate([hi.astype(BF16), lo.astype(BF16)], axis=1)
            hg = jnp.dot(x, wg_bf[...], preferred_element_type=F32)
            hu = jnp.dot(x, wu_bf[...], preferred_element_type=F32)
            hb = (_silu(hg) * hu).astype(BF16)
            ybuf[slot, rows, :] = _pack_pairs(jnp.dot(hb, wd_bf[...], preferred_element_type=F32))

        valid = cnt_ref[e] - (b - b_lo) * tb
        half = tb // 2

        @pl.when(valid > half)
        def _():
            swiglu(slice(0, tb))

        @pl.when(valid <= half)
        def _():
            swiglu(slice(0, half))
            ybuf[slot, half:tb, :] = jnp.zeros((tb - half,) + ybuf.shape[2:], ybuf.dtype)

        writeback(b, slot).start()
        return carry

    lax.fori_loop(b_lo, b_hi, block, 0)

    @pl.when(e == n_exp - 1)
    def _():
        fetch(0, n_total % 2).wait()

        @pl.when(n_total >= 2)
        def _():
            writeback(0, n_total % 2).wait()

        writeback(0, (n_total + 1) % 2).wait()
        ybuf[0] = jnp.zeros(ybuf.shape[1:], ybuf.dtype)

        def spare(blk):
            return pltpu.make_async_copy(ybuf.at[0], ysp_ref.at[pl.ds(pl.multiple_of(blk * tb, tb), tb)], ysems.at[0])

        def zero(blk, carry):
            spare(blk).start()
            return carry

        lax.fori_loop(n_total, n_pages, zero, 0)

        def zero_wait(blk, carry):
            spare(0).wait()
            return carry

        lax.fori_loop(n_total, n_pages, zero_wait, 0)


def _experts(counts, pages_flat, xs_sorted, w_gate, w_up, w_down, *, n_pages, table_lanes):
    tile = xs_sorted.shape[1:]
    n_exp, d, ff = w_gate.shape
    anyspec = pl.BlockSpec(memory_space=pl.ANY)
    grid_spec = pltpu.PrefetchScalarGridSpec(
        num_scalar_prefetch=2,
        grid=(n_exp,),
        in_specs=[anyspec, anyspec, anyspec, anyspec],
        out_specs=anyspec,
        scratch_shapes=[
            pltpu.VMEM((2, EXPERT_ROWS) + tile, U32), pltpu.VMEM((2, EXPERT_ROWS) + tile, U32),
            pltpu.VMEM((WEIGHT_SLOTS, d, ff), F32), pltpu.VMEM((WEIGHT_SLOTS, d, ff), F32),
            pltpu.VMEM((WEIGHT_SLOTS, ff, d), F32),
            pltpu.VMEM((d, ff), BF16), pltpu.VMEM((d, ff), BF16), pltpu.VMEM((ff, d), BF16),
            pltpu.SMEM((n_exp + 1,), I32), pltpu.SMEM((n_pages,), I32),
            pltpu.SemaphoreType.DMA((2,)), pltpu.SemaphoreType.DMA((2,)), pltpu.SemaphoreType.DMA((WEIGHT_SLOTS,)),
        ],
    )
    return pl.pallas_call(
        functools.partial(_experts_kernel, n_exp=n_exp, table_lanes=table_lanes),
        grid_spec=grid_spec,
        out_shape=jax.ShapeDtypeStruct((n_pages * EXPERT_ROWS,) + tile, U32),
        compiler_params=_params(("arbitrary",)),
        name="experts",
    )(counts, pages_flat, xs_sorted, w_gate, w_up, w_down)


def _combine_kernel(dest_ref, dest_next_ref, gates_ref, x1_ref, ysp_ref, gf_ref, yp_ref, ysmp_ref,
                    buf0, buf1, ybuf, sems, *, n_first):
    i = pl.program_id(0)
    n = pl.num_programs(0)
    tm = x1_ref.shape[0]
    slot = i % 2

    def gather(d_ref, s):
        def body(j, carry):
            for r in range(DMA_UNROLL):
                t = j * DMA_UNROLL + r
                _row_copy(ysp_ref, d_ref[0, t], buf0.at[s], t, sems.at[s]).start(priority=0)
                _row_copy(ysp_ref, d_ref[1, t], buf1.at[s], t, sems.at[s]).start(priority=1)
            return carry

        lax.fori_loop(0, tm // DMA_UNROLL, body, 0)

    def gather_wait(s):
        pltpu.make_async_copy(ysp_ref.at[pl.ds(0, tm)], buf0.at[s], sems.at[s]).wait()
        pltpu.make_async_copy(ysp_ref.at[pl.ds(0, tm)], buf1.at[s], sems.at[s]).wait()

    @pl.when(i == 0)
    def _():
        gather(dest_ref, 0)

    gather_wait(slot)
    for lo_row in range(0, tm, COMBINE_ROWS):
        rs = slice(lo_row, lo_row + COMBINE_ROWS)
        p0, p1 = buf0[slot, rs, :], buf1[slot, rs, :]
        xr, gr = x1_ref[rs, :], gates_ref[rs, :]
        for t in range(lo_row, lo_row + COMBINE_ROWS):
            _row_copy(ysp_ref, dest_next_ref[0, t], buf0.at[1 - slot], t, sems.at[1 - slot]).start(priority=0)
            _row_copy(ysp_ref, dest_next_ref[1, t], buf1.at[1 - slot], t, sems.at[1 - slot]).start(priority=1)
        hi0, lo0 = _unpack_pairs(p0)
        hi1, lo1 = _unpack_pairs(p1)
        g0, g1 = gr[:, 0:1], gr[:, 1:2]
        moe = jnp.concatenate([g0 * hi0 + g1 * hi1, g0 * lo0 + g1 * lo1], axis=1)
        ybuf[rs, :] = _rms(xr + moe, gf_ref[...])

    @pl.when(i < n_first)
    def _():
        yp_ref[...] = ybuf[...]

    @pl.when(i >= n_first)
    def _():
        ysmp_ref[...] = ybuf[...]

    @pl.when(i == n - 1)
    def _():
        gather_wait(1 - slot)


def _combine(dest, gates_t, x1, ysp, gf, *, n_p):
    n_all, d = x1.shape
    tile = ysp.shape[1:]
    n_first = n_p // ROW_TILE
    n_tiles = n_all // ROW_TILE
    first, second = _split_maps(n_first)
    return pl.pallas_call(
        functools.partial(_combine_kernel, n_first=n_first),
        grid=(n_tiles,),
        in_specs=[
            pl.BlockSpec((8, ROW_TILE), lambda i: (0, i), memory_space=pltpu.SMEM),
            pl.BlockSpec((8, ROW_TILE), lambda i: (0, jnp.minimum(i + 1, n_tiles - 1)), memory_space=pltpu.SMEM),
            pl.BlockSpec((ROW_TILE, 8), lambda i: (i, 0)),
            pl.BlockSpec((ROW_TILE, d), lambda i: (i, 0)),
            pl.BlockSpec(memory_space=pl.ANY),
            pl.BlockSpec((1, d), lambda i: (0, 0)),
        ],
        out_specs=[pl.BlockSpec((ROW_TILE, d), first), pl.BlockSpec((ROW_TILE, d), second)],
        out_shape=[jax.ShapeDtypeStruct((n_p, d), F32), jax.ShapeDtypeStruct((n_all - n_p, d), F32)],
        scratch_shapes=[pltpu.VMEM((2, ROW_TILE) + tile, U32), pltpu.VMEM((2, ROW_TILE) + tile, U32),
                        pltpu.VMEM((ROW_TILE, d), F32), pltpu.SemaphoreType.DMA((2,))],
        compiler_params=_params(("arbitrary",)),
        name="combine",
    )(dest, dest, gates_t, x1, ysp, gf)


def _chunk_for(t):
    return 64 if t % 64 == 0 else t


def kernel(x_prompt, x_sample, cache_conv, state_gla, norm1_g, w_in, w_lr2, b_lr2, w_dw, b_dw, conv_ln_g, conv_ln_b, gla_norm_g, w_out, norm2_g, w_router_coarse, b_router_coarse, w_router_fine, b_router_fine, w_exp_gate, w_exp_up, w_exp_down, norm_f_g):
    assert norm1_g.shape[0] == 1, "single trunk layer"
    bp, tp, d = x_prompt.shape
    bs, ts, _ = x_sample.shape
    heads, dk, dv = state_gla.shape[2:]
    c_conv = w_dw.shape[2]
    width = w_dw.shape[1]
    rank = w_lr2.shape[1]
    qk, vv = heads * dk, heads * dv
    n_groups, _, per_group = w_router_fine.shape[1:]
    n_exp = n_groups * per_group
    n_p, n_s = bp * tp, bs * ts
    n_s_pad = -(-n_s // ROW_TILE) * ROW_TILE
    pad_rows = lambda a: jnp.pad(a, ((0, n_s_pad - n_s), (0, 0)))
    n_all = n_p + n_s_pad
    assert n_p % ROW_TILE == 0 and width - 1 <= HIST_PAD

    xp = x_prompt.reshape(n_p, d)
    xs = pad_rows(x_sample.reshape(n_s, d))
    row = lambda a: a.reshape(1, -1)

    mixer_args = (row(norm1_g[0]), w_in[0].astype(BF16), w_lr2[0], row(b_lr2[0]),
                  w_dw[0], row(b_dw[0]), row(conv_ln_g[0]), row(conv_ln_b[0]), row(gla_norm_g[0]))
    hist_p = jnp.zeros((bp, HIST_PAD, c_conv), F32)
    hist_s = jnp.pad(cache_conv[0], ((0, 0), (HIST_PAD - (width - 1), 0), (0, 0)))
    s0_p = jnp.zeros((bp, heads, dk, dv), F32)
    n_pages = (2 * n_all) // EXPERT_ROWS + n_exp
    sorted_rows = n_pages * EXPERT_ROWS + -(-2 * ROW_TILE // EXPERT_ROWS) * EXPERT_ROWS
    c_p, o_p, tail_p, gla_p, zeroed = _front(xp, hist_p, s0_p, *mixer_args, n_seq=1, seq_rows=ROW_TILE,
                                             zero_shape=(sorted_rows, d // 2))
    c_s, o_s, tail_s, gla_s = _front(x_sample.reshape(n_s, d), hist_s, state_gla[0], *mixer_args,
                                     n_seq=bs, seq_rows=ts)
    c_s, o_s = pad_rows(c_s), pad_rows(o_s)

    wr = jnp.concatenate([w_router_coarse[0].T,
                          jnp.transpose(w_router_fine[0], (0, 2, 1)).reshape(n_exp, d)], axis=0)
    br = jnp.concatenate([b_router_coarse[0], b_router_fine[0].reshape(n_exp)])
    r_rows = -(-(n_groups + n_exp) // 8) * 8
    wr = jnp.pad(wr, ((0, r_rows - wr.shape[0]), (0, 0)))
    br = jnp.pad(br, (0, r_rows - br.shape[0])).reshape(r_rows, 1)
    wr_hi = wr.astype(BF16)
    wr = jnp.concatenate([wr_hi, (wr - wr_hi.astype(F32)).astype(BF16)], axis=0)
    n_pages = (2 * n_all) // EXPERT_ROWS + n_exp
    assert n_all // EXPERT_ROWS + 2 <= LANES, "page table row must hold one expert's pages"
    x1, dest, gates, pages, counts, xs_sorted = _outproj(
        xp, xs, c_p, c_s, o_p, o_s, w_out[0].astype(BF16), row(norm2_g[0]), wr, br, zeroed,
        n_groups=n_groups, per_group=per_group, n_pages=n_pages)
    ysp = _experts(counts[:, 0], pages.reshape(-1), xs_sorted, w_exp_gate[0], w_exp_up[0], w_exp_down[0],
                   n_pages=n_pages, table_lanes=LANES)
    y_p, y_s = _combine(dest, gates.T, x1, ysp, row(norm_f_g), n_p=n_p)

    lead = HIST_PAD - (width - 1)
    return (y_p.reshape(bp, tp, d), y_s[:n_s].reshape(bs, ts, d), tail_p[:, lead:][None], gla_p[None],
            tail_s[:, lead:][None], gla_s[None])
```

```python
import functools

import jax
import jax.numpy as jnp
from jax import lax
from jax.experimental import pallas as pl
from jax.experimental.pallas import tpu as pltpu

F32 = jnp.float32
BF16 = jnp.bfloat16
I32 = jnp.int32
U32 = jnp.uint32
EPS = 1e-6
GATE_TEMP = 16.0
HIGHEST = lax.Precision.HIGHEST

LANES = 128
ROW_TILE = 256
EXPERT_ROWS = 256
HIST_PAD = 32
DMA_UNROLL = 8
WEIGHT_SLOTS = 3
VMEM_LIMIT = 56 * 1024 * 1024


def _params(semantics, vmem=VMEM_LIMIT):
    return pltpu.CompilerParams(dimension_semantics=semantics, vmem_limit_bytes=vmem)


def _rms(x, g):
    return x * lax.rsqrt(jnp.mean(x * x, axis=-1, keepdims=True) + EPS) * g


def _silu(x):
    return x * jax.nn.sigmoid(x)


def _log_sigmoid(z):
    return jnp.minimum(z, 0.0) - jnp.log(1.0 + jnp.exp(-jnp.abs(z)))


def _pick(i, n_first, first_ref, second_ref):
    return jnp.where(i < n_first, first_ref[...], second_ref[...])


def _split_maps(n_first):
    first = lambda i: (jnp.minimum(i, n_first - 1), 0)
    second = lambda i: (jnp.maximum(i - n_first, 0), 0)
    return first, second


def _pack_pairs(x):
    half = x.shape[1] // 2
    hi = lax.bitcast_convert_type(x[:, :half].astype(BF16).astype(F32), U32)
    lo = lax.bitcast_convert_type(x[:, half:].astype(BF16).astype(F32), U32)
    return hi | (lo >> 16)


def _unpack_pairs(p):
    hi = lax.bitcast_convert_type(p & jnp.uint32(0xFFFF0000), F32)
    lo = lax.bitcast_convert_type(p << 16, F32)
    return hi, lo


def _row_copy(src, s, dst, d, sem):
    return pltpu.make_async_copy(src.at[pl.ds(s, 1)], dst.at[pl.ds(d, 1)], sem)


GLA_SUB = 16
GLA_CHUNK = 128
CONV_ROWS = 64


def _zero_fill_step(zero_ref, zbuf, zsem, n_chunks, per_step):
    step = pl.program_id(0) * pl.num_programs(1) + pl.program_id(1)
    last = pl.num_programs(0) * pl.num_programs(1) - 1
    rows = zbuf.shape[0]

    def copy(idx):
        return pltpu.make_async_copy(zbuf, zero_ref.at[pl.ds(pl.multiple_of(idx * rows, rows), rows)], zsem)

    @pl.when(step == 0)
    def _():
        zbuf[...] = jnp.zeros_like(zbuf)

    for p in range(per_step):
        earlier = (step - 1) * per_step + p

        @pl.when((step > 0) & (earlier < n_chunks))
        def _():
            copy(earlier).wait()

    for p in range(per_step):
        idx = step * per_step + p

        @pl.when(idx < n_chunks)
        def _():
            copy(idx).start()

        @pl.when((step == last) & (idx < n_chunks))
        def _():
            copy(idx).wait()


def _front_kernel(x_ref, hist_ref, s0_ref, g1_ref, w_ref, wlr2_ref, blr2_ref,
                  wdw_ref, bdw_ref, lg_ref, lb_ref, gn_ref,
                  c_ref, o_ref, tail_ref, sout_ref, *rest,
                  n_seq, seq_rows, chunk, heads, dk, dv, c_conv, width, zero_chunks):
    i = pl.program_id(1)
    qk, vv = heads * dk, heads * dv
    if zero_chunks:
        zero_ref, win, shifted, cbuf, state, zbuf, zsem = rest
        _zero_fill_step(zero_ref, zbuf, zsem, *zero_chunks)
    else:
        win, shifted, cbuf, state = rest

    @pl.when(i == 0)
    def _():
        for s in range(n_seq):
            win[s, 0:HIST_PAD, :] = hist_ref[s]
        state[...] = s0_ref[...]

    @pl.when(i > 0)
    def _():
        for s in range(n_seq):
            win[s, 0:HIST_PAD, :] = win[s, seq_rows:seq_rows + HIST_PAD, :]

    h = _rms(x_ref[...], g1_ref[...]).astype(BF16)

    def mm(lo, n):
        return jnp.dot(h, w_ref[:, lo:lo + n], preferred_element_type=F32)

    u = mm(0, c_conv) * jax.nn.sigmoid(mm(c_conv, c_conv))
    lead = HIST_PAD - (width - 1)
    span = shifted.shape[1]
    conv_rows = min(seq_rows, CONV_ROWS)
    for s in range(n_seq):
        r0 = s * seq_rows
        win[s, HIST_PAD:HIST_PAD + seq_rows, :] = u[r0:r0 + seq_rows, :]
        for cb in range(c_conv // LANES):
            cs = slice(cb * LANES, (cb + 1) * LANES)
            for r in range(1, 8):
                shifted[r - 1] = win[s, r:r + span, cs]
            for t0 in range(0, seq_rows, conv_rows):
                acc = jnp.broadcast_to(bdw_ref[:, cs], (conv_rows, LANES))
                for j in range(width):
                    r, a8 = (lead + j) % 8, (lead + j) // 8 * 8
                    if r == 0:
                        tap = win[s, t0 + a8:t0 + a8 + conv_rows, cs]
                    else:
                        tap = shifted[r - 1, t0 + a8:t0 + a8 + conv_rows, :]
                    acc = acc + wdw_ref[j:j + 1, cs] * tap
                cbuf[r0 + t0:r0 + t0 + conv_rows, cs] = acc
    cv = cbuf[...]
    mu = jnp.mean(cv, axis=-1, keepdims=True)
    xc = cv - mu
    cn = xc * lax.rsqrt(jnp.mean(xc * xc, axis=-1, keepdims=True) + EPS) * lg_ref[...] + lb_ref[...]
    c_ref[...] = _silu(cn).astype(c_ref.dtype)

    off = 2 * c_conv
    q = mm(off, qk) * (dk ** -0.5)
    k = mm(off + qk, qk)
    v = mm(off + 2 * qk, vv).astype(BF16)
    g = mm(off + 2 * qk + vv, vv)
    lr = mm(off + 2 * qk + 2 * vv, wlr2_ref.shape[0])
    z = jnp.dot(lr, wlr2_ref[...], precision=HIGHEST, preferred_element_type=F32) + blr2_ref[...]
    la = _log_sigmoid(z) * (1.0 / GATE_TEMP)

    n_sub = chunk // GLA_SUB
    r = lax.broadcasted_iota(I32, (chunk, chunk), 0)
    c = lax.broadcasted_iota(I32, (chunk, chunk), 1)
    causal = c <= r
    local_sum = (causal & ((r // GLA_SUB) == (c // GLA_SUB))).astype(BF16)
    sub_rows = [slice(j * GLA_SUB, (j + 1) * GLA_SUB) for j in range(n_sub)]
    eye = lax.broadcasted_iota(I32, (dk, dk), 0) == lax.broadcasted_iota(I32, (dk, dk), 1)
    for s in range(n_seq):
        for t0 in range(0, seq_rows, chunk):
            rows = slice(s * seq_rows + t0, s * seq_rows + t0 + chunk)
            la_1 = la[rows, :].astype(BF16)
            resid = la[rows, :] - la_1.astype(F32)
            la_2 = resid.astype(BF16)
            la_3 = (resid - la_2.astype(F32)).astype(BF16)
            sums = jnp.dot(local_sum, jnp.concatenate([la_1, la_2, la_3], axis=1), preferred_element_type=F32)
            local = sums[:, 0:qk] + sums[:, qk:2 * qk] + sums[:, 2 * qk:3 * qk]
            bases = [jnp.zeros((1, qk), F32)]
            for j in range(1, n_sub):
                bases.append(bases[-1] + local[j * GLA_SUB - 1:j * GLA_SUB, :])
            base = jnp.concatenate([jnp.broadcast_to(bs, (GLA_SUB, qk)) for bs in bases], axis=0)
            b = base + local
            b_end = b[chunk - 1:chunk, :]
            q_c, k_c = q[rows, :], k[rows, :]
            q_loc = q_c * jnp.exp(local)
            k_loc = k_c * jnp.exp(-local)
            q_in = (q_c * jnp.exp(b)).astype(BF16)
            k_out = (k_c * jnp.exp(b_end - b)).astype(BF16)
            decay_row = jnp.exp(b_end)
            zero_rows = jnp.zeros((GLA_SUB, qk), F32)
            q_parts, k_parts = [], []
            for j in range(n_sub):
                q_parts.append(jnp.concatenate(
                    [q_loc[sub_rows[m], :] * jnp.exp(jnp.minimum(bases[m] - bases[j], 0.0)) for m in range(n_sub)],
                    axis=0).astype(BF16))
                k_parts.append(jnp.concatenate(
                    [k_loc[sub_rows[m], :] if m == j else zero_rows for m in range(n_sub)], axis=0).astype(BF16))
            for hd in range(heads):
                ks = slice(hd * dk, (hd + 1) * dk)
                vs = slice(hd * dv, (hd + 1) * dv)
                vh = v[rows, vs]
                q_cat = jnp.concatenate([p[:, ks] for p in q_parts], axis=1)
                k_cat = jnp.concatenate([p[:, ks] for p in k_parts], axis=1)
                att = lax.dot_general(q_cat, k_cat, (((1,), (1,)), ((), ())), preferred_element_type=F32)
                att = jnp.where(causal, att, 0.0).astype(BF16)
                s_h = state[s, hd]
                o = jnp.dot(att, vh, preferred_element_type=F32)
                o = o + jnp.dot(q_in[:, ks], s_h.astype(BF16), preferred_element_type=F32)
                decay_col = jnp.sum(jnp.where(eye, jnp.broadcast_to(decay_row[:, ks], (dk, dk)), 0.0),
                                    axis=1, keepdims=True)
                state[s, hd] = decay_col * s_h + lax.dot_general(k_out[:, ks], vh, (((0,), (0,)), ((), ())),
                                                                 preferred_element_type=F32)
                o = o * lax.rsqrt(jnp.mean(o * o, axis=-1, keepdims=True) + EPS) * gn_ref[...]
                o_ref[rows, vs] = (o * _silu(g[rows, vs])).astype(o_ref.dtype)

    @pl.when(i == pl.num_programs(1) - 1)
    def _():
        for s in range(n_seq):
            tail_ref[s] = win[s, seq_rows:seq_rows + HIST_PAD, :]
        sout_ref[...] = state[...]


def _front(x2d, hist_pad, s0, g1, w_in, w_lr2, blr2, w_dw, b_dw, ln_g, ln_b, gn, *, n_seq, seq_rows,
           zero_shape=None):
    bsz, heads, dk, dv = s0.shape
    d = x2d.shape[1]
    t = x2d.shape[0] // bsz
    c_conv = w_dw.shape[1]
    width = w_dw.shape[0]
    qk, vv = heads * dk, heads * dv
    assert (n_seq == 1 and t % seq_rows == 0) or (seq_rows == t and bsz % n_seq == 0)
    nt = t // seq_rows
    rows = n_seq * seq_rows
    chunk = GLA_CHUNK if seq_rows % GLA_CHUNK == 0 else seq_rows
    assert chunk % GLA_SUB == 0 and seq_rows % min(seq_rows, CONV_ROWS) == 0
    const = lambda b, i: (0, 0)
    tile = lambda b, i: (b * nt + i, 0)
    per_seq3 = lambda b, i: (b, 0, 0)
    per_seq4 = lambda b, i: (b, 0, 0, 0)
    extra_out_specs, extra_out_shape, extra_scratch, zero_chunks = [], [], [], None
    if zero_shape is not None:
        assert zero_shape[0] % EXPERT_ROWS == 0
        n_chunks = zero_shape[0] // EXPERT_ROWS
        zero_chunks = (n_chunks, -(-n_chunks // ((bsz // n_seq) * nt)))
        extra_out_specs = [pl.BlockSpec(memory_space=pl.ANY)]
        extra_out_shape = [jax.ShapeDtypeStruct(zero_shape, U32)]
        extra_scratch = [pltpu.VMEM((EXPERT_ROWS, zero_shape[1]), U32), pltpu.SemaphoreType.DMA(())]
    return pl.pallas_call(
        functools.partial(_front_kernel, n_seq=n_seq, seq_rows=seq_rows, chunk=chunk, heads=heads, dk=dk, dv=dv,
                          c_conv=c_conv, width=width, zero_chunks=zero_chunks),
        grid=(bsz // n_seq, nt),
        in_specs=[
            pl.BlockSpec((rows, d), tile),
            pl.BlockSpec((n_seq, HIST_PAD, c_conv), per_seq3),
            pl.BlockSpec((n_seq, heads, dk, dv), per_seq4),
            pl.BlockSpec((1, d), const),
            pl.BlockSpec(w_in.shape, const, pipeline_mode=pl.Buffered(1)),
            pl.BlockSpec(w_lr2.shape, const),
            pl.BlockSpec((1, qk), const),
            pl.BlockSpec(w_dw.shape, const),
            pl.BlockSpec((1, c_conv), const), pl.BlockSpec((1, c_conv), const), pl.BlockSpec((1, c_conv), const),
            pl.BlockSpec((1, dv), const),
        ],
        out_specs=[
            pl.BlockSpec((rows, c_conv), tile), pl.BlockSpec((rows, vv), tile),
            pl.BlockSpec((n_seq, HIST_PAD, c_conv), per_seq3),
            pl.BlockSpec((n_seq, heads, dk, dv), per_seq4),
        ] + extra_out_specs,
        out_shape=[
            jax.ShapeDtypeStruct((bsz * t, c_conv), BF16), jax.ShapeDtypeStruct((bsz * t, vv), BF16),
            jax.ShapeDtypeStruct((bsz, HIST_PAD, c_conv), F32),
            jax.ShapeDtypeStruct((bsz, heads, dk, dv), F32),
        ] + extra_out_shape,
        scratch_shapes=[
            pltpu.VMEM((n_seq, seq_rows + HIST_PAD, c_conv), F32),
            pltpu.VMEM((7, seq_rows + HIST_PAD - 8, LANES), F32),
            pltpu.VMEM((rows, c_conv), F32),
            pltpu.VMEM((n_seq, heads, dk, dv), F32),
        ] + extra_scratch,
        compiler_params=_params(("arbitrary", "arbitrary")),
        name="front",
    )(x2d, hist_pad, s0, g1, w_in, w_lr2, blr2, w_dw, b_dw, ln_g, ln_b, gn)


def _outproj_kernel(xp_ref, xs_ref, cp_ref, cs_ref, op_ref, os_ref, w_ref, g2_ref, wr_ref, br_ref, zeroed_ref,
                    x1_ref, dest_ref, gates_ref, pages_ref, counts_ref, sorted_ref,
                    stage, dest_vmem, dest_smem, cnt_s, page_s, npage_s, table_s, scatter_sems, dest_sems,
                    *, n_first, c_conv, n_groups, per_group, trash_row):
    i = pl.program_id(0)
    n_steps = pl.num_programs(0)
    tm = x1_ref.shape[0]
    n_exp = n_groups * per_group
    slot = i % 2
    prev = 1 - slot

    def dest_copy(s):
        return pltpu.make_async_copy(dest_vmem.at[s], dest_smem.at[s], dest_sems.at[s])

    def scatter_wait(s):
        for _ in range(2):
            pltpu.make_async_copy(stage.at[s], sorted_ref.at[pl.ds(0, tm)], scatter_sems.at[s]).wait()

    @pl.when(i == 0)
    def _():
        cnt_s[...] = jnp.zeros_like(cnt_s)
        page_s[...] = jnp.zeros_like(page_s)
        npage_s[...] = jnp.zeros_like(npage_s)
        table_s[...] = jnp.zeros_like(table_s)
        stage[1] = jnp.zeros(stage.shape[1:], stage.dtype)

        def fill(t, carry):
            dest_smem[1, 0, t] = trash_row + t
            dest_smem[1, 1, t] = trash_row + tm + t
            return carry

        lax.fori_loop(0, tm, fill, 0)

    @pl.when(i >= 1)
    def _():
        dest_copy(prev).wait()
        scatter_wait(slot)

    for t in range(tm):
        _row_copy(stage.at[prev], t, sorted_ref, dest_smem[prev, 0, t], scatter_sems.at[prev]).start()
        _row_copy(stage.at[prev], t, sorted_ref, dest_smem[prev, 1, t], scatter_sems.at[prev]).start()

    x = _pick(i, n_first, xp_ref, xs_ref)
    cc = _pick(i, n_first, cp_ref, cs_ref)
    oo = _pick(i, n_first, op_ref, os_ref)
    del zeroed_ref
    mix = jnp.dot(jnp.concatenate([cc, oo], axis=1), w_ref[...], preferred_element_type=F32)
    x1 = x + mix
    x1_ref[...] = x1
    h2 = _rms(x1, g2_ref[...])
    stage[slot] = _pack_pairs(h2)
    h_hi = h2.astype(BF16)
    h_lo = (h2 - h_hi.astype(F32)).astype(BF16)
    parts = lax.dot_general(wr_ref[...], jnp.concatenate([h_hi, h_lo], axis=0), (((1,), (1,)), ((), ())),
                            preferred_element_type=F32)
    n_r = br_ref.shape[0]
    logits = (parts[0:n_r, 0:tm] + parts[0:n_r, tm:] + parts[n_r:, 0:tm] + parts[n_r:, tm:]) + br_ref[...]
    lc = logits[0:n_groups, :]
    mc = jnp.max(lc, axis=0, keepdims=True)
    p_group = 1.0 / jnp.sum(jnp.exp(lc - mc), axis=0, keepdims=True)
    rows_c = lax.broadcasted_iota(I32, (n_groups, tm), 0)
    g_idx = jnp.min(jnp.where(lc == mc, rows_c, n_groups), axis=0, keepdims=True)
    lf = logits[n_groups:n_groups + n_exp, :]
    rows_f = lax.broadcasted_iota(I32, (n_exp, tm), 0)
    in_group = (rows_f >= g_idx * per_group) & (rows_f < (g_idx + 1) * per_group)
    neg = jnp.float32(-jnp.inf)
    l1 = jnp.where(in_group, lf, neg)
    m1 = jnp.max(l1, axis=0, keepdims=True)
    e1 = jnp.min(jnp.where(l1 == m1, rows_f, n_exp), axis=0, keepdims=True)
    l2 = jnp.where(rows_f == e1, neg, l1)
    m2 = jnp.max(l2, axis=0, keepdims=True)
    e2 = jnp.min(jnp.where(l2 == m2, rows_f, n_exp), axis=0, keepdims=True)
    r2 = jnp.exp(m2 - m1)
    w1 = 1.0 / (1.0 + r2)
    row8 = lax.broadcasted_iota(I32, (8, tm), 0)
    gates_ref[...] = jnp.where(row8 == 0, p_group * w1, jnp.where(row8 == 1, p_group * (r2 * w1), 0.0))

    oh0 = (rows_f == e1).astype(F32)
    oh1 = (rows_f == e2).astype(F32)
    both = oh0 + oh1
    tr = lax.broadcasted_iota(I32, (tm, tm), 0)
    tc = lax.broadcasted_iota(I32, (tm, tm), 1)
    earlier = jnp.dot(both.astype(BF16), (tr < tc).astype(BF16), preferred_element_type=F32)
    cnt = cnt_s[...]
    rank_base = earlier + cnt
    tile_cnt = jnp.sum(both, axis=1, keepdims=True)
    page_rows = float(EXPERT_ROWS)
    k0 = jnp.floor(cnt * (1.0 / page_rows))
    new_cnt = cnt + tile_cnt
    limit = (k0 + 1.0) * page_rows
    need_a = ((cnt == k0 * page_rows) & (tile_cnt > 0.0)).astype(F32)
    need_b = (new_cnt > limit).astype(F32)
    need = need_a + need_b
    er = lax.broadcasted_iota(I32, (n_exp, n_exp), 0)
    ec = lax.broadcasted_iota(I32, (n_exp, n_exp), 1)
    before = jnp.dot((ec < er).astype(BF16), jnp.broadcast_to(need, (n_exp, LANES)).astype(BF16),
                     preferred_element_type=F32)[:, 0:1]
    base = npage_s[...] + before
    page_a = jnp.where(need_a > 0.0, base, page_s[...])
    page_b = base + need_a
    npage_s[...] = npage_s[...] + jnp.sum(need, axis=0, keepdims=True)
    lane = lax.broadcasted_iota(I32, table_s.shape, 1).astype(F32)
    table = jnp.where((lane == k0) & (need_a > 0.0), page_a, table_s[...])
    table_s[...] = jnp.where((lane == k0 + 1.0) & (need_b > 0.0), page_b, table)
    cnt_s[...] = new_cnt
    page_s[...] = jnp.where(jnp.floor(new_cnt * (1.0 / page_rows)) == k0, page_a, page_b)

    def dest_rows(oh):
        rank = jnp.sum(oh * rank_base, axis=0, keepdims=True)
        lim = jnp.sum(oh * limit, axis=0, keepdims=True)
        pa = jnp.sum(oh * page_a, axis=0, keepdims=True)
        pb = jnp.sum(oh * page_b, axis=0, keepdims=True)
        within = rank - jnp.floor(rank * (1.0 / page_rows)) * page_rows
        return jnp.where(rank < lim, pa, pb) * page_rows + within

    dest = jnp.where(row8 == 0, dest_rows(oh0), jnp.where(row8 == 1, dest_rows(oh1), 0.0)).astype(I32)
    dest_ref[...] = dest
    dest_vmem[slot] = dest
    dest_copy(slot).start()

    @pl.when(i == n_steps - 1)
    def _():
        pages_ref[...] = table_s[...].astype(I32)
        counts_ref[...] = jnp.broadcast_to(cnt_s[...], counts_ref.shape).astype(I32)
        dest_copy(slot).wait()
        scatter_wait(prev)

        def last(j, carry):
            for r in range(DMA_UNROLL):
                t = j * DMA_UNROLL + r
                _row_copy(stage.at[slot], t, sorted_ref, dest_smem[slot, 0, t], scatter_sems.at[slot]).start()
                _row_copy(stage.at[slot], t, sorted_ref, dest_smem[slot, 1, t], scatter_sems.at[slot]).start()
            return carry

        lax.fori_loop(0, tm // DMA_UNROLL, last, 0)
        scatter_wait(slot)


def _outproj(xp, xs, cp, cs, op, os_, w_out, g2, wr, br, zeroed, *, n_groups, per_group, n_pages):
    n_p, d = xp.shape
    n_all = n_p + xs.shape[0]
    n_first = n_p // ROW_TILE
    n_exp = n_groups * per_group
    c_conv = cp.shape[1]
    vv = op.shape[1]
    tile = (d // 2,)
    first, second = _split_maps(n_first)
    const = lambda i: (0, 0)
    row = lambda i: (i, 0)
    col = lambda i: (0, i)
    rows_sorted = zeroed.shape[0]
    assert rows_sorted >= n_pages * EXPERT_ROWS + 2 * ROW_TILE
    assert ROW_TILE <= EXPERT_ROWS, "a tile may open at most two pages per expert"
    return pl.pallas_call(
        functools.partial(_outproj_kernel, n_first=n_first, c_conv=c_conv, n_groups=n_groups, per_group=per_group,
                          trash_row=n_pages * EXPERT_ROWS),
        grid=(n_all // ROW_TILE,),
        in_specs=[
            pl.BlockSpec((ROW_TILE, d), first), pl.BlockSpec((ROW_TILE, d), second),
            pl.BlockSpec((ROW_TILE, c_conv), first), pl.BlockSpec((ROW_TILE, c_conv), second),
            pl.BlockSpec((ROW_TILE, vv), first), pl.BlockSpec((ROW_TILE, vv), second),
            pl.BlockSpec(w_out.shape, const, pipeline_mode=pl.Buffered(1)),
            pl.BlockSpec((1, d), const),
            pl.BlockSpec(wr.shape, const),
            pl.BlockSpec(br.shape, const),
            pl.BlockSpec(memory_space=pl.ANY),
        ],
        out_specs=[
            pl.BlockSpec((ROW_TILE, d), row),
            pl.BlockSpec((8, ROW_TILE), col), pl.BlockSpec((8, ROW_TILE), col),
            pl.BlockSpec((n_exp, LANES), const), pl.BlockSpec((n_exp, LANES), const),
            pl.BlockSpec(memory_space=pl.ANY),
        ],
        out_shape=[
            jax.ShapeDtypeStruct((n_all, d), F32),
            jax.ShapeDtypeStruct((8, n_all), I32), jax.ShapeDtypeStruct((8, n_all), F32),
            jax.ShapeDtypeStruct((n_exp, LANES), I32), jax.ShapeDtypeStruct((n_exp, LANES), I32),
            jax.ShapeDtypeStruct((rows_sorted,) + tile, U32),
        ],
        scratch_shapes=[
            pltpu.VMEM((2, ROW_TILE) + tile, U32),
            pltpu.VMEM((2, 8, ROW_TILE), I32), pltpu.SMEM((2, 8, ROW_TILE), I32),
            pltpu.VMEM((n_exp, 1), F32), pltpu.VMEM((n_exp, 1), F32), pltpu.VMEM((1, 1), F32),
            pltpu.VMEM((n_exp, LANES), F32),
            pltpu.SemaphoreType.DMA((2,)), pltpu.SemaphoreType.DMA((2,)),
        ],
        compiler_params=_params(("arbitrary",)),
        input_output_aliases={10: 5},
        name="outproj",
    )(xp, xs, cp, cs, op, os_, w_out, g2, wr, br, zeroed)


def _experts_kernel(cnt_ref, pages_ref, xs_ref, wg_ref, wu_ref, wd_ref, ysp_ref,
                    xbuf, ybuf, wg_f32, wu_f32, wd_f32, wg_bf, wu_bf, wd_bf, first_blk, page_seq,
                    gsems, ysems, wsems, *, n_exp, table_lanes):
    e = pl.program_id(0)
    tb = xbuf.shape[1]
    n_pages = page_seq.shape[0]

    def n_pages_of(ex):
        return (cnt_ref[ex] + (tb - 1)) // tb

    def page_rows(blk):
        return pl.ds(pl.multiple_of(page_seq[blk] * tb, tb), tb)

    def fetch(blk, slot):
        return pltpu.make_async_copy(xs_ref.at[page_rows(blk)], xbuf.at[slot], gsems.at[slot])

    def writeback(blk, slot):
        return pltpu.make_async_copy(ybuf.at[slot], ysp_ref.at[page_rows(blk)], ysems.at[slot])

    def weight_copies(ex, slot):
        return (pltpu.make_async_copy(wg_ref.at[ex], wg_f32.at[slot], wsems.at[slot]),
                pltpu.make_async_copy(wu_ref.at[ex], wu_f32.at[slot], wsems.at[slot]),
                pltpu.make_async_copy(wd_ref.at[ex], wd_f32.at[slot], wsems.at[slot]))

    @pl.when(e == 0)
    def _():
        for ahead in range(min(WEIGHT_SLOTS - 1, n_exp)):
            for cp in weight_copies(ahead, ahead):
                cp.start(priority=1)

        def per_expert(ex, blk):
            first_blk[ex] = blk

            def per_page(j, carry):
                page_seq[blk + j] = pages_ref[ex * table_lanes + j]
                return carry

            lax.fori_loop(0, n_pages_of(ex), per_page, 0)
            return blk + n_pages_of(ex)

        first_blk[n_exp] = lax.fori_loop(0, n_exp, per_expert, 0)
        fetch(0, 0).start()

    @pl.when(e + WEIGHT_SLOTS - 1 < n_exp)
    def _():
        for cp in weight_copies(e + WEIGHT_SLOTS - 1, (e + WEIGHT_SLOTS - 1) % WEIGHT_SLOTS):
            cp.start(priority=1)

    b_lo = first_blk[e]
    b_hi = first_blk[e + 1]
    n_total = first_blk[n_exp]
    wslot = e % WEIGHT_SLOTS
    for cp in weight_copies(e, wslot):
        cp.wait()
    wg_bf[...] = wg_f32[wslot].astype(BF16)
    wu_bf[...] = wu_f32[wslot].astype(BF16)
    wd_bf[...] = wd_f32[wslot].astype(BF16)

    def block(b, carry):
        slot = b % 2

        @pl.when(b >= 2)
        def _():
            writeback(b, slot).wait()

        fetch(b, slot).wait()
        fetch(jnp.minimum(b + 1, n_total - 1), 1 - slot).start()

        def swiglu(rows):
            hi, lo = _unpack_pairs(xbuf[slot, rows, :])
            x = jnp.concatenate([hi.astype(BF16), lo.astype(BF16)], axis=1)
            hg = jnp.dot(x, wg_bf[...], preferred_element_type=F32)
            hu = jnp.dot(x, wu_bf[...], preferred_element_type=F32)
            hb = (_silu(hg) * hu).astype(BF16)
            ybuf[slot, rows, :] = _pack_pairs(jnp.dot(hb, wd_bf[...], preferred_element_type=F32))

        valid = cnt_ref[e] - (b - b_lo) * tb
        half = tb // 2

        @pl.when(valid > half)
        def _():
            swiglu(slice(0, tb))

        @pl.when(valid <= half)
        def _():
            swiglu(slice(0, half))
            ybuf[slot, half:tb, :] = jnp.zeros((tb - half,) + ybuf.shape[2:], ybuf.dtype)

        writeback(b, slot).start()
        return carry

    lax.fori_loop(b_lo, b_hi, block, 0)

    @pl.when(e == n_exp - 1)
    def _():
        fetch(0, n_total % 2).wait()

        @pl.when(n_total >= 2)
        def _():
            writeback(0, n_total % 2).wait()

        writeback(0, (n_total + 1) % 2).wait()
        ybuf[0] = jnp.zeros(ybuf.shape[1:], ybuf.dtype)

        def spare(blk):
            return pltpu.make_async_copy(ybuf.at[0], ysp_ref.at[pl.ds(pl.multiple_of(blk * tb, tb), tb)], ysems.at[0])

        def zero(blk, carry):
            spare(blk).start()
            return carry

        lax.fori_loop(n_total, n_pages, zero, 0)

        def zero_wait(blk, carry):
            spare(0).wait()
            return carry

        lax.fori_loop(n_total, n_pages, zero_wait, 0)


def _experts(counts, pages_flat, xs_sorted, w_gate, w_up, w_down, *, n_pages, table_lanes):
    tile = xs_sorted.shape[1:]
    n_exp, d, ff = w_gate.shape
    anyspec = pl.BlockSpec(memory_space=pl.ANY)
    grid_spec = pltpu.PrefetchScalarGridSpec(
        num_scalar_prefetch=2,
        grid=(n_exp,),
        in_specs=[anyspec, anyspec, anyspec, anyspec],
        out_specs=anyspec,
        scratch_shapes=[
            pltpu.VMEM((2, EXPERT_ROWS) + tile, U32), pltpu.VMEM((2, EXPERT_ROWS) + tile, U32),
            pltpu.VMEM((WEIGHT_SLOTS, d, ff), F32), pltpu.VMEM((WEIGHT_SLOTS, d, ff), F32),
            pltpu.VMEM((WEIGHT_SLOTS, ff, d), F32),
            pltpu.VMEM((d, ff), BF16), pltpu.VMEM((d, ff), BF16), pltpu.VMEM((ff, d), BF16),
            pltpu.SMEM((n_exp + 1,), I32), pltpu.SMEM((n_pages,), I32),
            pltpu.SemaphoreType.DMA((2,)), pltpu.SemaphoreType.DMA((2,)), pltpu.SemaphoreType.DMA((WEIGHT_SLOTS,)),
        ],
    )
    return pl.pallas_call(
        functools.partial(_experts_kernel, n_exp=n_exp, table_lanes=table_lanes),
        grid_spec=grid_spec,
        out_shape=jax.ShapeDtypeStruct((n_pages * EXPERT_ROWS,) + tile, U32),
        compiler_params=_params(("arbitrary",)),
        name="experts",
    )(counts, pages_flat, xs_sorted, w_gate, w_up, w_down)


def _combine_kernel(dest_ref, dest_next_ref, gates_ref, x1_ref, ysp_ref, gf_ref, yp_ref, ysmp_ref,
                    buf0, buf1, sems, *, n_first):
    i = pl.program_id(0)
    n = pl.num_programs(0)
    tm = x1_ref.shape[0]
    slot = i % 2

    def gather(d_ref, s):
        def body(j, carry):
            for r in range(DMA_UNROLL):
                t = j * DMA_UNROLL + r
                _row_copy(ysp_ref, d_ref[0, t], buf0.at[s], t, sems.at[s]).start(priority=0)
                _row_copy(ysp_ref, d_ref[1, t], buf1.at[s], t, sems.at[s]).start(priority=1)
            return carry

        lax.fori_loop(0, tm // DMA_UNROLL, body, 0)

    def gather_wait(s):
        pltpu.make_async_copy(ysp_ref.at[pl.ds(0, tm)], buf0.at[s], sems.at[s]).wait()
        pltpu.make_async_copy(ysp_ref.at[pl.ds(0, tm)], buf1.at[s], sems.at[s]).wait()

    @pl.when(i == 0)
    def _():
        gather(dest_ref, 0)

    for t in range(tm):
        _row_copy(ysp_ref, dest_next_ref[0, t], buf0.at[1 - slot], t, sems.at[1 - slot]).start(priority=0)
        _row_copy(ysp_ref, dest_next_ref[1, t], buf1.at[1 - slot], t, sems.at[1 - slot]).start(priority=1)

    gather_wait(slot)
    hi0, lo0 = _unpack_pairs(buf0[slot])
    hi1, lo1 = _unpack_pairs(buf1[slot])
    g0 = gates_ref[:, 0:1]
    g1 = gates_ref[:, 1:2]
    moe = jnp.concatenate([g0 * hi0 + g1 * hi1, g0 * lo0 + g1 * lo1], axis=1)
    y = _rms(x1_ref[...] + moe, gf_ref[...])

    @pl.when(i < n_first)
    def _():
        yp_ref[...] = y

    @pl.when(i >= n_first)
    def _():
        ysmp_ref[...] = y

    @pl.when(i == n - 1)
    def _():
        gather_wait(1 - slot)


def _combine(dest, gates_t, x1, ysp, gf, *, n_p):
    n_all, d = x1.shape
    tile = ysp.shape[1:]
    n_first = n_p // ROW_TILE
    n_tiles = n_all // ROW_TILE
    first, second = _split_maps(n_first)
    return pl.pallas_call(
        functools.partial(_combine_kernel, n_first=n_first),
        grid=(n_tiles,),
        in_specs=[
            pl.BlockSpec((8, ROW_TILE), lambda i: (0, i), memory_space=pltpu.SMEM),
            pl.BlockSpec((8, ROW_TILE), lambda i: (0, jnp.minimum(i + 1, n_tiles - 1)), memory_space=pltpu.SMEM),
            pl.BlockSpec((ROW_TILE, 8), lambda i: (i, 0)),
            pl.BlockSpec((ROW_TILE, d), lambda i: (i, 0)),
            pl.BlockSpec(memory_space=pl.ANY),
            pl.BlockSpec((1, d), lambda i: (0, 0)),
        ],
        out_specs=[pl.BlockSpec((ROW_TILE, d), first), pl.BlockSpec((ROW_TILE, d), second)],
        out_shape=[jax.ShapeDtypeStruct((n_p, d), F32), jax.ShapeDtypeStruct((n_all - n_p, d), F32)],
        scratch_shapes=[pltpu.VMEM((2, ROW_TILE) + tile, U32), pltpu.VMEM((2, ROW_TILE) + tile, U32),
                        pltpu.SemaphoreType.DMA((2,))],
        compiler_params=_params(("arbitrary",)),
        name="combine",
    )(dest, dest, gates_t, x1, ysp, gf)


def kernel(x_prompt, x_sample, cache_conv, state_gla, norm1_g, w_in, w_lr2, b_lr2, w_dw, b_dw, conv_ln_g, conv_ln_b, gla_norm_g, w_out, norm2_g, w_router_coarse, b_router_coarse, w_router_fine, b_router_fine, w_exp_gate, w_exp_up, w_exp_down, norm_f_g):
    assert norm1_g.shape[0] == 1, "single trunk layer"
    bp, tp, d = x_prompt.shape
    bs, ts, _ = x_sample.shape
    heads, dk, dv = state_gla.shape[2:]
    c_conv = w_dw.shape[2]
    width = w_dw.shape[1]
    qk, vv = heads * dk, heads * dv
    n_groups, _, per_group = w_router_fine.shape[1:]
    n_exp = n_groups * per_group
    n_p, n_s = bp * tp, bs * ts
    n_s_pad = -(-n_s // ROW_TILE) * ROW_TILE
    pad_rows = lambda a: jnp.pad(a, ((0, n_s_pad - n_s), (0, 0)))
    n_all = n_p + n_s_pad
    assert n_p % ROW_TILE == 0 and width - 1 <= HIST_PAD

    xp = x_prompt.reshape(n_p, d)
    xs = pad_rows(x_sample.reshape(n_s, d))
    row = lambda a: a.reshape(1, -1)

    mixer_args = (row(norm1_g[0]), w_in[0].astype(BF16), w_lr2[0], row(b_lr2[0]),
                  w_dw[0], row(b_dw[0]), row(conv_ln_g[0]), row(conv_ln_b[0]), row(gla_norm_g[0]))
    hist_p = jnp.zeros((bp, HIST_PAD, c_conv), F32)
    hist_s = jnp.pad(cache_conv[0], ((0, 0), (HIST_PAD - (width - 1), 0), (0, 0)))
    s0_p = jnp.zeros((bp, heads, dk, dv), F32)
    n_pages = (2 * n_all) // EXPERT_ROWS + n_exp
    sorted_rows = n_pages * EXPERT_ROWS + -(-2 * ROW_TILE // EXPERT_ROWS) * EXPERT_ROWS
    c_p, o_p, tail_p, gla_p, zeroed = _front(xp, hist_p, s0_p, *mixer_args, n_seq=1, seq_rows=ROW_TILE,
                                             zero_shape=(sorted_rows, d // 2))
    c_s, o_s, tail_s, gla_s = _front(x_sample.reshape(n_s, d), hist_s, state_gla[0], *mixer_args,
                                     n_seq=bs, seq_rows=ts)
    c_s, o_s = pad_rows(c_s), pad_rows(o_s)

    wr = jnp.concatenate([w_router_coarse[0].T,
                          jnp.transpose(w_router_fine[0], (0, 2, 1)).reshape(n_exp, d)], axis=0)
    br = jnp.concatenate([b_router_coarse[0], b_router_fine[0].reshape(n_exp)])
    r_rows = -(-(n_groups + n_exp) // 8) * 8
    wr = jnp.pad(wr, ((0, r_rows - wr.shape[0]), (0, 0)))
    br = jnp.pad(br, (0, r_rows - br.shape[0])).reshape(r_rows, 1)
    wr_hi = wr.astype(BF16)
    wr = jnp.concatenate([wr_hi, (wr - wr_hi.astype(F32)).astype(BF16)], axis=0)
    assert n_all // EXPERT_ROWS + 2 <= LANES, "page table row must hold one expert's pages"
    x1, dest, gates, pages, counts, xs_sorted = _outproj(
        xp, xs, c_p, c_s, o_p, o_s, w_out[0].astype(BF16), row(norm2_g[0]), wr, br, zeroed,
        n_groups=n_groups, per_group=per_group, n_pages=n_pages)
    ysp = _experts(counts[:, 0], pages.reshape(-1), xs_sorted, w_exp_gate[0], w_exp_up[0], w_exp_down[0],
                   n_pages=n_pages, table_lanes=LANES)
    y_p, y_s = _combine(dest, gates.T, x1, ysp, row(norm_f_g), n_p=n_p)

    lead = HIST_PAD - (width - 1)
    return (y_p.reshape(bp, tp, d), y_s[:n_s].reshape(bs, ts, d), tail_p[:, lead:][None], gla_p[None],
            tail_s[:, lead:][None], gla_s[None])
```

```python
import functools

import jax
import jax.numpy as jnp
from jax import lax
from jax.experimental import pallas as pl
from jax.experimental.pallas import tpu as pltpu

F32 = jnp.float32
BF16 = jnp.bfloat16
I32 = jnp.int32
U32 = jnp.uint32
EPS = 1e-6
GATE_TEMP = 16.0
HIGHEST = lax.Precision.HIGHEST

LANES = 128
ROW_TILE = 256
EXPERT_ROWS = 256
HIST_PAD = 32
DMA_UNROLL = 8
WEIGHT_SLOTS = 3
VMEM_LIMIT = 56 * 1024 * 1024


def _params(semantics, vmem=VMEM_LIMIT):
    return pltpu.CompilerParams(dimension_semantics=semantics, vmem_limit_bytes=vmem)


def _rms(x, g):
    return x * lax.rsqrt(jnp.mean(x * x, axis=-1, keepdims=True) + EPS) * g


def _silu(x):
    return x * jax.nn.sigmoid(x)


def _log_sigmoid(z):
    return jnp.minimum(z, 0.0) - jnp.log(1.0 + jnp.exp(-jnp.abs(z)))


def _pick(i, n_first, first_ref, second_ref):
    return jnp.where(i < n_first, first_ref[...], second_ref[...])


def _split_maps(n_first):
    first = lambda i: (jnp.minimum(i, n_first - 1), 0)
    second = lambda i: (jnp.maximum(i - n_first, 0), 0)
    return first, second


def _pack_pairs(x):
    half = x.shape[1] // 2
    hi = lax.bitcast_convert_type(x[:, :half].astype(BF16).astype(F32), U32)
    lo = lax.bitcast_convert_type(x[:, half:].astype(BF16).astype(F32), U32)
    return hi | (lo >> 16)


def _unpack_pairs(p):
    hi = lax.bitcast_convert_type(p & jnp.uint32(0xFFFF0000), F32)
    lo = lax.bitcast_convert_type(p << 16, F32)
    return hi, lo


def _row_copy(src, s, dst, d, sem):
    return pltpu.make_async_copy(src.at[pl.ds(s, 1)], dst.at[pl.ds(d, 1)], sem)


GLA_SUB = 16
GLA_CHUNK = 128
CONV_ROWS = 64


def _zero_fill_step(zero_ref, zbuf, zsem, n_chunks, per_step):
    step = pl.program_id(0) * pl.num_programs(1) + pl.program_id(1)
    last = pl.num_programs(0) * pl.num_programs(1) - 1
    rows = zbuf.shape[0]

    def copy(idx):
        return pltpu.make_async_copy(zbuf, zero_ref.at[pl.ds(pl.multiple_of(idx * rows, rows), rows)], zsem)

    @pl.when(step == 0)
    def _():
        zbuf[...] = jnp.zeros_like(zbuf)

    for p in range(per_step):
        earlier = (step - 1) * per_step + p

        @pl.when((step > 0) & (earlier < n_chunks))
        def _():
            copy(earlier).wait()

    for p in range(per_step):
        idx = step * per_step + p

        @pl.when(idx < n_chunks)
        def _():
            copy(idx).start()

        @pl.when((step == last) & (idx < n_chunks))
        def _():
            copy(idx).wait()


def _front_kernel(x_ref, hist_ref, s0_ref, g1_ref, w_ref, wlr2_ref, blr2_ref,
                  wdw_ref, bdw_ref, lg_ref, lb_ref, gn_ref,
                  c_ref, o_ref, tail_ref, sout_ref, *rest,
                  n_seq, seq_rows, chunk, heads, dk, dv, c_conv, width, zero_chunks):
    i = pl.program_id(1)
    qk, vv = heads * dk, heads * dv
    if zero_chunks:
        zero_ref, win, cbuf, state, zbuf, zsem = rest
        _zero_fill_step(zero_ref, zbuf, zsem, *zero_chunks)
    else:
        win, cbuf, state = rest

    @pl.when(i == 0)
    def _():
        for s in range(n_seq):
            win[s, 0:HIST_PAD, :] = hist_ref[s]
        state[...] = s0_ref[...]

    @pl.when(i > 0)
    def _():
        for s in range(n_seq):
            win[s, 0:HIST_PAD, :] = win[s, seq_rows:seq_rows + HIST_PAD, :]

    h = _rms(x_ref[...], g1_ref[...]).astype(BF16)

    def mm(lo, n):
        return jnp.dot(h, w_ref[:, lo:lo + n], preferred_element_type=F32)

    u = mm(0, c_conv) * jax.nn.sigmoid(mm(c_conv, c_conv))
    lead = HIST_PAD - (width - 1)
    span = seq_rows + HIST_PAD - 8
    conv_rows = min(seq_rows, CONV_ROWS)
    for s in range(n_seq):
        r0 = s * seq_rows
        win[s, HIST_PAD:HIST_PAD + seq_rows, :] = u[r0:r0 + seq_rows, :]
        for cb in range(c_conv // LANES):
            cs = slice(cb * LANES, (cb + 1) * LANES)
            slab = win[s, 0:span + 8, cs]
            rolled = [slab] + [pltpu.roll(slab, span + 8 - r, axis=0) for r in range(1, 8)]
            for t0 in range(0, seq_rows, conv_rows):
                acc = jnp.broadcast_to(bdw_ref[:, cs], (conv_rows, LANES))
                for j in range(width):
                    r, a8 = (lead + j) % 8, (lead + j) // 8 * 8
                    tap = rolled[r][t0 + a8:t0 + a8 + conv_rows, :]
                    acc = acc + wdw_ref[j:j + 1, cs] * tap
                cbuf[r0 + t0:r0 + t0 + conv_rows, cs] = acc
    cv = cbuf[...]
    mu = jnp.mean(cv, axis=-1, keepdims=True)
    xc = cv - mu
    cn = xc * lax.rsqrt(jnp.mean(xc * xc, axis=-1, keepdims=True) + EPS) * lg_ref[...] + lb_ref[...]
    c_ref[...] = _silu(cn).astype(c_ref.dtype)

    off = 2 * c_conv
    q = mm(off, qk) * (dk ** -0.5)
    k = mm(off + qk, qk)
    v = mm(off + 2 * qk, vv).astype(BF16)
    g = mm(off + 2 * qk + vv, vv)
    lr = mm(off + 2 * qk + 2 * vv, wlr2_ref.shape[0])
    z = jnp.dot(lr, wlr2_ref[...], precision=HIGHEST, preferred_element_type=F32) + blr2_ref[...]
    la = _log_sigmoid(z) * (1.0 / GATE_TEMP)

    n_sub = chunk // GLA_SUB
    r = lax.broadcasted_iota(I32, (chunk, chunk), 0)
    c = lax.broadcasted_iota(I32, (chunk, chunk), 1)
    causal = c <= r
    local_sum = (causal & ((r // GLA_SUB) == (c // GLA_SUB))).astype(BF16)
    sub_rows = [slice(j * GLA_SUB, (j + 1) * GLA_SUB) for j in range(n_sub)]
    eye = lax.broadcasted_iota(I32, (dk, dk), 0) == lax.broadcasted_iota(I32, (dk, dk), 1)
    for s in range(n_seq):
        for t0 in range(0, seq_rows, chunk):
            rows = slice(s * seq_rows + t0, s * seq_rows + t0 + chunk)
            la_1 = la[rows, :].astype(BF16)
            resid = la[rows, :] - la_1.astype(F32)
            la_2 = resid.astype(BF16)
            la_3 = (resid - la_2.astype(F32)).astype(BF16)
            sums = jnp.dot(local_sum, jnp.concatenate([la_1, la_2, la_3], axis=1), preferred_element_type=F32)
            local = sums[:, 0:qk] + sums[:, qk:2 * qk] + sums[:, 2 * qk:3 * qk]
            bases = [jnp.zeros((1, qk), F32)]
            for j in range(1, n_sub):
                bases.append(bases[-1] + local[j * GLA_SUB - 1:j * GLA_SUB, :])
            base = jnp.concatenate([jnp.broadcast_to(bs, (GLA_SUB, qk)) for bs in bases], axis=0)
            b = base + local
            b_end = b[chunk - 1:chunk, :]
            q_c, k_c = q[rows, :], k[rows, :]
            q_loc = q_c * jnp.exp(local)
            k_loc = k_c * jnp.exp(-local)
            q_in = (q_c * jnp.exp(b)).astype(BF16)
            k_out = (k_c * jnp.exp(b_end - b)).astype(BF16)
            decay_row = jnp.exp(b_end)
            zero_rows = jnp.zeros((GLA_SUB, qk), F32)
            q_parts, k_parts = [], []
            for j in range(n_sub):
                q_parts.append(jnp.concatenate(
                    [q_loc[sub_rows[m], :] * jnp.exp(jnp.minimum(bases[m] - bases[j], 0.0)) for m in range(n_sub)],
                    axis=0).astype(BF16))
                k_parts.append(jnp.concatenate(
                    [k_loc[sub_rows[m], :] if m == j else zero_rows for m in range(n_sub)], axis=0).astype(BF16))
            for hd in range(heads):
                ks = slice(hd * dk, (hd + 1) * dk)
                vs = slice(hd * dv, (hd + 1) * dv)
                vh = v[rows, vs]
                q_cat = jnp.concatenate([p[:, ks] for p in q_parts], axis=1)
                k_cat = jnp.concatenate([p[:, ks] for p in k_parts], axis=1)
                att = lax.dot_general(q_cat, k_cat, (((1,), (1,)), ((), ())), preferred_element_type=F32)
                att = jnp.where(causal, att, 0.0).astype(BF16)
                s_h = state[s, hd]
                o = jnp.dot(att, vh, preferred_element_type=F32)
                o = o + jnp.dot(q_in[:, ks], s_h.astype(BF16), preferred_element_type=F32)
                decay_col = jnp.sum(jnp.where(eye, jnp.broadcast_to(decay_row[:, ks], (dk, dk)), 0.0),
                                    axis=1, keepdims=True)
                state[s, hd] = decay_col * s_h + lax.dot_general(k_out[:, ks], vh, (((0,), (0,)), ((), ())),
                                                                 preferred_element_type=F32)
                o = o * lax.rsqrt(jnp.mean(o * o, axis=-1, keepdims=True) + EPS) * gn_ref[...]
                o_ref[rows, vs] = (o * _silu(g[rows, vs])).astype(o_ref.dtype)

    @pl.when(i == pl.num_programs(1) - 1)
    def _():
        for s in range(n_seq):
            tail_ref[s] = win[s, seq_rows:seq_rows + HIST_PAD, :]
        sout_ref[...] = state[...]


def _front(x2d, hist_pad, s0, g1, w_in, w_lr2, blr2, w_dw, b_dw, ln_g, ln_b, gn, *, n_seq, seq_rows,
           zero_shape=None):
    bsz, heads, dk, dv = s0.shape
    d = x2d.shape[1]
    t = x2d.shape[0] // bsz
    c_conv = w_dw.shape[1]
    width = w_dw.shape[0]
    qk, vv = heads * dk, heads * dv
    assert (n_seq == 1 and t % seq_rows == 0) or (seq_rows == t and bsz % n_seq == 0)
    nt = t // seq_rows
    rows = n_seq * seq_rows
    chunk = GLA_CHUNK if seq_rows % GLA_CHUNK == 0 else seq_rows
    assert chunk % GLA_SUB == 0 and seq_rows % min(seq_rows, CONV_ROWS) == 0
    const = lambda b, i: (0, 0)
    tile = lambda b, i: (b * nt + i, 0)
    per_seq3 = lambda b, i: (b, 0, 0)
    per_seq4 = lambda b, i: (b, 0, 0, 0)
    extra_out_specs, extra_out_shape, extra_scratch, zero_chunks = [], [], [], None
    if zero_shape is not None:
        assert zero_shape[0] % EXPERT_ROWS == 0
        n_chunks = zero_shape[0] // EXPERT_ROWS
        zero_chunks = (n_chunks, -(-n_chunks // ((bsz // n_seq) * nt)))
        extra_out_specs = [pl.BlockSpec(memory_space=pl.ANY)]
        extra_out_shape = [jax.ShapeDtypeStruct(zero_shape, U32)]
        extra_scratch = [pltpu.VMEM((EXPERT_ROWS, zero_shape[1]), U32), pltpu.SemaphoreType.DMA(())]
    return pl.pallas_call(
        functools.partial(_front_kernel, n_seq=n_seq, seq_rows=seq_rows, chunk=chunk, heads=heads, dk=dk, dv=dv,
                          c_conv=c_conv, width=width, zero_chunks=zero_chunks),
        grid=(bsz // n_seq, nt),
        in_specs=[
            pl.BlockSpec((rows, d), tile),
            pl.BlockSpec((n_seq, HIST_PAD, c_conv), per_seq3),
            pl.BlockSpec((n_seq, heads, dk, dv), per_seq4),
            pl.BlockSpec((1, d), const),
            pl.BlockSpec(w_in.shape, const, pipeline_mode=pl.Buffered(1)),
            pl.BlockSpec(w_lr2.shape, const),
            pl.BlockSpec((1, qk), const),
            pl.BlockSpec(w_dw.shape, const),
            pl.BlockSpec((1, c_conv), const), pl.BlockSpec((1, c_conv), const), pl.BlockSpec((1, c_conv), const),
            pl.BlockSpec((1, dv), const),
        ],
        out_specs=[
            pl.BlockSpec((rows, c_conv), tile), pl.BlockSpec((rows, vv), tile),
            pl.BlockSpec((n_seq, HIST_PAD, c_conv), per_seq3),
            pl.BlockSpec((n_seq, heads, dk, dv), per_seq4),
        ] + extra_out_specs,
        out_shape=[
            jax.ShapeDtypeStruct((bsz * t, c_conv), BF16), jax.ShapeDtypeStruct((bsz * t, vv), BF16),
            jax.ShapeDtypeStruct((bsz, HIST_PAD, c_conv), F32),
            jax.ShapeDtypeStruct((bsz, heads, dk, dv), F32),
        ] + extra_out_shape,
        scratch_shapes=[
            pltpu.VMEM((n_seq, seq_rows + HIST_PAD, c_conv), F32),
            pltpu.VMEM((rows, c_conv), F32),
            pltpu.VMEM((n_seq, heads, dk, dv), F32),
        ] + extra_scratch,
        compiler_params=_params(("arbitrary", "arbitrary")),
        name="front",
    )(x2d, hist_pad, s0, g1, w_in, w_lr2, blr2, w_dw, b_dw, ln_g, ln_b, gn)


def _outproj_kernel(xp_ref, xs_ref, cp_ref, cs_ref, op_ref, os_ref, w_ref, g2_ref, wr_ref, br_ref, zeroed_ref,
                    x1_ref, dest_ref, gates_ref, pages_ref, counts_ref, sorted_ref,
                    stage, dest_vmem, dest_smem, cnt_s, page_s, npage_s, table_s, scatter_sems, dest_sems,
                    *, n_first, c_conv, n_groups, per_group, trash_row):
    i = pl.program_id(0)
    n_steps = pl.num_programs(0)
    tm = x1_ref.shape[0]
    n_exp = n_groups * per_group
    slot = i % 2
    prev = 1 - slot

    def dest_copy(s):
        return pltpu.make_async_copy(dest_vmem.at[s], dest_smem.at[s], dest_sems.at[s])

    def scatter_wait(s):
        for _ in range(2):
            pltpu.make_async_copy(stage.at[s], sorted_ref.at[pl.ds(0, tm)], scatter_sems.at[s]).wait()

    @pl.when(i == 0)
    def _():
        cnt_s[...] = jnp.zeros_like(cnt_s)
        page_s[...] = jnp.zeros_like(page_s)
        npage_s[...] = jnp.zeros_like(npage_s)
        table_s[...] = jnp.zeros_like(table_s)
        stage[1] = jnp.zeros(stage.shape[1:], stage.dtype)

        def fill(t, carry):
            dest_smem[1, 0, t] = trash_row + t
            dest_smem[1, 1, t] = trash_row + tm + t
            return carry

        lax.fori_loop(0, tm, fill, 0)

    @pl.when(i >= 1)
    def _():
        dest_copy(prev).wait()
        scatter_wait(slot)

    for t in range(tm):
        _row_copy(stage.at[prev], t, sorted_ref, dest_smem[prev, 0, t], scatter_sems.at[prev]).start()
        _row_copy(stage.at[prev], t, sorted_ref, dest_smem[prev, 1, t], scatter_sems.at[prev]).start()

    x = _pick(i, n_first, xp_ref, xs_ref)
    cc = _pick(i, n_first, cp_ref, cs_ref)
    oo = _pick(i, n_first, op_ref, os_ref)
    del zeroed_ref
    mix = jnp.dot(jnp.concatenate([cc, oo], axis=1), w_ref[...], preferred_element_type=F32)
    x1 = x + mix
    x1_ref[...] = x1
    h2 = _rms(x1, g2_ref[...])
    stage[slot] = _pack_pairs(h2)
    h_hi = h2.astype(BF16)
    h_lo = (h2 - h_hi.astype(F32)).astype(BF16)
    parts = lax.dot_general(wr_ref[...], jnp.concatenate([h_hi, h_lo], axis=0), (((1,), (1,)), ((), ())),
                            preferred_element_type=F32)
    n_r = br_ref.shape[0]
    logits = (parts[0:n_r, 0:tm] + parts[0:n_r, tm:] + parts[n_r:, 0:tm] + parts[n_r:, tm:]) + br_ref[...]
    lc = logits[0:n_groups, :]
    mc = jnp.max(lc, axis=0, keepdims=True)
    p_group = 1.0 / jnp.sum(jnp.exp(lc - mc), axis=0, keepdims=True)
    rows_c = lax.broadcasted_iota(I32, (n_groups, tm), 0)
    g_idx = jnp.min(jnp.where(lc == mc, rows_c, n_groups), axis=0, keepdims=True)
    lf = logits[n_groups:n_groups + n_exp, :]
    rows_f = lax.broadcasted_iota(I32, (n_exp, tm), 0)
    in_group = (rows_f >= g_idx * per_group) & (rows_f < (g_idx + 1) * per_group)
    neg = jnp.float32(-jnp.inf)
    l1 = jnp.where(in_group, lf, neg)
    m1 = jnp.max(l1, axis=0, keepdims=True)
    e1 = jnp.min(jnp.where(l1 == m1, rows_f, n_exp), axis=0, keepdims=True)
    l2 = jnp.where(rows_f == e1, neg, l1)
    m2 = jnp.max(l2, axis=0, keepdims=True)
    e2 = jnp.min(jnp.where(l2 == m2, rows_f, n_exp), axis=0, keepdims=True)
    r2 = jnp.exp(m2 - m1)
    w1 = 1.0 / (1.0 + r2)
    row8 = lax.broadcasted_iota(I32, (8, tm), 0)
    gates_ref[...] = jnp.where(row8 == 0, p_group * w1, jnp.where(row8 == 1, p_group * (r2 * w1), 0.0))

    oh0 = (rows_f == e1).astype(F32)
    oh1 = (rows_f == e2).astype(F32)
    both = oh0 + oh1
    tr = lax.broadcasted_iota(I32, (tm, tm), 0)
    tc = lax.broadcasted_iota(I32, (tm, tm), 1)
    earlier = jnp.dot(both.astype(BF16), (tr < tc).astype(BF16), preferred_element_type=F32)
    cnt = cnt_s[...]
    rank_base = earlier + cnt
    tile_cnt = jnp.sum(both, axis=1, keepdims=True)
    page_rows = float(EXPERT_ROWS)
    k0 = jnp.floor(cnt * (1.0 / page_rows))
    new_cnt = cnt + tile_cnt
    limit = (k0 + 1.0) * page_rows
    need_a = ((cnt == k0 * page_rows) & (tile_cnt > 0.0)).astype(F32)
    need_b = (new_cnt > limit).astype(F32)
    need = need_a + need_b
    er = lax.broadcasted_iota(I32, (n_exp, n_exp), 0)
    ec = lax.broadcasted_iota(I32, (n_exp, n_exp), 1)
    before = jnp.dot((ec < er).astype(BF16), jnp.broadcast_to(need, (n_exp, LANES)).astype(BF16),
                     preferred_element_type=F32)[:, 0:1]
    base = npage_s[...] + before
    page_a = jnp.where(need_a > 0.0, base, page_s[...])
    page_b = base + need_a
    npage_s[...] = npage_s[...] + jnp.sum(need, axis=0, keepdims=True)
    lane = lax.broadcasted_iota(I32, table_s.shape, 1).astype(F32)
    table = jnp.where((lane == k0) & (need_a > 0.0), page_a, table_s[...])
    table_s[...] = jnp.where((lane == k0 + 1.0) & (need_b > 0.0), page_b, table)
    cnt_s[...] = new_cnt
    page_s[...] = jnp.where(jnp.floor(new_cnt * (1.0 / page_rows)) == k0, page_a, page_b)

    def dest_rows(oh):
        rank = jnp.sum(oh * rank_base, axis=0, keepdims=True)
        lim = jnp.sum(oh * limit, axis=0, keepdims=True)
        pa = jnp.sum(oh * page_a, axis=0, keepdims=True)
        pb = jnp.sum(oh * page_b, axis=0, keepdims=True)
        within = rank - jnp.floor(rank * (1.0 / page_rows)) * page_rows
        return jnp.where(rank < lim, pa, pb) * page_rows + within

    dest = jnp.where(row8 == 0, dest_rows(oh0), jnp.where(row8 == 1, dest_rows(oh1), 0.0)).astype(I32)
    dest_ref[...] = dest
    dest_vmem[slot] = dest
    dest_copy(slot).start()

    @pl.when(i == n_steps - 1)
    def _():
        pages_ref[...] = table_s[...].astype(I32)
        counts_ref[...] = jnp.broadcast_to(cnt_s[...], counts_ref.shape).astype(I32)
        dest_copy(slot).wait()
        scatter_wait(prev)

        def last(j, carry):
            for r in range(DMA_UNROLL):
                t = j * DMA_UNROLL + r
                _row_copy(stage.at[slot], t, sorted_ref, dest_smem[slot, 0, t], scatter_sems.at[slot]).start()
                _row_copy(stage.at[slot], t, sorted_ref, dest_smem[slot, 1, t], scatter_sems.at[slot]).start()
            return carry

        lax.fori_loop(0, tm // DMA_UNROLL, last, 0)
        scatter_wait(slot)


def _outproj(xp, xs, cp, cs, op, os_, w_out, g2, wr, br, zeroed, *, n_groups, per_group, n_pages):
    n_p, d = xp.shape
    n_all = n_p + xs.shape[0]
    n_first = n_p // ROW_TILE
    n_exp = n_groups * per_group
    c_conv = cp.shape[1]
    vv = op.shape[1]
    tile = (d // 2,)
    first, second = _split_maps(n_first)
    const = lambda i: (0, 0)
    row = lambda i: (i, 0)
    col = lambda i: (0, i)
    rows_sorted = zeroed.shape[0]
    assert rows_sorted >= n_pages * EXPERT_ROWS + 2 * ROW_TILE
    assert ROW_TILE <= EXPERT_ROWS, "a tile may open at most two pages per expert"
    return pl.pallas_call(
        functools.partial(_outproj_kernel, n_first=n_first, c_conv=c_conv, n_groups=n_groups, per_group=per_group,
                          trash_row=n_pages * EXPERT_ROWS),
        grid=(n_all // ROW_TILE,),
        in_specs=[
            pl.BlockSpec((ROW_TILE, d), first), pl.BlockSpec((ROW_TILE, d), second),
            pl.BlockSpec((ROW_TILE, c_conv), first), pl.BlockSpec((ROW_TILE, c_conv), second),
            pl.BlockSpec((ROW_TILE, vv), first), pl.BlockSpec((ROW_TILE, vv), second),
            pl.BlockSpec(w_out.shape, const, pipeline_mode=pl.Buffered(1)),
            pl.BlockSpec((1, d), const),
            pl.BlockSpec(wr.shape, const),
            pl.BlockSpec(br.shape, const),
            pl.BlockSpec(memory_space=pl.ANY),
        ],
        out_specs=[
            pl.BlockSpec((ROW_TILE, d), row),
            pl.BlockSpec((8, ROW_TILE), col), pl.BlockSpec((8, ROW_TILE), col),
            pl.BlockSpec((n_exp, LANES), const), pl.BlockSpec((n_exp, LANES), const),
            pl.BlockSpec(memory_space=pl.ANY),
        ],
        out_shape=[
            jax.ShapeDtypeStruct((n_all, d), F32),
            jax.ShapeDtypeStruct((8, n_all), I32), jax.ShapeDtypeStruct((8, n_all), F32),
            jax.ShapeDtypeStruct((n_exp, LANES), I32), jax.ShapeDtypeStruct((n_exp, LANES), I32),
            jax.ShapeDtypeStruct((rows_sorted,) + tile, U32),
        ],
        scratch_shapes=[
            pltpu.VMEM((2, ROW_TILE) + tile, U32),
            pltpu.VMEM((2, 8, ROW_TILE), I32), pltpu.SMEM((2, 8, ROW_TILE), I32),
            pltpu.VMEM((n_exp, 1), F32), pltpu.VMEM((n_exp, 1), F32), pltpu.VMEM((1, 1), F32),
            pltpu.VMEM((n_exp, LANES), F32),
            pltpu.SemaphoreType.DMA((2,)), pltpu.SemaphoreType.DMA((2,)),
        ],
        compiler_params=_params(("arbitrary",)),
        input_output_aliases={10: 5},
        name="outproj",
    )(xp, xs, cp, cs, op, os_, w_out, g2, wr, br, zeroed)


def _experts_kernel(cnt_ref, pages_ref, xs_ref, wg_ref, wu_ref, wd_ref, ysp_ref,
                    xbuf, ybuf, wg_f32, wu_f32, wd_f32, wg_bf, wu_bf, wd_bf, first_blk, page_seq,
                    gsems, ysems, wsems, *, n_exp, table_lanes):
    e = pl.program_id(0)
    tb = xbuf.shape[1]
    n_pages = page_seq.shape[0]

    def n_pages_of(ex):
        return (cnt_ref[ex] + (tb - 1)) // tb

    def page_rows(blk):
        return pl.ds(pl.multiple_of(page_seq[blk] * tb, tb), tb)

    def fetch(blk, slot):
        return pltpu.make_async_copy(xs_ref.at[page_rows(blk)], xbuf.at[slot], gsems.at[slot])

    def writeback(blk, slot):
        return pltpu.make_async_copy(ybuf.at[slot], ysp_ref.at[page_rows(blk)], ysems.at[slot])

    def weight_copies(ex, slot):
        return (pltpu.make_async_copy(wg_ref.at[ex], wg_f32.at[slot], wsems.at[slot]),
                pltpu.make_async_copy(wu_ref.at[ex], wu_f32.at[slot], wsems.at[slot]),
                pltpu.make_async_copy(wd_ref.at[ex], wd_f32.at[slot], wsems.at[slot]))

    @pl.when(e == 0)
    def _():
        for ahead in range(min(WEIGHT_SLOTS - 1, n_exp)):
            for cp in weight_copies(ahead, ahead):
                cp.start(priority=1)

        def per_expert(ex, blk):
            first_blk[ex] = blk

            def per_page(j, carry):
                page_seq[blk + j] = pages_ref[ex * table_lanes + j]
                return carry

            lax.fori_loop(0, n_pages_of(ex), per_page, 0)
            return blk + n_pages_of(ex)

        first_blk[n_exp] = lax.fori_loop(0, n_exp, per_expert, 0)
        fetch(0, 0).start()

    @pl.when(e + WEIGHT_SLOTS - 1 < n_exp)
    def _():
        for cp in weight_copies(e + WEIGHT_SLOTS - 1, (e + WEIGHT_SLOTS - 1) % WEIGHT_SLOTS):
            cp.start(priority=1)

    b_lo = first_blk[e]
    b_hi = first_blk[e + 1]
    n_total = first_blk[n_exp]
    wslot = e % WEIGHT_SLOTS
    for cp in weight_copies(e, wslot):
        cp.wait()
    wg_bf[...] = wg_f32[wslot].astype(BF16)
    wu_bf[...] = wu_f32[wslot].astype(BF16)
    wd_bf[...] = wd_f32[wslot].astype(BF16)

    def block(b, carry):
        slot = b % 2

        @pl.when(b >= 2)
        def _():
            writeback(b, slot).wait()

        fetch(b, slot).wait()
        fetch(jnp.minimum(b + 1, n_total - 1), 1 - slot).start()

        def swiglu(rows):
            hi, lo = _unpack_pairs(xbuf[slot, rows, :])
            x = jnp.concatenate([hi.astype(BF16), lo.astype(BF16)], axis=1)
            hg = jnp.dot(x, wg_bf[...], preferred_element_type=F32)
            hu = jnp.dot(x, wu_bf[...], preferred_element_type=F32)
            hb = (_silu(hg) * hu).astype(BF16)
            ybuf[slot, rows, :] = _pack_pairs(jnp.dot(hb, wd_bf[...], preferred_element_type=F32))

        valid = cnt_ref[e] - (b - b_lo) * tb
        half = tb // 2

        @pl.when(valid > half)
        def _():
            swiglu(slice(0, tb))

        @pl.when(valid <= half)
        def _():
            swiglu(slice(0, half))
            ybuf[slot, half:tb, :] = jnp.zeros((tb - half,) + ybuf.shape[2:], ybuf.dtype)

        writeback(b, slot).start()
        return carry

    lax.fori_loop(b_lo, b_hi, block, 0)

    @pl.when(e == n_exp - 1)
    def _():
        fetch(0, n_total % 2).wait()

        @pl.when(n_total >= 2)
        def _():
            writeback(0, n_total % 2).wait()

        writeback(0, (n_total + 1) % 2).wait()
        ybuf[0] = jnp.zeros(ybuf.shape[1:], ybuf.dtype)

        def spare(blk):
            return pltpu.make_async_copy(ybuf.at[0], ysp_ref.at[pl.ds(pl.multiple_of(blk * tb, tb), tb)], ysems.at[0])

        def zero(blk, carry):
            spare(blk).start()
            return carry

        lax.fori_loop(n_total, n_pages, zero, 0)

        def zero_wait(blk, carry):
            spare(0).wait()
            return carry

        lax.fori_loop(n_total, n_pages, zero_wait, 0)


def _experts(counts, pages_flat, xs_sorted, w_gate, w_up, w_down, *, n_pages, table_lanes):
    tile = xs_sorted.shape[1:]
    n_exp, d, ff = w_gate.shape
    anyspec = pl.BlockSpec(memory_space=pl.ANY)
    grid_spec = pltpu.PrefetchScalarGridSpec(
        num_scalar_prefetch=2,
        grid=(n_exp,),
        in_specs=[anyspec, anyspec, anyspec, anyspec],
        out_specs=anyspec,
        scratch_shapes=[
            pltpu.VMEM((2, EXPERT_ROWS) + tile, U32), pltpu.VMEM((2, EXPERT_ROWS) + tile, U32),
            pltpu.VMEM((WEIGHT_SLOTS, d, ff), F32), pltpu.VMEM((WEIGHT_SLOTS, d, ff), F32),
            pltpu.VMEM((WEIGHT_SLOTS, ff, d), F32),
            pltpu.VMEM((d, ff), BF16), pltpu.VMEM((d, ff), BF16), pltpu.VMEM((ff, d), BF16),
            pltpu.SMEM((n_exp + 1,), I32), pltpu.SMEM((n_pages,), I32),
            pltpu.SemaphoreType.DMA((2,)), pltpu.SemaphoreType.DMA((2,)), pltpu.SemaphoreType.DMA((WEIGHT_SLOTS,)),
        ],
    )
    return pl.pallas_call(
        functools.partial(_experts_kernel, n_exp=n_exp, table_lanes=table_lanes),
        grid_spec=grid_spec,
        out_shape=jax.ShapeDtypeStruct((n_pages * EXPERT_ROWS,) + tile, U32),
        compiler_params=_params(("arbitrary",)),
        name="experts",
    )(counts, pages_flat, xs_sorted, w_gate, w_up, w_down)


def _combine_kernel(dest_ref, dest_next_ref, gates_ref, x1_ref, ysp_ref, gf_ref, yp_ref, ysmp_ref,
                    buf0, buf1, sems, *, n_first):
    i = pl.program_id(0)
    n = pl.num_programs(0)
    tm = x1_ref.shape[0]
    slot = i % 2

    def gather(d_ref, s):
        def body(j, carry):
            for r in range(DMA_UNROLL):
                t = j * DMA_UNROLL + r
                _row_copy(ysp_ref, d_ref[0, t], buf0.at[s], t, sems.at[s]).start(priority=0)
                _row_copy(ysp_ref, d_ref[1, t], buf1.at[s], t, sems.at[s]).start(priority=1)
            return carry

        lax.fori_loop(0, tm // DMA_UNROLL, body, 0)

    def gather_wait(s):
        pltpu.make_async_copy(ysp_ref.at[pl.ds(0, tm)], buf0.at[s], sems.at[s]).wait()
        pltpu.make_async_copy(ysp_ref.at[pl.ds(0, tm)], buf1.at[s], sems.at[s]).wait()

    @pl.when(i == 0)
    def _():
        gather(dest_ref, 0)

    for t in range(tm):
        _row_copy(ysp_ref, dest_next_ref[0, t], buf0.at[1 - slot], t, sems.at[1 - slot]).start(priority=0)
        _row_copy(ysp_ref, dest_next_ref[1, t], buf1.at[1 - slot], t, sems.at[1 - slot]).start(priority=1)

    gather_wait(slot)
    hi0, lo0 = _unpack_pairs(buf0[slot])
    hi1, lo1 = _unpack_pairs(buf1[slot])
    g0 = gates_ref[:, 0:1]
    g1 = gates_ref[:, 1:2]
    moe = jnp.concatenate([g0 * hi0 + g1 * hi1, g0 * lo0 + g1 * lo1], axis=1)
    y = _rms(x1_ref[...] + moe, gf_ref[...])

    @pl.when(i < n_first)
    def _():
        yp_ref[...] = y

    @pl.when(i >= n_first)
    def _():
        ysmp_ref[...] = y

    @pl.when(i == n - 1)
    def _():
        gather_wait(1 - slot)


def _combine(dest, gates_t, x1, ysp, gf, *, n_p):
    n_all, d = x1.shape
    tile = ysp.shape[1:]
    n_first = n_p // ROW_TILE
    n_tiles = n_all // ROW_TILE
    first, second = _split_maps(n_first)
    return pl.pallas_call(
        functools.partial(_combine_kernel, n_first=n_first),
        grid=(n_tiles,),
        in_specs=[
            pl.BlockSpec((8, ROW_TILE), lambda i: (0, i), memory_space=pltpu.SMEM),
            pl.BlockSpec((8, ROW_TILE), lambda i: (0, jnp.minimum(i + 1, n_tiles - 1)), memory_space=pltpu.SMEM),
            pl.BlockSpec((ROW_TILE, 8), lambda i: (i, 0)),
            pl.BlockSpec((ROW_TILE, d), lambda i: (i, 0)),
            pl.BlockSpec(memory_space=pl.ANY),
            pl.BlockSpec((1, d), lambda i: (0, 0)),
        ],
        out_specs=[pl.BlockSpec((ROW_TILE, d), first), pl.BlockSpec((ROW_TILE, d), second)],
        out_shape=[jax.ShapeDtypeStruct((n_p, d), F32), jax.ShapeDtypeStruct((n_all - n_p, d), F32)],
        scratch_shapes=[pltpu.VMEM((2, ROW_TILE) + tile, U32), pltpu.VMEM((2, ROW_TILE) + tile, U32),
                        pltpu.SemaphoreType.DMA((2,))],
        compiler_params=_params(("arbitrary",)),
        name="combine",
    )(dest, dest, gates_t, x1, ysp, gf)


def kernel(x_prompt, x_sample, cache_conv, state_gla, norm1_g, w_in, w_lr2, b_lr2, w_dw, b_dw, conv_ln_g, conv_ln_b, gla_norm_g, w_out, norm2_g, w_router_coarse, b_router_coarse, w_router_fine, b_router_fine, w_exp_gate, w_exp_up, w_exp_down, norm_f_g):
    assert norm1_g.shape[0] == 1, "single trunk layer"
    bp, tp, d = x_prompt.shape
    bs, ts, _ = x_sample.shape
    heads, dk, dv = state_gla.shape[2:]
    c_conv = w_dw.shape[2]
    width = w_dw.shape[1]
    qk, vv = heads * dk, heads * dv
    n_groups, _, per_group = w_router_fine.shape[1:]
    n_exp = n_groups * per_group
    n_p, n_s = bp * tp, bs * ts
    n_s_pad = -(-n_s // ROW_TILE) * ROW_TILE
    pad_rows = lambda a: jnp.pad(a, ((0, n_s_pad - n_s), (0, 0)))
    n_all = n_p + n_s_pad
    assert n_p % ROW_TILE == 0 and width - 1 <= HIST_PAD

    xp = x_prompt.reshape(n_p, d)
    xs = pad_rows(x_sample.reshape(n_s, d))
    row = lambda a: a.reshape(1, -1)

    mixer_args = (row(norm1_g[0]), w_in[0].astype(BF16), w_lr2[0], row(b_lr2[0]),
                  w_dw[0], row(b_dw[0]), row(conv_ln_g[0]), row(conv_ln_b[0]), row(gla_norm_g[0]))
    hist_p = jnp.zeros((bp, HIST_PAD, c_conv), F32)
    hist_s = jnp.pad(cache_conv[0], ((0, 0), (HIST_PAD - (width - 1), 0), (0, 0)))
    s0_p = jnp.zeros((bp, heads, dk, dv), F32)
    n_pages = (2 * n_all) // EXPERT_ROWS + n_exp
    sorted_rows = n_pages * EXPERT_ROWS + -(-2 * ROW_TILE // EXPERT_ROWS) * EXPERT_ROWS
    c_p, o_p, tail_p, gla_p, zeroed = _front(xp, hist_p, s0_p, *mixer_args, n_seq=1, seq_rows=ROW_TILE,
                                             zero_shape=(sorted_rows, d // 2))
    c_s, o_s, tail_s, gla_s = _front(x_sample.reshape(n_s, d), hist_s, state_gla[0], *mixer_args,
                                     n_seq=bs, seq_rows=ts)
    c_s, o_s = pad_rows(c_s), pad_rows(o_s)

    wr = jnp.concatenate([w_router_coarse[0].T,
                          jnp.transpose(w_router_fine[0], (0, 2, 1)).reshape(n_exp, d)], axis=0)
    br = jnp.concatenate([b_router_coarse[0], b_router_fine[0].reshape(n_exp)])
    r_rows = -(-(n_groups + n_exp) // 8) * 8
    wr = jnp.pad(wr, ((0, r_rows - wr.shape[0]), (0, 0)))
    br = jnp.pad(br, (0, r_rows - br.shape[0])).reshape(r_rows, 1)
    wr_hi = wr.astype(BF16)
    wr = jnp.concatenate([wr_hi, (wr - wr_hi.astype(F32)).astype(BF16)], axis=0)
    assert n_all // EXPERT_ROWS + 2 <= LANES, "page table row must hold one expert's pages"
    x1, dest, gates, pages, counts, xs_sorted = _outproj(
        xp, xs, c_p, c_s, o_p, o_s, w_out[0].astype(BF16), row(norm2_g[0]), wr, br, zeroed,
        n_groups=n_groups, per_group=per_group, n_pages=n_pages)
    ysp = _experts(counts[:, 0], pages.reshape(-1), xs_sorted, w_exp_gate[0], w_exp_up[0], w_exp_down[0],
                   n_pages=n_pages, table_lanes=LANES)
    y_p, y_s = _combine(dest, gates.T, x1, ysp, row(norm_f_g), n_p=n_p)

    lead = HIST_PAD - (width - 1)
    return (y_p.reshape(bp, tp, d), y_s[:n_s].reshape(bs, ts, d), tail_p[:, lead:][None], gla_p[None],
            tail_s[:, lead:][None], gla_s[None])
```

```python
import functools

import jax
import jax.numpy as jnp
from jax import lax
from jax.experimental import pallas as pl
from jax.experimental.pallas import tpu as pltpu

F32 = jnp.float32
BF16 = jnp.bfloat16
I32 = jnp.int32
U32 = jnp.uint32
EPS = 1e-6
GATE_TEMP = 16.0
HIGHEST = lax.Precision.HIGHEST

LANES = 128
SUBLANES = 8
ROW_TILE = 256
EXPERT_ROWS = 256
HIST_PAD = 32
DMA_UNROLL = 8
PAGE_PARTS = 4
WEIGHT_SLOTS = 3
VMEM_LIMIT = 56 * 1024 * 1024


def _params(semantics, vmem=VMEM_LIMIT):
    return pltpu.CompilerParams(dimension_semantics=semantics, vmem_limit_bytes=vmem)


def _rms(x, g):
    return x * lax.rsqrt(jnp.mean(x * x, axis=-1, keepdims=True) + EPS) * g


def _silu(x):
    return x * jax.nn.sigmoid(x)


def _log_sigmoid(z):
    return jnp.minimum(z, 0.0) - jnp.log(1.0 + jnp.exp(-jnp.abs(z)))


def _pick(i, n_first, first_ref, second_ref):
    return jnp.where(i < n_first, first_ref[...], second_ref[...])


def _split_maps(n_first):
    first = lambda i: (jnp.minimum(i, n_first - 1), 0)
    second = lambda i: (jnp.maximum(i - n_first, 0), 0)
    return first, second


def _pack_pairs(x):
    half = x.shape[1] // 2
    hi = lax.bitcast_convert_type(x[:, :half].astype(BF16).astype(F32), U32)
    lo = lax.bitcast_convert_type(x[:, half:].astype(BF16).astype(F32), U32)
    return hi | (lo >> 16)


def _unpack_pairs(p):
    hi = lax.bitcast_convert_type(p & jnp.uint32(0xFFFF0000), F32)
    lo = lax.bitcast_convert_type(p << 16, F32)
    return hi, lo


def _row_copy(src, s, dst, d, sem):
    return pltpu.make_async_copy(src.at[pl.ds(s, 1)], dst.at[pl.ds(d, 1)], sem)


GLA_SUB = 16
GLA_CHUNK = 128
CONV_ROWS = 64


def _zero_fill_step(zero_ref, zbuf, zsem, n_chunks, per_step):
    step = pl.program_id(0) * pl.num_programs(1) + pl.program_id(1)
    last = pl.num_programs(0) * pl.num_programs(1) - 1
    rows = zbuf.shape[0]

    def copy(idx):
        return pltpu.make_async_copy(zbuf, zero_ref.at[pl.ds(pl.multiple_of(idx * rows, rows), rows)], zsem)

    @pl.when(step == 0)
    def _():
        zbuf[...] = jnp.zeros_like(zbuf)

    for p in range(per_step):
        earlier = (step - 1) * per_step + p

        @pl.when((step > 0) & (earlier < n_chunks))
        def _():
            copy(earlier).wait()

    for p in range(per_step):
        idx = step * per_step + p

        @pl.when(idx < n_chunks)
        def _():
            copy(idx).start()

        @pl.when((step == last) & (idx < n_chunks))
        def _():
            copy(idx).wait()


def _front_kernel(x_ref, hist_ref, s0_ref, g1_ref, w_ref, wlr2_ref, blr2_ref,
                  wdw_ref, bdw_ref, lg_ref, lb_ref, gn_ref,
                  c_ref, o_ref, tail_ref, sout_ref, *rest,
                  n_seq, seq_rows, chunk, heads, dk, dv, c_conv, width, zero_chunks):
    i = pl.program_id(1)
    qk, vv = heads * dk, heads * dv
    if zero_chunks:
        zero_ref, win, cbuf, state, zbuf, zsem = rest
        _zero_fill_step(zero_ref, zbuf, zsem, *zero_chunks)
    else:
        win, cbuf, state = rest

    @pl.when(i == 0)
    def _():
        for s in range(n_seq):
            win[s, 0:HIST_PAD, :] = hist_ref[s]
        state[...] = s0_ref[...]

    @pl.when(i > 0)
    def _():
        for s in range(n_seq):
            win[s, 0:HIST_PAD, :] = win[s, seq_rows:seq_rows + HIST_PAD, :]

    h = _rms(x_ref[...], g1_ref[...]).astype(BF16)

    def mm(lo, n):
        return jnp.dot(h, w_ref[:, lo:lo + n], preferred_element_type=F32)

    u = mm(0, c_conv) * jax.nn.sigmoid(mm(c_conv, c_conv))
    lead = HIST_PAD - (width - 1)
    win_rows = seq_rows + HIST_PAD
    conv_rows = min(seq_rows, CONV_ROWS)
    for s in range(n_seq):
        r0 = s * seq_rows
        win[s, HIST_PAD:HIST_PAD + seq_rows, :] = u[r0:r0 + seq_rows, :]
        for cb in range(c_conv // LANES):
            cs = slice(cb * LANES, (cb + 1) * LANES)
            slab = win[s, :, cs]
            rolled = [slab] + [pltpu.roll(slab, win_rows - r, axis=0) for r in range(1, SUBLANES)]
            for t0 in range(0, seq_rows, conv_rows):
                acc = jnp.broadcast_to(bdw_ref[:, cs], (conv_rows, LANES))
                for j in range(width):
                    r, a8 = (lead + j) % SUBLANES, (lead + j) // SUBLANES * SUBLANES
                    tap = rolled[r][t0 + a8:t0 + a8 + conv_rows, :]
                    acc = acc + wdw_ref[j:j + 1, cs] * tap
                cbuf[r0 + t0:r0 + t0 + conv_rows, cs] = acc
    cv = cbuf[...]
    mu = jnp.mean(cv, axis=-1, keepdims=True)
    xc = cv - mu
    cn = xc * lax.rsqrt(jnp.mean(xc * xc, axis=-1, keepdims=True) + EPS) * lg_ref[...] + lb_ref[...]
    c_ref[...] = _silu(cn).astype(c_ref.dtype)

    off = 2 * c_conv
    q = mm(off, qk) * (dk ** -0.5)
    k = mm(off + qk, qk)
    v = mm(off + 2 * qk, vv).astype(BF16)
    g = mm(off + 2 * qk + vv, vv)
    lr = mm(off + 2 * qk + 2 * vv, wlr2_ref.shape[0])
    z = jnp.dot(lr, wlr2_ref[...], precision=HIGHEST, preferred_element_type=F32) + blr2_ref[...]
    la = _log_sigmoid(z) * (1.0 / GATE_TEMP)

    n_sub = chunk // GLA_SUB
    r = lax.broadcasted_iota(I32, (chunk, chunk), 0)
    c = lax.broadcasted_iota(I32, (chunk, chunk), 1)
    causal = c <= r
    local_sum = (causal & ((r // GLA_SUB) == (c // GLA_SUB))).astype(BF16)
    sub_rows = [slice(j * GLA_SUB, (j + 1) * GLA_SUB) for j in range(n_sub)]
    eye = lax.broadcasted_iota(I32, (dk, dk), 0) == lax.broadcasted_iota(I32, (dk, dk), 1)
    for s in range(n_seq):
        for t0 in range(0, seq_rows, chunk):
            rows = slice(s * seq_rows + t0, s * seq_rows + t0 + chunk)
            la_1 = la[rows, :].astype(BF16)
            resid = la[rows, :] - la_1.astype(F32)
            la_2 = resid.astype(BF16)
            la_3 = (resid - la_2.astype(F32)).astype(BF16)
            sums = jnp.dot(local_sum, jnp.concatenate([la_1, la_2, la_3], axis=1), preferred_element_type=F32)
            local = sums[:, 0:qk] + sums[:, qk:2 * qk] + sums[:, 2 * qk:3 * qk]
            bases = [jnp.zeros((1, qk), F32)]
            for j in range(1, n_sub):
                bases.append(bases[-1] + local[j * GLA_SUB - 1:j * GLA_SUB, :])
            base = jnp.concatenate([jnp.broadcast_to(bs, (GLA_SUB, qk)) for bs in bases], axis=0)
            b = base + local
            b_end = b[chunk - 1:chunk, :]
            q_c, k_c = q[rows, :], k[rows, :]
            q_loc = q_c * jnp.exp(local)
            k_loc = k_c * jnp.exp(-local)
            q_in = (q_c * jnp.exp(b)).astype(BF16)
            k_out = (k_c * jnp.exp(b_end - b)).astype(BF16)
            decay_row = jnp.exp(b_end)
            zero_rows = jnp.zeros((GLA_SUB, qk), F32)
            q_parts, k_parts = [], []
            for j in range(n_sub):
                q_parts.append(jnp.concatenate(
                    [q_loc[sub_rows[m], :] * jnp.exp(jnp.minimum(bases[m] - bases[j], 0.0)) if m > j
                     else q_loc[sub_rows[m], :] for m in range(n_sub)], axis=0).astype(BF16))
                k_parts.append(jnp.concatenate(
                    [k_loc[sub_rows[m], :] if m == j else zero_rows for m in range(n_sub)], axis=0).astype(BF16))
            for hd in range(heads):
                ks = slice(hd * dk, (hd + 1) * dk)
                vs = slice(hd * dv, (hd + 1) * dv)
                vh = v[rows, vs]
                q_cat = jnp.concatenate([p[:, ks] for p in q_parts], axis=1)
                k_cat = jnp.concatenate([p[:, ks] for p in k_parts], axis=1)
                att = lax.dot_general(q_cat, k_cat, (((1,), (1,)), ((), ())), preferred_element_type=F32)
                att = jnp.where(causal, att, 0.0).astype(BF16)
                s_h = state[s, hd]
                o = jnp.dot(att, vh, preferred_element_type=F32)
                o = o + jnp.dot(q_in[:, ks], s_h.astype(BF16), preferred_element_type=F32)
                decay_col = jnp.sum(jnp.where(eye, jnp.broadcast_to(decay_row[:, ks], (dk, dk)), 0.0),
                                    axis=1, keepdims=True)
                state[s, hd] = decay_col * s_h + lax.dot_general(k_out[:, ks], vh, (((0,), (0,)), ((), ())),
                                                                 preferred_element_type=F32)
                o = o * lax.rsqrt(jnp.mean(o * o, axis=-1, keepdims=True) + EPS) * gn_ref[...]
                o_ref[rows, vs] = (o * _silu(g[rows, vs])).astype(o_ref.dtype)

    @pl.when(i == pl.num_programs(1) - 1)
    def _():
        for s in range(n_seq):
            tail_ref[s] = win[s, seq_rows:seq_rows + HIST_PAD, :]
        sout_ref[...] = state[...]


def _front(x2d, hist_pad, s0, g1, w_in, w_lr2, blr2, w_dw, b_dw, ln_g, ln_b, gn, *, n_seq, seq_rows,
           zero_shape=None):
    bsz, heads, dk, dv = s0.shape
    d = x2d.shape[1]
    t = x2d.shape[0] // bsz
    c_conv = w_dw.shape[1]
    width = w_dw.shape[0]
    qk, vv = heads * dk, heads * dv
    assert (n_seq == 1 and t % seq_rows == 0) or (seq_rows == t and bsz % n_seq == 0)
    nt = t // seq_rows
    rows = n_seq * seq_rows
    chunk = GLA_CHUNK if seq_rows % GLA_CHUNK == 0 else seq_rows
    assert chunk % GLA_SUB == 0 and seq_rows % min(seq_rows, CONV_ROWS) == 0
    const = lambda b, i: (0, 0)
    tile = lambda b, i: (b * nt + i, 0)
    per_seq3 = lambda b, i: (b, 0, 0)
    per_seq4 = lambda b, i: (b, 0, 0, 0)
    extra_out_specs, extra_out_shape, extra_scratch, zero_chunks = [], [], [], None
    if zero_shape is not None:
        assert zero_shape[0] % EXPERT_ROWS == 0
        n_chunks = zero_shape[0] // EXPERT_ROWS
        zero_chunks = (n_chunks, -(-n_chunks // ((bsz // n_seq) * nt)))
        extra_out_specs = [pl.BlockSpec(memory_space=pl.ANY)]
        extra_out_shape = [jax.ShapeDtypeStruct(zero_shape, U32)]
        extra_scratch = [pltpu.VMEM((EXPERT_ROWS, zero_shape[1]), U32), pltpu.SemaphoreType.DMA(())]
    return pl.pallas_call(
        functools.partial(_front_kernel, n_seq=n_seq, seq_rows=seq_rows, chunk=chunk, heads=heads, dk=dk, dv=dv,
                          c_conv=c_conv, width=width, zero_chunks=zero_chunks),
        grid=(bsz // n_seq, nt),
        in_specs=[
            pl.BlockSpec((rows, d), tile),
            pl.BlockSpec((n_seq, HIST_PAD, c_conv), per_seq3),
            pl.BlockSpec((n_seq, heads, dk, dv), per_seq4),
            pl.BlockSpec((1, d), const),
            pl.BlockSpec(w_in.shape, const, pipeline_mode=pl.Buffered(1)),
            pl.BlockSpec(w_lr2.shape, const),
            pl.BlockSpec((1, qk), const),
            pl.BlockSpec(w_dw.shape, const),
            pl.BlockSpec((1, c_conv), const), pl.BlockSpec((1, c_conv), const), pl.BlockSpec((1, c_conv), const),
            pl.BlockSpec((1, dv), const),
        ],
        out_specs=[
            pl.BlockSpec((rows, c_conv), tile), pl.BlockSpec((rows, vv), tile),
            pl.BlockSpec((n_seq, HIST_PAD, c_conv), per_seq3),
            pl.BlockSpec((n_seq, heads, dk, dv), per_seq4),
        ] + extra_out_specs,
        out_shape=[
            jax.ShapeDtypeStruct((bsz * t, c_conv), BF16), jax.ShapeDtypeStruct((bsz * t, vv), BF16),
            jax.ShapeDtypeStruct((bsz, HIST_PAD, c_conv), F32),
            jax.ShapeDtypeStruct((bsz, heads, dk, dv), F32),
        ] + extra_out_shape,
        scratch_shapes=[
            pltpu.VMEM((n_seq, seq_rows + HIST_PAD, c_conv), F32),
            pltpu.VMEM((rows, c_conv), F32),
            pltpu.VMEM((n_seq, heads, dk, dv), F32),
        ] + extra_scratch,
        compiler_params=_params(("arbitrary", "arbitrary")),
        name="front",
    )(x2d, hist_pad, s0, g1, w_in, w_lr2, blr2, w_dw, b_dw, ln_g, ln_b, gn)


def _outproj_kernel(xp_ref, xs_ref, cp_ref, cs_ref, op_ref, os_ref, w_ref, g2_ref, wr_ref, br_ref, zeroed_ref,
                    x1_ref, dest_ref, gates_ref, pages_ref, counts_ref, sorted_ref,
                    stage, dest_vmem, dest_smem, cnt_s, page_s, npage_s, table_s, scatter_sems, dest_sems,
                    *, n_first, c_conv, n_groups, per_group, trash_row):
    i = pl.program_id(0)
    n_steps = pl.num_programs(0)
    tm = x1_ref.shape[0]
    n_exp = n_groups * per_group
    slot = i % 2
    prev = 1 - slot

    def dest_copy(s):
        return pltpu.make_async_copy(dest_vmem.at[s], dest_smem.at[s], dest_sems.at[s])

    def scatter_wait(s):
        for _ in range(2):
            pltpu.make_async_copy(stage.at[s], sorted_ref.at[pl.ds(0, tm)], scatter_sems.at[s]).wait()

    @pl.when(i == 0)
    def _():
        cnt_s[...] = jnp.zeros_like(cnt_s)
        page_s[...] = jnp.zeros_like(page_s)
        npage_s[...] = jnp.zeros_like(npage_s)
        table_s[...] = jnp.zeros_like(table_s)
        stage[1] = jnp.zeros(stage.shape[1:], stage.dtype)

        def fill(t, carry):
            dest_smem[1, 0, t] = trash_row + t
            dest_smem[1, 1, t] = trash_row + tm + t
            return carry

        lax.fori_loop(0, tm, fill, 0)

    @pl.when(i >= 1)
    def _():
        dest_copy(prev).wait()
        scatter_wait(slot)

    for t in range(tm):
        _row_copy(stage.at[prev], t, sorted_ref, dest_smem[prev, 0, t], scatter_sems.at[prev]).start()
        _row_copy(stage.at[prev], t, sorted_ref, dest_smem[prev, 1, t], scatter_sems.at[prev]).start()

    x = _pick(i, n_first, xp_ref, xs_ref)
    cc = _pick(i, n_first, cp_ref, cs_ref)
    oo = _pick(i, n_first, op_ref, os_ref)
    del zeroed_ref
    mix = jnp.dot(jnp.concatenate([cc, oo], axis=1), w_ref[...], preferred_element_type=F32)
    x1 = x + mix
    x1_ref[...] = x1
    h2 = _rms(x1, g2_ref[...])
    stage[slot] = _pack_pairs(h2)
    h_hi = h2.astype(BF16)
    h_lo = (h2 - h_hi.astype(F32)).astype(BF16)
    parts = lax.dot_general(wr_ref[...], jnp.concatenate([h_hi, h_lo], axis=0), (((1,), (1,)), ((), ())),
                            preferred_element_type=F32)
    n_r = br_ref.shape[0]
    logits = (parts[0:n_r, 0:tm] + parts[0:n_r, tm:] + parts[n_r:, 0:tm] + parts[n_r:, tm:]) + br_ref[...]
    lc = logits[0:n_groups, :]
    mc = jnp.max(lc, axis=0, keepdims=True)
    p_group = 1.0 / jnp.sum(jnp.exp(lc - mc), axis=0, keepdims=True)
    rows_c = lax.broadcasted_iota(I32, (n_groups, tm), 0)
    g_idx = jnp.min(jnp.where(lc == mc, rows_c, n_groups), axis=0, keepdims=True)
    lf = logits[n_groups:n_groups + n_exp, :]
    rows_f = lax.broadcasted_iota(I32, (n_exp, tm), 0)
    in_group = (rows_f >= g_idx * per_group) & (rows_f < (g_idx + 1) * per_group)
    neg = jnp.float32(-jnp.inf)
    l1 = jnp.where(in_group, lf, neg)
    m1 = jnp.max(l1, axis=0, keepdims=True)
    e1 = jnp.min(jnp.where(l1 == m1, rows_f, n_exp), axis=0, keepdims=True)
    l2 = jnp.where(rows_f == e1, neg, l1)
    m2 = jnp.max(l2, axis=0, keepdims=True)
    e2 = jnp.min(jnp.where(l2 == m2, rows_f, n_exp), axis=0, keepdims=True)
    r2 = jnp.exp(m2 - m1)
    w1 = 1.0 / (1.0 + r2)
    row8 = lax.broadcasted_iota(I32, (SUBLANES, tm), 0)
    gates_ref[...] = jnp.where(row8 == 0, p_group * w1, jnp.where(row8 == 1, p_group * (r2 * w1), 0.0))

    oh0 = (rows_f == e1).astype(F32)
    oh1 = (rows_f == e2).astype(F32)
    both = oh0 + oh1
    tr = lax.broadcasted_iota(I32, (tm, tm), 0)
    tc = lax.broadcasted_iota(I32, (tm, tm), 1)
    earlier = jnp.dot(both.astype(BF16), (tr < tc).astype(BF16), preferred_element_type=F32)
    cnt = cnt_s[...]
    rank_base = earlier + cnt
    tile_cnt = jnp.sum(both, axis=1, keepdims=True)
    page_rows = float(EXPERT_ROWS)
    k0 = jnp.floor(cnt * (1.0 / page_rows))
    new_cnt = cnt + tile_cnt
    limit = (k0 + 1.0) * page_rows
    need_a = ((cnt == k0 * page_rows) & (tile_cnt > 0.0)).astype(F32)
    need_b = (new_cnt > limit).astype(F32)
    need = need_a + need_b
    er = lax.broadcasted_iota(I32, (n_exp, n_exp), 0)
    ec = lax.broadcasted_iota(I32, (n_exp, n_exp), 1)
    before = jnp.dot((ec < er).astype(BF16), jnp.broadcast_to(need, (n_exp, LANES)).astype(BF16),
                     preferred_element_type=F32)[:, 0:1]
    base = npage_s[...] + before
    page_a = jnp.where(need_a > 0.0, base, page_s[...])
    page_b = base + need_a
    npage_s[...] = npage_s[...] + jnp.sum(need, axis=0, keepdims=True)
    lane = lax.broadcasted_iota(I32, table_s.shape, 1).astype(F32)
    table = jnp.where((lane == k0) & (need_a > 0.0), page_a, table_s[...])
    table_s[...] = jnp.where((lane == k0 + 1.0) & (need_b > 0.0), page_b, table)
    cnt_s[...] = new_cnt
    page_s[...] = jnp.where(jnp.floor(new_cnt * (1.0 / page_rows)) == k0, page_a, page_b)

    def dest_rows(oh):
        rank = jnp.sum(oh * rank_base, axis=0, keepdims=True)
        lim = jnp.sum(oh * limit, axis=0, keepdims=True)
        pa = jnp.sum(oh * page_a, axis=0, keepdims=True)
        pb = jnp.sum(oh * page_b, axis=0, keepdims=True)
        within = rank - jnp.floor(rank * (1.0 / page_rows)) * page_rows
        return jnp.where(rank < lim, pa, pb) * page_rows + within

    dest = jnp.where(row8 == 0, dest_rows(oh0), jnp.where(row8 == 1, dest_rows(oh1), 0.0)).astype(I32)
    dest_ref[...] = dest
    dest_vmem[slot] = dest
    dest_copy(slot).start()

    @pl.when(i == n_steps - 1)
    def _():
        pages_ref[...] = table_s[...].astype(I32)
        counts_ref[...] = jnp.broadcast_to(cnt_s[...], counts_ref.shape).astype(I32)
        dest_copy(slot).wait()
        scatter_wait(prev)

        def last(j, carry):
            for r in range(DMA_UNROLL):
                t = j * DMA_UNROLL + r
                _row_copy(stage.at[slot], t, sorted_ref, dest_smem[slot, 0, t], scatter_sems.at[slot]).start()
                _row_copy(stage.at[slot], t, sorted_ref, dest_smem[slot, 1, t], scatter_sems.at[slot]).start()
            return carry

        lax.fori_loop(0, tm // DMA_UNROLL, last, 0)
        scatter_wait(slot)


def _outproj(xp, xs, cp, cs, op, os_, w_out, g2, wr, br, zeroed, *, n_groups, per_group, n_pages):
    n_p, d = xp.shape
    n_all = n_p + xs.shape[0]
    n_first = n_p // ROW_TILE
    n_exp = n_groups * per_group
    c_conv = cp.shape[1]
    vv = op.shape[1]
    tile = (d // 2,)
    first, second = _split_maps(n_first)
    const = lambda i: (0, 0)
    row = lambda i: (i, 0)
    col = lambda i: (0, i)
    rows_sorted = zeroed.shape[0]
    assert rows_sorted >= n_pages * EXPERT_ROWS + 2 * ROW_TILE
    assert ROW_TILE <= EXPERT_ROWS, "a tile may open at most two pages per expert"
    return pl.pallas_call(
        functools.partial(_outproj_kernel, n_first=n_first, c_conv=c_conv, n_groups=n_groups, per_group=per_group,
                          trash_row=n_pages * EXPERT_ROWS),
        grid=(n_all // ROW_TILE,),
        in_specs=[
            pl.BlockSpec((ROW_TILE, d), first), pl.BlockSpec((ROW_TILE, d), second),
            pl.BlockSpec((ROW_TILE, c_conv), first), pl.BlockSpec((ROW_TILE, c_conv), second),
            pl.BlockSpec((ROW_TILE, vv), first), pl.BlockSpec((ROW_TILE, vv), second),
            pl.BlockSpec(w_out.shape, const, pipeline_mode=pl.Buffered(1)),
            pl.BlockSpec((1, d), const),
            pl.BlockSpec(wr.shape, const),
            pl.BlockSpec(br.shape, const),
            pl.BlockSpec(memory_space=pl.ANY),
        ],
        out_specs=[
            pl.BlockSpec((ROW_TILE, d), row),
            pl.BlockSpec((SUBLANES, ROW_TILE), col), pl.BlockSpec((SUBLANES, ROW_TILE), col),
            pl.BlockSpec((n_exp, LANES), const), pl.BlockSpec((n_exp, LANES), const),
            pl.BlockSpec(memory_space=pl.ANY),
        ],
        out_shape=[
            jax.ShapeDtypeStruct((n_all, d), F32),
            jax.ShapeDtypeStruct((SUBLANES, n_all), I32), jax.ShapeDtypeStruct((SUBLANES, n_all), F32),
            jax.ShapeDtypeStruct((n_exp, LANES), I32), jax.ShapeDtypeStruct((n_exp, LANES), I32),
            jax.ShapeDtypeStruct((rows_sorted,) + tile, U32),
        ],
        scratch_shapes=[
            pltpu.VMEM((2, ROW_TILE) + tile, U32),
            pltpu.VMEM((2, SUBLANES, ROW_TILE), I32), pltpu.SMEM((2, SUBLANES, ROW_TILE), I32),
            pltpu.VMEM((n_exp, 1), F32), pltpu.VMEM((n_exp, 1), F32), pltpu.VMEM((1, 1), F32),
            pltpu.VMEM((n_exp, LANES), F32),
            pltpu.SemaphoreType.DMA((2,)), pltpu.SemaphoreType.DMA((2,)),
        ],
        compiler_params=_params(("arbitrary",)),
        input_output_aliases={10: 5},
        name="outproj",
    )(xp, xs, cp, cs, op, os_, w_out, g2, wr, br, zeroed)


def _experts_kernel(cnt_ref, pages_ref, xs_ref, wg_ref, wu_ref, wd_ref, ysp_ref,
                    xbuf, ybuf, wg_f32, wu_f32, wd_f32, wg_bf, wu_bf, wd_bf, first_blk, page_seq,
                    gsems, ysems, wsems, *, n_exp, table_lanes):
    e = pl.program_id(0)
    tb = xbuf.shape[1]
    n_pages = page_seq.shape[0]

    def n_pages_of(ex):
        return (cnt_ref[ex] + (tb - 1)) // tb

    def page_rows(blk):
        return pl.ds(pl.multiple_of(page_seq[blk] * tb, tb), tb)

    def fetch(blk, slot):
        return pltpu.make_async_copy(xs_ref.at[page_rows(blk)], xbuf.at[slot], gsems.at[slot])

    def writeback(blk, slot):
        return pltpu.make_async_copy(ybuf.at[slot], ysp_ref.at[page_rows(blk)], ysems.at[slot])

    def weight_copies(ex, slot):
        return (pltpu.make_async_copy(wg_ref.at[ex], wg_f32.at[slot], wsems.at[slot]),
                pltpu.make_async_copy(wu_ref.at[ex], wu_f32.at[slot], wsems.at[slot]),
                pltpu.make_async_copy(wd_ref.at[ex], wd_f32.at[slot], wsems.at[slot]))

    @pl.when(e == 0)
    def _():
        for ahead in range(min(WEIGHT_SLOTS - 1, n_exp)):
            for cp in weight_copies(ahead, ahead):
                cp.start(priority=1)

        def per_expert(ex, blk):
            first_blk[ex] = blk

            def per_page(j, carry):
                page_seq[blk + j] = pages_ref[ex * table_lanes + j]
                return carry

            lax.fori_loop(0, n_pages_of(ex), per_page, 0)
            return blk + n_pages_of(ex)

        first_blk[n_exp] = lax.fori_loop(0, n_exp, per_expert, 0)
        fetch(0, 0).start()

    @pl.when(e + WEIGHT_SLOTS - 1 < n_exp)
    def _():
        for cp in weight_copies(e + WEIGHT_SLOTS - 1, (e + WEIGHT_SLOTS - 1) % WEIGHT_SLOTS):
            cp.start(priority=1)

    b_lo = first_blk[e]
    b_hi = first_blk[e + 1]
    n_total = first_blk[n_exp]
    wslot = e % WEIGHT_SLOTS
    for cp in weight_copies(e, wslot):
        cp.wait()
    wg_bf[...] = wg_f32[wslot].astype(BF16)
    wu_bf[...] = wu_f32[wslot].astype(BF16)
    wd_bf[...] = wd_f32[wslot].astype(BF16)

    def block(b, carry):
        slot = b % 2

        @pl.when(b >= 2)
        def _():
            writeback(b, slot).wait()

        fetch(b, slot).wait()
        fetch(jnp.minimum(b + 1, n_total - 1), 1 - slot).start()

        def swiglu(rows):
            hi, lo = _unpack_pairs(xbuf[slot, rows, :])
            x = jnp.concatenate([hi.astype(BF16), lo.astype(BF16)], axis=1)
            hg = jnp.dot(x, wg_bf[...], preferred_element_type=F32)
            hu = jnp.dot(x, wu_bf[...], preferred_element_type=F32)
            hb = (_silu(hg) * hu).astype(BF16)
            ybuf[slot, rows, :] = _pack_pairs(jnp.dot(hb, wd_bf[...], preferred_element_type=F32))

        valid = cnt_ref[e] - (b - b_lo) * tb
        part = tb // PAGE_PARTS
        for n_parts in range(1, PAGE_PARTS + 1):
            used = n_parts * part
            fits = valid > used - part
            if n_parts < PAGE_PARTS:
                fits = fits & (valid <= used)

            @pl.when(fits)
            def _(used=used):
                swiglu(slice(0, used))
                if used < tb:
                    ybuf[slot, used:tb, :] = jnp.zeros((tb - used,) + ybuf.shape[2:], ybuf.dtype)

        writeback(b, slot).start()
        return carry

    lax.fori_loop(b_lo, b_hi, block, 0)

    @pl.when(e == n_exp - 1)
    def _():
        fetch(0, n_total % 2).wait()

        @pl.when(n_total >= 2)
        def _():
            writeback(0, n_total % 2).wait()

        writeback(0, (n_total + 1) % 2).wait()
        ybuf[0] = jnp.zeros(ybuf.shape[1:], ybuf.dtype)

        def spare(blk):
            return pltpu.make_async_copy(ybuf.at[0], ysp_ref.at[pl.ds(pl.multiple_of(blk * tb, tb), tb)], ysems.at[0])

        def zero(blk, carry):
            spare(blk).start()
            return carry

        lax.fori_loop(n_total, n_pages, zero, 0)

        def zero_wait(blk, carry):
            spare(0).wait()
            return carry

        lax.fori_loop(n_total, n_pages, zero_wait, 0)


def _experts(counts, pages_flat, xs_sorted, w_gate, w_up, w_down, *, n_pages, table_lanes):
    tile = xs_sorted.shape[1:]
    n_exp, d, ff = w_gate.shape
    anyspec = pl.BlockSpec(memory_space=pl.ANY)
    grid_spec = pltpu.PrefetchScalarGridSpec(
        num_scalar_prefetch=2,
        grid=(n_exp,),
        in_specs=[anyspec, anyspec, anyspec, anyspec],
        out_specs=anyspec,
        scratch_shapes=[
            pltpu.VMEM((2, EXPERT_ROWS) + tile, U32), pltpu.VMEM((2, EXPERT_ROWS) + tile, U32),
            pltpu.VMEM((WEIGHT_SLOTS, d, ff), F32), pltpu.VMEM((WEIGHT_SLOTS, d, ff), F32),
            pltpu.VMEM((WEIGHT_SLOTS, ff, d), F32),
            pltpu.VMEM((d, ff), BF16), pltpu.VMEM((d, ff), BF16), pltpu.VMEM((ff, d), BF16),
            pltpu.SMEM((n_exp + 1,), I32), pltpu.SMEM((n_pages,), I32),
            pltpu.SemaphoreType.DMA((2,)), pltpu.SemaphoreType.DMA((2,)), pltpu.SemaphoreType.DMA((WEIGHT_SLOTS,)),
        ],
    )
    return pl.pallas_call(
        functools.partial(_experts_kernel, n_exp=n_exp, table_lanes=table_lanes),
        grid_spec=grid_spec,
        out_shape=jax.ShapeDtypeStruct((n_pages * EXPERT_ROWS,) + tile, U32),
        compiler_params=_params(("arbitrary",)),
        name="experts",
    )(counts, pages_flat, xs_sorted, w_gate, w_up, w_down)


def _combine_kernel(dest_ref, dest_next_ref, gates_ref, x1_ref, ysp_ref, gf_ref, yp_ref, ysmp_ref,
                    buf0, buf1, sems, *, n_first):
    i = pl.program_id(0)
    n = pl.num_programs(0)
    tm = x1_ref.shape[0]
    slot = i % 2

    def gather(d_ref, s):
        def body(j, carry):
            for r in range(DMA_UNROLL):
                t = j * DMA_UNROLL + r
                _row_copy(ysp_ref, d_ref[0, t], buf0.at[s], t, sems.at[s]).start(priority=0)
                _row_copy(ysp_ref, d_ref[1, t], buf1.at[s], t, sems.at[s]).start(priority=1)
            return carry

        lax.fori_loop(0, tm // DMA_UNROLL, body, 0)

    def gather_wait(s):
        pltpu.make_async_copy(ysp_ref.at[pl.ds(0, tm)], buf0.at[s], sems.at[s]).wait()
        pltpu.make_async_copy(ysp_ref.at[pl.ds(0, tm)], buf1.at[s], sems.at[s]).wait()

    @pl.when(i == 0)
    def _():
        gather(dest_ref, 0)

    for t in range(tm):
        _row_copy(ysp_ref, dest_next_ref[0, t], buf0.at[1 - slot], t, sems.at[1 - slot]).start(priority=0)
        _row_copy(ysp_ref, dest_next_ref[1, t], buf1.at[1 - slot], t, sems.at[1 - slot]).start(priority=1)

    gather_wait(slot)
    hi0, lo0 = _unpack_pairs(buf0[slot])
    hi1, lo1 = _unpack_pairs(buf1[slot])
    g0 = gates_ref[:, 0:1]
    g1 = gates_ref[:, 1:2]
    moe = jnp.concatenate([g0 * hi0 + g1 * hi1, g0 * lo0 + g1 * lo1], axis=1)
    y = _rms(x1_ref[...] + moe, gf_ref[...])

    @pl.when(i < n_first)
    def _():
        yp_ref[...] = y

    @pl.when(i >= n_first)
    def _():
        ysmp_ref[...] = y

    @pl.when(i == n - 1)
    def _():
        gather_wait(1 - slot)


def _combine(dest, gates_t, x1, ysp, gf, *, n_p):
    n_all, d = x1.shape
    tile = ysp.shape[1:]
    n_first = n_p // ROW_TILE
    n_tiles = n_all // ROW_TILE
    first, second = _split_maps(n_first)
    return pl.pallas_call(
        functools.partial(_combine_kernel, n_first=n_first),
        grid=(n_tiles,),
        in_specs=[
            pl.BlockSpec((SUBLANES, ROW_TILE), lambda i: (0, i), memory_space=pltpu.SMEM),
            pl.BlockSpec((SUBLANES, ROW_TILE), lambda i: (0, jnp.minimum(i + 1, n_tiles - 1)),
                         memory_space=pltpu.SMEM),
            pl.BlockSpec((ROW_TILE, SUBLANES), lambda i: (i, 0)),
            pl.BlockSpec((ROW_TILE, d), lambda i: (i, 0)),
            pl.BlockSpec(memory_space=pl.ANY),
            pl.BlockSpec((1, d), lambda i: (0, 0)),
        ],
        out_specs=[pl.BlockSpec((ROW_TILE, d), first), pl.BlockSpec((ROW_TILE, d), second)],
        out_shape=[jax.ShapeDtypeStruct((n_p, d), F32), jax.ShapeDtypeStruct((n_all - n_p, d), F32)],
        scratch_shapes=[pltpu.VMEM((2, ROW_TILE) + tile, U32), pltpu.VMEM((2, ROW_TILE) + tile, U32),
                        pltpu.SemaphoreType.DMA((2,))],
        compiler_params=_params(("arbitrary",)),
        name="combine",
    )(dest, dest, gates_t, x1, ysp, gf)


def kernel(x_prompt, x_sample, cache_conv, state_gla, norm1_g, w_in, w_lr2, b_lr2, w_dw, b_dw, conv_ln_g, conv_ln_b, gla_norm_g, w_out, norm2_g, w_router_coarse, b_router_coarse, w_router_fine, b_router_fine, w_exp_gate, w_exp_up, w_exp_down, norm_f_g):
    assert norm1_g.shape[0] == 1, "single trunk layer"
    bp, tp, d = x_prompt.shape
    bs, ts, _ = x_sample.shape
    heads, dk, dv = state_gla.shape[2:]
    c_conv = w_dw.shape[2]
    width = w_dw.shape[1]
    qk, vv = heads * dk, heads * dv
    n_groups, _, per_group = w_router_fine.shape[1:]
    n_exp = n_groups * per_group
    n_p, n_s = bp * tp, bs * ts
    n_s_pad = -(-n_s // ROW_TILE) * ROW_TILE
    pad_rows = lambda a: jnp.pad(a, ((0, n_s_pad - n_s), (0, 0)))
    n_all = n_p + n_s_pad
    assert n_p % ROW_TILE == 0 and width - 1 <= HIST_PAD

    xp = x_prompt.reshape(n_p, d)
    xs = pad_rows(x_sample.reshape(n_s, d))
    row = lambda a: a.reshape(1, -1)

    mixer_args = (row(norm1_g[0]), w_in[0].astype(BF16), w_lr2[0], row(b_lr2[0]),
                  w_dw[0], row(b_dw[0]), row(conv_ln_g[0]), row(conv_ln_b[0]), row(gla_norm_g[0]))
    hist_p = jnp.zeros((bp, HIST_PAD, c_conv), F32)
    hist_s = jnp.pad(cache_conv[0], ((0, 0), (HIST_PAD - (width - 1), 0), (0, 0)))
    s0_p = jnp.zeros((bp, heads, dk, dv), F32)
    n_pages = (2 * n_all) // EXPERT_ROWS + n_exp
    sorted_rows = n_pages * EXPERT_ROWS + -(-2 * ROW_TILE // EXPERT_ROWS) * EXPERT_ROWS
    c_p, o_p, tail_p, gla_p, zeroed = _front(xp, hist_p, s0_p, *mixer_args, n_seq=1, seq_rows=ROW_TILE,
                                             zero_shape=(sorted_rows, d // 2))
    c_s, o_s, tail_s, gla_s = _front(x_sample.reshape(n_s, d), hist_s, state_gla[0], *mixer_args,
                                     n_seq=bs, seq_rows=ts)
    c_s, o_s = pad_rows(c_s), pad_rows(o_s)

    wr = jnp.concatenate([w_router_coarse[0].T,
                          jnp.transpose(w_router_fine[0], (0, 2, 1)).reshape(n_exp, d)], axis=0)
    br = jnp.concatenate([b_router_coarse[0], b_router_fine[0].reshape(n_exp)])
    r_rows = -(-(n_groups + n_exp) // SUBLANES) * SUBLANES
    wr = jnp.pad(wr, ((0, r_rows - wr.shape[0]), (0, 0)))
    br = jnp.pad(br, (0, r_rows - br.shape[0])).reshape(r_rows, 1)
    wr_hi = wr.astype(BF16)
    wr = jnp.concatenate([wr_hi, (wr - wr_hi.astype(F32)).astype(BF16)], axis=0)
    assert n_all // EXPERT_ROWS + 2 <= LANES, "page table row must hold one expert's pages"
    x1, dest, gates, pages, counts, xs_sorted = _outproj(
        xp, xs, c_p, c_s, o_p, o_s, w_out[0].astype(BF16), row(norm2_g[0]), wr, br, zeroed,
        n_groups=n_groups, per_group=per_group, n_pages=n_pages)
    ysp = _experts(counts[:, 0], pages.reshape(-1), xs_sorted, w_exp_gate[0], w_exp_up[0], w_exp_down[0],
                   n_pages=n_pages, table_lanes=LANES)
    y_p, y_s = _combine(dest, gates.T, x1, ysp, row(norm_f_g), n_p=n_p)

    lead = HIST_PAD - (width - 1)
    return (y_p.reshape(bp, tp, d), y_s[:n_s].reshape(bs, ts, d), tail_p[:, lead:][None], gla_p[None],
            tail_s[:, lead:][None], gla_s[None])
```

```python
import functools

import jax
import jax.numpy as jnp
from jax import lax
from jax.experimental import pallas as pl
from jax.experimental.pallas import tpu as pltpu

F32 = jnp.float32
BF16 = jnp.bfloat16
I32 = jnp.int32
U32 = jnp.uint32
EPS = 1e-6
GATE_TEMP = 16.0
HIGHEST = lax.Precision.HIGHEST

LANES = 128
SUBLANES = 8
ROW_TILE = 256
EXPERT_ROWS = 256
HIST_PAD = 32
DMA_UNROLL = 8
WEIGHT_SLOTS = 3
VMEM_LIMIT = 56 * 1024 * 1024


def _params(semantics, vmem=VMEM_LIMIT):
    return pltpu.CompilerParams(dimension_semantics=semantics, vmem_limit_bytes=vmem)


def _rms(x, g):
    return x * lax.rsqrt(jnp.mean(x * x, axis=-1, keepdims=True) + EPS) * g


def _silu(x):
    return x * jax.nn.sigmoid(x)


def _log_sigmoid(z):
    return jnp.minimum(z, 0.0) - jnp.log(1.0 + jnp.exp(-jnp.abs(z)))


def _pick(i, n_first, first_ref, second_ref):
    return jnp.where(i < n_first, first_ref[...], second_ref[...])


def _split_maps(n_first):
    first = lambda i: (jnp.minimum(i, n_first - 1), 0)
    second = lambda i: (jnp.maximum(i - n_first, 0), 0)
    return first, second


def _pack_pairs(x):
    half = x.shape[1] // 2
    hi = lax.bitcast_convert_type(x[:, :half].astype(BF16).astype(F32), U32)
    lo = lax.bitcast_convert_type(x[:, half:].astype(BF16).astype(F32), U32)
    return hi | (lo >> 16)


def _unpack_pairs(p):
    hi = lax.bitcast_convert_type(p & jnp.uint32(0xFFFF0000), F32)
    lo = lax.bitcast_convert_type(p << 16, F32)
    return hi, lo


def _row_copy(src, s, dst, d, sem):
    return pltpu.make_async_copy(src.at[pl.ds(s, 1)], dst.at[pl.ds(d, 1)], sem)


GLA_SUB = 16
GLA_CHUNK = 128
CONV_ROWS = 64


def _zero_fill_step(zero_ref, zbuf, zsem, n_chunks, per_step):
    step = pl.program_id(0) * pl.num_programs(1) + pl.program_id(1)
    last = pl.num_programs(0) * pl.num_programs(1) - 1
    rows = zbuf.shape[0]

    def copy(idx):
        return pltpu.make_async_copy(zbuf, zero_ref.at[pl.ds(pl.multiple_of(idx * rows, rows), rows)], zsem)

    @pl.when(step == 0)
    def _():
        zbuf[...] = jnp.zeros_like(zbuf)

    for p in range(per_step):
        earlier = (step - 1) * per_step + p

        @pl.when((step > 0) & (earlier < n_chunks))
        def _():
            copy(earlier).wait()

    for p in range(per_step):
        idx = step * per_step + p

        @pl.when(idx < n_chunks)
        def _():
            copy(idx).start()

        @pl.when((step == last) & (idx < n_chunks))
        def _():
            copy(idx).wait()


def _front_kernel(x_ref, hist_ref, s0_ref, g1_ref, w_ref, wlr2_ref, blr2_ref,
                  wdw_ref, bdw_ref, lg_ref, lb_ref, gn_ref,
                  c_ref, o_ref, tail_ref, sout_ref, *rest,
                  n_seq, seq_rows, chunk, heads, dk, dv, c_conv, width, zero_chunks):
    i = pl.program_id(1)
    qk, vv = heads * dk, heads * dv
    if zero_chunks:
        zero_ref, win, cbuf, state, zbuf, zsem = rest
        _zero_fill_step(zero_ref, zbuf, zsem, *zero_chunks)
    else:
        win, cbuf, state = rest

    @pl.when(i == 0)
    def _():
        for s in range(n_seq):
            win[s, 0:HIST_PAD, :] = hist_ref[s]
        state[...] = s0_ref[...]

    @pl.when(i > 0)
    def _():
        for s in range(n_seq):
            win[s, 0:HIST_PAD, :] = win[s, seq_rows:seq_rows + HIST_PAD, :]

    h = _rms(x_ref[...].reshape(n_seq * seq_rows, x_ref.shape[2]), g1_ref[...]).astype(BF16)

    def mm(lo, n):
        return jnp.dot(h, w_ref[:, lo:lo + n], preferred_element_type=F32)

    u = mm(0, c_conv) * jax.nn.sigmoid(mm(c_conv, c_conv))
    lead = HIST_PAD - (width - 1)
    win_rows = seq_rows + HIST_PAD
    conv_rows = min(seq_rows, CONV_ROWS)
    for s in range(n_seq):
        r0 = s * seq_rows
        win[s, HIST_PAD:HIST_PAD + seq_rows, :] = u[r0:r0 + seq_rows, :]
        for cb in range(c_conv // LANES):
            cs = slice(cb * LANES, (cb + 1) * LANES)
            slab = win[s, :, cs]
            rolled = [slab] + [pltpu.roll(slab, win_rows - r, axis=0) for r in range(1, SUBLANES)]
            for t0 in range(0, seq_rows, conv_rows):
                acc = jnp.broadcast_to(bdw_ref[:, cs], (conv_rows, LANES))
                for j in range(width):
                    r, a8 = (lead + j) % SUBLANES, (lead + j) // SUBLANES * SUBLANES
                    tap = rolled[r][t0 + a8:t0 + a8 + conv_rows, :]
                    acc = acc + wdw_ref[j:j + 1, cs] * tap
                cbuf[r0 + t0:r0 + t0 + conv_rows, cs] = acc
    cv = cbuf[...]
    mu = jnp.mean(cv, axis=-1, keepdims=True)
    xc = cv - mu
    cn = xc * lax.rsqrt(jnp.mean(xc * xc, axis=-1, keepdims=True) + EPS) * lg_ref[...] + lb_ref[...]
    c_ref[...] = _silu(cn).astype(c_ref.dtype).reshape(c_ref.shape)

    off = 2 * c_conv
    q = mm(off, qk) * (dk ** -0.5)
    k = mm(off + qk, qk)
    v = mm(off + 2 * qk, vv).astype(BF16)
    g = mm(off + 2 * qk + vv, vv)
    lr = mm(off + 2 * qk + 2 * vv, wlr2_ref.shape[0])
    z = jnp.dot(lr, wlr2_ref[...], precision=HIGHEST, preferred_element_type=F32) + blr2_ref[...]
    la = _log_sigmoid(z) * (1.0 / GATE_TEMP)

    n_sub = chunk // GLA_SUB
    r = lax.broadcasted_iota(I32, (chunk, chunk), 0)
    c = lax.broadcasted_iota(I32, (chunk, chunk), 1)
    causal = c <= r
    local_sum = (causal & ((r // GLA_SUB) == (c // GLA_SUB))).astype(BF16)
    sub_rows = [slice(j * GLA_SUB, (j + 1) * GLA_SUB) for j in range(n_sub)]
    eye = lax.broadcasted_iota(I32, (dk, dk), 0) == lax.broadcasted_iota(I32, (dk, dk), 1)
    for s in range(n_seq):
        for t0 in range(0, seq_rows, chunk):
            rows = slice(s * seq_rows + t0, s * seq_rows + t0 + chunk)
            la_1 = la[rows, :].astype(BF16)
            resid = la[rows, :] - la_1.astype(F32)
            la_2 = resid.astype(BF16)
            la_3 = (resid - la_2.astype(F32)).astype(BF16)
            sums = jnp.dot(local_sum, jnp.concatenate([la_1, la_2, la_3], axis=1), preferred_element_type=F32)
            local = sums[:, 0:qk] + sums[:, qk:2 * qk] + sums[:, 2 * qk:3 * qk]
            bases = [jnp.zeros((1, qk), F32)]
            for j in range(1, n_sub):
                bases.append(bases[-1] + local[j * GLA_SUB - 1:j * GLA_SUB, :])
            base = jnp.concatenate([jnp.broadcast_to(bs, (GLA_SUB, qk)) for bs in bases], axis=0)
            b = base + local
            b_end = b[chunk - 1:chunk, :]
            q_c, k_c = q[rows, :], k[rows, :]
            q_loc = q_c * jnp.exp(local)
            k_loc = k_c * jnp.exp(-local)
            q_in = (q_c * jnp.exp(b)).astype(BF16)
            k_out = (k_c * jnp.exp(b_end - b)).astype(BF16)
            decay_row = jnp.exp(b_end)
            zero_rows = jnp.zeros((GLA_SUB, qk), F32)
            q_parts, k_parts = [], []
            for j in range(n_sub):
                q_parts.append(jnp.concatenate(
                    [q_loc[sub_rows[m], :] * jnp.exp(jnp.minimum(bases[m] - bases[j], 0.0)) if m > j
                     else q_loc[sub_rows[m], :] for m in range(n_sub)], axis=0).astype(BF16))
                k_parts.append(jnp.concatenate(
                    [k_loc[sub_rows[m], :] if m == j else zero_rows for m in range(n_sub)], axis=0).astype(BF16))
            for hd in range(heads):
                ks = slice(hd * dk, (hd + 1) * dk)
                vs = slice(hd * dv, (hd + 1) * dv)
                vh = v[rows, vs]
                q_cat = jnp.concatenate([p[:, ks] for p in q_parts], axis=1)
                k_cat = jnp.concatenate([p[:, ks] for p in k_parts], axis=1)
                att = lax.dot_general(q_cat, k_cat, (((1,), (1,)), ((), ())), preferred_element_type=F32)
                att = jnp.where(causal, att, 0.0).astype(BF16)
                s_h = state[s, hd]
                o = jnp.dot(att, vh, preferred_element_type=F32)
                o = o + jnp.dot(q_in[:, ks], s_h.astype(BF16), preferred_element_type=F32)
                decay_col = jnp.sum(jnp.where(eye, jnp.broadcast_to(decay_row[:, ks], (dk, dk)), 0.0),
                                    axis=1, keepdims=True)
                state[s, hd] = decay_col * s_h + lax.dot_general(k_out[:, ks], vh, (((0,), (0,)), ((), ())),
                                                                 preferred_element_type=F32)
                o = o * lax.rsqrt(jnp.mean(o * o, axis=-1, keepdims=True) + EPS) * gn_ref[...]
                o_ref[s, t0:t0 + chunk, vs] = (o * _silu(g[rows, vs])).astype(o_ref.dtype)

    @pl.when(i == pl.num_programs(1) - 1)
    def _():
        for s in range(n_seq):
            tail_ref[s] = win[s, seq_rows:seq_rows + HIST_PAD, :]
        sout_ref[...] = state[...]


def _front(x2d, hist_pad, s0, g1, w_in, w_lr2, blr2, w_dw, b_dw, ln_g, ln_b, gn, *, n_seq, seq_rows,
           zero_shape=None):
    bsz, heads, dk, dv = s0.shape
    d = x2d.shape[1]
    t = x2d.shape[0] // bsz
    c_conv = w_dw.shape[1]
    width = w_dw.shape[0]
    qk, vv = heads * dk, heads * dv
    assert t % seq_rows == 0 and bsz % n_seq == 0
    nt = t // seq_rows
    rows = n_seq * seq_rows
    chunk = GLA_CHUNK if seq_rows % GLA_CHUNK == 0 else seq_rows
    assert chunk % GLA_SUB == 0 and seq_rows % min(seq_rows, CONV_ROWS) == 0
    const = lambda b, i: (0, 0)
    tile = lambda b, i: (b, i, 0)
    per_seq3 = lambda b, i: (b, 0, 0)
    per_seq4 = lambda b, i: (b, 0, 0, 0)
    extra_out_specs, extra_out_shape, extra_scratch, zero_chunks = [], [], [], None
    if zero_shape is not None:
        assert zero_shape[0] % EXPERT_ROWS == 0
        n_chunks = zero_shape[0] // EXPERT_ROWS
        zero_chunks = (n_chunks, -(-n_chunks // ((bsz // n_seq) * nt)))
        extra_out_specs = [pl.BlockSpec(memory_space=pl.ANY)]
        extra_out_shape = [jax.ShapeDtypeStruct(zero_shape, U32)]
        extra_scratch = [pltpu.VMEM((EXPERT_ROWS, zero_shape[1]), U32), pltpu.SemaphoreType.DMA(())]
    return pl.pallas_call(
        functools.partial(_front_kernel, n_seq=n_seq, seq_rows=seq_rows, chunk=chunk, heads=heads, dk=dk, dv=dv,
                          c_conv=c_conv, width=width, zero_chunks=zero_chunks),
        grid=(bsz // n_seq, nt),
        in_specs=[
            pl.BlockSpec((n_seq, seq_rows, d), tile),
            pl.BlockSpec((n_seq, HIST_PAD, c_conv), per_seq3),
            pl.BlockSpec((n_seq, heads, dk, dv), per_seq4),
            pl.BlockSpec((1, d), const),
            pl.BlockSpec(w_in.shape, const, pipeline_mode=pl.Buffered(1)),
            pl.BlockSpec(w_lr2.shape, const),
            pl.BlockSpec((1, qk), const),
            pl.BlockSpec(w_dw.shape, const),
            pl.BlockSpec((1, c_conv), const), pl.BlockSpec((1, c_conv), const), pl.BlockSpec((1, c_conv), const),
            pl.BlockSpec((1, dv), const),
        ],
        out_specs=[
            pl.BlockSpec((n_seq, seq_rows, c_conv), tile), pl.BlockSpec((n_seq, seq_rows, vv), tile),
            pl.BlockSpec((n_seq, HIST_PAD, c_conv), per_seq3),
            pl.BlockSpec((n_seq, heads, dk, dv), per_seq4),
        ] + extra_out_specs,
        out_shape=[
            jax.ShapeDtypeStruct((bsz, t, c_conv), BF16), jax.ShapeDtypeStruct((bsz, t, vv), BF16),
            jax.ShapeDtypeStruct((bsz, HIST_PAD, c_conv), F32),
            jax.ShapeDtypeStruct((bsz, heads, dk, dv), F32),
        ] + extra_out_shape,
        scratch_shapes=[
            pltpu.VMEM((n_seq, seq_rows + HIST_PAD, c_conv), F32),
            pltpu.VMEM((rows, c_conv), F32),
            pltpu.VMEM((n_seq, heads, dk, dv), F32),
        ] + extra_scratch,
        compiler_params=_params(("arbitrary", "arbitrary")),
        name="front",
    )(x2d.reshape(bsz, t, d), hist_pad, s0, g1, w_in, w_lr2, blr2, w_dw, b_dw, ln_g, ln_b, gn)


def _outproj_kernel(xp_ref, xs_ref, cp_ref, cs_ref, op_ref, os_ref, w_ref, g2_ref, wr_ref, br_ref, zeroed_ref,
                    x1_ref, dest_ref, gates_ref, pages_ref, counts_ref, sorted_ref,
                    w_bf, stage, dest_vmem, dest_smem, cnt_s, page_s, npage_s, table_s, scatter_sems, dest_sems,
                    *, n_first, c_conv, n_groups, per_group, trash_row):
    i = pl.program_id(0)
    n_steps = pl.num_programs(0)
    tm = x1_ref.shape[0]
    n_exp = n_groups * per_group
    slot = i % 2
    prev = 1 - slot

    def dest_copy(s):
        return pltpu.make_async_copy(dest_vmem.at[s], dest_smem.at[s], dest_sems.at[s])

    def scatter_wait(s):
        for _ in range(2):
            pltpu.make_async_copy(stage.at[s], sorted_ref.at[pl.ds(0, tm)], scatter_sems.at[s]).wait()

    @pl.when(i == 0)
    def _():
        w_bf[...] = w_ref[...].astype(BF16)
        cnt_s[...] = jnp.zeros_like(cnt_s)
        page_s[...] = jnp.zeros_like(page_s)
        npage_s[...] = jnp.zeros_like(npage_s)
        table_s[...] = jnp.zeros_like(table_s)
        stage[1] = jnp.zeros(stage.shape[1:], stage.dtype)

        def fill(t, carry):
            dest_smem[1, 0, t] = trash_row + t
            dest_smem[1, 1, t] = trash_row + tm + t
            return carry

        lax.fori_loop(0, tm, fill, 0)

    @pl.when(i >= 1)
    def _():
        dest_copy(prev).wait()
        scatter_wait(slot)

    for t in range(tm):
        _row_copy(stage.at[prev], t, sorted_ref, dest_smem[prev, 0, t], scatter_sems.at[prev]).start()
        _row_copy(stage.at[prev], t, sorted_ref, dest_smem[prev, 1, t], scatter_sems.at[prev]).start()

    x = _pick(i, n_first, xp_ref, xs_ref)
    cc = _pick(i, n_first, cp_ref, cs_ref)
    oo = _pick(i, n_first, op_ref, os_ref)
    del zeroed_ref
    mix = jnp.dot(jnp.concatenate([cc, oo], axis=1), w_bf[...], preferred_element_type=F32)
    x1 = x + mix
    x1_ref[...] = x1
    h2 = _rms(x1, g2_ref[...])
    stage[slot] = _pack_pairs(h2)
    h_hi = h2.astype(BF16)
    h_lo = (h2 - h_hi.astype(F32)).astype(BF16)
    parts = lax.dot_general(wr_ref[...], jnp.concatenate([h_hi, h_lo], axis=0), (((1,), (1,)), ((), ())),
                            preferred_element_type=F32)
    n_r = br_ref.shape[0]
    logits = (parts[0:n_r, 0:tm] + parts[0:n_r, tm:] + parts[n_r:, 0:tm] + parts[n_r:, tm:]) + br_ref[...]
    lc = logits[0:n_groups, :]
    mc = jnp.max(lc, axis=0, keepdims=True)
    p_group = 1.0 / jnp.sum(jnp.exp(lc - mc), axis=0, keepdims=True)
    rows_c = lax.broadcasted_iota(I32, (n_groups, tm), 0)
    g_idx = jnp.min(jnp.where(lc == mc, rows_c, n_groups), axis=0, keepdims=True)
    lf = logits[n_groups:n_groups + n_exp, :]
    rows_f = lax.broadcasted_iota(I32, (n_exp, tm), 0)
    in_group = (rows_f >= g_idx * per_group) & (rows_f < (g_idx + 1) * per_group)
    neg = jnp.float32(-jnp.inf)
    l1 = jnp.where(in_group, lf, neg)
    m1 = jnp.max(l1, axis=0, keepdims=True)
    e1 = jnp.min(jnp.where(l1 == m1, rows_f, n_exp), axis=0, keepdims=True)
    l2 = jnp.where(rows_f == e1, neg, l1)
    m2 = jnp.max(l2, axis=0, keepdims=True)
    e2 = jnp.min(jnp.where(l2 == m2, rows_f, n_exp), axis=0, keepdims=True)
    r2 = jnp.exp(m2 - m1)
    w1 = 1.0 / (1.0 + r2)
    row8 = lax.broadcasted_iota(I32, (SUBLANES, tm), 0)
    gates_ref[...] = jnp.where(row8 == 0, p_group * w1, jnp.where(row8 == 1, p_group * (r2 * w1), 0.0))

    oh0 = (rows_f == e1).astype(F32)
    oh1 = (rows_f == e2).astype(F32)
    both = oh0 + oh1
    tr = lax.broadcasted_iota(I32, (tm, tm), 0)
    tc = lax.broadcasted_iota(I32, (tm, tm), 1)
    earlier = jnp.dot(both.astype(BF16), (tr < tc).astype(BF16), preferred_element_type=F32)
    cnt = cnt_s[...]
    rank_base = earlier + cnt
    tile_cnt = jnp.sum(both, axis=1, keepdims=True)
    page_rows = float(EXPERT_ROWS)
    k0 = jnp.floor(cnt * (1.0 / page_rows))
    new_cnt = cnt + tile_cnt
    limit = (k0 + 1.0) * page_rows
    need_a = ((cnt == k0 * page_rows) & (tile_cnt > 0.0)).astype(F32)
    need_b = (new_cnt > limit).astype(F32)
    need = need_a + need_b
    er = lax.broadcasted_iota(I32, (n_exp, n_exp), 0)
    ec = lax.broadcasted_iota(I32, (n_exp, n_exp), 1)
    before = jnp.dot((ec < er).astype(BF16), jnp.broadcast_to(need, (n_exp, LANES)).astype(BF16),
                     preferred_element_type=F32)[:, 0:1]
    base = npage_s[...] + before
    page_a = jnp.where(need_a > 0.0, base, page_s[...])
    page_b = base + need_a
    npage_s[...] = npage_s[...] + jnp.sum(need, axis=0, keepdims=True)
    lane = lax.broadcasted_iota(I32, table_s.shape, 1).astype(F32)
    table = jnp.where((lane == k0) & (need_a > 0.0), page_a, table_s[...])
    table_s[...] = jnp.where((lane == k0 + 1.0) & (need_b > 0.0), page_b, table)
    cnt_s[...] = new_cnt
    page_s[...] = jnp.where(jnp.floor(new_cnt * (1.0 / page_rows)) == k0, page_a, page_b)

    def dest_rows(oh):
        rank = jnp.sum(oh * rank_base, axis=0, keepdims=True)
        lim = jnp.sum(oh * limit, axis=0, keepdims=True)
        pa = jnp.sum(oh * page_a, axis=0, keepdims=True)
        pb = jnp.sum(oh * page_b, axis=0, keepdims=True)
        within = rank - jnp.floor(rank * (1.0 / page_rows)) * page_rows
        return jnp.where(rank < lim, pa, pb) * page_rows + within

    dest = jnp.where(row8 == 0, dest_rows(oh0), jnp.where(row8 == 1, dest_rows(oh1), 0.0)).astype(I32)
    dest_ref[...] = dest
    dest_vmem[slot] = dest
    dest_copy(slot).start()

    @pl.when(i == n_steps - 1)
    def _():
        pages_ref[...] = table_s[...].astype(I32)
        counts_ref[...] = jnp.broadcast_to(cnt_s[...], counts_ref.shape).astype(I32)
        dest_copy(slot).wait()
        scatter_wait(prev)

        def last(j, carry):
            for r in range(DMA_UNROLL):
                t = j * DMA_UNROLL + r
                _row_copy(stage.at[slot], t, sorted_ref, dest_smem[slot, 0, t], scatter_sems.at[slot]).start()
                _row_copy(stage.at[slot], t, sorted_ref, dest_smem[slot, 1, t], scatter_sems.at[slot]).start()
            return carry

        lax.fori_loop(0, tm // DMA_UNROLL, last, 0)
        scatter_wait(slot)


def _outproj(xp, xs, cp, cs, op, os_, w_out, g2, wr, br, zeroed, *, n_groups, per_group, n_pages):
    n_p, d = xp.shape
    n_all = n_p + xs.shape[0]
    n_first = n_p // ROW_TILE
    n_exp = n_groups * per_group
    c_conv = cp.shape[1]
    vv = op.shape[1]
    tile = (d // 2,)
    first, second = _split_maps(n_first)
    const = lambda i: (0, 0)
    row = lambda i: (i, 0)
    col = lambda i: (0, i)
    rows_sorted = zeroed.shape[0]
    assert rows_sorted >= n_pages * EXPERT_ROWS + 2 * ROW_TILE
    assert ROW_TILE <= EXPERT_ROWS, "a tile may open at most two pages per expert"
    return pl.pallas_call(
        functools.partial(_outproj_kernel, n_first=n_first, c_conv=c_conv, n_groups=n_groups, per_group=per_group,
                          trash_row=n_pages * EXPERT_ROWS),
        grid=(n_all // ROW_TILE,),
        in_specs=[
            pl.BlockSpec((ROW_TILE, d), first), pl.BlockSpec((ROW_TILE, d), second),
            pl.BlockSpec((ROW_TILE, c_conv), first), pl.BlockSpec((ROW_TILE, c_conv), second),
            pl.BlockSpec((ROW_TILE, vv), first), pl.BlockSpec((ROW_TILE, vv), second),
            pl.BlockSpec(w_out.shape, const, pipeline_mode=pl.Buffered(1)),
            pl.BlockSpec((1, d), const),
            pl.BlockSpec(wr.shape, const),
            pl.BlockSpec(br.shape, const),
            pl.BlockSpec(memory_space=pl.ANY),
        ],
        out_specs=[
            pl.BlockSpec((ROW_TILE, d), row),
            pl.BlockSpec((SUBLANES, ROW_TILE), col), pl.BlockSpec((SUBLANES, ROW_TILE), col),
            pl.BlockSpec((n_exp, LANES), const), pl.BlockSpec((n_exp, LANES), const),
            pl.BlockSpec(memory_space=pl.ANY),
        ],
        out_shape=[
            jax.ShapeDtypeStruct((n_all, d), F32),
            jax.ShapeDtypeStruct((SUBLANES, n_all), I32), jax.ShapeDtypeStruct((SUBLANES, n_all), F32),
            jax.ShapeDtypeStruct((n_exp, LANES), I32), jax.ShapeDtypeStruct((n_exp, LANES), I32),
            jax.ShapeDtypeStruct((rows_sorted,) + tile, U32),
        ],
        scratch_shapes=[
            pltpu.VMEM(w_out.shape, BF16),
            pltpu.VMEM((2, ROW_TILE) + tile, U32),
            pltpu.VMEM((2, SUBLANES, ROW_TILE), I32), pltpu.SMEM((2, SUBLANES, ROW_TILE), I32),
            pltpu.VMEM((n_exp, 1), F32), pltpu.VMEM((n_exp, 1), F32), pltpu.VMEM((1, 1), F32),
            pltpu.VMEM((n_exp, LANES), F32),
            pltpu.SemaphoreType.DMA((2,)), pltpu.SemaphoreType.DMA((2,)),
        ],
        compiler_params=_params(("arbitrary",)),
        input_output_aliases={10: 5},
        name="outproj",
    )(xp, xs, cp, cs, op, os_, w_out, g2, wr, br, zeroed)


def _experts_kernel(cnt_ref, pages_ref, xs_ref, wg_ref, wu_ref, wd_ref, ysp_ref,
                    xbuf, ybuf, wg_f32, wu_f32, wd_f32, wg_bf, wu_bf, wd_bf, first_blk, page_seq,
                    gsems, ysems, wsems, *, n_exp, table_lanes):
    e = pl.program_id(0)
    tb = xbuf.shape[1]
    n_pages = page_seq.shape[0]

    def n_pages_of(ex):
        return (cnt_ref[ex] + (tb - 1)) // tb

    def page_rows(blk):
        return pl.ds(pl.multiple_of(page_seq[blk] * tb, tb), tb)

    def fetch(blk, slot):
        return pltpu.make_async_copy(xs_ref.at[page_rows(blk)], xbuf.at[slot], gsems.at[slot])

    def writeback(blk, slot):
        return pltpu.make_async_copy(ybuf.at[slot], ysp_ref.at[page_rows(blk)], ysems.at[slot])

    def weight_copies(ex, slot):
        return (pltpu.make_async_copy(wg_ref.at[ex], wg_f32.at[slot], wsems.at[slot]),
                pltpu.make_async_copy(wu_ref.at[ex], wu_f32.at[slot], wsems.at[slot]),
                pltpu.make_async_copy(wd_ref.at[ex], wd_f32.at[slot], wsems.at[slot]))

    @pl.when(e == 0)
    def _():
        for ahead in range(min(WEIGHT_SLOTS - 1, n_exp)):
            for cp in weight_copies(ahead, ahead):
                cp.start(priority=1)

        def per_expert(ex, blk):
            first_blk[ex] = blk

            def per_page(j, carry):
                page_seq[blk + j] = pages_ref[ex * table_lanes + j]
                return carry

            lax.fori_loop(0, n_pages_of(ex), per_page, 0)
            return blk + n_pages_of(ex)

        first_blk[n_exp] = lax.fori_loop(0, n_exp, per_expert, 0)
        fetch(0, 0).start()

    @pl.when(e + WEIGHT_SLOTS - 1 < n_exp)
    def _():
        for cp in weight_copies(e + WEIGHT_SLOTS - 1, (e + WEIGHT_SLOTS - 1) % WEIGHT_SLOTS):
            cp.start(priority=1)

    b_lo = first_blk[e]
    b_hi = first_blk[e + 1]
    n_total = first_blk[n_exp]
    wslot = e % WEIGHT_SLOTS
    for cp in weight_copies(e, wslot):
        cp.wait()
    wg_bf[...] = wg_f32[wslot].astype(BF16)
    wu_bf[...] = wu_f32[wslot].astype(BF16)
    wd_bf[...] = wd_f32[wslot].astype(BF16)

    def block(b, carry):
        slot = b % 2

        @pl.when(b >= 2)
        def _():
            writeback(b, slot).wait()

        fetch(b, slot).wait()
        fetch(jnp.minimum(b + 1, n_total - 1), 1 - slot).start()

        def swiglu(rows):
            hi, lo = _unpack_pairs(xbuf[slot, rows, :])
            x = jnp.concatenate([hi.astype(BF16), lo.astype(BF16)], axis=1)
            hg = jnp.dot(x, wg_bf[...], preferred_element_type=F32)
            hu = jnp.dot(x, wu_bf[...], preferred_element_type=F32)
            hb = (_silu(hg) * hu).astype(BF16)
            ybuf[slot, rows, :] = _pack_pairs(jnp.dot(hb, wd_bf[...], preferred_element_type=F32))

        valid = cnt_ref[e] - (b - b_lo) * tb
        half = tb // 2

        @pl.when(valid > half)
        def _():
            swiglu(slice(0, tb))

        @pl.when(valid <= half)
        def _():
            swiglu(slice(0, half))
            ybuf[slot, half:tb, :] = jnp.zeros((tb - half,) + ybuf.shape[2:], ybuf.dtype)

        writeback(b, slot).start()
        return carry

    lax.fori_loop(b_lo, b_hi, block, 0)

    @pl.when(e == n_exp - 1)
    def _():
        fetch(0, n_total % 2).wait()

        @pl.when(n_total >= 2)
        def _():
            writeback(0, n_total % 2).wait()

        writeback(0, (n_total + 1) % 2).wait()
        ybuf[0] = jnp.zeros(ybuf.shape[1:], ybuf.dtype)

        def spare(blk):
            return pltpu.make_async_copy(ybuf.at[0], ysp_ref.at[pl.ds(pl.multiple_of(blk * tb, tb), tb)], ysems.at[0])

        def zero(blk, carry):
            spare(blk).start()
            return carry

        lax.fori_loop(n_total, n_pages, zero, 0)

        def zero_wait(blk, carry):
            spare(0).wait()
            return carry

        lax.fori_loop(n_total, n_pages, zero_wait, 0)


def _experts(counts, pages_flat, xs_sorted, w_gate, w_up, w_down, *, n_pages, table_lanes):
    tile = xs_sorted.shape[1:]
    n_exp, d, ff = w_gate.shape
    anyspec = pl.BlockSpec(memory_space=pl.ANY)
    grid_spec = pltpu.PrefetchScalarGridSpec(
        num_scalar_prefetch=2,
        grid=(n_exp,),
        in_specs=[anyspec, anyspec, anyspec, anyspec],
        out_specs=anyspec,
        scratch_shapes=[
            pltpu.VMEM((2, EXPERT_ROWS) + tile, U32), pltpu.VMEM((2, EXPERT_ROWS) + tile, U32),
            pltpu.VMEM((WEIGHT_SLOTS, d, ff), F32), pltpu.VMEM((WEIGHT_SLOTS, d, ff), F32),
            pltpu.VMEM((WEIGHT_SLOTS, ff, d), F32),
            pltpu.VMEM((d, ff), BF16), pltpu.VMEM((d, ff), BF16), pltpu.VMEM((ff, d), BF16),
            pltpu.SMEM((n_exp + 1,), I32), pltpu.SMEM((n_pages,), I32),
            pltpu.SemaphoreType.DMA((2,)), pltpu.SemaphoreType.DMA((2,)), pltpu.SemaphoreType.DMA((WEIGHT_SLOTS,)),
        ],
    )
    return pl.pallas_call(
        functools.partial(_experts_kernel, n_exp=n_exp, table_lanes=table_lanes),
        grid_spec=grid_spec,
        out_shape=jax.ShapeDtypeStruct((n_pages * EXPERT_ROWS,) + tile, U32),
        compiler_params=_params(("arbitrary",)),
        name="experts",
    )(counts, pages_flat, xs_sorted, w_gate, w_up, w_down)


def _combine_kernel(dest_ref, dest_next_ref, gates_ref, x1_ref, ysp_ref, gf_ref, yp_ref, ysmp_ref,
                    buf0, buf1, sems, *, n_first):
    i = pl.program_id(0)
    n = pl.num_programs(0)
    tm = x1_ref.shape[0]
    slot = i % 2

    def gather(d_ref, s):
        def body(j, carry):
            for r in range(DMA_UNROLL):
                t = j * DMA_UNROLL + r
                _row_copy(ysp_ref, d_ref[0, t], buf0.at[s], t, sems.at[s]).start(priority=0)
                _row_copy(ysp_ref, d_ref[1, t], buf1.at[s], t, sems.at[s]).start(priority=1)
            return carry

        lax.fori_loop(0, tm // DMA_UNROLL, body, 0)

    def gather_wait(s):
        pltpu.make_async_copy(ysp_ref.at[pl.ds(0, tm)], buf0.at[s], sems.at[s]).wait()
        pltpu.make_async_copy(ysp_ref.at[pl.ds(0, tm)], buf1.at[s], sems.at[s]).wait()

    @pl.when(i == 0)
    def _():
        gather(dest_ref, 0)

    for t in range(tm):
        _row_copy(ysp_ref, dest_next_ref[0, t], buf0.at[1 - slot], t, sems.at[1 - slot]).start(priority=0)
        _row_copy(ysp_ref, dest_next_ref[1, t], buf1.at[1 - slot], t, sems.at[1 - slot]).start(priority=1)

    gather_wait(slot)
    hi0, lo0 = _unpack_pairs(buf0[slot])
    hi1, lo1 = _unpack_pairs(buf1[slot])
    g0 = gates_ref[:, 0:1]
    g1 = gates_ref[:, 1:2]
    moe = jnp.concatenate([g0 * hi0 + g1 * hi1, g0 * lo0 + g1 * lo1], axis=1)
    y = _rms(x1_ref[...] + moe, gf_ref[...])

    @pl.when(i < n_first)
    def _():
        yp_ref[...] = y

    @pl.when(i >= n_first)
    def _():
        ysmp_ref[...] = y

    @pl.when(i == n - 1)
    def _():
        gather_wait(1 - slot)


def _combine(dest, gates_t, x1, ysp, gf, *, n_p):
    n_all, d = x1.shape
    tile = ysp.shape[1:]
    n_first = n_p // ROW_TILE
    n_tiles = n_all // ROW_TILE
    first, second = _split_maps(n_first)
    return pl.pallas_call(
        functools.partial(_combine_kernel, n_first=n_first),
        grid=(n_tiles,),
        in_specs=[
            pl.BlockSpec((SUBLANES, ROW_TILE), lambda i: (0, i), memory_space=pltpu.SMEM),
            pl.BlockSpec((SUBLANES, ROW_TILE), lambda i: (0, jnp.minimum(i + 1, n_tiles - 1)),
                         memory_space=pltpu.SMEM),
            pl.BlockSpec((ROW_TILE, SUBLANES), lambda i: (i, 0)),
            pl.BlockSpec((ROW_TILE, d), lambda i: (i, 0)),
            pl.BlockSpec(memory_space=pl.ANY),
            pl.BlockSpec((1, d), lambda i: (0, 0)),
        ],
        out_specs=[pl.BlockSpec((ROW_TILE, d), first), pl.BlockSpec((ROW_TILE, d), second)],
        out_shape=[jax.ShapeDtypeStruct((n_p, d), F32), jax.ShapeDtypeStruct((n_all - n_p, d), F32)],
        scratch_shapes=[pltpu.VMEM((2, ROW_TILE) + tile, U32), pltpu.VMEM((2, ROW_TILE) + tile, U32),
                        pltpu.SemaphoreType.DMA((2,))],
        compiler_params=_params(("arbitrary",)),
        name="combine",
    )(dest, dest, gates_t, x1, ysp, gf)


def kernel(x_prompt, x_sample, cache_conv, state_gla, norm1_g, w_in, w_lr2, b_lr2, w_dw, b_dw, conv_ln_g, conv_ln_b, gla_norm_g, w_out, norm2_g, w_router_coarse, b_router_coarse, w_router_fine, b_router_fine, w_exp_gate, w_exp_up, w_exp_down, norm_f_g):
    assert norm1_g.shape[0] == 1, "single trunk layer"
    bp, tp, d = x_prompt.shape
    bs, ts, _ = x_sample.shape
    heads, dk, dv = state_gla.shape[2:]
    c_conv = w_dw.shape[2]
    width = w_dw.shape[1]
    qk, vv = heads * dk, heads * dv
    n_groups, _, per_group = w_router_fine.shape[1:]
    n_exp = n_groups * per_group
    n_p, n_s = bp * tp, bs * ts
    n_s_pad = -(-n_s // ROW_TILE) * ROW_TILE
    pad_rows = lambda a: jnp.pad(a, ((0, n_s_pad - n_s), (0, 0)))
    n_all = n_p + n_s_pad
    assert n_p % ROW_TILE == 0 and width - 1 <= HIST_PAD

    xp = x_prompt.reshape(n_p, d)
    xs = pad_rows(x_sample.reshape(n_s, d))
    row = lambda a: a.reshape(1, -1)

    mixer_args = (row(norm1_g[0]), w_in[0].astype(BF16), w_lr2[0], row(b_lr2[0]),
                  w_dw[0], row(b_dw[0]), row(conv_ln_g[0]), row(conv_ln_b[0]), row(gla_norm_g[0]))
    hist_p = jnp.zeros((bp, HIST_PAD, c_conv), F32)
    hist_s = jnp.pad(cache_conv[0], ((0, 0), (HIST_PAD - (width - 1), 0), (0, 0)))
    s0_p = jnp.zeros((bp, heads, dk, dv), F32)
    n_pages = (2 * n_all) // EXPERT_ROWS + n_exp
    sorted_rows = n_pages * EXPERT_ROWS + -(-2 * ROW_TILE // EXPERT_ROWS) * EXPERT_ROWS
    c_p, o_p, tail_p, gla_p, zeroed = _front(xp, hist_p, s0_p, *mixer_args, n_seq=bp, seq_rows=ROW_TILE // bp,
                                             zero_shape=(sorted_rows, d // 2))
    c_s, o_s, tail_s, gla_s = _front(x_sample.reshape(n_s, d), hist_s, state_gla[0], *mixer_args,
                                     n_seq=bs, seq_rows=ts)
    c_p, o_p = c_p.reshape(n_p, c_conv), o_p.reshape(n_p, vv)
    c_s, o_s = pad_rows(c_s.reshape(n_s, c_conv)), pad_rows(o_s.reshape(n_s, vv))

    wr = jnp.concatenate([w_router_coarse[0].T,
                          jnp.transpose(w_router_fine[0], (0, 2, 1)).reshape(n_exp, d)], axis=0)
    br = jnp.concatenate([b_router_coarse[0], b_router_fine[0].reshape(n_exp)])
    r_rows = -(-(n_groups + n_exp) // SUBLANES) * SUBLANES
    wr = jnp.pad(wr, ((0, r_rows - wr.shape[0]), (0, 0)))
    br = jnp.pad(br, (0, r_rows - br.shape[0])).reshape(r_rows, 1)
    wr_hi = wr.astype(BF16)
    wr = jnp.concatenate([wr_hi, (wr - wr_hi.astype(F32)).astype(BF16)], axis=0)
    assert n_all // EXPERT_ROWS + 2 <= LANES, "page table row must hold one expert's pages"
    x1, dest, gates, pages, counts, xs_sorted = _outproj(
        xp, xs, c_p, c_s, o_p, o_s, w_out[0], row(norm2_g[0]), wr, br, zeroed,
        n_groups=n_groups, per_group=per_group, n_pages=n_pages)
    ysp = _experts(counts[:, 0], pages.reshape(-1), xs_sorted, w_exp_gate[0], w_exp_up[0], w_exp_down[0],
                   n_pages=n_pages, table_lanes=LANES)
    y_p, y_s = _combine(dest, gates.T, x1, ysp, row(norm_f_g), n_p=n_p)

    lead = HIST_PAD - (width - 1)
    return (y_p.reshape(bp, tp, d), y_s[:n_s].reshape(bs, ts, d), tail_p[:, lead:][None], gla_p[None],
            tail_s[:, lead:][None], gla_s[None])
```
